```python
import jax, jax.numpy as jnp
from jax import lax
import numpy as np

D_MODEL = 2048
BATCH = 8
SEQ = 2048
DEPTH = 1

D_MIX = D_MODEL
GM_WIDTH = D_MIX // 2
GM_GROUPS = 8
GM_DG = GM_WIDTH // GM_GROUPS
CHUNK = 128
SB_WIDTH = D_MIX - GM_WIDTH
SB_HEADS = 8
SB_HEAD_DIM = SB_WIDTH // SB_HEADS
Q_BLOCK = 128
D_FF = -(-(8 * D_MODEL) // (3 * 256)) * 256
N_IN = 2 * GM_WIDTH + 3 * SB_WIDTH
N_MOD = 6
EPS = 1e-6

kernel_name = "hybrid_gmlp_stickbreaking_adaln_block"


def rmsnorm(x, g):
    xf = x.astype(jnp.float32)
    y = xf * lax.rsqrt(jnp.mean(xf * xf, axis=-1, keepdims=True) + EPS)
    return (y * g.astype(jnp.float32)).astype(x.dtype)


def group_rmsnorm(x, g, groups):
    xf = x.astype(jnp.float32).reshape(*x.shape[:-1], groups, -1)
    y = xf * lax.rsqrt(jnp.mean(xf * xf, axis=-1, keepdims=True) + EPS)
    return (y.reshape(x.shape) * g.astype(jnp.float32)).astype(x.dtype)


def group_layernorm(x, g, groups):
    xf = x.astype(jnp.float32).reshape(*x.shape[:-1], groups, -1)
    mu = jnp.mean(xf, axis=-1, keepdims=True)
    xc = xf - mu
    y = xc * lax.rsqrt(jnp.mean(xc * xc, axis=-1, keepdims=True) + EPS)
    return (y.reshape(x.shape) * g.astype(jnp.float32)).astype(x.dtype)


def chunked_spatial_gating(z, v_norm_g, w_s, b_s):
    B, S, _ = z.shape
    u, v = z[..., :GM_WIDTH], z[..., GM_WIDTH:]
    v = group_layernorm(v, v_norm_g, GM_GROUPS)
    v = v.reshape(B, S // CHUNK, CHUNK, GM_GROUPS, GM_DG)
    causal = jnp.tril(jnp.ones((CHUNK, CHUNK), dtype=bool))
    w = jnp.where(causal[None], w_s, 0).astype(v.dtype)
    mixed = jnp.einsum('gts,bnsgd->bntgd', w, v) + b_s.T.astype(v.dtype)[None, None, :, :, None]
    return u * mixed.reshape(B, S, GM_WIDTH)


def stick_breaking_attention(q, k, v):
    B, S, H, Dh = q.shape
    scale = Dh ** -0.5
    outs = []
    for i in range(S // Q_BLOCK):
        start, end = i * Q_BLOCK, (i + 1) * Q_BLOCK
        qs, ks, vs = q[:, start:end], k[:, :end], v[:, :end]
        z = jnp.einsum('bqhd,bkhd->bhqk', qs, ks).astype(jnp.float32) * scale
        t_pos = start + jnp.arange(Q_BLOCK)[:, None]
        s_pos = jnp.arange(end)[None, :]
        mask = s_pos < t_pos
        log_beta = jax.nn.log_sigmoid(z)
        log_1m = jnp.where(mask, jax.nn.log_sigmoid(-z), 0.0)
        tail = lax.cumsum(log_1m, axis=3, reverse=True) - log_1m
        a = jnp.where(mask, jnp.exp(log_beta + tail), 0.0)
        outs.append(jnp.einsum('bhqk,bkhd->bqhd', a.astype(vs.dtype), vs))
    return jnp.concatenate(outs, axis=1)


def _fwd_setup_inputs(seed: int = 0) -> dict:
    key = jax.random.key(seed)
    ks = jax.random.split(key, 16)
    f32 = jnp.float32
    n = lambda k, shape: jax.random.normal(k, shape, dtype=f32)
    return {
        "x": n(ks[0], (BATCH, SEQ, D_MODEL)),
        "c": n(ks[1], (BATCH, D_MODEL)),
        "w_ada": n(ks[2], (DEPTH, D_MODEL, N_MOD * D_MODEL)) * (0.5 * D_MODEL ** -0.5),
        "b_ada": n(ks[3], (DEPTH, N_MOD * D_MODEL)) * 0.01,
        "norm1_g": 1.0 + 0.01 * n(ks[4], (DEPTH, D_MODEL)),
        "w_in": n(ks[5], (DEPTH, D_MODEL, N_IN)) * D_MODEL ** -0.5,
        "v_norm_g": 1.0 + 0.01 * n(ks[6], (DEPTH, GM_WIDTH)),
        "w_spatial": n(ks[7], (DEPTH, GM_GROUPS, CHUNK, CHUNK)) * CHUNK ** -0.5,
        "b_spatial": 1.0 + 0.01 * n(ks[8], (DEPTH, GM_GROUPS, CHUNK)),
        "out_norm_g": 1.0 + 0.01 * n(ks[9], (DEPTH, D_MIX)),
        "w_out": n(ks[10], (DEPTH, D_MIX, D_MODEL)) * D_MIX ** -0.5,
        "norm2_g": 1.0 + 0.01 * n(ks[11], (DEPTH, D_MODEL)),
        "w_gate": n(ks[12], (DEPTH, D_MODEL, D_FF)) * D_MODEL ** -0.5,
        "w_up": n(ks[13], (DEPTH, D_MODEL, D_FF)) * D_MODEL ** -0.5,
        "w_down": n(ks[14], (DEPTH, D_FF, D_MODEL)) * D_FF ** -0.5,
        "final_g": 1.0 + 0.01 * n(ks[15], (D_MODEL,)),
    }


def _fwd_reference(x, c, w_ada, b_ada, norm1_g, w_in, v_norm_g, w_spatial, b_spatial,
              out_norm_g, w_out, norm2_g, w_gate, w_up, w_down, final_g):
    B, S, _ = x.shape
    c_act = jax.nn.silu(c)
    for l in range(DEPTH):
        mod = c_act @ w_ada[l] + b_ada[l]
        shift1, scale1, gate1, shift2, scale2, gate2 = [m[:, None, :] for m in jnp.split(mod, N_MOD, axis=-1)]

        h = rmsnorm(x, norm1_g[l]) * (1.0 + scale1) + shift1
        proj = h @ w_in[l]
        z_gm = jax.nn.gelu(proj[..., :2 * GM_WIDTH], approximate=False)
        o_gm = chunked_spatial_gating(z_gm, v_norm_g[l], w_spatial[l], b_spatial[l])
        qkv = proj[..., 2 * GM_WIDTH:].reshape(B, S, 3, SB_HEADS, SB_HEAD_DIM)
        o_sb = stick_breaking_attention(qkv[:, :, 0], qkv[:, :, 1], qkv[:, :, 2]).reshape(B, S, SB_WIDTH)
        o = jnp.concatenate([o_gm, o_sb], axis=-1)
        o = group_rmsnorm(o, out_norm_g[l], GM_GROUPS + SB_HEADS)
        x = x + gate1 * (o @ w_out[l])

        h = rmsnorm(x, norm2_g[l]) * (1.0 + scale2) + shift2
        f = (jax.nn.silu(h @ w_gate[l]) * (h @ w_up[l])) @ w_down[l]
        x = x + gate2 * f
    return rmsnorm(x, final_g)


import jax as _jax
import jax.numpy as _jnp

TWIN_FORMAT = 'train_step'
FWD_PARAMS = ['x', 'c', 'w_ada', 'b_ada', 'norm1_g', 'w_in', 'v_norm_g', 'w_spatial', 'b_spatial', 'out_norm_g', 'w_out', 'norm2_g', 'w_gate', 'w_up', 'w_down', 'final_g']
TWIN_WEIGHTS = ['w_ada', 'b_ada', 'norm1_g', 'w_in', 'v_norm_g', 'w_spatial', 'b_spatial', 'out_norm_g', 'w_out', 'norm2_g', 'w_gate', 'w_up', 'w_down', 'final_g']
TWIN_DIFF_INPUT = 'x'
TWIN_INPUTS = ['x', 'c', 'w_ada', 'b_ada', 'norm1_g', 'w_in', 'v_norm_g', 'w_spatial', 'b_spatial', 'out_norm_g', 'w_out', 'norm2_g', 'w_gate', 'w_up', 'w_down', 'final_g', 'loss_target', 'm_w_ada', 'm_b_ada', 'm_norm1_g', 'm_w_in', 'm_v_norm_g', 'm_w_spatial', 'm_b_spatial', 'm_out_norm_g', 'm_w_out', 'm_norm2_g', 'm_w_gate', 'm_w_up', 'm_w_down', 'm_final_g', 'v_w_ada', 'v_b_ada', 'v_norm1_g', 'v_w_in', 'v_v_norm_g', 'v_w_spatial', 'v_b_spatial', 'v_out_norm_g', 'v_w_out', 'v_norm2_g', 'v_w_gate', 'v_w_up', 'v_w_down', 'v_final_g']
TWIN_OUTPUTS = ['loss', 'grad_x', 'grad_w_ada', 'grad_b_ada', 'grad_norm1_g', 'grad_w_in', 'grad_v_norm_g', 'grad_w_spatial', 'grad_b_spatial', 'grad_out_norm_g', 'grad_w_out', 'grad_norm2_g', 'grad_w_gate', 'grad_w_up', 'grad_w_down', 'grad_final_g', 'delta_w_ada', 'delta_b_ada', 'delta_norm1_g', 'delta_w_in', 'delta_v_norm_g', 'delta_w_spatial', 'delta_b_spatial', 'delta_out_norm_g', 'delta_w_out', 'delta_norm2_g', 'delta_w_gate', 'delta_w_up', 'delta_w_down', 'delta_final_g', 'new_m_w_ada', 'new_m_b_ada', 'new_m_norm1_g', 'new_m_w_in', 'new_m_v_norm_g', 'new_m_w_spatial', 'new_m_b_spatial', 'new_m_out_norm_g', 'new_m_w_out', 'new_m_norm2_g', 'new_m_w_gate', 'new_m_w_up', 'new_m_w_down', 'new_m_final_g', 'new_v_w_ada', 'new_v_b_ada', 'new_v_norm1_g', 'new_v_w_in', 'new_v_v_norm_g', 'new_v_w_spatial', 'new_v_b_spatial', 'new_v_out_norm_g', 'new_v_w_out', 'new_v_norm2_g', 'new_v_w_gate', 'new_v_w_up', 'new_v_w_down', 'new_v_final_g']
TWIN_LEAF_KINDS = {'loss': 'loss', 'grad_x': 'grad_x', 'grad_w_ada': 'grad_w', 'grad_b_ada': 'grad_w', 'grad_norm1_g': 'grad_w', 'grad_w_in': 'grad_w', 'grad_v_norm_g': 'grad_w', 'grad_w_spatial': 'grad_w', 'grad_b_spatial': 'grad_w', 'grad_out_norm_g': 'grad_w', 'grad_w_out': 'grad_w', 'grad_norm2_g': 'grad_w', 'grad_w_gate': 'grad_w', 'grad_w_up': 'grad_w', 'grad_w_down': 'grad_w', 'grad_final_g': 'grad_w', 'delta_w_ada': 'delta_w', 'delta_b_ada': 'delta_w', 'delta_norm1_g': 'delta_w', 'delta_w_in': 'delta_w', 'delta_v_norm_g': 'delta_w', 'delta_w_spatial': 'delta_w', 'delta_b_spatial': 'delta_w', 'delta_out_norm_g': 'delta_w', 'delta_w_out': 'delta_w', 'delta_norm2_g': 'delta_w', 'delta_w_gate': 'delta_w', 'delta_w_up': 'delta_w', 'delta_w_down': 'delta_w', 'delta_final_g': 'delta_w', 'new_m_w_ada': 'new_m', 'new_m_b_ada': 'new_m', 'new_m_norm1_g': 'new_m', 'new_m_w_in': 'new_m', 'new_m_v_norm_g': 'new_m', 'new_m_w_spatial': 'new_m', 'new_m_b_spatial': 'new_m', 'new_m_out_norm_g': 'new_m', 'new_m_w_out': 'new_m', 'new_m_norm2_g': 'new_m', 'new_m_w_gate': 'new_m', 'new_m_w_up': 'new_m', 'new_m_w_down': 'new_m', 'new_m_final_g': 'new_m', 'new_v_w_ada': 'new_v', 'new_v_b_ada': 'new_v', 'new_v_norm1_g': 'new_v', 'new_v_w_in': 'new_v', 'new_v_v_norm_g': 'new_v', 'new_v_w_spatial': 'new_v', 'new_v_b_spatial': 'new_v', 'new_v_out_norm_g': 'new_v', 'new_v_w_out': 'new_v', 'new_v_norm2_g': 'new_v', 'new_v_w_gate': 'new_v', 'new_v_w_up': 'new_v', 'new_v_w_down': 'new_v', 'new_v_final_g': 'new_v'}


def _forward(args):
    return _fwd_reference(*[args[k] for k in FWD_PARAMS])


def _output_shape():
    out = _jax.eval_shape(lambda: _forward(_fwd_setup_inputs(0)))
    return out.shape, out.dtype

N_MICROBATCH = 1
ADAM_LR = 0.001
ADAM_B1 = 0.9
ADAM_B2 = 0.999
ADAM_EPS = 1e-08
ADAM_WD = 0.01
ADAM_STEP = 10
PER_EXAMPLE_BATCH_AXIS = {'x': 0, 'c': 0, 'loss_target': 0}
SHARED_INPUTS = []
_WEIGHT_DTYPES = {'w_ada': _jnp.float32, 'b_ada': _jnp.float32, 'norm1_g': _jnp.float32, 'w_in': _jnp.float32, 'v_norm_g': _jnp.float32, 'w_spatial': _jnp.float32, 'b_spatial': _jnp.float32, 'out_norm_g': _jnp.float32, 'w_out': _jnp.float32, 'norm2_g': _jnp.float32, 'w_gate': _jnp.float32, 'w_up': _jnp.float32, 'w_down': _jnp.float32, 'final_g': _jnp.float32}
MOMENT_SCALE = {'w_ada': 2.095451e-02, 'b_ada': 3.555652e-02, 'norm1_g': 2.170937e-02, 'w_in': 1.432364e-02, 'v_norm_g': 1.031704e-02, 'w_spatial': 1.023704e-02, 'b_spatial': 8.556235e-03, 'out_norm_g': 1.832744e-02, 'w_out': 1.844523e-02, 'norm2_g': 1.764964e-02, 'w_gate': 7.975720e-03, 'w_up': 7.716784e-03, 'w_down': 1.277650e-02, 'final_g': 8.011106e+00}


def _to_microbatches(a, axis):
    t = _jnp.moveaxis(a, axis, 0)
    t = t.reshape((N_MICROBATCH, t.shape[0] // N_MICROBATCH) + t.shape[1:])
    return _jnp.moveaxis(t, 1, axis + 1)


def setup_inputs(seed: int = 0) -> dict:
    inp = _fwd_setup_inputs(seed)
    key = _jax.random.fold_in(_jax.random.key(seed), 7919)
    shape, _ = _output_shape()
    out = dict(inp)
    out["loss_target"] = _jax.random.normal(_jax.random.fold_in(key, 0), shape, _jnp.float32)
    for i, name in enumerate(TWIN_WEIGHTS):
        w = inp[name].astype(_jnp.float32)
        if MOMENT_SCALE is None:
            s = _jnp.sqrt(_jnp.mean(_jnp.square(w)) + 1e-30)
        else:
            s = MOMENT_SCALE[name]
        km, kv = _jax.random.split(_jax.random.fold_in(key, i + 1))
        out[name] = w
        out["m_" + name] = s * _jax.random.normal(km, w.shape, _jnp.float32)
        out["v_" + name] = (s * s) * _jax.random.uniform(kv, w.shape, _jnp.float32, 0.5, 1.5)
    if N_MICROBATCH > 1:
        for name, axis in PER_EXAMPLE_BATCH_AXIS.items():
            out[name] = _to_microbatches(out[name], axis)
    return {'x': out['x'], 'c': out['c'], 'w_ada': out['w_ada'], 'b_ada': out['b_ada'], 'norm1_g': out['norm1_g'], 'w_in': out['w_in'], 'v_norm_g': out['v_norm_g'], 'w_spatial': out['w_spatial'], 'b_spatial': out['b_spatial'], 'out_norm_g': out['out_norm_g'], 'w_out': out['w_out'], 'norm2_g': out['norm2_g'], 'w_gate': out['w_gate'], 'w_up': out['w_up'], 'w_down': out['w_down'], 'final_g': out['final_g'], 'loss_target': out['loss_target'], 'm_w_ada': out['m_w_ada'], 'm_b_ada': out['m_b_ada'], 'm_norm1_g': out['m_norm1_g'], 'm_w_in': out['m_w_in'], 'm_v_norm_g': out['m_v_norm_g'], 'm_w_spatial': out['m_w_spatial'], 'm_b_spatial': out['m_b_spatial'], 'm_out_norm_g': out['m_out_norm_g'], 'm_w_out': out['m_w_out'], 'm_norm2_g': out['m_norm2_g'], 'm_w_gate': out['m_w_gate'], 'm_w_up': out['m_w_up'], 'm_w_down': out['m_w_down'], 'm_final_g': out['m_final_g'], 'v_w_ada': out['v_w_ada'], 'v_b_ada': out['v_b_ada'], 'v_norm1_g': out['v_norm1_g'], 'v_w_in': out['v_w_in'], 'v_v_norm_g': out['v_v_norm_g'], 'v_w_spatial': out['v_w_spatial'], 'v_b_spatial': out['v_b_spatial'], 'v_out_norm_g': out['v_out_norm_g'], 'v_w_out': out['v_w_out'], 'v_norm2_g': out['v_norm2_g'], 'v_w_gate': out['v_w_gate'], 'v_w_up': out['v_w_up'], 'v_w_down': out['v_w_down'], 'v_final_g': out['v_final_g']}


def _loss(weights, diff, rest, loss_target):
    with _jax.named_scope("forward"):
        args = {**rest, TWIN_DIFF_INPUT: diff, **{k: w.astype(_WEIGHT_DTYPES[k]) for k, w in weights.items()}}
        y = _forward(args)
    with _jax.named_scope("loss_head"):
        err = _jnp.square(y.astype(_jnp.float32) - loss_target)
        return 0.5 * _jnp.sum(_jnp.mean(err, axis=-1)) if err.ndim else 0.5 * err


def _adamw(w, g, m, v):
    m = ADAM_B1 * m + (1.0 - ADAM_B1) * g
    v = ADAM_B2 * v + (1.0 - ADAM_B2) * _jnp.square(g)
    m_hat = m / (1.0 - ADAM_B1 ** ADAM_STEP)
    v_hat = v / (1.0 - ADAM_B2 ** ADAM_STEP)
    delta = -ADAM_LR * (m_hat / (_jnp.sqrt(v_hat) + ADAM_EPS) + ADAM_WD * w)
    return delta, m, v


def reference(x, c, w_ada, b_ada, norm1_g, w_in, v_norm_g, w_spatial, b_spatial, out_norm_g, w_out, norm2_g, w_gate, w_up, w_down, final_g, loss_target, m_w_ada, m_b_ada, m_norm1_g, m_w_in, m_v_norm_g, m_w_spatial, m_b_spatial, m_out_norm_g, m_w_out, m_norm2_g, m_w_gate, m_w_up, m_w_down, m_final_g, v_w_ada, v_b_ada, v_norm1_g, v_w_in, v_v_norm_g, v_w_spatial, v_b_spatial, v_out_norm_g, v_w_out, v_norm2_g, v_w_gate, v_w_up, v_w_down, v_final_g):
    given = dict(x=x, c=c, w_ada=w_ada, b_ada=b_ada, norm1_g=norm1_g, w_in=w_in, v_norm_g=v_norm_g, w_spatial=w_spatial, b_spatial=b_spatial, out_norm_g=out_norm_g, w_out=w_out, norm2_g=norm2_g, w_gate=w_gate, w_up=w_up, w_down=w_down, final_g=final_g, loss_target=loss_target, m_w_ada=m_w_ada, m_b_ada=m_b_ada, m_norm1_g=m_norm1_g, m_w_in=m_w_in, m_v_norm_g=m_v_norm_g, m_w_spatial=m_w_spatial, m_b_spatial=m_b_spatial, m_out_norm_g=m_out_norm_g, m_w_out=m_w_out, m_norm2_g=m_norm2_g, m_w_gate=m_w_gate, m_w_up=m_w_up, m_w_down=m_w_down, m_final_g=m_final_g, v_w_ada=v_w_ada, v_b_ada=v_b_ada, v_norm1_g=v_norm1_g, v_w_in=v_w_in, v_v_norm_g=v_v_norm_g, v_w_spatial=v_w_spatial, v_b_spatial=v_b_spatial, v_out_norm_g=v_out_norm_g, v_w_out=v_w_out, v_norm2_g=v_norm2_g, v_w_gate=v_w_gate, v_w_up=v_w_up, v_w_down=v_w_down, v_final_g=v_final_g)
    weights = {n: given[n] for n in TWIN_WEIGHTS}
    shared = {n: given[n] for n in SHARED_INPUTS}
    per_example = {n: given[n] for n in ['x', 'c']}
    grad_fn = _jax.value_and_grad(_loss, argnums=(0, 1))

    def one_microbatch(ex, loss_target):
        ex = dict(ex)
        diff = ex.pop(TWIN_DIFF_INPUT)
        return grad_fn(weights, diff, {**shared, **ex}, loss_target)

    if N_MICROBATCH == 1:
        loss, (grad_w, grad_x) = one_microbatch(per_example, given["loss_target"])
    else:
        def body(carry, xs):
            loss_sum, grad_sum = carry
            l_k, (gw_k, gx_k) = one_microbatch(xs[0], xs[1])
            with _jax.named_scope("update"):
                return (loss_sum + l_k, _jax.tree.map(_jnp.add, grad_sum, gw_k)), gx_k

        init = (_jnp.zeros((), _jnp.float32), _jax.tree.map(_jnp.zeros_like, weights))
        (loss, grad_w), grad_x = _jax.lax.scan(body, init, (per_example, given["loss_target"]))
    with _jax.named_scope("update"):
        delta_w, new_m, new_v = {}, {}, {}
        for n in TWIN_WEIGHTS:
            delta_w[n], new_m[n], new_v[n] = _adamw(weights[n], grad_w[n], given["m_" + n], given["v_" + n])
    return (loss, grad_x, *[grad_w[n] for n in TWIN_WEIGHTS], *[delta_w[n] for n in TWIN_WEIGHTS],
            *[new_m[n] for n in TWIN_WEIGHTS], *[new_v[n] for n in TWIN_WEIGHTS])
```

```python
import functools
import math

import jax
import jax.numpy as jnp
from jax import lax
from jax.experimental import pallas as pl
from jax.experimental.pallas import tpu as pltpu

F32 = jnp.float32
MXU_DTYPE = jnp.bfloat16
WIRE_DTYPE = jnp.bfloat16
EPS = 1e-6
LANE = 128
V7X_VMEM_LIMIT = 56 * 1024 * 1024
MESH = pl.DeviceIdType.MESH
N_CHIPS = 4
FLIPS = (2, 1, 3)
ANY = pl.BlockSpec(memory_space=pl.ANY)

ADAM_LR = 0.001
ADAM_B1 = 0.9
ADAM_B2 = 0.999
ADAM_EPS = 1e-08
ADAM_WD = 0.01
ADAM_STEP = 10


def _params(*semantics):
    return pltpu.CompilerParams(dimension_semantics=semantics or None, vmem_limit_bytes=V7X_VMEM_LIMIT)


def _tile(dim, pref, unit=LANE):
    best = None
    t = unit
    while t <= min(dim, pref):
        if dim % t == 0:
            best = t
        t += unit
    return best if best is not None else dim


def _sum0(v):
    return jnp.sum(v, axis=0, keepdims=True)


def _mean1(v):
    return jnp.mean(v, axis=-1, keepdims=True)


def _gelu(x):
    return 0.5 * x * (1.0 + lax.erf(x * (1.0 / math.sqrt(2.0))))


def _gelu_grad(x):
    cdf = 0.5 * (1.0 + lax.erf(x * (1.0 / math.sqrt(2.0))))
    return cdf + x * jnp.exp(-0.5 * x * x) * (1.0 / math.sqrt(2.0 * math.pi))


def _dot(a, b, dims):
    return lax.dot_general(a, b, (dims, ((), ())), preferred_element_type=F32)


NN = ((1,), (0,))
NT = ((1,), (1,))
TN = ((0,), (0,))


def _mm(kind, a, b, out_dtypes, name, tm=1024, tn=512, extras=(), epilogue=None):
    if kind == "nn":
        (M, K), N = a.shape, b.shape[1]
    elif kind == "nt":
        (M, K), N = a.shape, b.shape[0]
    else:
        (K, M), N = a.shape, b.shape[1]
    tm, tn = _tile(M, tm), _tile(N, tn)
    a_spec = pl.BlockSpec((K, tm), lambda i, j: (0, i)) if kind == "tn" else pl.BlockSpec((tm, K), lambda i, j: (i, 0))
    b_spec = pl.BlockSpec((tn, K), lambda i, j: (j, 0)) if kind == "nt" else pl.BlockSpec((K, tn), lambda i, j: (0, j))
    mn_spec = pl.BlockSpec((tm, tn), lambda i, j: (i, j))
    dims = {"nn": NN, "nt": NT, "tn": TN}[kind]
    n_extra = len(extras)

    def body(a_ref, b_ref, *rest):
        acc = _dot(a_ref[...], b_ref[...], dims)
        res = (acc,) if epilogue is None else epilogue(acc, *[e[...] for e in rest[:n_extra]])
        for o_ref, r in zip(rest[n_extra:], res):
            o_ref[...] = r.astype(o_ref.dtype)

    outs = pl.pallas_call(
        body, name=name, grid=(M // tm, N // tn),
        in_specs=[a_spec, b_spec] + [mn_spec] * n_extra,
        out_specs=[mn_spec] * len(out_dtypes),
        out_shape=[jax.ShapeDtypeStruct((M, N), d) for d in out_dtypes],
        compiler_params=_params("parallel", "arbitrary"),
    )(a, b, *extras)
    return outs


def _gate_up(h, wg, wu, name):
    (M, K), N = h.shape, wg.shape[1]
    tm, tn = _tile(M, 1024), _tile(N, 512)

    def body(h_ref, wg_ref, wu_ref, ag_ref, au_ref, f_ref):
        hv = h_ref[...]
        ag = _dot(hv, wg_ref[...], NN)
        au = _dot(hv, wu_ref[...], NN)
        ag_ref[...] = ag
        au_ref[...] = au
        f_ref[...] = (ag * jax.nn.sigmoid(ag) * au).astype(f_ref.dtype)

    w_spec = pl.BlockSpec((K, tn), lambda i, j: (0, j))
    mn_spec = pl.BlockSpec((tm, tn), lambda i, j: (i, j))
    return pl.pallas_call(
        body, name=name, grid=(M // tm, N // tn),
        in_specs=[pl.BlockSpec((tm, K), lambda i, j: (i, 0)), w_spec, w_spec],
        out_specs=[mn_spec] * 3,
        out_shape=[jax.ShapeDtypeStruct((M, N), F32), jax.ShapeDtypeStruct((M, N), F32),
                   jax.ShapeDtypeStruct((M, N), MXU_DTYPE)],
        compiler_params=_params("parallel", "arbitrary"),
    )(h, wg, wu)


def _swiglu_bwd_epilogue(dfin, ag, au):
    sg = jax.nn.sigmoid(ag)
    d_au = dfin * (ag * sg)
    d_ag = dfin * au * (sg * (1.0 + ag * (1.0 - sg)))
    return d_ag, d_au


def _add_epilogue(acc, other):
    return (acc + other,)


def _row_specs(ts, width):
    return pl.BlockSpec((ts, width), lambda i: (i, 0)), pl.BlockSpec((1, width), lambda i: (0, 0))


def _cast(a, dtype, name):
    R, C = a.shape
    tr = _tile(R, 512, 16)
    spec = pl.BlockSpec((tr, C), lambda i: (i, 0))

    def body(a_ref, o_ref):
        o_ref[...] = a_ref[...].astype(o_ref.dtype)

    return pl.pallas_call(body, name=name, grid=(R // tr,), in_specs=[spec], out_specs=spec,
                          out_shape=jax.ShapeDtypeStruct((R, C), dtype), compiler_params=_params("parallel"))(a)


def _norm_mod(x, g, scale, shift, name):
    S, D = x.shape
    ts = _tile(S, 256, 16)
    tile, vec = _row_specs(ts, D)

    def body(x_ref, g_ref, sc_ref, sh_ref, h_ref):
        xv = x_ref[...]
        r = lax.rsqrt(_mean1(xv * xv) + EPS)
        h_ref[...] = ((xv * r) * g_ref[...] * (1.0 + sc_ref[...]) + sh_ref[...]).astype(h_ref.dtype)

    return pl.pallas_call(body, name=name, grid=(S // ts,), in_specs=[tile, vec, vec, vec], out_specs=tile,
                          out_shape=jax.ShapeDtypeStruct((S, D), MXU_DTYPE), compiler_params=_params("parallel"))(x, g, scale, shift)


def _residual_norm_mod(x, attn, gate, g, scale, shift, name):
    S, D = x.shape
    ts = _tile(S, 256, 16)
    tile, vec = _row_specs(ts, D)

    def body(x_ref, a_ref, gate_ref, g_ref, sc_ref, sh_ref, x1_ref, h_ref):
        x1 = x_ref[...] + gate_ref[...] * a_ref[...]
        x1_ref[...] = x1
        r = lax.rsqrt(_mean1(x1 * x1) + EPS)
        h_ref[...] = ((x1 * r) * g_ref[...] * (1.0 + sc_ref[...]) + sh_ref[...]).astype(h_ref.dtype)

    return pl.pallas_call(body, name=name, grid=(S // ts,), in_specs=[tile, tile, vec, vec, vec, vec],
                          out_specs=[tile, tile],
                          out_shape=[jax.ShapeDtypeStruct((S, D), F32), jax.ShapeDtypeStruct((S, D), MXU_DTYPE)],
                          compiler_params=_params("parallel"))(x, attn, gate, g, scale, shift)


def _final_loss_bwd(x1, f, gate2, final_g, target, name):
    S, D = x1.shape
    ts = _tile(S, 256, 16)
    tile, vec = _row_specs(ts, D)
    loss_spec = pl.BlockSpec((1, LANE), lambda i: (0, 0))

    def body(x1_ref, f_ref, gate_ref, g_ref, t_ref, dx2_ref, df_ref, dgate_ref, dg_ref, loss_ref):
        @pl.when(pl.program_id(0) == 0)
        def _():
            dgate_ref[...] = jnp.zeros_like(dgate_ref)
            dg_ref[...] = jnp.zeros_like(dg_ref)
            loss_ref[...] = jnp.zeros_like(loss_ref)

        fv, gate, g = f_ref[...], gate_ref[...], g_ref[...]
        x2 = x1_ref[...] + gate * fv
        r = lax.rsqrt(_mean1(x2 * x2) + EPS)
        xn = x2 * r
        err = xn * g - t_ref[...]
        loss_ref[...] += jnp.broadcast_to(0.5 * _sum0(_mean1(err * err)), loss_ref.shape)
        dy = err * (1.0 / D)
        dg_ref[...] += _sum0(dy * xn)
        dxn = dy * g
        dx2 = r * (dxn - xn * _mean1(dxn * xn))
        dx2_ref[...] = dx2
        dgate_ref[...] += _sum0(dx2 * fv)
        df_ref[...] = (dx2 * gate).astype(df_ref.dtype)

    return pl.pallas_call(
        body, name=name, grid=(S // ts,), in_specs=[tile, tile, vec, vec, tile],
        out_specs=[tile, tile, vec, vec, loss_spec],
        out_shape=[jax.ShapeDtypeStruct((S, D), F32), jax.ShapeDtypeStruct((S, D), MXU_DTYPE),
                   jax.ShapeDtypeStruct((1, D), F32), jax.ShapeDtypeStruct((1, D), F32),
                   jax.ShapeDtypeStruct((1, LANE), F32)],
        compiler_params=_params("arbitrary"),
    )(x1, f, gate2, final_g, target)


def _norm_mod_bwd(dh, xin, dres, g, scale, name, branch=None, gate=None):
    S, D = xin.shape
    ts = _tile(S, 256, 16)
    tile, vec = _row_specs(ts, D)
    with_gate = branch is not None

    def body(*refs):
        if with_gate:
            dh_ref, x_ref, dres_ref, g_ref, sc_ref, br_ref, gate_ref, dx_ref, dshift_ref, dscale_ref, dg_ref, dgate_ref, dbr_ref = refs
            accs = (dshift_ref, dscale_ref, dg_ref, dgate_ref)
        else:
            dh_ref, x_ref, dres_ref, g_ref, sc_ref, dx_ref, dshift_ref, dscale_ref, dg_ref = refs
            accs = (dshift_ref, dscale_ref, dg_ref)

        @pl.when(pl.program_id(0) == 0)
        def _():
            for acc in accs:
                acc[...] = jnp.zeros_like(acc)

        dh_v, xv, g_v = dh_ref[...], x_ref[...], g_ref[...]
        one_sc = 1.0 + sc_ref[...]
        r = lax.rsqrt(_mean1(xv * xv) + EPS)
        xn = xv * r
        dshift_ref[...] += _sum0(dh_v)
        dscale_ref[...] += _sum0(dh_v * (xn * g_v))
        dg_ref[...] += _sum0(dh_v * one_sc * xn)
        dxn = dh_v * (g_v * one_sc)
        dx = dres_ref[...] + r * (dxn - xn * _mean1(dxn * xn))
        dx_ref[...] = dx
        if with_gate:
            dgate_ref[...] += _sum0(dx * br_ref[...])
            dbr_ref[...] = (dx * gate_ref[...]).astype(dbr_ref.dtype)

    ins = [dh, xin, dres, g, scale] + ([branch, gate] if with_gate else [])
    in_specs = [tile, tile, tile, vec, vec] + ([tile, vec] if with_gate else [])
    out_specs = [tile, vec, vec, vec] + ([vec, tile] if with_gate else [])
    out_shape = [jax.ShapeDtypeStruct((S, D), F32)] + [jax.ShapeDtypeStruct((1, D), F32)] * 3
    if with_gate:
        out_shape += [jax.ShapeDtypeStruct((1, D), F32), jax.ShapeDtypeStruct((S, D), MXU_DTYPE)]
    return pl.pallas_call(body, name=name, grid=(S // ts,), in_specs=in_specs, out_specs=out_specs,
                          out_shape=out_shape, compiler_params=_params("arbitrary"))(*ins)


def _causal_weights(ws_ref, wt_ref, n_g):
    row = lax.broadcasted_iota(jnp.int32, (LANE, LANE), 0)
    col = lax.broadcasted_iota(jnp.int32, (LANE, LANE), 1)
    for g in range(n_g):
        wt_ref[g] = jnp.where(col <= row, ws_ref[g], 0.0).astype(wt_ref.dtype)


def _group_layernorm(v):
    xc = v - _mean1(v)
    rstd = lax.rsqrt(_mean1(xc * xc) + EPS)
    return xc * rstd, rstd


def _gmlp_fwd(proj, v_gain, w_s, b_t, out_gain, n_g, name):
    S = proj.shape[0]
    GW = n_g * LANE

    def body(p_ref, vg_ref, ws_ref, bt_ref, og_ref, on_ref, wt_ref):
        @pl.when(pl.program_id(0) == 0)
        def _():
            _causal_weights(ws_ref, wt_ref, n_g)

        for g in range(n_g):
            cols = slice(g * LANE, (g + 1) * LANE)
            u = _gelu(p_ref[:, cols])
            v = _gelu(p_ref[:, GW + g * LANE:GW + (g + 1) * LANE])
            vhat, _ = _group_layernorm(v)
            vln = (vhat * vg_ref[:, cols]).astype(MXU_DTYPE)
            mixed = _dot(wt_ref[g], vln, NN) + bt_ref[:, g:g + 1]
            o = u * mixed
            r = lax.rsqrt(_mean1(o * o) + EPS)
            on_ref[:, cols] = (o * r * og_ref[:, cols]).astype(on_ref.dtype)

    return pl.pallas_call(
        body, name=name, grid=(S // LANE,),
        in_specs=[pl.BlockSpec((LANE, 2 * GW), lambda n: (n, 0)),
                  pl.BlockSpec((1, GW), lambda n: (0, 0)),
                  pl.BlockSpec((n_g, LANE, LANE), lambda n: (0, 0, 0)),
                  pl.BlockSpec((LANE, n_g), lambda n: (0, 0)),
                  pl.BlockSpec((1, GW), lambda n: (0, 0))],
        out_specs=pl.BlockSpec((LANE, GW), lambda n: (n, 0)),
        out_shape=jax.ShapeDtypeStruct((S, GW), MXU_DTYPE),
        scratch_shapes=[pltpu.VMEM((n_g, LANE, LANE), MXU_DTYPE)],
        compiler_params=_params("arbitrary"),
    )(proj, v_gain, w_s, b_t, out_gain)


def _gmlp_bwd(proj, d_on, v_gain, w_s, b_t, out_gain, n_g, name):
    S = proj.shape[0]
    GW = n_g * LANE

    def body(p_ref, dn_ref, vg_ref, ws_ref, bt_ref, og_ref, dp_ref, dws_ref, dbt_ref, dvg_ref, dog_ref, wt_ref):
        @pl.when(pl.program_id(0) == 0)
        def _():
            _causal_weights(ws_ref, wt_ref, n_g)
            dws_ref[...] = jnp.zeros_like(dws_ref)
            dbt_ref[...] = jnp.zeros_like(dbt_ref)
            dvg_ref[...] = jnp.zeros_like(dvg_ref)
            dog_ref[...] = jnp.zeros_like(dog_ref)

        row = lax.broadcasted_iota(jnp.int32, (LANE, LANE), 0)
        col = lax.broadcasted_iota(jnp.int32, (LANE, LANE), 1)
        for g in range(n_g):
            cols = slice(g * LANE, (g + 1) * LANE)
            vcols = slice(GW + g * LANE, GW + (g + 1) * LANE)
            pu, pv = p_ref[:, cols], p_ref[:, vcols]
            u, v = _gelu(pu), _gelu(pv)
            vhat, rstd = _group_layernorm(v)
            gain = vg_ref[:, cols]
            vln = (vhat * gain).astype(MXU_DTYPE)
            mixed = _dot(wt_ref[g], vln, NN) + bt_ref[:, g:g + 1]
            o = u * mixed
            r = lax.rsqrt(_mean1(o * o) + EPS)
            oh = o * r
            dn = dn_ref[:, cols]
            dog_ref[:, cols] += _sum0(dn * oh)
            dhn = dn * og_ref[:, cols]
            d_o = r * (dhn - oh * _mean1(dhn * oh))
            du = d_o * mixed
            dmix = d_o * u
            dbt_ref[:, g:g + 1] += jnp.sum(dmix, axis=1, keepdims=True)
            dmix_b = dmix.astype(MXU_DTYPE)
            dws_ref[g] += jnp.where(col <= row, _dot(dmix_b, vln, NT), 0.0)
            dvln = _dot(wt_ref[g], dmix_b, TN)
            dvg_ref[:, cols] += _sum0(dvln * vhat)
            dxh = dvln * gain
            dv = rstd * (dxh - _mean1(dxh) - vhat * _mean1(dxh * vhat))
            dp_ref[:, cols] = (du * _gelu_grad(pu)).astype(dp_ref.dtype)
            dp_ref[:, vcols] = (dv * _gelu_grad(pv)).astype(dp_ref.dtype)

    return pl.pallas_call(
        body, name=name, grid=(S // LANE,),
        in_specs=[pl.BlockSpec((LANE, 2 * GW), lambda n: (n, 0)),
                  pl.BlockSpec((LANE, GW), lambda n: (n, 0)),
                  pl.BlockSpec((1, GW), lambda n: (0, 0)),
                  pl.BlockSpec((n_g, LANE, LANE), lambda n: (0, 0, 0)),
                  pl.BlockSpec((LANE, n_g), lambda n: (0, 0)),
                  pl.BlockSpec((1, GW), lambda n: (0, 0))],
        out_specs=[pl.BlockSpec((LANE, 2 * GW), lambda n: (n, 0)),
                   pl.BlockSpec((n_g, LANE, LANE), lambda n: (0, 0, 0)),
                   pl.BlockSpec((LANE, n_g), lambda n: (0, 0)),
                   pl.BlockSpec((1, GW), lambda n: (0, 0)),
                   pl.BlockSpec((1, GW), lambda n: (0, 0))],
        out_shape=[jax.ShapeDtypeStruct((S, 2 * GW), MXU_DTYPE),
                   jax.ShapeDtypeStruct((n_g, LANE, LANE), F32),
                   jax.ShapeDtypeStruct((LANE, n_g), F32),
                   jax.ShapeDtypeStruct((1, GW), F32),
                   jax.ShapeDtypeStruct((1, GW), F32)],
        scratch_shapes=[pltpu.VMEM((n_g, LANE, LANE), MXU_DTYPE)],
        compiler_params=_params("arbitrary"),
    )(proj, d_on, v_gain, w_s, b_t, out_gain)


def _tri_sum(v, tri):
    hi = v.astype(MXU_DTYPE)
    lo = (v - hi.astype(F32)).astype(MXU_DTYPE)
    return _dot(hi, tri, NN) + _dot(lo, tri, NN)


def _log_sigmoids(z):
    sp = jnp.log1p(jnp.exp(-jnp.abs(z)))
    return jnp.minimum(z, 0.0) - sp, jnp.minimum(-z, 0.0) - sp


def _rows(i):
    return pl.ds(pl.multiple_of(i * LANE, LANE), LANE)


def _sb_specs(S, n_g, n_h):
    base = 2 * n_g
    q_spec = pl.BlockSpec((S, LANE), lambda h: (0, base + h))
    k_spec = pl.BlockSpec((S, LANE), lambda h: (0, base + n_h + h))
    v_spec = pl.BlockSpec((S, LANE), lambda h: (0, base + 2 * n_h + h))
    gain_spec = pl.BlockSpec((1, LANE), lambda h: (0, n_g + h))
    head_spec = pl.BlockSpec((S, LANE), lambda h: (0, h))
    return q_spec, k_spec, v_spec, gain_spec, head_spec


def _sb_fwd(proj, out_gain, n_g, n_h, name):
    S = proj.shape[0]
    NQ = S // LANE
    scale = LANE ** -0.5
    q_spec, k_spec, v_spec, gain_spec, head_spec = _sb_specs(S, n_g, n_h)

    def body(q_ref, k_ref, v_ref, og_ref, o_ref, on_ref, ls_ref, qb, kb, vb):
        qb[...] = q_ref[...].astype(MXU_DTYPE)
        kb[...] = k_ref[...].astype(MXU_DTYPE)
        vb[...] = v_ref[...].astype(MXU_DTYPE)
        row = lax.broadcasted_iota(jnp.int32, (LANE, LANE), 0)
        col = lax.broadcasted_iota(jnp.int32, (LANE, LANE), 1)
        strict = col < row
        after = (row > col).astype(MXU_DTYPE)

        def block(qi, j, ctail, acc, diag):
            z = _dot(qi, kb[_rows(j), :], NT) * scale
            lb, l1m = _log_sigmoids(z)
            if diag:
                l1m = jnp.where(strict, l1m, 0.0)
            a = jnp.exp(lb + ctail + _tri_sum(l1m, after))
            if diag:
                a = jnp.where(strict, a, 0.0)
            acc = acc + _dot(a.astype(MXU_DTYPE), vb[_rows(j), :], NN)
            return ctail + jnp.sum(l1m, axis=1, keepdims=True), acc

        def q_loop(i, carry):
            qi = qb[_rows(i), :]
            state = block(qi, i, jnp.zeros((LANE, 1), F32), jnp.zeros((LANE, LANE), F32), True)
            ctail, acc = lax.fori_loop(0, i, lambda jj, st: block(qi, i - 1 - jj, st[0], st[1], False), state)
            ls_ref[_rows(i), :] = jnp.broadcast_to(ctail, (LANE, LANE))
            o_ref[_rows(i), :] = acc
            r = lax.rsqrt(_mean1(acc * acc) + EPS)
            on_ref[_rows(i), :] = (acc * r * og_ref[...]).astype(on_ref.dtype)
            return carry

        lax.fori_loop(0, NQ, q_loop, 0)

    return pl.pallas_call(
        body, name=name, grid=(n_h,),
        in_specs=[q_spec, k_spec, v_spec, gain_spec],
        out_specs=[head_spec, head_spec, head_spec],
        out_shape=[jax.ShapeDtypeStruct((S, n_h * LANE), F32), jax.ShapeDtypeStruct((S, n_h * LANE), MXU_DTYPE),
                   jax.ShapeDtypeStruct((S, n_h * LANE), F32)],
        scratch_shapes=[pltpu.VMEM((S, LANE), MXU_DTYPE)] * 3,
        compiler_params=_params("parallel"),
    )(proj, proj, proj, out_gain)


def _sb_bwd(proj, o_sb, l_sum, d_on, out_gain, n_g, n_h, name):
    S = proj.shape[0]
    NQ = S // LANE
    scale = LANE ** -0.5
    q_spec, k_spec, v_spec, gain_spec, head_spec = _sb_specs(S, n_g, n_h)
    dn_spec = pl.BlockSpec((S, LANE), lambda h: (0, n_g + h))
    dgain_spec = pl.BlockSpec((1, LANE), lambda h: (0, h))

    def body(q_ref, k_ref, v_ref, o_ref, ls_ref, dn_ref, og_ref, dq_ref, dk_ref, dv_ref, dog_ref,
             qb, kb, vb, dob, dk_acc, dv_acc):
        qb[...] = q_ref[...].astype(MXU_DTYPE)
        kb[...] = k_ref[...].astype(MXU_DTYPE)
        vb[...] = v_ref[...].astype(MXU_DTYPE)
        o, dn = o_ref[...], dn_ref[...]
        r = lax.rsqrt(_mean1(o * o) + EPS)
        oh = o * r
        dog_ref[...] = _sum0(dn * oh)
        dhn = dn * og_ref[...]
        dob[...] = (r * (dhn - oh * _mean1(dhn * oh))).astype(MXU_DTYPE)
        dk_acc[...] = jnp.zeros_like(dk_acc)
        dv_acc[...] = jnp.zeros_like(dv_acc)

        row = lax.broadcasted_iota(jnp.int32, (LANE, LANE), 0)
        col = lax.broadcasted_iota(jnp.int32, (LANE, LANE), 1)
        strict = col < row
        up_to = (row <= col).astype(MXU_DTYPE)
        before = (row < col).astype(MXU_DTYPE)

        def block(qi, doi, ltot, j, cl, cdl, dq, diag):
            kj, vj = kb[_rows(j), :], vb[_rows(j), :]
            z = _dot(qi, kj, NT) * scale
            lb, l1m_all = _log_sigmoids(z)
            l1m = jnp.where(strict, l1m_all, 0.0) if diag else l1m_all
            a = jnp.exp(lb + (ltot - (cl + _tri_sum(l1m, up_to))))
            if diag:
                a = jnp.where(strict, a, 0.0)
            dl = _dot(doi, vj, NT) * a
            d_l1m = cdl + _tri_sum(dl, before)
            dz = dl * jnp.exp(l1m_all) - jnp.exp(lb) * d_l1m
            if diag:
                dz = jnp.where(strict, dz, 0.0)
            dzs = (dz * scale).astype(MXU_DTYPE)
            dq = dq + _dot(dzs, kj, NN)
            dk_acc[_rows(j), :] += _dot(dzs, qi, TN)
            dv_acc[_rows(j), :] += _dot(a.astype(MXU_DTYPE), doi, TN)
            return (cl + jnp.sum(l1m, axis=1, keepdims=True), cdl + jnp.sum(dl, axis=1, keepdims=True), dq)

        def q_loop(i, carry):
            qi, doi, ltot = qb[_rows(i), :], dob[_rows(i), :], ls_ref[_rows(i), :]
            zero_col = jnp.zeros((LANE, 1), F32)
            state = lax.fori_loop(
                0, i, lambda j, st: block(qi, doi, ltot, j, st[0], st[1], st[2], False),
                (zero_col, zero_col, jnp.zeros((LANE, LANE), F32)))
            _, _, dq = block(qi, doi, ltot, i, state[0], state[1], state[2], True)
            dq_ref[_rows(i), :] = dq.astype(dq_ref.dtype)
            return carry

        lax.fori_loop(0, NQ, q_loop, 0)
        dk_ref[...] = dk_acc[...].astype(dk_ref.dtype)
        dv_ref[...] = dv_acc[...].astype(dv_ref.dtype)

    W = n_h * LANE
    return pl.pallas_call(
        body, name=name, grid=(n_h,),
        in_specs=[q_spec, k_spec, v_spec, head_spec, head_spec, dn_spec, gain_spec],
        out_specs=[head_spec, head_spec, head_spec, dgain_spec],
        out_shape=[jax.ShapeDtypeStruct((S, W), MXU_DTYPE)] * 3 + [jax.ShapeDtypeStruct((1, W), F32)],
        scratch_shapes=[pltpu.VMEM((S, LANE), MXU_DTYPE)] * 4 + [pltpu.VMEM((S, LANE), F32)] * 2,
        compiler_params=_params("parallel"),
    )(proj, proj, proj, o_sb, l_sum, d_on, out_gain)


def _mod_part(c_all, w_ada, b_ada_cols, name):
    B, K = c_all.shape
    N = w_ada.shape[1]
    tn = _tile(N, 512)

    def body(c_ref, w_ref, b_ref, o_ref):
        cv = c_ref[...]
        ca = (cv * jax.nn.sigmoid(cv)).astype(MXU_DTYPE)
        o_ref[...] = _dot(ca, w_ref[...].astype(MXU_DTYPE), NN) + b_ref[...]

    return pl.pallas_call(
        body, name=name, grid=(N // tn,),
        in_specs=[pl.BlockSpec((B, K), lambda j: (0, 0)), pl.BlockSpec((K, tn), lambda j: (0, j)),
                  pl.BlockSpec((1, tn), lambda j: (0, j))],
        out_specs=pl.BlockSpec((B, tn), lambda j: (0, j)),
        out_shape=jax.ShapeDtypeStruct((B, N), F32), compiler_params=_params("parallel"))(c_all, w_ada, b_ada_cols)


def _adamw_math(w, g, m, v):
    m = ADAM_B1 * m + (1.0 - ADAM_B1) * g
    v = ADAM_B2 * v + (1.0 - ADAM_B2) * (g * g)
    m_hat = m / (1.0 - ADAM_B1 ** ADAM_STEP)
    v_hat = v / (1.0 - ADAM_B2 ** ADAM_STEP)
    delta = -ADAM_LR * (m_hat / (jnp.sqrt(v_hat) + ADAM_EPS) + ADAM_WD * w)
    return delta, m, v


def _adamw(w, g, m, v, name):
    R, C = w.shape
    tr = _tile(R, max(8, (1 << 19) // C), 8)
    spec = pl.BlockSpec((tr, C), lambda i: (i, 0))

    def body(w_ref, g_ref, m_ref, v_ref, d_ref, mo_ref, vo_ref):
        d_ref[...], mo_ref[...], vo_ref[...] = _adamw_math(w_ref[...], g_ref[...], m_ref[...], v_ref[...])

    return pl.pallas_call(body, name=name, grid=(R // tr,), in_specs=[spec] * 4, out_specs=[spec] * 3,
                          out_shape=[jax.ShapeDtypeStruct((R, C), F32)] * 3, compiler_params=_params("parallel"))(w, g, m, v)


def _adamw_ada(c_all, dmod_cols, w, m, v, name):
    K, N = w.shape
    B = c_all.shape[0]
    tk, tn = _tile(K, 512), _tile(N, 1024)
    spec = pl.BlockSpec((tk, tn), lambda i, j: (i, j))

    def body(c_ref, dm_ref, w_ref, m_ref, v_ref, g_ref, d_ref, mo_ref, vo_ref):
        cv = c_ref[...]
        ca = (cv * jax.nn.sigmoid(cv)).astype(MXU_DTYPE)
        g = _dot(ca, dm_ref[...].astype(MXU_DTYPE), TN)
        g_ref[...] = g
        d_ref[...], mo_ref[...], vo_ref[...] = _adamw_math(w_ref[...], g, m_ref[...], v_ref[...])

    return pl.pallas_call(
        body, name=name, grid=(K // tk, N // tn),
        in_specs=[pl.BlockSpec((B, tk), lambda i, j: (0, i)), pl.BlockSpec((B, tn), lambda i, j: (0, j)), spec, spec, spec],
        out_specs=[spec] * 4, out_shape=[jax.ShapeDtypeStruct((K, N), F32)] * 4,
        compiler_params=_params("parallel", "parallel"))(c_all, dmod_cols, w, m, v)


def _sum_devices(gathered, n_dev, name):
    R = gathered.shape[0] // n_dev
    C = gathered.shape[1]
    tr = _tile(R, 512, 8)
    n_blk = R // tr

    def body(*refs):
        acc = refs[0][...]
        for r in refs[1:n_dev]:
            acc = acc + r[...]
        refs[n_dev][...] = acc

    in_specs = [pl.BlockSpec((tr, C), functools.partial(lambda i, d: (d * n_blk + i, 0), d=d)) for d in range(n_dev)]
    return pl.pallas_call(body, name=name, grid=(n_blk,), in_specs=in_specs,
                          out_specs=pl.BlockSpec((tr, C), lambda i: (i, 0)),
                          out_shape=jax.ShapeDtypeStruct((R, C), F32), compiler_params=_params("parallel"))(*([gathered] * n_dev))


def _place():
    x, y, c = lax.axis_index("x"), lax.axis_index("y"), lax.axis_index("c")
    return x, y, c


def _allgather8(blk, name):
    m_per, n = blk.shape

    def body(x_ref, out_ref, send_sems, recv_sems, local_sem):
        x, y, c = _place()
        me, sibling = (x, y, c), (x, y, 1 - c)
        chips = [(1 - x, y), (x, 1 - y), (1 - x, 1 - y)]

        def rows(px, py, pc):
            return out_ref.at[pl.ds((4 * px + 2 * py + pc) * m_per, m_per), :]

        def copy(k, block, to, src=None):
            return pltpu.make_async_remote_copy(
                src_ref=rows(*block) if src is None else src, dst_ref=rows(*block),
                send_sem=send_sems.at[k], recv_sem=recv_sems.at[k], device_id=to, device_id_type=MESH)

        mine = pltpu.make_async_copy(x_ref, rows(*me), local_sem)
        mine.start()
        first = [copy(0, me, sibling, src=x_ref)]
        first += [copy(1 + j, me, (*chip, c), src=x_ref) for j, chip in enumerate(chips)]
        for cp in first:
            cp.start()
        passed = [copy(4 + j, (*chip, c), sibling) for j, chip in enumerate(chips)]
        for j, chip in enumerate(chips):
            copy(1 + j, (*chip, c), me).wait_recv()
            passed[j].start()
        copy(0, sibling, me).wait_recv()
        for j, chip in enumerate(chips):
            copy(4 + j, (*chip, 1 - c), me).wait_recv()
        for cp in first + passed:
            cp.wait_send()
        mine.wait()

    return pl.pallas_call(
        body, name=name,
        out_shape=jax.ShapeDtypeStruct((8 * m_per, n), blk.dtype),
        in_specs=[pl.BlockSpec(memory_space=pltpu.VMEM)],
        out_specs=pl.BlockSpec(memory_space=pltpu.VMEM),
        scratch_shapes=[pltpu.SemaphoreType.DMA((7,)), pltpu.SemaphoreType.DMA((7,)), pltpu.SemaphoreType.DMA],
        compiler_params=pltpu.CompilerParams(vmem_limit_bytes=V7X_VMEM_LIMIT),
    )(blk)


class _Sharded:
    def __init__(self, shard_shape, by_cols):
        r, c = shard_shape
        self.by_cols = by_cols
        self.full = (r, N_CHIPS * c) if by_cols else (N_CHIPS * r, c)
        self.shard = (r, c)
        self.half_rows = r // 2
        self.half = (r // 2, c)

    def shard_of(self, ref, k):
        r, c = self.shard
        return ref.at[:, pl.ds(k * c, c)] if self.by_cols else ref.at[pl.ds(k * r, r), :]

    def half_of(self, ref, k, hc):
        r, c = self.shard
        h = self.half_rows
        if self.by_cols:
            return ref.at[pl.ds(hc * h, h), pl.ds(k * c, c)]
        return ref.at[pl.ds(k * r + hc * h, h), :]

    def half_of_shard(self, ref, hc):
        return ref.at[pl.ds(hc * self.half_rows, self.half_rows), :]

    def part_of_halves(self, ref, k):
        r, c = self.shard
        h = self.half_rows
        return ref.at[:, pl.ds(k * c, c)] if self.by_cols else ref.at[pl.ds(k * h, h), :]


def _on_each_place(x, y, c, fn, by_chip=True, by_core=True):
    q = 2 * x + y
    for k in range(N_CHIPS if by_chip else 1):
        for cc in range(2 if by_core else 1):
            cond = None
            if by_chip:
                cond = q == k
            if by_core:
                cond = (c == cc) if cond is None else jnp.logical_and(cond, c == cc)
            pl.when(cond)(functools.partial(fn, k, cc))


def _chip_id(k, c):
    return (k // 2, k % 2, c)


def _gather_weights(shards, geoms, name):
    n_w = len(shards)

    def body(*refs):
        shard_refs, full_refs = refs[:n_w], refs[n_w:2 * n_w]
        send_sems, recv_sems, local_sems = refs[2 * n_w:]
        x, y, c = _place()

        def at_place(k, cc):
            def remote(slot, src, dst, to):
                return pltpu.make_async_remote_copy(src_ref=src, dst_ref=dst, send_sem=send_sems.at[slot],
                                                    recv_sem=recv_sems.at[slot], device_id=to, device_id_type=MESH)

            local, first, passed = [], [], []
            for i, (g, s_ref, f_ref) in enumerate(zip(geoms, shard_refs, full_refs)):
                cp = pltpu.make_async_copy(s_ref, g.shard_of(f_ref, k), local_sems.at[i])
                cp.start()
                local.append(cp)
                for j, flip in enumerate(FLIPS):
                    cp = remote(3 * i + j, g.half_of_shard(s_ref, cc), g.half_of(f_ref, k, cc), _chip_id(k ^ flip, cc))
                    cp.start()
                    first.append(cp)
            for i, (g, f_ref) in enumerate(zip(geoms, full_refs)):
                for j, flip in enumerate(FLIPS):
                    landed = g.half_of(f_ref, k ^ flip, cc)
                    remote(3 * i + j, landed, landed, _chip_id(k, cc)).wait_recv()
                    cp = remote(3 * n_w + 3 * i + j, landed, landed, _chip_id(k, 1 - cc))
                    cp.start()
                    passed.append(cp)
            for i, (g, f_ref) in enumerate(zip(geoms, full_refs)):
                for j, flip in enumerate(FLIPS):
                    from_sibling = g.half_of(f_ref, k ^ flip, 1 - cc)
                    remote(3 * n_w + 3 * i + j, from_sibling, from_sibling, _chip_id(k, cc)).wait_recv()
            for cp in first + passed:
                cp.wait_send()
            for cp in local:
                cp.wait()

        _on_each_place(x, y, c, at_place)

    return pl.pallas_call(
        body, name=name,
        out_shape=[jax.ShapeDtypeStruct(g.full, WIRE_DTYPE) for g in geoms],
        in_specs=[ANY] * n_w, out_specs=[ANY] * n_w,
        scratch_shapes=[pltpu.SemaphoreType.DMA((6 * n_w,)), pltpu.SemaphoreType.DMA((6 * n_w,)),
                        pltpu.SemaphoreType.DMA((n_w,))],
    )(*shards)


def _swap_core_halves(grads, geoms, name):
    n_w = len(grads)
    n_cp = sum(1 if g.by_cols else N_CHIPS for g in geoms)

    def body(*refs):
        g_refs, t_refs = refs[:n_w], refs[n_w:2 * n_w]
        send_sems, recv_sems = refs[2 * n_w:]
        x, y, c = _place()

        def at_place(_, cc):
            def pairs(hc):
                out = []
                for g, g_ref, t_ref in zip(geoms, g_refs, t_refs):
                    if g.by_cols:
                        out.append((g_ref.at[pl.ds(hc * g.half_rows, g.half_rows), :], t_ref))
                    else:
                        out += [(g.half_of(g_ref, k, hc), g.part_of_halves(t_ref, k)) for k in range(N_CHIPS)]
                return out

            sends = [pltpu.make_async_remote_copy(src_ref=src, dst_ref=dst, send_sem=send_sems.at[n],
                                                  recv_sem=recv_sems.at[n], device_id=(x, y, 1 - cc), device_id_type=MESH)
                     for n, (src, dst) in enumerate(pairs(1 - cc))]
            for cp in sends:
                cp.start()
            for n, (src, dst) in enumerate(pairs(cc)):
                pltpu.make_async_remote_copy(src_ref=src, dst_ref=dst, send_sem=send_sems.at[n], recv_sem=recv_sems.at[n],
                                             device_id=(x, y, cc), device_id_type=MESH).wait_recv()
            for cp in sends:
                cp.wait_send()

        _on_each_place(x, y, c, at_place, by_chip=False)

    return pl.pallas_call(
        body, name=name,
        out_shape=[jax.ShapeDtypeStruct((g.full[0] // 2, g.full[1]), F32) for g in geoms],
        in_specs=[ANY] * n_w, out_specs=[ANY] * n_w,
        scratch_shapes=[pltpu.SemaphoreType.DMA((n_cp,)), pltpu.SemaphoreType.DMA((n_cp,))],
    )(*grads)


def _scatter_chip_sums(sums, geoms, name):
    n_w = len(sums)

    def body(*refs):
        s_refs, r_refs = refs[:n_w], refs[n_w:2 * n_w]
        send_sems, recv_sems = refs[2 * n_w:]
        x, y, c = _place()

        def at_place(k, _):
            sends = []
            for i, (g, s_ref, r_ref) in enumerate(zip(geoms, s_refs, r_refs)):
                for j, flip in enumerate(FLIPS):
                    kk = k ^ flip
                    cp = pltpu.make_async_remote_copy(
                        src_ref=g.part_of_halves(s_ref, kk), dst_ref=r_ref.at[j], send_sem=send_sems.at[3 * i + j],
                        recv_sem=recv_sems.at[3 * i + j], device_id=(kk // 2, kk % 2, c), device_id_type=MESH)
                    cp.start()
                    sends.append(cp)
            for i, (g, s_ref, r_ref) in enumerate(zip(geoms, s_refs, r_refs)):
                for j in range(len(FLIPS)):
                    pltpu.make_async_remote_copy(
                        src_ref=g.part_of_halves(s_ref, k), dst_ref=r_ref.at[j], send_sem=send_sems.at[3 * i + j],
                        recv_sem=recv_sems.at[3 * i + j], device_id=(x, y, c), device_id_type=MESH).wait_recv()
            for cp in sends:
                cp.wait_send()

        _on_each_place(x, y, c, at_place, by_core=False)

    return pl.pallas_call(
        body, name=name,
        out_shape=[jax.ShapeDtypeStruct((len(FLIPS),) + g.half, WIRE_DTYPE) for g in geoms],
        in_specs=[ANY] * n_w, out_specs=[ANY] * n_w,
        scratch_shapes=[pltpu.SemaphoreType.DMA((3 * n_w,)), pltpu.SemaphoreType.DMA((3 * n_w,))],
    )(*sums)


def _share_reduced_halves(reduced, geoms, name):
    n_w = len(reduced)

    def body(*refs):
        out_refs = refs[n_w:2 * n_w]
        send_sems, recv_sems = refs[2 * n_w:]
        x, y, c = _place()

        def at_place(_, cc):
            sends = []
            for i, (g, ref) in enumerate(zip(geoms, out_refs)):
                mine = g.half_of_shard(ref, cc)
                cp = pltpu.make_async_remote_copy(src_ref=mine, dst_ref=mine, send_sem=send_sems.at[i],
                                                  recv_sem=recv_sems.at[i], device_id=(x, y, 1 - cc), device_id_type=MESH)
                cp.start()
                sends.append(cp)
            for i, (g, ref) in enumerate(zip(geoms, out_refs)):
                theirs = g.half_of_shard(ref, 1 - cc)
                pltpu.make_async_remote_copy(src_ref=theirs, dst_ref=theirs, send_sem=send_sems.at[i],
                                             recv_sem=recv_sems.at[i], device_id=(x, y, cc), device_id_type=MESH).wait_recv()
            for cp in sends:
                cp.wait_send()

        _on_each_place(x, y, c, at_place, by_chip=False)

    return pl.pallas_call(
        body, name=name,
        out_shape=[jax.ShapeDtypeStruct(g.shard, F32) for g in geoms],
        in_specs=[ANY] * n_w, out_specs=[ANY] * n_w,
        input_output_aliases={i: i for i in range(n_w)},
        scratch_shapes=[pltpu.SemaphoreType.DMA((n_w,)), pltpu.SemaphoreType.DMA((n_w,))],
    )(*reduced)


def _chip_sum(place, grad, theirs, g, name):
    RH, C = theirs.shape
    h = g.half_rows
    tr = _tile(h, 256, 16)
    tc = _tile(C, 2048)
    per_half = h // tr

    if g.by_cols:
        grad_map = lambda i, j, p: (p[1] * per_half + i, j)
    else:
        grad_map = lambda i, j, p: ((i // per_half) * 2 * per_half + p[1] * per_half + i % per_half, j)

    def body(p_ref, a_ref, b_ref, o_ref):
        o_ref[...] = (a_ref[...] + b_ref[...]).astype(o_ref.dtype)

    return pl.pallas_call(
        body, name=name,
        grid_spec=pltpu.PrefetchScalarGridSpec(
            num_scalar_prefetch=1, grid=(RH // tr, C // tc),
            in_specs=[pl.BlockSpec((tr, tc), grad_map), pl.BlockSpec((tr, tc), lambda i, j, p: (i, j))],
            out_specs=pl.BlockSpec((tr, tc), lambda i, j, p: (i, j))),
        out_shape=jax.ShapeDtypeStruct((RH, C), WIRE_DTYPE),
        compiler_params=_params("parallel", "parallel"),
    )(place, grad, theirs)


def _reduce_half(place, grad, theirs, others, g, name):
    h, wc = g.half
    tr = _tile(h, 256, 16)
    per_half = h // tr
    if g.by_cols:
        tc = wc
        grad_map = lambda i, p: (p[1] * per_half + i, p[0])
        theirs_map = lambda i, p: (i, p[0])
    else:
        tc = wc
        grad_map = lambda i, p: (p[0] * 2 * per_half + p[1] * per_half + i, 0)
        theirs_map = lambda i, p: (p[0] * per_half + i, 0)

    def body(p_ref, a_ref, b_ref, o0_ref, o1_ref, o2_ref, out_ref):
        acc = a_ref[...] + b_ref[...]
        for o_ref in (o0_ref, o1_ref, o2_ref):
            acc = acc + o_ref[...].astype(F32)
        out_ref[...] = acc

    other_specs = [pl.BlockSpec((None, tr, tc), functools.partial(lambda i, p, j: (j, i, 0), j=j)) for j in range(len(FLIPS))]
    return pl.pallas_call(
        body, name=name,
        grid_spec=pltpu.PrefetchScalarGridSpec(
            num_scalar_prefetch=1, grid=(per_half,),
            in_specs=[pl.BlockSpec((tr, tc), grad_map), pl.BlockSpec((tr, tc), theirs_map)] + other_specs,
            out_specs=pl.BlockSpec((tr, tc), lambda i, p: (p[1] * per_half + i, 0))),
        out_shape=jax.ShapeDtypeStruct(g.shard, F32),
        compiler_params=_params("arbitrary"),
    )(place, grad, theirs, others, others, others)


SMALL = ("b_ada", "norm1_g", "v_norm_g", "w_spatial", "b_spatial", "out_norm_g", "norm2_g", "final_g")
BIG = ("w_in", "w_out", "w_gate", "w_up", "w_down")
BY_COLS = {"w_in": True, "w_out": False, "w_gate": True, "w_up": True, "w_down": False}
ORDER = ("w_ada", "b_ada", "norm1_g", "w_in", "v_norm_g", "w_spatial", "b_spatial", "out_norm_g", "w_out",
         "norm2_g", "w_gate", "w_up", "w_down", "final_g")


def _pack(parts):
    return jnp.concatenate([parts[n].reshape(-1) for n in SMALL]).reshape(-1, LANE)


def _unpack(slab, shapes):
    flat = slab.reshape(-1)
    out, at = {}, 0
    for n in SMALL:
        size = math.prod(shapes[n])
        out[n] = flat[at:at + size].reshape(shapes[n])
        at += size
    return out


def kernel(x, c, w_ada, b_ada, norm1_g, w_in, v_norm_g, w_spatial, b_spatial, out_norm_g, w_out, norm2_g, w_gate, w_up, w_down, final_g, loss_target, m_w_ada, m_b_ada, m_norm1_g, m_w_in, m_v_norm_g, m_w_spatial, m_b_spatial, m_out_norm_g, m_w_out, m_norm2_g, m_w_gate, m_w_up, m_w_down, m_final_g, v_w_ada, v_b_ada, v_norm1_g, v_w_in, v_v_norm_g, v_w_spatial, v_b_spatial, v_out_norm_g, v_w_out, v_norm2_g, v_w_gate, v_w_up, v_w_down, v_final_g):
    weights = dict(w_ada=w_ada, b_ada=b_ada, norm1_g=norm1_g, w_in=w_in, v_norm_g=v_norm_g, w_spatial=w_spatial,
                   b_spatial=b_spatial, out_norm_g=out_norm_g, w_out=w_out, norm2_g=norm2_g, w_gate=w_gate, w_up=w_up,
                   w_down=w_down, final_g=final_g)
    m_in = dict(w_ada=m_w_ada, b_ada=m_b_ada, norm1_g=m_norm1_g, w_in=m_w_in, v_norm_g=m_v_norm_g, w_spatial=m_w_spatial,
                b_spatial=m_b_spatial, out_norm_g=m_out_norm_g, w_out=m_w_out, norm2_g=m_norm2_g, w_gate=m_w_gate,
                w_up=m_w_up, w_down=m_w_down, final_g=m_final_g)
    v_in = dict(w_ada=v_w_ada, b_ada=v_b_ada, norm1_g=v_norm1_g, w_in=v_w_in, v_norm_g=v_v_norm_g, w_spatial=v_w_spatial,
                b_spatial=v_b_spatial, out_norm_g=v_out_norm_g, w_out=v_w_out, norm2_g=v_norm2_g, w_gate=v_w_gate,
                w_up=v_w_up, w_down=v_w_down, final_g=v_final_g)

    S, D = x.shape[1], x.shape[2]
    n_g = v_norm_g.shape[-1] // LANE
    n_h = (D - n_g * LANE) // LANE
    GW = n_g * LANE
    xi, yi, ci = _place()
    chip = 2 * xi + yi
    me = 4 * xi + 2 * yi + ci
    place = jnp.stack([chip, ci]).astype(jnp.int32)

    xs, target = x[0], loss_target[0]
    geoms = [_Sharded(weights[n].shape[1:], BY_COLS[n]) for n in BIG]

    full = dict(zip(BIG, _gather_weights([_cast(weights[n][0], WIRE_DTYPE, "cast_" + n) for n in BIG], geoms, "gather_weights")))

    c_pad = jnp.concatenate([c, jnp.zeros((7, D), F32)], axis=0)
    c_all = _allgather8(c_pad, "gather_c")[::8]
    n_ada = w_ada.shape[2]
    b_cols = lax.dynamic_slice(b_ada, (0, chip * n_ada), (1, n_ada))
    mod_parts = _allgather8(_mod_part(c_all, w_ada[0], b_cols, "mod_part"), "gather_mod")
    mod_all = mod_parts.reshape(N_CHIPS, 2, 8, n_ada)[:, 0].transpose(1, 0, 2).reshape(8, N_CHIPS * n_ada)
    mod = lax.dynamic_slice(mod_all, (me, 0), (1, 6 * D))
    shift1, scale1, gate1, shift2, scale2, gate2 = [mod[:, i * D:(i + 1) * D] for i in range(6)]

    b_t = b_spatial[0].T
    h1 = _norm_mod(xs, norm1_g, scale1, shift1, "norm1")
    proj, = _mm("nn", h1, full["w_in"], [F32], "proj")
    on_gm = _gmlp_fwd(proj, v_norm_g, w_spatial[0], b_t, out_norm_g, n_g, "gmlp_fwd")
    o_sb, on_sb, l_sum = _sb_fwd(proj, out_norm_g, n_g, n_h, "sb_fwd")
    o_n = jnp.concatenate([on_gm, on_sb], axis=1)
    attn, = _mm("nn", o_n, full["w_out"], [F32], "attn_out")
    x1, h2 = _residual_norm_mod(xs, attn, gate1, norm2_g, scale2, shift2, "norm2")
    a_g, a_u, f_in = _gate_up(h2, full["w_gate"], full["w_up"], "gate_up")
    f, = _mm("nn", f_in, full["w_down"], [F32], "down", tm=512)
    dx2, df, d_gate2, d_final_g, loss_part = _final_loss_bwd(x1, f, gate2, final_g.reshape(1, D), target, "final")
    loss = lax.psum(loss_part[0, 0], ("x", "y", "c"))

    grads = {}
    d_ag, d_au = _mm("nt", df, full["w_down"], [MXU_DTYPE, MXU_DTYPE], "d_ffn_in", extras=(a_g, a_u),
                     epilogue=_swiglu_bwd_epilogue)
    grads["w_down"], = _mm("tn", f_in, df, [F32], "d_w_down", tm=512, tn=1024)
    dh2_g, = _mm("nt", d_ag, full["w_gate"], [F32], "d_h2_gate", tm=512)
    dh2, = _mm("nt", d_au, full["w_up"], [F32], "d_h2", tm=512, extras=(dh2_g,), epilogue=_add_epilogue)
    grads["w_gate"], = _mm("tn", h2, d_ag, [F32], "d_w_gate")
    grads["w_up"], = _mm("tn", h2, d_au, [F32], "d_w_up")
    dx1, d_shift2, d_scale2, d_norm2_g, d_gate1, d_attn = _norm_mod_bwd(dh2, x1, dx2, norm2_g, scale2, "norm2_bwd",
                                                                        branch=attn, gate=gate1)
    d_on, = _mm("nt", d_attn, full["w_out"], [F32], "d_o")
    grads["w_out"], = _mm("tn", o_n, d_attn, [F32], "d_w_out")
    dp_gm, d_w_spatial, d_b_t, d_v_norm_g, d_og_gm = _gmlp_bwd(proj, d_on, v_norm_g, w_spatial[0], b_t, out_norm_g, n_g, "gmlp_bwd")
    dq, dk, dv, d_og_sb = _sb_bwd(proj, o_sb, l_sum, d_on, out_norm_g, n_g, n_h, "sb_bwd")
    dproj = jnp.concatenate([dp_gm, dq, dk, dv], axis=1)
    dh1, = _mm("nt", dproj, full["w_in"], [F32], "d_h1", tm=512)
    grads["w_in"], = _mm("tn", h1, dproj, [F32], "d_w_in")
    grad_x, d_shift1, d_scale1, d_norm1_g = _norm_mod_bwd(dh1, xs, dx1, norm1_g, scale1, "norm1_bwd")

    dmod = jnp.concatenate([d_shift1, d_scale1, d_gate1, d_shift2, d_scale2, d_gate2], axis=1)
    small_parts = dict(b_ada=dmod, norm1_g=d_norm1_g, v_norm_g=d_v_norm_g, w_spatial=d_w_spatial, b_spatial=d_b_t.T,
                       out_norm_g=jnp.concatenate([d_og_gm, d_og_sb], axis=1), norm2_g=d_norm2_g, final_g=d_final_g)
    slab = _pack(small_parts)
    rows = slab.shape[0]
    gathered = _allgather8(slab, "gather_small")
    small_shapes = {n: weights[n].shape for n in SMALL}
    small_sum = _sum_devices(gathered, 8, "sum_small")
    dmod_all = gathered.reshape(8, rows * LANE)[:, :6 * D]
    dmod_cols = lax.dynamic_slice(dmod_all, (0, chip * n_ada), (8, n_ada))

    theirs = _swap_core_halves([grads[n] for n in BIG], geoms, "swap_core_halves")
    sums = [_chip_sum(place, grads[n], t, g, "chip_sum_" + n) for n, t, g in zip(BIG, theirs, geoms)]
    others = _scatter_chip_sums(sums, geoms, "scatter_chip_sums")
    halves = [_reduce_half(place, grads[n], t, o, g, "reduce_" + n) for n, t, o, g in zip(BIG, theirs, others, geoms)]
    reduced = dict(zip(BIG, _share_reduced_halves(halves, geoms, "share_halves")))

    grad_out, delta, new_m, new_v = {}, {}, {}, {}
    for n in BIG:
        grad_out[n] = reduced[n][None]
        d, mo, vo = _adamw(weights[n][0], reduced[n], m_in[n][0], v_in[n][0], "adamw_" + n)
        delta[n], new_m[n], new_v[n] = d[None], mo[None], vo[None]
    g_ada, d, mo, vo = _adamw_ada(c_all, dmod_cols, w_ada[0], m_w_ada[0], v_w_ada[0], "adamw_w_ada")
    grad_out["w_ada"], delta["w_ada"], new_m["w_ada"], new_v["w_ada"] = g_ada[None], d[None], mo[None], vo[None]
    d, mo, vo = _adamw(_pack({n: weights[n] for n in SMALL}), small_sum, _pack({n: m_in[n] for n in SMALL}),
                       _pack({n: v_in[n] for n in SMALL}), "adamw_small")
    for dst, slab_out in ((grad_out, small_sum), (delta, d), (new_m, mo), (new_v, vo)):
        dst.update(_unpack(slab_out, small_shapes))

    return (loss, grad_x[None], *[grad_out[n] for n in ORDER], *[delta[n] for n in ORDER],
            *[new_m[n] for n in ORDER], *[new_v[n] for n in ORDER])
```

```python
import functools
import math

import jax
import jax.numpy as jnp
from jax import lax
from jax.experimental import pallas as pl
from jax.experimental.pallas import tpu as pltpu

F32 = jnp.float32
MXU_DTYPE = jnp.bfloat16
WIRE_DTYPE = jnp.bfloat16
EPS = 1e-6
LANE = 128
V7X_VMEM_LIMIT = 56 * 1024 * 1024
MESH = pl.DeviceIdType.MESH
N_CHIPS = 4
FLIPS = (2, 1, 3)
ANY = pl.BlockSpec(memory_space=pl.ANY)

ADAM_LR = 0.001
ADAM_B1 = 0.9
ADAM_B2 = 0.999
ADAM_EPS = 1e-08
ADAM_WD = 0.01
ADAM_STEP = 10


def _params(*semantics):
    return pltpu.CompilerParams(dimension_semantics=semantics or None, vmem_limit_bytes=V7X_VMEM_LIMIT)


def _tile(dim, pref, unit=LANE):
    best = None
    t = unit
    while t <= min(dim, pref):
        if dim % t == 0:
            best = t
        t += unit
    return best if best is not None else dim


def _sum0(v):
    return jnp.sum(v, axis=0, keepdims=True)


def _mean1(v):
    return jnp.mean(v, axis=-1, keepdims=True)


def _gelu(x):
    return 0.5 * x * (1.0 + lax.erf(x * (1.0 / math.sqrt(2.0))))


def _gelu_grad(x):
    cdf = 0.5 * (1.0 + lax.erf(x * (1.0 / math.sqrt(2.0))))
    return cdf + x * jnp.exp(-0.5 * x * x) * (1.0 / math.sqrt(2.0 * math.pi))


def _dot(a, b, dims):
    return lax.dot_general(a, b, (dims, ((), ())), preferred_element_type=F32)


NN = ((1,), (0,))
NT = ((1,), (1,))
TN = ((0,), (0,))


def _mm(kind, a, b, out_dtypes, name, tm=1024, tn=512, extras=(), epilogue=None):
    if kind == "nn":
        (M, K), N = a.shape, b.shape[1]
    elif kind == "nt":
        (M, K), N = a.shape, b.shape[0]
    else:
        (K, M), N = a.shape, b.shape[1]
    tm, tn = _tile(M, tm), _tile(N, tn)
    a_spec = pl.BlockSpec((K, tm), lambda i, j: (0, i)) if kind == "tn" else pl.BlockSpec((tm, K), lambda i, j: (i, 0))
    b_spec = pl.BlockSpec((tn, K), lambda i, j: (j, 0)) if kind == "nt" else pl.BlockSpec((K, tn), lambda i, j: (0, j))
    mn_spec = pl.BlockSpec((tm, tn), lambda i, j: (i, j))
    dims = {"nn": NN, "nt": NT, "tn": TN}[kind]
    n_extra = len(extras)

    def body(a_ref, b_ref, *rest):
        acc = _dot(a_ref[...], b_ref[...], dims)
        res = (acc,) if epilogue is None else epilogue(acc, *[e[...] for e in rest[:n_extra]])
        for o_ref, r in zip(rest[n_extra:], res):
            o_ref[...] = r.astype(o_ref.dtype)

    outs = pl.pallas_call(
        body, name=name, grid=(M // tm, N // tn),
        in_specs=[a_spec, b_spec] + [mn_spec] * n_extra,
        out_specs=[mn_spec] * len(out_dtypes),
        out_shape=[jax.ShapeDtypeStruct((M, N), d) for d in out_dtypes],
        compiler_params=_params("parallel", "arbitrary"),
    )(a, b, *extras)
    return outs


def _gate_up(h, wg, wu, name):
    (M, K), N = h.shape, wg.shape[1]
    tm, tn = _tile(M, 1024), _tile(N, 512)

    def body(h_ref, wg_ref, wu_ref, ag_ref, au_ref, f_ref):
        hv = h_ref[...]
        ag = _dot(hv, wg_ref[...], NN)
        au = _dot(hv, wu_ref[...], NN)
        ag_ref[...] = ag
        au_ref[...] = au
        f_ref[...] = (ag * jax.nn.sigmoid(ag) * au).astype(f_ref.dtype)

    w_spec = pl.BlockSpec((K, tn), lambda i, j: (0, j))
    mn_spec = pl.BlockSpec((tm, tn), lambda i, j: (i, j))
    return pl.pallas_call(
        body, name=name, grid=(M // tm, N // tn),
        in_specs=[pl.BlockSpec((tm, K), lambda i, j: (i, 0)), w_spec, w_spec],
        out_specs=[mn_spec] * 3,
        out_shape=[jax.ShapeDtypeStruct((M, N), F32), jax.ShapeDtypeStruct((M, N), F32),
                   jax.ShapeDtypeStruct((M, N), MXU_DTYPE)],
        compiler_params=_params("parallel", "arbitrary"),
    )(h, wg, wu)


def _swiglu_bwd_epilogue(dfin, ag, au):
    sg = jax.nn.sigmoid(ag)
    d_au = dfin * (ag * sg)
    d_ag = dfin * au * (sg * (1.0 + ag * (1.0 - sg)))
    return d_ag, d_au


def _add_epilogue(acc, other):
    return (acc + other,)


def _row_specs(ts, width):
    return pl.BlockSpec((ts, width), lambda i: (i, 0)), pl.BlockSpec((1, width), lambda i: (0, 0))


def _cast(a, dtype, name):
    R, C = a.shape
    tr = _tile(R, 512, 16)
    spec = pl.BlockSpec((tr, C), lambda i: (i, 0))

    def body(a_ref, o_ref):
        o_ref[...] = a_ref[...].astype(o_ref.dtype)

    return pl.pallas_call(body, name=name, grid=(R // tr,), in_specs=[spec], out_specs=spec,
                          out_shape=jax.ShapeDtypeStruct((R, C), dtype), compiler_params=_params("parallel"))(a)


def _norm_mod(x, g, scale, shift, name):
    S, D = x.shape
    ts = _tile(S, 256, 16)
    tile, vec = _row_specs(ts, D)

    def body(x_ref, g_ref, sc_ref, sh_ref, h_ref):
        xv = x_ref[...]
        r = lax.rsqrt(_mean1(xv * xv) + EPS)
        h_ref[...] = ((xv * r) * g_ref[...] * (1.0 + sc_ref[...]) + sh_ref[...]).astype(h_ref.dtype)

    return pl.pallas_call(body, name=name, grid=(S // ts,), in_specs=[tile, vec, vec, vec], out_specs=tile,
                          out_shape=jax.ShapeDtypeStruct((S, D), MXU_DTYPE), compiler_params=_params("parallel"))(x, g, scale, shift)


def _residual_norm_mod(x, attn, gate, g, scale, shift, name):
    S, D = x.shape
    ts = _tile(S, 256, 16)
    tile, vec = _row_specs(ts, D)

    def body(x_ref, a_ref, gate_ref, g_ref, sc_ref, sh_ref, x1_ref, h_ref):
        x1 = x_ref[...] + gate_ref[...] * a_ref[...]
        x1_ref[...] = x1
        r = lax.rsqrt(_mean1(x1 * x1) + EPS)
        h_ref[...] = ((x1 * r) * g_ref[...] * (1.0 + sc_ref[...]) + sh_ref[...]).astype(h_ref.dtype)

    return pl.pallas_call(body, name=name, grid=(S // ts,), in_specs=[tile, tile, vec, vec, vec, vec],
                          out_specs=[tile, tile],
                          out_shape=[jax.ShapeDtypeStruct((S, D), F32), jax.ShapeDtypeStruct((S, D), MXU_DTYPE)],
                          compiler_params=_params("parallel"))(x, attn, gate, g, scale, shift)


def _final_loss_bwd(x1, f, gate2, final_g, target, name):
    S, D = x1.shape
    ts = _tile(S, 256, 16)
    tile, vec = _row_specs(ts, D)
    loss_spec = pl.BlockSpec((1, LANE), lambda i: (0, 0))

    def body(x1_ref, f_ref, gate_ref, g_ref, t_ref, dx2_ref, df_ref, dgate_ref, dg_ref, loss_ref):
        @pl.when(pl.program_id(0) == 0)
        def _():
            dgate_ref[...] = jnp.zeros_like(dgate_ref)
            dg_ref[...] = jnp.zeros_like(dg_ref)
            loss_ref[...] = jnp.zeros_like(loss_ref)

        fv, gate, g = f_ref[...], gate_ref[...], g_ref[...]
        x2 = x1_ref[...] + gate * fv
        r = lax.rsqrt(_mean1(x2 * x2) + EPS)
        xn = x2 * r
        err = xn * g - t_ref[...]
        loss_ref[...] += jnp.broadcast_to(0.5 * _sum0(_mean1(err * err)), loss_ref.shape)
        dy = err * (1.0 / D)
        dg_ref[...] += _sum0(dy * xn)
        dxn = dy * g
        dx2 = r * (dxn - xn * _mean1(dxn * xn))
        dx2_ref[...] = dx2
        dgate_ref[...] += _sum0(dx2 * fv)
        df_ref[...] = (dx2 * gate).astype(df_ref.dtype)

    return pl.pallas_call(
        body, name=name, grid=(S // ts,), in_specs=[tile, tile, vec, vec, tile],
        out_specs=[tile, tile, vec, vec, loss_spec],
        out_shape=[jax.ShapeDtypeStruct((S, D), F32), jax.ShapeDtypeStruct((S, D), MXU_DTYPE),
                   jax.ShapeDtypeStruct((1, D), F32), jax.ShapeDtypeStruct((1, D), F32),
                   jax.ShapeDtypeStruct((1, LANE), F32)],
        compiler_params=_params("arbitrary"),
    )(x1, f, gate2, final_g, target)


def _norm_mod_bwd(dh, xin, dres, g, scale, name, branch=None, gate=None):
    S, D = xin.shape
    ts = _tile(S, 256, 16)
    tile, vec = _row_specs(ts, D)
    with_gate = branch is not None

    def body(*refs):
        if with_gate:
            dh_ref, x_ref, dres_ref, g_ref, sc_ref, br_ref, gate_ref, dx_ref, dshift_ref, dscale_ref, dg_ref, dgate_ref, dbr_ref = refs
            accs = (dshift_ref, dscale_ref, dg_ref, dgate_ref)
        else:
            dh_ref, x_ref, dres_ref, g_ref, sc_ref, dx_ref, dshift_ref, dscale_ref, dg_ref = refs
            accs = (dshift_ref, dscale_ref, dg_ref)

        @pl.when(pl.program_id(0) == 0)
        def _():
            for acc in accs:
                acc[...] = jnp.zeros_like(acc)

        dh_v, xv, g_v = dh_ref[...], x_ref[...], g_ref[...]
        one_sc = 1.0 + sc_ref[...]
        r = lax.rsqrt(_mean1(xv * xv) + EPS)
        xn = xv * r
        dshift_ref[...] += _sum0(dh_v)
        dscale_ref[...] += _sum0(dh_v * (xn * g_v))
        dg_ref[...] += _sum0(dh_v * one_sc * xn)
        dxn = dh_v * (g_v * one_sc)
        dx = dres_ref[...] + r * (dxn - xn * _mean1(dxn * xn))
        dx_ref[...] = dx
        if with_gate:
            dgate_ref[...] += _sum0(dx * br_ref[...])
            dbr_ref[...] = (dx * gate_ref[...]).astype(dbr_ref.dtype)

    ins = [dh, xin, dres, g, scale] + ([branch, gate] if with_gate else [])
    in_specs = [tile, tile, tile, vec, vec] + ([tile, vec] if with_gate else [])
    out_specs = [tile, vec, vec, vec] + ([vec, tile] if with_gate else [])
    out_shape = [jax.ShapeDtypeStruct((S, D), F32)] + [jax.ShapeDtypeStruct((1, D), F32)] * 3
    if with_gate:
        out_shape += [jax.ShapeDtypeStruct((1, D), F32), jax.ShapeDtypeStruct((S, D), MXU_DTYPE)]
    return pl.pallas_call(body, name=name, grid=(S // ts,), in_specs=in_specs, out_specs=out_specs,
                          out_shape=out_shape, compiler_params=_params("arbitrary"))(*ins)


def _causal_weights(ws_ref, wt_ref, n_g):
    row = lax.broadcasted_iota(jnp.int32, (LANE, LANE), 0)
    col = lax.broadcasted_iota(jnp.int32, (LANE, LANE), 1)
    for g in range(n_g):
        wt_ref[g] = jnp.where(col <= row, ws_ref[g], 0.0).astype(wt_ref.dtype)


def _group_layernorm(v):
    xc = v - _mean1(v)
    rstd = lax.rsqrt(_mean1(xc * xc) + EPS)
    return xc * rstd, rstd


def _gmlp_fwd(proj, v_gain, w_s, b_t, out_gain, n_g, name):
    S = proj.shape[0]
    GW = n_g * LANE

    def body(p_ref, vg_ref, ws_ref, bt_ref, og_ref, on_ref, wt_ref):
        @pl.when(pl.program_id(0) == 0)
        def _():
            _causal_weights(ws_ref, wt_ref, n_g)

        for g in range(n_g):
            cols = slice(g * LANE, (g + 1) * LANE)
            u = _gelu(p_ref[:, cols])
            v = _gelu(p_ref[:, GW + g * LANE:GW + (g + 1) * LANE])
            vhat, _ = _group_layernorm(v)
            vln = (vhat * vg_ref[:, cols]).astype(MXU_DTYPE)
            mixed = _dot(wt_ref[g], vln, NN) + bt_ref[:, g:g + 1]
            o = u * mixed
            r = lax.rsqrt(_mean1(o * o) + EPS)
            on_ref[:, cols] = (o * r * og_ref[:, cols]).astype(on_ref.dtype)

    return pl.pallas_call(
        body, name=name, grid=(S // LANE,),
        in_specs=[pl.BlockSpec((LANE, 2 * GW), lambda n: (n, 0)),
                  pl.BlockSpec((1, GW), lambda n: (0, 0)),
                  pl.BlockSpec((n_g, LANE, LANE), lambda n: (0, 0, 0)),
                  pl.BlockSpec((LANE, n_g), lambda n: (0, 0)),
                  pl.BlockSpec((1, GW), lambda n: (0, 0))],
        out_specs=pl.BlockSpec((LANE, GW), lambda n: (n, 0)),
        out_shape=jax.ShapeDtypeStruct((S, GW), MXU_DTYPE),
        scratch_shapes=[pltpu.VMEM((n_g, LANE, LANE), MXU_DTYPE)],
        compiler_params=_params("arbitrary"),
    )(proj, v_gain, w_s, b_t, out_gain)


def _gmlp_bwd(proj, d_on, v_gain, w_s, b_t, out_gain, n_g, name):
    S = proj.shape[0]
    GW = n_g * LANE

    def body(p_ref, dn_ref, vg_ref, ws_ref, bt_ref, og_ref, dp_ref, dws_ref, dbt_ref, dvg_ref, dog_ref, wt_ref):
        @pl.when(pl.program_id(0) == 0)
        def _():
            _causal_weights(ws_ref, wt_ref, n_g)
            dws_ref[...] = jnp.zeros_like(dws_ref)
            dbt_ref[...] = jnp.zeros_like(dbt_ref)
            dvg_ref[...] = jnp.zeros_like(dvg_ref)
            dog_ref[...] = jnp.zeros_like(dog_ref)

        row = lax.broadcasted_iota(jnp.int32, (LANE, LANE), 0)
        col = lax.broadcasted_iota(jnp.int32, (LANE, LANE), 1)
        for g in range(n_g):
            cols = slice(g * LANE, (g + 1) * LANE)
            vcols = slice(GW + g * LANE, GW + (g + 1) * LANE)
            pu, pv = p_ref[:, cols], p_ref[:, vcols]
            u, v = _gelu(pu), _gelu(pv)
            vhat, rstd = _group_layernorm(v)
            gain = vg_ref[:, cols]
            vln = (vhat * gain).astype(MXU_DTYPE)
            mixed = _dot(wt_ref[g], vln, NN) + bt_ref[:, g:g + 1]
            o = u * mixed
            r = lax.rsqrt(_mean1(o * o) + EPS)
            oh = o * r
            dn = dn_ref[:, cols]
            dog_ref[:, cols] += _sum0(dn * oh)
            dhn = dn * og_ref[:, cols]
            d_o = r * (dhn - oh * _mean1(dhn * oh))
            du = d_o * mixed
            dmix = d_o * u
            dbt_ref[:, g:g + 1] += jnp.sum(dmix, axis=1, keepdims=True)
            dmix_b = dmix.astype(MXU_DTYPE)
            dws_ref[g] += jnp.where(col <= row, _dot(dmix_b, vln, NT), 0.0)
            dvln = _dot(wt_ref[g], dmix_b, TN)
            dvg_ref[:, cols] += _sum0(dvln * vhat)
            dxh = dvln * gain
            dv = rstd * (dxh - _mean1(dxh) - vhat * _mean1(dxh * vhat))
            dp_ref[:, cols] = (du * _gelu_grad(pu)).astype(dp_ref.dtype)
            dp_ref[:, vcols] = (dv * _gelu_grad(pv)).astype(dp_ref.dtype)

    return pl.pallas_call(
        body, name=name, grid=(S // LANE,),
        in_specs=[pl.BlockSpec((LANE, 2 * GW), lambda n: (n, 0)),
                  pl.BlockSpec((LANE, GW), lambda n: (n, 0)),
                  pl.BlockSpec((1, GW), lambda n: (0, 0)),
                  pl.BlockSpec((n_g, LANE, LANE), lambda n: (0, 0, 0)),
                  pl.BlockSpec((LANE, n_g), lambda n: (0, 0)),
                  pl.BlockSpec((1, GW), lambda n: (0, 0))],
        out_specs=[pl.BlockSpec((LANE, 2 * GW), lambda n: (n, 0)),
                   pl.BlockSpec((n_g, LANE, LANE), lambda n: (0, 0, 0)),
                   pl.BlockSpec((LANE, n_g), lambda n: (0, 0)),
                   pl.BlockSpec((1, GW), lambda n: (0, 0)),
                   pl.BlockSpec((1, GW), lambda n: (0, 0))],
        out_shape=[jax.ShapeDtypeStruct((S, 2 * GW), MXU_DTYPE),
                   jax.ShapeDtypeStruct((n_g, LANE, LANE), F32),
                   jax.ShapeDtypeStruct((LANE, n_g), F32),
                   jax.ShapeDtypeStruct((1, GW), F32),
                   jax.ShapeDtypeStruct((1, GW), F32)],
        scratch_shapes=[pltpu.VMEM((n_g, LANE, LANE), MXU_DTYPE)],
        compiler_params=_params("arbitrary"),
    )(proj, d_on, v_gain, w_s, b_t, out_gain)


def _tri_sum(v, tri):
    hi = v.astype(MXU_DTYPE)
    lo = (v - hi.astype(F32)).astype(MXU_DTYPE)
    return _dot(hi, tri, NN) + _dot(lo, tri, NN)


def _log_sigmoids(z):
    sp = jnp.log1p(jnp.exp(-jnp.abs(z)))
    return jnp.minimum(z, 0.0) - sp, jnp.minimum(-z, 0.0) - sp


def _rows(i, size):
    return pl.ds(pl.multiple_of(i * size, size), size)


SB_QUERY_TILE = 512
SB_KEY_TILE = 256


def _sb_tiles(S):
    tq = _tile(S, SB_QUERY_TILE)
    tk = _tile(tq, SB_KEY_TILE)
    return tq, tk, S // tq, tq // tk


def _triangle(n, keep):
    row = lax.broadcasted_iota(jnp.int32, (n, n), 0)
    col = lax.broadcasted_iota(jnp.int32, (n, n), 1)
    return jnp.where(keep(row, col), 1.0, 0.0).astype(MXU_DTYPE)


def _strictly_before(tq, tk, key_offset):
    row = lax.broadcasted_iota(jnp.int32, (tq, tk), 0)
    col = lax.broadcasted_iota(jnp.int32, (tq, tk), 1)
    return col + key_offset < row


def _sb_specs(S, n_g, n_h):
    base = 2 * n_g
    q_spec = pl.BlockSpec((S, LANE), lambda h: (0, base + h))
    k_spec = pl.BlockSpec((S, LANE), lambda h: (0, base + n_h + h))
    v_spec = pl.BlockSpec((S, LANE), lambda h: (0, base + 2 * n_h + h))
    gain_spec = pl.BlockSpec((1, LANE), lambda h: (0, n_g + h))
    head_spec = pl.BlockSpec((S, LANE), lambda h: (0, h))
    return q_spec, k_spec, v_spec, gain_spec, head_spec


def _sb_fwd(proj, out_gain, n_g, n_h, name):
    S = proj.shape[0]
    TQ, TK, NQ, KPQ = _sb_tiles(S)
    scale = LANE ** -0.5
    q_spec, k_spec, v_spec, gain_spec, head_spec = _sb_specs(S, n_g, n_h)

    def body(q_ref, k_ref, v_ref, og_ref, o_ref, on_ref, ls_ref, qb, kb, vb):
        qb[...] = q_ref[...].astype(MXU_DTYPE)
        kb[...] = k_ref[...].astype(MXU_DTYPE)
        vb[...] = v_ref[...].astype(MXU_DTYPE)
        after = _triangle(TK, lambda r, c: r > c)

        def block(qi, j, ctail, acc, key_offset):
            z = _dot(qi, kb[_rows(j, TK), :], NT) * scale
            lb, l1m = _log_sigmoids(z)
            if key_offset is not None:
                strict = _strictly_before(TQ, TK, key_offset)
                l1m = jnp.where(strict, l1m, 0.0)
            a = jnp.exp(lb + ctail + _tri_sum(l1m, after))
            if key_offset is not None:
                a = jnp.where(strict, a, 0.0)
            acc = acc + _dot(a.astype(MXU_DTYPE), vb[_rows(j, TK), :], NN)
            return ctail + jnp.sum(l1m, axis=1, keepdims=True), acc

        def q_loop(i, carry):
            qi = qb[_rows(i, TQ), :]
            state = (jnp.zeros((TQ, 1), F32), jnp.zeros((TQ, LANE), F32))
            for d in reversed(range(KPQ)):
                state = block(qi, i * KPQ + d, state[0], state[1], d * TK)
            ctail, acc = lax.fori_loop(
                0, i * KPQ, lambda jj, st: block(qi, i * KPQ - 1 - jj, st[0], st[1], None), state)
            ls_ref[_rows(i, TQ), :] = jnp.broadcast_to(ctail, (TQ, LANE))
            o_ref[_rows(i, TQ), :] = acc
            r = lax.rsqrt(_mean1(acc * acc) + EPS)
            on_ref[_rows(i, TQ), :] = (acc * r * og_ref[...]).astype(on_ref.dtype)
            return carry

        lax.fori_loop(0, NQ, q_loop, 0)

    return pl.pallas_call(
        body, name=name, grid=(n_h,),
        in_specs=[q_spec, k_spec, v_spec, gain_spec],
        out_specs=[head_spec, head_spec, head_spec],
        out_shape=[jax.ShapeDtypeStruct((S, n_h * LANE), F32), jax.ShapeDtypeStruct((S, n_h * LANE), MXU_DTYPE),
                   jax.ShapeDtypeStruct((S, n_h * LANE), F32)],
        scratch_shapes=[pltpu.VMEM((S, LANE), MXU_DTYPE)] * 3,
        compiler_params=_params("parallel"),
    )(proj, proj, proj, out_gain)


def _sb_bwd(proj, o_sb, l_sum, d_on, out_gain, n_g, n_h, name):
    S = proj.shape[0]
    TQ, TK, NQ, KPQ = _sb_tiles(S)
    scale = LANE ** -0.5
    q_spec, k_spec, v_spec, gain_spec, head_spec = _sb_specs(S, n_g, n_h)
    dn_spec = pl.BlockSpec((S, LANE), lambda h: (0, n_g + h))
    dgain_spec = pl.BlockSpec((1, LANE), lambda h: (0, h))

    def body(q_ref, k_ref, v_ref, o_ref, ls_ref, dn_ref, og_ref, dq_ref, dk_ref, dv_ref, dog_ref,
             qb, kb, vb, dob, dk_acc, dv_acc):
        qb[...] = q_ref[...].astype(MXU_DTYPE)
        kb[...] = k_ref[...].astype(MXU_DTYPE)
        vb[...] = v_ref[...].astype(MXU_DTYPE)
        o, dn = o_ref[...], dn_ref[...]
        r = lax.rsqrt(_mean1(o * o) + EPS)
        oh = o * r
        dog_ref[...] = _sum0(dn * oh)
        dhn = dn * og_ref[...]
        dob[...] = (r * (dhn - oh * _mean1(dhn * oh))).astype(MXU_DTYPE)
        dk_acc[...] = jnp.zeros_like(dk_acc)
        dv_acc[...] = jnp.zeros_like(dv_acc)

        up_to = _triangle(TK, lambda r, c: r <= c)
        before = _triangle(TK, lambda r, c: r < c)

        def block(qi, doi, ltot, j, cl, cdl, dq, key_offset):
            kj, vj = kb[_rows(j, TK), :], vb[_rows(j, TK), :]
            z = _dot(qi, kj, NT) * scale
            lb, l1m_all = _log_sigmoids(z)
            l1m = l1m_all
            if key_offset is not None:
                strict = _strictly_before(TQ, TK, key_offset)
                l1m = jnp.where(strict, l1m_all, 0.0)
            a = jnp.exp(lb + (ltot - (cl + _tri_sum(l1m, up_to))))
            if key_offset is not None:
                a = jnp.where(strict, a, 0.0)
            dl = _dot(doi, vj, NT) * a
            d_l1m = cdl + _tri_sum(dl, before)
            dz = dl * jnp.exp(l1m_all) - jnp.exp(lb) * d_l1m
            if key_offset is not None:
                dz = jnp.where(strict, dz, 0.0)
            dzs = (dz * scale).astype(MXU_DTYPE)
            dq = dq + _dot(dzs, kj, NN)
            dk_acc[_rows(j, TK), :] += _dot(dzs, qi, TN)
            dv_acc[_rows(j, TK), :] += _dot(a.astype(MXU_DTYPE), doi, TN)
            return (cl + jnp.sum(l1m, axis=1, keepdims=True), cdl + jnp.sum(dl, axis=1, keepdims=True), dq)

        def q_loop(i, carry):
            qi, doi = qb[_rows(i, TQ), :], dob[_rows(i, TQ), :]
            ltot = ls_ref[_rows(i, TQ), :][:, :1]
            zero_col = jnp.zeros((TQ, 1), F32)
            state = lax.fori_loop(
                0, i * KPQ, lambda j, st: block(qi, doi, ltot, j, st[0], st[1], st[2], None),
                (zero_col, zero_col, jnp.zeros((TQ, LANE), F32)))
            for d in range(KPQ):
                state = block(qi, doi, ltot, i * KPQ + d, state[0], state[1], state[2], d * TK)
            dq_ref[_rows(i, TQ), :] = state[2].astype(dq_ref.dtype)
            return carry

        lax.fori_loop(0, NQ, q_loop, 0)
        dk_ref[...] = dk_acc[...].astype(dk_ref.dtype)
        dv_ref[...] = dv_acc[...].astype(dv_ref.dtype)

    W = n_h * LANE
    return pl.pallas_call(
        body, name=name, grid=(n_h,),
        in_specs=[q_spec, k_spec, v_spec, head_spec, head_spec, dn_spec, gain_spec],
        out_specs=[head_spec, head_spec, head_spec, dgain_spec],
        out_shape=[jax.ShapeDtypeStruct((S, W), MXU_DTYPE)] * 3 + [jax.ShapeDtypeStruct((1, W), F32)],
        scratch_shapes=[pltpu.VMEM((S, LANE), MXU_DTYPE)] * 4 + [pltpu.VMEM((S, LANE), F32)] * 2,
        compiler_params=_params("parallel"),
    )(proj, proj, proj, o_sb, l_sum, d_on, out_gain)


def _mod_part(c_all, w_ada, b_ada_cols, name):
    B, K = c_all.shape
    N = w_ada.shape[1]
    tn = _tile(N, 512)

    def body(c_ref, w_ref, b_ref, o_ref):
        cv = c_ref[...]
        ca = (cv * jax.nn.sigmoid(cv)).astype(MXU_DTYPE)
        o_ref[...] = _dot(ca, w_ref[...].astype(MXU_DTYPE), NN) + b_ref[...]

    return pl.pallas_call(
        body, name=name, grid=(N // tn,),
        in_specs=[pl.BlockSpec((B, K), lambda j: (0, 0)), pl.BlockSpec((K, tn), lambda j: (0, j)),
                  pl.BlockSpec((1, tn), lambda j: (0, j))],
        out_specs=pl.BlockSpec((B, tn), lambda j: (0, j)),
        out_shape=jax.ShapeDtypeStruct((B, N), F32), compiler_params=_params("parallel"))(c_all, w_ada, b_ada_cols)


def _adamw_math(w, g, m, v):
    m = ADAM_B1 * m + (1.0 - ADAM_B1) * g
    v = ADAM_B2 * v + (1.0 - ADAM_B2) * (g * g)
    m_hat = m / (1.0 - ADAM_B1 ** ADAM_STEP)
    v_hat = v / (1.0 - ADAM_B2 ** ADAM_STEP)
    delta = -ADAM_LR * (m_hat / (jnp.sqrt(v_hat) + ADAM_EPS) + ADAM_WD * w)
    return delta, m, v


def _adamw(w, g, m, v, name):
    R, C = w.shape
    tr = _tile(R, max(8, (1 << 19) // C), 8)
    spec = pl.BlockSpec((tr, C), lambda i: (i, 0))

    def body(w_ref, g_ref, m_ref, v_ref, d_ref, mo_ref, vo_ref):
        d_ref[...], mo_ref[...], vo_ref[...] = _adamw_math(w_ref[...], g_ref[...], m_ref[...], v_ref[...])

    return pl.pallas_call(body, name=name, grid=(R // tr,), in_specs=[spec] * 4, out_specs=[spec] * 3,
                          out_shape=[jax.ShapeDtypeStruct((R, C), F32)] * 3, compiler_params=_params("parallel"))(w, g, m, v)


def _adamw_ada(c_all, dmod_cols, w, m, v, name):
    K, N = w.shape
    B = c_all.shape[0]
    tk, tn = _tile(K, 512), _tile(N, 1024)
    spec = pl.BlockSpec((tk, tn), lambda i, j: (i, j))

    def body(c_ref, dm_ref, w_ref, m_ref, v_ref, g_ref, d_ref, mo_ref, vo_ref):
        cv = c_ref[...]
        ca = (cv * jax.nn.sigmoid(cv)).astype(MXU_DTYPE)
        g = _dot(ca, dm_ref[...].astype(MXU_DTYPE), TN)
        g_ref[...] = g
        d_ref[...], mo_ref[...], vo_ref[...] = _adamw_math(w_ref[...], g, m_ref[...], v_ref[...])

    return pl.pallas_call(
        body, name=name, grid=(K // tk, N // tn),
        in_specs=[pl.BlockSpec((B, tk), lambda i, j: (0, i)), pl.BlockSpec((B, tn), lambda i, j: (0, j)), spec, spec, spec],
        out_specs=[spec] * 4, out_shape=[jax.ShapeDtypeStruct((K, N), F32)] * 4,
        compiler_params=_params("parallel", "parallel"))(c_all, dmod_cols, w, m, v)


def _sum_devices(gathered, n_dev, name):
    R = gathered.shape[0] // n_dev
    C = gathered.shape[1]
    tr = _tile(R, 512, 8)
    n_blk = R // tr

    def body(*refs):
        acc = refs[0][...]
        for r in refs[1:n_dev]:
            acc = acc + r[...]
        refs[n_dev][...] = acc

    in_specs = [pl.BlockSpec((tr, C), functools.partial(lambda i, d: (d * n_blk + i, 0), d=d)) for d in range(n_dev)]
    return pl.pallas_call(body, name=name, grid=(n_blk,), in_specs=in_specs,
                          out_specs=pl.BlockSpec((tr, C), lambda i: (i, 0)),
                          out_shape=jax.ShapeDtypeStruct((R, C), F32), compiler_params=_params("parallel"))(*([gathered] * n_dev))


def _place():
    x, y, c = lax.axis_index("x"), lax.axis_index("y"), lax.axis_index("c")
    return x, y, c


def _allgather8(blk, name):
    m_per, n = blk.shape

    def body(x_ref, out_ref, send_sems, recv_sems, local_sem):
        x, y, c = _place()
        me, sibling = (x, y, c), (x, y, 1 - c)
        chips = [(1 - x, y), (x, 1 - y), (1 - x, 1 - y)]

        def rows(px, py, pc):
            return out_ref.at[pl.ds((4 * px + 2 * py + pc) * m_per, m_per), :]

        def copy(k, block, to, src=None):
            return pltpu.make_async_remote_copy(
                src_ref=rows(*block) if src is None else src, dst_ref=rows(*block),
                send_sem=send_sems.at[k], recv_sem=recv_sems.at[k], device_id=to, device_id_type=MESH)

        mine = pltpu.make_async_copy(x_ref, rows(*me), local_sem)
        mine.start()
        first = [copy(0, me, sibling, src=x_ref)]
        first += [copy(1 + j, me, (*chip, c), src=x_ref) for j, chip in enumerate(chips)]
        for cp in first:
            cp.start()
        passed = [copy(4 + j, (*chip, c), sibling) for j, chip in enumerate(chips)]
        for j, chip in enumerate(chips):
            copy(1 + j, (*chip, c), me).wait_recv()
            passed[j].start()
        copy(0, sibling, me).wait_recv()
        for j, chip in enumerate(chips):
            copy(4 + j, (*chip, 1 - c), me).wait_recv()
        for cp in first + passed:
            cp.wait_send()
        mine.wait()

    return pl.pallas_call(
        body, name=name,
        out_shape=jax.ShapeDtypeStruct((8 * m_per, n), blk.dtype),
        in_specs=[pl.BlockSpec(memory_space=pltpu.VMEM)],
        out_specs=pl.BlockSpec(memory_space=pltpu.VMEM),
        scratch_shapes=[pltpu.SemaphoreType.DMA((7,)), pltpu.SemaphoreType.DMA((7,)), pltpu.SemaphoreType.DMA],
        compiler_params=pltpu.CompilerParams(vmem_limit_bytes=V7X_VMEM_LIMIT),
    )(blk)


class _Sharded:
    def __init__(self, shard_shape, by_cols):
        r, c = shard_shape
        self.by_cols = by_cols
        self.full = (r, N_CHIPS * c) if by_cols else (N_CHIPS * r, c)
        self.shard = (r, c)
        self.half_rows = r // 2
        self.half = (r // 2, c)

    def shard_of(self, ref, k):
        r, c = self.shard
        return ref.at[:, pl.ds(k * c, c)] if self.by_cols else ref.at[pl.ds(k * r, r), :]

    def half_of(self, ref, k, hc):
        r, c = self.shard
        h = self.half_rows
        if self.by_cols:
            return ref.at[pl.ds(hc * h, h), pl.ds(k * c, c)]
        return ref.at[pl.ds(k * r + hc * h, h), :]

    def half_of_shard(self, ref, hc):
        return ref.at[pl.ds(hc * self.half_rows, self.half_rows), :]

    def part_of_halves(self, ref, k):
        r, c = self.shard
        h = self.half_rows
        return ref.at[:, pl.ds(k * c, c)] if self.by_cols else ref.at[pl.ds(k * h, h), :]


def _on_each_place(x, y, c, fn, by_chip=True, by_core=True):
    q = 2 * x + y
    for k in range(N_CHIPS if by_chip else 1):
        for cc in range(2 if by_core else 1):
            cond = None
            if by_chip:
                cond = q == k
            if by_core:
                cond = (c == cc) if cond is None else jnp.logical_and(cond, c == cc)
            pl.when(cond)(functools.partial(fn, k, cc))


def _chip_id(k, c):
    return (k // 2, k % 2, c)


def _gather_weights(shards, geoms, name):
    n_w = len(shards)

    def body(*refs):
        shard_refs, full_refs = refs[:n_w], refs[n_w:2 * n_w]
        send_sems, recv_sems, local_sems = refs[2 * n_w:]
        x, y, c = _place()

        def at_place(k, cc):
            def remote(slot, src, dst, to):
                return pltpu.make_async_remote_copy(src_ref=src, dst_ref=dst, send_sem=send_sems.at[slot],
                                                    recv_sem=recv_sems.at[slot], device_id=to, device_id_type=MESH)

            local, first, passed = [], [], []
            for i, (g, s_ref, f_ref) in enumerate(zip(geoms, shard_refs, full_refs)):
                cp = pltpu.make_async_copy(s_ref, g.shard_of(f_ref, k), local_sems.at[i])
                cp.start()
                local.append(cp)
                for j, flip in enumerate(FLIPS):
                    cp = remote(3 * i + j, g.half_of_shard(s_ref, cc), g.half_of(f_ref, k, cc), _chip_id(k ^ flip, cc))
                    cp.start()
                    first.append(cp)
            for i, (g, f_ref) in enumerate(zip(geoms, full_refs)):
                for j, flip in enumerate(FLIPS):
                    landed = g.half_of(f_ref, k ^ flip, cc)
                    remote(3 * i + j, landed, landed, _chip_id(k, cc)).wait_recv()
                    cp = remote(3 * n_w + 3 * i + j, landed, landed, _chip_id(k, 1 - cc))
                    cp.start()
                    passed.append(cp)
            for i, (g, f_ref) in enumerate(zip(geoms, full_refs)):
                for j, flip in enumerate(FLIPS):
                    from_sibling = g.half_of(f_ref, k ^ flip, 1 - cc)
                    remote(3 * n_w + 3 * i + j, from_sibling, from_sibling, _chip_id(k, cc)).wait_recv()
            for cp in first + passed:
                cp.wait_send()
            for cp in local:
                cp.wait()

        _on_each_place(x, y, c, at_place)

    return pl.pallas_call(
        body, name=name,
        out_shape=[jax.ShapeDtypeStruct(g.full, WIRE_DTYPE) for g in geoms],
        in_specs=[ANY] * n_w, out_specs=[ANY] * n_w,
        scratch_shapes=[pltpu.SemaphoreType.DMA((6 * n_w,)), pltpu.SemaphoreType.DMA((6 * n_w,)),
                        pltpu.SemaphoreType.DMA((n_w,))],
    )(*shards)


def _swap_core_halves(grads, geoms, name):
    n_w = len(grads)
    n_cp = sum(1 if g.by_cols else N_CHIPS for g in geoms)

    def body(*refs):
        g_refs, t_refs = refs[:n_w], refs[n_w:2 * n_w]
        send_sems, recv_sems = refs[2 * n_w:]
        x, y, c = _place()

        def at_place(_, cc):
            def pairs(hc):
                out = []
                for g, g_ref, t_ref in zip(geoms, g_refs, t_refs):
                    if g.by_cols:
                        out.append((g_ref.at[pl.ds(hc * g.half_rows, g.half_rows), :], t_ref))
                    else:
                        out += [(g.half_of(g_ref, k, hc), g.part_of_halves(t_ref, k)) for k in range(N_CHIPS)]
                return out

            sends = [pltpu.make_async_remote_copy(src_ref=src, dst_ref=dst, send_sem=send_sems.at[n],
                                                  recv_sem=recv_sems.at[n], device_id=(x, y, 1 - cc), device_id_type=MESH)
                     for n, (src, dst) in enumerate(pairs(1 - cc))]
            for cp in sends:
                cp.start()
            for n, (src, dst) in enumerate(pairs(cc)):
                pltpu.make_async_remote_copy(src_ref=src, dst_ref=dst, send_sem=send_sems.at[n], recv_sem=recv_sems.at[n],
                                             device_id=(x, y, cc), device_id_type=MESH).wait_recv()
            for cp in sends:
                cp.wait_send()

        _on_each_place(x, y, c, at_place, by_chip=False)

    return pl.pallas_call(
        body, name=name,
        out_shape=[jax.ShapeDtypeStruct((g.full[0] // 2, g.full[1]), F32) for g in geoms],
        in_specs=[ANY] * n_w, out_specs=[ANY] * n_w,
        scratch_shapes=[pltpu.SemaphoreType.DMA((n_cp,)), pltpu.SemaphoreType.DMA((n_cp,))],
    )(*grads)


def _scatter_chip_sums(sums, geoms, name):
    n_w = len(sums)

    def body(*refs):
        s_refs, r_refs = refs[:n_w], refs[n_w:2 * n_w]
        send_sems, recv_sems = refs[2 * n_w:]
        x, y, c = _place()

        def at_place(k, _):
            sends = []
            for i, (g, s_ref, r_ref) in enumerate(zip(geoms, s_refs, r_refs)):
                for j, flip in enumerate(FLIPS):
                    kk = k ^ flip
                    cp = pltpu.make_async_remote_copy(
                        src_ref=g.part_of_halves(s_ref, kk), dst_ref=r_ref.at[j], send_sem=send_sems.at[3 * i + j],
                        recv_sem=recv_sems.at[3 * i + j], device_id=(kk // 2, kk % 2, c), device_id_type=MESH)
                    cp.start()
                    sends.append(cp)
            for i, (g, s_ref, r_ref) in enumerate(zip(geoms, s_refs, r_refs)):
                for j in range(len(FLIPS)):
                    pltpu.make_async_remote_copy(
                        src_ref=g.part_of_halves(s_ref, k), dst_ref=r_ref.at[j], send_sem=send_sems.at[3 * i + j],
                        recv_sem=recv_sems.at[3 * i + j], device_id=(x, y, c), device_id_type=MESH).wait_recv()
            for cp in sends:
                cp.wait_send()

        _on_each_place(x, y, c, at_place, by_core=False)

    return pl.pallas_call(
        body, name=name,
        out_shape=[jax.ShapeDtypeStruct((len(FLIPS),) + g.half, WIRE_DTYPE) for g in geoms],
        in_specs=[ANY] * n_w, out_specs=[ANY] * n_w,
        scratch_shapes=[pltpu.SemaphoreType.DMA((3 * n_w,)), pltpu.SemaphoreType.DMA((3 * n_w,))],
    )(*sums)


def _share_reduced_halves(reduced, geoms, name):
    n_w = len(reduced)

    def body(*refs):
        out_refs = refs[n_w:2 * n_w]
        send_sems, recv_sems = refs[2 * n_w:]
        x, y, c = _place()

        def at_place(_, cc):
            sends = []
            for i, (g, ref) in enumerate(zip(geoms, out_refs)):
                mine = g.half_of_shard(ref, cc)
                cp = pltpu.make_async_remote_copy(src_ref=mine, dst_ref=mine, send_sem=send_sems.at[i],
                                                  recv_sem=recv_sems.at[i], device_id=(x, y, 1 - cc), device_id_type=MESH)
                cp.start()
                sends.append(cp)
            for i, (g, ref) in enumerate(zip(geoms, out_refs)):
                theirs = g.half_of_shard(ref, 1 - cc)
                pltpu.make_async_remote_copy(src_ref=theirs, dst_ref=theirs, send_sem=send_sems.at[i],
                                             recv_sem=recv_sems.at[i], device_id=(x, y, cc), device_id_type=MESH).wait_recv()
            for cp in sends:
                cp.wait_send()

        _on_each_place(x, y, c, at_place, by_chip=False)

    return pl.pallas_call(
        body, name=name,
        out_shape=[jax.ShapeDtypeStruct(g.shard, F32) for g in geoms],
        in_specs=[ANY] * n_w, out_specs=[ANY] * n_w,
        input_output_aliases={i: i for i in range(n_w)},
        scratch_shapes=[pltpu.SemaphoreType.DMA((n_w,)), pltpu.SemaphoreType.DMA((n_w,))],
    )(*reduced)


def _chip_sum(place, grad, theirs, g, name):
    RH, C = theirs.shape
    h = g.half_rows
    tr = _tile(h, 256, 16)
    tc = _tile(C, 2048)
    per_half = h // tr

    if g.by_cols:
        grad_map = lambda i, j, p: (p[1] * per_half + i, j)
    else:
        grad_map = lambda i, j, p: ((i // per_half) * 2 * per_half + p[1] * per_half + i % per_half, j)

    def body(p_ref, a_ref, b_ref, o_ref):
        o_ref[...] = (a_ref[...] + b_ref[...]).astype(o_ref.dtype)

    return pl.pallas_call(
        body, name=name,
        grid_spec=pltpu.PrefetchScalarGridSpec(
            num_scalar_prefetch=1, grid=(RH // tr, C // tc),
            in_specs=[pl.BlockSpec((tr, tc), grad_map), pl.BlockSpec((tr, tc), lambda i, j, p: (i, j))],
            out_specs=pl.BlockSpec((tr, tc), lambda i, j, p: (i, j))),
        out_shape=jax.ShapeDtypeStruct((RH, C), WIRE_DTYPE),
        compiler_params=_params("parallel", "parallel"),
    )(place, grad, theirs)


def _reduce_half(place, grad, theirs, others, g, name):
    h, wc = g.half
    tr = _tile(h, 256, 16)
    per_half = h // tr
    if g.by_cols:
        tc = wc
        grad_map = lambda i, p: (p[1] * per_half + i, p[0])
        theirs_map = lambda i, p: (i, p[0])
    else:
        tc = wc
        grad_map = lambda i, p: (p[0] * 2 * per_half + p[1] * per_half + i, 0)
        theirs_map = lambda i, p: (p[0] * per_half + i, 0)

    def body(p_ref, a_ref, b_ref, o0_ref, o1_ref, o2_ref, out_ref):
        acc = a_ref[...] + b_ref[...]
        for o_ref in (o0_ref, o1_ref, o2_ref):
            acc = acc + o_ref[...].astype(F32)
        out_ref[...] = acc

    other_specs = [pl.BlockSpec((None, tr, tc), functools.partial(lambda i, p, j: (j, i, 0), j=j)) for j in range(len(FLIPS))]
    return pl.pallas_call(
        body, name=name,
        grid_spec=pltpu.PrefetchScalarGridSpec(
            num_scalar_prefetch=1, grid=(per_half,),
            in_specs=[pl.BlockSpec((tr, tc), grad_map), pl.BlockSpec((tr, tc), theirs_map)] + other_specs,
            out_specs=pl.BlockSpec((tr, tc), lambda i, p: (p[1] * per_half + i, 0))),
        out_shape=jax.ShapeDtypeStruct(g.shard, F32),
        compiler_params=_params("arbitrary"),
    )(place, grad, theirs, others, others, others)


SMALL = ("b_ada", "norm1_g", "v_norm_g", "w_spatial", "b_spatial", "out_norm_g", "norm2_g", "final_g")
BIG = ("w_in", "w_out", "w_gate", "w_up", "w_down")
BY_COLS = {"w_in": True, "w_out": False, "w_gate": True, "w_up": True, "w_down": False}
ORDER = ("w_ada", "b_ada", "norm1_g", "w_in", "v_norm_g", "w_spatial", "b_spatial", "out_norm_g", "w_out",
         "norm2_g", "w_gate", "w_up", "w_down", "final_g")


def _pack(parts):
    return jnp.concatenate([parts[n].reshape(-1) for n in SMALL]).reshape(-1, LANE)


def _unpack(slab, shapes):
    flat = slab.reshape(-1)
    out, at = {}, 0
    for n in SMALL:
        size = math.prod(shapes[n])
        out[n] = flat[at:at + size].reshape(shapes[n])
        at += size
    return out


def kernel(x, c, w_ada, b_ada, norm1_g, w_in, v_norm_g, w_spatial, b_spatial, out_norm_g, w_out, norm2_g, w_gate, w_up, w_down, final_g, loss_target, m_w_ada, m_b_ada, m_norm1_g, m_w_in, m_v_norm_g, m_w_spatial, m_b_spatial, m_out_norm_g, m_w_out, m_norm2_g, m_w_gate, m_w_up, m_w_down, m_final_g, v_w_ada, v_b_ada, v_norm1_g, v_w_in, v_v_norm_g, v_w_spatial, v_b_spatial, v_out_norm_g, v_w_out, v_norm2_g, v_w_gate, v_w_up, v_w_down, v_final_g):
    weights = dict(w_ada=w_ada, b_ada=b_ada, norm1_g=norm1_g, w_in=w_in, v_norm_g=v_norm_g, w_spatial=w_spatial,
                   b_spatial=b_spatial, out_norm_g=out_norm_g, w_out=w_out, norm2_g=norm2_g, w_gate=w_gate, w_up=w_up,
                   w_down=w_down, final_g=final_g)
    m_in = dict(w_ada=m_w_ada, b_ada=m_b_ada, norm1_g=m_norm1_g, w_in=m_w_in, v_norm_g=m_v_norm_g, w_spatial=m_w_spatial,
                b_spatial=m_b_spatial, out_norm_g=m_out_norm_g, w_out=m_w_out, norm2_g=m_norm2_g, w_gate=m_w_gate,
                w_up=m_w_up, w_down=m_w_down, final_g=m_final_g)
    v_in = dict(w_ada=v_w_ada, b_ada=v_b_ada, norm1_g=v_norm1_g, w_in=v_w_in, v_norm_g=v_v_norm_g, w_spatial=v_w_spatial,
                b_spatial=v_b_spatial, out_norm_g=v_out_norm_g, w_out=v_w_out, norm2_g=v_norm2_g, w_gate=v_w_gate,
                w_up=v_w_up, w_down=v_w_down, final_g=v_final_g)

    S, D = x.shape[1], x.shape[2]
    n_g = v_norm_g.shape[-1] // LANE
    n_h = (D - n_g * LANE) // LANE
    GW = n_g * LANE
    xi, yi, ci = _place()
    chip = 2 * xi + yi
    me = 4 * xi + 2 * yi + ci
    place = jnp.stack([chip, ci]).astype(jnp.int32)

    xs, target = x[0], loss_target[0]
    geoms = [_Sharded(weights[n].shape[1:], BY_COLS[n]) for n in BIG]

    full = dict(zip(BIG, _gather_weights([_cast(weights[n][0], WIRE_DTYPE, "cast_" + n) for n in BIG], geoms, "gather_weights")))

    c_pad = jnp.concatenate([c, jnp.zeros((7, D), F32)], axis=0)
    c_all = _allgather8(c_pad, "gather_c")[::8]
    n_ada = w_ada.shape[2]
    b_cols = lax.dynamic_slice(b_ada, (0, chip * n_ada), (1, n_ada))
    mod_parts = _allgather8(_mod_part(c_all, w_ada[0], b_cols, "mod_part"), "gather_mod")
    mod_all = mod_parts.reshape(N_CHIPS, 2, 8, n_ada)[:, 0].transpose(1, 0, 2).reshape(8, N_CHIPS * n_ada)
    mod = lax.dynamic_slice(mod_all, (me, 0), (1, 6 * D))
    shift1, scale1, gate1, shift2, scale2, gate2 = [mod[:, i * D:(i + 1) * D] for i in range(6)]

    b_t = b_spatial[0].T
    h1 = _norm_mod(xs, norm1_g, scale1, shift1, "norm1")
    proj, = _mm("nn", h1, full["w_in"], [F32], "proj")
    on_gm = _gmlp_fwd(proj, v_norm_g, w_spatial[0], b_t, out_norm_g, n_g, "gmlp_fwd")
    o_sb, on_sb, l_sum = _sb_fwd(proj, out_norm_g, n_g, n_h, "sb_fwd")
    o_n = jnp.concatenate([on_gm, on_sb], axis=1)
    attn, = _mm("nn", o_n, full["w_out"], [F32], "attn_out")
    x1, h2 = _residual_norm_mod(xs, attn, gate1, norm2_g, scale2, shift2, "norm2")
    a_g, a_u, f_in = _gate_up(h2, full["w_gate"], full["w_up"], "gate_up")
    f, = _mm("nn", f_in, full["w_down"], [F32], "down", tm=512)
    dx2, df, d_gate2, d_final_g, loss_part = _final_loss_bwd(x1, f, gate2, final_g.reshape(1, D), target, "final")
    loss = lax.psum(loss_part[0, 0], ("x", "y", "c"))

    grads = {}
    d_ag, d_au = _mm("nt", df, full["w_down"], [MXU_DTYPE, MXU_DTYPE], "d_ffn_in", extras=(a_g, a_u),
                     epilogue=_swiglu_bwd_epilogue)
    grads["w_down"], = _mm("tn", f_in, df, [F32], "d_w_down", tm=512, tn=1024)
    dh2_g, = _mm("nt", d_ag, full["w_gate"], [F32], "d_h2_gate", tm=512)
    dh2, = _mm("nt", d_au, full["w_up"], [F32], "d_h2", tm=512, extras=(dh2_g,), epilogue=_add_epilogue)
    grads["w_gate"], = _mm("tn", h2, d_ag, [F32], "d_w_gate")
    grads["w_up"], = _mm("tn", h2, d_au, [F32], "d_w_up")
    dx1, d_shift2, d_scale2, d_norm2_g, d_gate1, d_attn = _norm_mod_bwd(dh2, x1, dx2, norm2_g, scale2, "norm2_bwd",
                                                                        branch=attn, gate=gate1)
    d_on, = _mm("nt", d_attn, full["w_out"], [F32], "d_o")
    grads["w_out"], = _mm("tn", o_n, d_attn, [F32], "d_w_out")
    dp_gm, d_w_spatial, d_b_t, d_v_norm_g, d_og_gm = _gmlp_bwd(proj, d_on, v_norm_g, w_spatial[0], b_t, out_norm_g, n_g, "gmlp_bwd")
    dq, dk, dv, d_og_sb = _sb_bwd(proj, o_sb, l_sum, d_on, out_norm_g, n_g, n_h, "sb_bwd")
    dproj = jnp.concatenate([dp_gm, dq, dk, dv], axis=1)
    dh1, = _mm("nt", dproj, full["w_in"], [F32], "d_h1", tm=512)
    grads["w_in"], = _mm("tn", h1, dproj, [F32], "d_w_in")
    grad_x, d_shift1, d_scale1, d_norm1_g = _norm_mod_bwd(dh1, xs, dx1, norm1_g, scale1, "norm1_bwd")

    dmod = jnp.concatenate([d_shift1, d_scale1, d_gate1, d_shift2, d_scale2, d_gate2], axis=1)
    small_parts = dict(b_ada=dmod, norm1_g=d_norm1_g, v_norm_g=d_v_norm_g, w_spatial=d_w_spatial, b_spatial=d_b_t.T,
                       out_norm_g=jnp.concatenate([d_og_gm, d_og_sb], axis=1), norm2_g=d_norm2_g, final_g=d_final_g)
    slab = _pack(small_parts)
    rows = slab.shape[0]
    gathered = _allgather8(slab, "gather_small")
    small_shapes = {n: weights[n].shape for n in SMALL}
    small_sum = _sum_devices(gathered, 8, "sum_small")
    dmod_all = gathered.reshape(8, rows * LANE)[:, :6 * D]
    dmod_cols = lax.dynamic_slice(dmod_all, (0, chip * n_ada), (8, n_ada))

    theirs = _swap_core_halves([grads[n] for n in BIG], geoms, "swap_core_halves")
    sums = [_chip_sum(place, grads[n], t, g, "chip_sum_" + n) for n, t, g in zip(BIG, theirs, geoms)]
    others = _scatter_chip_sums(sums, geoms, "scatter_chip_sums")
    halves = [_reduce_half(place, grads[n], t, o, g, "reduce_" + n) for n, t, o, g in zip(BIG, theirs, others, geoms)]
    reduced = dict(zip(BIG, _share_reduced_halves(halves, geoms, "share_halves")))

    grad_out, delta, new_m, new_v = {}, {}, {}, {}
    for n in BIG:
        grad_out[n] = reduced[n][None]
        d, mo, vo = _adamw(weights[n][0], reduced[n], m_in[n][0], v_in[n][0], "adamw_" + n)
        delta[n], new_m[n], new_v[n] = d[None], mo[None], vo[None]
    g_ada, d, mo, vo = _adamw_ada(c_all, dmod_cols, w_ada[0], m_w_ada[0], v_w_ada[0], "adamw_w_ada")
    grad_out["w_ada"], delta["w_ada"], new_m["w_ada"], new_v["w_ada"] = g_ada[None], d[None], mo[None], vo[None]
    d, mo, vo = _adamw(_pack({n: weights[n] for n in SMALL}), small_sum, _pack({n: m_in[n] for n in SMALL}),
                       _pack({n: v_in[n] for n in SMALL}), "adamw_small")
    for dst, slab_out in ((grad_out, small_sum), (delta, d), (new_m, mo), (new_v, vo)):
        dst.update(_unpack(slab_out, small_shapes))

    return (loss, grad_x[None], *[grad_out[n] for n in ORDER], *[delta[n] for n in ORDER],
            *[new_m[n] for n in ORDER], *[new_v[n] for n in ORDER])
```

```python
import functools
import math

import jax
import jax.numpy as jnp
from jax import lax
from jax.experimental import pallas as pl
from jax.experimental.pallas import tpu as pltpu
from jax.experimental.pallas import tpu_sc as plsc

F32 = jnp.float32
MXU_DTYPE = jnp.bfloat16
WIRE_DTYPE = jnp.bfloat16
EPS = 1e-6
LANE = 128
V7X_VMEM_LIMIT = 56 * 1024 * 1024
MESH = pl.DeviceIdType.MESH
N_CHIPS = 4
FLIPS = (2, 1, 3)
ANY = pl.BlockSpec(memory_space=pl.ANY)

ADAM_LR = 0.001
ADAM_B1 = 0.9
ADAM_B2 = 0.999
ADAM_EPS = 1e-08
ADAM_WD = 0.01
ADAM_STEP = 10


def _params(*semantics):
    return pltpu.CompilerParams(dimension_semantics=semantics or None, vmem_limit_bytes=V7X_VMEM_LIMIT)


def _tile(dim, pref, unit=LANE):
    best = None
    t = unit
    while t <= min(dim, pref):
        if dim % t == 0:
            best = t
        t += unit
    return best if best is not None else dim


def _sum0(v):
    return jnp.sum(v, axis=0, keepdims=True)


def _mean1(v):
    return jnp.mean(v, axis=-1, keepdims=True)


def _gelu(x):
    return 0.5 * x * (1.0 + lax.erf(x * (1.0 / math.sqrt(2.0))))


def _gelu_grad(x):
    cdf = 0.5 * (1.0 + lax.erf(x * (1.0 / math.sqrt(2.0))))
    return cdf + x * jnp.exp(-0.5 * x * x) * (1.0 / math.sqrt(2.0 * math.pi))


def _dot(a, b, dims):
    return lax.dot_general(a, b, (dims, ((), ())), preferred_element_type=F32)


NN = ((1,), (0,))
NT = ((1,), (1,))
TN = ((0,), (0,))


def _mm(kind, a, b, out_dtypes, name, tm=1024, tn=512, extras=(), epilogue=None):
    if kind == "nn":
        (M, K), N = a.shape, b.shape[1]
    elif kind == "nt":
        (M, K), N = a.shape, b.shape[0]
    else:
        (K, M), N = a.shape, b.shape[1]
    tm, tn = _tile(M, tm), _tile(N, tn)
    a_spec = pl.BlockSpec((K, tm), lambda i, j: (0, i)) if kind == "tn" else pl.BlockSpec((tm, K), lambda i, j: (i, 0))
    b_spec = pl.BlockSpec((tn, K), lambda i, j: (j, 0)) if kind == "nt" else pl.BlockSpec((K, tn), lambda i, j: (0, j))
    mn_spec = pl.BlockSpec((tm, tn), lambda i, j: (i, j))
    dims = {"nn": NN, "nt": NT, "tn": TN}[kind]
    n_extra = len(extras)

    def body(a_ref, b_ref, *rest):
        acc = _dot(a_ref[...], b_ref[...], dims)
        res = (acc,) if epilogue is None else epilogue(acc, *[e[...] for e in rest[:n_extra]])
        for o_ref, r in zip(rest[n_extra:], res):
            o_ref[...] = r.astype(o_ref.dtype)

    outs = pl.pallas_call(
        body, name=name, grid=(M // tm, N // tn),
        in_specs=[a_spec, b_spec] + [mn_spec] * n_extra,
        out_specs=[mn_spec] * len(out_dtypes),
        out_shape=[jax.ShapeDtypeStruct((M, N), d) for d in out_dtypes],
        compiler_params=_params("parallel", "arbitrary"),
    )(a, b, *extras)
    return outs


def _gate_up(h, wg, wu, name):
    (M, K), N = h.shape, wg.shape[1]
    tm, tn = _tile(M, 1024), _tile(N, 512)

    def body(h_ref, wg_ref, wu_ref, ag_ref, au_ref, f_ref):
        hv = h_ref[...]
        ag = _dot(hv, wg_ref[...], NN)
        au = _dot(hv, wu_ref[...], NN)
        ag_ref[...] = ag
        au_ref[...] = au
        f_ref[...] = (ag * jax.nn.sigmoid(ag) * au).astype(f_ref.dtype)

    w_spec = pl.BlockSpec((K, tn), lambda i, j: (0, j))
    mn_spec = pl.BlockSpec((tm, tn), lambda i, j: (i, j))
    return pl.pallas_call(
        body, name=name, grid=(M // tm, N // tn),
        in_specs=[pl.BlockSpec((tm, K), lambda i, j: (i, 0)), w_spec, w_spec],
        out_specs=[mn_spec] * 3,
        out_shape=[jax.ShapeDtypeStruct((M, N), F32), jax.ShapeDtypeStruct((M, N), F32),
                   jax.ShapeDtypeStruct((M, N), MXU_DTYPE)],
        compiler_params=_params("parallel", "arbitrary"),
    )(h, wg, wu)


def _swiglu_bwd_epilogue(dfin, ag, au):
    sg = jax.nn.sigmoid(ag)
    d_au = dfin * (ag * sg)
    d_ag = dfin * au * (sg * (1.0 + ag * (1.0 - sg)))
    return d_ag, d_au


def _add_epilogue(acc, other):
    return (acc + other,)


def _row_specs(ts, width):
    return pl.BlockSpec((ts, width), lambda i: (i, 0)), pl.BlockSpec((1, width), lambda i: (0, 0))


def _cast(a, dtype, name):
    R, C = a.shape
    tr = _tile(R, 512, 16)
    spec = pl.BlockSpec((tr, C), lambda i: (i, 0))

    def body(a_ref, o_ref):
        o_ref[...] = a_ref[...].astype(o_ref.dtype)

    return pl.pallas_call(body, name=name, grid=(R // tr,), in_specs=[spec], out_specs=spec,
                          out_shape=jax.ShapeDtypeStruct((R, C), dtype), compiler_params=_params("parallel"))(a)


def _norm_mod(x, g, scale, shift, name):
    S, D = x.shape
    ts = _tile(S, 256, 16)
    tile, vec = _row_specs(ts, D)

    def body(x_ref, g_ref, sc_ref, sh_ref, h_ref):
        xv = x_ref[...]
        r = lax.rsqrt(_mean1(xv * xv) + EPS)
        h_ref[...] = ((xv * r) * g_ref[...] * (1.0 + sc_ref[...]) + sh_ref[...]).astype(h_ref.dtype)

    return pl.pallas_call(body, name=name, grid=(S // ts,), in_specs=[tile, vec, vec, vec], out_specs=tile,
                          out_shape=jax.ShapeDtypeStruct((S, D), MXU_DTYPE), compiler_params=_params("parallel"))(x, g, scale, shift)


def _residual_norm_mod(x, attn, gate, g, scale, shift, name):
    S, D = x.shape
    ts = _tile(S, 256, 16)
    tile, vec = _row_specs(ts, D)

    def body(x_ref, a_ref, gate_ref, g_ref, sc_ref, sh_ref, x1_ref, h_ref):
        x1 = x_ref[...] + gate_ref[...] * a_ref[...]
        x1_ref[...] = x1
        r = lax.rsqrt(_mean1(x1 * x1) + EPS)
        h_ref[...] = ((x1 * r) * g_ref[...] * (1.0 + sc_ref[...]) + sh_ref[...]).astype(h_ref.dtype)

    return pl.pallas_call(body, name=name, grid=(S // ts,), in_specs=[tile, tile, vec, vec, vec, vec],
                          out_specs=[tile, tile],
                          out_shape=[jax.ShapeDtypeStruct((S, D), F32), jax.ShapeDtypeStruct((S, D), MXU_DTYPE)],
                          compiler_params=_params("parallel"))(x, attn, gate, g, scale, shift)


def _final_loss_bwd(x1, f, gate2, final_g, target, name):
    S, D = x1.shape
    ts = _tile(S, 256, 16)
    tile, vec = _row_specs(ts, D)
    loss_spec = pl.BlockSpec((1, LANE), lambda i: (0, 0))

    def body(x1_ref, f_ref, gate_ref, g_ref, t_ref, dx2_ref, df_ref, dgate_ref, dg_ref, loss_ref):
        @pl.when(pl.program_id(0) == 0)
        def _():
            dgate_ref[...] = jnp.zeros_like(dgate_ref)
            dg_ref[...] = jnp.zeros_like(dg_ref)
            loss_ref[...] = jnp.zeros_like(loss_ref)

        fv, gate, g = f_ref[...], gate_ref[...], g_ref[...]
        x2 = x1_ref[...] + gate * fv
        r = lax.rsqrt(_mean1(x2 * x2) + EPS)
        xn = x2 * r
        err = xn * g - t_ref[...]
        loss_ref[...] += jnp.broadcast_to(0.5 * _sum0(_mean1(err * err)), loss_ref.shape)
        dy = err * (1.0 / D)
        dg_ref[...] += _sum0(dy * xn)
        dxn = dy * g
        dx2 = r * (dxn - xn * _mean1(dxn * xn))
        dx2_ref[...] = dx2
        dgate_ref[...] += _sum0(dx2 * fv)
        df_ref[...] = (dx2 * gate).astype(df_ref.dtype)

    return pl.pallas_call(
        body, name=name, grid=(S // ts,), in_specs=[tile, tile, vec, vec, tile],
        out_specs=[tile, tile, vec, vec, loss_spec],
        out_shape=[jax.ShapeDtypeStruct((S, D), F32), jax.ShapeDtypeStruct((S, D), MXU_DTYPE),
                   jax.ShapeDtypeStruct((1, D), F32), jax.ShapeDtypeStruct((1, D), F32),
                   jax.ShapeDtypeStruct((1, LANE), F32)],
        compiler_params=_params("arbitrary"),
    )(x1, f, gate2, final_g, target)


def _norm_mod_bwd(dh, xin, dres, g, scale, name, branch=None, gate=None):
    S, D = xin.shape
    ts = _tile(S, 256, 16)
    tile, vec = _row_specs(ts, D)
    with_gate = branch is not None

    def body(*refs):
        if with_gate:
            dh_ref, x_ref, dres_ref, g_ref, sc_ref, br_ref, gate_ref, dx_ref, dshift_ref, dscale_ref, dg_ref, dgate_ref, dbr_ref = refs
            accs = (dshift_ref, dscale_ref, dg_ref, dgate_ref)
        else:
            dh_ref, x_ref, dres_ref, g_ref, sc_ref, dx_ref, dshift_ref, dscale_ref, dg_ref = refs
            accs = (dshift_ref, dscale_ref, dg_ref)

        @pl.when(pl.program_id(0) == 0)
        def _():
            for acc in accs:
                acc[...] = jnp.zeros_like(acc)

        dh_v, xv, g_v = dh_ref[...], x_ref[...], g_ref[...]
        one_sc = 1.0 + sc_ref[...]
        r = lax.rsqrt(_mean1(xv * xv) + EPS)
        xn = xv * r
        dshift_ref[...] += _sum0(dh_v)
        dscale_ref[...] += _sum0(dh_v * (xn * g_v))
        dg_ref[...] += _sum0(dh_v * one_sc * xn)
        dxn = dh_v * (g_v * one_sc)
        dx = dres_ref[...] + r * (dxn - xn * _mean1(dxn * xn))
        dx_ref[...] = dx
        if with_gate:
            dgate_ref[...] += _sum0(dx * br_ref[...])
            dbr_ref[...] = (dx * gate_ref[...]).astype(dbr_ref.dtype)

    ins = [dh, xin, dres, g, scale] + ([branch, gate] if with_gate else [])
    in_specs = [tile, tile, tile, vec, vec] + ([tile, vec] if with_gate else [])
    out_specs = [tile, vec, vec, vec] + ([vec, tile] if with_gate else [])
    out_shape = [jax.ShapeDtypeStruct((S, D), F32)] + [jax.ShapeDtypeStruct((1, D), F32)] * 3
    if with_gate:
        out_shape += [jax.ShapeDtypeStruct((1, D), F32), jax.ShapeDtypeStruct((S, D), MXU_DTYPE)]
    return pl.pallas_call(body, name=name, grid=(S // ts,), in_specs=in_specs, out_specs=out_specs,
                          out_shape=out_shape, compiler_params=_params("arbitrary"))(*ins)


def _causal_weights(ws_ref, wt_ref, n_g):
    row = lax.broadcasted_iota(jnp.int32, (LANE, LANE), 0)
    col = lax.broadcasted_iota(jnp.int32, (LANE, LANE), 1)
    for g in range(n_g):
        wt_ref[g] = jnp.where(col <= row, ws_ref[g], 0.0).astype(wt_ref.dtype)


def _group_layernorm(v):
    xc = v - _mean1(v)
    rstd = lax.rsqrt(_mean1(xc * xc) + EPS)
    return xc * rstd, rstd


def _gmlp_fwd(proj, v_gain, w_s, b_t, out_gain, n_g, name):
    S = proj.shape[0]
    GW = n_g * LANE

    def body(p_ref, vg_ref, ws_ref, bt_ref, og_ref, on_ref, wt_ref):
        @pl.when(pl.program_id(0) == 0)
        def _():
            _causal_weights(ws_ref, wt_ref, n_g)

        for g in range(n_g):
            cols = slice(g * LANE, (g + 1) * LANE)
            u = _gelu(p_ref[:, cols])
            v = _gelu(p_ref[:, GW + g * LANE:GW + (g + 1) * LANE])
            vhat, _ = _group_layernorm(v)
            vln = (vhat * vg_ref[:, cols]).astype(MXU_DTYPE)
            mixed = _dot(wt_ref[g], vln, NN) + bt_ref[:, g:g + 1]
            o = u * mixed
            r = lax.rsqrt(_mean1(o * o) + EPS)
            on_ref[:, cols] = (o * r * og_ref[:, cols]).astype(on_ref.dtype)

    return pl.pallas_call(
        body, name=name, grid=(S // LANE,),
        in_specs=[pl.BlockSpec((LANE, 2 * GW), lambda n: (n, 0)),
                  pl.BlockSpec((1, GW), lambda n: (0, 0)),
                  pl.BlockSpec((n_g, LANE, LANE), lambda n: (0, 0, 0)),
                  pl.BlockSpec((LANE, n_g), lambda n: (0, 0)),
                  pl.BlockSpec((1, GW), lambda n: (0, 0))],
        out_specs=pl.BlockSpec((LANE, GW), lambda n: (n, 0)),
        out_shape=jax.ShapeDtypeStruct((S, GW), MXU_DTYPE),
        scratch_shapes=[pltpu.VMEM((n_g, LANE, LANE), MXU_DTYPE)],
        compiler_params=_params("arbitrary"),
    )(proj, v_gain, w_s, b_t, out_gain)


def _gmlp_bwd(proj, d_on, v_gain, w_s, b_t, out_gain, n_g, name):
    S = proj.shape[0]
    GW = n_g * LANE

    def body(p_ref, dn_ref, vg_ref, ws_ref, bt_ref, og_ref, dp_ref, dws_ref, dbt_ref, dvg_ref, dog_ref, wt_ref):
        @pl.when(pl.program_id(0) == 0)
        def _():
            _causal_weights(ws_ref, wt_ref, n_g)
            dws_ref[...] = jnp.zeros_like(dws_ref)
            dbt_ref[...] = jnp.zeros_like(dbt_ref)
            dvg_ref[...] = jnp.zeros_like(dvg_ref)
            dog_ref[...] = jnp.zeros_like(dog_ref)

        row = lax.broadcasted_iota(jnp.int32, (LANE, LANE), 0)
        col = lax.broadcasted_iota(jnp.int32, (LANE, LANE), 1)
        for g in range(n_g):
            cols = slice(g * LANE, (g + 1) * LANE)
            vcols = slice(GW + g * LANE, GW + (g + 1) * LANE)
            pu, pv = p_ref[:, cols], p_ref[:, vcols]
            u, v = _gelu(pu), _gelu(pv)
            vhat, rstd = _group_layernorm(v)
            gain = vg_ref[:, cols]
            vln = (vhat * gain).astype(MXU_DTYPE)
            mixed = _dot(wt_ref[g], vln, NN) + bt_ref[:, g:g + 1]
            o = u * mixed
            r = lax.rsqrt(_mean1(o * o) + EPS)
            oh = o * r
            dn = dn_ref[:, cols]
            dog_ref[:, cols] += _sum0(dn * oh)
            dhn = dn * og_ref[:, cols]
            d_o = r * (dhn - oh * _mean1(dhn * oh))
            du = d_o * mixed
            dmix = d_o * u
            dbt_ref[:, g:g + 1] += jnp.sum(dmix, axis=1, keepdims=True)
            dmix_b = dmix.astype(MXU_DTYPE)
            dws_ref[g] += jnp.where(col <= row, _dot(dmix_b, vln, NT), 0.0)
            dvln = _dot(wt_ref[g], dmix_b, TN)
            dvg_ref[:, cols] += _sum0(dvln * vhat)
            dxh = dvln * gain
            dv = rstd * (dxh - _mean1(dxh) - vhat * _mean1(dxh * vhat))
            dp_ref[:, cols] = (du * _gelu_grad(pu)).astype(dp_ref.dtype)
            dp_ref[:, vcols] = (dv * _gelu_grad(pv)).astype(dp_ref.dtype)

    return pl.pallas_call(
        body, name=name, grid=(S // LANE,),
        in_specs=[pl.BlockSpec((LANE, 2 * GW), lambda n: (n, 0)),
                  pl.BlockSpec((LANE, GW), lambda n: (n, 0)),
                  pl.BlockSpec((1, GW), lambda n: (0, 0)),
                  pl.BlockSpec((n_g, LANE, LANE), lambda n: (0, 0, 0)),
                  pl.BlockSpec((LANE, n_g), lambda n: (0, 0)),
                  pl.BlockSpec((1, GW), lambda n: (0, 0))],
        out_specs=[pl.BlockSpec((LANE, 2 * GW), lambda n: (n, 0)),
                   pl.BlockSpec((n_g, LANE, LANE), lambda n: (0, 0, 0)),
                   pl.BlockSpec((LANE, n_g), lambda n: (0, 0)),
                   pl.BlockSpec((1, GW), lambda n: (0, 0)),
                   pl.BlockSpec((1, GW), lambda n: (0, 0))],
        out_shape=[jax.ShapeDtypeStruct((S, 2 * GW), MXU_DTYPE),
                   jax.ShapeDtypeStruct((n_g, LANE, LANE), F32),
                   jax.ShapeDtypeStruct((LANE, n_g), F32),
                   jax.ShapeDtypeStruct((1, GW), F32),
                   jax.ShapeDtypeStruct((1, GW), F32)],
        scratch_shapes=[pltpu.VMEM((n_g, LANE, LANE), MXU_DTYPE)],
        compiler_params=_params("arbitrary"),
    )(proj, d_on, v_gain, w_s, b_t, out_gain)


def _tri_sum(v, tri):
    hi = v.astype(MXU_DTYPE)
    lo = (v - hi.astype(F32)).astype(MXU_DTYPE)
    return _dot(hi, tri, NN) + _dot(lo, tri, NN)


def _log_sigmoids(z):
    sp = jnp.log1p(jnp.exp(-jnp.abs(z)))
    return jnp.minimum(z, 0.0) - sp, jnp.minimum(-z, 0.0) - sp


def _rows(i, size):
    return pl.ds(pl.multiple_of(i * size, size), size)


SB_QUERY_TILE = 512
SB_KEY_TILE = 256


def _sb_tiles(S):
    tq = _tile(S, SB_QUERY_TILE)
    tk = _tile(tq, SB_KEY_TILE)
    return tq, tk, S // tq, tq // tk


def _triangle(n, keep):
    row = lax.broadcasted_iota(jnp.int32, (n, n), 0)
    col = lax.broadcasted_iota(jnp.int32, (n, n), 1)
    return jnp.where(keep(row, col), 1.0, 0.0).astype(MXU_DTYPE)


def _strictly_before(tq, tk, key_offset):
    row = lax.broadcasted_iota(jnp.int32, (tq, tk), 0)
    col = lax.broadcasted_iota(jnp.int32, (tq, tk), 1)
    return col + key_offset < row


def _sb_specs(S, n_g, n_h):
    base = 2 * n_g
    q_spec = pl.BlockSpec((S, LANE), lambda h: (0, base + h))
    k_spec = pl.BlockSpec((S, LANE), lambda h: (0, base + n_h + h))
    v_spec = pl.BlockSpec((S, LANE), lambda h: (0, base + 2 * n_h + h))
    gain_spec = pl.BlockSpec((1, LANE), lambda h: (0, n_g + h))
    head_spec = pl.BlockSpec((S, LANE), lambda h: (0, h))
    return q_spec, k_spec, v_spec, gain_spec, head_spec


def _sb_fwd(proj, out_gain, n_g, n_h, name):
    S = proj.shape[0]
    TQ, TK, NQ, KPQ = _sb_tiles(S)
    scale = LANE ** -0.5
    q_spec, k_spec, v_spec, gain_spec, head_spec = _sb_specs(S, n_g, n_h)

    def body(q_ref, k_ref, v_ref, og_ref, o_ref, on_ref, ls_ref, qb, kb, vb):
        qb[...] = q_ref[...].astype(MXU_DTYPE)
        kb[...] = k_ref[...].astype(MXU_DTYPE)
        vb[...] = v_ref[...].astype(MXU_DTYPE)
        after = _triangle(TK, lambda r, c: r > c)

        def block(qi, j, ctail, acc, key_offset):
            z = _dot(qi, kb[_rows(j, TK), :], NT) * scale
            lb, l1m = _log_sigmoids(z)
            if key_offset is not None:
                strict = _strictly_before(TQ, TK, key_offset)
                l1m = jnp.where(strict, l1m, 0.0)
            a = jnp.exp(lb + ctail + _tri_sum(l1m, after))
            if key_offset is not None:
                a = jnp.where(strict, a, 0.0)
            acc = acc + _dot(a.astype(MXU_DTYPE), vb[_rows(j, TK), :], NN)
            return ctail + jnp.sum(l1m, axis=1, keepdims=True), acc

        def q_loop(i, carry):
            qi = qb[_rows(i, TQ), :]
            state = (jnp.zeros((TQ, 1), F32), jnp.zeros((TQ, LANE), F32))
            for d in reversed(range(KPQ)):
                state = block(qi, i * KPQ + d, state[0], state[1], d * TK)
            ctail, acc = lax.fori_loop(
                0, i * KPQ, lambda jj, st: block(qi, i * KPQ - 1 - jj, st[0], st[1], None), state)
            ls_ref[_rows(i, TQ), :] = jnp.broadcast_to(ctail, (TQ, LANE))
            o_ref[_rows(i, TQ), :] = acc
            r = lax.rsqrt(_mean1(acc * acc) + EPS)
            on_ref[_rows(i, TQ), :] = (acc * r * og_ref[...]).astype(on_ref.dtype)
            return carry

        lax.fori_loop(0, NQ, q_loop, 0)

    return pl.pallas_call(
        body, name=name, grid=(n_h,),
        in_specs=[q_spec, k_spec, v_spec, gain_spec],
        out_specs=[head_spec, head_spec, head_spec],
        out_shape=[jax.ShapeDtypeStruct((S, n_h * LANE), F32), jax.ShapeDtypeStruct((S, n_h * LANE), MXU_DTYPE),
                   jax.ShapeDtypeStruct((S, n_h * LANE), F32)],
        scratch_shapes=[pltpu.VMEM((S, LANE), MXU_DTYPE)] * 3,
        compiler_params=_params("parallel"),
    )(proj, proj, proj, out_gain)


def _sb_bwd(proj, o_sb, l_sum, d_on, out_gain, n_g, n_h, name):
    S = proj.shape[0]
    TQ, TK, NQ, KPQ = _sb_tiles(S)
    scale = LANE ** -0.5
    q_spec, k_spec, v_spec, gain_spec, head_spec = _sb_specs(S, n_g, n_h)
    dn_spec = pl.BlockSpec((S, LANE), lambda h: (0, n_g + h))
    dgain_spec = pl.BlockSpec((1, LANE), lambda h: (0, h))

    def body(q_ref, k_ref, v_ref, o_ref, ls_ref, dn_ref, og_ref, dq_ref, dk_ref, dv_ref, dog_ref,
             qb, kb, vb, dob, dk_acc, dv_acc):
        qb[...] = q_ref[...].astype(MXU_DTYPE)
        kb[...] = k_ref[...].astype(MXU_DTYPE)
        vb[...] = v_ref[...].astype(MXU_DTYPE)
        o, dn = o_ref[...], dn_ref[...]
        r = lax.rsqrt(_mean1(o * o) + EPS)
        oh = o * r
        dog_ref[...] = _sum0(dn * oh)
        dhn = dn * og_ref[...]
        dob[...] = (r * (dhn - oh * _mean1(dhn * oh))).astype(MXU_DTYPE)
        dk_acc[...] = jnp.zeros_like(dk_acc)
        dv_acc[...] = jnp.zeros_like(dv_acc)

        up_to = _triangle(TK, lambda r, c: r <= c)
        before = _triangle(TK, lambda r, c: r < c)

        def block(qi, doi, ltot, j, cl, cdl, dq, key_offset):
            kj, vj = kb[_rows(j, TK), :], vb[_rows(j, TK), :]
            z = _dot(qi, kj, NT) * scale
            lb, l1m_all = _log_sigmoids(z)
            l1m = l1m_all
            if key_offset is not None:
                strict = _strictly_before(TQ, TK, key_offset)
                l1m = jnp.where(strict, l1m_all, 0.0)
            a = jnp.exp(lb + (ltot - (cl + _tri_sum(l1m, up_to))))
            if key_offset is not None:
                a = jnp.where(strict, a, 0.0)
            dl = _dot(doi, vj, NT) * a
            d_l1m = cdl + _tri_sum(dl, before)
            dz = dl * jnp.exp(l1m_all) - jnp.exp(lb) * d_l1m
            if key_offset is not None:
                dz = jnp.where(strict, dz, 0.0)
            dzs = (dz * scale).astype(MXU_DTYPE)
            dq = dq + _dot(dzs, kj, NN)
            dk_acc[_rows(j, TK), :] += _dot(dzs, qi, TN)
            dv_acc[_rows(j, TK), :] += _dot(a.astype(MXU_DTYPE), doi, TN)
            return (cl + jnp.sum(l1m, axis=1, keepdims=True), cdl + jnp.sum(dl, axis=1, keepdims=True), dq)

        def q_loop(i, carry):
            qi, doi = qb[_rows(i, TQ), :], dob[_rows(i, TQ), :]
            ltot = ls_ref[_rows(i, TQ), :][:, :1]
            zero_col = jnp.zeros((TQ, 1), F32)
            state = lax.fori_loop(
                0, i * KPQ, lambda j, st: block(qi, doi, ltot, j, st[0], st[1], st[2], None),
                (zero_col, zero_col, jnp.zeros((TQ, LANE), F32)))
            for d in range(KPQ):
                state = block(qi, doi, ltot, i * KPQ + d, state[0], state[1], state[2], d * TK)
            dq_ref[_rows(i, TQ), :] = state[2].astype(dq_ref.dtype)
            return carry

        lax.fori_loop(0, NQ, q_loop, 0)
        dk_ref[...] = dk_acc[...].astype(dk_ref.dtype)
        dv_ref[...] = dv_acc[...].astype(dv_ref.dtype)

    W = n_h * LANE
    return pl.pallas_call(
        body, name=name, grid=(n_h,),
        in_specs=[q_spec, k_spec, v_spec, head_spec, head_spec, dn_spec, gain_spec],
        out_specs=[head_spec, head_spec, head_spec, dgain_spec],
        out_shape=[jax.ShapeDtypeStruct((S, W), MXU_DTYPE)] * 3 + [jax.ShapeDtypeStruct((1, W), F32)],
        scratch_shapes=[pltpu.VMEM((S, LANE), MXU_DTYPE)] * 4 + [pltpu.VMEM((S, LANE), F32)] * 2,
        compiler_params=_params("parallel"),
    )(proj, proj, proj, o_sb, l_sum, d_on, out_gain)


def _mod_part(c_all, w_ada, b_ada_cols, name):
    B, K = c_all.shape
    N = w_ada.shape[1]
    tn = _tile(N, 512)

    def body(c_ref, w_ref, b_ref, o_ref):
        cv = c_ref[...]
        ca = (cv * jax.nn.sigmoid(cv)).astype(MXU_DTYPE)
        o_ref[...] = _dot(ca, w_ref[...].astype(MXU_DTYPE), NN) + b_ref[...]

    return pl.pallas_call(
        body, name=name, grid=(N // tn,),
        in_specs=[pl.BlockSpec((B, K), lambda j: (0, 0)), pl.BlockSpec((K, tn), lambda j: (0, j)),
                  pl.BlockSpec((1, tn), lambda j: (0, j))],
        out_specs=pl.BlockSpec((B, tn), lambda j: (0, j)),
        out_shape=jax.ShapeDtypeStruct((B, N), F32), compiler_params=_params("parallel"))(c_all, w_ada, b_ada_cols)


def _adamw_math(w, g, m, v):
    m = ADAM_B1 * m + (1.0 - ADAM_B1) * g
    v = ADAM_B2 * v + (1.0 - ADAM_B2) * (g * g)
    m_hat = m / (1.0 - ADAM_B1 ** ADAM_STEP)
    v_hat = v / (1.0 - ADAM_B2 ** ADAM_STEP)
    delta = -ADAM_LR * (m_hat / (jnp.sqrt(v_hat) + ADAM_EPS) + ADAM_WD * w)
    return delta, m, v


def _adamw(w, g, m, v, name):
    R, C = w.shape
    tr = _tile(R, max(8, (1 << 19) // C), 8)
    spec = pl.BlockSpec((tr, C), lambda i: (i, 0))

    def body(w_ref, g_ref, m_ref, v_ref, d_ref, mo_ref, vo_ref):
        d_ref[...], mo_ref[...], vo_ref[...] = _adamw_math(w_ref[...], g_ref[...], m_ref[...], v_ref[...])

    return pl.pallas_call(body, name=name, grid=(R // tr,), in_specs=[spec] * 4, out_specs=[spec] * 3,
                          out_shape=[jax.ShapeDtypeStruct((R, C), F32)] * 3, compiler_params=_params("parallel"))(w, g, m, v)


def _adamw_ada(c_all, dmod_cols, w, m, v, name):
    K, N = w.shape
    B = c_all.shape[0]
    tk, tn = _tile(K, 512), _tile(N, 1024)
    spec = pl.BlockSpec((tk, tn), lambda i, j: (i, j))

    def body(c_ref, dm_ref, w_ref, m_ref, v_ref, g_ref, d_ref, mo_ref, vo_ref):
        cv = c_ref[...]
        ca = (cv * jax.nn.sigmoid(cv)).astype(MXU_DTYPE)
        g = _dot(ca, dm_ref[...].astype(MXU_DTYPE), TN)
        g_ref[...] = g
        d_ref[...], mo_ref[...], vo_ref[...] = _adamw_math(w_ref[...], g, m_ref[...], v_ref[...])

    return pl.pallas_call(
        body, name=name, grid=(K // tk, N // tn),
        in_specs=[pl.BlockSpec((B, tk), lambda i, j: (0, i)), pl.BlockSpec((B, tn), lambda i, j: (0, j)), spec, spec, spec],
        out_specs=[spec] * 4, out_shape=[jax.ShapeDtypeStruct((K, N), F32)] * 4,
        compiler_params=_params("parallel", "parallel"))(c_all, dmod_cols, w, m, v)


def _sum_devices(gathered, n_dev, name):
    R = gathered.shape[0] // n_dev
    C = gathered.shape[1]
    tr = _tile(R, 512, 8)
    n_blk = R // tr

    def body(*refs):
        acc = refs[0][...]
        for r in refs[1:n_dev]:
            acc = acc + r[...]
        refs[n_dev][...] = acc

    in_specs = [pl.BlockSpec((tr, C), functools.partial(lambda i, d: (d * n_blk + i, 0), d=d)) for d in range(n_dev)]
    return pl.pallas_call(body, name=name, grid=(n_blk,), in_specs=in_specs,
                          out_specs=pl.BlockSpec((tr, C), lambda i: (i, 0)),
                          out_shape=jax.ShapeDtypeStruct((R, C), F32), compiler_params=_params("parallel"))(*([gathered] * n_dev))


def _place():
    x, y, c = lax.axis_index("x"), lax.axis_index("y"), lax.axis_index("c")
    return x, y, c


def _allgather8(blk, name):
    m_per, n = blk.shape

    def body(x_ref, out_ref, send_sems, recv_sems, local_sem):
        x, y, c = _place()
        me, sibling = (x, y, c), (x, y, 1 - c)
        chips = [(1 - x, y), (x, 1 - y), (1 - x, 1 - y)]

        def rows(px, py, pc):
            return out_ref.at[pl.ds((4 * px + 2 * py + pc) * m_per, m_per), :]

        def copy(k, block, to, src=None):
            return pltpu.make_async_remote_copy(
                src_ref=rows(*block) if src is None else src, dst_ref=rows(*block),
                send_sem=send_sems.at[k], recv_sem=recv_sems.at[k], device_id=to, device_id_type=MESH)

        mine = pltpu.make_async_copy(x_ref, rows(*me), local_sem)
        mine.start()
        first = [copy(0, me, sibling, src=x_ref)]
        first += [copy(1 + j, me, (*chip, c), src=x_ref) for j, chip in enumerate(chips)]
        for cp in first:
            cp.start()
        passed = [copy(4 + j, (*chip, c), sibling) for j, chip in enumerate(chips)]
        for j, chip in enumerate(chips):
            copy(1 + j, (*chip, c), me).wait_recv()
            passed[j].start()
        copy(0, sibling, me).wait_recv()
        for j, chip in enumerate(chips):
            copy(4 + j, (*chip, 1 - c), me).wait_recv()
        for cp in first + passed:
            cp.wait_send()
        mine.wait()

    return pl.pallas_call(
        body, name=name,
        out_shape=jax.ShapeDtypeStruct((8 * m_per, n), blk.dtype),
        in_specs=[pl.BlockSpec(memory_space=pltpu.VMEM)],
        out_specs=pl.BlockSpec(memory_space=pltpu.VMEM),
        scratch_shapes=[pltpu.SemaphoreType.DMA((7,)), pltpu.SemaphoreType.DMA((7,)), pltpu.SemaphoreType.DMA],
        compiler_params=pltpu.CompilerParams(vmem_limit_bytes=V7X_VMEM_LIMIT),
    )(blk)


class _Sharded:
    def __init__(self, shard_shape, by_cols):
        r, c = shard_shape
        self.by_cols = by_cols
        self.full = (r, N_CHIPS * c) if by_cols else (N_CHIPS * r, c)
        self.shard = (r, c)
        self.half_rows = r // 2
        self.half = (r // 2, c)

    def shard_of(self, ref, k):
        r, c = self.shard
        return ref.at[:, pl.ds(k * c, c)] if self.by_cols else ref.at[pl.ds(k * r, r), :]

    def half_of(self, ref, k, hc):
        r, c = self.shard
        h = self.half_rows
        if self.by_cols:
            return ref.at[pl.ds(hc * h, h), pl.ds(k * c, c)]
        return ref.at[pl.ds(k * r + hc * h, h), :]

    def half_of_shard(self, ref, hc):
        return ref.at[pl.ds(hc * self.half_rows, self.half_rows), :]

    def part_of_halves(self, ref, k):
        r, c = self.shard
        h = self.half_rows
        return ref.at[:, pl.ds(k * c, c)] if self.by_cols else ref.at[pl.ds(k * h, h), :]


def _on_each_place(x, y, c, fn, by_chip=True, by_core=True):
    q = 2 * x + y
    for k in range(N_CHIPS if by_chip else 1):
        for cc in range(2 if by_core else 1):
            cond = None
            if by_chip:
                cond = q == k
            if by_core:
                cond = (c == cc) if cond is None else jnp.logical_and(cond, c == cc)
            pl.when(cond)(functools.partial(fn, k, cc))


def _chip_id(k, c):
    return (k // 2, k % 2, c)


def _handshake(peers):
    barrier = pltpu.get_barrier_semaphore()
    for peer in peers:
        pl.semaphore_signal(barrier, inc=1, device_id=peer, device_id_type=MESH)
    pl.semaphore_wait(barrier, len(peers))


def _gather_weight(shard, g, name, collective_id):
    s_ref = jax.new_ref(shard, memory_space=pltpu.MemorySpace.HBM)
    f_ref = jax.empty_ref(jax.ShapeDtypeStruct(g.full, WIRE_DTYPE), memory_space=pltpu.MemorySpace.HBM)
    n_flips = len(FLIPS)

    @pl.kernel(mesh=plsc.ScalarSubcoreMesh(axis_name="sequencer", num_cores=1), name=name,
               scratch_types=(pltpu.SemaphoreType.DMA((2 * n_flips,)), pltpu.SemaphoreType.DMA((2 * n_flips,)),
                              pltpu.SemaphoreType.DMA),
               compiler_params=pltpu.CompilerParams(collective_id=collective_id))
    def launch(send_sems, recv_sems, local_sem):
        x, y, c = _place()
        _handshake([(x, y, 1 - c), (1 - x, y, c), (x, 1 - y, c), (1 - x, 1 - y, c)])

        def at_place(k, cc):
            def remote(slot, src, dst, to):
                return pltpu.make_async_remote_copy(src_ref=src, dst_ref=dst, send_sem=send_sems.at[slot],
                                                    recv_sem=recv_sems.at[slot], device_id=to, device_id_type=MESH)

            local = pltpu.make_async_copy(s_ref, g.shard_of(f_ref, k), local_sem)
            local.start()
            first, passed = [], []
            for j, flip in enumerate(FLIPS):
                cp = remote(j, g.half_of_shard(s_ref, cc), g.half_of(f_ref, k, cc), _chip_id(k ^ flip, cc))
                cp.start()
                first.append(cp)
            for j, flip in enumerate(FLIPS):
                landed = g.half_of(f_ref, k ^ flip, cc)
                remote(j, landed, landed, _chip_id(k, cc)).wait_recv()
                cp = remote(n_flips + j, landed, landed, _chip_id(k, 1 - cc))
                cp.start()
                passed.append(cp)
            for j, flip in enumerate(FLIPS):
                from_sibling = g.half_of(f_ref, k ^ flip, 1 - cc)
                remote(n_flips + j, from_sibling, from_sibling, _chip_id(k, cc)).wait_recv()
            for cp in first + passed:
                cp.wait_send()
            local.wait()

        _on_each_place(x, y, c, at_place)

    launch()
    return f_ref[...]


def _swap_core_halves(grads, geoms, name):
    n_w = len(grads)
    n_cp = sum(1 if g.by_cols else N_CHIPS for g in geoms)

    def body(*refs):
        g_refs, t_refs = refs[:n_w], refs[n_w:2 * n_w]
        send_sems, recv_sems = refs[2 * n_w:]
        x, y, c = _place()

        def at_place(_, cc):
            def pairs(hc):
                out = []
                for g, g_ref, t_ref in zip(geoms, g_refs, t_refs):
                    if g.by_cols:
                        out.append((g_ref.at[pl.ds(hc * g.half_rows, g.half_rows), :], t_ref))
                    else:
                        out += [(g.half_of(g_ref, k, hc), g.part_of_halves(t_ref, k)) for k in range(N_CHIPS)]
                return out

            sends = [pltpu.make_async_remote_copy(src_ref=src, dst_ref=dst, send_sem=send_sems.at[n],
                                                  recv_sem=recv_sems.at[n], device_id=(x, y, 1 - cc), device_id_type=MESH)
                     for n, (src, dst) in enumerate(pairs(1 - cc))]
            for cp in sends:
                cp.start()
            for n, (src, dst) in enumerate(pairs(cc)):
                pltpu.make_async_remote_copy(src_ref=src, dst_ref=dst, send_sem=send_sems.at[n], recv_sem=recv_sems.at[n],
                                             device_id=(x, y, cc), device_id_type=MESH).wait_recv()
            for cp in sends:
                cp.wait_send()

        _on_each_place(x, y, c, at_place, by_chip=False)

    return pl.pallas_call(
        body, name=name,
        out_shape=[jax.ShapeDtypeStruct((g.full[0] // 2, g.full[1]), F32) for g in geoms],
        in_specs=[ANY] * n_w, out_specs=[ANY] * n_w,
        scratch_shapes=[pltpu.SemaphoreType.DMA((n_cp,)), pltpu.SemaphoreType.DMA((n_cp,))],
    )(*grads)


def _scatter_chip_sums(sums, geoms, name):
    n_w = len(sums)

    def body(*refs):
        s_refs, r_refs = refs[:n_w], refs[n_w:2 * n_w]
        send_sems, recv_sems = refs[2 * n_w:]
        x, y, c = _place()

        def at_place(k, _):
            sends = []
            for i, (g, s_ref, r_ref) in enumerate(zip(geoms, s_refs, r_refs)):
                for j, flip in enumerate(FLIPS):
                    kk = k ^ flip
                    cp = pltpu.make_async_remote_copy(
                        src_ref=g.part_of_halves(s_ref, kk), dst_ref=r_ref.at[j], send_sem=send_sems.at[3 * i + j],
                        recv_sem=recv_sems.at[3 * i + j], device_id=(kk // 2, kk % 2, c), device_id_type=MESH)
                    cp.start()
                    sends.append(cp)
            for i, (g, s_ref, r_ref) in enumerate(zip(geoms, s_refs, r_refs)):
                for j in range(len(FLIPS)):
                    pltpu.make_async_remote_copy(
                        src_ref=g.part_of_halves(s_ref, k), dst_ref=r_ref.at[j], send_sem=send_sems.at[3 * i + j],
                        recv_sem=recv_sems.at[3 * i + j], device_id=(x, y, c), device_id_type=MESH).wait_recv()
            for cp in sends:
                cp.wait_send()

        _on_each_place(x, y, c, at_place, by_core=False)

    return pl.pallas_call(
        body, name=name,
        out_shape=[jax.ShapeDtypeStruct((len(FLIPS),) + g.half, WIRE_DTYPE) for g in geoms],
        in_specs=[ANY] * n_w, out_specs=[ANY] * n_w,
        scratch_shapes=[pltpu.SemaphoreType.DMA((3 * n_w,)), pltpu.SemaphoreType.DMA((3 * n_w,))],
    )(*sums)


def _share_reduced_halves(reduced, geoms, name):
    n_w = len(reduced)

    def body(*refs):
        out_refs = refs[n_w:2 * n_w]
        send_sems, recv_sems = refs[2 * n_w:]
        x, y, c = _place()

        def at_place(_, cc):
            sends = []
            for i, (g, ref) in enumerate(zip(geoms, out_refs)):
                mine = g.half_of_shard(ref, cc)
                cp = pltpu.make_async_remote_copy(src_ref=mine, dst_ref=mine, send_sem=send_sems.at[i],
                                                  recv_sem=recv_sems.at[i], device_id=(x, y, 1 - cc), device_id_type=MESH)
                cp.start()
                sends.append(cp)
            for i, (g, ref) in enumerate(zip(geoms, out_refs)):
                theirs = g.half_of_shard(ref, 1 - cc)
                pltpu.make_async_remote_copy(src_ref=theirs, dst_ref=theirs, send_sem=send_sems.at[i],
                                             recv_sem=recv_sems.at[i], device_id=(x, y, cc), device_id_type=MESH).wait_recv()
            for cp in sends:
                cp.wait_send()

        _on_each_place(x, y, c, at_place, by_chip=False)

    return pl.pallas_call(
        body, name=name,
        out_shape=[jax.ShapeDtypeStruct(g.shard, F32) for g in geoms],
        in_specs=[ANY] * n_w, out_specs=[ANY] * n_w,
        input_output_aliases={i: i for i in range(n_w)},
        scratch_shapes=[pltpu.SemaphoreType.DMA((n_w,)), pltpu.SemaphoreType.DMA((n_w,))],
    )(*reduced)


def _chip_sum(place, grad, theirs, g, name):
    RH, C = theirs.shape
    h = g.half_rows
    tr = _tile(h, 256, 16)
    tc = _tile(C, 2048)
    per_half = h // tr

    if g.by_cols:
        grad_map = lambda i, j, p: (p[1] * per_half + i, j)
    else:
        grad_map = lambda i, j, p: ((i // per_half) * 2 * per_half + p[1] * per_half + i % per_half, j)

    def body(p_ref, a_ref, b_ref, o_ref):
        o_ref[...] = (a_ref[...] + b_ref[...]).astype(o_ref.dtype)

    return pl.pallas_call(
        body, name=name,
        grid_spec=pltpu.PrefetchScalarGridSpec(
            num_scalar_prefetch=1, grid=(RH // tr, C // tc),
            in_specs=[pl.BlockSpec((tr, tc), grad_map), pl.BlockSpec((tr, tc), lambda i, j, p: (i, j))],
            out_specs=pl.BlockSpec((tr, tc), lambda i, j, p: (i, j))),
        out_shape=jax.ShapeDtypeStruct((RH, C), WIRE_DTYPE),
        compiler_params=_params("parallel", "parallel"),
    )(place, grad, theirs)


def _reduce_half(place, grad, theirs, others, g, name):
    h, wc = g.half
    tr = _tile(h, 256, 16)
    per_half = h // tr
    if g.by_cols:
        tc = wc
        grad_map = lambda i, p: (p[1] * per_half + i, p[0])
        theirs_map = lambda i, p: (i, p[0])
    else:
        tc = wc
        grad_map = lambda i, p: (p[0] * 2 * per_half + p[1] * per_half + i, 0)
        theirs_map = lambda i, p: (p[0] * per_half + i, 0)

    def body(p_ref, a_ref, b_ref, o0_ref, o1_ref, o2_ref, out_ref):
        acc = a_ref[...] + b_ref[...]
        for o_ref in (o0_ref, o1_ref, o2_ref):
            acc = acc + o_ref[...].astype(F32)
        out_ref[...] = acc

    other_specs = [pl.BlockSpec((None, tr, tc), functools.partial(lambda i, p, j: (j, i, 0), j=j)) for j in range(len(FLIPS))]
    return pl.pallas_call(
        body, name=name,
        grid_spec=pltpu.PrefetchScalarGridSpec(
            num_scalar_prefetch=1, grid=(per_half,),
            in_specs=[pl.BlockSpec((tr, tc), grad_map), pl.BlockSpec((tr, tc), theirs_map)] + other_specs,
            out_specs=pl.BlockSpec((tr, tc), lambda i, p: (p[1] * per_half + i, 0))),
        out_shape=jax.ShapeDtypeStruct(g.shard, F32),
        compiler_params=_params("arbitrary"),
    )(place, grad, theirs, others, others, others)


SMALL = ("b_ada", "norm1_g", "v_norm_g", "w_spatial", "b_spatial", "out_norm_g", "norm2_g", "final_g")
BIG = ("w_in", "w_out", "w_gate", "w_up", "w_down")
BY_COLS = {"w_in": True, "w_out": False, "w_gate": True, "w_up": True, "w_down": False}
ORDER = ("w_ada", "b_ada", "norm1_g", "w_in", "v_norm_g", "w_spatial", "b_spatial", "out_norm_g", "w_out",
         "norm2_g", "w_gate", "w_up", "w_down", "final_g")


def _pack(parts):
    return jnp.concatenate([parts[n].reshape(-1) for n in SMALL]).reshape(-1, LANE)


def _unpack(slab, shapes):
    flat = slab.reshape(-1)
    out, at = {}, 0
    for n in SMALL:
        size = math.prod(shapes[n])
        out[n] = flat[at:at + size].reshape(shapes[n])
        at += size
    return out


def kernel(x, c, w_ada, b_ada, norm1_g, w_in, v_norm_g, w_spatial, b_spatial, out_norm_g, w_out, norm2_g, w_gate, w_up, w_down, final_g, loss_target, m_w_ada, m_b_ada, m_norm1_g, m_w_in, m_v_norm_g, m_w_spatial, m_b_spatial, m_out_norm_g, m_w_out, m_norm2_g, m_w_gate, m_w_up, m_w_down, m_final_g, v_w_ada, v_b_ada, v_norm1_g, v_w_in, v_v_norm_g, v_w_spatial, v_b_spatial, v_out_norm_g, v_w_out, v_norm2_g, v_w_gate, v_w_up, v_w_down, v_final_g):
    weights = dict(w_ada=w_ada, b_ada=b_ada, norm1_g=norm1_g, w_in=w_in, v_norm_g=v_norm_g, w_spatial=w_spatial,
                   b_spatial=b_spatial, out_norm_g=out_norm_g, w_out=w_out, norm2_g=norm2_g, w_gate=w_gate, w_up=w_up,
                   w_down=w_down, final_g=final_g)
    m_in = dict(w_ada=m_w_ada, b_ada=m_b_ada, norm1_g=m_norm1_g, w_in=m_w_in, v_norm_g=m_v_norm_g, w_spatial=m_w_spatial,
                b_spatial=m_b_spatial, out_norm_g=m_out_norm_g, w_out=m_w_out, norm2_g=m_norm2_g, w_gate=m_w_gate,
                w_up=m_w_up, w_down=m_w_down, final_g=m_final_g)
    v_in = dict(w_ada=v_w_ada, b_ada=v_b_ada, norm1_g=v_norm1_g, w_in=v_w_in, v_norm_g=v_v_norm_g, w_spatial=v_w_spatial,
                b_spatial=v_b_spatial, out_norm_g=v_out_norm_g, w_out=v_w_out, norm2_g=v_norm2_g, w_gate=v_w_gate,
                w_up=v_w_up, w_down=v_w_down, final_g=v_final_g)

    S, D = x.shape[1], x.shape[2]
    n_g = v_norm_g.shape[-1] // LANE
    n_h = (D - n_g * LANE) // LANE
    GW = n_g * LANE
    xi, yi, ci = _place()
    chip = 2 * xi + yi
    me = 4 * xi + 2 * yi + ci
    place = jnp.stack([chip, ci]).astype(jnp.int32)

    xs, target = x[0], loss_target[0]
    geoms = [_Sharded(weights[n].shape[1:], BY_COLS[n]) for n in BIG]

    full = {n: _gather_weight(_cast(weights[n][0], WIRE_DTYPE, "cast_" + n), g, "gather_" + n, 1 + i)
            for i, (n, g) in enumerate(zip(BIG, geoms))}

    c_pad = jnp.concatenate([c, jnp.zeros((7, D), F32)], axis=0)
    c_all = _allgather8(c_pad, "gather_c")[::8]
    n_ada = w_ada.shape[2]
    b_cols = lax.dynamic_slice(b_ada, (0, chip * n_ada), (1, n_ada))
    mod_parts = _allgather8(_mod_part(c_all, w_ada[0], b_cols, "mod_part"), "gather_mod")
    mod_all = mod_parts.reshape(N_CHIPS, 2, 8, n_ada)[:, 0].transpose(1, 0, 2).reshape(8, N_CHIPS * n_ada)
    mod = lax.dynamic_slice(mod_all, (me, 0), (1, 6 * D))
    shift1, scale1, gate1, shift2, scale2, gate2 = [mod[:, i * D:(i + 1) * D] for i in range(6)]

    b_t = b_spatial[0].T
    h1 = _norm_mod(xs, norm1_g, scale1, shift1, "norm1")
    proj, = _mm("nn", h1, full["w_in"], [F32], "proj")
    on_gm = _gmlp_fwd(proj, v_norm_g, w_spatial[0], b_t, out_norm_g, n_g, "gmlp_fwd")
    o_sb, on_sb, l_sum = _sb_fwd(proj, out_norm_g, n_g, n_h, "sb_fwd")
    o_n = jnp.concatenate([on_gm, on_sb], axis=1)
    attn, = _mm("nn", o_n, full["w_out"], [F32], "attn_out")
    x1, h2 = _residual_norm_mod(xs, attn, gate1, norm2_g, scale2, shift2, "norm2")
    a_g, a_u, f_in = _gate_up(h2, full["w_gate"], full["w_up"], "gate_up")
    f, = _mm("nn", f_in, full["w_down"], [F32], "down", tm=512)
    dx2, df, d_gate2, d_final_g, loss_part = _final_loss_bwd(x1, f, gate2, final_g.reshape(1, D), target, "final")
    loss = lax.psum(loss_part[0, 0], ("x", "y", "c"))

    grads = {}
    d_ag, d_au = _mm("nt", df, full["w_down"], [MXU_DTYPE, MXU_DTYPE], "d_ffn_in", extras=(a_g, a_u),
                     epilogue=_swiglu_bwd_epilogue)
    grads["w_down"], = _mm("tn", f_in, df, [F32], "d_w_down", tm=512, tn=1024)
    dh2_g, = _mm("nt", d_ag, full["w_gate"], [F32], "d_h2_gate", tm=512)
    dh2, = _mm("nt", d_au, full["w_up"], [F32], "d_h2", tm=512, extras=(dh2_g,), epilogue=_add_epilogue)
    grads["w_gate"], = _mm("tn", h2, d_ag, [F32], "d_w_gate")
    grads["w_up"], = _mm("tn", h2, d_au, [F32], "d_w_up")
    dx1, d_shift2, d_scale2, d_norm2_g, d_gate1, d_attn = _norm_mod_bwd(dh2, x1, dx2, norm2_g, scale2, "norm2_bwd",
                                                                        branch=attn, gate=gate1)
    d_on, = _mm("nt", d_attn, full["w_out"], [F32], "d_o")
    grads["w_out"], = _mm("tn", o_n, d_attn, [F32], "d_w_out")
    dp_gm, d_w_spatial, d_b_t, d_v_norm_g, d_og_gm = _gmlp_bwd(proj, d_on, v_norm_g, w_spatial[0], b_t, out_norm_g, n_g, "gmlp_bwd")
    dq, dk, dv, d_og_sb = _sb_bwd(proj, o_sb, l_sum, d_on, out_norm_g, n_g, n_h, "sb_bwd")
    dproj = jnp.concatenate([dp_gm, dq, dk, dv], axis=1)
    dh1, = _mm("nt", dproj, full["w_in"], [F32], "d_h1", tm=512)
    grads["w_in"], = _mm("tn", h1, dproj, [F32], "d_w_in")
    grad_x, d_shift1, d_scale1, d_norm1_g = _norm_mod_bwd(dh1, xs, dx1, norm1_g, scale1, "norm1_bwd")

    dmod = jnp.concatenate([d_shift1, d_scale1, d_gate1, d_shift2, d_scale2, d_gate2], axis=1)
    small_parts = dict(b_ada=dmod, norm1_g=d_norm1_g, v_norm_g=d_v_norm_g, w_spatial=d_w_spatial, b_spatial=d_b_t.T,
                       out_norm_g=jnp.concatenate([d_og_gm, d_og_sb], axis=1), norm2_g=d_norm2_g, final_g=d_final_g)
    slab = _pack(small_parts)
    rows = slab.shape[0]
    gathered = _allgather8(slab, "gather_small")
    small_shapes = {n: weights[n].shape for n in SMALL}
    small_sum = _sum_devices(gathered, 8, "sum_small")
    dmod_all = gathered.reshape(8, rows * LANE)[:, :6 * D]
    dmod_cols = lax.dynamic_slice(dmod_all, (0, chip * n_ada), (8, n_ada))

    theirs = _swap_core_halves([grads[n] for n in BIG], geoms, "swap_core_halves")
    sums = [_chip_sum(place, grads[n], t, g, "chip_sum_" + n) for n, t, g in zip(BIG, theirs, geoms)]
    others = _scatter_chip_sums(sums, geoms, "scatter_chip_sums")
    halves = [_reduce_half(place, grads[n], t, o, g, "reduce_" + n) for n, t, o, g in zip(BIG, theirs, others, geoms)]
    reduced = dict(zip(BIG, _share_reduced_halves(halves, geoms, "share_halves")))

    grad_out, delta, new_m, new_v = {}, {}, {}, {}
    for n in BIG:
        grad_out[n] = reduced[n][None]
        d, mo, vo = _adamw(weights[n][0], reduced[n], m_in[n][0], v_in[n][0], "adamw_" + n)
        delta[n], new_m[n], new_v[n] = d[None], mo[None], vo[None]
    g_ada, d, mo, vo = _adamw_ada(c_all, dmod_cols, w_ada[0], m_w_ada[0], v_w_ada[0], "adamw_w_ada")
    grad_out["w_ada"], delta["w_ada"], new_m["w_ada"], new_v["w_ada"] = g_ada[None], d[None], mo[None], vo[None]
    d, mo, vo = _adamw(_pack({n: weights[n] for n in SMALL}), small_sum, _pack({n: m_in[n] for n in SMALL}),
                       _pack({n: v_in[n] for n in SMALL}), "adamw_small")
    for dst, slab_out in ((grad_out, small_sum), (delta, d), (new_m, mo), (new_v, vo)):
        dst.update(_unpack(slab_out, small_shapes))

    return (loss, grad_x[None], *[grad_out[n] for n in ORDER], *[delta[n] for n in ORDER],
            *[new_m[n] for n in ORDER], *[new_v[n] for n in ORDER])
```

```python
import functools
import math

import jax
import jax.numpy as jnp
from jax import lax
from jax.experimental import pallas as pl
from jax.experimental.pallas import tpu as pltpu
from jax.experimental.pallas import tpu_sc as plsc

F32 = jnp.float32
MXU_DTYPE = jnp.bfloat16
WIRE_DTYPE = jnp.bfloat16
EPS = 1e-6
LANE = 128
V7X_VMEM_LIMIT = 56 * 1024 * 1024
MESH = pl.DeviceIdType.MESH
N_CHIPS = 4
FLIPS = (2, 1, 3)
ANY = pl.BlockSpec(memory_space=pl.ANY)

ADAM_LR = 0.001
ADAM_B1 = 0.9
ADAM_B2 = 0.999
ADAM_EPS = 1e-08
ADAM_WD = 0.01
ADAM_STEP = 10


def _params(*semantics):
    return pltpu.CompilerParams(dimension_semantics=semantics or None, vmem_limit_bytes=V7X_VMEM_LIMIT)


def _tile(dim, pref, unit=LANE):
    best = None
    t = unit
    while t <= min(dim, pref):
        if dim % t == 0:
            best = t
        t += unit
    return best if best is not None else dim


def _sum0(v):
    return jnp.sum(v, axis=0, keepdims=True)


def _mean1(v):
    return jnp.mean(v, axis=-1, keepdims=True)


def _gelu(x):
    return 0.5 * x * (1.0 + lax.erf(x * (1.0 / math.sqrt(2.0))))


def _gelu_grad(x):
    cdf = 0.5 * (1.0 + lax.erf(x * (1.0 / math.sqrt(2.0))))
    return cdf + x * jnp.exp(-0.5 * x * x) * (1.0 / math.sqrt(2.0 * math.pi))


def _dot(a, b, dims):
    return lax.dot_general(a, b, (dims, ((), ())), preferred_element_type=F32)


NN = ((1,), (0,))
NT = ((1,), (1,))
TN = ((0,), (0,))


def _mm(kind, a, b, out_dtypes, name, tm=1024, tn=512, extras=(), epilogue=None):
    if kind == "nn":
        (M, K), N = a.shape, b.shape[1]
    elif kind == "nt":
        (M, K), N = a.shape, b.shape[0]
    else:
        (K, M), N = a.shape, b.shape[1]
    tm, tn = _tile(M, tm), _tile(N, tn)
    a_spec = pl.BlockSpec((K, tm), lambda i, j: (0, i)) if kind == "tn" else pl.BlockSpec((tm, K), lambda i, j: (i, 0))
    b_spec = pl.BlockSpec((tn, K), lambda i, j: (j, 0)) if kind == "nt" else pl.BlockSpec((K, tn), lambda i, j: (0, j))
    mn_spec = pl.BlockSpec((tm, tn), lambda i, j: (i, j))
    dims = {"nn": NN, "nt": NT, "tn": TN}[kind]
    n_extra = len(extras)

    def body(a_ref, b_ref, *rest):
        acc = _dot(a_ref[...], b_ref[...], dims)
        res = (acc,) if epilogue is None else epilogue(acc, *[e[...] for e in rest[:n_extra]])
        for o_ref, r in zip(rest[n_extra:], res):
            o_ref[...] = r.astype(o_ref.dtype)

    outs = pl.pallas_call(
        body, name=name, grid=(M // tm, N // tn),
        in_specs=[a_spec, b_spec] + [mn_spec] * n_extra,
        out_specs=[mn_spec] * len(out_dtypes),
        out_shape=[jax.ShapeDtypeStruct((M, N), d) for d in out_dtypes],
        compiler_params=_params("parallel", "arbitrary"),
    )(a, b, *extras)
    return outs


def _gate_up(h, wg, wu, name):
    (M, K), N = h.shape, wg.shape[1]
    tm, tn = _tile(M, 1024), _tile(N, 512)

    def body(h_ref, wg_ref, wu_ref, ag_ref, au_ref, f_ref):
        hv = h_ref[...]
        ag = _dot(hv, wg_ref[...], NN)
        au = _dot(hv, wu_ref[...], NN)
        ag_ref[...] = ag
        au_ref[...] = au
        f_ref[...] = (ag * jax.nn.sigmoid(ag) * au).astype(f_ref.dtype)

    w_spec = pl.BlockSpec((K, tn), lambda i, j: (0, j))
    mn_spec = pl.BlockSpec((tm, tn), lambda i, j: (i, j))
    return pl.pallas_call(
        body, name=name, grid=(M // tm, N // tn),
        in_specs=[pl.BlockSpec((tm, K), lambda i, j: (i, 0)), w_spec, w_spec],
        out_specs=[mn_spec] * 3,
        out_shape=[jax.ShapeDtypeStruct((M, N), F32), jax.ShapeDtypeStruct((M, N), F32),
                   jax.ShapeDtypeStruct((M, N), MXU_DTYPE)],
        compiler_params=_params("parallel", "arbitrary"),
    )(h, wg, wu)


def _swiglu_bwd_epilogue(dfin, ag, au):
    sg = jax.nn.sigmoid(ag)
    d_au = dfin * (ag * sg)
    d_ag = dfin * au * (sg * (1.0 + ag * (1.0 - sg)))
    return d_ag, d_au


def _add_epilogue(acc, other):
    return (acc + other,)


def _row_specs(ts, width):
    return pl.BlockSpec((ts, width), lambda i: (i, 0)), pl.BlockSpec((1, width), lambda i: (0, 0))


def _cast(a, dtype, name):
    R, C = a.shape
    tr = _tile(R, 512, 16)
    spec = pl.BlockSpec((tr, C), lambda i: (i, 0))

    def body(a_ref, o_ref):
        o_ref[...] = a_ref[...].astype(o_ref.dtype)

    return pl.pallas_call(body, name=name, grid=(R // tr,), in_specs=[spec], out_specs=spec,
                          out_shape=jax.ShapeDtypeStruct((R, C), dtype), compiler_params=_params("parallel"))(a)


def _norm_mod(x, g, scale, shift, name):
    S, D = x.shape
    ts = _tile(S, 256, 16)
    tile, vec = _row_specs(ts, D)

    def body(x_ref, g_ref, sc_ref, sh_ref, h_ref):
        xv = x_ref[...]
        r = lax.rsqrt(_mean1(xv * xv) + EPS)
        h_ref[...] = ((xv * r) * g_ref[...] * (1.0 + sc_ref[...]) + sh_ref[...]).astype(h_ref.dtype)

    return pl.pallas_call(body, name=name, grid=(S // ts,), in_specs=[tile, vec, vec, vec], out_specs=tile,
                          out_shape=jax.ShapeDtypeStruct((S, D), MXU_DTYPE), compiler_params=_params("parallel"))(x, g, scale, shift)


def _residual_norm_mod(x, attn, gate, g, scale, shift, name):
    S, D = x.shape
    ts = _tile(S, 256, 16)
    tile, vec = _row_specs(ts, D)

    def body(x_ref, a_ref, gate_ref, g_ref, sc_ref, sh_ref, x1_ref, h_ref):
        x1 = x_ref[...] + gate_ref[...] * a_ref[...]
        x1_ref[...] = x1
        r = lax.rsqrt(_mean1(x1 * x1) + EPS)
        h_ref[...] = ((x1 * r) * g_ref[...] * (1.0 + sc_ref[...]) + sh_ref[...]).astype(h_ref.dtype)

    return pl.pallas_call(body, name=name, grid=(S // ts,), in_specs=[tile, tile, vec, vec, vec, vec],
                          out_specs=[tile, tile],
                          out_shape=[jax.ShapeDtypeStruct((S, D), F32), jax.ShapeDtypeStruct((S, D), MXU_DTYPE)],
                          compiler_params=_params("parallel"))(x, attn, gate, g, scale, shift)


def _final_loss_bwd(x1, f, gate2, final_g, target, name):
    S, D = x1.shape
    ts = _tile(S, 256, 16)
    tile, vec = _row_specs(ts, D)
    loss_spec = pl.BlockSpec((1, LANE), lambda i: (0, 0))

    def body(x1_ref, f_ref, gate_ref, g_ref, t_ref, dx2_ref, df_ref, dgate_ref, dg_ref, loss_ref):
        @pl.when(pl.program_id(0) == 0)
        def _():
            dgate_ref[...] = jnp.zeros_like(dgate_ref)
            dg_ref[...] = jnp.zeros_like(dg_ref)
            loss_ref[...] = jnp.zeros_like(loss_ref)

        fv, gate, g = f_ref[...], gate_ref[...], g_ref[...]
        x2 = x1_ref[...] + gate * fv
        r = lax.rsqrt(_mean1(x2 * x2) + EPS)
        xn = x2 * r
        err = xn * g - t_ref[...]
        loss_ref[...] += jnp.broadcast_to(0.5 * _sum0(_mean1(err * err)), loss_ref.shape)
        dy = err * (1.0 / D)
        dg_ref[...] += _sum0(dy * xn)
        dxn = dy * g
        dx2 = r * (dxn - xn * _mean1(dxn * xn))
        dx2_ref[...] = dx2
        dgate_ref[...] += _sum0(dx2 * fv)
        df_ref[...] = (dx2 * gate).astype(df_ref.dtype)

    return pl.pallas_call(
        body, name=name, grid=(S // ts,), in_specs=[tile, tile, vec, vec, tile],
        out_specs=[tile, tile, vec, vec, loss_spec],
        out_shape=[jax.ShapeDtypeStruct((S, D), F32), jax.ShapeDtypeStruct((S, D), MXU_DTYPE),
                   jax.ShapeDtypeStruct((1, D), F32), jax.ShapeDtypeStruct((1, D), F32),
                   jax.ShapeDtypeStruct((1, LANE), F32)],
        compiler_params=_params("arbitrary"),
    )(x1, f, gate2, final_g, target)


def _norm_mod_bwd(dh, xin, dres, g, scale, name, branch=None, gate=None):
    S, D = xin.shape
    ts = _tile(S, 256, 16)
    tile, vec = _row_specs(ts, D)
    with_gate = branch is not None

    def body(*refs):
        if with_gate:
            dh_ref, x_ref, dres_ref, g_ref, sc_ref, br_ref, gate_ref, dx_ref, dshift_ref, dscale_ref, dg_ref, dgate_ref, dbr_ref = refs
            accs = (dshift_ref, dscale_ref, dg_ref, dgate_ref)
        else:
            dh_ref, x_ref, dres_ref, g_ref, sc_ref, dx_ref, dshift_ref, dscale_ref, dg_ref = refs
            accs = (dshift_ref, dscale_ref, dg_ref)

        @pl.when(pl.program_id(0) == 0)
        def _():
            for acc in accs:
                acc[...] = jnp.zeros_like(acc)

        dh_v, xv, g_v = dh_ref[...], x_ref[...], g_ref[...]
        one_sc = 1.0 + sc_ref[...]
        r = lax.rsqrt(_mean1(xv * xv) + EPS)
        xn = xv * r
        dshift_ref[...] += _sum0(dh_v)
        dscale_ref[...] += _sum0(dh_v * (xn * g_v))
        dg_ref[...] += _sum0(dh_v * one_sc * xn)
        dxn = dh_v * (g_v * one_sc)
        dx = dres_ref[...] + r * (dxn - xn * _mean1(dxn * xn))
        dx_ref[...] = dx
        if with_gate:
            dgate_ref[...] += _sum0(dx * br_ref[...])
            dbr_ref[...] = (dx * gate_ref[...]).astype(dbr_ref.dtype)

    ins = [dh, xin, dres, g, scale] + ([branch, gate] if with_gate else [])
    in_specs = [tile, tile, tile, vec, vec] + ([tile, vec] if with_gate else [])
    out_specs = [tile, vec, vec, vec] + ([vec, tile] if with_gate else [])
    out_shape = [jax.ShapeDtypeStruct((S, D), F32)] + [jax.ShapeDtypeStruct((1, D), F32)] * 3
    if with_gate:
        out_shape += [jax.ShapeDtypeStruct((1, D), F32), jax.ShapeDtypeStruct((S, D), MXU_DTYPE)]
    return pl.pallas_call(body, name=name, grid=(S // ts,), in_specs=in_specs, out_specs=out_specs,
                          out_shape=out_shape, compiler_params=_params("arbitrary"))(*ins)


def _causal_weights(ws_ref, wt_ref, n_g):
    row = lax.broadcasted_iota(jnp.int32, (LANE, LANE), 0)
    col = lax.broadcasted_iota(jnp.int32, (LANE, LANE), 1)
    for g in range(n_g):
        wt_ref[g] = jnp.where(col <= row, ws_ref[g], 0.0).astype(wt_ref.dtype)


def _group_layernorm(v):
    xc = v - _mean1(v)
    rstd = lax.rsqrt(_mean1(xc * xc) + EPS)
    return xc * rstd, rstd


def _gmlp_fwd(proj, v_gain, w_s, b_t, out_gain, n_g, name):
    S = proj.shape[0]
    GW = n_g * LANE

    def body(p_ref, vg_ref, ws_ref, bt_ref, og_ref, on_ref, wt_ref):
        @pl.when(pl.program_id(0) == 0)
        def _():
            _causal_weights(ws_ref, wt_ref, n_g)

        for g in range(n_g):
            cols = slice(g * LANE, (g + 1) * LANE)
            u = _gelu(p_ref[:, cols])
            v = _gelu(p_ref[:, GW + g * LANE:GW + (g + 1) * LANE])
            vhat, _ = _group_layernorm(v)
            vln = (vhat * vg_ref[:, cols]).astype(MXU_DTYPE)
            mixed = _dot(wt_ref[g], vln, NN) + bt_ref[:, g:g + 1]
            o = u * mixed
            r = lax.rsqrt(_mean1(o * o) + EPS)
            on_ref[:, cols] = (o * r * og_ref[:, cols]).astype(on_ref.dtype)

    return pl.pallas_call(
        body, name=name, grid=(S // LANE,),
        in_specs=[pl.BlockSpec((LANE, 2 * GW), lambda n: (n, 0)),
                  pl.BlockSpec((1, GW), lambda n: (0, 0)),
                  pl.BlockSpec((n_g, LANE, LANE), lambda n: (0, 0, 0)),
                  pl.BlockSpec((LANE, n_g), lambda n: (0, 0)),
                  pl.BlockSpec((1, GW), lambda n: (0, 0))],
        out_specs=pl.BlockSpec((LANE, GW), lambda n: (n, 0)),
        out_shape=jax.ShapeDtypeStruct((S, GW), MXU_DTYPE),
        scratch_shapes=[pltpu.VMEM((n_g, LANE, LANE), MXU_DTYPE)],
        compiler_params=_params("arbitrary"),
    )(proj, v_gain, w_s, b_t, out_gain)


def _gmlp_bwd(proj, d_on, v_gain, w_s, b_t, out_gain, n_g, name):
    S = proj.shape[0]
    GW = n_g * LANE

    def body(p_ref, dn_ref, vg_ref, ws_ref, bt_ref, og_ref, dp_ref, dws_ref, dbt_ref, dvg_ref, dog_ref, wt_ref):
        @pl.when(pl.program_id(0) == 0)
        def _():
            _causal_weights(ws_ref, wt_ref, n_g)
            dws_ref[...] = jnp.zeros_like(dws_ref)
            dbt_ref[...] = jnp.zeros_like(dbt_ref)
            dvg_ref[...] = jnp.zeros_like(dvg_ref)
            dog_ref[...] = jnp.zeros_like(dog_ref)

        row = lax.broadcasted_iota(jnp.int32, (LANE, LANE), 0)
        col = lax.broadcasted_iota(jnp.int32, (LANE, LANE), 1)
        for g in range(n_g):
            cols = slice(g * LANE, (g + 1) * LANE)
            vcols = slice(GW + g * LANE, GW + (g + 1) * LANE)
            pu, pv = p_ref[:, cols], p_ref[:, vcols]
            u, v = _gelu(pu), _gelu(pv)
            vhat, rstd = _group_layernorm(v)
            gain = vg_ref[:, cols]
            vln = (vhat * gain).astype(MXU_DTYPE)
            mixed = _dot(wt_ref[g], vln, NN) + bt_ref[:, g:g + 1]
            o = u * mixed
            r = lax.rsqrt(_mean1(o * o) + EPS)
            oh = o * r
            dn = dn_ref[:, cols]
            dog_ref[:, cols] += _sum0(dn * oh)
            dhn = dn * og_ref[:, cols]
            d_o = r * (dhn - oh * _mean1(dhn * oh))
            du = d_o * mixed
            dmix = d_o * u
            dbt_ref[:, g:g + 1] += jnp.sum(dmix, axis=1, keepdims=True)
            dmix_b = dmix.astype(MXU_DTYPE)
            dws_ref[g] += jnp.where(col <= row, _dot(dmix_b, vln, NT), 0.0)
            dvln = _dot(wt_ref[g], dmix_b, TN)
            dvg_ref[:, cols] += _sum0(dvln * vhat)
            dxh = dvln * gain
            dv = rstd * (dxh - _mean1(dxh) - vhat * _mean1(dxh * vhat))
            dp_ref[:, cols] = (du * _gelu_grad(pu)).astype(dp_ref.dtype)
            dp_ref[:, vcols] = (dv * _gelu_grad(pv)).astype(dp_ref.dtype)

    return pl.pallas_call(
        body, name=name, grid=(S // LANE,),
        in_specs=[pl.BlockSpec((LANE, 2 * GW), lambda n: (n, 0)),
                  pl.BlockSpec((LANE, GW), lambda n: (n, 0)),
                  pl.BlockSpec((1, GW), lambda n: (0, 0)),
                  pl.BlockSpec((n_g, LANE, LANE), lambda n: (0, 0, 0)),
                  pl.BlockSpec((LANE, n_g), lambda n: (0, 0)),
                  pl.BlockSpec((1, GW), lambda n: (0, 0))],
        out_specs=[pl.BlockSpec((LANE, 2 * GW), lambda n: (n, 0)),
                   pl.BlockSpec((n_g, LANE, LANE), lambda n: (0, 0, 0)),
                   pl.BlockSpec((LANE, n_g), lambda n: (0, 0)),
                   pl.BlockSpec((1, GW), lambda n: (0, 0)),
                   pl.BlockSpec((1, GW), lambda n: (0, 0))],
        out_shape=[jax.ShapeDtypeStruct((S, 2 * GW), MXU_DTYPE),
                   jax.ShapeDtypeStruct((n_g, LANE, LANE), F32),
                   jax.ShapeDtypeStruct((LANE, n_g), F32),
                   jax.ShapeDtypeStruct((1, GW), F32),
                   jax.ShapeDtypeStruct((1, GW), F32)],
        scratch_shapes=[pltpu.VMEM((n_g, LANE, LANE), MXU_DTYPE)],
        compiler_params=_params("arbitrary"),
    )(proj, d_on, v_gain, w_s, b_t, out_gain)


def _tri_sum(v, tri):
    hi = v.astype(MXU_DTYPE)
    lo = (v - hi.astype(F32)).astype(MXU_DTYPE)
    return _dot(hi, tri, NN) + _dot(lo, tri, NN)


def _log_sigmoids(z):
    sp = jnp.log1p(jnp.exp(-jnp.abs(z)))
    return jnp.minimum(z, 0.0) - sp, jnp.minimum(-z, 0.0) - sp


def _rows(i, size):
    return pl.ds(pl.multiple_of(i * size, size), size)


SB_QUERY_TILE = 512
SB_KEY_TILE = 256


def _sb_tiles(S):
    tq = _tile(S, SB_QUERY_TILE)
    tk = _tile(tq, SB_KEY_TILE)
    return tq, tk, S // tq, tq // tk


def _triangle(n, keep):
    row = lax.broadcasted_iota(jnp.int32, (n, n), 0)
    col = lax.broadcasted_iota(jnp.int32, (n, n), 1)
    return jnp.where(keep(row, col), 1.0, 0.0).astype(MXU_DTYPE)


def _strictly_before(tq, tk, key_offset):
    row = lax.broadcasted_iota(jnp.int32, (tq, tk), 0)
    col = lax.broadcasted_iota(jnp.int32, (tq, tk), 1)
    return col + key_offset < row


def _sb_specs(S, n_g, n_h):
    base = 2 * n_g
    q_spec = pl.BlockSpec((S, LANE), lambda h: (0, base + h))
    k_spec = pl.BlockSpec((S, LANE), lambda h: (0, base + n_h + h))
    v_spec = pl.BlockSpec((S, LANE), lambda h: (0, base + 2 * n_h + h))
    gain_spec = pl.BlockSpec((1, LANE), lambda h: (0, n_g + h))
    head_spec = pl.BlockSpec((S, LANE), lambda h: (0, h))
    return q_spec, k_spec, v_spec, gain_spec, head_spec


def _sb_fwd(proj, out_gain, n_g, n_h, name):
    S = proj.shape[0]
    TQ, TK, NQ, KPQ = _sb_tiles(S)
    scale = LANE ** -0.5
    q_spec, k_spec, v_spec, gain_spec, head_spec = _sb_specs(S, n_g, n_h)

    def body(q_ref, k_ref, v_ref, og_ref, o_ref, on_ref, ls_ref, qb, kb, vb):
        qb[...] = q_ref[...].astype(MXU_DTYPE)
        kb[...] = k_ref[...].astype(MXU_DTYPE)
        vb[...] = v_ref[...].astype(MXU_DTYPE)
        after = _triangle(TK, lambda r, c: r > c)

        def block(qi, j, ctail, acc, key_offset):
            z = _dot(qi, kb[_rows(j, TK), :], NT) * scale
            lb, l1m = _log_sigmoids(z)
            if key_offset is not None:
                strict = _strictly_before(TQ, TK, key_offset)
                l1m = jnp.where(strict, l1m, 0.0)
            a = jnp.exp(lb + ctail + _tri_sum(l1m, after))
            if key_offset is not None:
                a = jnp.where(strict, a, 0.0)
            acc = acc + _dot(a.astype(MXU_DTYPE), vb[_rows(j, TK), :], NN)
            return ctail + jnp.sum(l1m, axis=1, keepdims=True), acc

        def q_loop(i, carry):
            qi = qb[_rows(i, TQ), :]
            state = (jnp.zeros((TQ, 1), F32), jnp.zeros((TQ, LANE), F32))
            for d in reversed(range(KPQ)):
                state = block(qi, i * KPQ + d, state[0], state[1], d * TK)
            ctail, acc = lax.fori_loop(
                0, i * KPQ, lambda jj, st: block(qi, i * KPQ - 1 - jj, st[0], st[1], None), state)
            ls_ref[_rows(i, TQ), :] = jnp.broadcast_to(ctail, (TQ, LANE))
            o_ref[_rows(i, TQ), :] = acc
            r = lax.rsqrt(_mean1(acc * acc) + EPS)
            on_ref[_rows(i, TQ), :] = (acc * r * og_ref[...]).astype(on_ref.dtype)
            return carry

        lax.fori_loop(0, NQ, q_loop, 0)

    return pl.pallas_call(
        body, name=name, grid=(n_h,),
        in_specs=[q_spec, k_spec, v_spec, gain_spec],
        out_specs=[head_spec, head_spec, head_spec],
        out_shape=[jax.ShapeDtypeStruct((S, n_h * LANE), F32), jax.ShapeDtypeStruct((S, n_h * LANE), MXU_DTYPE),
                   jax.ShapeDtypeStruct((S, n_h * LANE), F32)],
        scratch_shapes=[pltpu.VMEM((S, LANE), MXU_DTYPE)] * 3,
        compiler_params=_params("parallel"),
    )(proj, proj, proj, out_gain)


def _sb_bwd(proj, o_sb, l_sum, d_on, out_gain, n_g, n_h, name):
    S = proj.shape[0]
    TQ, TK, NQ, KPQ = _sb_tiles(S)
    scale = LANE ** -0.5
    q_spec, k_spec, v_spec, gain_spec, head_spec = _sb_specs(S, n_g, n_h)
    dn_spec = pl.BlockSpec((S, LANE), lambda h: (0, n_g + h))
    dgain_spec = pl.BlockSpec((1, LANE), lambda h: (0, h))

    def body(q_ref, k_ref, v_ref, o_ref, ls_ref, dn_ref, og_ref, dq_ref, dk_ref, dv_ref, dog_ref,
             qb, kb, vb, dob, dk_acc, dv_acc):
        qb[...] = q_ref[...].astype(MXU_DTYPE)
        kb[...] = k_ref[...].astype(MXU_DTYPE)
        vb[...] = v_ref[...].astype(MXU_DTYPE)
        o, dn = o_ref[...], dn_ref[...]
        r = lax.rsqrt(_mean1(o * o) + EPS)
        oh = o * r
        dog_ref[...] = _sum0(dn * oh)
        dhn = dn * og_ref[...]
        dob[...] = (r * (dhn - oh * _mean1(dhn * oh))).astype(MXU_DTYPE)
        dk_acc[...] = jnp.zeros_like(dk_acc)
        dv_acc[...] = jnp.zeros_like(dv_acc)

        up_to = _triangle(TK, lambda r, c: r <= c)
        before = _triangle(TK, lambda r, c: r < c)

        def block(qi, doi, ltot, j, cl, cdl, dq, key_offset):
            kj, vj = kb[_rows(j, TK), :], vb[_rows(j, TK), :]
            z = _dot(qi, kj, NT) * scale
            lb, l1m_all = _log_sigmoids(z)
            l1m = l1m_all
            if key_offset is not None:
                strict = _strictly_before(TQ, TK, key_offset)
                l1m = jnp.where(strict, l1m_all, 0.0)
            a = jnp.exp(lb + (ltot - (cl + _tri_sum(l1m, up_to))))
            if key_offset is not None:
                a = jnp.where(strict, a, 0.0)
            dl = _dot(doi, vj, NT) * a
            d_l1m = cdl + _tri_sum(dl, before)
            dz = dl * jnp.exp(l1m_all) - jnp.exp(lb) * d_l1m
            if key_offset is not None:
                dz = jnp.where(strict, dz, 0.0)
            dzs = (dz * scale).astype(MXU_DTYPE)
            dq = dq + _dot(dzs, kj, NN)
            dk_acc[_rows(j, TK), :] += _dot(dzs, qi, TN)
            dv_acc[_rows(j, TK), :] += _dot(a.astype(MXU_DTYPE), doi, TN)
            return (cl + jnp.sum(l1m, axis=1, keepdims=True), cdl + jnp.sum(dl, axis=1, keepdims=True), dq)

        def q_loop(i, carry):
            qi, doi = qb[_rows(i, TQ), :], dob[_rows(i, TQ), :]
            ltot = ls_ref[_rows(i, TQ), :][:, :1]
            zero_col = jnp.zeros((TQ, 1), F32)
            state = lax.fori_loop(
                0, i * KPQ, lambda j, st: block(qi, doi, ltot, j, st[0], st[1], st[2], None),
                (zero_col, zero_col, jnp.zeros((TQ, LANE), F32)))
            for d in range(KPQ):
                state = block(qi, doi, ltot, i * KPQ + d, state[0], state[1], state[2], d * TK)
            dq_ref[_rows(i, TQ), :] = state[2].astype(dq_ref.dtype)
            return carry

        lax.fori_loop(0, NQ, q_loop, 0)
        dk_ref[...] = dk_acc[...].astype(dk_ref.dtype)
        dv_ref[...] = dv_acc[...].astype(dv_ref.dtype)

    W = n_h * LANE
    return pl.pallas_call(
        body, name=name, grid=(n_h,),
        in_specs=[q_spec, k_spec, v_spec, head_spec, head_spec, dn_spec, gain_spec],
        out_specs=[head_spec, head_spec, head_spec, dgain_spec],
        out_shape=[jax.ShapeDtypeStruct((S, W), MXU_DTYPE)] * 3 + [jax.ShapeDtypeStruct((1, W), F32)],
        scratch_shapes=[pltpu.VMEM((S, LANE), MXU_DTYPE)] * 4 + [pltpu.VMEM((S, LANE), F32)] * 2,
        compiler_params=_params("parallel"),
    )(proj, proj, proj, o_sb, l_sum, d_on, out_gain)


def _mod_part(c_all, w_ada, b_ada_cols, name):
    B, K = c_all.shape
    N = w_ada.shape[1]
    tn = _tile(N, 512)

    def body(c_ref, w_ref, b_ref, o_ref):
        cv = c_ref[...]
        ca = (cv * jax.nn.sigmoid(cv)).astype(MXU_DTYPE)
        o_ref[...] = _dot(ca, w_ref[...].astype(MXU_DTYPE), NN) + b_ref[...]

    return pl.pallas_call(
        body, name=name, grid=(N // tn,),
        in_specs=[pl.BlockSpec((B, K), lambda j: (0, 0)), pl.BlockSpec((K, tn), lambda j: (0, j)),
                  pl.BlockSpec((1, tn), lambda j: (0, j))],
        out_specs=pl.BlockSpec((B, tn), lambda j: (0, j)),
        out_shape=jax.ShapeDtypeStruct((B, N), F32), compiler_params=_params("parallel"))(c_all, w_ada, b_ada_cols)


def _adamw_math(w, g, m, v):
    m = ADAM_B1 * m + (1.0 - ADAM_B1) * g
    v = ADAM_B2 * v + (1.0 - ADAM_B2) * (g * g)
    m_hat = m / (1.0 - ADAM_B1 ** ADAM_STEP)
    v_hat = v / (1.0 - ADAM_B2 ** ADAM_STEP)
    delta = -ADAM_LR * (m_hat / (jnp.sqrt(v_hat) + ADAM_EPS) + ADAM_WD * w)
    return delta, m, v


def _adamw(w, g, m, v, name):
    R, C = w.shape
    tr = _tile(R, max(8, (1 << 19) // C), 8)
    spec = pl.BlockSpec((tr, C), lambda i: (i, 0))

    def body(w_ref, g_ref, m_ref, v_ref, d_ref, mo_ref, vo_ref):
        d_ref[...], mo_ref[...], vo_ref[...] = _adamw_math(w_ref[...], g_ref[...], m_ref[...], v_ref[...])

    return pl.pallas_call(body, name=name, grid=(R // tr,), in_specs=[spec] * 4, out_specs=[spec] * 3,
                          out_shape=[jax.ShapeDtypeStruct((R, C), F32)] * 3, compiler_params=_params("parallel"))(w, g, m, v)


def _adamw_ada(c_all, dmod_cols, w, m, v, name):
    K, N = w.shape
    B = c_all.shape[0]
    tk, tn = _tile(K, 512), _tile(N, 1024)
    spec = pl.BlockSpec((tk, tn), lambda i, j: (i, j))

    def body(c_ref, dm_ref, w_ref, m_ref, v_ref, g_ref, d_ref, mo_ref, vo_ref):
        cv = c_ref[...]
        ca = (cv * jax.nn.sigmoid(cv)).astype(MXU_DTYPE)
        g = _dot(ca, dm_ref[...].astype(MXU_DTYPE), TN)
        g_ref[...] = g
        d_ref[...], mo_ref[...], vo_ref[...] = _adamw_math(w_ref[...], g, m_ref[...], v_ref[...])

    return pl.pallas_call(
        body, name=name, grid=(K // tk, N // tn),
        in_specs=[pl.BlockSpec((B, tk), lambda i, j: (0, i)), pl.BlockSpec((B, tn), lambda i, j: (0, j)), spec, spec, spec],
        out_specs=[spec] * 4, out_shape=[jax.ShapeDtypeStruct((K, N), F32)] * 4,
        compiler_params=_params("parallel", "parallel"))(c_all, dmod_cols, w, m, v)


def _sum_devices(gathered, n_dev, name):
    R = gathered.shape[0] // n_dev
    C = gathered.shape[1]
    tr = _tile(R, 512, 8)
    n_blk = R // tr

    def body(*refs):
        acc = refs[0][...]
        for r in refs[1:n_dev]:
            acc = acc + r[...]
        refs[n_dev][...] = acc

    in_specs = [pl.BlockSpec((tr, C), functools.partial(lambda i, d: (d * n_blk + i, 0), d=d)) for d in range(n_dev)]
    return pl.pallas_call(body, name=name, grid=(n_blk,), in_specs=in_specs,
                          out_specs=pl.BlockSpec((tr, C), lambda i: (i, 0)),
                          out_shape=jax.ShapeDtypeStruct((R, C), F32), compiler_params=_params("parallel"))(*([gathered] * n_dev))


def _place():
    x, y, c = lax.axis_index("x"), lax.axis_index("y"), lax.axis_index("c")
    return x, y, c


def _allgather8(blk, name):
    m_per, n = blk.shape

    def body(x_ref, out_ref, send_sems, recv_sems, local_sem):
        x, y, c = _place()
        me, sibling = (x, y, c), (x, y, 1 - c)
        chips = [(1 - x, y), (x, 1 - y), (1 - x, 1 - y)]

        def rows(px, py, pc):
            return out_ref.at[pl.ds((4 * px + 2 * py + pc) * m_per, m_per), :]

        def copy(k, block, to, src=None):
            return pltpu.make_async_remote_copy(
                src_ref=rows(*block) if src is None else src, dst_ref=rows(*block),
                send_sem=send_sems.at[k], recv_sem=recv_sems.at[k], device_id=to, device_id_type=MESH)

        mine = pltpu.make_async_copy(x_ref, rows(*me), local_sem)
        mine.start()
        first = [copy(0, me, sibling, src=x_ref)]
        first += [copy(1 + j, me, (*chip, c), src=x_ref) for j, chip in enumerate(chips)]
        for cp in first:
            cp.start()
        passed = [copy(4 + j, (*chip, c), sibling) for j, chip in enumerate(chips)]
        for j, chip in enumerate(chips):
            copy(1 + j, (*chip, c), me).wait_recv()
            passed[j].start()
        copy(0, sibling, me).wait_recv()
        for j, chip in enumerate(chips):
            copy(4 + j, (*chip, 1 - c), me).wait_recv()
        for cp in first + passed:
            cp.wait_send()
        mine.wait()

    return pl.pallas_call(
        body, name=name,
        out_shape=jax.ShapeDtypeStruct((8 * m_per, n), blk.dtype),
        in_specs=[pl.BlockSpec(memory_space=pltpu.VMEM)],
        out_specs=pl.BlockSpec(memory_space=pltpu.VMEM),
        scratch_shapes=[pltpu.SemaphoreType.DMA((7,)), pltpu.SemaphoreType.DMA((7,)), pltpu.SemaphoreType.DMA],
        compiler_params=pltpu.CompilerParams(vmem_limit_bytes=V7X_VMEM_LIMIT),
    )(blk)


class _Sharded:
    def __init__(self, shard_shape, by_cols):
        r, c = shard_shape
        self.by_cols = by_cols
        self.full = (r, N_CHIPS * c) if by_cols else (N_CHIPS * r, c)
        self.shard = (r, c)
        self.half_rows = r // 2
        self.half = (r // 2, c)

    def shard_of(self, ref, k):
        r, c = self.shard
        return ref.at[:, pl.ds(k * c, c)] if self.by_cols else ref.at[pl.ds(k * r, r), :]

    def half_of(self, ref, k, hc):
        r, c = self.shard
        h = self.half_rows
        if self.by_cols:
            return ref.at[pl.ds(hc * h, h), pl.ds(k * c, c)]
        return ref.at[pl.ds(k * r + hc * h, h), :]

    def half_of_shard(self, ref, hc):
        return ref.at[pl.ds(hc * self.half_rows, self.half_rows), :]

    def part_of_halves(self, ref, k):
        r, c = self.shard
        h = self.half_rows
        return ref.at[:, pl.ds(k * c, c)] if self.by_cols else ref.at[pl.ds(k * h, h), :]


def _on_each_place(x, y, c, fn, by_chip=True, by_core=True):
    q = 2 * x + y
    for k in range(N_CHIPS if by_chip else 1):
        for cc in range(2 if by_core else 1):
            cond = None
            if by_chip:
                cond = q == k
            if by_core:
                cond = (c == cc) if cond is None else jnp.logical_and(cond, c == cc)
            pl.when(cond)(functools.partial(fn, k, cc))


def _chip_id(k, c):
    return (k // 2, k % 2, c)


def _handshake(peers):
    barrier = pltpu.get_barrier_semaphore()
    for peer in peers:
        pl.semaphore_signal(barrier, inc=1, device_id=peer, device_id_type=MESH)
    pl.semaphore_wait(barrier, len(peers))


def _on_sequencer(body, inputs, out_structs, n_copies, peers_of, name, collective_id, return_inputs=False):
    in_refs = [jax.new_ref(a, memory_space=pltpu.MemorySpace.HBM) for a in inputs]
    out_refs = [jax.empty_ref(s, memory_space=pltpu.MemorySpace.HBM) for s in out_structs]

    @pl.kernel(mesh=plsc.ScalarSubcoreMesh(axis_name="sequencer", num_cores=1), name=name,
               scratch_types=(pltpu.SemaphoreType.DMA((n_copies,)), pltpu.SemaphoreType.DMA((n_copies,))),
               compiler_params=pltpu.CompilerParams(collective_id=collective_id))
    def launch(send_sems, recv_sems):
        x, y, c = _place()
        _handshake(peers_of(x, y, c))
        body(in_refs, out_refs, send_sems, recv_sems, x, y, c)

    launch()
    return [r[...] for r in (in_refs if return_inputs else out_refs)]


def _sibling(x, y, c):
    return [(x, y, 1 - c)]


def _same_core_of_other_chips(x, y, c):
    return [(1 - x, y, c), (x, 1 - y, c), (1 - x, 1 - y, c)]


def _gather_weight(shard, g, name, collective_id):
    s_ref = jax.new_ref(shard, memory_space=pltpu.MemorySpace.HBM)
    f_ref = jax.empty_ref(jax.ShapeDtypeStruct(g.full, WIRE_DTYPE), memory_space=pltpu.MemorySpace.HBM)
    n_flips = len(FLIPS)

    @pl.kernel(mesh=plsc.ScalarSubcoreMesh(axis_name="sequencer", num_cores=1), name=name,
               scratch_types=(pltpu.SemaphoreType.DMA((2 * n_flips,)), pltpu.SemaphoreType.DMA((2 * n_flips,)),
                              pltpu.SemaphoreType.DMA),
               compiler_params=pltpu.CompilerParams(collective_id=collective_id))
    def launch(send_sems, recv_sems, local_sem):
        x, y, c = _place()
        _handshake([(x, y, 1 - c), (1 - x, y, c), (x, 1 - y, c), (1 - x, 1 - y, c)])

        def at_place(k, cc):
            def remote(slot, src, dst, to):
                return pltpu.make_async_remote_copy(src_ref=src, dst_ref=dst, send_sem=send_sems.at[slot],
                                                    recv_sem=recv_sems.at[slot], device_id=to, device_id_type=MESH)

            local = pltpu.make_async_copy(s_ref, g.shard_of(f_ref, k), local_sem)
            local.start()
            first, passed = [], []
            for j, flip in enumerate(FLIPS):
                cp = remote(j, g.half_of_shard(s_ref, cc), g.half_of(f_ref, k, cc), _chip_id(k ^ flip, cc))
                cp.start()
                first.append(cp)
            for j, flip in enumerate(FLIPS):
                landed = g.half_of(f_ref, k ^ flip, cc)
                remote(j, landed, landed, _chip_id(k, cc)).wait_recv()
                cp = remote(n_flips + j, landed, landed, _chip_id(k, 1 - cc))
                cp.start()
                passed.append(cp)
            for j, flip in enumerate(FLIPS):
                from_sibling = g.half_of(f_ref, k ^ flip, 1 - cc)
                remote(n_flips + j, from_sibling, from_sibling, _chip_id(k, cc)).wait_recv()
            for cp in first + passed:
                cp.wait_send()
            local.wait()

        _on_each_place(x, y, c, at_place)

    launch()
    return f_ref[...]


def _swap_core_halves(grads, geoms, name, collective_id):
    n_cp = sum(1 if g.by_cols else N_CHIPS for g in geoms)

    def body(g_refs, t_refs, send_sems, recv_sems, x, y, c):

        def at_place(_, cc):
            def pairs(hc):
                out = []
                for g, g_ref, t_ref in zip(geoms, g_refs, t_refs):
                    if g.by_cols:
                        out.append((g_ref.at[pl.ds(hc * g.half_rows, g.half_rows), :], t_ref))
                    else:
                        out += [(g.half_of(g_ref, k, hc), g.part_of_halves(t_ref, k)) for k in range(N_CHIPS)]
                return out

            sends = [pltpu.make_async_remote_copy(src_ref=src, dst_ref=dst, send_sem=send_sems.at[n],
                                                  recv_sem=recv_sems.at[n], device_id=(x, y, 1 - cc), device_id_type=MESH)
                     for n, (src, dst) in enumerate(pairs(1 - cc))]
            for cp in sends:
                cp.start()
            for n, (src, dst) in enumerate(pairs(cc)):
                pltpu.make_async_remote_copy(src_ref=src, dst_ref=dst, send_sem=send_sems.at[n], recv_sem=recv_sems.at[n],
                                             device_id=(x, y, cc), device_id_type=MESH).wait_recv()
            for cp in sends:
                cp.wait_send()

        _on_each_place(x, y, c, at_place, by_chip=False)

    return _on_sequencer(body, grads, [jax.ShapeDtypeStruct((g.full[0] // 2, g.full[1]), F32) for g in geoms],
                         n_cp, _sibling, name, collective_id)


def _scatter_chip_sums(sums, geoms, name, collective_id):
    def body(s_refs, r_refs, send_sems, recv_sems, x, y, c):

        def at_place(k, _):
            sends = []
            for i, (g, s_ref, r_ref) in enumerate(zip(geoms, s_refs, r_refs)):
                for j, flip in enumerate(FLIPS):
                    kk = k ^ flip
                    cp = pltpu.make_async_remote_copy(
                        src_ref=g.part_of_halves(s_ref, kk), dst_ref=r_ref.at[j], send_sem=send_sems.at[3 * i + j],
                        recv_sem=recv_sems.at[3 * i + j], device_id=(kk // 2, kk % 2, c), device_id_type=MESH)
                    cp.start()
                    sends.append(cp)
            for i, (g, s_ref, r_ref) in enumerate(zip(geoms, s_refs, r_refs)):
                for j in range(len(FLIPS)):
                    pltpu.make_async_remote_copy(
                        src_ref=g.part_of_halves(s_ref, k), dst_ref=r_ref.at[j], send_sem=send_sems.at[3 * i + j],
                        recv_sem=recv_sems.at[3 * i + j], device_id=(x, y, c), device_id_type=MESH).wait_recv()
            for cp in sends:
                cp.wait_send()

        _on_each_place(x, y, c, at_place, by_core=False)

    return _on_sequencer(body, sums, [jax.ShapeDtypeStruct((len(FLIPS),) + g.half, WIRE_DTYPE) for g in geoms],
                         len(FLIPS) * len(sums), _same_core_of_other_chips, name, collective_id)


def _share_reduced_halves(reduced, geoms, name, collective_id):
    def body(out_refs, _, send_sems, recv_sems, x, y, c):

        def at_place(_, cc):
            sends = []
            for i, (g, ref) in enumerate(zip(geoms, out_refs)):
                mine = g.half_of_shard(ref, cc)
                cp = pltpu.make_async_remote_copy(src_ref=mine, dst_ref=mine, send_sem=send_sems.at[i],
                                                  recv_sem=recv_sems.at[i], device_id=(x, y, 1 - cc), device_id_type=MESH)
                cp.start()
                sends.append(cp)
            for i, (g, ref) in enumerate(zip(geoms, out_refs)):
                theirs = g.half_of_shard(ref, 1 - cc)
                pltpu.make_async_remote_copy(src_ref=theirs, dst_ref=theirs, send_sem=send_sems.at[i],
                                             recv_sem=recv_sems.at[i], device_id=(x, y, cc), device_id_type=MESH).wait_recv()
            for cp in sends:
                cp.wait_send()

        _on_each_place(x, y, c, at_place, by_chip=False)

    return _on_sequencer(body, reduced, [], len(reduced), _sibling, name, collective_id, return_inputs=True)


def _chip_sum(place, grad, theirs, g, name):
    RH, C = theirs.shape
    h = g.half_rows
    tr = _tile(h, 256, 16)
    tc = _tile(C, 2048)
    per_half = h // tr

    if g.by_cols:
        grad_map = lambda i, j, p: (p[1] * per_half + i, j)
    else:
        grad_map = lambda i, j, p: ((i // per_half) * 2 * per_half + p[1] * per_half + i % per_half, j)

    def body(p_ref, a_ref, b_ref, o_ref):
        o_ref[...] = (a_ref[...] + b_ref[...]).astype(o_ref.dtype)

    return pl.pallas_call(
        body, name=name,
        grid_spec=pltpu.PrefetchScalarGridSpec(
            num_scalar_prefetch=1, grid=(RH // tr, C // tc),
            in_specs=[pl.BlockSpec((tr, tc), grad_map), pl.BlockSpec((tr, tc), lambda i, j, p: (i, j))],
            out_specs=pl.BlockSpec((tr, tc), lambda i, j, p: (i, j))),
        out_shape=jax.ShapeDtypeStruct((RH, C), WIRE_DTYPE),
        compiler_params=_params("parallel", "parallel"),
    )(place, grad, theirs)


def _reduce_half(place, grad, theirs, others, g, name):
    h, wc = g.half
    tr = _tile(h, 256, 16)
    per_half = h // tr
    if g.by_cols:
        tc = wc
        grad_map = lambda i, p: (p[1] * per_half + i, p[0])
        theirs_map = lambda i, p: (i, p[0])
    else:
        tc = wc
        grad_map = lambda i, p: (p[0] * 2 * per_half + p[1] * per_half + i, 0)
        theirs_map = lambda i, p: (p[0] * per_half + i, 0)

    def body(p_ref, a_ref, b_ref, o0_ref, o1_ref, o2_ref, out_ref):
        acc = a_ref[...] + b_ref[...]
        for o_ref in (o0_ref, o1_ref, o2_ref):
            acc = acc + o_ref[...].astype(F32)
        out_ref[...] = acc

    other_specs = [pl.BlockSpec((None, tr, tc), functools.partial(lambda i, p, j: (j, i, 0), j=j)) for j in range(len(FLIPS))]
    return pl.pallas_call(
        body, name=name,
        grid_spec=pltpu.PrefetchScalarGridSpec(
            num_scalar_prefetch=1, grid=(per_half,),
            in_specs=[pl.BlockSpec((tr, tc), grad_map), pl.BlockSpec((tr, tc), theirs_map)] + other_specs,
            out_specs=pl.BlockSpec((tr, tc), lambda i, p: (p[1] * per_half + i, 0))),
        out_shape=jax.ShapeDtypeStruct(g.shard, F32),
        compiler_params=_params("arbitrary"),
    )(place, grad, theirs, others, others, others)


SMALL = ("b_ada", "norm1_g", "v_norm_g", "w_spatial", "b_spatial", "out_norm_g", "norm2_g", "final_g")
BIG = ("w_in", "w_out", "w_gate", "w_up", "w_down")
BY_COLS = {"w_in": True, "w_out": False, "w_gate": True, "w_up": True, "w_down": False}
ORDER = ("w_ada", "b_ada", "norm1_g", "w_in", "v_norm_g", "w_spatial", "b_spatial", "out_norm_g", "w_out",
         "norm2_g", "w_gate", "w_up", "w_down", "final_g")


def _pack(parts):
    return jnp.concatenate([parts[n].reshape(-1) for n in SMALL]).reshape(-1, LANE)


def _unpack(slab, shapes):
    flat = slab.reshape(-1)
    out, at = {}, 0
    for n in SMALL:
        size = math.prod(shapes[n])
        out[n] = flat[at:at + size].reshape(shapes[n])
        at += size
    return out


def kernel(x, c, w_ada, b_ada, norm1_g, w_in, v_norm_g, w_spatial, b_spatial, out_norm_g, w_out, norm2_g, w_gate, w_up, w_down, final_g, loss_target, m_w_ada, m_b_ada, m_norm1_g, m_w_in, m_v_norm_g, m_w_spatial, m_b_spatial, m_out_norm_g, m_w_out, m_norm2_g, m_w_gate, m_w_up, m_w_down, m_final_g, v_w_ada, v_b_ada, v_norm1_g, v_w_in, v_v_norm_g, v_w_spatial, v_b_spatial, v_out_norm_g, v_w_out, v_norm2_g, v_w_gate, v_w_up, v_w_down, v_final_g):
    weights = dict(w_ada=w_ada, b_ada=b_ada, norm1_g=norm1_g, w_in=w_in, v_norm_g=v_norm_g, w_spatial=w_spatial,
                   b_spatial=b_spatial, out_norm_g=out_norm_g, w_out=w_out, norm2_g=norm2_g, w_gate=w_gate, w_up=w_up,
                   w_down=w_down, final_g=final_g)
    m_in = dict(w_ada=m_w_ada, b_ada=m_b_ada, norm1_g=m_norm1_g, w_in=m_w_in, v_norm_g=m_v_norm_g, w_spatial=m_w_spatial,
                b_spatial=m_b_spatial, out_norm_g=m_out_norm_g, w_out=m_w_out, norm2_g=m_norm2_g, w_gate=m_w_gate,
                w_up=m_w_up, w_down=m_w_down, final_g=m_final_g)
    v_in = dict(w_ada=v_w_ada, b_ada=v_b_ada, norm1_g=v_norm1_g, w_in=v_w_in, v_norm_g=v_v_norm_g, w_spatial=v_w_spatial,
                b_spatial=v_b_spatial, out_norm_g=v_out_norm_g, w_out=v_w_out, norm2_g=v_norm2_g, w_gate=v_w_gate,
                w_up=v_w_up, w_down=v_w_down, final_g=v_final_g)

    S, D = x.shape[1], x.shape[2]
    n_g = v_norm_g.shape[-1] // LANE
    n_h = (D - n_g * LANE) // LANE
    GW = n_g * LANE
    xi, yi, ci = _place()
    chip = 2 * xi + yi
    me = 4 * xi + 2 * yi + ci
    place = jnp.stack([chip, ci]).astype(jnp.int32)

    xs, target = x[0], loss_target[0]
    geoms = [_Sharded(weights[n].shape[1:], BY_COLS[n]) for n in BIG]

    full = {n: _gather_weight(_cast(weights[n][0], WIRE_DTYPE, "cast_" + n), g, "gather_" + n, 1 + i)
            for i, (n, g) in enumerate(zip(BIG, geoms))}

    c_pad = jnp.concatenate([c, jnp.zeros((7, D), F32)], axis=0)
    c_all = _allgather8(c_pad, "gather_c")[::8]
    n_ada = w_ada.shape[2]
    b_cols = lax.dynamic_slice(b_ada, (0, chip * n_ada), (1, n_ada))
    mod_parts = _allgather8(_mod_part(c_all, w_ada[0], b_cols, "mod_part"), "gather_mod")
    mod_all = mod_parts.reshape(N_CHIPS, 2, 8, n_ada)[:, 0].transpose(1, 0, 2).reshape(8, N_CHIPS * n_ada)
    mod = lax.dynamic_slice(mod_all, (me, 0), (1, 6 * D))
    shift1, scale1, gate1, shift2, scale2, gate2 = [mod[:, i * D:(i + 1) * D] for i in range(6)]

    b_t = b_spatial[0].T
    h1 = _norm_mod(xs, norm1_g, scale1, shift1, "norm1")
    proj, = _mm("nn", h1, full["w_in"], [F32], "proj")
    on_gm = _gmlp_fwd(proj, v_norm_g, w_spatial[0], b_t, out_norm_g, n_g, "gmlp_fwd")
    o_sb, on_sb, l_sum = _sb_fwd(proj, out_norm_g, n_g, n_h, "sb_fwd")
    o_n = jnp.concatenate([on_gm, on_sb], axis=1)
    attn, = _mm("nn", o_n, full["w_out"], [F32], "attn_out")
    x1, h2 = _residual_norm_mod(xs, attn, gate1, norm2_g, scale2, shift2, "norm2")
    a_g, a_u, f_in = _gate_up(h2, full["w_gate"], full["w_up"], "gate_up")
    f, = _mm("nn", f_in, full["w_down"], [F32], "down", tm=512)
    dx2, df, d_gate2, d_final_g, loss_part = _final_loss_bwd(x1, f, gate2, final_g.reshape(1, D), target, "final")
    loss = lax.psum(loss_part[0, 0], ("x", "y", "c"))

    grads = {}
    d_ag, d_au = _mm("nt", df, full["w_down"], [MXU_DTYPE, MXU_DTYPE], "d_ffn_in", extras=(a_g, a_u),
                     epilogue=_swiglu_bwd_epilogue)
    grads["w_down"], = _mm("tn", f_in, df, [F32], "d_w_down", tm=512, tn=1024)
    dh2_g, = _mm("nt", d_ag, full["w_gate"], [F32], "d_h2_gate", tm=512)
    dh2, = _mm("nt", d_au, full["w_up"], [F32], "d_h2", tm=512, extras=(dh2_g,), epilogue=_add_epilogue)
    grads["w_gate"], = _mm("tn", h2, d_ag, [F32], "d_w_gate")
    grads["w_up"], = _mm("tn", h2, d_au, [F32], "d_w_up")
    dx1, d_shift2, d_scale2, d_norm2_g, d_gate1, d_attn = _norm_mod_bwd(dh2, x1, dx2, norm2_g, scale2, "norm2_bwd",
                                                                        branch=attn, gate=gate1)
    d_on, = _mm("nt", d_attn, full["w_out"], [F32], "d_o")
    grads["w_out"], = _mm("tn", o_n, d_attn, [F32], "d_w_out")
    dp_gm, d_w_spatial, d_b_t, d_v_norm_g, d_og_gm = _gmlp_bwd(proj, d_on, v_norm_g, w_spatial[0], b_t, out_norm_g, n_g, "gmlp_bwd")
    dq, dk, dv, d_og_sb = _sb_bwd(proj, o_sb, l_sum, d_on, out_norm_g, n_g, n_h, "sb_bwd")
    dproj = jnp.concatenate([dp_gm, dq, dk, dv], axis=1)
    dh1, = _mm("nt", dproj, full["w_in"], [F32], "d_h1", tm=512)
    grads["w_in"], = _mm("tn", h1, dproj, [F32], "d_w_in")
    grad_x, d_shift1, d_scale1, d_norm1_g = _norm_mod_bwd(dh1, xs, dx1, norm1_g, scale1, "norm1_bwd")

    dmod = jnp.concatenate([d_shift1, d_scale1, d_gate1, d_shift2, d_scale2, d_gate2], axis=1)
    small_parts = dict(b_ada=dmod, norm1_g=d_norm1_g, v_norm_g=d_v_norm_g, w_spatial=d_w_spatial, b_spatial=d_b_t.T,
                       out_norm_g=jnp.concatenate([d_og_gm, d_og_sb], axis=1), norm2_g=d_norm2_g, final_g=d_final_g)
    slab = _pack(small_parts)
    rows = slab.shape[0]
    gathered = _allgather8(slab, "gather_small")
    small_shapes = {n: weights[n].shape for n in SMALL}
    small_sum = _sum_devices(gathered, 8, "sum_small")
    dmod_all = gathered.reshape(8, rows * LANE)[:, :6 * D]
    dmod_cols = lax.dynamic_slice(dmod_all, (0, chip * n_ada), (8, n_ada))

    geom_of = dict(zip(BIG, geoms))
    reduced = {}
    for gi, group in enumerate((("w_down",), ("w_gate", "w_up"), ("w_out",), ("w_in",))):
        gg = [geom_of[n] for n in group]
        tag = "_".join(group)
        theirs = _swap_core_halves([grads[n] for n in group], gg, "swap_" + tag, 6 + 3 * gi)
        sums = [_chip_sum(place, grads[n], t, g, "chip_sum_" + n) for n, t, g in zip(group, theirs, gg)]
        others = _scatter_chip_sums(sums, gg, "scatter_" + tag, 7 + 3 * gi)
        halves = [_reduce_half(place, grads[n], t, o, g, "reduce_" + n) for n, t, o, g in zip(group, theirs, others, gg)]
        reduced.update(zip(group, _share_reduced_halves(halves, gg, "share_" + tag, 8 + 3 * gi)))

    grad_out, delta, new_m, new_v = {}, {}, {}, {}
    for n in BIG:
        grad_out[n] = reduced[n][None]
        d, mo, vo = _adamw(weights[n][0], reduced[n], m_in[n][0], v_in[n][0], "adamw_" + n)
        delta[n], new_m[n], new_v[n] = d[None], mo[None], vo[None]
    g_ada, d, mo, vo = _adamw_ada(c_all, dmod_cols, w_ada[0], m_w_ada[0], v_w_ada[0], "adamw_w_ada")
    grad_out["w_ada"], delta["w_ada"], new_m["w_ada"], new_v["w_ada"] = g_ada[None], d[None], mo[None], vo[None]
    d, mo, vo = _adamw(_pack({n: weights[n] for n in SMALL}), small_sum, _pack({n: m_in[n] for n in SMALL}),
                       _pack({n: v_in[n] for n in SMALL}), "adamw_small")
    for dst, slab_out in ((grad_out, small_sum), (delta, d), (new_m, mo), (new_v, vo)):
        dst.update(_unpack(slab_out, small_shapes))

    return (loss, grad_x[None], *[grad_out[n] for n in ORDER], *[delta[n] for n in ORDER],
            *[new_m[n] for n in ORDER], *[new_v[n] for n in ORDER])
```

```python
import functools
import math

import jax
import jax.numpy as jnp
from jax import lax
from jax.experimental import pallas as pl
from jax.experimental.pallas import tpu as pltpu
from jax.experimental.pallas import tpu_sc as plsc

F32 = jnp.float32
MXU_DTYPE = jnp.bfloat16
WIRE_DTYPE = jnp.bfloat16
EPS = 1e-6
LANE = 128
V7X_VMEM_LIMIT = 56 * 1024 * 1024
MESH = pl.DeviceIdType.MESH
N_CHIPS = 4
FLIPS = (2, 1, 3)

ADAM_LR = 0.001
ADAM_B1 = 0.9
ADAM_B2 = 0.999
ADAM_EPS = 1e-08
ADAM_WD = 0.01
ADAM_STEP = 10


def _params(*semantics):
    return pltpu.CompilerParams(dimension_semantics=semantics or None, vmem_limit_bytes=V7X_VMEM_LIMIT)


def _tile(dim, pref, unit=LANE):
    best = None
    t = unit
    while t <= min(dim, pref):
        if dim % t == 0:
            best = t
        t += unit
    return best if best is not None else dim


def _then(first, second):
    return lax.optimization_barrier((first, second))[1]


def _sum0(v):
    return jnp.sum(v, axis=0, keepdims=True)


def _mean1(v):
    return jnp.mean(v, axis=-1, keepdims=True)


def _gelu(x):
    return 0.5 * x * (1.0 + lax.erf(x * (1.0 / math.sqrt(2.0))))


def _gelu_grad(x):
    cdf = 0.5 * (1.0 + lax.erf(x * (1.0 / math.sqrt(2.0))))
    return cdf + x * jnp.exp(-0.5 * x * x) * (1.0 / math.sqrt(2.0 * math.pi))


def _dot(a, b, dims):
    return lax.dot_general(a, b, (dims, ((), ())), preferred_element_type=F32)


NN = ((1,), (0,))
NT = ((1,), (1,))
TN = ((0,), (0,))


def _mm(kind, a, b, out_dtypes, name, tm=1024, tn=512, extras=(), epilogue=None):
    if kind == "nn":
        (M, K), N = a.shape, b.shape[1]
    elif kind == "nt":
        (M, K), N = a.shape, b.shape[0]
    else:
        (K, M), N = a.shape, b.shape[1]
    tm, tn = _tile(M, tm), _tile(N, tn)
    a_spec = pl.BlockSpec((K, tm), lambda i, j: (0, i)) if kind == "tn" else pl.BlockSpec((tm, K), lambda i, j: (i, 0))
    b_spec = pl.BlockSpec((tn, K), lambda i, j: (j, 0)) if kind == "nt" else pl.BlockSpec((K, tn), lambda i, j: (0, j))
    mn_spec = pl.BlockSpec((tm, tn), lambda i, j: (i, j))
    dims = {"nn": NN, "nt": NT, "tn": TN}[kind]
    n_extra = len(extras)

    def body(a_ref, b_ref, *rest):
        acc = _dot(a_ref[...], b_ref[...], dims)
        res = (acc,) if epilogue is None else epilogue(acc, *[e[...] for e in rest[:n_extra]])
        for o_ref, r in zip(rest[n_extra:], res):
            o_ref[...] = r.astype(o_ref.dtype)

    outs = pl.pallas_call(
        body, name=name, grid=(M // tm, N // tn),
        in_specs=[a_spec, b_spec] + [mn_spec] * n_extra,
        out_specs=[mn_spec] * len(out_dtypes),
        out_shape=[jax.ShapeDtypeStruct((M, N), d) for d in out_dtypes],
        compiler_params=_params("parallel", "arbitrary"),
    )(a, b, *extras)
    return outs


def _gate_up(h, wg, wu, name):
    (M, K), N = h.shape, wg.shape[1]
    tm, tn = _tile(M, 1024), _tile(N, 512)

    def body(h_ref, wg_ref, wu_ref, ag_ref, au_ref, f_ref):
        hv = h_ref[...]
        ag = _dot(hv, wg_ref[...], NN)
        au = _dot(hv, wu_ref[...], NN)
        ag_ref[...] = ag
        au_ref[...] = au
        f_ref[...] = (ag * jax.nn.sigmoid(ag) * au).astype(f_ref.dtype)

    w_spec = pl.BlockSpec((K, tn), lambda i, j: (0, j))
    mn_spec = pl.BlockSpec((tm, tn), lambda i, j: (i, j))
    return pl.pallas_call(
        body, name=name, grid=(M // tm, N // tn),
        in_specs=[pl.BlockSpec((tm, K), lambda i, j: (i, 0)), w_spec, w_spec],
        out_specs=[mn_spec] * 3,
        out_shape=[jax.ShapeDtypeStruct((M, N), F32), jax.ShapeDtypeStruct((M, N), F32),
                   jax.ShapeDtypeStruct((M, N), MXU_DTYPE)],
        compiler_params=_params("parallel", "arbitrary"),
    )(h, wg, wu)


def _swiglu_bwd_epilogue(dfin, ag, au):
    sg = jax.nn.sigmoid(ag)
    d_au = dfin * (ag * sg)
    d_ag = dfin * au * (sg * (1.0 + ag * (1.0 - sg)))
    return d_ag, d_au


def _add_epilogue(acc, other):
    return (acc + other,)


def _row_specs(ts, width):
    return pl.BlockSpec((ts, width), lambda i: (i, 0)), pl.BlockSpec((1, width), lambda i: (0, 0))


def _cast(a, dtype, name):
    R, C = a.shape
    tr = _tile(R, 512, 16)
    spec = pl.BlockSpec((tr, C), lambda i: (i, 0))

    def body(a_ref, o_ref):
        o_ref[...] = a_ref[...].astype(o_ref.dtype)

    return pl.pallas_call(body, name=name, grid=(R // tr,), in_specs=[spec], out_specs=spec,
                          out_shape=jax.ShapeDtypeStruct((R, C), dtype), compiler_params=_params("parallel"))(a)


def _norm_mod(x, g, scale, shift, name):
    S, D = x.shape
    ts = _tile(S, 256, 16)
    tile, vec = _row_specs(ts, D)

    def body(x_ref, g_ref, sc_ref, sh_ref, h_ref):
        xv = x_ref[...]
        r = lax.rsqrt(_mean1(xv * xv) + EPS)
        h_ref[...] = ((xv * r) * g_ref[...] * (1.0 + sc_ref[...]) + sh_ref[...]).astype(h_ref.dtype)

    return pl.pallas_call(body, name=name, grid=(S // ts,), in_specs=[tile, vec, vec, vec], out_specs=tile,
                          out_shape=jax.ShapeDtypeStruct((S, D), MXU_DTYPE), compiler_params=_params("parallel"))(x, g, scale, shift)


def _residual_norm_mod(x, attn, gate, g, scale, shift, name):
    S, D = x.shape
    ts = _tile(S, 256, 16)
    tile, vec = _row_specs(ts, D)

    def body(x_ref, a_ref, gate_ref, g_ref, sc_ref, sh_ref, x1_ref, h_ref):
        x1 = x_ref[...] + gate_ref[...] * a_ref[...]
        x1_ref[...] = x1
        r = lax.rsqrt(_mean1(x1 * x1) + EPS)
        h_ref[...] = ((x1 * r) * g_ref[...] * (1.0 + sc_ref[...]) + sh_ref[...]).astype(h_ref.dtype)

    return pl.pallas_call(body, name=name, grid=(S // ts,), in_specs=[tile, tile, vec, vec, vec, vec],
                          out_specs=[tile, tile],
                          out_shape=[jax.ShapeDtypeStruct((S, D), F32), jax.ShapeDtypeStruct((S, D), MXU_DTYPE)],
                          compiler_params=_params("parallel"))(x, attn, gate, g, scale, shift)


def _final_loss_bwd(x1, f, gate2, final_g, target, name):
    S, D = x1.shape
    ts = _tile(S, 256, 16)
    tile, vec = _row_specs(ts, D)
    loss_spec = pl.BlockSpec((1, LANE), lambda i: (0, 0))

    def body(x1_ref, f_ref, gate_ref, g_ref, t_ref, dx2_ref, df_ref, dgate_ref, dg_ref, loss_ref):
        @pl.when(pl.program_id(0) == 0)
        def _():
            dgate_ref[...] = jnp.zeros_like(dgate_ref)
            dg_ref[...] = jnp.zeros_like(dg_ref)
            loss_ref[...] = jnp.zeros_like(loss_ref)

        fv, gate, g = f_ref[...], gate_ref[...], g_ref[...]
        x2 = x1_ref[...] + gate * fv
        r = lax.rsqrt(_mean1(x2 * x2) + EPS)
        xn = x2 * r
        err = xn * g - t_ref[...]
        loss_ref[...] += jnp.broadcast_to(0.5 * _sum0(_mean1(err * err)), loss_ref.shape)
        dy = err * (1.0 / D)
        dg_ref[...] += _sum0(dy * xn)
        dxn = dy * g
        dx2 = r * (dxn - xn * _mean1(dxn * xn))
        dx2_ref[...] = dx2
        dgate_ref[...] += _sum0(dx2 * fv)
        df_ref[...] = (dx2 * gate).astype(df_ref.dtype)

    return pl.pallas_call(
        body, name=name, grid=(S // ts,), in_specs=[tile, tile, vec, vec, tile],
        out_specs=[tile, tile, vec, vec, loss_spec],
        out_shape=[jax.ShapeDtypeStruct((S, D), F32), jax.ShapeDtypeStruct((S, D), MXU_DTYPE),
                   jax.ShapeDtypeStruct((1, D), F32), jax.ShapeDtypeStruct((1, D), F32),
                   jax.ShapeDtypeStruct((1, LANE), F32)],
        compiler_params=_params("arbitrary"),
    )(x1, f, gate2, final_g, target)


def _norm_mod_bwd(dh, xin, dres, g, scale, name, branch=None, gate=None):
    S, D = xin.shape
    ts = _tile(S, 256, 16)
    tile, vec = _row_specs(ts, D)
    with_gate = branch is not None

    def body(*refs):
        if with_gate:
            dh_ref, x_ref, dres_ref, g_ref, sc_ref, br_ref, gate_ref, dx_ref, dshift_ref, dscale_ref, dg_ref, dgate_ref, dbr_ref = refs
            accs = (dshift_ref, dscale_ref, dg_ref, dgate_ref)
        else:
            dh_ref, x_ref, dres_ref, g_ref, sc_ref, dx_ref, dshift_ref, dscale_ref, dg_ref = refs
            accs = (dshift_ref, dscale_ref, dg_ref)

        @pl.when(pl.program_id(0) == 0)
        def _():
            for acc in accs:
                acc[...] = jnp.zeros_like(acc)

        dh_v, xv, g_v = dh_ref[...], x_ref[...], g_ref[...]
        one_sc = 1.0 + sc_ref[...]
        r = lax.rsqrt(_mean1(xv * xv) + EPS)
        xn = xv * r
        dshift_ref[...] += _sum0(dh_v)
        dscale_ref[...] += _sum0(dh_v * (xn * g_v))
        dg_ref[...] += _sum0(dh_v * one_sc * xn)
        dxn = dh_v * (g_v * one_sc)
        dx = dres_ref[...] + r * (dxn - xn * _mean1(dxn * xn))
        dx_ref[...] = dx
        if with_gate:
            dgate_ref[...] += _sum0(dx * br_ref[...])
            dbr_ref[...] = (dx * gate_ref[...]).astype(dbr_ref.dtype)

    ins = [dh, xin, dres, g, scale] + ([branch, gate] if with_gate else [])
    in_specs = [tile, tile, tile, vec, vec] + ([tile, vec] if with_gate else [])
    out_specs = [tile, vec, vec, vec] + ([vec, tile] if with_gate else [])
    out_shape = [jax.ShapeDtypeStruct((S, D), F32)] + [jax.ShapeDtypeStruct((1, D), F32)] * 3
    if with_gate:
        out_shape += [jax.ShapeDtypeStruct((1, D), F32), jax.ShapeDtypeStruct((S, D), MXU_DTYPE)]
    return pl.pallas_call(body, name=name, grid=(S // ts,), in_specs=in_specs, out_specs=out_specs,
                          out_shape=out_shape, compiler_params=_params("arbitrary"))(*ins)


def _causal_weights(ws_ref, wt_ref, n_g):
    row = lax.broadcasted_iota(jnp.int32, (LANE, LANE), 0)
    col = lax.broadcasted_iota(jnp.int32, (LANE, LANE), 1)
    for g in range(n_g):
        wt_ref[g] = jnp.where(col <= row, ws_ref[g], 0.0).astype(wt_ref.dtype)


def _group_layernorm(v):
    xc = v - _mean1(v)
    rstd = lax.rsqrt(_mean1(xc * xc) + EPS)
    return xc * rstd, rstd


def _gmlp_fwd(proj, v_gain, w_s, b_t, out_gain, n_g, name):
    S = proj.shape[0]
    GW = n_g * LANE

    def body(p_ref, vg_ref, ws_ref, bt_ref, og_ref, on_ref, wt_ref):
        @pl.when(pl.program_id(0) == 0)
        def _():
            _causal_weights(ws_ref, wt_ref, n_g)

        for g in range(n_g):
            cols = slice(g * LANE, (g + 1) * LANE)
            u = _gelu(p_ref[:, cols])
            v = _gelu(p_ref[:, GW + g * LANE:GW + (g + 1) * LANE])
            vhat, _ = _group_layernorm(v)
            vln = (vhat * vg_ref[:, cols]).astype(MXU_DTYPE)
            mixed = _dot(wt_ref[g], vln, NN) + bt_ref[:, g:g + 1]
            o = u * mixed
            r = lax.rsqrt(_mean1(o * o) + EPS)
            on_ref[:, cols] = (o * r * og_ref[:, cols]).astype(on_ref.dtype)

    return pl.pallas_call(
        body, name=name, grid=(S // LANE,),
        in_specs=[pl.BlockSpec((LANE, 2 * GW), lambda n: (n, 0)),
                  pl.BlockSpec((1, GW), lambda n: (0, 0)),
                  pl.BlockSpec((n_g, LANE, LANE), lambda n: (0, 0, 0)),
                  pl.BlockSpec((LANE, n_g), lambda n: (0, 0)),
                  pl.BlockSpec((1, GW), lambda n: (0, 0))],
        out_specs=pl.BlockSpec((LANE, GW), lambda n: (n, 0)),
        out_shape=jax.ShapeDtypeStruct((S, GW), MXU_DTYPE),
        scratch_shapes=[pltpu.VMEM((n_g, LANE, LANE), MXU_DTYPE)],
        compiler_params=_params("arbitrary"),
    )(proj, v_gain, w_s, b_t, out_gain)


def _gmlp_bwd(proj, d_on, v_gain, w_s, b_t, out_gain, n_g, name):
    S = proj.shape[0]
    GW = n_g * LANE

    def body(p_ref, dn_ref, vg_ref, ws_ref, bt_ref, og_ref, dp_ref, dws_ref, dbt_ref, dvg_ref, dog_ref, wt_ref):
        @pl.when(pl.program_id(0) == 0)
        def _():
            _causal_weights(ws_ref, wt_ref, n_g)
            dws_ref[...] = jnp.zeros_like(dws_ref)
            dbt_ref[...] = jnp.zeros_like(dbt_ref)
            dvg_ref[...] = jnp.zeros_like(dvg_ref)
            dog_ref[...] = jnp.zeros_like(dog_ref)

        row = lax.broadcasted_iota(jnp.int32, (LANE, LANE), 0)
        col = lax.broadcasted_iota(jnp.int32, (LANE, LANE), 1)
        for g in range(n_g):
            cols = slice(g * LANE, (g + 1) * LANE)
            vcols = slice(GW + g * LANE, GW + (g + 1) * LANE)
            pu, pv = p_ref[:, cols], p_ref[:, vcols]
            u, v = _gelu(pu), _gelu(pv)
            vhat, rstd = _group_layernorm(v)
            gain = vg_ref[:, cols]
            vln = (vhat * gain).astype(MXU_DTYPE)
            mixed = _dot(wt_ref[g], vln, NN) + bt_ref[:, g:g + 1]
            o = u * mixed
            r = lax.rsqrt(_mean1(o * o) + EPS)
            oh = o * r
            dn = dn_ref[:, cols]
            dog_ref[:, cols] += _sum0(dn * oh)
            dhn = dn * og_ref[:, cols]
            d_o = r * (dhn - oh * _mean1(dhn * oh))
            du = d_o * mixed
            dmix = d_o * u
            dbt_ref[:, g:g + 1] += jnp.sum(dmix, axis=1, keepdims=True)
            dmix_b = dmix.astype(MXU_DTYPE)
            dws_ref[g] += jnp.where(col <= row, _dot(dmix_b, vln, NT), 0.0)
            dvln = _dot(wt_ref[g], dmix_b, TN)
            dvg_ref[:, cols] += _sum0(dvln * vhat)
            dxh = dvln * gain
            dv = rstd * (dxh - _mean1(dxh) - vhat * _mean1(dxh * vhat))
            dp_ref[:, cols] = (du * _gelu_grad(pu)).astype(dp_ref.dtype)
            dp_ref[:, vcols] = (dv * _gelu_grad(pv)).astype(dp_ref.dtype)

    return pl.pallas_call(
        body, name=name, grid=(S // LANE,),
        in_specs=[pl.BlockSpec((LANE, 2 * GW), lambda n: (n, 0)),
                  pl.BlockSpec((LANE, GW), lambda n: (n, 0)),
                  pl.BlockSpec((1, GW), lambda n: (0, 0)),
                  pl.BlockSpec((n_g, LANE, LANE), lambda n: (0, 0, 0)),
                  pl.BlockSpec((LANE, n_g), lambda n: (0, 0)),
                  pl.BlockSpec((1, GW), lambda n: (0, 0))],
        out_specs=[pl.BlockSpec((LANE, 2 * GW), lambda n: (n, 0)),
                   pl.BlockSpec((n_g, LANE, LANE), lambda n: (0, 0, 0)),
                   pl.BlockSpec((LANE, n_g), lambda n: (0, 0)),
                   pl.BlockSpec((1, GW), lambda n: (0, 0)),
                   pl.BlockSpec((1, GW), lambda n: (0, 0))],
        out_shape=[jax.ShapeDtypeStruct((S, 2 * GW), MXU_DTYPE),
                   jax.ShapeDtypeStruct((n_g, LANE, LANE), F32),
                   jax.ShapeDtypeStruct((LANE, n_g), F32),
                   jax.ShapeDtypeStruct((1, GW), F32),
                   jax.ShapeDtypeStruct((1, GW), F32)],
        scratch_shapes=[pltpu.VMEM((n_g, LANE, LANE), MXU_DTYPE)],
        compiler_params=_params("arbitrary"),
    )(proj, d_on, v_gain, w_s, b_t, out_gain)


def _tri_sum(v, tri):
    hi = v.astype(MXU_DTYPE)
    lo = (v - hi.astype(F32)).astype(MXU_DTYPE)
    return _dot(hi, tri, NN) + _dot(lo, tri, NN)


def _log_sigmoids(z):
    sp = jnp.log1p(jnp.exp(-jnp.abs(z)))
    return jnp.minimum(z, 0.0) - sp, jnp.minimum(-z, 0.0) - sp


def _rows(i, size):
    return pl.ds(pl.multiple_of(i * size, size), size)


SB_QUERY_TILE = 512
SB_KEY_TILE = 256


def _sb_tiles(S):
    tq = _tile(S, SB_QUERY_TILE)
    tk = _tile(tq, SB_KEY_TILE)
    return tq, tk, S // tq, tq // tk


def _triangle(n, keep):
    row = lax.broadcasted_iota(jnp.int32, (n, n), 0)
    col = lax.broadcasted_iota(jnp.int32, (n, n), 1)
    return jnp.where(keep(row, col), 1.0, 0.0).astype(MXU_DTYPE)


def _strictly_before(tq, tk, key_offset):
    row = lax.broadcasted_iota(jnp.int32, (tq, tk), 0)
    col = lax.broadcasted_iota(jnp.int32, (tq, tk), 1)
    return col + key_offset < row


def _sb_specs(S, n_g, n_h):
    base = 2 * n_g
    q_spec = pl.BlockSpec((S, LANE), lambda h: (0, base + h))
    k_spec = pl.BlockSpec((S, LANE), lambda h: (0, base + n_h + h))
    v_spec = pl.BlockSpec((S, LANE), lambda h: (0, base + 2 * n_h + h))
    gain_spec = pl.BlockSpec((1, LANE), lambda h: (0, n_g + h))
    head_spec = pl.BlockSpec((S, LANE), lambda h: (0, h))
    return q_spec, k_spec, v_spec, gain_spec, head_spec


def _sb_fwd(proj, out_gain, n_g, n_h, name):
    S = proj.shape[0]
    TQ, TK, NQ, KPQ = _sb_tiles(S)
    scale = LANE ** -0.5
    q_spec, k_spec, v_spec, gain_spec, head_spec = _sb_specs(S, n_g, n_h)

    def body(q_ref, k_ref, v_ref, og_ref, o_ref, on_ref, ls_ref, qb, kb, vb):
        qb[...] = q_ref[...].astype(MXU_DTYPE)
        kb[...] = k_ref[...].astype(MXU_DTYPE)
        vb[...] = v_ref[...].astype(MXU_DTYPE)
        after = _triangle(TK, lambda r, c: r > c)

        def block(qi, j, ctail, acc, key_offset):
            z = _dot(qi, kb[_rows(j, TK), :], NT) * scale
            lb, l1m = _log_sigmoids(z)
            if key_offset is not None:
                strict = _strictly_before(TQ, TK, key_offset)
                l1m = jnp.where(strict, l1m, 0.0)
            a = jnp.exp(lb + ctail + _tri_sum(l1m, after))
            if key_offset is not None:
                a = jnp.where(strict, a, 0.0)
            acc = acc + _dot(a.astype(MXU_DTYPE), vb[_rows(j, TK), :], NN)
            return ctail + jnp.sum(l1m, axis=1, keepdims=True), acc

        def q_loop(i, carry):
            qi = qb[_rows(i, TQ), :]
            state = (jnp.zeros((TQ, 1), F32), jnp.zeros((TQ, LANE), F32))
            for d in reversed(range(KPQ)):
                state = block(qi, i * KPQ + d, state[0], state[1], d * TK)
            ctail, acc = lax.fori_loop(
                0, i * KPQ, lambda jj, st: block(qi, i * KPQ - 1 - jj, st[0], st[1], None), state)
            ls_ref[_rows(i, TQ), :] = jnp.broadcast_to(ctail, (TQ, LANE))
            o_ref[_rows(i, TQ), :] = acc
            r = lax.rsqrt(_mean1(acc * acc) + EPS)
            on_ref[_rows(i, TQ), :] = (acc * r * og_ref[...]).astype(on_ref.dtype)
            return carry

        lax.fori_loop(0, NQ, q_loop, 0)

    return pl.pallas_call(
        body, name=name, grid=(n_h,),
        in_specs=[q_spec, k_spec, v_spec, gain_spec],
        out_specs=[head_spec, head_spec, head_spec],
        out_shape=[jax.ShapeDtypeStruct((S, n_h * LANE), F32), jax.ShapeDtypeStruct((S, n_h * LANE), MXU_DTYPE),
                   jax.ShapeDtypeStruct((S, n_h * LANE), F32)],
        scratch_shapes=[pltpu.VMEM((S, LANE), MXU_DTYPE)] * 3,
        compiler_params=_params("parallel"),
    )(proj, proj, proj, out_gain)


def _sb_bwd(proj, o_sb, l_sum, d_on, out_gain, n_g, n_h, name):
    S = proj.shape[0]
    TQ, TK, NQ, KPQ = _sb_tiles(S)
    scale = LANE ** -0.5
    q_spec, k_spec, v_spec, gain_spec, head_spec = _sb_specs(S, n_g, n_h)
    dn_spec = pl.BlockSpec((S, LANE), lambda h: (0, n_g + h))
    dgain_spec = pl.BlockSpec((1, LANE), lambda h: (0, h))

    def body(q_ref, k_ref, v_ref, o_ref, ls_ref, dn_ref, og_ref, dq_ref, dk_ref, dv_ref, dog_ref,
             qb, kb, vb, dob, dk_acc, dv_acc):
        qb[...] = q_ref[...].astype(MXU_DTYPE)
        kb[...] = k_ref[...].astype(MXU_DTYPE)
        vb[...] = v_ref[...].astype(MXU_DTYPE)
        o, dn = o_ref[...], dn_ref[...]
        r = lax.rsqrt(_mean1(o * o) + EPS)
        oh = o * r
        dog_ref[...] = _sum0(dn * oh)
        dhn = dn * og_ref[...]
        dob[...] = (r * (dhn - oh * _mean1(dhn * oh))).astype(MXU_DTYPE)
        dk_acc[...] = jnp.zeros_like(dk_acc)
        dv_acc[...] = jnp.zeros_like(dv_acc)

        up_to = _triangle(TK, lambda r, c: r <= c)
        before = _triangle(TK, lambda r, c: r < c)

        def block(qi, doi, ltot, j, cl, cdl, dq, key_offset):
            kj, vj = kb[_rows(j, TK), :], vb[_rows(j, TK), :]
            z = _dot(qi, kj, NT) * scale
            lb, l1m_all = _log_sigmoids(z)
            l1m = l1m_all
            if key_offset is not None:
                strict = _strictly_before(TQ, TK, key_offset)
                l1m = jnp.where(strict, l1m_all, 0.0)
            a = jnp.exp(lb + (ltot - (cl + _tri_sum(l1m, up_to))))
            if key_offset is not None:
                a = jnp.where(strict, a, 0.0)
            dl = _dot(doi, vj, NT) * a
            d_l1m = cdl + _tri_sum(dl, before)
            dz = dl * jnp.exp(l1m_all) - jnp.exp(lb) * d_l1m
            if key_offset is not None:
                dz = jnp.where(strict, dz, 0.0)
            dzs = (dz * scale).astype(MXU_DTYPE)
            dq = dq + _dot(dzs, kj, NN)
            dk_acc[_rows(j, TK), :] += _dot(dzs, qi, TN)
            dv_acc[_rows(j, TK), :] += _dot(a.astype(MXU_DTYPE), doi, TN)
            return (cl + jnp.sum(l1m, axis=1, keepdims=True), cdl + jnp.sum(dl, axis=1, keepdims=True), dq)

        def q_loop(i, carry):
            qi, doi = qb[_rows(i, TQ), :], dob[_rows(i, TQ), :]
            ltot = ls_ref[_rows(i, TQ), :][:, :1]
            zero_col = jnp.zeros((TQ, 1), F32)
            state = lax.fori_loop(
                0, i * KPQ, lambda j, st: block(qi, doi, ltot, j, st[0], st[1], st[2], None),
                (zero_col, zero_col, jnp.zeros((TQ, LANE), F32)))
            for d in range(KPQ):
                state = block(qi, doi, ltot, i * KPQ + d, state[0], state[1], state[2], d * TK)
            dq_ref[_rows(i, TQ), :] = state[2].astype(dq_ref.dtype)
            return carry

        lax.fori_loop(0, NQ, q_loop, 0)
        dk_ref[...] = dk_acc[...].astype(dk_ref.dtype)
        dv_ref[...] = dv_acc[...].astype(dv_ref.dtype)

    W = n_h * LANE
    return pl.pallas_call(
        body, name=name, grid=(n_h,),
        in_specs=[q_spec, k_spec, v_spec, head_spec, head_spec, dn_spec, gain_spec],
        out_specs=[head_spec, head_spec, head_spec, dgain_spec],
        out_shape=[jax.ShapeDtypeStruct((S, W), MXU_DTYPE)] * 3 + [jax.ShapeDtypeStruct((1, W), F32)],
        scratch_shapes=[pltpu.VMEM((S, LANE), MXU_DTYPE)] * 4 + [pltpu.VMEM((S, LANE), F32)] * 2,
        compiler_params=_params("parallel"),
    )(proj, proj, proj, o_sb, l_sum, d_on, out_gain)


def _mod_part(c_all, w_ada, b_ada_cols, name):
    B, K = c_all.shape
    N = w_ada.shape[1]
    tn = _tile(N, 512)

    def body(c_ref, w_ref, b_ref, o_ref):
        cv = c_ref[...]
        ca = (cv * jax.nn.sigmoid(cv)).astype(MXU_DTYPE)
        o_ref[...] = _dot(ca, w_ref[...].astype(MXU_DTYPE), NN) + b_ref[...]

    return pl.pallas_call(
        body, name=name, grid=(N // tn,),
        in_specs=[pl.BlockSpec((B, K), lambda j: (0, 0)), pl.BlockSpec((K, tn), lambda j: (0, j)),
                  pl.BlockSpec((1, tn), lambda j: (0, j))],
        out_specs=pl.BlockSpec((B, tn), lambda j: (0, j)),
        out_shape=jax.ShapeDtypeStruct((B, N), F32), compiler_params=_params("parallel"))(c_all, w_ada, b_ada_cols)


def _adamw_math(w, g, m, v):
    m = ADAM_B1 * m + (1.0 - ADAM_B1) * g
    v = ADAM_B2 * v + (1.0 - ADAM_B2) * (g * g)
    m_hat = m / (1.0 - ADAM_B1 ** ADAM_STEP)
    v_hat = v / (1.0 - ADAM_B2 ** ADAM_STEP)
    delta = -ADAM_LR * (m_hat / (jnp.sqrt(v_hat) + ADAM_EPS) + ADAM_WD * w)
    return delta, m, v


def _adamw(w, g, m, v, name):
    R, C = w.shape
    tr = _tile(R, max(8, (1 << 19) // C), 8)
    spec = pl.BlockSpec((tr, C), lambda i: (i, 0))

    def body(w_ref, g_ref, m_ref, v_ref, d_ref, mo_ref, vo_ref):
        d_ref[...], mo_ref[...], vo_ref[...] = _adamw_math(w_ref[...], g_ref[...], m_ref[...], v_ref[...])

    return pl.pallas_call(body, name=name, grid=(R // tr,), in_specs=[spec] * 4, out_specs=[spec] * 3,
                          out_shape=[jax.ShapeDtypeStruct((R, C), F32)] * 3, compiler_params=_params("parallel"))(w, g, m, v)


def _adamw_ada(c_all, dmod_cols, w, m, v, name):
    K, N = w.shape
    B = c_all.shape[0]
    tk, tn = _tile(K, 512), _tile(N, 1024)
    spec = pl.BlockSpec((tk, tn), lambda i, j: (i, j))

    def body(c_ref, dm_ref, w_ref, m_ref, v_ref, g_ref, d_ref, mo_ref, vo_ref):
        cv = c_ref[...]
        ca = (cv * jax.nn.sigmoid(cv)).astype(MXU_DTYPE)
        g = _dot(ca, dm_ref[...].astype(MXU_DTYPE), TN)
        g_ref[...] = g
        d_ref[...], mo_ref[...], vo_ref[...] = _adamw_math(w_ref[...], g, m_ref[...], v_ref[...])

    return pl.pallas_call(
        body, name=name, grid=(K // tk, N // tn),
        in_specs=[pl.BlockSpec((B, tk), lambda i, j: (0, i)), pl.BlockSpec((B, tn), lambda i, j: (0, j)), spec, spec, spec],
        out_specs=[spec] * 4, out_shape=[jax.ShapeDtypeStruct((K, N), F32)] * 4,
        compiler_params=_params("parallel", "parallel"))(c_all, dmod_cols, w, m, v)


def _sum_devices(gathered, n_dev, name):
    R = gathered.shape[0] // n_dev
    C = gathered.shape[1]
    tr = _tile(R, 512, 8)
    n_blk = R // tr

    def body(*refs):
        acc = refs[0][...]
        for r in refs[1:n_dev]:
            acc = acc + r[...]
        refs[n_dev][...] = acc

    in_specs = [pl.BlockSpec((tr, C), functools.partial(lambda i, d: (d * n_blk + i, 0), d=d)) for d in range(n_dev)]
    return pl.pallas_call(body, name=name, grid=(n_blk,), in_specs=in_specs,
                          out_specs=pl.BlockSpec((tr, C), lambda i: (i, 0)),
                          out_shape=jax.ShapeDtypeStruct((R, C), F32), compiler_params=_params("parallel"))(*([gathered] * n_dev))


def _place():
    x, y, c = lax.axis_index("x"), lax.axis_index("y"), lax.axis_index("c")
    return x, y, c


def _allgather8(blk, name):
    m_per, n = blk.shape

    def body(x_ref, out_ref, send_sems, recv_sems, local_sem):
        x, y, c = _place()
        me, sibling = (x, y, c), (x, y, 1 - c)
        chips = [(1 - x, y), (x, 1 - y), (1 - x, 1 - y)]

        def rows(px, py, pc):
            return out_ref.at[pl.ds((4 * px + 2 * py + pc) * m_per, m_per), :]

        def copy(k, block, to, src=None):
            return pltpu.make_async_remote_copy(
                src_ref=rows(*block) if src is None else src, dst_ref=rows(*block),
                send_sem=send_sems.at[k], recv_sem=recv_sems.at[k], device_id=to, device_id_type=MESH)

        mine = pltpu.make_async_copy(x_ref, rows(*me), local_sem)
        mine.start()
        first = [copy(0, me, sibling, src=x_ref)]
        first += [copy(1 + j, me, (*chip, c), src=x_ref) for j, chip in enumerate(chips)]
        for cp in first:
            cp.start()
        passed = [copy(4 + j, (*chip, c), sibling) for j, chip in enumerate(chips)]
        for j, chip in enumerate(chips):
            copy(1 + j, (*chip, c), me).wait_recv()
            passed[j].start()
        copy(0, sibling, me).wait_recv()
        for j, chip in enumerate(chips):
            copy(4 + j, (*chip, 1 - c), me).wait_recv()
        for cp in first + passed:
            cp.wait_send()
        mine.wait()

    return pl.pallas_call(
        body, name=name,
        out_shape=jax.ShapeDtypeStruct((8 * m_per, n), blk.dtype),
        in_specs=[pl.BlockSpec(memory_space=pltpu.VMEM)],
        out_specs=pl.BlockSpec(memory_space=pltpu.VMEM),
        scratch_shapes=[pltpu.SemaphoreType.DMA((7,)), pltpu.SemaphoreType.DMA((7,)), pltpu.SemaphoreType.DMA],
        compiler_params=pltpu.CompilerParams(vmem_limit_bytes=V7X_VMEM_LIMIT),
    )(blk)


class _Sharded:
    def __init__(self, shard_shape, by_cols):
        r, c = shard_shape
        self.by_cols = by_cols
        self.full = (r, N_CHIPS * c) if by_cols else (N_CHIPS * r, c)
        self.shard = (r, c)
        self.half_rows = r // 2
        self.half = (r // 2, c)

    def shard_of(self, ref, k):
        r, c = self.shard
        return ref.at[:, pl.ds(k * c, c)] if self.by_cols else ref.at[pl.ds(k * r, r), :]

    def half_of(self, ref, k, hc):
        r, c = self.shard
        h = self.half_rows
        if self.by_cols:
            return ref.at[pl.ds(hc * h, h), pl.ds(k * c, c)]
        return ref.at[pl.ds(k * r + hc * h, h), :]

    def half_of_shard(self, ref, hc):
        return ref.at[pl.ds(hc * self.half_rows, self.half_rows), :]

    def part_of_halves(self, ref, k):
        r, c = self.shard
        h = self.half_rows
        return ref.at[:, pl.ds(k * c, c)] if self.by_cols else ref.at[pl.ds(k * h, h), :]


def _on_each_place(x, y, c, fn, by_chip=True, by_core=True):
    q = 2 * x + y
    for k in range(N_CHIPS if by_chip else 1):
        for cc in range(2 if by_core else 1):
            cond = None
            if by_chip:
                cond = q == k
            if by_core:
                cond = (c == cc) if cond is None else jnp.logical_and(cond, c == cc)
            pl.when(cond)(functools.partial(fn, k, cc))


def _chip_id(k, c):
    return (k // 2, k % 2, c)


def _handshake(peers):
    barrier = pltpu.get_barrier_semaphore()
    for peer in peers:
        pl.semaphore_signal(barrier, inc=1, device_id=peer, device_id_type=MESH)
    pl.semaphore_wait(barrier, len(peers))


def _on_sequencer(body, inputs, out_structs, n_copies, peers_of, name, collective_id, return_inputs=False):
    in_refs = [jax.new_ref(a, memory_space=pltpu.MemorySpace.HBM) for a in inputs]
    out_refs = [jax.empty_ref(s, memory_space=pltpu.MemorySpace.HBM) for s in out_structs]

    @pl.kernel(mesh=plsc.ScalarSubcoreMesh(axis_name="sequencer", num_cores=1), name=name,
               scratch_types=(pltpu.SemaphoreType.DMA((n_copies,)), pltpu.SemaphoreType.DMA((n_copies,))),
               compiler_params=pltpu.CompilerParams(collective_id=collective_id))
    def launch(send_sems, recv_sems):
        x, y, c = _place()
        _handshake(peers_of(x, y, c))
        body(in_refs, out_refs, send_sems, recv_sems, x, y, c)

    launch()
    return [r[...] for r in (in_refs if return_inputs else out_refs)]


def _sibling(x, y, c):
    return [(x, y, 1 - c)]


def _same_core_of_other_chips(x, y, c):
    return [(1 - x, y, c), (x, 1 - y, c), (1 - x, 1 - y, c)]


def _gather_weight(shard, g, name, collective_id):
    s_ref = jax.new_ref(shard, memory_space=pltpu.MemorySpace.HBM)
    f_ref = jax.empty_ref(jax.ShapeDtypeStruct(g.full, WIRE_DTYPE), memory_space=pltpu.MemorySpace.HBM)
    n_flips = len(FLIPS)

    @pl.kernel(mesh=plsc.ScalarSubcoreMesh(axis_name="sequencer", num_cores=1), name=name,
               scratch_types=(pltpu.SemaphoreType.DMA((2 * n_flips,)), pltpu.SemaphoreType.DMA((2 * n_flips,)),
                              pltpu.SemaphoreType.DMA),
               compiler_params=pltpu.CompilerParams(collective_id=collective_id))
    def launch(send_sems, recv_sems, local_sem):
        x, y, c = _place()
        _handshake([(x, y, 1 - c), (1 - x, y, c), (x, 1 - y, c), (1 - x, 1 - y, c)])

        def at_place(k, cc):
            def remote(slot, src, dst, to):
                return pltpu.make_async_remote_copy(src_ref=src, dst_ref=dst, send_sem=send_sems.at[slot],
                                                    recv_sem=recv_sems.at[slot], device_id=to, device_id_type=MESH)

            local = pltpu.make_async_copy(s_ref, g.shard_of(f_ref, k), local_sem)
            local.start()
            first, passed = [], []
            for j, flip in enumerate(FLIPS):
                cp = remote(j, g.half_of_shard(s_ref, cc), g.half_of(f_ref, k, cc), _chip_id(k ^ flip, cc))
                cp.start()
                first.append(cp)
            for j, flip in enumerate(FLIPS):
                landed = g.half_of(f_ref, k ^ flip, cc)
                remote(j, landed, landed, _chip_id(k, cc)).wait_recv()
                cp = remote(n_flips + j, landed, landed, _chip_id(k, 1 - cc))
                cp.start()
                passed.append(cp)
            for j, flip in enumerate(FLIPS):
                from_sibling = g.half_of(f_ref, k ^ flip, 1 - cc)
                remote(n_flips + j, from_sibling, from_sibling, _chip_id(k, cc)).wait_recv()
            for cp in first + passed:
                cp.wait_send()
            local.wait()

        _on_each_place(x, y, c, at_place)

    launch()
    return f_ref[...]


def _swap_core_halves(grads, geoms, name, collective_id):
    n_cp = sum(1 if g.by_cols else N_CHIPS for g in geoms)

    def body(g_refs, t_refs, send_sems, recv_sems, x, y, c):

        def at_place(_, cc):
            def pairs(hc):
                out = []
                for g, g_ref, t_ref in zip(geoms, g_refs, t_refs):
                    if g.by_cols:
                        out.append((g_ref.at[pl.ds(hc * g.half_rows, g.half_rows), :], t_ref))
                    else:
                        out += [(g.half_of(g_ref, k, hc), g.part_of_halves(t_ref, k)) for k in range(N_CHIPS)]
                return out

            sends = [pltpu.make_async_remote_copy(src_ref=src, dst_ref=dst, send_sem=send_sems.at[n],
                                                  recv_sem=recv_sems.at[n], device_id=(x, y, 1 - cc), device_id_type=MESH)
                     for n, (src, dst) in enumerate(pairs(1 - cc))]
            for cp in sends:
                cp.start()
            for n, (src, dst) in enumerate(pairs(cc)):
                pltpu.make_async_remote_copy(src_ref=src, dst_ref=dst, send_sem=send_sems.at[n], recv_sem=recv_sems.at[n],
                                             device_id=(x, y, cc), device_id_type=MESH).wait_recv()
            for cp in sends:
                cp.wait_send()

        _on_each_place(x, y, c, at_place, by_chip=False)

    return _on_sequencer(body, grads, [jax.ShapeDtypeStruct((g.full[0] // 2, g.full[1]), F32) for g in geoms],
                         n_cp, _sibling, name, collective_id)


def _scatter_chip_sums(sums, geoms, name, collective_id):
    def body(s_refs, r_refs, send_sems, recv_sems, x, y, c):

        def at_place(k, _):
            sends = []
            for i, (g, s_ref, r_ref) in enumerate(zip(geoms, s_refs, r_refs)):
                for j, flip in enumerate(FLIPS):
                    kk = k ^ flip
                    cp = pltpu.make_async_remote_copy(
                        src_ref=g.part_of_halves(s_ref, kk), dst_ref=r_ref.at[j], send_sem=send_sems.at[3 * i + j],
                        recv_sem=recv_sems.at[3 * i + j], device_id=(kk // 2, kk % 2, c), device_id_type=MESH)
                    cp.start()
                    sends.append(cp)
            for i, (g, s_ref, r_ref) in enumerate(zip(geoms, s_refs, r_refs)):
                for j in range(len(FLIPS)):
                    pltpu.make_async_remote_copy(
                        src_ref=g.part_of_halves(s_ref, k), dst_ref=r_ref.at[j], send_sem=send_sems.at[3 * i + j],
                        recv_sem=recv_sems.at[3 * i + j], device_id=(x, y, c), device_id_type=MESH).wait_recv()
            for cp in sends:
                cp.wait_send()

        _on_each_place(x, y, c, at_place, by_core=False)

    return _on_sequencer(body, sums, [jax.ShapeDtypeStruct((len(FLIPS),) + g.half, WIRE_DTYPE) for g in geoms],
                         len(FLIPS) * len(sums), _same_core_of_other_chips, name, collective_id)


def _share_reduced_halves(reduced, geoms, name, collective_id):
    def body(out_refs, _, send_sems, recv_sems, x, y, c):

        def at_place(_, cc):
            sends = []
            for i, (g, ref) in enumerate(zip(geoms, out_refs)):
                mine = g.half_of_shard(ref, cc)
                cp = pltpu.make_async_remote_copy(src_ref=mine, dst_ref=mine, send_sem=send_sems.at[i],
                                                  recv_sem=recv_sems.at[i], device_id=(x, y, 1 - cc), device_id_type=MESH)
                cp.start()
                sends.append(cp)
            for i, (g, ref) in enumerate(zip(geoms, out_refs)):
                theirs = g.half_of_shard(ref, 1 - cc)
                pltpu.make_async_remote_copy(src_ref=theirs, dst_ref=theirs, send_sem=send_sems.at[i],
                                             recv_sem=recv_sems.at[i], device_id=(x, y, cc), device_id_type=MESH).wait_recv()
            for cp in sends:
                cp.wait_send()

        _on_each_place(x, y, c, at_place, by_chip=False)

    return _on_sequencer(body, reduced, [], len(reduced), _sibling, name, collective_id, return_inputs=True)


def _chip_sum(place, grad, theirs, g, name):
    RH, C = theirs.shape
    h = g.half_rows
    tr = _tile(h, 256, 16)
    tc = _tile(C, 2048)
    per_half = h // tr

    if g.by_cols:
        grad_map = lambda i, j, p: (p[1] * per_half + i, j)
    else:
        grad_map = lambda i, j, p: ((i // per_half) * 2 * per_half + p[1] * per_half + i % per_half, j)

    def body(p_ref, a_ref, b_ref, o_ref):
        o_ref[...] = (a_ref[...] + b_ref[...]).astype(o_ref.dtype)

    return pl.pallas_call(
        body, name=name,
        grid_spec=pltpu.PrefetchScalarGridSpec(
            num_scalar_prefetch=1, grid=(RH // tr, C // tc),
            in_specs=[pl.BlockSpec((tr, tc), grad_map), pl.BlockSpec((tr, tc), lambda i, j, p: (i, j))],
            out_specs=pl.BlockSpec((tr, tc), lambda i, j, p: (i, j))),
        out_shape=jax.ShapeDtypeStruct((RH, C), WIRE_DTYPE),
        compiler_params=_params("parallel", "parallel"),
    )(place, grad, theirs)


def _reduce_half(place, grad, theirs, others, g, name):
    h, wc = g.half
    tr = _tile(h, 256, 16)
    per_half = h // tr
    if g.by_cols:
        tc = wc
        grad_map = lambda i, p: (p[1] * per_half + i, p[0])
        theirs_map = lambda i, p: (i, p[0])
    else:
        tc = wc
        grad_map = lambda i, p: (p[0] * 2 * per_half + p[1] * per_half + i, 0)
        theirs_map = lambda i, p: (p[0] * per_half + i, 0)

    def body(p_ref, a_ref, b_ref, o0_ref, o1_ref, o2_ref, out_ref):
        acc = a_ref[...] + b_ref[...]
        for o_ref in (o0_ref, o1_ref, o2_ref):
            acc = acc + o_ref[...].astype(F32)
        out_ref[...] = acc

    other_specs = [pl.BlockSpec((None, tr, tc), functools.partial(lambda i, p, j: (j, i, 0), j=j)) for j in range(len(FLIPS))]
    return pl.pallas_call(
        body, name=name,
        grid_spec=pltpu.PrefetchScalarGridSpec(
            num_scalar_prefetch=1, grid=(per_half,),
            in_specs=[pl.BlockSpec((tr, tc), grad_map), pl.BlockSpec((tr, tc), theirs_map)] + other_specs,
            out_specs=pl.BlockSpec((tr, tc), lambda i, p: (p[1] * per_half + i, 0))),
        out_shape=jax.ShapeDtypeStruct(g.shard, F32),
        compiler_params=_params("arbitrary"),
    )(place, grad, theirs, others, others, others)


SMALL = ("b_ada", "norm1_g", "v_norm_g", "w_spatial", "b_spatial", "out_norm_g", "norm2_g", "final_g")
BIG = ("w_in", "w_out", "w_gate", "w_up", "w_down")
BY_COLS = {"w_in": True, "w_out": False, "w_gate": True, "w_up": True, "w_down": False}
ORDER = ("w_ada", "b_ada", "norm1_g", "w_in", "v_norm_g", "w_spatial", "b_spatial", "out_norm_g", "w_out",
         "norm2_g", "w_gate", "w_up", "w_down", "final_g")


def _pack(parts):
    return jnp.concatenate([parts[n].reshape(-1) for n in SMALL]).reshape(-1, LANE)


def _unpack(slab, shapes):
    flat = slab.reshape(-1)
    out, at = {}, 0
    for n in SMALL:
        size = math.prod(shapes[n])
        out[n] = flat[at:at + size].reshape(shapes[n])
        at += size
    return out


def kernel(x, c, w_ada, b_ada, norm1_g, w_in, v_norm_g, w_spatial, b_spatial, out_norm_g, w_out, norm2_g, w_gate, w_up, w_down, final_g, loss_target, m_w_ada, m_b_ada, m_norm1_g, m_w_in, m_v_norm_g, m_w_spatial, m_b_spatial, m_out_norm_g, m_w_out, m_norm2_g, m_w_gate, m_w_up, m_w_down, m_final_g, v_w_ada, v_b_ada, v_norm1_g, v_w_in, v_v_norm_g, v_w_spatial, v_b_spatial, v_out_norm_g, v_w_out, v_norm2_g, v_w_gate, v_w_up, v_w_down, v_final_g):
    weights = dict(w_ada=w_ada, b_ada=b_ada, norm1_g=norm1_g, w_in=w_in, v_norm_g=v_norm_g, w_spatial=w_spatial,
                   b_spatial=b_spatial, out_norm_g=out_norm_g, w_out=w_out, norm2_g=norm2_g, w_gate=w_gate, w_up=w_up,
                   w_down=w_down, final_g=final_g)
    m_in = dict(w_ada=m_w_ada, b_ada=m_b_ada, norm1_g=m_norm1_g, w_in=m_w_in, v_norm_g=m_v_norm_g, w_spatial=m_w_spatial,
                b_spatial=m_b_spatial, out_norm_g=m_out_norm_g, w_out=m_w_out, norm2_g=m_norm2_g, w_gate=m_w_gate,
                w_up=m_w_up, w_down=m_w_down, final_g=m_final_g)
    v_in = dict(w_ada=v_w_ada, b_ada=v_b_ada, norm1_g=v_norm1_g, w_in=v_w_in, v_norm_g=v_v_norm_g, w_spatial=v_w_spatial,
                b_spatial=v_b_spatial, out_norm_g=v_out_norm_g, w_out=v_w_out, norm2_g=v_norm2_g, w_gate=v_w_gate,
                w_up=v_w_up, w_down=v_w_down, final_g=v_final_g)

    S, D = x.shape[1], x.shape[2]
    n_g = v_norm_g.shape[-1] // LANE
    n_h = (D - n_g * LANE) // LANE
    GW = n_g * LANE
    xi, yi, ci = _place()
    chip = 2 * xi + yi
    me = 4 * xi + 2 * yi + ci
    place = jnp.stack([chip, ci]).astype(jnp.int32)

    xs, target = x[0], loss_target[0]
    geoms = [_Sharded(weights[n].shape[1:], BY_COLS[n]) for n in BIG]

    full = {n: _gather_weight(_cast(weights[n][0], WIRE_DTYPE, "cast_" + n), g, "gather_" + n, 1 + i)
            for i, (n, g) in enumerate(zip(BIG, geoms))}

    c_pad = jnp.concatenate([c, jnp.zeros((7, D), F32)], axis=0)
    c_all = _allgather8(c_pad, "gather_c")[::8]
    n_ada = w_ada.shape[2]
    b_cols = lax.dynamic_slice(b_ada, (0, chip * n_ada), (1, n_ada))
    mod_parts = _allgather8(_mod_part(c_all, w_ada[0], b_cols, "mod_part"), "gather_mod")
    mod_all = mod_parts.reshape(N_CHIPS, 2, 8, n_ada)[:, 0].transpose(1, 0, 2).reshape(8, N_CHIPS * n_ada)
    mod = lax.dynamic_slice(mod_all, (me, 0), (1, 6 * D))
    shift1, scale1, gate1, shift2, scale2, gate2 = [mod[:, i * D:(i + 1) * D] for i in range(6)]

    b_t = b_spatial[0].T
    h1 = _norm_mod(xs, norm1_g, scale1, shift1, "norm1")
    proj, = _mm("nn", h1, full["w_in"], [F32], "proj")
    on_gm = _gmlp_fwd(proj, v_norm_g, w_spatial[0], b_t, out_norm_g, n_g, "gmlp_fwd")
    o_sb, on_sb, l_sum = _sb_fwd(proj, out_norm_g, n_g, n_h, "sb_fwd")
    o_n = jnp.concatenate([on_gm, on_sb], axis=1)
    attn, = _mm("nn", o_n, full["w_out"], [F32], "attn_out")
    x1, h2 = _residual_norm_mod(xs, attn, gate1, norm2_g, scale2, shift2, "norm2")
    a_g, a_u, f_in = _gate_up(h2, full["w_gate"], full["w_up"], "gate_up")
    f, = _mm("nn", f_in, full["w_down"], [F32], "down", tm=512)
    dx2, df, d_gate2, d_final_g, loss_part = _final_loss_bwd(x1, f, gate2, final_g.reshape(1, D), target, "final")
    loss = lax.psum(loss_part[0, 0], ("x", "y", "c"))

    geom_of = dict(zip(BIG, geoms))
    grad_out, delta, new_m, new_v = {}, {}, {}, {}

    def swap(group, grads, collective_id):
        return _swap_core_halves(grads, [geom_of[n] for n in group], "swap_" + "_".join(group), collective_id)

    def chip_sums(group, grads, theirs, after):
        return [_chip_sum(place, gr, _then(after, t), geom_of[n], "chip_sum_" + n) for n, gr, t in zip(group, grads, theirs)]

    def scatter(group, sums, collective_id):
        return _scatter_chip_sums(sums, [geom_of[n] for n in group], "scatter_" + "_".join(group), collective_id)

    def reduce_halves(group, grads, theirs, others, after):
        return [_reduce_half(place, gr, t, _then(after, o), geom_of[n], "reduce_" + n)
                for n, gr, t, o in zip(group, grads, theirs, others)]

    def share(group, halves, collective_id):
        return _share_reduced_halves(halves, [geom_of[n] for n in group], "share_" + "_".join(group), collective_id)

    def adamw(group, reduced, after):
        for n, r in zip(group, reduced):
            grad_out[n] = r[None]
            d, mo, vo = _adamw(weights[n][0], _then(after, r), m_in[n][0], v_in[n][0], "adamw_" + n)
            delta[n], new_m[n], new_v[n] = d[None], mo[None], vo[None]
        return d

    g_down = ("w_down",)
    g_ffn = ("w_gate", "w_up")
    g_out = ("w_out",)
    g_in = ("w_in",)

    gr_down = _mm("tn", f_in, df, [F32], "d_w_down", tm=512, tn=1024)
    th_down = swap(g_down, gr_down, 6)
    d_ag, d_au = _mm("nt", df, full["w_down"], [MXU_DTYPE, MXU_DTYPE], "d_ffn_in", extras=(a_g, a_u),
                     epilogue=_swiglu_bwd_epilogue)
    sm_down = chip_sums(g_down, gr_down, th_down, after=d_ag)
    ot_down = scatter(g_down, sm_down, 7)
    gr_ffn = [_mm("tn", h2, _then(sm_down, d_ag), [F32], "d_w_gate")[0], _mm("tn", h2, d_au, [F32], "d_w_up")[0]]
    th_ffn = swap(g_ffn, gr_ffn, 9)
    dh2_g, = _mm("nt", _then(gr_ffn, d_ag), full["w_gate"], [F32], "d_h2_gate", tm=512)
    dh2, = _mm("nt", d_au, full["w_up"], [F32], "d_h2", tm=512, extras=(dh2_g,), epilogue=_add_epilogue)
    sm_ffn = chip_sums(g_ffn, gr_ffn, th_ffn, after=dh2)
    ot_ffn = scatter(g_ffn, sm_ffn, 10)
    hv_down = reduce_halves(g_down, gr_down, th_down, ot_down, after=sm_ffn)
    rd_down = share(g_down, hv_down, 8)
    dx1, d_shift2, d_scale2, d_norm2_g, d_gate1, d_attn = _norm_mod_bwd(
        _then(hv_down, dh2), x1, dx2, norm2_g, scale2, "norm2_bwd", branch=attn, gate=gate1)
    gr_out = _mm("tn", o_n, d_attn, [F32], "d_w_out")
    th_out = swap(g_out, gr_out, 12)
    d_on, = _mm("nt", _then(gr_out, d_attn), full["w_out"], [F32], "d_o")
    dp_gm, d_w_spatial, d_b_t, d_v_norm_g, d_og_gm = _gmlp_bwd(proj, d_on, v_norm_g, w_spatial[0], b_t, out_norm_g, n_g, "gmlp_bwd")
    dq, dk, dv, d_og_sb = _sb_bwd(proj, o_sb, l_sum, _then(dp_gm, d_on), out_norm_g, n_g, n_h, "sb_bwd")
    sm_out = chip_sums(g_out, gr_out, th_out, after=dq)
    ot_out = scatter(g_out, sm_out, 13)
    hv_ffn = reduce_halves(g_ffn, gr_ffn, th_ffn, ot_ffn, after=sm_out)
    rd_ffn = share(g_ffn, hv_ffn, 11)
    dproj = jnp.concatenate([_then(hv_ffn, dp_gm), dq, dk, dv], axis=1)
    gr_in = _mm("tn", h1, dproj, [F32], "d_w_in")
    th_in = swap(g_in, gr_in, 15)
    dh1, = _mm("nt", _then(gr_in, dproj), full["w_in"], [F32], "d_h1", tm=512)
    grad_x, d_shift1, d_scale1, d_norm1_g = _norm_mod_bwd(dh1, xs, dx1, norm1_g, scale1, "norm1_bwd")
    sm_in = chip_sums(g_in, gr_in, th_in, after=grad_x)
    ot_in = scatter(g_in, sm_in, 16)
    hv_out = reduce_halves(g_out, gr_out, th_out, ot_out, after=sm_in)
    rd_out = share(g_out, hv_out, 14)

    dmod = jnp.concatenate([d_shift1, d_scale1, d_gate1, d_shift2, d_scale2, d_gate2], axis=1)
    small_parts = dict(b_ada=dmod, norm1_g=d_norm1_g, v_norm_g=d_v_norm_g, w_spatial=d_w_spatial, b_spatial=d_b_t.T,
                       out_norm_g=jnp.concatenate([d_og_gm, d_og_sb], axis=1), norm2_g=d_norm2_g, final_g=d_final_g)
    slab = _then(hv_out, _pack(small_parts))
    rows = slab.shape[0]
    gathered = _allgather8(slab, "gather_small")
    small_shapes = {n: weights[n].shape for n in SMALL}
    small_sum = _sum_devices(gathered, 8, "sum_small")
    dmod_all = gathered.reshape(8, rows * LANE)[:, :6 * D]
    dmod_cols = lax.dynamic_slice(dmod_all, (0, chip * n_ada), (8, n_ada))
    g_ada, d, mo, vo = _adamw_ada(c_all, dmod_cols, w_ada[0], m_w_ada[0], v_w_ada[0], "adamw_w_ada")
    grad_out["w_ada"], delta["w_ada"], new_m["w_ada"], new_v["w_ada"] = g_ada[None], d[None], mo[None], vo[None]
    d_small, mo, vo = _adamw(_pack({n: weights[n] for n in SMALL}), small_sum, _pack({n: m_in[n] for n in SMALL}),
                             _pack({n: v_in[n] for n in SMALL}), "adamw_small")
    for dst, slab_out in ((grad_out, small_sum), (delta, d_small), (new_m, mo), (new_v, vo)):
        dst.update(_unpack(slab_out, small_shapes))
    done = adamw(g_down, rd_down, after=d)
    done = adamw(g_ffn, rd_ffn, after=done)
    done = adamw(g_out, rd_out, after=done)
    hv_in = reduce_halves(g_in, gr_in, th_in, ot_in, after=done)
    adamw(g_in, share(g_in, hv_in, 17), after=done)

    return (loss, grad_x[None], *[grad_out[n] for n in ORDER], *[delta[n] for n in ORDER],
            *[new_m[n] for n in ORDER], *[new_v[n] for n in ORDER])
```

```python
import functools
import math

import jax
import jax.numpy as jnp
from jax import lax
from jax.experimental import pallas as pl
from jax.experimental.pallas import tpu as pltpu
from jax.experimental.pallas import tpu_sc as plsc

F32 = jnp.float32
MXU_DTYPE = jnp.bfloat16
WIRE_DTYPE = jnp.bfloat16
EPS = 1e-6
LANE = 128
V7X_VMEM_LIMIT = 56 * 1024 * 1024
MESH = pl.DeviceIdType.MESH
N_CHIPS = 4
FLIPS = (2, 1, 3)

ADAM_LR = 0.001
ADAM_B1 = 0.9
ADAM_B2 = 0.999
ADAM_EPS = 1e-08
ADAM_WD = 0.01
ADAM_STEP = 10


def _params(*semantics):
    return pltpu.CompilerParams(dimension_semantics=semantics or None, vmem_limit_bytes=V7X_VMEM_LIMIT)


def _tile(dim, pref, unit=LANE):
    best = None
    t = unit
    while t <= min(dim, pref):
        if dim % t == 0:
            best = t
        t += unit
    return best if best is not None else dim


def _then(first, second):
    return lax.optimization_barrier((first, second))[1]


def _sum0(v):
    return jnp.sum(v, axis=0, keepdims=True)


def _mean1(v):
    return jnp.mean(v, axis=-1, keepdims=True)


def _gelu(x):
    return 0.5 * x * (1.0 + lax.erf(x * (1.0 / math.sqrt(2.0))))


def _gelu_grad(x):
    cdf = 0.5 * (1.0 + lax.erf(x * (1.0 / math.sqrt(2.0))))
    return cdf + x * jnp.exp(-0.5 * x * x) * (1.0 / math.sqrt(2.0 * math.pi))


def _dot(a, b, dims):
    return lax.dot_general(a, b, (dims, ((), ())), preferred_element_type=F32)


NN = ((1,), (0,))
NT = ((1,), (1,))
TN = ((0,), (0,))


def _mm(kind, a, b, out_dtypes, name, tm=2048, tn=512, extras=(), epilogue=None):
    if kind == "nn":
        (M, K), N = a.shape, b.shape[1]
    elif kind == "nt":
        (M, K), N = a.shape, b.shape[0]
    else:
        (K, M), N = a.shape, b.shape[1]
    tm, tn = _tile(M, tm), _tile(N, tn)
    a_spec = pl.BlockSpec((K, tm), lambda i, j: (0, i)) if kind == "tn" else pl.BlockSpec((tm, K), lambda i, j: (i, 0))
    b_spec = pl.BlockSpec((tn, K), lambda i, j: (j, 0)) if kind == "nt" else pl.BlockSpec((K, tn), lambda i, j: (0, j))
    mn_spec = pl.BlockSpec((tm, tn), lambda i, j: (i, j))
    dims = {"nn": NN, "nt": NT, "tn": TN}[kind]
    n_extra = len(extras)

    def body(a_ref, b_ref, *rest):
        acc = _dot(a_ref[...], b_ref[...], dims)
        res = (acc,) if epilogue is None else epilogue(acc, *[e[...] for e in rest[:n_extra]])
        for o_ref, r in zip(rest[n_extra:], res):
            o_ref[...] = r.astype(o_ref.dtype)

    outs = pl.pallas_call(
        body, name=name, grid=(M // tm, N // tn),
        in_specs=[a_spec, b_spec] + [mn_spec] * n_extra,
        out_specs=[mn_spec] * len(out_dtypes),
        out_shape=[jax.ShapeDtypeStruct((M, N), d) for d in out_dtypes],
        compiler_params=_params("parallel", "arbitrary"),
    )(a, b, *extras)
    return outs


def _mm_ktiled(kind, pairs, name, tm=2048, tn=1024, tk=512):
    a0, b0 = pairs[0]
    M, K = a0.shape
    N = b0.shape[1] if kind == "nn" else b0.shape[0]
    tm, tn, tk = _tile(M, tm), _tile(N, tn), _tile(K, tk)
    a_spec = pl.BlockSpec((tm, tk), lambda i, j, k: (i, k))
    b_spec = pl.BlockSpec((tk, tn), lambda i, j, k: (k, j)) if kind == "nn" else pl.BlockSpec((tn, tk), lambda i, j, k: (j, k))
    dims = NN if kind == "nn" else NT
    n_pairs = len(pairs)

    def body(*refs):
        o_ref = refs[2 * n_pairs]
        acc = _dot(refs[0][...], refs[1][...], dims)
        for p in range(1, n_pairs):
            acc = acc + _dot(refs[2 * p][...], refs[2 * p + 1][...], dims)

        @pl.when(pl.program_id(2) == 0)
        def _():
            o_ref[...] = acc

        @pl.when(pl.program_id(2) != 0)
        def _():
            o_ref[...] += acc

    return pl.pallas_call(
        body, name=name, grid=(M // tm, N // tn, K // tk),
        in_specs=[a_spec, b_spec] * n_pairs,
        out_specs=pl.BlockSpec((tm, tn), lambda i, j, k: (i, j)),
        out_shape=jax.ShapeDtypeStruct((M, N), F32),
        compiler_params=_params("parallel", "parallel", "arbitrary"),
    )(*[x for pair in pairs for x in pair])


def _gate_up(h, wg, wu, name):
    (M, K), N = h.shape, wg.shape[1]
    tm, tn = _tile(M, 2048), _tile(N, 512)

    def body(h_ref, wg_ref, wu_ref, ag_ref, au_ref, f_ref):
        hv = h_ref[...]
        ag = _dot(hv, wg_ref[...], NN)
        au = _dot(hv, wu_ref[...], NN)
        ag_ref[...] = ag.astype(ag_ref.dtype)
        au_ref[...] = au.astype(au_ref.dtype)
        f_ref[...] = (ag * jax.nn.sigmoid(ag) * au).astype(f_ref.dtype)

    w_spec = pl.BlockSpec((K, tn), lambda i, j: (0, j))
    mn_spec = pl.BlockSpec((tm, tn), lambda i, j: (i, j))
    return pl.pallas_call(
        body, name=name, grid=(M // tm, N // tn),
        in_specs=[pl.BlockSpec((tm, K), lambda i, j: (i, 0)), w_spec, w_spec],
        out_specs=[mn_spec] * 3,
        out_shape=[jax.ShapeDtypeStruct((M, N), MXU_DTYPE)] * 3,
        compiler_params=_params("parallel", "arbitrary"),
    )(h, wg, wu)


def _swiglu_bwd_epilogue(dfin, ag, au):
    ag, au = ag.astype(F32), au.astype(F32)
    sg = jax.nn.sigmoid(ag)
    d_au = dfin * (ag * sg)
    d_ag = dfin * au * (sg * (1.0 + ag * (1.0 - sg)))
    return d_ag, d_au


def _row_specs(ts, width):
    return pl.BlockSpec((ts, width), lambda i: (i, 0)), pl.BlockSpec((1, width), lambda i: (0, 0))


def _cast(a, dtype, name):
    R, C = a.shape
    tr = _tile(R, 512, 16)
    spec = pl.BlockSpec((tr, C), lambda i: (i, 0))

    def body(a_ref, o_ref):
        o_ref[...] = a_ref[...].astype(o_ref.dtype)

    return pl.pallas_call(body, name=name, grid=(R // tr,), in_specs=[spec], out_specs=spec,
                          out_shape=jax.ShapeDtypeStruct((R, C), dtype), compiler_params=_params("parallel"))(a)


def _norm_mod(x, g, scale, shift, name):
    S, D = x.shape
    ts = _tile(S, 256, 16)
    tile, vec = _row_specs(ts, D)

    def body(x_ref, g_ref, sc_ref, sh_ref, h_ref):
        xv = x_ref[...]
        r = lax.rsqrt(_mean1(xv * xv) + EPS)
        h_ref[...] = ((xv * r) * g_ref[...] * (1.0 + sc_ref[...]) + sh_ref[...]).astype(h_ref.dtype)

    return pl.pallas_call(body, name=name, grid=(S // ts,), in_specs=[tile, vec, vec, vec], out_specs=tile,
                          out_shape=jax.ShapeDtypeStruct((S, D), MXU_DTYPE), compiler_params=_params("parallel"))(x, g, scale, shift)


def _residual_norm_mod(x, attn, gate, g, scale, shift, name):
    S, D = x.shape
    ts = _tile(S, 256, 16)
    tile, vec = _row_specs(ts, D)

    def body(x_ref, a_ref, gate_ref, g_ref, sc_ref, sh_ref, x1_ref, h_ref):
        x1 = x_ref[...] + gate_ref[...] * a_ref[...]
        x1_ref[...] = x1
        r = lax.rsqrt(_mean1(x1 * x1) + EPS)
        h_ref[...] = ((x1 * r) * g_ref[...] * (1.0 + sc_ref[...]) + sh_ref[...]).astype(h_ref.dtype)

    return pl.pallas_call(body, name=name, grid=(S // ts,), in_specs=[tile, tile, vec, vec, vec, vec],
                          out_specs=[tile, tile],
                          out_shape=[jax.ShapeDtypeStruct((S, D), F32), jax.ShapeDtypeStruct((S, D), MXU_DTYPE)],
                          compiler_params=_params("parallel"))(x, attn, gate, g, scale, shift)


def _final_loss_bwd(x1, f, gate2, final_g, target, name):
    S, D = x1.shape
    ts = _tile(S, 256, 16)
    tile, vec = _row_specs(ts, D)
    loss_spec = pl.BlockSpec((1, LANE), lambda i: (0, 0))

    def body(x1_ref, f_ref, gate_ref, g_ref, t_ref, dx2_ref, df_ref, dgate_ref, dg_ref, loss_ref):
        @pl.when(pl.program_id(0) == 0)
        def _():
            dgate_ref[...] = jnp.zeros_like(dgate_ref)
            dg_ref[...] = jnp.zeros_like(dg_ref)
            loss_ref[...] = jnp.zeros_like(loss_ref)

        fv, gate, g = f_ref[...], gate_ref[...], g_ref[...]
        x2 = x1_ref[...] + gate * fv
        r = lax.rsqrt(_mean1(x2 * x2) + EPS)
        xn = x2 * r
        err = xn * g - t_ref[...]
        loss_ref[...] += jnp.broadcast_to(0.5 * _sum0(_mean1(err * err)), loss_ref.shape)
        dy = err * (1.0 / D)
        dg_ref[...] += _sum0(dy * xn)
        dxn = dy * g
        dx2 = r * (dxn - xn * _mean1(dxn * xn))
        dx2_ref[...] = dx2
        dgate_ref[...] += _sum0(dx2 * fv)
        df_ref[...] = (dx2 * gate).astype(df_ref.dtype)

    return pl.pallas_call(
        body, name=name, grid=(S // ts,), in_specs=[tile, tile, vec, vec, tile],
        out_specs=[tile, tile, vec, vec, loss_spec],
        out_shape=[jax.ShapeDtypeStruct((S, D), F32), jax.ShapeDtypeStruct((S, D), MXU_DTYPE),
                   jax.ShapeDtypeStruct((1, D), F32), jax.ShapeDtypeStruct((1, D), F32),
                   jax.ShapeDtypeStruct((1, LANE), F32)],
        compiler_params=_params("arbitrary"),
    )(x1, f, gate2, final_g, target)


def _norm_mod_bwd(dh, xin, dres, g, scale, name, branch=None, gate=None):
    S, D = xin.shape
    ts = _tile(S, 256, 16)
    tile, vec = _row_specs(ts, D)
    with_gate = branch is not None

    def body(*refs):
        if with_gate:
            dh_ref, x_ref, dres_ref, g_ref, sc_ref, br_ref, gate_ref, dx_ref, dshift_ref, dscale_ref, dg_ref, dgate_ref, dbr_ref = refs
            accs = (dshift_ref, dscale_ref, dg_ref, dgate_ref)
        else:
            dh_ref, x_ref, dres_ref, g_ref, sc_ref, dx_ref, dshift_ref, dscale_ref, dg_ref = refs
            accs = (dshift_ref, dscale_ref, dg_ref)

        @pl.when(pl.program_id(0) == 0)
        def _():
            for acc in accs:
                acc[...] = jnp.zeros_like(acc)

        dh_v, xv, g_v = dh_ref[...], x_ref[...], g_ref[...]
        one_sc = 1.0 + sc_ref[...]
        r = lax.rsqrt(_mean1(xv * xv) + EPS)
        xn = xv * r
        dshift_ref[...] += _sum0(dh_v)
        dscale_ref[...] += _sum0(dh_v * (xn * g_v))
        dg_ref[...] += _sum0(dh_v * one_sc * xn)
        dxn = dh_v * (g_v * one_sc)
        dx = dres_ref[...] + r * (dxn - xn * _mean1(dxn * xn))
        dx_ref[...] = dx
        if with_gate:
            dgate_ref[...] += _sum0(dx * br_ref[...])
            dbr_ref[...] = (dx * gate_ref[...]).astype(dbr_ref.dtype)

    ins = [dh, xin, dres, g, scale] + ([branch, gate] if with_gate else [])
    in_specs = [tile, tile, tile, vec, vec] + ([tile, vec] if with_gate else [])
    out_specs = [tile, vec, vec, vec] + ([vec, tile] if with_gate else [])
    out_shape = [jax.ShapeDtypeStruct((S, D), F32)] + [jax.ShapeDtypeStruct((1, D), F32)] * 3
    if with_gate:
        out_shape += [jax.ShapeDtypeStruct((1, D), F32), jax.ShapeDtypeStruct((S, D), MXU_DTYPE)]
    return pl.pallas_call(body, name=name, grid=(S // ts,), in_specs=in_specs, out_specs=out_specs,
                          out_shape=out_shape, compiler_params=_params("arbitrary"))(*ins)


def _causal_weights(ws_ref, wt_ref, n_g):
    row = lax.broadcasted_iota(jnp.int32, (LANE, LANE), 0)
    col = lax.broadcasted_iota(jnp.int32, (LANE, LANE), 1)
    for g in range(n_g):
        wt_ref[g] = jnp.where(col <= row, ws_ref[g], 0.0).astype(wt_ref.dtype)


def _group_layernorm(v):
    xc = v - _mean1(v)
    rstd = lax.rsqrt(_mean1(xc * xc) + EPS)
    return xc * rstd, rstd


def _gmlp_fwd(proj, v_gain, w_s, b_t, out_gain, n_g, name):
    S = proj.shape[0]
    GW = n_g * LANE

    def body(p_ref, vg_ref, ws_ref, bt_ref, og_ref, on_ref, wt_ref):
        @pl.when(pl.program_id(0) == 0)
        def _():
            _causal_weights(ws_ref, wt_ref, n_g)

        for g in range(n_g):
            cols = slice(g * LANE, (g + 1) * LANE)
            u = _gelu(p_ref[:, cols])
            v = _gelu(p_ref[:, GW + g * LANE:GW + (g + 1) * LANE])
            vhat, _ = _group_layernorm(v)
            vln = (vhat * vg_ref[:, cols]).astype(MXU_DTYPE)
            mixed = _dot(wt_ref[g], vln, NN) + bt_ref[:, g:g + 1]
            o = u * mixed
            r = lax.rsqrt(_mean1(o * o) + EPS)
            on_ref[:, cols] = (o * r * og_ref[:, cols]).astype(on_ref.dtype)

    return pl.pallas_call(
        body, name=name, grid=(S // LANE,),
        in_specs=[pl.BlockSpec((LANE, 2 * GW), lambda n: (n, 0)),
                  pl.BlockSpec((1, GW), lambda n: (0, 0)),
                  pl.BlockSpec((n_g, LANE, LANE), lambda n: (0, 0, 0)),
                  pl.BlockSpec((LANE, n_g), lambda n: (0, 0)),
                  pl.BlockSpec((1, GW), lambda n: (0, 0))],
        out_specs=pl.BlockSpec((LANE, GW), lambda n: (n, 0)),
        out_shape=jax.ShapeDtypeStruct((S, GW), MXU_DTYPE),
        scratch_shapes=[pltpu.VMEM((n_g, LANE, LANE), MXU_DTYPE)],
        compiler_params=_params("arbitrary"),
    )(proj, v_gain, w_s, b_t, out_gain)


def _gmlp_bwd(proj, d_on, v_gain, w_s, b_t, out_gain, n_g, name):
    S = proj.shape[0]
    GW = n_g * LANE

    def body(p_ref, dn_ref, vg_ref, ws_ref, bt_ref, og_ref, dp_ref, dws_ref, dbt_ref, dvg_ref, dog_ref, wt_ref):
        @pl.when(pl.program_id(0) == 0)
        def _():
            _causal_weights(ws_ref, wt_ref, n_g)
            dws_ref[...] = jnp.zeros_like(dws_ref)
            dbt_ref[...] = jnp.zeros_like(dbt_ref)
            dvg_ref[...] = jnp.zeros_like(dvg_ref)
            dog_ref[...] = jnp.zeros_like(dog_ref)

        row = lax.broadcasted_iota(jnp.int32, (LANE, LANE), 0)
        col = lax.broadcasted_iota(jnp.int32, (LANE, LANE), 1)
        for g in range(n_g):
            cols = slice(g * LANE, (g + 1) * LANE)
            vcols = slice(GW + g * LANE, GW + (g + 1) * LANE)
            pu, pv = p_ref[:, cols], p_ref[:, vcols]
            u, v = _gelu(pu), _gelu(pv)
            vhat, rstd = _group_layernorm(v)
            gain = vg_ref[:, cols]
            vln = (vhat * gain).astype(MXU_DTYPE)
            mixed = _dot(wt_ref[g], vln, NN) + bt_ref[:, g:g + 1]
            o = u * mixed
            r = lax.rsqrt(_mean1(o * o) + EPS)
            oh = o * r
            dn = dn_ref[:, cols]
            dog_ref[:, cols] += _sum0(dn * oh)
            dhn = dn * og_ref[:, cols]
            d_o = r * (dhn - oh * _mean1(dhn * oh))
            du = d_o * mixed
            dmix = d_o * u
            dbt_ref[:, g:g + 1] += jnp.sum(dmix, axis=1, keepdims=True)
            dmix_b = dmix.astype(MXU_DTYPE)
            dws_ref[g] += jnp.where(col <= row, _dot(dmix_b, vln, NT), 0.0)
            dvln = _dot(wt_ref[g], dmix_b, TN)
            dvg_ref[:, cols] += _sum0(dvln * vhat)
            dxh = dvln * gain
            dv = rstd * (dxh - _mean1(dxh) - vhat * _mean1(dxh * vhat))
            dp_ref[:, cols] = (du * _gelu_grad(pu)).astype(dp_ref.dtype)
            dp_ref[:, vcols] = (dv * _gelu_grad(pv)).astype(dp_ref.dtype)

    return pl.pallas_call(
        body, name=name, grid=(S // LANE,),
        in_specs=[pl.BlockSpec((LANE, 2 * GW), lambda n: (n, 0)),
                  pl.BlockSpec((LANE, GW), lambda n: (n, 0)),
                  pl.BlockSpec((1, GW), lambda n: (0, 0)),
                  pl.BlockSpec((n_g, LANE, LANE), lambda n: (0, 0, 0)),
                  pl.BlockSpec((LANE, n_g), lambda n: (0, 0)),
                  pl.BlockSpec((1, GW), lambda n: (0, 0))],
        out_specs=[pl.BlockSpec((LANE, 2 * GW), lambda n: (n, 0)),
                   pl.BlockSpec((n_g, LANE, LANE), lambda n: (0, 0, 0)),
                   pl.BlockSpec((LANE, n_g), lambda n: (0, 0)),
                   pl.BlockSpec((1, GW), lambda n: (0, 0)),
                   pl.BlockSpec((1, GW), lambda n: (0, 0))],
        out_shape=[jax.ShapeDtypeStruct((S, 2 * GW), MXU_DTYPE),
                   jax.ShapeDtypeStruct((n_g, LANE, LANE), F32),
                   jax.ShapeDtypeStruct((LANE, n_g), F32),
                   jax.ShapeDtypeStruct((1, GW), F32),
                   jax.ShapeDtypeStruct((1, GW), F32)],
        scratch_shapes=[pltpu.VMEM((n_g, LANE, LANE), MXU_DTYPE)],
        compiler_params=_params("arbitrary"),
    )(proj, d_on, v_gain, w_s, b_t, out_gain)


def _tri_sum(v, tri):
    hi = v.astype(MXU_DTYPE)
    lo = (v - hi.astype(F32)).astype(MXU_DTYPE)
    return _dot(hi, tri, NN) + _dot(lo, tri, NN)


def _log_sigmoids(z):
    sp = jnp.log1p(jnp.exp(-jnp.abs(z)))
    return jnp.minimum(z, 0.0) - sp, jnp.minimum(-z, 0.0) - sp


def _rows(i, size):
    return pl.ds(pl.multiple_of(i * size, size), size)


SB_QUERY_TILE = 512
SB_KEY_TILE = 256


def _sb_tiles(S):
    tq = _tile(S, SB_QUERY_TILE)
    tk = _tile(tq, SB_KEY_TILE)
    return tq, tk, S // tq, tq // tk


def _triangle(n, keep):
    row = lax.broadcasted_iota(jnp.int32, (n, n), 0)
    col = lax.broadcasted_iota(jnp.int32, (n, n), 1)
    return jnp.where(keep(row, col), 1.0, 0.0).astype(MXU_DTYPE)


def _strictly_before(tq, tk, key_offset):
    row = lax.broadcasted_iota(jnp.int32, (tq, tk), 0)
    col = lax.broadcasted_iota(jnp.int32, (tq, tk), 1)
    return col + key_offset < row


def _sb_specs(S, n_g, n_h):
    base = 2 * n_g
    q_spec = pl.BlockSpec((S, LANE), lambda h: (0, base + h))
    k_spec = pl.BlockSpec((S, LANE), lambda h: (0, base + n_h + h))
    v_spec = pl.BlockSpec((S, LANE), lambda h: (0, base + 2 * n_h + h))
    gain_spec = pl.BlockSpec((1, LANE), lambda h: (0, n_g + h))
    head_spec = pl.BlockSpec((S, LANE), lambda h: (0, h))
    return q_spec, k_spec, v_spec, gain_spec, head_spec


def _sb_fwd(proj, out_gain, n_g, n_h, name):
    S = proj.shape[0]
    TQ, TK, NQ, KPQ = _sb_tiles(S)
    scale = LANE ** -0.5
    q_spec, k_spec, v_spec, gain_spec, head_spec = _sb_specs(S, n_g, n_h)

    def body(q_ref, k_ref, v_ref, og_ref, o_ref, on_ref, ls_ref, qb, kb, vb):
        qb[...] = q_ref[...].astype(MXU_DTYPE)
        kb[...] = k_ref[...].astype(MXU_DTYPE)
        vb[...] = v_ref[...].astype(MXU_DTYPE)
        after = _triangle(TK, lambda r, c: r > c)

        def block(qi, j, ctail, acc, key_offset):
            z = _dot(qi, kb[_rows(j, TK), :], NT) * scale
            lb, l1m = _log_sigmoids(z)
            if key_offset is not None:
                strict = _strictly_before(TQ, TK, key_offset)
                l1m = jnp.where(strict, l1m, 0.0)
            a = jnp.exp(lb + ctail + _tri_sum(l1m, after))
            if key_offset is not None:
                a = jnp.where(strict, a, 0.0)
            acc = acc + _dot(a.astype(MXU_DTYPE), vb[_rows(j, TK), :], NN)
            return ctail + jnp.sum(l1m, axis=1, keepdims=True), acc

        def q_loop(i, carry):
            qi = qb[_rows(i, TQ), :]
            state = (jnp.zeros((TQ, 1), F32), jnp.zeros((TQ, LANE), F32))
            for d in reversed(range(KPQ)):
                state = block(qi, i * KPQ + d, state[0], state[1], d * TK)
            ctail, acc = lax.fori_loop(
                0, i * KPQ, lambda jj, st: block(qi, i * KPQ - 1 - jj, st[0], st[1], None), state)
            ls_ref[_rows(i, TQ), :] = jnp.broadcast_to(ctail, (TQ, LANE))
            o_ref[_rows(i, TQ), :] = acc
            r = lax.rsqrt(_mean1(acc * acc) + EPS)
            on_ref[_rows(i, TQ), :] = (acc * r * og_ref[...]).astype(on_ref.dtype)
            return carry

        lax.fori_loop(0, NQ, q_loop, 0)

    return pl.pallas_call(
        body, name=name, grid=(n_h,),
        in_specs=[q_spec, k_spec, v_spec, gain_spec],
        out_specs=[head_spec, head_spec, head_spec],
        out_shape=[jax.ShapeDtypeStruct((S, n_h * LANE), F32), jax.ShapeDtypeStruct((S, n_h * LANE), MXU_DTYPE),
                   jax.ShapeDtypeStruct((S, n_h * LANE), F32)],
        scratch_shapes=[pltpu.VMEM((S, LANE), MXU_DTYPE)] * 3,
        compiler_params=_params("parallel"),
    )(proj, proj, proj, out_gain)


def _sb_bwd(proj, o_sb, l_sum, d_on, out_gain, n_g, n_h, name):
    S = proj.shape[0]
    TQ, TK, NQ, KPQ = _sb_tiles(S)
    scale = LANE ** -0.5
    q_spec, k_spec, v_spec, gain_spec, head_spec = _sb_specs(S, n_g, n_h)
    dn_spec = pl.BlockSpec((S, LANE), lambda h: (0, n_g + h))
    dgain_spec = pl.BlockSpec((1, LANE), lambda h: (0, h))

    def body(q_ref, k_ref, v_ref, o_ref, ls_ref, dn_ref, og_ref, dq_ref, dk_ref, dv_ref, dog_ref,
             qb, kb, vb, dob, dk_acc, dv_acc):
        qb[...] = q_ref[...].astype(MXU_DTYPE)
        kb[...] = k_ref[...].astype(MXU_DTYPE)
        vb[...] = v_ref[...].astype(MXU_DTYPE)
        o, dn = o_ref[...], dn_ref[...]
        r = lax.rsqrt(_mean1(o * o) + EPS)
        oh = o * r
        dog_ref[...] = _sum0(dn * oh)
        dhn = dn * og_ref[...]
        dob[...] = (r * (dhn - oh * _mean1(dhn * oh))).astype(MXU_DTYPE)
        dk_acc[...] = jnp.zeros_like(dk_acc)
        dv_acc[...] = jnp.zeros_like(dv_acc)

        up_to = _triangle(TK, lambda r, c: r <= c)
        before = _triangle(TK, lambda r, c: r < c)

        def block(qi, doi, ltot, j, cl, cdl, dq, key_offset):
            kj, vj = kb[_rows(j, TK), :], vb[_rows(j, TK), :]
            z = _dot(qi, kj, NT) * scale
            lb, l1m_all = _log_sigmoids(z)
            l1m = l1m_all
            if key_offset is not None:
                strict = _strictly_before(TQ, TK, key_offset)
                l1m = jnp.where(strict, l1m_all, 0.0)
            a = jnp.exp(lb + (ltot - (cl + _tri_sum(l1m, up_to))))
            if key_offset is not None:
                a = jnp.where(strict, a, 0.0)
            dl = _dot(doi, vj, NT) * a
            d_l1m = cdl + _tri_sum(dl, before)
            dz = dl * jnp.exp(l1m_all) - jnp.exp(lb) * d_l1m
            if key_offset is not None:
                dz = jnp.where(strict, dz, 0.0)
            dzs = (dz * scale).astype(MXU_DTYPE)
            dq = dq + _dot(dzs, kj, NN)
            dk_acc[_rows(j, TK), :] += _dot(dzs, qi, TN)
            dv_acc[_rows(j, TK), :] += _dot(a.astype(MXU_DTYPE), doi, TN)
            return (cl + jnp.sum(l1m, axis=1, keepdims=True), cdl + jnp.sum(dl, axis=1, keepdims=True), dq)

        def q_loop(i, carry):
            qi, doi = qb[_rows(i, TQ), :], dob[_rows(i, TQ), :]
            ltot = ls_ref[_rows(i, TQ), :][:, :1]
            zero_col = jnp.zeros((TQ, 1), F32)
            state = lax.fori_loop(
                0, i * KPQ, lambda j, st: block(qi, doi, ltot, j, st[0], st[1], st[2], None),
                (zero_col, zero_col, jnp.zeros((TQ, LANE), F32)))
            for d in range(KPQ):
                state = block(qi, doi, ltot, i * KPQ + d, state[0], state[1], state[2], d * TK)
            dq_ref[_rows(i, TQ), :] = state[2].astype(dq_ref.dtype)
            return carry

        lax.fori_loop(0, NQ, q_loop, 0)
        dk_ref[...] = dk_acc[...].astype(dk_ref.dtype)
        dv_ref[...] = dv_acc[...].astype(dv_ref.dtype)

    W = n_h * LANE
    return pl.pallas_call(
        body, name=name, grid=(n_h,),
        in_specs=[q_spec, k_spec, v_spec, head_spec, head_spec, dn_spec, gain_spec],
        out_specs=[head_spec, head_spec, head_spec, dgain_spec],
        out_shape=[jax.ShapeDtypeStruct((S, W), MXU_DTYPE)] * 3 + [jax.ShapeDtypeStruct((1, W), F32)],
        scratch_shapes=[pltpu.VMEM((S, LANE), MXU_DTYPE)] * 4 + [pltpu.VMEM((S, LANE), F32)] * 2,
        compiler_params=_params("parallel"),
    )(proj, proj, proj, o_sb, l_sum, d_on, out_gain)


def _mod_part(c_all, w_ada, b_ada_cols, name):
    B, K = c_all.shape
    N = w_ada.shape[1]
    tn = _tile(N, 512)

    def body(c_ref, w_ref, b_ref, o_ref):
        cv = c_ref[...]
        ca = (cv * jax.nn.sigmoid(cv)).astype(MXU_DTYPE)
        o_ref[...] = _dot(ca, w_ref[...].astype(MXU_DTYPE), NN) + b_ref[...]

    return pl.pallas_call(
        body, name=name, grid=(N // tn,),
        in_specs=[pl.BlockSpec((B, K), lambda j: (0, 0)), pl.BlockSpec((K, tn), lambda j: (0, j)),
                  pl.BlockSpec((1, tn), lambda j: (0, j))],
        out_specs=pl.BlockSpec((B, tn), lambda j: (0, j)),
        out_shape=jax.ShapeDtypeStruct((B, N), F32), compiler_params=_params("parallel"))(c_all, w_ada, b_ada_cols)


def _adamw_math(w, g, m, v):
    m = ADAM_B1 * m + (1.0 - ADAM_B1) * g
    v = ADAM_B2 * v + (1.0 - ADAM_B2) * (g * g)
    m_hat = m / (1.0 - ADAM_B1 ** ADAM_STEP)
    v_hat = v / (1.0 - ADAM_B2 ** ADAM_STEP)
    delta = -ADAM_LR * (m_hat / (jnp.sqrt(v_hat) + ADAM_EPS) + ADAM_WD * w)
    return delta, m, v


def _adamw(w, g, m, v, name):
    R, C = w.shape
    tr = _tile(R, max(8, (1 << 19) // C), 8)
    spec = pl.BlockSpec((tr, C), lambda i: (i, 0))

    def body(w_ref, g_ref, m_ref, v_ref, d_ref, mo_ref, vo_ref):
        d_ref[...], mo_ref[...], vo_ref[...] = _adamw_math(w_ref[...], g_ref[...], m_ref[...], v_ref[...])

    return pl.pallas_call(body, name=name, grid=(R // tr,), in_specs=[spec] * 4, out_specs=[spec] * 3,
                          out_shape=[jax.ShapeDtypeStruct((R, C), F32)] * 3, compiler_params=_params("parallel"))(w, g, m, v)


def _adamw_ada(c_all, dmod_cols, w, m, v, name):
    K, N = w.shape
    B = c_all.shape[0]
    tk, tn = _tile(K, 512), _tile(N, 1024)
    spec = pl.BlockSpec((tk, tn), lambda i, j: (i, j))

    def body(c_ref, dm_ref, w_ref, m_ref, v_ref, g_ref, d_ref, mo_ref, vo_ref):
        cv = c_ref[...]
        ca = (cv * jax.nn.sigmoid(cv)).astype(MXU_DTYPE)
        g = _dot(ca, dm_ref[...].astype(MXU_DTYPE), TN)
        g_ref[...] = g
        d_ref[...], mo_ref[...], vo_ref[...] = _adamw_math(w_ref[...], g, m_ref[...], v_ref[...])

    return pl.pallas_call(
        body, name=name, grid=(K // tk, N // tn),
        in_specs=[pl.BlockSpec((B, tk), lambda i, j: (0, i)), pl.BlockSpec((B, tn), lambda i, j: (0, j)), spec, spec, spec],
        out_specs=[spec] * 4, out_shape=[jax.ShapeDtypeStruct((K, N), F32)] * 4,
        compiler_params=_params("parallel", "parallel"))(c_all, dmod_cols, w, m, v)


def _sum_devices(gathered, n_dev, name):
    R = gathered.shape[0] // n_dev
    C = gathered.shape[1]
    tr = _tile(R, 512, 8)
    n_blk = R // tr

    def body(*refs):
        acc = refs[0][...]
        for r in refs[1:n_dev]:
            acc = acc + r[...]
        refs[n_dev][...] = acc

    in_specs = [pl.BlockSpec((tr, C), functools.partial(lambda i, d: (d * n_blk + i, 0), d=d)) for d in range(n_dev)]
    return pl.pallas_call(body, name=name, grid=(n_blk,), in_specs=in_specs,
                          out_specs=pl.BlockSpec((tr, C), lambda i: (i, 0)),
                          out_shape=jax.ShapeDtypeStruct((R, C), F32), compiler_params=_params("parallel"))(*([gathered] * n_dev))


def _place():
    x, y, c = lax.axis_index("x"), lax.axis_index("y"), lax.axis_index("c")
    return x, y, c


def _allgather8(blk, name):
    m_per, n = blk.shape

    def body(x_ref, out_ref, send_sems, recv_sems, local_sem):
        x, y, c = _place()
        me, sibling = (x, y, c), (x, y, 1 - c)
        chips = [(1 - x, y), (x, 1 - y), (1 - x, 1 - y)]

        def rows(px, py, pc):
            return out_ref.at[pl.ds((4 * px + 2 * py + pc) * m_per, m_per), :]

        def copy(k, block, to, src=None):
            return pltpu.make_async_remote_copy(
                src_ref=rows(*block) if src is None else src, dst_ref=rows(*block),
                send_sem=send_sems.at[k], recv_sem=recv_sems.at[k], device_id=to, device_id_type=MESH)

        mine = pltpu.make_async_copy(x_ref, rows(*me), local_sem)
        mine.start()
        first = [copy(0, me, sibling, src=x_ref)]
        first += [copy(1 + j, me, (*chip, c), src=x_ref) for j, chip in enumerate(chips)]
        for cp in first:
            cp.start()
        passed = [copy(4 + j, (*chip, c), sibling) for j, chip in enumerate(chips)]
        for j, chip in enumerate(chips):
            copy(1 + j, (*chip, c), me).wait_recv()
            passed[j].start()
        copy(0, sibling, me).wait_recv()
        for j, chip in enumerate(chips):
            copy(4 + j, (*chip, 1 - c), me).wait_recv()
        for cp in first + passed:
            cp.wait_send()
        mine.wait()

    return pl.pallas_call(
        body, name=name,
        out_shape=jax.ShapeDtypeStruct((8 * m_per, n), blk.dtype),
        in_specs=[pl.BlockSpec(memory_space=pltpu.VMEM)],
        out_specs=pl.BlockSpec(memory_space=pltpu.VMEM),
        scratch_shapes=[pltpu.SemaphoreType.DMA((7,)), pltpu.SemaphoreType.DMA((7,)), pltpu.SemaphoreType.DMA],
        compiler_params=pltpu.CompilerParams(vmem_limit_bytes=V7X_VMEM_LIMIT),
    )(blk)


class _Sharded:
    def __init__(self, shard_shape, by_cols):
        r, c = shard_shape
        self.by_cols = by_cols
        self.full = (r, N_CHIPS * c) if by_cols else (N_CHIPS * r, c)
        self.shard = (r, c)
        self.half_rows = r // 2
        self.half = (r // 2, c)

    def shard_of(self, ref, k):
        r, c = self.shard
        return ref.at[:, pl.ds(k * c, c)] if self.by_cols else ref.at[pl.ds(k * r, r), :]

    def half_of(self, ref, k, hc):
        r, c = self.shard
        h = self.half_rows
        if self.by_cols:
            return ref.at[pl.ds(hc * h, h), pl.ds(k * c, c)]
        return ref.at[pl.ds(k * r + hc * h, h), :]

    def half_of_shard(self, ref, hc):
        return ref.at[pl.ds(hc * self.half_rows, self.half_rows), :]

    def part_of_halves(self, ref, k):
        r, c = self.shard
        h = self.half_rows
        return ref.at[:, pl.ds(k * c, c)] if self.by_cols else ref.at[pl.ds(k * h, h), :]


def _on_each_place(x, y, c, fn, by_chip=True, by_core=True):
    q = 2 * x + y
    for k in range(N_CHIPS if by_chip else 1):
        for cc in range(2 if by_core else 1):
            cond = None
            if by_chip:
                cond = q == k
            if by_core:
                cond = (c == cc) if cond is None else jnp.logical_and(cond, c == cc)
            pl.when(cond)(functools.partial(fn, k, cc))


def _chip_id(k, c):
    return (k // 2, k % 2, c)


def _handshake(peers):
    barrier = pltpu.get_barrier_semaphore()
    for peer in peers:
        pl.semaphore_signal(barrier, inc=1, device_id=peer, device_id_type=MESH)
    pl.semaphore_wait(barrier, len(peers))


def _on_sequencer(body, inputs, out_structs, n_copies, peers_of, name, collective_id, return_inputs=False):
    in_refs = [jax.new_ref(a, memory_space=pltpu.MemorySpace.HBM) for a in inputs]
    out_refs = [jax.empty_ref(s, memory_space=pltpu.MemorySpace.HBM) for s in out_structs]

    @pl.kernel(mesh=plsc.ScalarSubcoreMesh(axis_name="sequencer", num_cores=1), name=name,
               scratch_types=(pltpu.SemaphoreType.DMA((n_copies,)), pltpu.SemaphoreType.DMA((n_copies,))),
               compiler_params=pltpu.CompilerParams(collective_id=collective_id))
    def launch(send_sems, recv_sems):
        x, y, c = _place()
        _handshake(peers_of(x, y, c))
        body(in_refs, out_refs, send_sems, recv_sems, x, y, c)

    launch()
    return [r[...] for r in (in_refs if return_inputs else out_refs)]


def _sibling(x, y, c):
    return [(x, y, 1 - c)]


def _same_core_of_other_chips(x, y, c):
    return [(1 - x, y, c), (x, 1 - y, c), (1 - x, 1 - y, c)]


def _gather_weight(shard, g, name, collective_id):
    s_ref = jax.new_ref(shard, memory_space=pltpu.MemorySpace.HBM)
    f_ref = jax.empty_ref(jax.ShapeDtypeStruct(g.full, WIRE_DTYPE), memory_space=pltpu.MemorySpace.HBM)
    n_flips = len(FLIPS)

    @pl.kernel(mesh=plsc.ScalarSubcoreMesh(axis_name="sequencer", num_cores=1), name=name,
               scratch_types=(pltpu.SemaphoreType.DMA((2 * n_flips,)), pltpu.SemaphoreType.DMA((2 * n_flips,)),
                              pltpu.SemaphoreType.DMA),
               compiler_params=pltpu.CompilerParams(collective_id=collective_id))
    def launch(send_sems, recv_sems, local_sem):
        x, y, c = _place()
        _handshake([(x, y, 1 - c), (1 - x, y, c), (x, 1 - y, c), (1 - x, 1 - y, c)])

        def at_place(k, cc):
            def remote(slot, src, dst, to):
                return pltpu.make_async_remote_copy(src_ref=src, dst_ref=dst, send_sem=send_sems.at[slot],
                                                    recv_sem=recv_sems.at[slot], device_id=to, device_id_type=MESH)

            local = pltpu.make_async_copy(s_ref, g.shard_of(f_ref, k), local_sem)
            local.start()
            first, passed = [], []
            for j, flip in enumerate(FLIPS):
                cp = remote(j, g.half_of_shard(s_ref, cc), g.half_of(f_ref, k, cc), _chip_id(k ^ flip, cc))
                cp.start()
                first.append(cp)
            for j, flip in enumerate(FLIPS):
                landed = g.half_of(f_ref, k ^ flip, cc)
                remote(j, landed, landed, _chip_id(k, cc)).wait_recv()
                cp = remote(n_flips + j, landed, landed, _chip_id(k, 1 - cc))
                cp.start()
                passed.append(cp)
            for j, flip in enumerate(FLIPS):
                from_sibling = g.half_of(f_ref, k ^ flip, 1 - cc)
                remote(n_flips + j, from_sibling, from_sibling, _chip_id(k, cc)).wait_recv()
            for cp in first + passed:
                cp.wait_send()
            local.wait()

        _on_each_place(x, y, c, at_place)

    launch()
    return f_ref[...]


def _swap_core_halves(grads, geoms, name, collective_id):
    n_cp = sum(1 if g.by_cols else N_CHIPS for g in geoms)

    def body(g_refs, t_refs, send_sems, recv_sems, x, y, c):

        def at_place(_, cc):
            def pairs(hc):
                out = []
                for g, g_ref, t_ref in zip(geoms, g_refs, t_refs):
                    if g.by_cols:
                        out.append((g_ref.at[pl.ds(hc * g.half_rows, g.half_rows), :], t_ref))
                    else:
                        out += [(g.half_of(g_ref, k, hc), g.part_of_halves(t_ref, k)) for k in range(N_CHIPS)]
                return out

            sends = [pltpu.make_async_remote_copy(src_ref=src, dst_ref=dst, send_sem=send_sems.at[n],
                                                  recv_sem=recv_sems.at[n], device_id=(x, y, 1 - cc), device_id_type=MESH)
                     for n, (src, dst) in enumerate(pairs(1 - cc))]
            for cp in sends:
                cp.start()
            for n, (src, dst) in enumerate(pairs(cc)):
                pltpu.make_async_remote_copy(src_ref=src, dst_ref=dst, send_sem=send_sems.at[n], recv_sem=recv_sems.at[n],
                                             device_id=(x, y, cc), device_id_type=MESH).wait_recv()
            for cp in sends:
                cp.wait_send()

        _on_each_place(x, y, c, at_place, by_chip=False)

    return _on_sequencer(body, grads, [jax.ShapeDtypeStruct((g.full[0] // 2, g.full[1]), F32) for g in geoms],
                         n_cp, _sibling, name, collective_id)


def _scatter_chip_sums(sums, geoms, name, collective_id):
    def body(s_refs, r_refs, send_sems, recv_sems, x, y, c):

        def at_place(k, _):
            sends = []
            for i, (g, s_ref, r_ref) in enumerate(zip(geoms, s_refs, r_refs)):
                for j, flip in enumerate(FLIPS):
                    kk = k ^ flip
                    cp = pltpu.make_async_remote_copy(
                        src_ref=g.part_of_halves(s_ref, kk), dst_ref=r_ref.at[j], send_sem=send_sems.at[3 * i + j],
                        recv_sem=recv_sems.at[3 * i + j], device_id=(kk // 2, kk % 2, c), device_id_type=MESH)
                    cp.start()
                    sends.append(cp)
            for i, (g, s_ref, r_ref) in enumerate(zip(geoms, s_refs, r_refs)):
                for j in range(len(FLIPS)):
                    pltpu.make_async_remote_copy(
                        src_ref=g.part_of_halves(s_ref, k), dst_ref=r_ref.at[j], send_sem=send_sems.at[3 * i + j],
                        recv_sem=recv_sems.at[3 * i + j], device_id=(x, y, c), device_id_type=MESH).wait_recv()
            for cp in sends:
                cp.wait_send()

        _on_each_place(x, y, c, at_place, by_core=False)

    return _on_sequencer(body, sums, [jax.ShapeDtypeStruct((len(FLIPS),) + g.half, WIRE_DTYPE) for g in geoms],
                         len(FLIPS) * len(sums), _same_core_of_other_chips, name, collective_id)


def _share_reduced_halves(reduced, geoms, name, collective_id):
    def body(out_refs, _, send_sems, recv_sems, x, y, c):

        def at_place(_, cc):
            sends = []
            for i, (g, ref) in enumerate(zip(geoms, out_refs)):
                mine = g.half_of_shard(ref, cc)
                cp = pltpu.make_async_remote_copy(src_ref=mine, dst_ref=mine, send_sem=send_sems.at[i],
                                                  recv_sem=recv_sems.at[i], device_id=(x, y, 1 - cc), device_id_type=MESH)
                cp.start()
                sends.append(cp)
            for i, (g, ref) in enumerate(zip(geoms, out_refs)):
                theirs = g.half_of_shard(ref, 1 - cc)
                pltpu.make_async_remote_copy(src_ref=theirs, dst_ref=theirs, send_sem=send_sems.at[i],
                                             recv_sem=recv_sems.at[i], device_id=(x, y, cc), device_id_type=MESH).wait_recv()
            for cp in sends:
                cp.wait_send()

        _on_each_place(x, y, c, at_place, by_chip=False)

    return _on_sequencer(body, reduced, [], len(reduced), _sibling, name, collective_id, return_inputs=True)


def _chip_sum(place, grad, theirs, g, name):
    RH, C = theirs.shape
    h = g.half_rows
    tr = _tile(h, 256, 16)
    tc = _tile(C, 2048)
    per_half = h // tr

    if g.by_cols:
        grad_map = lambda i, j, p: (p[1] * per_half + i, j)
    else:
        grad_map = lambda i, j, p: ((i // per_half) * 2 * per_half + p[1] * per_half + i % per_half, j)

    def body(p_ref, a_ref, b_ref, o_ref):
        o_ref[...] = (a_ref[...] + b_ref[...]).astype(o_ref.dtype)

    return pl.pallas_call(
        body, name=name,
        grid_spec=pltpu.PrefetchScalarGridSpec(
            num_scalar_prefetch=1, grid=(RH // tr, C // tc),
            in_specs=[pl.BlockSpec((tr, tc), grad_map), pl.BlockSpec((tr, tc), lambda i, j, p: (i, j))],
            out_specs=pl.BlockSpec((tr, tc), lambda i, j, p: (i, j))),
        out_shape=jax.ShapeDtypeStruct((RH, C), WIRE_DTYPE),
        compiler_params=_params("parallel", "parallel"),
    )(place, grad, theirs)


def _reduce_half(place, grad, theirs, others, g, name):
    h, wc = g.half
    tr = _tile(h, 256, 16)
    per_half = h // tr
    if g.by_cols:
        tc = wc
        grad_map = lambda i, p: (p[1] * per_half + i, p[0])
        theirs_map = lambda i, p: (i, p[0])
    else:
        tc = wc
        grad_map = lambda i, p: (p[0] * 2 * per_half + p[1] * per_half + i, 0)
        theirs_map = lambda i, p: (p[0] * per_half + i, 0)

    def body(p_ref, a_ref, b_ref, o0_ref, o1_ref, o2_ref, out_ref):
        acc = a_ref[...] + b_ref[...]
        for o_ref in (o0_ref, o1_ref, o2_ref):
            acc = acc + o_ref[...].astype(F32)
        out_ref[...] = acc

    other_specs = [pl.BlockSpec((None, tr, tc), functools.partial(lambda i, p, j: (j, i, 0), j=j)) for j in range(len(FLIPS))]
    return pl.pallas_call(
        body, name=name,
        grid_spec=pltpu.PrefetchScalarGridSpec(
            num_scalar_prefetch=1, grid=(per_half,),
            in_specs=[pl.BlockSpec((tr, tc), grad_map), pl.BlockSpec((tr, tc), theirs_map)] + other_specs,
            out_specs=pl.BlockSpec((tr, tc), lambda i, p: (p[1] * per_half + i, 0))),
        out_shape=jax.ShapeDtypeStruct(g.shard, F32),
        compiler_params=_params("arbitrary"),
    )(place, grad, theirs, others, others, others)


SMALL = ("b_ada", "norm1_g", "v_norm_g", "w_spatial", "b_spatial", "out_norm_g", "norm2_g", "final_g")
BIG = ("w_in", "w_out", "w_gate", "w_up", "w_down")
BY_COLS = {"w_in": True, "w_out": False, "w_gate": True, "w_up": True, "w_down": False}
ORDER = ("w_ada", "b_ada", "norm1_g", "w_in", "v_norm_g", "w_spatial", "b_spatial", "out_norm_g", "w_out",
         "norm2_g", "w_gate", "w_up", "w_down", "final_g")


def _pack(parts):
    return jnp.concatenate([parts[n].reshape(-1) for n in SMALL]).reshape(-1, LANE)


def _unpack(slab, shapes):
    flat = slab.reshape(-1)
    out, at = {}, 0
    for n in SMALL:
        size = math.prod(shapes[n])
        out[n] = flat[at:at + size].reshape(shapes[n])
        at += size
    return out


def kernel(x, c, w_ada, b_ada, norm1_g, w_in, v_norm_g, w_spatial, b_spatial, out_norm_g, w_out, norm2_g, w_gate, w_up, w_down, final_g, loss_target, m_w_ada, m_b_ada, m_norm1_g, m_w_in, m_v_norm_g, m_w_spatial, m_b_spatial, m_out_norm_g, m_w_out, m_norm2_g, m_w_gate, m_w_up, m_w_down, m_final_g, v_w_ada, v_b_ada, v_norm1_g, v_w_in, v_v_norm_g, v_w_spatial, v_b_spatial, v_out_norm_g, v_w_out, v_norm2_g, v_w_gate, v_w_up, v_w_down, v_final_g):
    weights = dict(w_ada=w_ada, b_ada=b_ada, norm1_g=norm1_g, w_in=w_in, v_norm_g=v_norm_g, w_spatial=w_spatial,
                   b_spatial=b_spatial, out_norm_g=out_norm_g, w_out=w_out, norm2_g=norm2_g, w_gate=w_gate, w_up=w_up,
                   w_down=w_down, final_g=final_g)
    m_in = dict(w_ada=m_w_ada, b_ada=m_b_ada, norm1_g=m_norm1_g, w_in=m_w_in, v_norm_g=m_v_norm_g, w_spatial=m_w_spatial,
                b_spatial=m_b_spatial, out_norm_g=m_out_norm_g, w_out=m_w_out, norm2_g=m_norm2_g, w_gate=m_w_gate,
                w_up=m_w_up, w_down=m_w_down, final_g=m_final_g)
    v_in = dict(w_ada=v_w_ada, b_ada=v_b_ada, norm1_g=v_norm1_g, w_in=v_w_in, v_norm_g=v_v_norm_g, w_spatial=v_w_spatial,
                b_spatial=v_b_spatial, out_norm_g=v_out_norm_g, w_out=v_w_out, norm2_g=v_norm2_g, w_gate=v_w_gate,
                w_up=v_w_up, w_down=v_w_down, final_g=v_final_g)

    S, D = x.shape[1], x.shape[2]
    n_g = v_norm_g.shape[-1] // LANE
    n_h = (D - n_g * LANE) // LANE
    GW = n_g * LANE
    xi, yi, ci = _place()
    chip = 2 * xi + yi
    me = 4 * xi + 2 * yi + ci
    place = jnp.stack([chip, ci]).astype(jnp.int32)

    xs, target = x[0], loss_target[0]
    geoms = [_Sharded(weights[n].shape[1:], BY_COLS[n]) for n in BIG]

    full = {n: _gather_weight(_cast(weights[n][0], WIRE_DTYPE, "cast_" + n), g, "gather_" + n, 1 + i)
            for i, (n, g) in enumerate(zip(BIG, geoms))}

    c_pad = jnp.concatenate([c, jnp.zeros((7, D), F32)], axis=0)
    c_all = _allgather8(c_pad, "gather_c")[::8]
    n_ada = w_ada.shape[2]
    b_cols = lax.dynamic_slice(b_ada, (0, chip * n_ada), (1, n_ada))
    mod_parts = _allgather8(_mod_part(c_all, w_ada[0], b_cols, "mod_part"), "gather_mod")
    mod_all = mod_parts.reshape(N_CHIPS, 2, 8, n_ada)[:, 0].transpose(1, 0, 2).reshape(8, N_CHIPS * n_ada)
    mod = lax.dynamic_slice(mod_all, (me, 0), (1, 6 * D))
    shift1, scale1, gate1, shift2, scale2, gate2 = [mod[:, i * D:(i + 1) * D] for i in range(6)]

    b_t = b_spatial[0].T
    h1 = _norm_mod(xs, norm1_g, scale1, shift1, "norm1")
    proj, = _mm("nn", h1, full["w_in"], [F32], "proj")
    on_gm = _gmlp_fwd(proj, v_norm_g, w_spatial[0], b_t, out_norm_g, n_g, "gmlp_fwd")
    o_sb, on_sb, l_sum = _sb_fwd(proj, out_norm_g, n_g, n_h, "sb_fwd")
    o_n = jnp.concatenate([on_gm, on_sb], axis=1)
    attn, = _mm("nn", o_n, full["w_out"], [F32], "attn_out")
    x1, h2 = _residual_norm_mod(xs, attn, gate1, norm2_g, scale2, shift2, "norm2")
    a_g, a_u, f_in = _gate_up(h2, full["w_gate"], full["w_up"], "gate_up")
    f = _mm_ktiled("nn", [(f_in, full["w_down"])], "down")
    dx2, df, d_gate2, d_final_g, loss_part = _final_loss_bwd(x1, f, gate2, final_g.reshape(1, D), target, "final")
    loss = lax.psum(loss_part[0, 0], ("x", "y", "c"))

    geom_of = dict(zip(BIG, geoms))
    grad_out, delta, new_m, new_v = {}, {}, {}, {}

    def swap(group, grads, collective_id):
        return _swap_core_halves(grads, [geom_of[n] for n in group], "swap_" + "_".join(group), collective_id)

    def chip_sums(group, grads, theirs, after):
        return [_chip_sum(place, gr, _then(after, t), geom_of[n], "chip_sum_" + n) for n, gr, t in zip(group, grads, theirs)]

    def scatter(group, sums, collective_id):
        return _scatter_chip_sums(sums, [geom_of[n] for n in group], "scatter_" + "_".join(group), collective_id)

    def reduce_halves(group, grads, theirs, others, after):
        return [_reduce_half(place, gr, t, _then(after, o), geom_of[n], "reduce_" + n)
                for n, gr, t, o in zip(group, grads, theirs, others)]

    def share(group, halves, collective_id):
        return _share_reduced_halves(halves, [geom_of[n] for n in group], "share_" + "_".join(group), collective_id)

    def adamw(group, reduced, after):
        for n, r in zip(group, reduced):
            grad_out[n] = r[None]
            d, mo, vo = _adamw(weights[n][0], _then(after, r), m_in[n][0], v_in[n][0], "adamw_" + n)
            delta[n], new_m[n], new_v[n] = d[None], mo[None], vo[None]
        return d

    g_down = ("w_down",)
    g_ffn = ("w_gate", "w_up")
    g_out = ("w_out",)
    g_in = ("w_in",)

    gr_down = _mm("tn", f_in, df, [F32], "d_w_down", tm=1408, tn=1024)
    th_down = swap(g_down, gr_down, 6)
    d_ag, d_au = _mm("nt", df, full["w_down"], [MXU_DTYPE, MXU_DTYPE], "d_ffn_in", extras=(a_g, a_u),
                     epilogue=_swiglu_bwd_epilogue)
    sm_down = chip_sums(g_down, gr_down, th_down, after=d_ag)
    ot_down = scatter(g_down, sm_down, 7)
    gr_ffn = [_mm("tn", h2, _then(sm_down, d_ag), [F32], "d_w_gate")[0], _mm("tn", h2, d_au, [F32], "d_w_up")[0]]
    th_ffn = swap(g_ffn, gr_ffn, 9)
    dh2 = _mm_ktiled("nt", [(_then(gr_ffn, d_ag), full["w_gate"]), (d_au, full["w_up"])], "d_h2")
    sm_ffn = chip_sums(g_ffn, gr_ffn, th_ffn, after=dh2)
    ot_ffn = scatter(g_ffn, sm_ffn, 10)
    hv_down = reduce_halves(g_down, gr_down, th_down, ot_down, after=sm_ffn)
    rd_down = share(g_down, hv_down, 8)
    dx1, d_shift2, d_scale2, d_norm2_g, d_gate1, d_attn = _norm_mod_bwd(
        _then(hv_down, dh2), x1, dx2, norm2_g, scale2, "norm2_bwd", branch=attn, gate=gate1)
    gr_out = _mm("tn", o_n, d_attn, [F32], "d_w_out")
    th_out = swap(g_out, gr_out, 12)
    d_on, = _mm("nt", _then(gr_out, d_attn), full["w_out"], [F32], "d_o")
    dp_gm, d_w_spatial, d_b_t, d_v_norm_g, d_og_gm = _gmlp_bwd(proj, d_on, v_norm_g, w_spatial[0], b_t, out_norm_g, n_g, "gmlp_bwd")
    dq, dk, dv, d_og_sb = _sb_bwd(proj, o_sb, l_sum, _then(dp_gm, d_on), out_norm_g, n_g, n_h, "sb_bwd")
    sm_out = chip_sums(g_out, gr_out, th_out, after=dq)
    ot_out = scatter(g_out, sm_out, 13)
    hv_ffn = reduce_halves(g_ffn, gr_ffn, th_ffn, ot_ffn, after=sm_out)
    rd_ffn = share(g_ffn, hv_ffn, 11)
    dproj = jnp.concatenate([_then(hv_ffn, dp_gm), dq, dk, dv], axis=1)
    gr_in = _mm("tn", h1, dproj, [F32], "d_w_in")
    th_in = swap(g_in, gr_in, 15)
    dh1 = _mm_ktiled("nt", [(_then(gr_in, dproj), full["w_in"])], "d_h1")
    grad_x, d_shift1, d_scale1, d_norm1_g = _norm_mod_bwd(dh1, xs, dx1, norm1_g, scale1, "norm1_bwd")
    sm_in = chip_sums(g_in, gr_in, th_in, after=grad_x)
    ot_in = scatter(g_in, sm_in, 16)
    hv_out = reduce_halves(g_out, gr_out, th_out, ot_out, after=sm_in)
    rd_out = share(g_out, hv_out, 14)

    dmod = jnp.concatenate([d_shift1, d_scale1, d_gate1, d_shift2, d_scale2, d_gate2], axis=1)
    small_parts = dict(b_ada=dmod, norm1_g=d_norm1_g, v_norm_g=d_v_norm_g, w_spatial=d_w_spatial, b_spatial=d_b_t.T,
                       out_norm_g=jnp.concatenate([d_og_gm, d_og_sb], axis=1), norm2_g=d_norm2_g, final_g=d_final_g)
    slab = _then(hv_out, _pack(small_parts))
    rows = slab.shape[0]
    gathered = _allgather8(slab, "gather_small")
    small_shapes = {n: weights[n].shape for n in SMALL}
    small_sum = _sum_devices(gathered, 8, "sum_small")
    dmod_all = gathered.reshape(8, rows * LANE)[:, :6 * D]
    dmod_cols = lax.dynamic_slice(dmod_all, (0, chip * n_ada), (8, n_ada))
    g_ada, d, mo, vo = _adamw_ada(c_all, dmod_cols, w_ada[0], m_w_ada[0], v_w_ada[0], "adamw_w_ada")
    grad_out["w_ada"], delta["w_ada"], new_m["w_ada"], new_v["w_ada"] = g_ada[None], d[None], mo[None], vo[None]
    d_small, mo, vo = _adamw(_pack({n: weights[n] for n in SMALL}), small_sum, _pack({n: m_in[n] for n in SMALL}),
                             _pack({n: v_in[n] for n in SMALL}), "adamw_small")
    for dst, slab_out in ((grad_out, small_sum), (delta, d_small), (new_m, mo), (new_v, vo)):
        dst.update(_unpack(slab_out, small_shapes))
    done = adamw(g_down, rd_down, after=d)
    done = adamw(g_ffn, rd_ffn, after=done)
    done = adamw(g_out, rd_out, after=done)
    hv_in = reduce_halves(g_in, gr_in, th_in, ot_in, after=done)
    adamw(g_in, share(g_in, hv_in, 17), after=done)

    return (loss, grad_x[None], *[grad_out[n] for n in ORDER], *[delta[n] for n in ORDER],
            *[new_m[n] for n in ORDER], *[new_v[n] for n in ORDER])
```

```python
import functools
import math

import jax
import jax.numpy as jnp
from jax import lax
from jax.experimental import pallas as pl
from jax.experimental.pallas import tpu as pltpu
from jax.experimental.pallas import tpu_sc as plsc

F32 = jnp.float32
MXU_DTYPE = jnp.bfloat16
WIRE_DTYPE = jnp.bfloat16
EPS = 1e-6
LANE = 128
V7X_VMEM_LIMIT = 56 * 1024 * 1024
MESH = pl.DeviceIdType.MESH
N_CHIPS = 4
FLIPS = (2, 1, 3)

ADAM_LR = 0.001
ADAM_B1 = 0.9
ADAM_B2 = 0.999
ADAM_EPS = 1e-08
ADAM_WD = 0.01
ADAM_STEP = 10


def _params(*semantics):
    return pltpu.CompilerParams(dimension_semantics=semantics or None, vmem_limit_bytes=V7X_VMEM_LIMIT)


def _tile(dim, pref, unit=LANE):
    best = None
    t = unit
    while t <= min(dim, pref):
        if dim % t == 0:
            best = t
        t += unit
    return best if best is not None else dim


def _then(first, second):
    return lax.optimization_barrier((first, second))[1]


def _sum0(v):
    return jnp.sum(v, axis=0, keepdims=True)


def _mean1(v):
    return jnp.mean(v, axis=-1, keepdims=True)


def _gelu(x):
    return 0.5 * x * (1.0 + lax.erf(x * (1.0 / math.sqrt(2.0))))


def _gelu_grad(x):
    cdf = 0.5 * (1.0 + lax.erf(x * (1.0 / math.sqrt(2.0))))
    return cdf + x * jnp.exp(-0.5 * x * x) * (1.0 / math.sqrt(2.0 * math.pi))


def _dot(a, b, dims):
    return lax.dot_general(a, b, (dims, ((), ())), preferred_element_type=F32)


NN = ((1,), (0,))
NT = ((1,), (1,))
TN = ((0,), (0,))


def _mm(kind, a, b, out_dtypes, name, tm=2048, tn=512, extras=(), epilogue=None):
    if kind == "nn":
        (M, K), N = a.shape, b.shape[1]
    elif kind == "nt":
        (M, K), N = a.shape, b.shape[0]
    else:
        (K, M), N = a.shape, b.shape[1]
    tm, tn = _tile(M, tm), _tile(N, tn)
    a_spec = pl.BlockSpec((K, tm), lambda i, j: (0, i)) if kind == "tn" else pl.BlockSpec((tm, K), lambda i, j: (i, 0))
    b_spec = pl.BlockSpec((tn, K), lambda i, j: (j, 0)) if kind == "nt" else pl.BlockSpec((K, tn), lambda i, j: (0, j))
    mn_spec = pl.BlockSpec((tm, tn), lambda i, j: (i, j))
    dims = {"nn": NN, "nt": NT, "tn": TN}[kind]
    n_extra = len(extras)

    def body(a_ref, b_ref, *rest):
        acc = _dot(a_ref[...], b_ref[...], dims)
        res = (acc,) if epilogue is None else epilogue(acc, *[e[...] for e in rest[:n_extra]])
        for o_ref, r in zip(rest[n_extra:], res):
            o_ref[...] = r.astype(o_ref.dtype)

    outs = pl.pallas_call(
        body, name=name, grid=(M // tm, N // tn),
        in_specs=[a_spec, b_spec] + [mn_spec] * n_extra,
        out_specs=[mn_spec] * len(out_dtypes),
        out_shape=[jax.ShapeDtypeStruct((M, N), d) for d in out_dtypes],
        compiler_params=_params("parallel", "arbitrary"),
    )(a, b, *extras)
    return outs


def _mm_ktiled(kind, pairs, name, tm=2048, tn=1024, tk=1408):
    a0, b0 = pairs[0]
    M, K = a0.shape
    N = b0.shape[1] if kind == "nn" else b0.shape[0]
    tm, tn, tk = _tile(M, tm), _tile(N, tn), _tile(K, tk)
    a_spec = pl.BlockSpec((tm, tk), lambda i, j, k: (i, k))
    b_spec = pl.BlockSpec((tk, tn), lambda i, j, k: (k, j)) if kind == "nn" else pl.BlockSpec((tn, tk), lambda i, j, k: (j, k))
    dims = NN if kind == "nn" else NT
    n_pairs = len(pairs)

    def body(*refs):
        o_ref = refs[2 * n_pairs]
        acc = _dot(refs[0][...], refs[1][...], dims)
        for p in range(1, n_pairs):
            acc = acc + _dot(refs[2 * p][...], refs[2 * p + 1][...], dims)

        @pl.when(pl.program_id(2) == 0)
        def _():
            o_ref[...] = acc

        @pl.when(pl.program_id(2) != 0)
        def _():
            o_ref[...] += acc

    return pl.pallas_call(
        body, name=name, grid=(M // tm, N // tn, K // tk),
        in_specs=[a_spec, b_spec] * n_pairs,
        out_specs=pl.BlockSpec((tm, tn), lambda i, j, k: (i, j)),
        out_shape=jax.ShapeDtypeStruct((M, N), F32),
        compiler_params=_params("parallel", "parallel", "arbitrary"),
    )(*[x for pair in pairs for x in pair])


def _gate_up(h, wg, wu, name):
    (M, K), N = h.shape, wg.shape[1]
    tm, tn = _tile(M, 2048), _tile(N, 512)

    def body(h_ref, wg_ref, wu_ref, ag_ref, au_ref, f_ref):
        hv = h_ref[...]
        ag = _dot(hv, wg_ref[...], NN)
        au = _dot(hv, wu_ref[...], NN)
        ag_ref[...] = ag.astype(ag_ref.dtype)
        au_ref[...] = au.astype(au_ref.dtype)
        f_ref[...] = (ag * jax.nn.sigmoid(ag) * au).astype(f_ref.dtype)

    w_spec = pl.BlockSpec((K, tn), lambda i, j: (0, j))
    mn_spec = pl.BlockSpec((tm, tn), lambda i, j: (i, j))
    return pl.pallas_call(
        body, name=name, grid=(M // tm, N // tn),
        in_specs=[pl.BlockSpec((tm, K), lambda i, j: (i, 0)), w_spec, w_spec],
        out_specs=[mn_spec] * 3,
        out_shape=[jax.ShapeDtypeStruct((M, N), MXU_DTYPE)] * 3,
        compiler_params=_params("parallel", "arbitrary"),
    )(h, wg, wu)


def _swiglu_bwd_epilogue(dfin, ag, au):
    ag, au = ag.astype(F32), au.astype(F32)
    sg = jax.nn.sigmoid(ag)
    d_au = dfin * (ag * sg)
    d_ag = dfin * au * (sg * (1.0 + ag * (1.0 - sg)))
    return d_ag, d_au


def _row_specs(ts, width):
    return pl.BlockSpec((ts, width), lambda i: (i, 0)), pl.BlockSpec((1, width), lambda i: (0, 0))


def _cast(a, dtype, name):
    R, C = a.shape
    tr = _tile(R, 512, 16)
    spec = pl.BlockSpec((tr, C), lambda i: (i, 0))

    def body(a_ref, o_ref):
        o_ref[...] = a_ref[...].astype(o_ref.dtype)

    return pl.pallas_call(body, name=name, grid=(R // tr,), in_specs=[spec], out_specs=spec,
                          out_shape=jax.ShapeDtypeStruct((R, C), dtype), compiler_params=_params("parallel"))(a)


def _norm_mod(x, g, scale, shift, name):
    S, D = x.shape
    ts = _tile(S, 256, 16)
    tile, vec = _row_specs(ts, D)

    def body(x_ref, g_ref, sc_ref, sh_ref, h_ref):
        xv = x_ref[...]
        r = lax.rsqrt(_mean1(xv * xv) + EPS)
        h_ref[...] = ((xv * r) * g_ref[...] * (1.0 + sc_ref[...]) + sh_ref[...]).astype(h_ref.dtype)

    return pl.pallas_call(body, name=name, grid=(S // ts,), in_specs=[tile, vec, vec, vec], out_specs=tile,
                          out_shape=jax.ShapeDtypeStruct((S, D), MXU_DTYPE), compiler_params=_params("parallel"))(x, g, scale, shift)


def _residual_norm_mod(x, attn, gate, g, scale, shift, name):
    S, D = x.shape
    ts = _tile(S, 256, 16)
    tile, vec = _row_specs(ts, D)

    def body(x_ref, a_ref, gate_ref, g_ref, sc_ref, sh_ref, x1_ref, h_ref):
        x1 = x_ref[...] + gate_ref[...] * a_ref[...]
        x1_ref[...] = x1
        r = lax.rsqrt(_mean1(x1 * x1) + EPS)
        h_ref[...] = ((x1 * r) * g_ref[...] * (1.0 + sc_ref[...]) + sh_ref[...]).astype(h_ref.dtype)

    return pl.pallas_call(body, name=name, grid=(S // ts,), in_specs=[tile, tile, vec, vec, vec, vec],
                          out_specs=[tile, tile],
                          out_shape=[jax.ShapeDtypeStruct((S, D), F32), jax.ShapeDtypeStruct((S, D), MXU_DTYPE)],
                          compiler_params=_params("parallel"))(x, attn, gate, g, scale, shift)


def _final_loss_bwd(x1, f, gate2, final_g, target, name):
    S, D = x1.shape
    ts = _tile(S, 256, 16)
    tile, vec = _row_specs(ts, D)
    loss_spec = pl.BlockSpec((1, LANE), lambda i: (0, 0))

    def body(x1_ref, f_ref, gate_ref, g_ref, t_ref, dx2_ref, df_ref, dgate_ref, dg_ref, loss_ref):
        @pl.when(pl.program_id(0) == 0)
        def _():
            dgate_ref[...] = jnp.zeros_like(dgate_ref)
            dg_ref[...] = jnp.zeros_like(dg_ref)
            loss_ref[...] = jnp.zeros_like(loss_ref)

        fv, gate, g = f_ref[...], gate_ref[...], g_ref[...]
        x2 = x1_ref[...] + gate * fv
        r = lax.rsqrt(_mean1(x2 * x2) + EPS)
        xn = x2 * r
        err = xn * g - t_ref[...]
        loss_ref[...] += jnp.broadcast_to(0.5 * _sum0(_mean1(err * err)), loss_ref.shape)
        dy = err * (1.0 / D)
        dg_ref[...] += _sum0(dy * xn)
        dxn = dy * g
        dx2 = r * (dxn - xn * _mean1(dxn * xn))
        dx2_ref[...] = dx2
        dgate_ref[...] += _sum0(dx2 * fv)
        df_ref[...] = (dx2 * gate).astype(df_ref.dtype)

    return pl.pallas_call(
        body, name=name, grid=(S // ts,), in_specs=[tile, tile, vec, vec, tile],
        out_specs=[tile, tile, vec, vec, loss_spec],
        out_shape=[jax.ShapeDtypeStruct((S, D), F32), jax.ShapeDtypeStruct((S, D), MXU_DTYPE),
                   jax.ShapeDtypeStruct((1, D), F32), jax.ShapeDtypeStruct((1, D), F32),
                   jax.ShapeDtypeStruct((1, LANE), F32)],
        compiler_params=_params("arbitrary"),
    )(x1, f, gate2, final_g, target)


def _norm_mod_bwd(dh, xin, dres, g, scale, name, branch=None, gate=None):
    S, D = xin.shape
    ts = _tile(S, 256, 16)
    tile, vec = _row_specs(ts, D)
    with_gate = branch is not None

    def body(*refs):
        if with_gate:
            dh_ref, x_ref, dres_ref, g_ref, sc_ref, br_ref, gate_ref, dx_ref, dshift_ref, dscale_ref, dg_ref, dgate_ref, dbr_ref = refs
            accs = (dshift_ref, dscale_ref, dg_ref, dgate_ref)
        else:
            dh_ref, x_ref, dres_ref, g_ref, sc_ref, dx_ref, dshift_ref, dscale_ref, dg_ref = refs
            accs = (dshift_ref, dscale_ref, dg_ref)

        @pl.when(pl.program_id(0) == 0)
        def _():
            for acc in accs:
                acc[...] = jnp.zeros_like(acc)

        dh_v, xv, g_v = dh_ref[...], x_ref[...], g_ref[...]
        one_sc = 1.0 + sc_ref[...]
        r = lax.rsqrt(_mean1(xv * xv) + EPS)
        xn = xv * r
        dshift_ref[...] += _sum0(dh_v)
        dscale_ref[...] += _sum0(dh_v * (xn * g_v))
        dg_ref[...] += _sum0(dh_v * one_sc * xn)
        dxn = dh_v * (g_v * one_sc)
        dx = dres_ref[...] + r * (dxn - xn * _mean1(dxn * xn))
        dx_ref[...] = dx
        if with_gate:
            dgate_ref[...] += _sum0(dx * br_ref[...])
            dbr_ref[...] = (dx * gate_ref[...]).astype(dbr_ref.dtype)

    ins = [dh, xin, dres, g, scale] + ([branch, gate] if with_gate else [])
    in_specs = [tile, tile, tile, vec, vec] + ([tile, vec] if with_gate else [])
    out_specs = [tile, vec, vec, vec] + ([vec, tile] if with_gate else [])
    out_shape = [jax.ShapeDtypeStruct((S, D), F32)] + [jax.ShapeDtypeStruct((1, D), F32)] * 3
    if with_gate:
        out_shape += [jax.ShapeDtypeStruct((1, D), F32), jax.ShapeDtypeStruct((S, D), MXU_DTYPE)]
    return pl.pallas_call(body, name=name, grid=(S // ts,), in_specs=in_specs, out_specs=out_specs,
                          out_shape=out_shape, compiler_params=_params("arbitrary"))(*ins)


def _causal_weights(ws_ref, wt_ref, n_g):
    row = lax.broadcasted_iota(jnp.int32, (LANE, LANE), 0)
    col = lax.broadcasted_iota(jnp.int32, (LANE, LANE), 1)
    for g in range(n_g):
        wt_ref[g] = jnp.where(col <= row, ws_ref[g], 0.0).astype(wt_ref.dtype)


def _group_layernorm(v):
    xc = v - _mean1(v)
    rstd = lax.rsqrt(_mean1(xc * xc) + EPS)
    return xc * rstd, rstd


def _gmlp_fwd(proj, v_gain, w_s, b_t, out_gain, n_g, name):
    S = proj.shape[0]
    GW = n_g * LANE

    def body(p_ref, vg_ref, ws_ref, bt_ref, og_ref, on_ref, wt_ref):
        @pl.when(pl.program_id(0) == 0)
        def _():
            _causal_weights(ws_ref, wt_ref, n_g)

        for g in range(n_g):
            cols = slice(g * LANE, (g + 1) * LANE)
            u = _gelu(p_ref[:, cols])
            v = _gelu(p_ref[:, GW + g * LANE:GW + (g + 1) * LANE])
            vhat, _ = _group_layernorm(v)
            vln = (vhat * vg_ref[:, cols]).astype(MXU_DTYPE)
            mixed = _dot(wt_ref[g], vln, NN) + bt_ref[:, g:g + 1]
            o = u * mixed
            r = lax.rsqrt(_mean1(o * o) + EPS)
            on_ref[:, cols] = (o * r * og_ref[:, cols]).astype(on_ref.dtype)

    return pl.pallas_call(
        body, name=name, grid=(S // LANE,),
        in_specs=[pl.BlockSpec((LANE, 2 * GW), lambda n: (n, 0)),
                  pl.BlockSpec((1, GW), lambda n: (0, 0)),
                  pl.BlockSpec((n_g, LANE, LANE), lambda n: (0, 0, 0)),
                  pl.BlockSpec((LANE, n_g), lambda n: (0, 0)),
                  pl.BlockSpec((1, GW), lambda n: (0, 0))],
        out_specs=pl.BlockSpec((LANE, GW), lambda n: (n, 0)),
        out_shape=jax.ShapeDtypeStruct((S, GW), MXU_DTYPE),
        scratch_shapes=[pltpu.VMEM((n_g, LANE, LANE), MXU_DTYPE)],
        compiler_params=_params("arbitrary"),
    )(proj, v_gain, w_s, b_t, out_gain)


def _gmlp_bwd(proj, d_on, v_gain, w_s, b_t, out_gain, n_g, name):
    S = proj.shape[0]
    GW = n_g * LANE

    def body(p_ref, dn_ref, vg_ref, ws_ref, bt_ref, og_ref, dp_ref, dws_ref, dbt_ref, dvg_ref, dog_ref, wt_ref):
        @pl.when(pl.program_id(0) == 0)
        def _():
            _causal_weights(ws_ref, wt_ref, n_g)
            dws_ref[...] = jnp.zeros_like(dws_ref)
            dbt_ref[...] = jnp.zeros_like(dbt_ref)
            dvg_ref[...] = jnp.zeros_like(dvg_ref)
            dog_ref[...] = jnp.zeros_like(dog_ref)

        row = lax.broadcasted_iota(jnp.int32, (LANE, LANE), 0)
        col = lax.broadcasted_iota(jnp.int32, (LANE, LANE), 1)
        for g in range(n_g):
            cols = slice(g * LANE, (g + 1) * LANE)
            vcols = slice(GW + g * LANE, GW + (g + 1) * LANE)
            pu, pv = p_ref[:, cols], p_ref[:, vcols]
            u, v = _gelu(pu), _gelu(pv)
            vhat, rstd = _group_layernorm(v)
            gain = vg_ref[:, cols]
            vln = (vhat * gain).astype(MXU_DTYPE)
            mixed = _dot(wt_ref[g], vln, NN) + bt_ref[:, g:g + 1]
            o = u * mixed
            r = lax.rsqrt(_mean1(o * o) + EPS)
            oh = o * r
            dn = dn_ref[:, cols]
            dog_ref[:, cols] += _sum0(dn * oh)
            dhn = dn * og_ref[:, cols]
            d_o = r * (dhn - oh * _mean1(dhn * oh))
            du = d_o * mixed
            dmix = d_o * u
            dbt_ref[:, g:g + 1] += jnp.sum(dmix, axis=1, keepdims=True)
            dmix_b = dmix.astype(MXU_DTYPE)
            dws_ref[g] += jnp.where(col <= row, _dot(dmix_b, vln, NT), 0.0)
            dvln = _dot(wt_ref[g], dmix_b, TN)
            dvg_ref[:, cols] += _sum0(dvln * vhat)
            dxh = dvln * gain
            dv = rstd * (dxh - _mean1(dxh) - vhat * _mean1(dxh * vhat))
            dp_ref[:, cols] = (du * _gelu_grad(pu)).astype(dp_ref.dtype)
            dp_ref[:, vcols] = (dv * _gelu_grad(pv)).astype(dp_ref.dtype)

    return pl.pallas_call(
        body, name=name, grid=(S // LANE,),
        in_specs=[pl.BlockSpec((LANE, 2 * GW), lambda n: (n, 0)),
                  pl.BlockSpec((LANE, GW), lambda n: (n, 0)),
                  pl.BlockSpec((1, GW), lambda n: (0, 0)),
                  pl.BlockSpec((n_g, LANE, LANE), lambda n: (0, 0, 0)),
                  pl.BlockSpec((LANE, n_g), lambda n: (0, 0)),
                  pl.BlockSpec((1, GW), lambda n: (0, 0))],
        out_specs=[pl.BlockSpec((LANE, 2 * GW), lambda n: (n, 0)),
                   pl.BlockSpec((n_g, LANE, LANE), lambda n: (0, 0, 0)),
                   pl.BlockSpec((LANE, n_g), lambda n: (0, 0)),
                   pl.BlockSpec((1, GW), lambda n: (0, 0)),
                   pl.BlockSpec((1, GW), lambda n: (0, 0))],
        out_shape=[jax.ShapeDtypeStruct((S, 2 * GW), MXU_DTYPE),
                   jax.ShapeDtypeStruct((n_g, LANE, LANE), F32),
                   jax.ShapeDtypeStruct((LANE, n_g), F32),
                   jax.ShapeDtypeStruct((1, GW), F32),
                   jax.ShapeDtypeStruct((1, GW), F32)],
        scratch_shapes=[pltpu.VMEM((n_g, LANE, LANE), MXU_DTYPE)],
        compiler_params=_params("arbitrary"),
    )(proj, d_on, v_gain, w_s, b_t, out_gain)


def _tri_sum(v, tri):
    hi = v.astype(MXU_DTYPE)
    lo = (v - hi.astype(F32)).astype(MXU_DTYPE)
    return _dot(hi, tri, NN) + _dot(lo, tri, NN)


def _log_sigmoids(z):
    sp = jnp.log1p(jnp.exp(-jnp.abs(z)))
    return jnp.minimum(z, 0.0) - sp, jnp.minimum(-z, 0.0) - sp


def _rows(i, size):
    return pl.ds(pl.multiple_of(i * size, size), size)


SB_QUERY_TILE = 512
SB_KEY_TILE = 256


def _sb_tiles(S):
    tq = _tile(S, SB_QUERY_TILE)
    tk = _tile(tq, SB_KEY_TILE)
    return tq, tk, S // tq, tq // tk


def _triangle(n, keep):
    row = lax.broadcasted_iota(jnp.int32, (n, n), 0)
    col = lax.broadcasted_iota(jnp.int32, (n, n), 1)
    return jnp.where(keep(row, col), 1.0, 0.0).astype(MXU_DTYPE)


def _strictly_before(tq, tk, key_offset):
    row = lax.broadcasted_iota(jnp.int32, (tq, tk), 0)
    col = lax.broadcasted_iota(jnp.int32, (tq, tk), 1)
    return col + key_offset < row


def _sb_specs(S, n_g, n_h):
    base = 2 * n_g
    q_spec = pl.BlockSpec((S, LANE), lambda h: (0, base + h))
    k_spec = pl.BlockSpec((S, LANE), lambda h: (0, base + n_h + h))
    v_spec = pl.BlockSpec((S, LANE), lambda h: (0, base + 2 * n_h + h))
    gain_spec = pl.BlockSpec((1, LANE), lambda h: (0, n_g + h))
    head_spec = pl.BlockSpec((S, LANE), lambda h: (0, h))
    return q_spec, k_spec, v_spec, gain_spec, head_spec


def _sb_fwd(proj, out_gain, n_g, n_h, name):
    S = proj.shape[0]
    TQ, TK, NQ, KPQ = _sb_tiles(S)
    scale = LANE ** -0.5
    q_spec, k_spec, v_spec, gain_spec, head_spec = _sb_specs(S, n_g, n_h)

    def body(q_ref, k_ref, v_ref, og_ref, o_ref, on_ref, ls_ref, qb, kb, vb):
        qb[...] = q_ref[...].astype(MXU_DTYPE)
        kb[...] = k_ref[...].astype(MXU_DTYPE)
        vb[...] = v_ref[...].astype(MXU_DTYPE)
        after = _triangle(TK, lambda r, c: r > c)

        def block(qi, j, ctail, acc, key_offset):
            z = _dot(qi, kb[_rows(j, TK), :], NT) * scale
            lb, l1m = _log_sigmoids(z)
            if key_offset is not None:
                strict = _strictly_before(TQ, TK, key_offset)
                l1m = jnp.where(strict, l1m, 0.0)
            a = jnp.exp(lb + ctail + _tri_sum(l1m, after))
            if key_offset is not None:
                a = jnp.where(strict, a, 0.0)
            acc = acc + _dot(a.astype(MXU_DTYPE), vb[_rows(j, TK), :], NN)
            return ctail + jnp.sum(l1m, axis=1, keepdims=True), acc

        def q_loop(i, carry):
            qi = qb[_rows(i, TQ), :]
            state = (jnp.zeros((TQ, 1), F32), jnp.zeros((TQ, LANE), F32))
            for d in reversed(range(KPQ)):
                state = block(qi, i * KPQ + d, state[0], state[1], d * TK)
            ctail, acc = lax.fori_loop(
                0, i * KPQ, lambda jj, st: block(qi, i * KPQ - 1 - jj, st[0], st[1], None), state)
            ls_ref[_rows(i, TQ), :] = jnp.broadcast_to(ctail, (TQ, LANE))
            o_ref[_rows(i, TQ), :] = acc
            r = lax.rsqrt(_mean1(acc * acc) + EPS)
            on_ref[_rows(i, TQ), :] = (acc * r * og_ref[...]).astype(on_ref.dtype)
            return carry

        lax.fori_loop(0, NQ, q_loop, 0)

    return pl.pallas_call(
        body, name=name, grid=(n_h,),
        in_specs=[q_spec, k_spec, v_spec, gain_spec],
        out_specs=[head_spec, head_spec, head_spec],
        out_shape=[jax.ShapeDtypeStruct((S, n_h * LANE), F32), jax.ShapeDtypeStruct((S, n_h * LANE), MXU_DTYPE),
                   jax.ShapeDtypeStruct((S, n_h * LANE), F32)],
        scratch_shapes=[pltpu.VMEM((S, LANE), MXU_DTYPE)] * 3,
        compiler_params=_params("parallel"),
    )(proj, proj, proj, out_gain)


def _sb_bwd(proj, o_sb, l_sum, d_on, out_gain, n_g, n_h, name):
    S = proj.shape[0]
    TQ, TK, NQ, KPQ = _sb_tiles(S)
    scale = LANE ** -0.5
    q_spec, k_spec, v_spec, gain_spec, head_spec = _sb_specs(S, n_g, n_h)
    dn_spec = pl.BlockSpec((S, LANE), lambda h: (0, n_g + h))
    dgain_spec = pl.BlockSpec((1, LANE), lambda h: (0, h))

    def body(q_ref, k_ref, v_ref, o_ref, ls_ref, dn_ref, og_ref, dq_ref, dk_ref, dv_ref, dog_ref,
             qb, kb, vb, dob, dk_acc, dv_acc):
        qb[...] = q_ref[...].astype(MXU_DTYPE)
        kb[...] = k_ref[...].astype(MXU_DTYPE)
        vb[...] = v_ref[...].astype(MXU_DTYPE)
        o, dn = o_ref[...], dn_ref[...]
        r = lax.rsqrt(_mean1(o * o) + EPS)
        oh = o * r
        dog_ref[...] = _sum0(dn * oh)
        dhn = dn * og_ref[...]
        dob[...] = (r * (dhn - oh * _mean1(dhn * oh))).astype(MXU_DTYPE)
        dk_acc[...] = jnp.zeros_like(dk_acc)
        dv_acc[...] = jnp.zeros_like(dv_acc)

        up_to = _triangle(TK, lambda r, c: r <= c)
        before = _triangle(TK, lambda r, c: r < c)

        def block(qi, doi, ltot, j, cl, cdl, dq, key_offset):
            kj, vj = kb[_rows(j, TK), :], vb[_rows(j, TK), :]
            z = _dot(qi, kj, NT) * scale
            lb, l1m_all = _log_sigmoids(z)
            l1m = l1m_all
            if key_offset is not None:
                strict = _strictly_before(TQ, TK, key_offset)
                l1m = jnp.where(strict, l1m_all, 0.0)
            a = jnp.exp(lb + (ltot - (cl + _tri_sum(l1m, up_to))))
            if key_offset is not None:
                a = jnp.where(strict, a, 0.0)
            dl = _dot(doi, vj, NT) * a
            d_l1m = cdl + _tri_sum(dl, before)
            dz = dl * jnp.exp(l1m_all) - jnp.exp(lb) * d_l1m
            if key_offset is not None:
                dz = jnp.where(strict, dz, 0.0)
            dzs = (dz * scale).astype(MXU_DTYPE)
            dq = dq + _dot(dzs, kj, NN)
            dk_acc[_rows(j, TK), :] += _dot(dzs, qi, TN)
            dv_acc[_rows(j, TK), :] += _dot(a.astype(MXU_DTYPE), doi, TN)
            return (cl + jnp.sum(l1m, axis=1, keepdims=True), cdl + jnp.sum(dl, axis=1, keepdims=True), dq)

        def q_loop(i, carry):
            qi, doi = qb[_rows(i, TQ), :], dob[_rows(i, TQ), :]
            ltot = ls_ref[_rows(i, TQ), :][:, :1]
            zero_col = jnp.zeros((TQ, 1), F32)
            state = lax.fori_loop(
                0, i * KPQ, lambda j, st: block(qi, doi, ltot, j, st[0], st[1], st[2], None),
                (zero_col, zero_col, jnp.zeros((TQ, LANE), F32)))
            for d in range(KPQ):
                state = block(qi, doi, ltot, i * KPQ + d, state[0], state[1], state[2], d * TK)
            dq_ref[_rows(i, TQ), :] = state[2].astype(dq_ref.dtype)
            return carry

        lax.fori_loop(0, NQ, q_loop, 0)
        dk_ref[...] = dk_acc[...].astype(dk_ref.dtype)
        dv_ref[...] = dv_acc[...].astype(dv_ref.dtype)

    W = n_h * LANE
    return pl.pallas_call(
        body, name=name, grid=(n_h,),
        in_specs=[q_spec, k_spec, v_spec, head_spec, head_spec, dn_spec, gain_spec],
        out_specs=[head_spec, head_spec, head_spec, dgain_spec],
        out_shape=[jax.ShapeDtypeStruct((S, W), MXU_DTYPE)] * 3 + [jax.ShapeDtypeStruct((1, W), F32)],
        scratch_shapes=[pltpu.VMEM((S, LANE), MXU_DTYPE)] * 4 + [pltpu.VMEM((S, LANE), F32)] * 2,
        compiler_params=_params("parallel"),
    )(proj, proj, proj, o_sb, l_sum, d_on, out_gain)


def _mod_part(c_all, w_ada, b_ada_cols, name):
    B, K = c_all.shape
    N = w_ada.shape[1]
    tn = _tile(N, 512)

    def body(c_ref, w_ref, b_ref, o_ref):
        cv = c_ref[...]
        ca = (cv * jax.nn.sigmoid(cv)).astype(MXU_DTYPE)
        o_ref[...] = _dot(ca, w_ref[...].astype(MXU_DTYPE), NN) + b_ref[...]

    return pl.pallas_call(
        body, name=name, grid=(N // tn,),
        in_specs=[pl.BlockSpec((B, K), lambda j: (0, 0)), pl.BlockSpec((K, tn), lambda j: (0, j)),
                  pl.BlockSpec((1, tn), lambda j: (0, j))],
        out_specs=pl.BlockSpec((B, tn), lambda j: (0, j)),
        out_shape=jax.ShapeDtypeStruct((B, N), F32), compiler_params=_params("parallel"))(c_all, w_ada, b_ada_cols)


def _adamw_math(w, g, m, v):
    m = ADAM_B1 * m + (1.0 - ADAM_B1) * g
    v = ADAM_B2 * v + (1.0 - ADAM_B2) * (g * g)
    m_hat = m / (1.0 - ADAM_B1 ** ADAM_STEP)
    v_hat = v / (1.0 - ADAM_B2 ** ADAM_STEP)
    delta = -ADAM_LR * (m_hat / (jnp.sqrt(v_hat) + ADAM_EPS) + ADAM_WD * w)
    return delta, m, v


def _adamw(w, g, m, v, name):
    R, C = w.shape
    tr = _tile(R, max(8, (1 << 19) // C), 8)
    spec = pl.BlockSpec((tr, C), lambda i: (i, 0))

    def body(w_ref, g_ref, m_ref, v_ref, d_ref, mo_ref, vo_ref):
        d_ref[...], mo_ref[...], vo_ref[...] = _adamw_math(w_ref[...], g_ref[...], m_ref[...], v_ref[...])

    return pl.pallas_call(body, name=name, grid=(R // tr,), in_specs=[spec] * 4, out_specs=[spec] * 3,
                          out_shape=[jax.ShapeDtypeStruct((R, C), F32)] * 3, compiler_params=_params("parallel"))(w, g, m, v)


def _adamw_ada(c_all, dmod_cols, w, m, v, name):
    K, N = w.shape
    B = c_all.shape[0]
    tk, tn = _tile(K, 512), _tile(N, 1024)
    spec = pl.BlockSpec((tk, tn), lambda i, j: (i, j))

    def body(c_ref, dm_ref, w_ref, m_ref, v_ref, g_ref, d_ref, mo_ref, vo_ref):
        cv = c_ref[...]
        ca = (cv * jax.nn.sigmoid(cv)).astype(MXU_DTYPE)
        g = _dot(ca, dm_ref[...].astype(MXU_DTYPE), TN)
        g_ref[...] = g
        d_ref[...], mo_ref[...], vo_ref[...] = _adamw_math(w_ref[...], g, m_ref[...], v_ref[...])

    return pl.pallas_call(
        body, name=name, grid=(K // tk, N // tn),
        in_specs=[pl.BlockSpec((B, tk), lambda i, j: (0, i)), pl.BlockSpec((B, tn), lambda i, j: (0, j)), spec, spec, spec],
        out_specs=[spec] * 4, out_shape=[jax.ShapeDtypeStruct((K, N), F32)] * 4,
        compiler_params=_params("parallel", "parallel"))(c_all, dmod_cols, w, m, v)


def _sum_devices(gathered, n_dev, name):
    R = gathered.shape[0] // n_dev
    C = gathered.shape[1]
    tr = _tile(R, 512, 8)
    n_blk = R // tr

    def body(*refs):
        acc = refs[0][...]
        for r in refs[1:n_dev]:
            acc = acc + r[...]
        refs[n_dev][...] = acc

    in_specs = [pl.BlockSpec((tr, C), functools.partial(lambda i, d: (d * n_blk + i, 0), d=d)) for d in range(n_dev)]
    return pl.pallas_call(body, name=name, grid=(n_blk,), in_specs=in_specs,
                          out_specs=pl.BlockSpec((tr, C), lambda i: (i, 0)),
                          out_shape=jax.ShapeDtypeStruct((R, C), F32), compiler_params=_params("parallel"))(*([gathered] * n_dev))


def _place():
    x, y, c = lax.axis_index("x"), lax.axis_index("y"), lax.axis_index("c")
    return x, y, c


def _allgather8(blk, name):
    m_per, n = blk.shape

    def body(x_ref, out_ref, send_sems, recv_sems, local_sem):
        x, y, c = _place()
        me, sibling = (x, y, c), (x, y, 1 - c)
        chips = [(1 - x, y), (x, 1 - y), (1 - x, 1 - y)]

        def rows(px, py, pc):
            return out_ref.at[pl.ds((4 * px + 2 * py + pc) * m_per, m_per), :]

        def copy(k, block, to, src=None):
            return pltpu.make_async_remote_copy(
                src_ref=rows(*block) if src is None else src, dst_ref=rows(*block),
                send_sem=send_sems.at[k], recv_sem=recv_sems.at[k], device_id=to, device_id_type=MESH)

        mine = pltpu.make_async_copy(x_ref, rows(*me), local_sem)
        mine.start()
        first = [copy(0, me, sibling, src=x_ref)]
        first += [copy(1 + j, me, (*chip, c), src=x_ref) for j, chip in enumerate(chips)]
        for cp in first:
            cp.start()
        passed = [copy(4 + j, (*chip, c), sibling) for j, chip in enumerate(chips)]
        for j, chip in enumerate(chips):
            copy(1 + j, (*chip, c), me).wait_recv()
            passed[j].start()
        copy(0, sibling, me).wait_recv()
        for j, chip in enumerate(chips):
            copy(4 + j, (*chip, 1 - c), me).wait_recv()
        for cp in first + passed:
            cp.wait_send()
        mine.wait()

    return pl.pallas_call(
        body, name=name,
        out_shape=jax.ShapeDtypeStruct((8 * m_per, n), blk.dtype),
        in_specs=[pl.BlockSpec(memory_space=pltpu.VMEM)],
        out_specs=pl.BlockSpec(memory_space=pltpu.VMEM),
        scratch_shapes=[pltpu.SemaphoreType.DMA((7,)), pltpu.SemaphoreType.DMA((7,)), pltpu.SemaphoreType.DMA],
        compiler_params=pltpu.CompilerParams(vmem_limit_bytes=V7X_VMEM_LIMIT),
    )(blk)


class _Sharded:
    def __init__(self, shard_shape, by_cols):
        r, c = shard_shape
        self.by_cols = by_cols
        self.full = (r, N_CHIPS * c) if by_cols else (N_CHIPS * r, c)
        self.shard = (r, c)
        self.half_rows = r // 2
        self.half = (r // 2, c)

    def shard_of(self, ref, k):
        r, c = self.shard
        return ref.at[:, pl.ds(k * c, c)] if self.by_cols else ref.at[pl.ds(k * r, r), :]

    def half_of(self, ref, k, hc):
        r, c = self.shard
        h = self.half_rows
        if self.by_cols:
            return ref.at[pl.ds(hc * h, h), pl.ds(k * c, c)]
        return ref.at[pl.ds(k * r + hc * h, h), :]

    def piece_of(self, ref, k, hc, p):
        r, c = self.shard
        h = self.half_rows
        q = h // 2
        if self.by_cols:
            return ref.at[pl.ds(hc * h + p * q, q), pl.ds(k * c, c)]
        return ref.at[pl.ds(k * r + hc * h + p * q, q), :]

    def half_of_shard(self, ref, hc):
        return ref.at[pl.ds(hc * self.half_rows, self.half_rows), :]

    def part_of_halves(self, ref, k):
        r, c = self.shard
        h = self.half_rows
        return ref.at[:, pl.ds(k * c, c)] if self.by_cols else ref.at[pl.ds(k * h, h), :]


def _on_each_place(x, y, c, fn, by_chip=True, by_core=True):
    q = 2 * x + y
    for k in range(N_CHIPS if by_chip else 1):
        for cc in range(2 if by_core else 1):
            cond = None
            if by_chip:
                cond = q == k
            if by_core:
                cond = (c == cc) if cond is None else jnp.logical_and(cond, c == cc)
            pl.when(cond)(functools.partial(fn, k, cc))


def _chip_id(k, c):
    return (k // 2, k % 2, c)


def _handshake(peers):
    barrier = pltpu.get_barrier_semaphore()
    for peer in peers:
        pl.semaphore_signal(barrier, inc=1, device_id=peer, device_id_type=MESH)
    pl.semaphore_wait(barrier, len(peers))


def _on_sequencer(body, inputs, out_structs, n_copies, peers_of, name, collective_id, return_inputs=False):
    in_refs = [jax.new_ref(a, memory_space=pltpu.MemorySpace.HBM) for a in inputs]
    out_refs = [jax.empty_ref(s, memory_space=pltpu.MemorySpace.HBM) for s in out_structs]

    @pl.kernel(mesh=plsc.ScalarSubcoreMesh(axis_name="sequencer", num_cores=1), name=name,
               scratch_types=(pltpu.SemaphoreType.DMA((n_copies,)), pltpu.SemaphoreType.DMA((n_copies,))),
               compiler_params=pltpu.CompilerParams(collective_id=collective_id))
    def launch(send_sems, recv_sems):
        x, y, c = _place()
        _handshake(peers_of(x, y, c))
        body(in_refs, out_refs, send_sems, recv_sems, x, y, c)

    launch()
    return [r[...] for r in (in_refs if return_inputs else out_refs)]


def _sibling(x, y, c):
    return [(x, y, 1 - c)]


def _same_core_of_other_chips(x, y, c):
    return [(1 - x, y, c), (x, 1 - y, c), (1 - x, 1 - y, c)]


GATHER_COPIES = 8


def _gather_weights(shards, geoms, name, collective_id):
    n_w = len(shards)
    s_refs = [jax.new_ref(s, memory_space=pltpu.MemorySpace.HBM) for s in shards]
    f_refs = [jax.empty_ref(jax.ShapeDtypeStruct(g.full, WIRE_DTYPE), memory_space=pltpu.MemorySpace.HBM) for g in geoms]
    FLIP_X, FLIP_Y, FLIP_BOTH = FLIPS

    @pl.kernel(mesh=plsc.ScalarSubcoreMesh(axis_name="sequencer", num_cores=1), name=name,
               scratch_types=(pltpu.SemaphoreType.DMA((GATHER_COPIES * n_w,)), pltpu.SemaphoreType.DMA((GATHER_COPIES * n_w,)),
                              pltpu.SemaphoreType.DMA((n_w,))),
               compiler_params=pltpu.CompilerParams(collective_id=collective_id))
    def launch(send_sems, recv_sems, local_sems):
        x, y, c = _place()
        _handshake([(x, y, 1 - c), (1 - x, y, c), (x, 1 - y, c)])

        def at_place(k, cc):
            kx, ky, kd = k ^ FLIP_X, k ^ FLIP_Y, k ^ FLIP_BOTH
            sibling = _chip_id(k, 1 - cc)
            started = []

            def copy(i, slot, src, dst, to, start=True):
                cp = pltpu.make_async_remote_copy(src_ref=src, dst_ref=dst, send_sem=send_sems.at[GATHER_COPIES * i + slot],
                                                  recv_sem=recv_sems.at[GATHER_COPIES * i + slot], device_id=to, device_id_type=MESH)
                if start:
                    cp.start()
                    started.append(cp)
                return cp

            def landed(i, slot, ref):
                copy(i, slot, ref, ref, _chip_id(k, cc), start=False).wait_recv()

            local = []
            for i, (g, s_ref, f_ref) in enumerate(zip(geoms, s_refs, f_refs)):
                cp = pltpu.make_async_copy(s_ref, g.shard_of(f_ref, k), local_sems.at[i])
                cp.start()
                local.append(cp)
                mine = g.half_of_shard(s_ref, cc)
                copy(i, 0, mine, g.half_of(f_ref, k, cc), _chip_id(kx, cc))
                copy(i, 1, mine, g.half_of(f_ref, k, cc), _chip_id(ky, cc))
            for i, (g, f_ref) in enumerate(zip(geoms, f_refs)):
                from_x = g.half_of(f_ref, kx, cc)
                landed(i, 0, from_x)
                piece = g.piece_of(f_ref, kx, cc, 0)
                copy(i, 2, piece, piece, _chip_id(ky, cc))
                copy(i, 4, from_x, from_x, sibling)
                from_y = g.half_of(f_ref, ky, cc)
                landed(i, 1, from_y)
                piece = g.piece_of(f_ref, ky, cc, 1)
                copy(i, 3, piece, piece, _chip_id(kx, cc))
                copy(i, 5, from_y, from_y, sibling)
            for i, (g, f_ref) in enumerate(zip(geoms, f_refs)):
                for p, slot in ((0, 2), (1, 3)):
                    piece = g.piece_of(f_ref, kd, cc, p)
                    landed(i, slot, piece)
                    copy(i, 6 + p, piece, piece, sibling)
            for i, (g, f_ref) in enumerate(zip(geoms, f_refs)):
                landed(i, 4, g.half_of(f_ref, kx, 1 - cc))
                landed(i, 5, g.half_of(f_ref, ky, 1 - cc))
                for p in range(2):
                    landed(i, 6 + p, g.piece_of(f_ref, kd, 1 - cc, p))
            for cp in started:
                cp.wait_send()
            for cp in local:
                cp.wait()

        _on_each_place(x, y, c, at_place)

    launch()
    return [f_ref[...] for f_ref in f_refs]


def _swap_core_halves(grads, geoms, name, collective_id):
    n_cp = sum(1 if g.by_cols else N_CHIPS for g in geoms)

    def body(g_refs, t_refs, send_sems, recv_sems, x, y, c):

        def at_place(_, cc):
            def pairs(hc):
                out = []
                for g, g_ref, t_ref in zip(geoms, g_refs, t_refs):
                    if g.by_cols:
                        out.append((g_ref.at[pl.ds(hc * g.half_rows, g.half_rows), :], t_ref))
                    else:
                        out += [(g.half_of(g_ref, k, hc), g.part_of_halves(t_ref, k)) for k in range(N_CHIPS)]
                return out

            sends = [pltpu.make_async_remote_copy(src_ref=src, dst_ref=dst, send_sem=send_sems.at[n],
                                                  recv_sem=recv_sems.at[n], device_id=(x, y, 1 - cc), device_id_type=MESH)
                     for n, (src, dst) in enumerate(pairs(1 - cc))]
            for cp in sends:
                cp.start()
            for n, (src, dst) in enumerate(pairs(cc)):
                pltpu.make_async_remote_copy(src_ref=src, dst_ref=dst, send_sem=send_sems.at[n], recv_sem=recv_sems.at[n],
                                             device_id=(x, y, cc), device_id_type=MESH).wait_recv()
            for cp in sends:
                cp.wait_send()

        _on_each_place(x, y, c, at_place, by_chip=False)

    return _on_sequencer(body, grads, [jax.ShapeDtypeStruct((g.full[0] // 2, g.full[1]), F32) for g in geoms],
                         n_cp, _sibling, name, collective_id)


def _scatter_chip_sums(sums, geoms, name, collective_id):
    def body(s_refs, r_refs, send_sems, recv_sems, x, y, c):

        def at_place(k, _):
            sends = []
            for i, (g, s_ref, r_ref) in enumerate(zip(geoms, s_refs, r_refs)):
                for j, flip in enumerate(FLIPS):
                    kk = k ^ flip
                    cp = pltpu.make_async_remote_copy(
                        src_ref=g.part_of_halves(s_ref, kk), dst_ref=r_ref.at[j], send_sem=send_sems.at[3 * i + j],
                        recv_sem=recv_sems.at[3 * i + j], device_id=(kk // 2, kk % 2, c), device_id_type=MESH)
                    cp.start()
                    sends.append(cp)
            for i, (g, s_ref, r_ref) in enumerate(zip(geoms, s_refs, r_refs)):
                for j in range(len(FLIPS)):
                    pltpu.make_async_remote_copy(
                        src_ref=g.part_of_halves(s_ref, k), dst_ref=r_ref.at[j], send_sem=send_sems.at[3 * i + j],
                        recv_sem=recv_sems.at[3 * i + j], device_id=(x, y, c), device_id_type=MESH).wait_recv()
            for cp in sends:
                cp.wait_send()

        _on_each_place(x, y, c, at_place, by_core=False)

    return _on_sequencer(body, sums, [jax.ShapeDtypeStruct((len(FLIPS),) + g.half, WIRE_DTYPE) for g in geoms],
                         len(FLIPS) * len(sums), _same_core_of_other_chips, name, collective_id)


def _share_reduced_halves(reduced, geoms, name, collective_id):
    def body(out_refs, _, send_sems, recv_sems, x, y, c):

        def at_place(_, cc):
            sends = []
            for i, (g, ref) in enumerate(zip(geoms, out_refs)):
                mine = g.half_of_shard(ref, cc)
                cp = pltpu.make_async_remote_copy(src_ref=mine, dst_ref=mine, send_sem=send_sems.at[i],
                                                  recv_sem=recv_sems.at[i], device_id=(x, y, 1 - cc), device_id_type=MESH)
                cp.start()
                sends.append(cp)
            for i, (g, ref) in enumerate(zip(geoms, out_refs)):
                theirs = g.half_of_shard(ref, 1 - cc)
                pltpu.make_async_remote_copy(src_ref=theirs, dst_ref=theirs, send_sem=send_sems.at[i],
                                             recv_sem=recv_sems.at[i], device_id=(x, y, cc), device_id_type=MESH).wait_recv()
            for cp in sends:
                cp.wait_send()

        _on_each_place(x, y, c, at_place, by_chip=False)

    return _on_sequencer(body, reduced, [], len(reduced), _sibling, name, collective_id, return_inputs=True)


def _chip_sum(place, grad, theirs, g, name):
    RH, C = theirs.shape
    h = g.half_rows
    tr = _tile(h, 256, 16)
    tc = _tile(C, 2048)
    per_half = h // tr

    if g.by_cols:
        grad_map = lambda i, j, p: (p[1] * per_half + i, j)
    else:
        grad_map = lambda i, j, p: ((i // per_half) * 2 * per_half + p[1] * per_half + i % per_half, j)

    def body(p_ref, a_ref, b_ref, o_ref):
        o_ref[...] = (a_ref[...] + b_ref[...]).astype(o_ref.dtype)

    return pl.pallas_call(
        body, name=name,
        grid_spec=pltpu.PrefetchScalarGridSpec(
            num_scalar_prefetch=1, grid=(RH // tr, C // tc),
            in_specs=[pl.BlockSpec((tr, tc), grad_map), pl.BlockSpec((tr, tc), lambda i, j, p: (i, j))],
            out_specs=pl.BlockSpec((tr, tc), lambda i, j, p: (i, j))),
        out_shape=jax.ShapeDtypeStruct((RH, C), WIRE_DTYPE),
        compiler_params=_params("parallel", "parallel"),
    )(place, grad, theirs)


def _reduce_half(place, grad, theirs, others, g, name):
    h, wc = g.half
    tr = _tile(h, 256, 16)
    per_half = h // tr
    if g.by_cols:
        tc = wc
        grad_map = lambda i, p: (p[1] * per_half + i, p[0])
        theirs_map = lambda i, p: (i, p[0])
    else:
        tc = wc
        grad_map = lambda i, p: (p[0] * 2 * per_half + p[1] * per_half + i, 0)
        theirs_map = lambda i, p: (p[0] * per_half + i, 0)

    def body(p_ref, a_ref, b_ref, o0_ref, o1_ref, o2_ref, out_ref):
        acc = a_ref[...] + b_ref[...]
        for o_ref in (o0_ref, o1_ref, o2_ref):
            acc = acc + o_ref[...].astype(F32)
        out_ref[...] = acc

    other_specs = [pl.BlockSpec((None, tr, tc), functools.partial(lambda i, p, j: (j, i, 0), j=j)) for j in range(len(FLIPS))]
    return pl.pallas_call(
        body, name=name,
        grid_spec=pltpu.PrefetchScalarGridSpec(
            num_scalar_prefetch=1, grid=(per_half,),
            in_specs=[pl.BlockSpec((tr, tc), grad_map), pl.BlockSpec((tr, tc), theirs_map)] + other_specs,
            out_specs=pl.BlockSpec((tr, tc), lambda i, p: (p[1] * per_half + i, 0))),
        out_shape=jax.ShapeDtypeStruct(g.shard, F32),
        compiler_params=_params("arbitrary"),
    )(place, grad, theirs, others, others, others)


SMALL = ("b_ada", "norm1_g", "v_norm_g", "w_spatial", "b_spatial", "out_norm_g", "norm2_g", "final_g")
BIG = ("w_in", "w_out", "w_gate", "w_up", "w_down")
BY_COLS = {"w_in": True, "w_out": False, "w_gate": True, "w_up": True, "w_down": False}
ORDER = ("w_ada", "b_ada", "norm1_g", "w_in", "v_norm_g", "w_spatial", "b_spatial", "out_norm_g", "w_out",
         "norm2_g", "w_gate", "w_up", "w_down", "final_g")


def _pack(parts):
    return jnp.concatenate([parts[n].reshape(-1) for n in SMALL]).reshape(-1, LANE)


def _unpack(slab, shapes):
    flat = slab.reshape(-1)
    out, at = {}, 0
    for n in SMALL:
        size = math.prod(shapes[n])
        out[n] = flat[at:at + size].reshape(shapes[n])
        at += size
    return out


def kernel(x, c, w_ada, b_ada, norm1_g, w_in, v_norm_g, w_spatial, b_spatial, out_norm_g, w_out, norm2_g, w_gate, w_up, w_down, final_g, loss_target, m_w_ada, m_b_ada, m_norm1_g, m_w_in, m_v_norm_g, m_w_spatial, m_b_spatial, m_out_norm_g, m_w_out, m_norm2_g, m_w_gate, m_w_up, m_w_down, m_final_g, v_w_ada, v_b_ada, v_norm1_g, v_w_in, v_v_norm_g, v_w_spatial, v_b_spatial, v_out_norm_g, v_w_out, v_norm2_g, v_w_gate, v_w_up, v_w_down, v_final_g):
    weights = dict(w_ada=w_ada, b_ada=b_ada, norm1_g=norm1_g, w_in=w_in, v_norm_g=v_norm_g, w_spatial=w_spatial,
                   b_spatial=b_spatial, out_norm_g=out_norm_g, w_out=w_out, norm2_g=norm2_g, w_gate=w_gate, w_up=w_up,
                   w_down=w_down, final_g=final_g)
    m_in = dict(w_ada=m_w_ada, b_ada=m_b_ada, norm1_g=m_norm1_g, w_in=m_w_in, v_norm_g=m_v_norm_g, w_spatial=m_w_spatial,
                b_spatial=m_b_spatial, out_norm_g=m_out_norm_g, w_out=m_w_out, norm2_g=m_norm2_g, w_gate=m_w_gate,
                w_up=m_w_up, w_down=m_w_down, final_g=m_final_g)
    v_in = dict(w_ada=v_w_ada, b_ada=v_b_ada, norm1_g=v_norm1_g, w_in=v_w_in, v_norm_g=v_v_norm_g, w_spatial=v_w_spatial,
                b_spatial=v_b_spatial, out_norm_g=v_out_norm_g, w_out=v_w_out, norm2_g=v_norm2_g, w_gate=v_w_gate,
                w_up=v_w_up, w_down=v_w_down, final_g=v_final_g)

    S, D = x.shape[1], x.shape[2]
    n_g = v_norm_g.shape[-1] // LANE
    n_h = (D - n_g * LANE) // LANE
    GW = n_g * LANE
    xi, yi, ci = _place()
    chip = 2 * xi + yi
    me = 4 * xi + 2 * yi + ci
    place = jnp.stack([chip, ci]).astype(jnp.int32)

    xs, target = x[0], loss_target[0]
    geoms = [_Sharded(weights[n].shape[1:], BY_COLS[n]) for n in BIG]

    full = {}
    for i, group in enumerate((("w_in",), ("w_out",), ("w_gate", "w_up"), ("w_down",))):
        shards = [_cast(weights[n][0], WIRE_DTYPE, "cast_" + n) for n in group]
        gathered = _gather_weights(shards, [geoms[BIG.index(n)] for n in group], "gather_" + "_".join(group), 1 + i)
        full.update(zip(group, gathered))

    c_pad = jnp.concatenate([c, jnp.zeros((7, D), F32)], axis=0)
    c_all = _allgather8(c_pad, "gather_c")[::8]
    n_ada = w_ada.shape[2]
    b_cols = lax.dynamic_slice(b_ada, (0, chip * n_ada), (1, n_ada))
    mod_parts = _allgather8(_mod_part(c_all, w_ada[0], b_cols, "mod_part"), "gather_mod")
    mod_all = mod_parts.reshape(N_CHIPS, 2, 8, n_ada)[:, 0].transpose(1, 0, 2).reshape(8, N_CHIPS * n_ada)
    mod = lax.dynamic_slice(mod_all, (me, 0), (1, 6 * D))
    shift1, scale1, gate1, shift2, scale2, gate2 = [mod[:, i * D:(i + 1) * D] for i in range(6)]

    b_t = b_spatial[0].T
    h1 = _norm_mod(xs, norm1_g, scale1, shift1, "norm1")
    proj, = _mm("nn", h1, full["w_in"], [F32], "proj")
    on_gm = _gmlp_fwd(proj, v_norm_g, w_spatial[0], b_t, out_norm_g, n_g, "gmlp_fwd")
    o_sb, on_sb, l_sum = _sb_fwd(proj, out_norm_g, n_g, n_h, "sb_fwd")
    o_n = jnp.concatenate([on_gm, on_sb], axis=1)
    attn, = _mm("nn", o_n, full["w_out"], [F32], "attn_out")
    x1, h2 = _residual_norm_mod(xs, attn, gate1, norm2_g, scale2, shift2, "norm2")
    a_g, a_u, f_in = _gate_up(h2, full["w_gate"], full["w_up"], "gate_up")
    f = _mm_ktiled("nn", [(f_in, full["w_down"])], "down")
    dx2, df, d_gate2, d_final_g, loss_part = _final_loss_bwd(x1, f, gate2, final_g.reshape(1, D), target, "final")
    loss = lax.psum(loss_part[0, 0], ("x", "y", "c"))

    geom_of = dict(zip(BIG, geoms))
    grad_out, delta, new_m, new_v = {}, {}, {}, {}

    def swap(group, grads, collective_id):
        return _swap_core_halves(grads, [geom_of[n] for n in group], "swap_" + "_".join(group), collective_id)

    def chip_sums(group, grads, theirs, after):
        return [_chip_sum(place, gr, _then(after, t), geom_of[n], "chip_sum_" + n) for n, gr, t in zip(group, grads, theirs)]

    def scatter(group, sums, collective_id):
        return _scatter_chip_sums(sums, [geom_of[n] for n in group], "scatter_" + "_".join(group), collective_id)

    def reduce_halves(group, grads, theirs, others, after):
        return [_reduce_half(place, gr, t, _then(after, o), geom_of[n], "reduce_" + n)
                for n, gr, t, o in zip(group, grads, theirs, others)]

    def share(group, halves, collective_id):
        return _share_reduced_halves(halves, [geom_of[n] for n in group], "share_" + "_".join(group), collective_id)

    def adamw(group, reduced, after):
        for n, r in zip(group, reduced):
            grad_out[n] = r[None]
            d, mo, vo = _adamw(weights[n][0], _then(after, r), m_in[n][0], v_in[n][0], "adamw_" + n)
            delta[n], new_m[n], new_v[n] = d[None], mo[None], vo[None]
        return d

    g_down = ("w_down",)
    g_ffn = ("w_gate", "w_up")
    g_out = ("w_out",)
    g_in = ("w_in",)

    gr_down = _mm("tn", f_in, df, [F32], "d_w_down", tm=1408, tn=1024)
    th_down = swap(g_down, gr_down, 6)
    d_ag, d_au = _mm("nt", df, full["w_down"], [MXU_DTYPE, MXU_DTYPE], "d_ffn_in", extras=(a_g, a_u),
                     epilogue=_swiglu_bwd_epilogue)
    sm_down = chip_sums(g_down, gr_down, th_down, after=d_ag)
    ot_down = scatter(g_down, sm_down, 7)
    gr_ffn = [_mm("tn", h2, _then(sm_down, d_ag), [F32], "d_w_gate")[0], _mm("tn", h2, d_au, [F32], "d_w_up")[0]]
    th_ffn = swap(g_ffn, gr_ffn, 9)
    dh2 = _mm_ktiled("nt", [(_then(gr_ffn, d_ag), full["w_gate"]), (d_au, full["w_up"])], "d_h2", tn=512)
    sm_ffn = chip_sums(g_ffn, gr_ffn, th_ffn, after=dh2)
    ot_ffn = scatter(g_ffn, sm_ffn, 10)
    hv_down = reduce_halves(g_down, gr_down, th_down, ot_down, after=sm_ffn)
    rd_down = share(g_down, hv_down, 8)
    dx1, d_shift2, d_scale2, d_norm2_g, d_gate1, d_attn = _norm_mod_bwd(
        _then(hv_down, dh2), x1, dx2, norm2_g, scale2, "norm2_bwd", branch=attn, gate=gate1)
    gr_out = _mm("tn", o_n, d_attn, [F32], "d_w_out")
    th_out = swap(g_out, gr_out, 12)
    d_on, = _mm("nt", _then(gr_out, d_attn), full["w_out"], [F32], "d_o")
    dp_gm, d_w_spatial, d_b_t, d_v_norm_g, d_og_gm = _gmlp_bwd(proj, d_on, v_norm_g, w_spatial[0], b_t, out_norm_g, n_g, "gmlp_bwd")
    dq, dk, dv, d_og_sb = _sb_bwd(proj, o_sb, l_sum, _then(dp_gm, d_on), out_norm_g, n_g, n_h, "sb_bwd")
    sm_out = chip_sums(g_out, gr_out, th_out, after=dq)
    ot_out = scatter(g_out, sm_out, 13)
    hv_ffn = reduce_halves(g_ffn, gr_ffn, th_ffn, ot_ffn, after=sm_out)
    rd_ffn = share(g_ffn, hv_ffn, 11)
    dproj = jnp.concatenate([_then(hv_ffn, dp_gm), dq, dk, dv], axis=1)
    gr_in = _mm("tn", h1, dproj, [F32], "d_w_in")
    th_in = swap(g_in, gr_in, 15)
    dh1 = _mm_ktiled("nt", [(_then(gr_in, dproj), full["w_in"])], "d_h1")
    grad_x, d_shift1, d_scale1, d_norm1_g = _norm_mod_bwd(dh1, xs, dx1, norm1_g, scale1, "norm1_bwd")
    sm_in = chip_sums(g_in, gr_in, th_in, after=grad_x)
    ot_in = scatter(g_in, sm_in, 16)
    hv_out = reduce_halves(g_out, gr_out, th_out, ot_out, after=sm_in)
    rd_out = share(g_out, hv_out, 14)

    dmod = jnp.concatenate([d_shift1, d_scale1, d_gate1, d_shift2, d_scale2, d_gate2], axis=1)
    small_parts = dict(b_ada=dmod, norm1_g=d_norm1_g, v_norm_g=d_v_norm_g, w_spatial=d_w_spatial, b_spatial=d_b_t.T,
                       out_norm_g=jnp.concatenate([d_og_gm, d_og_sb], axis=1), norm2_g=d_norm2_g, final_g=d_final_g)
    slab = _then(hv_out, _pack(small_parts))
    rows = slab.shape[0]
    gathered = _allgather8(slab, "gather_small")
    small_shapes = {n: weights[n].shape for n in SMALL}
    small_sum = _sum_devices(gathered, 8, "sum_small")
    dmod_all = gathered.reshape(8, rows * LANE)[:, :6 * D]
    dmod_cols = lax.dynamic_slice(dmod_all, (0, chip * n_ada), (8, n_ada))
    g_ada, d, mo, vo = _adamw_ada(c_all, dmod_cols, w_ada[0], m_w_ada[0], v_w_ada[0], "adamw_w_ada")
    grad_out["w_ada"], delta["w_ada"], new_m["w_ada"], new_v["w_ada"] = g_ada[None], d[None], mo[None], vo[None]
    d_small, mo, vo = _adamw(_pack({n: weights[n] for n in SMALL}), small_sum, _pack({n: m_in[n] for n in SMALL}),
                             _pack({n: v_in[n] for n in SMALL}), "adamw_small")
    for dst, slab_out in ((grad_out, small_sum), (delta, d_small), (new_m, mo), (new_v, vo)):
        dst.update(_unpack(slab_out, small_shapes))
    done = adamw(g_down, rd_down, after=d)
    done = adamw(g_ffn, rd_ffn, after=done)
    done = adamw(g_out, rd_out, after=done)
    hv_in = reduce_halves(g_in, gr_in, th_in, ot_in, after=done)
    adamw(g_in, share(g_in, hv_in, 17), after=done)

    return (loss, grad_x[None], *[grad_out[n] for n in ORDER], *[delta[n] for n in ORDER],
            *[new_m[n] for n in ORDER], *[new_v[n] for n in ORDER])
```

```python
import functools
import math

import jax
import jax.numpy as jnp
from jax import lax
from jax.experimental import pallas as pl
from jax.experimental.pallas import tpu as pltpu
from jax.experimental.pallas import tpu_sc as plsc

F32 = jnp.float32
MXU_DTYPE = jnp.bfloat16
WIRE_DTYPE = jnp.bfloat16
EPS = 1e-6
LANE = 128
V7X_VMEM_LIMIT = 56 * 1024 * 1024
MESH = pl.DeviceIdType.MESH
N_CHIPS = 4
FLIPS = (2, 1, 3)

ADAM_LR = 0.001
ADAM_B1 = 0.9
ADAM_B2 = 0.999
ADAM_EPS = 1e-08
ADAM_WD = 0.01
ADAM_STEP = 10


def _params(*semantics):
    return pltpu.CompilerParams(dimension_semantics=semantics or None, vmem_limit_bytes=V7X_VMEM_LIMIT)


def _tile(dim, pref, unit=LANE):
    best = None
    t = unit
    while t <= min(dim, pref):
        if dim % t == 0:
            best = t
        t += unit
    return best if best is not None else dim


def _then(first, second):
    return lax.optimization_barrier((first, second))[1]


def _sum0(v):
    return jnp.sum(v, axis=0, keepdims=True)


def _mean1(v):
    return jnp.mean(v, axis=-1, keepdims=True)


def _gelu(x):
    return 0.5 * x * (1.0 + lax.erf(x * (1.0 / math.sqrt(2.0))))


def _gelu_grad(x):
    cdf = 0.5 * (1.0 + lax.erf(x * (1.0 / math.sqrt(2.0))))
    return cdf + x * jnp.exp(-0.5 * x * x) * (1.0 / math.sqrt(2.0 * math.pi))


def _dot(a, b, dims):
    return lax.dot_general(a, b, (dims, ((), ())), preferred_element_type=F32)


NN = ((1,), (0,))
NT = ((1,), (1,))
TN = ((0,), (0,))


def _mm(kind, a, b, out_dtypes, name, tm=2048, tn=512, extras=(), epilogue=None):
    if kind == "nn":
        (M, K), N = a.shape, b.shape[1]
    elif kind == "nt":
        (M, K), N = a.shape, b.shape[0]
    else:
        (K, M), N = a.shape, b.shape[1]
    tm, tn = _tile(M, tm), _tile(N, tn)
    a_spec = pl.BlockSpec((K, tm), lambda i, j: (0, i)) if kind == "tn" else pl.BlockSpec((tm, K), lambda i, j: (i, 0))
    b_spec = pl.BlockSpec((tn, K), lambda i, j: (j, 0)) if kind == "nt" else pl.BlockSpec((K, tn), lambda i, j: (0, j))
    mn_spec = pl.BlockSpec((tm, tn), lambda i, j: (i, j))
    dims = {"nn": NN, "nt": NT, "tn": TN}[kind]
    n_extra = len(extras)

    def body(a_ref, b_ref, *rest):
        acc = _dot(a_ref[...], b_ref[...], dims)
        res = (acc,) if epilogue is None else epilogue(acc, *[e[...] for e in rest[:n_extra]])
        for o_ref, r in zip(rest[n_extra:], res):
            o_ref[...] = r.astype(o_ref.dtype)

    outs = pl.pallas_call(
        body, name=name, grid=(M // tm, N // tn),
        in_specs=[a_spec, b_spec] + [mn_spec] * n_extra,
        out_specs=[mn_spec] * len(out_dtypes),
        out_shape=[jax.ShapeDtypeStruct((M, N), d) for d in out_dtypes],
        compiler_params=_params("parallel", "arbitrary"),
    )(a, b, *extras)
    return outs


def _mm_ktiled(kind, pairs, name, tm=2048, tn=1024, tk=1408):
    a0, b0 = pairs[0]
    M, K = a0.shape
    N = b0.shape[1] if kind == "nn" else b0.shape[0]
    tm, tn, tk = _tile(M, tm), _tile(N, tn), _tile(K, tk)
    a_spec = pl.BlockSpec((tm, tk), lambda i, j, k: (i, k))
    b_spec = pl.BlockSpec((tk, tn), lambda i, j, k: (k, j)) if kind == "nn" else pl.BlockSpec((tn, tk), lambda i, j, k: (j, k))
    dims = NN if kind == "nn" else NT
    n_pairs = len(pairs)

    def body(*refs):
        o_ref = refs[2 * n_pairs]
        acc = _dot(refs[0][...], refs[1][...], dims)
        for p in range(1, n_pairs):
            acc = acc + _dot(refs[2 * p][...], refs[2 * p + 1][...], dims)

        @pl.when(pl.program_id(2) == 0)
        def _():
            o_ref[...] = acc

        @pl.when(pl.program_id(2) != 0)
        def _():
            o_ref[...] += acc

    return pl.pallas_call(
        body, name=name, grid=(M // tm, N // tn, K // tk),
        in_specs=[a_spec, b_spec] * n_pairs,
        out_specs=pl.BlockSpec((tm, tn), lambda i, j, k: (i, j)),
        out_shape=jax.ShapeDtypeStruct((M, N), F32),
        compiler_params=_params("parallel", "parallel", "arbitrary"),
    )(*[x for pair in pairs for x in pair])


def _gate_up(h, wg, wu, name):
    (M, K), N = h.shape, wg.shape[1]
    tm, tn = _tile(M, 2048), _tile(N, 512)

    def body(h_ref, wg_ref, wu_ref, ag_ref, au_ref, f_ref):
        hv = h_ref[...]
        ag = _dot(hv, wg_ref[...], NN)
        au = _dot(hv, wu_ref[...], NN)
        ag_ref[...] = ag.astype(ag_ref.dtype)
        au_ref[...] = au.astype(au_ref.dtype)
        f_ref[...] = (ag * jax.nn.sigmoid(ag) * au).astype(f_ref.dtype)

    w_spec = pl.BlockSpec((K, tn), lambda i, j: (0, j))
    mn_spec = pl.BlockSpec((tm, tn), lambda i, j: (i, j))
    return pl.pallas_call(
        body, name=name, grid=(M // tm, N // tn),
        in_specs=[pl.BlockSpec((tm, K), lambda i, j: (i, 0)), w_spec, w_spec],
        out_specs=[mn_spec] * 3,
        out_shape=[jax.ShapeDtypeStruct((M, N), MXU_DTYPE)] * 3,
        compiler_params=_params("parallel", "arbitrary"),
    )(h, wg, wu)


def _swiglu_bwd_epilogue(dfin, ag, au):
    ag, au = ag.astype(F32), au.astype(F32)
    sg = jax.nn.sigmoid(ag)
    d_au = dfin * (ag * sg)
    d_ag = dfin * au * (sg * (1.0 + ag * (1.0 - sg)))
    return d_ag, d_au


def _row_specs(ts, width):
    return pl.BlockSpec((ts, width), lambda i: (i, 0)), pl.BlockSpec((1, width), lambda i: (0, 0))


def _cast(a, dtype, name):
    R, C = a.shape
    tr = _tile(R, 512, 16)
    spec = pl.BlockSpec((tr, C), lambda i: (i, 0))

    def body(a_ref, o_ref):
        o_ref[...] = a_ref[...].astype(o_ref.dtype)

    return pl.pallas_call(body, name=name, grid=(R // tr,), in_specs=[spec], out_specs=spec,
                          out_shape=jax.ShapeDtypeStruct((R, C), dtype), compiler_params=_params("parallel"))(a)


def _norm_mod(x, g, scale, shift, name):
    S, D = x.shape
    ts = _tile(S, 256, 16)
    tile, vec = _row_specs(ts, D)

    def body(x_ref, g_ref, sc_ref, sh_ref, h_ref):
        xv = x_ref[...]
        r = lax.rsqrt(_mean1(xv * xv) + EPS)
        h_ref[...] = ((xv * r) * g_ref[...] * (1.0 + sc_ref[...]) + sh_ref[...]).astype(h_ref.dtype)

    return pl.pallas_call(body, name=name, grid=(S // ts,), in_specs=[tile, vec, vec, vec], out_specs=tile,
                          out_shape=jax.ShapeDtypeStruct((S, D), MXU_DTYPE), compiler_params=_params("parallel"))(x, g, scale, shift)


def _residual_norm_mod(x, attn, gate, g, scale, shift, name):
    S, D = x.shape
    ts = _tile(S, 256, 16)
    tile, vec = _row_specs(ts, D)

    def body(x_ref, a_ref, gate_ref, g_ref, sc_ref, sh_ref, x1_ref, h_ref):
        x1 = x_ref[...] + gate_ref[...] * a_ref[...]
        x1_ref[...] = x1
        r = lax.rsqrt(_mean1(x1 * x1) + EPS)
        h_ref[...] = ((x1 * r) * g_ref[...] * (1.0 + sc_ref[...]) + sh_ref[...]).astype(h_ref.dtype)

    return pl.pallas_call(body, name=name, grid=(S // ts,), in_specs=[tile, tile, vec, vec, vec, vec],
                          out_specs=[tile, tile],
                          out_shape=[jax.ShapeDtypeStruct((S, D), F32), jax.ShapeDtypeStruct((S, D), MXU_DTYPE)],
                          compiler_params=_params("parallel"))(x, attn, gate, g, scale, shift)


def _final_loss_bwd(x1, f, gate2, final_g, target, name):
    S, D = x1.shape
    ts = _tile(S, 256, 16)
    tile, vec = _row_specs(ts, D)
    loss_spec = pl.BlockSpec((1, LANE), lambda i: (0, 0))

    def body(x1_ref, f_ref, gate_ref, g_ref, t_ref, dx2_ref, df_ref, dgate_ref, dg_ref, loss_ref):
        @pl.when(pl.program_id(0) == 0)
        def _():
            dgate_ref[...] = jnp.zeros_like(dgate_ref)
            dg_ref[...] = jnp.zeros_like(dg_ref)
            loss_ref[...] = jnp.zeros_like(loss_ref)

        fv, gate, g = f_ref[...], gate_ref[...], g_ref[...]
        x2 = x1_ref[...] + gate * fv
        r = lax.rsqrt(_mean1(x2 * x2) + EPS)
        xn = x2 * r
        err = xn * g - t_ref[...]
        loss_ref[...] += jnp.broadcast_to(0.5 * _sum0(_mean1(err * err)), loss_ref.shape)
        dy = err * (1.0 / D)
        dg_ref[...] += _sum0(dy * xn)
        dxn = dy * g
        dx2 = r * (dxn - xn * _mean1(dxn * xn))
        dx2_ref[...] = dx2
        dgate_ref[...] += _sum0(dx2 * fv)
        df_ref[...] = (dx2 * gate).astype(df_ref.dtype)

    return pl.pallas_call(
        body, name=name, grid=(S // ts,), in_specs=[tile, tile, vec, vec, tile],
        out_specs=[tile, tile, vec, vec, loss_spec],
        out_shape=[jax.ShapeDtypeStruct((S, D), F32), jax.ShapeDtypeStruct((S, D), MXU_DTYPE),
                   jax.ShapeDtypeStruct((1, D), F32), jax.ShapeDtypeStruct((1, D), F32),
                   jax.ShapeDtypeStruct((1, LANE), F32)],
        compiler_params=_params("arbitrary"),
    )(x1, f, gate2, final_g, target)


def _norm_mod_bwd(dh, xin, dres, g, scale, name, branch=None, gate=None):
    S, D = xin.shape
    ts = _tile(S, 256, 16)
    tile, vec = _row_specs(ts, D)
    with_gate = branch is not None

    def body(*refs):
        if with_gate:
            dh_ref, x_ref, dres_ref, g_ref, sc_ref, br_ref, gate_ref, dx_ref, dshift_ref, dscale_ref, dg_ref, dgate_ref, dbr_ref = refs
            accs = (dshift_ref, dscale_ref, dg_ref, dgate_ref)
        else:
            dh_ref, x_ref, dres_ref, g_ref, sc_ref, dx_ref, dshift_ref, dscale_ref, dg_ref = refs
            accs = (dshift_ref, dscale_ref, dg_ref)

        @pl.when(pl.program_id(0) == 0)
        def _():
            for acc in accs:
                acc[...] = jnp.zeros_like(acc)

        dh_v, xv, g_v = dh_ref[...], x_ref[...], g_ref[...]
        one_sc = 1.0 + sc_ref[...]
        r = lax.rsqrt(_mean1(xv * xv) + EPS)
        xn = xv * r
        dshift_ref[...] += _sum0(dh_v)
        dscale_ref[...] += _sum0(dh_v * (xn * g_v))
        dg_ref[...] += _sum0(dh_v * one_sc * xn)
        dxn = dh_v * (g_v * one_sc)
        dx = dres_ref[...] + r * (dxn - xn * _mean1(dxn * xn))
        dx_ref[...] = dx
        if with_gate:
            dgate_ref[...] += _sum0(dx * br_ref[...])
            dbr_ref[...] = (dx * gate_ref[...]).astype(dbr_ref.dtype)

    ins = [dh, xin, dres, g, scale] + ([branch, gate] if with_gate else [])
    in_specs = [tile, tile, tile, vec, vec] + ([tile, vec] if with_gate else [])
    out_specs = [tile, vec, vec, vec] + ([vec, tile] if with_gate else [])
    out_shape = [jax.ShapeDtypeStruct((S, D), F32)] + [jax.ShapeDtypeStruct((1, D), F32)] * 3
    if with_gate:
        out_shape += [jax.ShapeDtypeStruct((1, D), F32), jax.ShapeDtypeStruct((S, D), MXU_DTYPE)]
    return pl.pallas_call(body, name=name, grid=(S // ts,), in_specs=in_specs, out_specs=out_specs,
                          out_shape=out_shape, compiler_params=_params("arbitrary"))(*ins)


def _causal_weights(ws_ref, wt_ref, n_g):
    row = lax.broadcasted_iota(jnp.int32, (LANE, LANE), 0)
    col = lax.broadcasted_iota(jnp.int32, (LANE, LANE), 1)
    for g in range(n_g):
        wt_ref[g] = jnp.where(col <= row, ws_ref[g], 0.0).astype(wt_ref.dtype)


def _group_layernorm(v):
    xc = v - _mean1(v)
    rstd = lax.rsqrt(_mean1(xc * xc) + EPS)
    return xc * rstd, rstd


def _gmlp_fwd(proj, v_gain, w_s, b_t, out_gain, n_g, name):
    S = proj.shape[0]
    GW = n_g * LANE

    def body(p_ref, vg_ref, ws_ref, bt_ref, og_ref, on_ref, wt_ref):
        @pl.when(pl.program_id(0) == 0)
        def _():
            _causal_weights(ws_ref, wt_ref, n_g)

        for g in range(n_g):
            cols = slice(g * LANE, (g + 1) * LANE)
            u = _gelu(p_ref[:, cols])
            v = _gelu(p_ref[:, GW + g * LANE:GW + (g + 1) * LANE])
            vhat, _ = _group_layernorm(v)
            vln = (vhat * vg_ref[:, cols]).astype(MXU_DTYPE)
            mixed = _dot(wt_ref[g], vln, NN) + bt_ref[:, g:g + 1]
            o = u * mixed
            r = lax.rsqrt(_mean1(o * o) + EPS)
            on_ref[:, cols] = (o * r * og_ref[:, cols]).astype(on_ref.dtype)

    return pl.pallas_call(
        body, name=name, grid=(S // LANE,),
        in_specs=[pl.BlockSpec((LANE, 2 * GW), lambda n: (n, 0)),
                  pl.BlockSpec((1, GW), lambda n: (0, 0)),
                  pl.BlockSpec((n_g, LANE, LANE), lambda n: (0, 0, 0)),
                  pl.BlockSpec((LANE, n_g), lambda n: (0, 0)),
                  pl.BlockSpec((1, GW), lambda n: (0, 0))],
        out_specs=pl.BlockSpec((LANE, GW), lambda n: (n, 0)),
        out_shape=jax.ShapeDtypeStruct((S, GW), MXU_DTYPE),
        scratch_shapes=[pltpu.VMEM((n_g, LANE, LANE), MXU_DTYPE)],
        compiler_params=_params("arbitrary"),
    )(proj, v_gain, w_s, b_t, out_gain)


def _gmlp_bwd(proj, d_on, v_gain, w_s, b_t, out_gain, n_g, name):
    S = proj.shape[0]
    GW = n_g * LANE

    def body(p_ref, dn_ref, vg_ref, ws_ref, bt_ref, og_ref, dp_ref, dws_ref, dbt_ref, dvg_ref, dog_ref, wt_ref):
        @pl.when(pl.program_id(0) == 0)
        def _():
            _causal_weights(ws_ref, wt_ref, n_g)
            dws_ref[...] = jnp.zeros_like(dws_ref)
            dbt_ref[...] = jnp.zeros_like(dbt_ref)
            dvg_ref[...] = jnp.zeros_like(dvg_ref)
            dog_ref[...] = jnp.zeros_like(dog_ref)

        row = lax.broadcasted_iota(jnp.int32, (LANE, LANE), 0)
        col = lax.broadcasted_iota(jnp.int32, (LANE, LANE), 1)
        for g in range(n_g):
            cols = slice(g * LANE, (g + 1) * LANE)
            vcols = slice(GW + g * LANE, GW + (g + 1) * LANE)
            pu, pv = p_ref[:, cols], p_ref[:, vcols]
            u, v = _gelu(pu), _gelu(pv)
            vhat, rstd = _group_layernorm(v)
            gain = vg_ref[:, cols]
            vln = (vhat * gain).astype(MXU_DTYPE)
            mixed = _dot(wt_ref[g], vln, NN) + bt_ref[:, g:g + 1]
            o = u * mixed
            r = lax.rsqrt(_mean1(o * o) + EPS)
            oh = o * r
            dn = dn_ref[:, cols]
            dog_ref[:, cols] += _sum0(dn * oh)
            dhn = dn * og_ref[:, cols]
            d_o = r * (dhn - oh * _mean1(dhn * oh))
            du = d_o * mixed
            dmix = d_o * u
            dbt_ref[:, g:g + 1] += jnp.sum(dmix, axis=1, keepdims=True)
            dmix_b = dmix.astype(MXU_DTYPE)
            dws_ref[g] += jnp.where(col <= row, _dot(dmix_b, vln, NT), 0.0)
            dvln = _dot(wt_ref[g], dmix_b, TN)
            dvg_ref[:, cols] += _sum0(dvln * vhat)
            dxh = dvln * gain
            dv = rstd * (dxh - _mean1(dxh) - vhat * _mean1(dxh * vhat))
            dp_ref[:, cols] = (du * _gelu_grad(pu)).astype(dp_ref.dtype)
            dp_ref[:, vcols] = (dv * _gelu_grad(pv)).astype(dp_ref.dtype)

    return pl.pallas_call(
        body, name=name, grid=(S // LANE,),
        in_specs=[pl.BlockSpec((LANE, 2 * GW), lambda n: (n, 0)),
                  pl.BlockSpec((LANE, GW), lambda n: (n, 0)),
                  pl.BlockSpec((1, GW), lambda n: (0, 0)),
                  pl.BlockSpec((n_g, LANE, LANE), lambda n: (0, 0, 0)),
                  pl.BlockSpec((LANE, n_g), lambda n: (0, 0)),
                  pl.BlockSpec((1, GW), lambda n: (0, 0))],
        out_specs=[pl.BlockSpec((LANE, 2 * GW), lambda n: (n, 0)),
                   pl.BlockSpec((n_g, LANE, LANE), lambda n: (0, 0, 0)),
                   pl.BlockSpec((LANE, n_g), lambda n: (0, 0)),
                   pl.BlockSpec((1, GW), lambda n: (0, 0)),
                   pl.BlockSpec((1, GW), lambda n: (0, 0))],
        out_shape=[jax.ShapeDtypeStruct((S, 2 * GW), MXU_DTYPE),
                   jax.ShapeDtypeStruct((n_g, LANE, LANE), F32),
                   jax.ShapeDtypeStruct((LANE, n_g), F32),
                   jax.ShapeDtypeStruct((1, GW), F32),
                   jax.ShapeDtypeStruct((1, GW), F32)],
        scratch_shapes=[pltpu.VMEM((n_g, LANE, LANE), MXU_DTYPE)],
        compiler_params=_params("arbitrary"),
    )(proj, d_on, v_gain, w_s, b_t, out_gain)


def _tri_sum(v, tri):
    hi = v.astype(MXU_DTYPE)
    lo = (v - hi.astype(F32)).astype(MXU_DTYPE)
    return _dot(hi, tri, NN) + _dot(lo, tri, NN)


def _log_sigmoids(z):
    sp = jnp.log1p(jnp.exp(-jnp.abs(z)))
    return jnp.minimum(z, 0.0) - sp, jnp.minimum(-z, 0.0) - sp


def _rows(i, size):
    return pl.ds(pl.multiple_of(i * size, size), size)


SB_QUERY_TILE = 512
SB_KEY_TILE = 256


def _sb_tiles(S):
    tq = _tile(S, SB_QUERY_TILE)
    tk = _tile(tq, SB_KEY_TILE)
    return tq, tk, S // tq, tq // tk


def _triangle(n, keep):
    row = lax.broadcasted_iota(jnp.int32, (n, n), 0)
    col = lax.broadcasted_iota(jnp.int32, (n, n), 1)
    return jnp.where(keep(row, col), 1.0, 0.0).astype(MXU_DTYPE)


def _strictly_before(tq, tk, key_offset):
    row = lax.broadcasted_iota(jnp.int32, (tq, tk), 0)
    col = lax.broadcasted_iota(jnp.int32, (tq, tk), 1)
    return col + key_offset < row


def _sb_specs(S, n_g, n_h):
    base = 2 * n_g
    q_spec = pl.BlockSpec((S, LANE), lambda h: (0, base + h))
    k_spec = pl.BlockSpec((S, LANE), lambda h: (0, base + n_h + h))
    v_spec = pl.BlockSpec((S, LANE), lambda h: (0, base + 2 * n_h + h))
    gain_spec = pl.BlockSpec((1, LANE), lambda h: (0, n_g + h))
    head_spec = pl.BlockSpec((S, LANE), lambda h: (0, h))
    return q_spec, k_spec, v_spec, gain_spec, head_spec


def _sb_fwd(proj, out_gain, n_g, n_h, name):
    S = proj.shape[0]
    TQ, TK, NQ, KPQ = _sb_tiles(S)
    scale = LANE ** -0.5
    q_spec, k_spec, v_spec, gain_spec, head_spec = _sb_specs(S, n_g, n_h)

    def body(q_ref, k_ref, v_ref, og_ref, o_ref, on_ref, ls_ref, qb, kb, vb):
        qb[...] = q_ref[...].astype(MXU_DTYPE)
        kb[...] = k_ref[...].astype(MXU_DTYPE)
        vb[...] = v_ref[...].astype(MXU_DTYPE)
        after = _triangle(TK, lambda r, c: r > c)

        def block(qi, j, ctail, acc, key_offset):
            z = _dot(qi, kb[_rows(j, TK), :], NT) * scale
            lb, l1m = _log_sigmoids(z)
            if key_offset is not None:
                strict = _strictly_before(TQ, TK, key_offset)
                l1m = jnp.where(strict, l1m, 0.0)
            a = jnp.exp(lb + ctail + _tri_sum(l1m, after))
            if key_offset is not None:
                a = jnp.where(strict, a, 0.0)
            acc = acc + _dot(a.astype(MXU_DTYPE), vb[_rows(j, TK), :], NN)
            return ctail + jnp.sum(l1m, axis=1, keepdims=True), acc

        def q_loop(i, carry):
            qi = qb[_rows(i, TQ), :]
            state = (jnp.zeros((TQ, 1), F32), jnp.zeros((TQ, LANE), F32))
            for d in reversed(range(KPQ)):
                state = block(qi, i * KPQ + d, state[0], state[1], d * TK)
            ctail, acc = lax.fori_loop(
                0, i * KPQ, lambda jj, st: block(qi, i * KPQ - 1 - jj, st[0], st[1], None), state)
            ls_ref[_rows(i, TQ), :] = jnp.broadcast_to(ctail, (TQ, LANE))
            o_ref[_rows(i, TQ), :] = acc
            r = lax.rsqrt(_mean1(acc * acc) + EPS)
            on_ref[_rows(i, TQ), :] = (acc * r * og_ref[...]).astype(on_ref.dtype)
            return carry

        lax.fori_loop(0, NQ, q_loop, 0)

    return pl.pallas_call(
        body, name=name, grid=(n_h,),
        in_specs=[q_spec, k_spec, v_spec, gain_spec],
        out_specs=[head_spec, head_spec, head_spec],
        out_shape=[jax.ShapeDtypeStruct((S, n_h * LANE), F32), jax.ShapeDtypeStruct((S, n_h * LANE), MXU_DTYPE),
                   jax.ShapeDtypeStruct((S, n_h * LANE), F32)],
        scratch_shapes=[pltpu.VMEM((S, LANE), MXU_DTYPE)] * 3,
        compiler_params=_params("parallel"),
    )(proj, proj, proj, out_gain)


def _sb_bwd(proj, o_sb, l_sum, d_on, out_gain, n_g, n_h, name):
    S = proj.shape[0]
    TQ, TK, NQ, KPQ = _sb_tiles(S)
    scale = LANE ** -0.5
    q_spec, k_spec, v_spec, gain_spec, head_spec = _sb_specs(S, n_g, n_h)
    dn_spec = pl.BlockSpec((S, LANE), lambda h: (0, n_g + h))
    dgain_spec = pl.BlockSpec((1, LANE), lambda h: (0, h))

    def body(q_ref, k_ref, v_ref, o_ref, ls_ref, dn_ref, og_ref, dq_ref, dk_ref, dv_ref, dog_ref,
             qb, kb, vb, dob, dk_acc, dv_acc):
        qb[...] = q_ref[...].astype(MXU_DTYPE)
        kb[...] = k_ref[...].astype(MXU_DTYPE)
        vb[...] = v_ref[...].astype(MXU_DTYPE)
        o, dn = o_ref[...], dn_ref[...]
        r = lax.rsqrt(_mean1(o * o) + EPS)
        oh = o * r
        dog_ref[...] = _sum0(dn * oh)
        dhn = dn * og_ref[...]
        dob[...] = (r * (dhn - oh * _mean1(dhn * oh))).astype(MXU_DTYPE)
        dk_acc[...] = jnp.zeros_like(dk_acc)
        dv_acc[...] = jnp.zeros_like(dv_acc)

        up_to = _triangle(TK, lambda r, c: r <= c)
        before = _triangle(TK, lambda r, c: r < c)

        def block(qi, doi, ltot, j, cl, cdl, dq, key_offset):
            kj, vj = kb[_rows(j, TK), :], vb[_rows(j, TK), :]
            z = _dot(qi, kj, NT) * scale
            lb, l1m_all = _log_sigmoids(z)
            l1m = l1m_all
            if key_offset is not None:
                strict = _strictly_before(TQ, TK, key_offset)
                l1m = jnp.where(strict, l1m_all, 0.0)
            a = jnp.exp(lb + (ltot - (cl + _tri_sum(l1m, up_to))))
            if key_offset is not None:
                a = jnp.where(strict, a, 0.0)
            dl = _dot(doi, vj, NT) * a
            d_l1m = cdl + _tri_sum(dl, before)
            dz = dl * jnp.exp(l1m_all) - jnp.exp(lb) * d_l1m
            if key_offset is not None:
                dz = jnp.where(strict, dz, 0.0)
            dzs = (dz * scale).astype(MXU_DTYPE)
            dq = dq + _dot(dzs, kj, NN)
            dk_acc[_rows(j, TK), :] += _dot(dzs, qi, TN)
            dv_acc[_rows(j, TK), :] += _dot(a.astype(MXU_DTYPE), doi, TN)
            return (cl + jnp.sum(l1m, axis=1, keepdims=True), cdl + jnp.sum(dl, axis=1, keepdims=True), dq)

        def q_loop(i, carry):
            qi, doi = qb[_rows(i, TQ), :], dob[_rows(i, TQ), :]
            ltot = ls_ref[_rows(i, TQ), :][:, :1]
            zero_col = jnp.zeros((TQ, 1), F32)
            state = lax.fori_loop(
                0, i * KPQ, lambda j, st: block(qi, doi, ltot, j, st[0], st[1], st[2], None),
                (zero_col, zero_col, jnp.zeros((TQ, LANE), F32)))
            for d in range(KPQ):
                state = block(qi, doi, ltot, i * KPQ + d, state[0], state[1], state[2], d * TK)
            dq_ref[_rows(i, TQ), :] = state[2].astype(dq_ref.dtype)
            return carry

        lax.fori_loop(0, NQ, q_loop, 0)
        dk_ref[...] = dk_acc[...].astype(dk_ref.dtype)
        dv_ref[...] = dv_acc[...].astype(dv_ref.dtype)

    W = n_h * LANE
    return pl.pallas_call(
        body, name=name, grid=(n_h,),
        in_specs=[q_spec, k_spec, v_spec, head_spec, head_spec, dn_spec, gain_spec],
        out_specs=[head_spec, head_spec, head_spec, dgain_spec],
        out_shape=[jax.ShapeDtypeStruct((S, W), MXU_DTYPE)] * 3 + [jax.ShapeDtypeStruct((1, W), F32)],
        scratch_shapes=[pltpu.VMEM((S, LANE), MXU_DTYPE)] * 4 + [pltpu.VMEM((S, LANE), F32)] * 2,
        compiler_params=_params("parallel"),
    )(proj, proj, proj, o_sb, l_sum, d_on, out_gain)


def _mod_part(c_all, w_ada, b_ada_cols, name):
    B, K = c_all.shape
    N = w_ada.shape[1]
    tn = _tile(N, 512)

    def body(c_ref, w_ref, b_ref, o_ref):
        cv = c_ref[...]
        ca = (cv * jax.nn.sigmoid(cv)).astype(MXU_DTYPE)
        o_ref[...] = _dot(ca, w_ref[...].astype(MXU_DTYPE), NN) + b_ref[...]

    return pl.pallas_call(
        body, name=name, grid=(N // tn,),
        in_specs=[pl.BlockSpec((B, K), lambda j: (0, 0)), pl.BlockSpec((K, tn), lambda j: (0, j)),
                  pl.BlockSpec((1, tn), lambda j: (0, j))],
        out_specs=pl.BlockSpec((B, tn), lambda j: (0, j)),
        out_shape=jax.ShapeDtypeStruct((B, N), F32), compiler_params=_params("parallel"))(c_all, w_ada, b_ada_cols)


def _adamw_math(w, g, m, v):
    m = ADAM_B1 * m + (1.0 - ADAM_B1) * g
    v = ADAM_B2 * v + (1.0 - ADAM_B2) * (g * g)
    m_hat = m / (1.0 - ADAM_B1 ** ADAM_STEP)
    v_hat = v / (1.0 - ADAM_B2 ** ADAM_STEP)
    delta = -ADAM_LR * (m_hat / (jnp.sqrt(v_hat) + ADAM_EPS) + ADAM_WD * w)
    return delta, m, v


def _adamw(w, g, m, v, name):
    R, C = w.shape
    tr = _tile(R, max(8, (1 << 19) // C), 8)
    spec = pl.BlockSpec((tr, C), lambda i: (i, 0))

    def body(w_ref, g_ref, m_ref, v_ref, d_ref, mo_ref, vo_ref):
        d_ref[...], mo_ref[...], vo_ref[...] = _adamw_math(w_ref[...], g_ref[...], m_ref[...], v_ref[...])

    return pl.pallas_call(body, name=name, grid=(R // tr,), in_specs=[spec] * 4, out_specs=[spec] * 3,
                          out_shape=[jax.ShapeDtypeStruct((R, C), F32)] * 3, compiler_params=_params("parallel"))(w, g, m, v)


def _adamw_ada(c_all, dmod_cols, w, m, v, name):
    K, N = w.shape
    B = c_all.shape[0]
    tk, tn = _tile(K, 512), _tile(N, 1024)
    spec = pl.BlockSpec((tk, tn), lambda i, j: (i, j))

    def body(c_ref, dm_ref, w_ref, m_ref, v_ref, g_ref, d_ref, mo_ref, vo_ref):
        cv = c_ref[...]
        ca = (cv * jax.nn.sigmoid(cv)).astype(MXU_DTYPE)
        g = _dot(ca, dm_ref[...].astype(MXU_DTYPE), TN)
        g_ref[...] = g
        d_ref[...], mo_ref[...], vo_ref[...] = _adamw_math(w_ref[...], g, m_ref[...], v_ref[...])

    return pl.pallas_call(
        body, name=name, grid=(K // tk, N // tn),
        in_specs=[pl.BlockSpec((B, tk), lambda i, j: (0, i)), pl.BlockSpec((B, tn), lambda i, j: (0, j)), spec, spec, spec],
        out_specs=[spec] * 4, out_shape=[jax.ShapeDtypeStruct((K, N), F32)] * 4,
        compiler_params=_params("parallel", "parallel"))(c_all, dmod_cols, w, m, v)


def _sum_devices(gathered, n_dev, name):
    R = gathered.shape[0] // n_dev
    C = gathered.shape[1]
    tr = _tile(R, 512, 8)
    n_blk = R // tr

    def body(*refs):
        acc = refs[0][...]
        for r in refs[1:n_dev]:
            acc = acc + r[...]
        refs[n_dev][...] = acc

    in_specs = [pl.BlockSpec((tr, C), functools.partial(lambda i, d: (d * n_blk + i, 0), d=d)) for d in range(n_dev)]
    return pl.pallas_call(body, name=name, grid=(n_blk,), in_specs=in_specs,
                          out_specs=pl.BlockSpec((tr, C), lambda i: (i, 0)),
                          out_shape=jax.ShapeDtypeStruct((R, C), F32), compiler_params=_params("parallel"))(*([gathered] * n_dev))


def _place():
    x, y, c = lax.axis_index("x"), lax.axis_index("y"), lax.axis_index("c")
    return x, y, c


def _allgather8(blk, name):
    m_per, n = blk.shape

    def body(x_ref, out_ref, send_sems, recv_sems, local_sem):
        x, y, c = _place()
        me, sibling = (x, y, c), (x, y, 1 - c)
        chips = [(1 - x, y), (x, 1 - y), (1 - x, 1 - y)]

        def rows(px, py, pc):
            return out_ref.at[pl.ds((4 * px + 2 * py + pc) * m_per, m_per), :]

        def copy(k, block, to, src=None):
            return pltpu.make_async_remote_copy(
                src_ref=rows(*block) if src is None else src, dst_ref=rows(*block),
                send_sem=send_sems.at[k], recv_sem=recv_sems.at[k], device_id=to, device_id_type=MESH)

        mine = pltpu.make_async_copy(x_ref, rows(*me), local_sem)
        mine.start()
        first = [copy(0, me, sibling, src=x_ref)]
        first += [copy(1 + j, me, (*chip, c), src=x_ref) for j, chip in enumerate(chips)]
        for cp in first:
            cp.start()
        passed = [copy(4 + j, (*chip, c), sibling) for j, chip in enumerate(chips)]
        for j, chip in enumerate(chips):
            copy(1 + j, (*chip, c), me).wait_recv()
            passed[j].start()
        copy(0, sibling, me).wait_recv()
        for j, chip in enumerate(chips):
            copy(4 + j, (*chip, 1 - c), me).wait_recv()
        for cp in first + passed:
            cp.wait_send()
        mine.wait()

    return pl.pallas_call(
        body, name=name,
        out_shape=jax.ShapeDtypeStruct((8 * m_per, n), blk.dtype),
        in_specs=[pl.BlockSpec(memory_space=pltpu.VMEM)],
        out_specs=pl.BlockSpec(memory_space=pltpu.VMEM),
        scratch_shapes=[pltpu.SemaphoreType.DMA((7,)), pltpu.SemaphoreType.DMA((7,)), pltpu.SemaphoreType.DMA],
        compiler_params=pltpu.CompilerParams(vmem_limit_bytes=V7X_VMEM_LIMIT),
    )(blk)


class _Sharded:
    def __init__(self, shard_shape, by_cols):
        r, c = shard_shape
        self.by_cols = by_cols
        self.full = (r, N_CHIPS * c) if by_cols else (N_CHIPS * r, c)
        self.shard = (r, c)
        self.half_rows = r // 2
        self.half = (r // 2, c)

    def shard_of(self, ref, k):
        r, c = self.shard
        return ref.at[:, pl.ds(k * c, c)] if self.by_cols else ref.at[pl.ds(k * r, r), :]

    def half_of(self, ref, k, hc):
        r, c = self.shard
        h = self.half_rows
        if self.by_cols:
            return ref.at[pl.ds(hc * h, h), pl.ds(k * c, c)]
        return ref.at[pl.ds(k * r + hc * h, h), :]

    def chunk_of(self, ref, k, hc, ch, n):
        r, c = self.shard
        h = self.half_rows
        q = h // n
        if self.by_cols:
            return ref.at[pl.ds(hc * h + ch * q, q), pl.ds(k * c, c)]
        return ref.at[pl.ds(k * r + hc * h + ch * q, q), :]

    def chunk_of_shard(self, ref, hc, ch, n):
        q = self.half_rows // n
        return ref.at[pl.ds(hc * self.half_rows + ch * q, q), :]

    def half_of_shard(self, ref, hc):
        return ref.at[pl.ds(hc * self.half_rows, self.half_rows), :]

    def part_of_halves(self, ref, k):
        r, c = self.shard
        h = self.half_rows
        return ref.at[:, pl.ds(k * c, c)] if self.by_cols else ref.at[pl.ds(k * h, h), :]


def _on_each_place(x, y, c, fn, by_chip=True, by_core=True):
    q = 2 * x + y
    for k in range(N_CHIPS if by_chip else 1):
        for cc in range(2 if by_core else 1):
            cond = None
            if by_chip:
                cond = q == k
            if by_core:
                cond = (c == cc) if cond is None else jnp.logical_and(cond, c == cc)
            pl.when(cond)(functools.partial(fn, k, cc))


def _chip_id(k, c):
    return (k // 2, k % 2, c)


def _handshake(peers):
    barrier = pltpu.get_barrier_semaphore()
    for peer in peers:
        pl.semaphore_signal(barrier, inc=1, device_id=peer, device_id_type=MESH)
    pl.semaphore_wait(barrier, len(peers))


def _on_sequencer(body, inputs, out_structs, n_copies, peers_of, name, collective_id, return_inputs=False):
    in_refs = [jax.new_ref(a, memory_space=pltpu.MemorySpace.HBM) for a in inputs]
    out_refs = [jax.empty_ref(s, memory_space=pltpu.MemorySpace.HBM) for s in out_structs]

    @pl.kernel(mesh=plsc.ScalarSubcoreMesh(axis_name="sequencer", num_cores=1), name=name,
               scratch_types=(pltpu.SemaphoreType.DMA((n_copies,)), pltpu.SemaphoreType.DMA((n_copies,))),
               compiler_params=pltpu.CompilerParams(collective_id=collective_id))
    def launch(send_sems, recv_sems):
        x, y, c = _place()
        _handshake(peers_of(x, y, c))
        body(in_refs, out_refs, send_sems, recv_sems, x, y, c)

    launch()
    return [r[...] for r in (in_refs if return_inputs else out_refs)]


def _sibling(x, y, c):
    return [(x, y, 1 - c)]


def _same_core_of_other_chips(x, y, c):
    return [(1 - x, y, c), (x, 1 - y, c), (1 - x, 1 - y, c)]


GATHER_CHUNKS = 4
GATHER_COPIES = 6 * GATHER_CHUNKS


def _gather_weights(shards, geoms, name, collective_id):
    n_w = len(shards)
    n_ch, n_relay = GATHER_CHUNKS, GATHER_CHUNKS // 2
    s_refs = [jax.new_ref(s, memory_space=pltpu.MemorySpace.HBM) for s in shards]
    f_refs = [jax.empty_ref(jax.ShapeDtypeStruct(g.full, WIRE_DTYPE), memory_space=pltpu.MemorySpace.HBM) for g in geoms]
    FLIP_X, FLIP_Y, FLIP_BOTH = FLIPS
    TO_X, TO_Y, RELAY_TO_Y, RELAY_TO_X, ON_X, ON_Y, ON_DIAG = 0, n_ch, 2 * n_ch, 2 * n_ch + n_relay, 3 * n_ch, 4 * n_ch, 5 * n_ch

    @pl.kernel(mesh=plsc.ScalarSubcoreMesh(axis_name="sequencer", num_cores=1), name=name,
               scratch_types=(pltpu.SemaphoreType.DMA((GATHER_COPIES * n_w,)), pltpu.SemaphoreType.DMA((GATHER_COPIES * n_w,)),
                              pltpu.SemaphoreType.DMA((n_w,))),
               compiler_params=pltpu.CompilerParams(collective_id=collective_id))
    def launch(send_sems, recv_sems, local_sems):
        x, y, c = _place()
        _handshake([(x, y, 1 - c), (1 - x, y, c), (x, 1 - y, c)])

        def at_place(k, cc):
            kx, ky, kd = k ^ FLIP_X, k ^ FLIP_Y, k ^ FLIP_BOTH
            me, sibling = _chip_id(k, cc), _chip_id(k, 1 - cc)
            started = []

            def copy(i, slot, src, dst, to, start=True):
                cp = pltpu.make_async_remote_copy(src_ref=src, dst_ref=dst, send_sem=send_sems.at[GATHER_COPIES * i + slot],
                                                  recv_sem=recv_sems.at[GATHER_COPIES * i + slot], device_id=to, device_id_type=MESH)
                if start:
                    cp.start()
                    started.append(cp)
                return cp

            def pass_on(i, slot, ref, to):
                copy(i, slot, ref, ref, to)

            def landed(i, slot, ref):
                copy(i, slot, ref, ref, me, start=False).wait_recv()

            y_order = [(n_relay + s) % n_ch for s in range(n_ch)]
            local = []
            for i, (g, s_ref, f_ref) in enumerate(zip(geoms, s_refs, f_refs)):
                cp = pltpu.make_async_copy(s_ref, g.shard_of(f_ref, k), local_sems.at[i])
                cp.start()
                local.append(cp)
                for s in range(n_ch):
                    copy(i, TO_X + s, g.chunk_of_shard(s_ref, cc, s, n_ch), g.chunk_of(f_ref, k, cc, s, n_ch), _chip_id(kx, cc))
                    ch = y_order[s]
                    copy(i, TO_Y + ch, g.chunk_of_shard(s_ref, cc, ch, n_ch), g.chunk_of(f_ref, k, cc, ch, n_ch), _chip_id(ky, cc))
            for i, (g, f_ref) in enumerate(zip(geoms, f_refs)):
                for s in range(n_ch):
                    from_x = g.chunk_of(f_ref, kx, cc, s, n_ch)
                    landed(i, TO_X + s, from_x)
                    if s < n_relay:
                        pass_on(i, RELAY_TO_Y + s, from_x, _chip_id(ky, cc))
                    pass_on(i, ON_X + s, from_x, sibling)
                    ch = y_order[s]
                    from_y = g.chunk_of(f_ref, ky, cc, ch, n_ch)
                    landed(i, TO_Y + ch, from_y)
                    if ch >= n_relay:
                        pass_on(i, RELAY_TO_X + ch - n_relay, from_y, _chip_id(kx, cc))
                    pass_on(i, ON_Y + ch, from_y, sibling)
                for r in range(n_relay):
                    via_y = g.chunk_of(f_ref, kd, cc, r, n_ch)
                    landed(i, RELAY_TO_Y + r, via_y)
                    pass_on(i, ON_DIAG + r, via_y, sibling)
                    via_x = g.chunk_of(f_ref, kd, cc, n_relay + r, n_ch)
                    landed(i, RELAY_TO_X + r, via_x)
                    pass_on(i, ON_DIAG + n_relay + r, via_x, sibling)
            for i, (g, f_ref) in enumerate(zip(geoms, f_refs)):
                for slot, kk in ((ON_X, kx), (ON_Y, ky), (ON_DIAG, kd)):
                    for ch in range(n_ch):
                        landed(i, slot + ch, g.chunk_of(f_ref, kk, 1 - cc, ch, n_ch))
            for cp in started:
                cp.wait_send()
            for cp in local:
                cp.wait()

        _on_each_place(x, y, c, at_place)

    launch()
    return [f_ref[...] for f_ref in f_refs]


def _swap_core_halves(grads, geoms, name, collective_id):
    n_cp = sum(1 if g.by_cols else N_CHIPS for g in geoms)

    def body(g_refs, t_refs, send_sems, recv_sems, x, y, c):

        def at_place(_, cc):
            def pairs(hc):
                out = []
                for g, g_ref, t_ref in zip(geoms, g_refs, t_refs):
                    if g.by_cols:
                        out.append((g_ref.at[pl.ds(hc * g.half_rows, g.half_rows), :], t_ref))
                    else:
                        out += [(g.half_of(g_ref, k, hc), g.part_of_halves(t_ref, k)) for k in range(N_CHIPS)]
                return out

            sends = [pltpu.make_async_remote_copy(src_ref=src, dst_ref=dst, send_sem=send_sems.at[n],
                                                  recv_sem=recv_sems.at[n], device_id=(x, y, 1 - cc), device_id_type=MESH)
                     for n, (src, dst) in enumerate(pairs(1 - cc))]
            for cp in sends:
                cp.start()
            for n, (src, dst) in enumerate(pairs(cc)):
                pltpu.make_async_remote_copy(src_ref=src, dst_ref=dst, send_sem=send_sems.at[n], recv_sem=recv_sems.at[n],
                                             device_id=(x, y, cc), device_id_type=MESH).wait_recv()
            for cp in sends:
                cp.wait_send()

        _on_each_place(x, y, c, at_place, by_chip=False)

    return _on_sequencer(body, grads, [jax.ShapeDtypeStruct((g.full[0] // 2, g.full[1]), F32) for g in geoms],
                         n_cp, _sibling, name, collective_id)


def _scatter_chip_sums(sums, geoms, name, collective_id):
    def body(s_refs, r_refs, send_sems, recv_sems, x, y, c):

        def at_place(k, _):
            sends = []
            for i, (g, s_ref, r_ref) in enumerate(zip(geoms, s_refs, r_refs)):
                for j, flip in enumerate(FLIPS):
                    kk = k ^ flip
                    cp = pltpu.make_async_remote_copy(
                        src_ref=g.part_of_halves(s_ref, kk), dst_ref=r_ref.at[j], send_sem=send_sems.at[3 * i + j],
                        recv_sem=recv_sems.at[3 * i + j], device_id=(kk // 2, kk % 2, c), device_id_type=MESH)
                    cp.start()
                    sends.append(cp)
            for i, (g, s_ref, r_ref) in enumerate(zip(geoms, s_refs, r_refs)):
                for j in range(len(FLIPS)):
                    pltpu.make_async_remote_copy(
                        src_ref=g.part_of_halves(s_ref, k), dst_ref=r_ref.at[j], send_sem=send_sems.at[3 * i + j],
                        recv_sem=recv_sems.at[3 * i + j], device_id=(x, y, c), device_id_type=MESH).wait_recv()
            for cp in sends:
                cp.wait_send()

        _on_each_place(x, y, c, at_place, by_core=False)

    return _on_sequencer(body, sums, [jax.ShapeDtypeStruct((len(FLIPS),) + g.half, WIRE_DTYPE) for g in geoms],
                         len(FLIPS) * len(sums), _same_core_of_other_chips, name, collective_id)


def _share_reduced_halves(reduced, geoms, name, collective_id):
    def body(out_refs, _, send_sems, recv_sems, x, y, c):

        def at_place(_, cc):
            sends = []
            for i, (g, ref) in enumerate(zip(geoms, out_refs)):
                mine = g.half_of_shard(ref, cc)
                cp = pltpu.make_async_remote_copy(src_ref=mine, dst_ref=mine, send_sem=send_sems.at[i],
                                                  recv_sem=recv_sems.at[i], device_id=(x, y, 1 - cc), device_id_type=MESH)
                cp.start()
                sends.append(cp)
            for i, (g, ref) in enumerate(zip(geoms, out_refs)):
                theirs = g.half_of_shard(ref, 1 - cc)
                pltpu.make_async_remote_copy(src_ref=theirs, dst_ref=theirs, send_sem=send_sems.at[i],
                                             recv_sem=recv_sems.at[i], device_id=(x, y, cc), device_id_type=MESH).wait_recv()
            for cp in sends:
                cp.wait_send()

        _on_each_place(x, y, c, at_place, by_chip=False)

    return _on_sequencer(body, reduced, [], len(reduced), _sibling, name, collective_id, return_inputs=True)


def _chip_sum(place, grad, theirs, g, name):
    RH, C = theirs.shape
    h = g.half_rows
    tr = _tile(h, 256, 16)
    tc = _tile(C, 2048)
    per_half = h // tr

    if g.by_cols:
        grad_map = lambda i, j, p: (p[1] * per_half + i, j)
    else:
        grad_map = lambda i, j, p: ((i // per_half) * 2 * per_half + p[1] * per_half + i % per_half, j)

    def body(p_ref, a_ref, b_ref, o_ref):
        o_ref[...] = (a_ref[...] + b_ref[...]).astype(o_ref.dtype)

    return pl.pallas_call(
        body, name=name,
        grid_spec=pltpu.PrefetchScalarGridSpec(
            num_scalar_prefetch=1, grid=(RH // tr, C // tc),
            in_specs=[pl.BlockSpec((tr, tc), grad_map), pl.BlockSpec((tr, tc), lambda i, j, p: (i, j))],
            out_specs=pl.BlockSpec((tr, tc), lambda i, j, p: (i, j))),
        out_shape=jax.ShapeDtypeStruct((RH, C), WIRE_DTYPE),
        compiler_params=_params("parallel", "parallel"),
    )(place, grad, theirs)


def _reduce_half(place, grad, theirs, others, g, name):
    h, wc = g.half
    tr = _tile(h, 256, 16)
    per_half = h // tr
    if g.by_cols:
        tc = wc
        grad_map = lambda i, p: (p[1] * per_half + i, p[0])
        theirs_map = lambda i, p: (i, p[0])
    else:
        tc = wc
        grad_map = lambda i, p: (p[0] * 2 * per_half + p[1] * per_half + i, 0)
        theirs_map = lambda i, p: (p[0] * per_half + i, 0)

    def body(p_ref, a_ref, b_ref, o0_ref, o1_ref, o2_ref, out_ref):
        acc = a_ref[...] + b_ref[...]
        for o_ref in (o0_ref, o1_ref, o2_ref):
            acc = acc + o_ref[...].astype(F32)
        out_ref[...] = acc

    other_specs = [pl.BlockSpec((None, tr, tc), functools.partial(lambda i, p, j: (j, i, 0), j=j)) for j in range(len(FLIPS))]
    return pl.pallas_call(
        body, name=name,
        grid_spec=pltpu.PrefetchScalarGridSpec(
            num_scalar_prefetch=1, grid=(per_half,),
            in_specs=[pl.BlockSpec((tr, tc), grad_map), pl.BlockSpec((tr, tc), theirs_map)] + other_specs,
            out_specs=pl.BlockSpec((tr, tc), lambda i, p: (p[1] * per_half + i, 0))),
        out_shape=jax.ShapeDtypeStruct(g.shard, F32),
        compiler_params=_params("arbitrary"),
    )(place, grad, theirs, others, others, others)


SMALL = ("b_ada", "norm1_g", "v_norm_g", "w_spatial", "b_spatial", "out_norm_g", "norm2_g", "final_g")
BIG = ("w_in", "w_out", "w_gate", "w_up", "w_down")
BY_COLS = {"w_in": True, "w_out": False, "w_gate": True, "w_up": True, "w_down": False}
ORDER = ("w_ada", "b_ada", "norm1_g", "w_in", "v_norm_g", "w_spatial", "b_spatial", "out_norm_g", "w_out",
         "norm2_g", "w_gate", "w_up", "w_down", "final_g")


def _pack(parts):
    return jnp.concatenate([parts[n].reshape(-1) for n in SMALL]).reshape(-1, LANE)


def _unpack(slab, shapes):
    flat = slab.reshape(-1)
    out, at = {}, 0
    for n in SMALL:
        size = math.prod(shapes[n])
        out[n] = flat[at:at + size].reshape(shapes[n])
        at += size
    return out


def kernel(x, c, w_ada, b_ada, norm1_g, w_in, v_norm_g, w_spatial, b_spatial, out_norm_g, w_out, norm2_g, w_gate, w_up, w_down, final_g, loss_target, m_w_ada, m_b_ada, m_norm1_g, m_w_in, m_v_norm_g, m_w_spatial, m_b_spatial, m_out_norm_g, m_w_out, m_norm2_g, m_w_gate, m_w_up, m_w_down, m_final_g, v_w_ada, v_b_ada, v_norm1_g, v_w_in, v_v_norm_g, v_w_spatial, v_b_spatial, v_out_norm_g, v_w_out, v_norm2_g, v_w_gate, v_w_up, v_w_down, v_final_g):
    weights = dict(w_ada=w_ada, b_ada=b_ada, norm1_g=norm1_g, w_in=w_in, v_norm_g=v_norm_g, w_spatial=w_spatial,
                   b_spatial=b_spatial, out_norm_g=out_norm_g, w_out=w_out, norm2_g=norm2_g, w_gate=w_gate, w_up=w_up,
                   w_down=w_down, final_g=final_g)
    m_in = dict(w_ada=m_w_ada, b_ada=m_b_ada, norm1_g=m_norm1_g, w_in=m_w_in, v_norm_g=m_v_norm_g, w_spatial=m_w_spatial,
                b_spatial=m_b_spatial, out_norm_g=m_out_norm_g, w_out=m_w_out, norm2_g=m_norm2_g, w_gate=m_w_gate,
                w_up=m_w_up, w_down=m_w_down, final_g=m_final_g)
    v_in = dict(w_ada=v_w_ada, b_ada=v_b_ada, norm1_g=v_norm1_g, w_in=v_w_in, v_norm_g=v_v_norm_g, w_spatial=v_w_spatial,
                b_spatial=v_b_spatial, out_norm_g=v_out_norm_g, w_out=v_w_out, norm2_g=v_norm2_g, w_gate=v_w_gate,
                w_up=v_w_up, w_down=v_w_down, final_g=v_final_g)

    S, D = x.shape[1], x.shape[2]
    n_g = v_norm_g.shape[-1] // LANE
    n_h = (D - n_g * LANE) // LANE
    GW = n_g * LANE
    xi, yi, ci = _place()
    chip = 2 * xi + yi
    me = 4 * xi + 2 * yi + ci
    place = jnp.stack([chip, ci]).astype(jnp.int32)

    xs, target = x[0], loss_target[0]
    geoms = [_Sharded(weights[n].shape[1:], BY_COLS[n]) for n in BIG]

    full = {}
    for i, group in enumerate((("w_in",), ("w_out",), ("w_gate", "w_up"), ("w_down",))):
        shards = [_cast(weights[n][0], WIRE_DTYPE, "cast_" + n) for n in group]
        gathered = _gather_weights(shards, [geoms[BIG.index(n)] for n in group], "gather_" + "_".join(group), 1 + i)
        full.update(zip(group, gathered))

    c_pad = jnp.concatenate([c, jnp.zeros((7, D), F32)], axis=0)
    c_all = _allgather8(c_pad, "gather_c")[::8]
    n_ada = w_ada.shape[2]
    b_cols = lax.dynamic_slice(b_ada, (0, chip * n_ada), (1, n_ada))
    mod_parts = _allgather8(_mod_part(c_all, w_ada[0], b_cols, "mod_part"), "gather_mod")
    mod_all = mod_parts.reshape(N_CHIPS, 2, 8, n_ada)[:, 0].transpose(1, 0, 2).reshape(8, N_CHIPS * n_ada)
    mod = lax.dynamic_slice(mod_all, (me, 0), (1, 6 * D))
    shift1, scale1, gate1, shift2, scale2, gate2 = [mod[:, i * D:(i + 1) * D] for i in range(6)]

    b_t = b_spatial[0].T
    h1 = _norm_mod(xs, norm1_g, scale1, shift1, "norm1")
    proj, = _mm("nn", h1, full["w_in"], [F32], "proj")
    on_gm = _gmlp_fwd(proj, v_norm_g, w_spatial[0], b_t, out_norm_g, n_g, "gmlp_fwd")
    o_sb, on_sb, l_sum = _sb_fwd(proj, out_norm_g, n_g, n_h, "sb_fwd")
    o_n = jnp.concatenate([on_gm, on_sb], axis=1)
    attn, = _mm("nn", o_n, full["w_out"], [F32], "attn_out")
    x1, h2 = _residual_norm_mod(xs, attn, gate1, norm2_g, scale2, shift2, "norm2")
    a_g, a_u, f_in = _gate_up(h2, full["w_gate"], full["w_up"], "gate_up")
    f = _mm_ktiled("nn", [(f_in, full["w_down"])], "down")
    dx2, df, d_gate2, d_final_g, loss_part = _final_loss_bwd(x1, f, gate2, final_g.reshape(1, D), target, "final")
    loss = lax.psum(loss_part[0, 0], ("x", "y", "c"))

    geom_of = dict(zip(BIG, geoms))
    grad_out, delta, new_m, new_v = {}, {}, {}, {}

    def swap(group, grads, collective_id):
        return _swap_core_halves(grads, [geom_of[n] for n in group], "swap_" + "_".join(group), collective_id)

    def chip_sums(group, grads, theirs, after):
        return [_chip_sum(place, gr, _then(after, t), geom_of[n], "chip_sum_" + n) for n, gr, t in zip(group, grads, theirs)]

    def scatter(group, sums, collective_id):
        return _scatter_chip_sums(sums, [geom_of[n] for n in group], "scatter_" + "_".join(group), collective_id)

    def reduce_halves(group, grads, theirs, others, after):
        return [_reduce_half(place, gr, t, _then(after, o), geom_of[n], "reduce_" + n)
                for n, gr, t, o in zip(group, grads, theirs, others)]

    def share(group, halves, collective_id):
        return _share_reduced_halves(halves, [geom_of[n] for n in group], "share_" + "_".join(group), collective_id)

    def adamw(group, reduced, after):
        for n, r in zip(group, reduced):
            grad_out[n] = r[None]
            d, mo, vo = _adamw(weights[n][0], _then(after, r), m_in[n][0], v_in[n][0], "adamw_" + n)
            delta[n], new_m[n], new_v[n] = d[None], mo[None], vo[None]
        return d

    g_down = ("w_down",)
    g_ffn = ("w_gate", "w_up")
    g_out = ("w_out",)
    g_in = ("w_in",)

    gr_down = _mm("tn", f_in, df, [F32], "d_w_down", tm=1408, tn=1024)
    th_down = swap(g_down, gr_down, 6)
    d_ag, d_au = _mm("nt", df, full["w_down"], [MXU_DTYPE, MXU_DTYPE], "d_ffn_in", extras=(a_g, a_u),
                     epilogue=_swiglu_bwd_epilogue)
    sm_down = chip_sums(g_down, gr_down, th_down, after=d_ag)
    ot_down = scatter(g_down, sm_down, 7)
    gr_ffn = [_mm("tn", h2, _then(sm_down, d_ag), [F32], "d_w_gate")[0], _mm("tn", h2, d_au, [F32], "d_w_up")[0]]
    th_ffn = swap(g_ffn, gr_ffn, 9)
    dh2 = _mm_ktiled("nt", [(_then(gr_ffn, d_ag), full["w_gate"]), (d_au, full["w_up"])], "d_h2", tn=512)
    sm_ffn = chip_sums(g_ffn, gr_ffn, th_ffn, after=dh2)
    ot_ffn = scatter(g_ffn, sm_ffn, 10)
    hv_down = reduce_halves(g_down, gr_down, th_down, ot_down, after=sm_ffn)
    rd_down = share(g_down, hv_down, 8)
    dx1, d_shift2, d_scale2, d_norm2_g, d_gate1, d_attn = _norm_mod_bwd(
        _then(hv_down, dh2), x1, dx2, norm2_g, scale2, "norm2_bwd", branch=attn, gate=gate1)
    gr_out = _mm("tn", o_n, d_attn, [F32], "d_w_out")
    th_out = swap(g_out, gr_out, 12)
    d_on, = _mm("nt", _then(gr_out, d_attn), full["w_out"], [F32], "d_o")
    dp_gm, d_w_spatial, d_b_t, d_v_norm_g, d_og_gm = _gmlp_bwd(proj, d_on, v_norm_g, w_spatial[0], b_t, out_norm_g, n_g, "gmlp_bwd")
    dq, dk, dv, d_og_sb = _sb_bwd(proj, o_sb, l_sum, _then(dp_gm, d_on), out_norm_g, n_g, n_h, "sb_bwd")
    sm_out = chip_sums(g_out, gr_out, th_out, after=dq)
    ot_out = scatter(g_out, sm_out, 13)
    hv_ffn = reduce_halves(g_ffn, gr_ffn, th_ffn, ot_ffn, after=sm_out)
    rd_ffn = share(g_ffn, hv_ffn, 11)
    dproj = jnp.concatenate([_then(hv_ffn, dp_gm), dq, dk, dv], axis=1)
    gr_in = _mm("tn", h1, dproj, [F32], "d_w_in")
    th_in = swap(g_in, gr_in, 15)
    dh1 = _mm_ktiled("nt", [(_then(gr_in, dproj), full["w_in"])], "d_h1")
    grad_x, d_shift1, d_scale1, d_norm1_g = _norm_mod_bwd(dh1, xs, dx1, norm1_g, scale1, "norm1_bwd")
    sm_in = chip_sums(g_in, gr_in, th_in, after=grad_x)
    ot_in = scatter(g_in, sm_in, 16)
    hv_out = reduce_halves(g_out, gr_out, th_out, ot_out, after=sm_in)
    rd_out = share(g_out, hv_out, 14)

    dmod = jnp.concatenate([d_shift1, d_scale1, d_gate1, d_shift2, d_scale2, d_gate2], axis=1)
    small_parts = dict(b_ada=dmod, norm1_g=d_norm1_g, v_norm_g=d_v_norm_g, w_spatial=d_w_spatial, b_spatial=d_b_t.T,
                       out_norm_g=jnp.concatenate([d_og_gm, d_og_sb], axis=1), norm2_g=d_norm2_g, final_g=d_final_g)
    slab = _then(hv_out, _pack(small_parts))
    rows = slab.shape[0]
    gathered = _allgather8(slab, "gather_small")
    small_shapes = {n: weights[n].shape for n in SMALL}
    small_sum = _sum_devices(gathered, 8, "sum_small")
    dmod_all = gathered.reshape(8, rows * LANE)[:, :6 * D]
    dmod_cols = lax.dynamic_slice(dmod_all, (0, chip * n_ada), (8, n_ada))
    g_ada, d, mo, vo = _adamw_ada(c_all, dmod_cols, w_ada[0], m_w_ada[0], v_w_ada[0], "adamw_w_ada")
    grad_out["w_ada"], delta["w_ada"], new_m["w_ada"], new_v["w_ada"] = g_ada[None], d[None], mo[None], vo[None]
    d_small, mo, vo = _adamw(_pack({n: weights[n] for n in SMALL}), small_sum, _pack({n: m_in[n] for n in SMALL}),
                             _pack({n: v_in[n] for n in SMALL}), "adamw_small")
    for dst, slab_out in ((grad_out, small_sum), (delta, d_small), (new_m, mo), (new_v, vo)):
        dst.update(_unpack(slab_out, small_shapes))
    done = adamw(g_down, rd_down, after=d)
    done = adamw(g_ffn, rd_ffn, after=done)
    done = adamw(g_out, rd_out, after=done)
    hv_in = reduce_halves(g_in, gr_in, th_in, ot_in, after=done)
    adamw(g_in, share(g_in, hv_in, 17), after=done)

    return (loss, grad_x[None], *[grad_out[n] for n in ORDER], *[delta[n] for n in ORDER],
            *[new_m[n] for n in ORDER], *[new_v[n] for n in ORDER])
```

```python
import functools
import math

import jax
import jax.numpy as jnp
from jax import lax
from jax.experimental import pallas as pl
from jax.experimental.pallas import tpu as pltpu
from jax.experimental.pallas import tpu_sc as plsc

F32 = jnp.float32
MXU_DTYPE = jnp.bfloat16
WIRE_DTYPE = jnp.bfloat16
EPS = 1e-6
LANE = 128
V7X_VMEM_LIMIT = 56 * 1024 * 1024
MESH = pl.DeviceIdType.MESH
N_CHIPS = 4
FLIPS = (2, 1, 3)

ADAM_LR = 0.001
ADAM_B1 = 0.9
ADAM_B2 = 0.999
ADAM_EPS = 1e-08
ADAM_WD = 0.01
ADAM_STEP = 10


def _params(*semantics):
    return pltpu.CompilerParams(dimension_semantics=semantics or None, vmem_limit_bytes=V7X_VMEM_LIMIT)


def _tile(dim, pref, unit=LANE):
    best = None
    t = unit
    while t <= min(dim, pref):
        if dim % t == 0:
            best = t
        t += unit
    return best if best is not None else dim


def _then(first, second):
    return lax.optimization_barrier((first, second))[1]


def _sum0(v):
    return jnp.sum(v, axis=0, keepdims=True)


def _mean1(v):
    return jnp.mean(v, axis=-1, keepdims=True)


def _gelu(x):
    return 0.5 * x * (1.0 + lax.erf(x * (1.0 / math.sqrt(2.0))))


def _gelu_grad(x):
    cdf = 0.5 * (1.0 + lax.erf(x * (1.0 / math.sqrt(2.0))))
    return cdf + x * jnp.exp(-0.5 * x * x) * (1.0 / math.sqrt(2.0 * math.pi))


def _dot(a, b, dims):
    return lax.dot_general(a, b, (dims, ((), ())), preferred_element_type=F32)


NN = ((1,), (0,))
NT = ((1,), (1,))
TN = ((0,), (0,))


def _mm(kind, a, b, out_dtypes, name, tm=2048, tn=512, extras=(), epilogue=None):
    if kind == "nn":
        (M, K), N = a.shape, b.shape[1]
    elif kind == "nt":
        (M, K), N = a.shape, b.shape[0]
    else:
        (K, M), N = a.shape, b.shape[1]
    tm, tn = _tile(M, tm), _tile(N, tn)
    a_spec = pl.BlockSpec((K, tm), lambda i, j: (0, i)) if kind == "tn" else pl.BlockSpec((tm, K), lambda i, j: (i, 0))
    b_spec = pl.BlockSpec((tn, K), lambda i, j: (j, 0)) if kind == "nt" else pl.BlockSpec((K, tn), lambda i, j: (0, j))
    mn_spec = pl.BlockSpec((tm, tn), lambda i, j: (i, j))
    dims = {"nn": NN, "nt": NT, "tn": TN}[kind]
    n_extra = len(extras)

    def body(a_ref, b_ref, *rest):
        acc = _dot(a_ref[...], b_ref[...], dims)
        res = (acc,) if epilogue is None else epilogue(acc, *[e[...] for e in rest[:n_extra]])
        for o_ref, r in zip(rest[n_extra:], res):
            o_ref[...] = r.astype(o_ref.dtype)

    outs = pl.pallas_call(
        body, name=name, grid=(M // tm, N // tn),
        in_specs=[a_spec, b_spec] + [mn_spec] * n_extra,
        out_specs=[mn_spec] * len(out_dtypes),
        out_shape=[jax.ShapeDtypeStruct((M, N), d) for d in out_dtypes],
        compiler_params=_params("parallel", "arbitrary"),
    )(a, b, *extras)
    return outs


def _mm_ktiled(kind, pairs, name, tm=2048, tn=1024, tk=1408):
    a0, b0 = pairs[0]
    M, K = a0.shape
    N = b0.shape[1] if kind == "nn" else b0.shape[0]
    tm, tn, tk = _tile(M, tm), _tile(N, tn), _tile(K, tk)
    a_spec = pl.BlockSpec((tm, tk), lambda i, j, k: (i, k))
    b_spec = pl.BlockSpec((tk, tn), lambda i, j, k: (k, j)) if kind == "nn" else pl.BlockSpec((tn, tk), lambda i, j, k: (j, k))
    dims = NN if kind == "nn" else NT
    n_pairs = len(pairs)

    def body(*refs):
        o_ref = refs[2 * n_pairs]
        acc = _dot(refs[0][...], refs[1][...], dims)
        for p in range(1, n_pairs):
            acc = acc + _dot(refs[2 * p][...], refs[2 * p + 1][...], dims)

        @pl.when(pl.program_id(2) == 0)
        def _():
            o_ref[...] = acc

        @pl.when(pl.program_id(2) != 0)
        def _():
            o_ref[...] += acc

    return pl.pallas_call(
        body, name=name, grid=(M // tm, N // tn, K // tk),
        in_specs=[a_spec, b_spec] * n_pairs,
        out_specs=pl.BlockSpec((tm, tn), lambda i, j, k: (i, j)),
        out_shape=jax.ShapeDtypeStruct((M, N), F32),
        compiler_params=_params("parallel", "parallel", "arbitrary"),
    )(*[x for pair in pairs for x in pair])


def _gate_up(h, wg, wu, name):
    (M, K), N = h.shape, wg.shape[1]
    tm, tn = _tile(M, 2048), _tile(N, 512)

    def body(h_ref, wg_ref, wu_ref, ag_ref, au_ref, f_ref):
        hv = h_ref[...]
        ag = _dot(hv, wg_ref[...], NN)
        au = _dot(hv, wu_ref[...], NN)
        ag_ref[...] = ag.astype(ag_ref.dtype)
        au_ref[...] = au.astype(au_ref.dtype)
        f_ref[...] = (ag * jax.nn.sigmoid(ag) * au).astype(f_ref.dtype)

    w_spec = pl.BlockSpec((K, tn), lambda i, j: (0, j))
    mn_spec = pl.BlockSpec((tm, tn), lambda i, j: (i, j))
    return pl.pallas_call(
        body, name=name, grid=(M // tm, N // tn),
        in_specs=[pl.BlockSpec((tm, K), lambda i, j: (i, 0)), w_spec, w_spec],
        out_specs=[mn_spec] * 3,
        out_shape=[jax.ShapeDtypeStruct((M, N), MXU_DTYPE)] * 3,
        compiler_params=_params("parallel", "arbitrary"),
    )(h, wg, wu)


def _swiglu_bwd_epilogue(dfin, ag, au):
    ag, au = ag.astype(F32), au.astype(F32)
    sg = jax.nn.sigmoid(ag)
    d_au = dfin * (ag * sg)
    d_ag = dfin * au * (sg * (1.0 + ag * (1.0 - sg)))
    return d_ag, d_au


def _row_specs(ts, width):
    return pl.BlockSpec((ts, width), lambda i: (i, 0)), pl.BlockSpec((1, width), lambda i: (0, 0))


def _cast_into_full(place, shard, g, name):
    R, C = shard.shape
    tr = _tile(R, 256, 16)
    n_blk = R // tr
    out_map = (lambda i, p: (i, p[0])) if g.by_cols else (lambda i, p: (p[0] * n_blk + i, 0))

    def body(p_ref, a_ref, o_ref):
        o_ref[...] = a_ref[...].astype(o_ref.dtype)

    return pl.pallas_call(
        body, name=name,
        grid_spec=pltpu.PrefetchScalarGridSpec(
            num_scalar_prefetch=1, grid=(n_blk,),
            in_specs=[pl.BlockSpec((tr, C), lambda i, p: (i, 0))],
            out_specs=pl.BlockSpec((tr, C), out_map)),
        out_shape=jax.ShapeDtypeStruct(g.full, WIRE_DTYPE),
        compiler_params=_params("arbitrary"),
    )(place, shard)


def _norm_mod(x, g, scale, shift, name):
    S, D = x.shape
    ts = _tile(S, 256, 16)
    tile, vec = _row_specs(ts, D)

    def body(x_ref, g_ref, sc_ref, sh_ref, h_ref):
        xv = x_ref[...]
        r = lax.rsqrt(_mean1(xv * xv) + EPS)
        h_ref[...] = ((xv * r) * g_ref[...] * (1.0 + sc_ref[...]) + sh_ref[...]).astype(h_ref.dtype)

    return pl.pallas_call(body, name=name, grid=(S // ts,), in_specs=[tile, vec, vec, vec], out_specs=tile,
                          out_shape=jax.ShapeDtypeStruct((S, D), MXU_DTYPE), compiler_params=_params("parallel"))(x, g, scale, shift)


def _residual_norm_mod(x, attn, gate, g, scale, shift, name):
    S, D = x.shape
    ts = _tile(S, 256, 16)
    tile, vec = _row_specs(ts, D)

    def body(x_ref, a_ref, gate_ref, g_ref, sc_ref, sh_ref, x1_ref, h_ref):
        x1 = x_ref[...] + gate_ref[...] * a_ref[...]
        x1_ref[...] = x1
        r = lax.rsqrt(_mean1(x1 * x1) + EPS)
        h_ref[...] = ((x1 * r) * g_ref[...] * (1.0 + sc_ref[...]) + sh_ref[...]).astype(h_ref.dtype)

    return pl.pallas_call(body, name=name, grid=(S // ts,), in_specs=[tile, tile, vec, vec, vec, vec],
                          out_specs=[tile, tile],
                          out_shape=[jax.ShapeDtypeStruct((S, D), F32), jax.ShapeDtypeStruct((S, D), MXU_DTYPE)],
                          compiler_params=_params("parallel"))(x, attn, gate, g, scale, shift)


def _final_loss_bwd(x1, f, gate2, final_g, target, name):
    S, D = x1.shape
    ts = _tile(S, 256, 16)
    tile, vec = _row_specs(ts, D)
    loss_spec = pl.BlockSpec((1, LANE), lambda i: (0, 0))

    def body(x1_ref, f_ref, gate_ref, g_ref, t_ref, dx2_ref, df_ref, dgate_ref, dg_ref, loss_ref):
        @pl.when(pl.program_id(0) == 0)
        def _():
            dgate_ref[...] = jnp.zeros_like(dgate_ref)
            dg_ref[...] = jnp.zeros_like(dg_ref)
            loss_ref[...] = jnp.zeros_like(loss_ref)

        fv, gate, g = f_ref[...], gate_ref[...], g_ref[...]
        x2 = x1_ref[...] + gate * fv
        r = lax.rsqrt(_mean1(x2 * x2) + EPS)
        xn = x2 * r
        err = xn * g - t_ref[...]
        loss_ref[...] += jnp.broadcast_to(0.5 * _sum0(_mean1(err * err)), loss_ref.shape)
        dy = err * (1.0 / D)
        dg_ref[...] += _sum0(dy * xn)
        dxn = dy * g
        dx2 = r * (dxn - xn * _mean1(dxn * xn))
        dx2_ref[...] = dx2
        dgate_ref[...] += _sum0(dx2 * fv)
        df_ref[...] = (dx2 * gate).astype(df_ref.dtype)

    return pl.pallas_call(
        body, name=name, grid=(S // ts,), in_specs=[tile, tile, vec, vec, tile],
        out_specs=[tile, tile, vec, vec, loss_spec],
        out_shape=[jax.ShapeDtypeStruct((S, D), F32), jax.ShapeDtypeStruct((S, D), MXU_DTYPE),
                   jax.ShapeDtypeStruct((1, D), F32), jax.ShapeDtypeStruct((1, D), F32),
                   jax.ShapeDtypeStruct((1, LANE), F32)],
        compiler_params=_params("arbitrary"),
    )(x1, f, gate2, final_g, target)


def _norm_mod_bwd(dh, xin, dres, g, scale, name, branch=None, gate=None):
    S, D = xin.shape
    ts = _tile(S, 256, 16)
    tile, vec = _row_specs(ts, D)
    with_gate = branch is not None

    def body(*refs):
        if with_gate:
            dh_ref, x_ref, dres_ref, g_ref, sc_ref, br_ref, gate_ref, dx_ref, dshift_ref, dscale_ref, dg_ref, dgate_ref, dbr_ref = refs
            accs = (dshift_ref, dscale_ref, dg_ref, dgate_ref)
        else:
            dh_ref, x_ref, dres_ref, g_ref, sc_ref, dx_ref, dshift_ref, dscale_ref, dg_ref = refs
            accs = (dshift_ref, dscale_ref, dg_ref)

        @pl.when(pl.program_id(0) == 0)
        def _():
            for acc in accs:
                acc[...] = jnp.zeros_like(acc)

        dh_v, xv, g_v = dh_ref[...], x_ref[...], g_ref[...]
        one_sc = 1.0 + sc_ref[...]
        r = lax.rsqrt(_mean1(xv * xv) + EPS)
        xn = xv * r
        dshift_ref[...] += _sum0(dh_v)
        dscale_ref[...] += _sum0(dh_v * (xn * g_v))
        dg_ref[...] += _sum0(dh_v * one_sc * xn)
        dxn = dh_v * (g_v * one_sc)
        dx = dres_ref[...] + r * (dxn - xn * _mean1(dxn * xn))
        dx_ref[...] = dx
        if with_gate:
            dgate_ref[...] += _sum0(dx * br_ref[...])
            dbr_ref[...] = (dx * gate_ref[...]).astype(dbr_ref.dtype)

    ins = [dh, xin, dres, g, scale] + ([branch, gate] if with_gate else [])
    in_specs = [tile, tile, tile, vec, vec] + ([tile, vec] if with_gate else [])
    out_specs = [tile, vec, vec, vec] + ([vec, tile] if with_gate else [])
    out_shape = [jax.ShapeDtypeStruct((S, D), F32)] + [jax.ShapeDtypeStruct((1, D), F32)] * 3
    if with_gate:
        out_shape += [jax.ShapeDtypeStruct((1, D), F32), jax.ShapeDtypeStruct((S, D), MXU_DTYPE)]
    return pl.pallas_call(body, name=name, grid=(S // ts,), in_specs=in_specs, out_specs=out_specs,
                          out_shape=out_shape, compiler_params=_params("arbitrary"))(*ins)


def _causal_weights(ws_ref, wt_ref, n_g):
    row = lax.broadcasted_iota(jnp.int32, (LANE, LANE), 0)
    col = lax.broadcasted_iota(jnp.int32, (LANE, LANE), 1)
    for g in range(n_g):
        wt_ref[g] = jnp.where(col <= row, ws_ref[g], 0.0).astype(wt_ref.dtype)


def _group_layernorm(v):
    xc = v - _mean1(v)
    rstd = lax.rsqrt(_mean1(xc * xc) + EPS)
    return xc * rstd, rstd


def _gmlp_fwd(proj, v_gain, w_s, b_t, out_gain, n_g, name):
    S = proj.shape[0]
    GW = n_g * LANE

    def body(p_ref, vg_ref, ws_ref, bt_ref, og_ref, on_ref, wt_ref):
        @pl.when(pl.program_id(0) == 0)
        def _():
            _causal_weights(ws_ref, wt_ref, n_g)

        for g in range(n_g):
            cols = slice(g * LANE, (g + 1) * LANE)
            u = _gelu(p_ref[:, cols])
            v = _gelu(p_ref[:, GW + g * LANE:GW + (g + 1) * LANE])
            vhat, _ = _group_layernorm(v)
            vln = (vhat * vg_ref[:, cols]).astype(MXU_DTYPE)
            mixed = _dot(wt_ref[g], vln, NN) + bt_ref[:, g:g + 1]
            o = u * mixed
            r = lax.rsqrt(_mean1(o * o) + EPS)
            on_ref[:, cols] = (o * r * og_ref[:, cols]).astype(on_ref.dtype)

    return pl.pallas_call(
        body, name=name, grid=(S // LANE,),
        in_specs=[pl.BlockSpec((LANE, 2 * GW), lambda n: (n, 0)),
                  pl.BlockSpec((1, GW), lambda n: (0, 0)),
                  pl.BlockSpec((n_g, LANE, LANE), lambda n: (0, 0, 0)),
                  pl.BlockSpec((LANE, n_g), lambda n: (0, 0)),
                  pl.BlockSpec((1, GW), lambda n: (0, 0))],
        out_specs=pl.BlockSpec((LANE, GW), lambda n: (n, 0)),
        out_shape=jax.ShapeDtypeStruct((S, GW), MXU_DTYPE),
        scratch_shapes=[pltpu.VMEM((n_g, LANE, LANE), MXU_DTYPE)],
        compiler_params=_params("arbitrary"),
    )(proj, v_gain, w_s, b_t, out_gain)


def _gmlp_bwd(proj, d_on, v_gain, w_s, b_t, out_gain, n_g, name):
    S = proj.shape[0]
    GW = n_g * LANE

    def body(p_ref, dn_ref, vg_ref, ws_ref, bt_ref, og_ref, dp_ref, dws_ref, dbt_ref, dvg_ref, dog_ref, wt_ref):
        @pl.when(pl.program_id(0) == 0)
        def _():
            _causal_weights(ws_ref, wt_ref, n_g)
            dws_ref[...] = jnp.zeros_like(dws_ref)
            dbt_ref[...] = jnp.zeros_like(dbt_ref)
            dvg_ref[...] = jnp.zeros_like(dvg_ref)
            dog_ref[...] = jnp.zeros_like(dog_ref)

        row = lax.broadcasted_iota(jnp.int32, (LANE, LANE), 0)
        col = lax.broadcasted_iota(jnp.int32, (LANE, LANE), 1)
        for g in range(n_g):
            cols = slice(g * LANE, (g + 1) * LANE)
            vcols = slice(GW + g * LANE, GW + (g + 1) * LANE)
            pu, pv = p_ref[:, cols], p_ref[:, vcols]
            u, v = _gelu(pu), _gelu(pv)
            vhat, rstd = _group_layernorm(v)
            gain = vg_ref[:, cols]
            vln = (vhat * gain).astype(MXU_DTYPE)
            mixed = _dot(wt_ref[g], vln, NN) + bt_ref[:, g:g + 1]
            o = u * mixed
            r = lax.rsqrt(_mean1(o * o) + EPS)
            oh = o * r
            dn = dn_ref[:, cols]
            dog_ref[:, cols] += _sum0(dn * oh)
            dhn = dn * og_ref[:, cols]
            d_o = r * (dhn - oh * _mean1(dhn * oh))
            du = d_o * mixed
            dmix = d_o * u
            dbt_ref[:, g:g + 1] += jnp.sum(dmix, axis=1, keepdims=True)
            dmix_b = dmix.astype(MXU_DTYPE)
            dws_ref[g] += jnp.where(col <= row, _dot(dmix_b, vln, NT), 0.0)
            dvln = _dot(wt_ref[g], dmix_b, TN)
            dvg_ref[:, cols] += _sum0(dvln * vhat)
            dxh = dvln * gain
            dv = rstd * (dxh - _mean1(dxh) - vhat * _mean1(dxh * vhat))
            dp_ref[:, cols] = (du * _gelu_grad(pu)).astype(dp_ref.dtype)
            dp_ref[:, vcols] = (dv * _gelu_grad(pv)).astype(dp_ref.dtype)

    return pl.pallas_call(
        body, name=name, grid=(S // LANE,),
        in_specs=[pl.BlockSpec((LANE, 2 * GW), lambda n: (n, 0)),
                  pl.BlockSpec((LANE, GW), lambda n: (n, 0)),
                  pl.BlockSpec((1, GW), lambda n: (0, 0)),
                  pl.BlockSpec((n_g, LANE, LANE), lambda n: (0, 0, 0)),
                  pl.BlockSpec((LANE, n_g), lambda n: (0, 0)),
                  pl.BlockSpec((1, GW), lambda n: (0, 0))],
        out_specs=[pl.BlockSpec((LANE, 2 * GW), lambda n: (n, 0)),
                   pl.BlockSpec((n_g, LANE, LANE), lambda n: (0, 0, 0)),
                   pl.BlockSpec((LANE, n_g), lambda n: (0, 0)),
                   pl.BlockSpec((1, GW), lambda n: (0, 0)),
                   pl.BlockSpec((1, GW), lambda n: (0, 0))],
        out_shape=[jax.ShapeDtypeStruct((S, 2 * GW), MXU_DTYPE),
                   jax.ShapeDtypeStruct((n_g, LANE, LANE), F32),
                   jax.ShapeDtypeStruct((LANE, n_g), F32),
                   jax.ShapeDtypeStruct((1, GW), F32),
                   jax.ShapeDtypeStruct((1, GW), F32)],
        scratch_shapes=[pltpu.VMEM((n_g, LANE, LANE), MXU_DTYPE)],
        compiler_params=_params("arbitrary"),
    )(proj, d_on, v_gain, w_s, b_t, out_gain)


def _tri_sum(v, tri):
    hi = v.astype(MXU_DTYPE)
    lo = (v - hi.astype(F32)).astype(MXU_DTYPE)
    return _dot(hi, tri, NN) + _dot(lo, tri, NN)


def _log_sigmoids(z):
    sp = jnp.log1p(jnp.exp(-jnp.abs(z)))
    return jnp.minimum(z, 0.0) - sp, jnp.minimum(-z, 0.0) - sp


def _rows(i, size):
    return pl.ds(pl.multiple_of(i * size, size), size)


SB_QUERY_TILE = 512
SB_KEY_TILE = 256


def _sb_tiles(S):
    tq = _tile(S, SB_QUERY_TILE)
    tk = _tile(tq, SB_KEY_TILE)
    return tq, tk, S // tq, tq // tk


def _triangle(n, keep):
    row = lax.broadcasted_iota(jnp.int32, (n, n), 0)
    col = lax.broadcasted_iota(jnp.int32, (n, n), 1)
    return jnp.where(keep(row, col), 1.0, 0.0).astype(MXU_DTYPE)


def _strictly_before(tq, tk, key_offset):
    row = lax.broadcasted_iota(jnp.int32, (tq, tk), 0)
    col = lax.broadcasted_iota(jnp.int32, (tq, tk), 1)
    return col + key_offset < row


def _sb_specs(S, n_g, n_h):
    base = 2 * n_g
    q_spec = pl.BlockSpec((S, LANE), lambda h: (0, base + h))
    k_spec = pl.BlockSpec((S, LANE), lambda h: (0, base + n_h + h))
    v_spec = pl.BlockSpec((S, LANE), lambda h: (0, base + 2 * n_h + h))
    gain_spec = pl.BlockSpec((1, LANE), lambda h: (0, n_g + h))
    head_spec = pl.BlockSpec((S, LANE), lambda h: (0, h))
    return q_spec, k_spec, v_spec, gain_spec, head_spec


def _sb_fwd(proj, out_gain, n_g, n_h, name):
    S = proj.shape[0]
    TQ, TK, NQ, KPQ = _sb_tiles(S)
    scale = LANE ** -0.5
    q_spec, k_spec, v_spec, gain_spec, head_spec = _sb_specs(S, n_g, n_h)

    def body(q_ref, k_ref, v_ref, og_ref, o_ref, on_ref, ls_ref, qb, kb, vb):
        qb[...] = q_ref[...].astype(MXU_DTYPE)
        kb[...] = k_ref[...].astype(MXU_DTYPE)
        vb[...] = v_ref[...].astype(MXU_DTYPE)
        after = _triangle(TK, lambda r, c: r > c)

        def block(qi, j, ctail, acc, key_offset):
            z = _dot(qi, kb[_rows(j, TK), :], NT) * scale
            lb, l1m = _log_sigmoids(z)
            if key_offset is not None:
                strict = _strictly_before(TQ, TK, key_offset)
                l1m = jnp.where(strict, l1m, 0.0)
            a = jnp.exp(lb + ctail + _tri_sum(l1m, after))
            if key_offset is not None:
                a = jnp.where(strict, a, 0.0)
            acc = acc + _dot(a.astype(MXU_DTYPE), vb[_rows(j, TK), :], NN)
            return ctail + jnp.sum(l1m, axis=1, keepdims=True), acc

        def q_loop(i, carry):
            qi = qb[_rows(i, TQ), :]
            state = (jnp.zeros((TQ, 1), F32), jnp.zeros((TQ, LANE), F32))
            for d in reversed(range(KPQ)):
                state = block(qi, i * KPQ + d, state[0], state[1], d * TK)
            ctail, acc = lax.fori_loop(
                0, i * KPQ, lambda jj, st: block(qi, i * KPQ - 1 - jj, st[0], st[1], None), state)
            ls_ref[_rows(i, TQ), :] = jnp.broadcast_to(ctail, (TQ, LANE))
            o_ref[_rows(i, TQ), :] = acc
            r = lax.rsqrt(_mean1(acc * acc) + EPS)
            on_ref[_rows(i, TQ), :] = (acc * r * og_ref[...]).astype(on_ref.dtype)
            return carry

        lax.fori_loop(0, NQ, q_loop, 0)

    return pl.pallas_call(
        body, name=name, grid=(n_h,),
        in_specs=[q_spec, k_spec, v_spec, gain_spec],
        out_specs=[head_spec, head_spec, head_spec],
        out_shape=[jax.ShapeDtypeStruct((S, n_h * LANE), F32), jax.ShapeDtypeStruct((S, n_h * LANE), MXU_DTYPE),
                   jax.ShapeDtypeStruct((S, n_h * LANE), F32)],
        scratch_shapes=[pltpu.VMEM((S, LANE), MXU_DTYPE)] * 3,
        compiler_params=_params("parallel"),
    )(proj, proj, proj, out_gain)


def _sb_bwd(proj, o_sb, l_sum, d_on, out_gain, n_g, n_h, name):
    S = proj.shape[0]
    TQ, TK, NQ, KPQ = _sb_tiles(S)
    scale = LANE ** -0.5
    q_spec, k_spec, v_spec, gain_spec, head_spec = _sb_specs(S, n_g, n_h)
    dn_spec = pl.BlockSpec((S, LANE), lambda h: (0, n_g + h))
    dgain_spec = pl.BlockSpec((1, LANE), lambda h: (0, h))

    def body(q_ref, k_ref, v_ref, o_ref, ls_ref, dn_ref, og_ref, dq_ref, dk_ref, dv_ref, dog_ref,
             qb, kb, vb, dob, dk_acc, dv_acc):
        qb[...] = q_ref[...].astype(MXU_DTYPE)
        kb[...] = k_ref[...].astype(MXU_DTYPE)
        vb[...] = v_ref[...].astype(MXU_DTYPE)
        o, dn = o_ref[...], dn_ref[...]
        r = lax.rsqrt(_mean1(o * o) + EPS)
        oh = o * r
        dog_ref[...] = _sum0(dn * oh)
        dhn = dn * og_ref[...]
        dob[...] = (r * (dhn - oh * _mean1(dhn * oh))).astype(MXU_DTYPE)
        dk_acc[...] = jnp.zeros_like(dk_acc)
        dv_acc[...] = jnp.zeros_like(dv_acc)

        up_to = _triangle(TK, lambda r, c: r <= c)
        before = _triangle(TK, lambda r, c: r < c)

        def block(qi, doi, ltot, j, cl, cdl, dq, key_offset):
            kj, vj = kb[_rows(j, TK), :], vb[_rows(j, TK), :]
            z = _dot(qi, kj, NT) * scale
            lb, l1m_all = _log_sigmoids(z)
            l1m = l1m_all
            if key_offset is not None:
                strict = _strictly_before(TQ, TK, key_offset)
                l1m = jnp.where(strict, l1m_all, 0.0)
            a = jnp.exp(lb + (ltot - (cl + _tri_sum(l1m, up_to))))
            if key_offset is not None:
                a = jnp.where(strict, a, 0.0)
            dl = _dot(doi, vj, NT) * a
            d_l1m = cdl + _tri_sum(dl, before)
            dz = dl * jnp.exp(l1m_all) - jnp.exp(lb) * d_l1m
            if key_offset is not None:
                dz = jnp.where(strict, dz, 0.0)
            dzs = (dz * scale).astype(MXU_DTYPE)
            dq = dq + _dot(dzs, kj, NN)
            dk_acc[_rows(j, TK), :] += _dot(dzs, qi, TN)
            dv_acc[_rows(j, TK), :] += _dot(a.astype(MXU_DTYPE), doi, TN)
            return (cl + jnp.sum(l1m, axis=1, keepdims=True), cdl + jnp.sum(dl, axis=1, keepdims=True), dq)

        def q_loop(i, carry):
            qi, doi = qb[_rows(i, TQ), :], dob[_rows(i, TQ), :]
            ltot = ls_ref[_rows(i, TQ), :][:, :1]
            zero_col = jnp.zeros((TQ, 1), F32)
            state = lax.fori_loop(
                0, i * KPQ, lambda j, st: block(qi, doi, ltot, j, st[0], st[1], st[2], None),
                (zero_col, zero_col, jnp.zeros((TQ, LANE), F32)))
            for d in range(KPQ):
                state = block(qi, doi, ltot, i * KPQ + d, state[0], state[1], state[2], d * TK)
            dq_ref[_rows(i, TQ), :] = state[2].astype(dq_ref.dtype)
            return carry

        lax.fori_loop(0, NQ, q_loop, 0)
        dk_ref[...] = dk_acc[...].astype(dk_ref.dtype)
        dv_ref[...] = dv_acc[...].astype(dv_ref.dtype)

    W = n_h * LANE
    return pl.pallas_call(
        body, name=name, grid=(n_h,),
        in_specs=[q_spec, k_spec, v_spec, head_spec, head_spec, dn_spec, gain_spec],
        out_specs=[head_spec, head_spec, head_spec, dgain_spec],
        out_shape=[jax.ShapeDtypeStruct((S, W), MXU_DTYPE)] * 3 + [jax.ShapeDtypeStruct((1, W), F32)],
        scratch_shapes=[pltpu.VMEM((S, LANE), MXU_DTYPE)] * 4 + [pltpu.VMEM((S, LANE), F32)] * 2,
        compiler_params=_params("parallel"),
    )(proj, proj, proj, o_sb, l_sum, d_on, out_gain)


def _mod_part(c_all, w_ada, b_ada_cols, name):
    B, K = c_all.shape
    N = w_ada.shape[1]
    tn = _tile(N, 512)

    def body(c_ref, w_ref, b_ref, o_ref):
        cv = c_ref[...]
        ca = (cv * jax.nn.sigmoid(cv)).astype(MXU_DTYPE)
        o_ref[...] = _dot(ca, w_ref[...].astype(MXU_DTYPE), NN) + b_ref[...]

    return pl.pallas_call(
        body, name=name, grid=(N // tn,),
        in_specs=[pl.BlockSpec((B, K), lambda j: (0, 0)), pl.BlockSpec((K, tn), lambda j: (0, j)),
                  pl.BlockSpec((1, tn), lambda j: (0, j))],
        out_specs=pl.BlockSpec((B, tn), lambda j: (0, j)),
        out_shape=jax.ShapeDtypeStruct((B, N), F32), compiler_params=_params("parallel"))(c_all, w_ada, b_ada_cols)


def _adamw_math(w, g, m, v):
    m = ADAM_B1 * m + (1.0 - ADAM_B1) * g
    v = ADAM_B2 * v + (1.0 - ADAM_B2) * (g * g)
    m_hat = m / (1.0 - ADAM_B1 ** ADAM_STEP)
    v_hat = v / (1.0 - ADAM_B2 ** ADAM_STEP)
    delta = -ADAM_LR * (m_hat / (jnp.sqrt(v_hat) + ADAM_EPS) + ADAM_WD * w)
    return delta, m, v


def _adamw(w, g, m, v, name):
    R, C = w.shape
    tr = _tile(R, max(8, (1 << 19) // C), 8)
    spec = pl.BlockSpec((tr, C), lambda i: (i, 0))

    def body(w_ref, g_ref, m_ref, v_ref, d_ref, mo_ref, vo_ref):
        d_ref[...], mo_ref[...], vo_ref[...] = _adamw_math(w_ref[...], g_ref[...], m_ref[...], v_ref[...])

    return pl.pallas_call(body, name=name, grid=(R // tr,), in_specs=[spec] * 4, out_specs=[spec] * 3,
                          out_shape=[jax.ShapeDtypeStruct((R, C), F32)] * 3, compiler_params=_params("parallel"))(w, g, m, v)


def _adamw_ada(c_all, dmod_cols, w, m, v, name):
    K, N = w.shape
    B = c_all.shape[0]
    tk, tn = _tile(K, 512), _tile(N, 1024)
    spec = pl.BlockSpec((tk, tn), lambda i, j: (i, j))

    def body(c_ref, dm_ref, w_ref, m_ref, v_ref, g_ref, d_ref, mo_ref, vo_ref):
        cv = c_ref[...]
        ca = (cv * jax.nn.sigmoid(cv)).astype(MXU_DTYPE)
        g = _dot(ca, dm_ref[...].astype(MXU_DTYPE), TN)
        g_ref[...] = g
        d_ref[...], mo_ref[...], vo_ref[...] = _adamw_math(w_ref[...], g, m_ref[...], v_ref[...])

    return pl.pallas_call(
        body, name=name, grid=(K // tk, N // tn),
        in_specs=[pl.BlockSpec((B, tk), lambda i, j: (0, i)), pl.BlockSpec((B, tn), lambda i, j: (0, j)), spec, spec, spec],
        out_specs=[spec] * 4, out_shape=[jax.ShapeDtypeStruct((K, N), F32)] * 4,
        compiler_params=_params("parallel", "parallel"))(c_all, dmod_cols, w, m, v)


def _sum_devices(gathered, n_dev, name):
    R = gathered.shape[0] // n_dev
    C = gathered.shape[1]
    tr = _tile(R, 512, 8)
    n_blk = R // tr

    def body(*refs):
        acc = refs[0][...]
        for r in refs[1:n_dev]:
            acc = acc + r[...]
        refs[n_dev][...] = acc

    in_specs = [pl.BlockSpec((tr, C), functools.partial(lambda i, d: (d * n_blk + i, 0), d=d)) for d in range(n_dev)]
    return pl.pallas_call(body, name=name, grid=(n_blk,), in_specs=in_specs,
                          out_specs=pl.BlockSpec((tr, C), lambda i: (i, 0)),
                          out_shape=jax.ShapeDtypeStruct((R, C), F32), compiler_params=_params("parallel"))(*([gathered] * n_dev))


def _place():
    x, y, c = lax.axis_index("x"), lax.axis_index("y"), lax.axis_index("c")
    return x, y, c


def _allgather8(blk, name):
    m_per, n = blk.shape

    def body(x_ref, out_ref, send_sems, recv_sems, local_sem):
        x, y, c = _place()
        me, sibling = (x, y, c), (x, y, 1 - c)
        chips = [(1 - x, y), (x, 1 - y), (1 - x, 1 - y)]

        def rows(px, py, pc):
            return out_ref.at[pl.ds((4 * px + 2 * py + pc) * m_per, m_per), :]

        def copy(k, block, to, src=None):
            return pltpu.make_async_remote_copy(
                src_ref=rows(*block) if src is None else src, dst_ref=rows(*block),
                send_sem=send_sems.at[k], recv_sem=recv_sems.at[k], device_id=to, device_id_type=MESH)

        mine = pltpu.make_async_copy(x_ref, rows(*me), local_sem)
        mine.start()
        first = [copy(0, me, sibling, src=x_ref)]
        first += [copy(1 + j, me, (*chip, c), src=x_ref) for j, chip in enumerate(chips)]
        for cp in first:
            cp.start()
        passed = [copy(4 + j, (*chip, c), sibling) for j, chip in enumerate(chips)]
        for j, chip in enumerate(chips):
            copy(1 + j, (*chip, c), me).wait_recv()
            passed[j].start()
        copy(0, sibling, me).wait_recv()
        for j, chip in enumerate(chips):
            copy(4 + j, (*chip, 1 - c), me).wait_recv()
        for cp in first + passed:
            cp.wait_send()
        mine.wait()

    return pl.pallas_call(
        body, name=name,
        out_shape=jax.ShapeDtypeStruct((8 * m_per, n), blk.dtype),
        in_specs=[pl.BlockSpec(memory_space=pltpu.VMEM)],
        out_specs=pl.BlockSpec(memory_space=pltpu.VMEM),
        scratch_shapes=[pltpu.SemaphoreType.DMA((7,)), pltpu.SemaphoreType.DMA((7,)), pltpu.SemaphoreType.DMA],
        compiler_params=pltpu.CompilerParams(vmem_limit_bytes=V7X_VMEM_LIMIT),
    )(blk)


class _Sharded:
    def __init__(self, shard_shape, by_cols):
        r, c = shard_shape
        self.by_cols = by_cols
        self.full = (r, N_CHIPS * c) if by_cols else (N_CHIPS * r, c)
        self.shard = (r, c)
        self.half_rows = r // 2
        self.half = (r // 2, c)

    def shard_of(self, ref, k):
        r, c = self.shard
        return ref.at[:, pl.ds(k * c, c)] if self.by_cols else ref.at[pl.ds(k * r, r), :]

    def half_of(self, ref, k, hc):
        r, c = self.shard
        h = self.half_rows
        if self.by_cols:
            return ref.at[pl.ds(hc * h, h), pl.ds(k * c, c)]
        return ref.at[pl.ds(k * r + hc * h, h), :]

    def chunk_of(self, ref, k, hc, ch, n):
        r, c = self.shard
        h = self.half_rows
        q = h // n
        if self.by_cols:
            return ref.at[pl.ds(hc * h + ch * q, q), pl.ds(k * c, c)]
        return ref.at[pl.ds(k * r + hc * h + ch * q, q), :]

    def half_of_shard(self, ref, hc):
        return ref.at[pl.ds(hc * self.half_rows, self.half_rows), :]

    def part_of_halves(self, ref, k):
        r, c = self.shard
        h = self.half_rows
        return ref.at[:, pl.ds(k * c, c)] if self.by_cols else ref.at[pl.ds(k * h, h), :]


def _on_each_place(x, y, c, fn, by_chip=True, by_core=True):
    q = 2 * x + y
    for k in range(N_CHIPS if by_chip else 1):
        for cc in range(2 if by_core else 1):
            cond = None
            if by_chip:
                cond = q == k
            if by_core:
                cond = (c == cc) if cond is None else jnp.logical_and(cond, c == cc)
            pl.when(cond)(functools.partial(fn, k, cc))


def _chip_id(k, c):
    return (k // 2, k % 2, c)


def _handshake(peers):
    barrier = pltpu.get_barrier_semaphore()
    for peer in peers:
        pl.semaphore_signal(barrier, inc=1, device_id=peer, device_id_type=MESH)
    pl.semaphore_wait(barrier, len(peers))


def _on_sequencer(body, inputs, out_structs, n_copies, peers_of, name, collective_id, return_inputs=False):
    in_refs = [jax.new_ref(a, memory_space=pltpu.MemorySpace.HBM) for a in inputs]
    out_refs = [jax.empty_ref(s, memory_space=pltpu.MemorySpace.HBM) for s in out_structs]

    @pl.kernel(mesh=plsc.ScalarSubcoreMesh(axis_name="sequencer", num_cores=1), name=name,
               scratch_types=(pltpu.SemaphoreType.DMA((n_copies,)), pltpu.SemaphoreType.DMA((n_copies,))),
               compiler_params=pltpu.CompilerParams(collective_id=collective_id))
    def launch(send_sems, recv_sems):
        x, y, c = _place()
        _handshake(peers_of(x, y, c))
        body(in_refs, out_refs, send_sems, recv_sems, x, y, c)

    launch()
    return [r[...] for r in (in_refs if return_inputs else out_refs)]


def _sibling(x, y, c):
    return [(x, y, 1 - c)]


def _same_core_of_other_chips(x, y, c):
    return [(1 - x, y, c), (x, 1 - y, c), (1 - x, 1 - y, c)]


GATHER_CHUNKS = 4
GATHER_COPIES = 6 * GATHER_CHUNKS


def _gather_weights(fulls, geoms, name, collective_id):
    n_w = len(fulls)
    n_ch, n_relay = GATHER_CHUNKS, GATHER_CHUNKS // 2
    f_refs = [jax.new_ref(f, memory_space=pltpu.MemorySpace.HBM) for f in fulls]
    FLIP_X, FLIP_Y, FLIP_BOTH = FLIPS
    TO_X, TO_Y, RELAY_TO_Y, RELAY_TO_X, ON_X, ON_Y, ON_DIAG = 0, n_ch, 2 * n_ch, 2 * n_ch + n_relay, 3 * n_ch, 4 * n_ch, 5 * n_ch

    @pl.kernel(mesh=plsc.ScalarSubcoreMesh(axis_name="sequencer", num_cores=1), name=name,
               scratch_types=(pltpu.SemaphoreType.DMA((GATHER_COPIES * n_w,)), pltpu.SemaphoreType.DMA((GATHER_COPIES * n_w,))),
               compiler_params=pltpu.CompilerParams(collective_id=collective_id))
    def launch(send_sems, recv_sems):
        x, y, c = _place()
        _handshake([(x, y, 1 - c), (1 - x, y, c), (x, 1 - y, c)])

        def at_place(k, cc):
            kx, ky, kd = k ^ FLIP_X, k ^ FLIP_Y, k ^ FLIP_BOTH
            me, sibling = _chip_id(k, cc), _chip_id(k, 1 - cc)
            started = []

            def copy(i, slot, src, dst, to, start=True):
                cp = pltpu.make_async_remote_copy(src_ref=src, dst_ref=dst, send_sem=send_sems.at[GATHER_COPIES * i + slot],
                                                  recv_sem=recv_sems.at[GATHER_COPIES * i + slot], device_id=to, device_id_type=MESH)
                if start:
                    cp.start()
                    started.append(cp)
                return cp

            def pass_on(i, slot, ref, to):
                copy(i, slot, ref, ref, to)

            def landed(i, slot, ref):
                copy(i, slot, ref, ref, me, start=False).wait_recv()

            y_order = [(n_relay + s) % n_ch for s in range(n_ch)]
            for i, (g, f_ref) in enumerate(zip(geoms, f_refs)):
                for s in range(n_ch):
                    pass_on(i, TO_X + s, g.chunk_of(f_ref, k, cc, s, n_ch), _chip_id(kx, cc))
                    pass_on(i, TO_Y + y_order[s], g.chunk_of(f_ref, k, cc, y_order[s], n_ch), _chip_id(ky, cc))
            for i, (g, f_ref) in enumerate(zip(geoms, f_refs)):
                for s in range(n_ch):
                    from_x = g.chunk_of(f_ref, kx, cc, s, n_ch)
                    landed(i, TO_X + s, from_x)
                    if s < n_relay:
                        pass_on(i, RELAY_TO_Y + s, from_x, _chip_id(ky, cc))
                    pass_on(i, ON_X + s, from_x, sibling)
                    ch = y_order[s]
                    from_y = g.chunk_of(f_ref, ky, cc, ch, n_ch)
                    landed(i, TO_Y + ch, from_y)
                    if ch >= n_relay:
                        pass_on(i, RELAY_TO_X + ch - n_relay, from_y, _chip_id(kx, cc))
                    pass_on(i, ON_Y + ch, from_y, sibling)
                for r in range(n_relay):
                    via_y = g.chunk_of(f_ref, kd, cc, r, n_ch)
                    landed(i, RELAY_TO_Y + r, via_y)
                    pass_on(i, ON_DIAG + r, via_y, sibling)
                    via_x = g.chunk_of(f_ref, kd, cc, n_relay + r, n_ch)
                    landed(i, RELAY_TO_X + r, via_x)
                    pass_on(i, ON_DIAG + n_relay + r, via_x, sibling)
            for i, (g, f_ref) in enumerate(zip(geoms, f_refs)):
                for slot, kk in ((ON_X, kx), (ON_Y, ky), (ON_DIAG, kd)):
                    for ch in range(n_ch):
                        landed(i, slot + ch, g.chunk_of(f_ref, kk, 1 - cc, ch, n_ch))
            for cp in started:
                cp.wait_send()

        _on_each_place(x, y, c, at_place)

    launch()
    return [f_ref[...] for f_ref in f_refs]


def _swap_core_halves(grads, geoms, name, collective_id):
    n_cp = sum(1 if g.by_cols else N_CHIPS for g in geoms)

    def body(g_refs, t_refs, send_sems, recv_sems, x, y, c):

        def at_place(_, cc):
            def pairs(hc):
                out = []
                for g, g_ref, t_ref in zip(geoms, g_refs, t_refs):
                    if g.by_cols:
                        out.append((g_ref.at[pl.ds(hc * g.half_rows, g.half_rows), :], t_ref))
                    else:
                        out += [(g.half_of(g_ref, k, hc), g.part_of_halves(t_ref, k)) for k in range(N_CHIPS)]
                return out

            sends = [pltpu.make_async_remote_copy(src_ref=src, dst_ref=dst, send_sem=send_sems.at[n],
                                                  recv_sem=recv_sems.at[n], device_id=(x, y, 1 - cc), device_id_type=MESH)
                     for n, (src, dst) in enumerate(pairs(1 - cc))]
            for cp in sends:
                cp.start()
            for n, (src, dst) in enumerate(pairs(cc)):
                pltpu.make_async_remote_copy(src_ref=src, dst_ref=dst, send_sem=send_sems.at[n], recv_sem=recv_sems.at[n],
                                             device_id=(x, y, cc), device_id_type=MESH).wait_recv()
            for cp in sends:
                cp.wait_send()

        _on_each_place(x, y, c, at_place, by_chip=False)

    return _on_sequencer(body, grads, [jax.ShapeDtypeStruct((g.full[0] // 2, g.full[1]), F32) for g in geoms],
                         n_cp, _sibling, name, collective_id)


def _scatter_chip_sums(sums, geoms, name, collective_id):
    def body(s_refs, r_refs, send_sems, recv_sems, x, y, c):

        def at_place(k, _):
            sends = []
            for i, (g, s_ref, r_ref) in enumerate(zip(geoms, s_refs, r_refs)):
                for j, flip in enumerate(FLIPS):
                    kk = k ^ flip
                    cp = pltpu.make_async_remote_copy(
                        src_ref=g.part_of_halves(s_ref, kk), dst_ref=r_ref.at[j], send_sem=send_sems.at[3 * i + j],
                        recv_sem=recv_sems.at[3 * i + j], device_id=(kk // 2, kk % 2, c), device_id_type=MESH)
                    cp.start()
                    sends.append(cp)
            for i, (g, s_ref, r_ref) in enumerate(zip(geoms, s_refs, r_refs)):
                for j in range(len(FLIPS)):
                    pltpu.make_async_remote_copy(
                        src_ref=g.part_of_halves(s_ref, k), dst_ref=r_ref.at[j], send_sem=send_sems.at[3 * i + j],
                        recv_sem=recv_sems.at[3 * i + j], device_id=(x, y, c), device_id_type=MESH).wait_recv()
            for cp in sends:
                cp.wait_send()

        _on_each_place(x, y, c, at_place, by_core=False)

    return _on_sequencer(body, sums, [jax.ShapeDtypeStruct((len(FLIPS),) + g.half, WIRE_DTYPE) for g in geoms],
                         len(FLIPS) * len(sums), _same_core_of_other_chips, name, collective_id)


def _share_reduced_halves(reduced, geoms, name, collective_id):
    def body(out_refs, _, send_sems, recv_sems, x, y, c):

        def at_place(_, cc):
            sends = []
            for i, (g, ref) in enumerate(zip(geoms, out_refs)):
                mine = g.half_of_shard(ref, cc)
                cp = pltpu.make_async_remote_copy(src_ref=mine, dst_ref=mine, send_sem=send_sems.at[i],
                                                  recv_sem=recv_sems.at[i], device_id=(x, y, 1 - cc), device_id_type=MESH)
                cp.start()
                sends.append(cp)
            for i, (g, ref) in enumerate(zip(geoms, out_refs)):
                theirs = g.half_of_shard(ref, 1 - cc)
                pltpu.make_async_remote_copy(src_ref=theirs, dst_ref=theirs, send_sem=send_sems.at[i],
                                             recv_sem=recv_sems.at[i], device_id=(x, y, cc), device_id_type=MESH).wait_recv()
            for cp in sends:
                cp.wait_send()

        _on_each_place(x, y, c, at_place, by_chip=False)

    return _on_sequencer(body, reduced, [], len(reduced), _sibling, name, collective_id, return_inputs=True)


def _chip_sum(place, grad, theirs, g, name):
    RH, C = theirs.shape
    h = g.half_rows
    tr = _tile(h, 256, 16)
    tc = _tile(C, 2048)
    per_half = h // tr

    if g.by_cols:
        grad_map = lambda i, j, p: (p[1] * per_half + i, j)
    else:
        grad_map = lambda i, j, p: ((i // per_half) * 2 * per_half + p[1] * per_half + i % per_half, j)

    def body(p_ref, a_ref, b_ref, o_ref):
        o_ref[...] = (a_ref[...] + b_ref[...]).astype(o_ref.dtype)

    return pl.pallas_call(
        body, name=name,
        grid_spec=pltpu.PrefetchScalarGridSpec(
            num_scalar_prefetch=1, grid=(RH // tr, C // tc),
            in_specs=[pl.BlockSpec((tr, tc), grad_map), pl.BlockSpec((tr, tc), lambda i, j, p: (i, j))],
            out_specs=pl.BlockSpec((tr, tc), lambda i, j, p: (i, j))),
        out_shape=jax.ShapeDtypeStruct((RH, C), WIRE_DTYPE),
        compiler_params=_params("parallel", "parallel"),
    )(place, grad, theirs)


def _reduce_half(place, grad, theirs, others, g, name):
    h, wc = g.half
    tr = _tile(h, 256, 16)
    per_half = h // tr
    if g.by_cols:
        tc = wc
        grad_map = lambda i, p: (p[1] * per_half + i, p[0])
        theirs_map = lambda i, p: (i, p[0])
    else:
        tc = wc
        grad_map = lambda i, p: (p[0] * 2 * per_half + p[1] * per_half + i, 0)
        theirs_map = lambda i, p: (p[0] * per_half + i, 0)

    def body(p_ref, a_ref, b_ref, o0_ref, o1_ref, o2_ref, out_ref):
        acc = a_ref[...] + b_ref[...]
        for o_ref in (o0_ref, o1_ref, o2_ref):
            acc = acc + o_ref[...].astype(F32)
        out_ref[...] = acc

    other_specs = [pl.BlockSpec((None, tr, tc), functools.partial(lambda i, p, j: (j, i, 0), j=j)) for j in range(len(FLIPS))]
    return pl.pallas_call(
        body, name=name,
        grid_spec=pltpu.PrefetchScalarGridSpec(
            num_scalar_prefetch=1, grid=(per_half,),
            in_specs=[pl.BlockSpec((tr, tc), grad_map), pl.BlockSpec((tr, tc), theirs_map)] + other_specs,
            out_specs=pl.BlockSpec((tr, tc), lambda i, p: (p[1] * per_half + i, 0))),
        out_shape=jax.ShapeDtypeStruct(g.shard, F32),
        compiler_params=_params("arbitrary"),
    )(place, grad, theirs, others, others, others)


SMALL = ("b_ada", "norm1_g", "v_norm_g", "w_spatial", "b_spatial", "out_norm_g", "norm2_g", "final_g")
BIG = ("w_in", "w_out", "w_gate", "w_up", "w_down")
BY_COLS = {"w_in": True, "w_out": False, "w_gate": True, "w_up": True, "w_down": False}
ORDER = ("w_ada", "b_ada", "norm1_g", "w_in", "v_norm_g", "w_spatial", "b_spatial", "out_norm_g", "w_out",
         "norm2_g", "w_gate", "w_up", "w_down", "final_g")


def _pack(parts):
    return jnp.concatenate([parts[n].reshape(-1) for n in SMALL]).reshape(-1, LANE)


def _unpack(slab, shapes):
    flat = slab.reshape(-1)
    out, at = {}, 0
    for n in SMALL:
        size = math.prod(shapes[n])
        out[n] = flat[at:at + size].reshape(shapes[n])
        at += size
    return out


def kernel(x, c, w_ada, b_ada, norm1_g, w_in, v_norm_g, w_spatial, b_spatial, out_norm_g, w_out, norm2_g, w_gate, w_up, w_down, final_g, loss_target, m_w_ada, m_b_ada, m_norm1_g, m_w_in, m_v_norm_g, m_w_spatial, m_b_spatial, m_out_norm_g, m_w_out, m_norm2_g, m_w_gate, m_w_up, m_w_down, m_final_g, v_w_ada, v_b_ada, v_norm1_g, v_w_in, v_v_norm_g, v_w_spatial, v_b_spatial, v_out_norm_g, v_w_out, v_norm2_g, v_w_gate, v_w_up, v_w_down, v_final_g):
    weights = dict(w_ada=w_ada, b_ada=b_ada, norm1_g=norm1_g, w_in=w_in, v_norm_g=v_norm_g, w_spatial=w_spatial,
                   b_spatial=b_spatial, out_norm_g=out_norm_g, w_out=w_out, norm2_g=norm2_g, w_gate=w_gate, w_up=w_up,
                   w_down=w_down, final_g=final_g)
    m_in = dict(w_ada=m_w_ada, b_ada=m_b_ada, norm1_g=m_norm1_g, w_in=m_w_in, v_norm_g=m_v_norm_g, w_spatial=m_w_spatial,
                b_spatial=m_b_spatial, out_norm_g=m_out_norm_g, w_out=m_w_out, norm2_g=m_norm2_g, w_gate=m_w_gate,
                w_up=m_w_up, w_down=m_w_down, final_g=m_final_g)
    v_in = dict(w_ada=v_w_ada, b_ada=v_b_ada, norm1_g=v_norm1_g, w_in=v_w_in, v_norm_g=v_v_norm_g, w_spatial=v_w_spatial,
                b_spatial=v_b_spatial, out_norm_g=v_out_norm_g, w_out=v_w_out, norm2_g=v_norm2_g, w_gate=v_w_gate,
                w_up=v_w_up, w_down=v_w_down, final_g=v_final_g)

    S, D = x.shape[1], x.shape[2]
    n_g = v_norm_g.shape[-1] // LANE
    n_h = (D - n_g * LANE) // LANE
    GW = n_g * LANE
    xi, yi, ci = _place()
    chip = 2 * xi + yi
    me = 4 * xi + 2 * yi + ci
    place = jnp.stack([chip, ci]).astype(jnp.int32)

    xs, target = x[0], loss_target[0]
    geoms = [_Sharded(weights[n].shape[1:], BY_COLS[n]) for n in BIG]

    full = {}
    for i, group in enumerate((("w_in",), ("w_out",), ("w_gate", "w_up"), ("w_down",))):
        gg = [geoms[BIG.index(n)] for n in group]
        own = [_cast_into_full(place, weights[n][0], g, "cast_" + n) for n, g in zip(group, gg)]
        gathered = _gather_weights(own, gg, "gather_" + "_".join(group), 1 + i)
        full.update(zip(group, gathered))

    c_pad = jnp.concatenate([c, jnp.zeros((7, D), F32)], axis=0)
    c_all = _allgather8(c_pad, "gather_c")[::8]
    n_ada = w_ada.shape[2]
    b_cols = lax.dynamic_slice(b_ada, (0, chip * n_ada), (1, n_ada))
    mod_parts = _allgather8(_mod_part(c_all, w_ada[0], b_cols, "mod_part"), "gather_mod")
    mod_all = mod_parts.reshape(N_CHIPS, 2, 8, n_ada)[:, 0].transpose(1, 0, 2).reshape(8, N_CHIPS * n_ada)
    mod = lax.dynamic_slice(mod_all, (me, 0), (1, 6 * D))
    shift1, scale1, gate1, shift2, scale2, gate2 = [mod[:, i * D:(i + 1) * D] for i in range(6)]

    b_t = b_spatial[0].T
    h1 = _norm_mod(xs, norm1_g, scale1, shift1, "norm1")
    proj, = _mm("nn", h1, full["w_in"], [F32], "proj")
    on_gm = _gmlp_fwd(proj, v_norm_g, w_spatial[0], b_t, out_norm_g, n_g, "gmlp_fwd")
    o_sb, on_sb, l_sum = _sb_fwd(proj, out_norm_g, n_g, n_h, "sb_fwd")
    o_n = jnp.concatenate([on_gm, on_sb], axis=1)
    attn, = _mm("nn", o_n, full["w_out"], [F32], "attn_out")
    x1, h2 = _residual_norm_mod(xs, attn, gate1, norm2_g, scale2, shift2, "norm2")
    a_g, a_u, f_in = _gate_up(h2, full["w_gate"], full["w_up"], "gate_up")
    f = _mm_ktiled("nn", [(f_in, full["w_down"])], "down")
    dx2, df, d_gate2, d_final_g, loss_part = _final_loss_bwd(x1, f, gate2, final_g.reshape(1, D), target, "final")
    loss = lax.psum(loss_part[0, 0], ("x", "y", "c"))

    geom_of = dict(zip(BIG, geoms))
    grad_out, delta, new_m, new_v = {}, {}, {}, {}

    def swap(group, grads, collective_id):
        return _swap_core_halves(grads, [geom_of[n] for n in group], "swap_" + "_".join(group), collective_id)

    def chip_sums(group, grads, theirs, after):
        return [_chip_sum(place, gr, _then(after, t), geom_of[n], "chip_sum_" + n) for n, gr, t in zip(group, grads, theirs)]

    def scatter(group, sums, collective_id):
        return _scatter_chip_sums(sums, [geom_of[n] for n in group], "scatter_" + "_".join(group), collective_id)

    def reduce_halves(group, grads, theirs, others, after):
        return [_reduce_half(place, gr, t, _then(after, o), geom_of[n], "reduce_" + n)
                for n, gr, t, o in zip(group, grads, theirs, others)]

    def share(group, halves, collective_id):
        return _share_reduced_halves(halves, [geom_of[n] for n in group], "share_" + "_".join(group), collective_id)

    def adamw(group, reduced, after):
        for n, r in zip(group, reduced):
            grad_out[n] = r[None]
            d, mo, vo = _adamw(weights[n][0], _then(after, r), m_in[n][0], v_in[n][0], "adamw_" + n)
            delta[n], new_m[n], new_v[n] = d[None], mo[None], vo[None]
        return d

    g_down = ("w_down",)
    g_ffn = ("w_gate", "w_up")
    g_out = ("w_out",)
    g_in = ("w_in",)

    gr_down = _mm("tn", f_in, df, [F32], "d_w_down", tm=1408, tn=1024)
    th_down = swap(g_down, gr_down, 6)
    d_ag, d_au = _mm("nt", df, full["w_down"], [MXU_DTYPE, MXU_DTYPE], "d_ffn_in", extras=(a_g, a_u),
                     epilogue=_swiglu_bwd_epilogue)
    sm_down = chip_sums(g_down, gr_down, th_down, after=d_ag)
    ot_down = scatter(g_down, sm_down, 7)
    gr_ffn = [_mm("tn", h2, _then(sm_down, d_ag), [F32], "d_w_gate")[0], _mm("tn", h2, d_au, [F32], "d_w_up")[0]]
    th_ffn = swap(g_ffn, gr_ffn, 9)
    dh2 = _mm_ktiled("nt", [(_then(gr_ffn, d_ag), full["w_gate"]), (d_au, full["w_up"])], "d_h2", tn=512)
    sm_ffn = chip_sums(g_ffn, gr_ffn, th_ffn, after=dh2)
    ot_ffn = scatter(g_ffn, sm_ffn, 10)
    hv_down = reduce_halves(g_down, gr_down, th_down, ot_down, after=sm_ffn)
    rd_down = share(g_down, hv_down, 8)
    dx1, d_shift2, d_scale2, d_norm2_g, d_gate1, d_attn = _norm_mod_bwd(
        _then(hv_down, dh2), x1, dx2, norm2_g, scale2, "norm2_bwd", branch=attn, gate=gate1)
    gr_out = _mm("tn", o_n, d_attn, [F32], "d_w_out")
    th_out = swap(g_out, gr_out, 12)
    d_on, = _mm("nt", _then(gr_out, d_attn), full["w_out"], [F32], "d_o")
    dp_gm, d_w_spatial, d_b_t, d_v_norm_g, d_og_gm = _gmlp_bwd(proj, d_on, v_norm_g, w_spatial[0], b_t, out_norm_g, n_g, "gmlp_bwd")
    dq, dk, dv, d_og_sb = _sb_bwd(proj, o_sb, l_sum, _then(dp_gm, d_on), out_norm_g, n_g, n_h, "sb_bwd")
    sm_out = chip_sums(g_out, gr_out, th_out, after=dq)
    ot_out = scatter(g_out, sm_out, 13)
    hv_ffn = reduce_halves(g_ffn, gr_ffn, th_ffn, ot_ffn, after=sm_out)
    rd_ffn = share(g_ffn, hv_ffn, 11)
    dproj = jnp.concatenate([_then(hv_ffn, dp_gm), dq, dk, dv], axis=1)
    gr_in = _mm("tn", h1, dproj, [F32], "d_w_in")
    th_in = swap(g_in, gr_in, 15)
    dh1 = _mm_ktiled("nt", [(_then(gr_in, dproj), full["w_in"])], "d_h1")
    grad_x, d_shift1, d_scale1, d_norm1_g = _norm_mod_bwd(dh1, xs, dx1, norm1_g, scale1, "norm1_bwd")
    sm_in = chip_sums(g_in, gr_in, th_in, after=grad_x)
    ot_in = scatter(g_in, sm_in, 16)
    hv_out = reduce_halves(g_out, gr_out, th_out, ot_out, after=sm_in)
    rd_out = share(g_out, hv_out, 14)

    dmod = jnp.concatenate([d_shift1, d_scale1, d_gate1, d_shift2, d_scale2, d_gate2], axis=1)
    small_parts = dict(b_ada=dmod, norm1_g=d_norm1_g, v_norm_g=d_v_norm_g, w_spatial=d_w_spatial, b_spatial=d_b_t.T,
                       out_norm_g=jnp.concatenate([d_og_gm, d_og_sb], axis=1), norm2_g=d_norm2_g, final_g=d_final_g)
    slab = _then(hv_out, _pack(small_parts))
    rows = slab.shape[0]
    gathered = _allgather8(slab, "gather_small")
    small_shapes = {n: weights[n].shape for n in SMALL}
    small_sum = _sum_devices(gathered, 8, "sum_small")
    dmod_all = gathered.reshape(8, rows * LANE)[:, :6 * D]
    dmod_cols = lax.dynamic_slice(dmod_all, (0, chip * n_ada), (8, n_ada))
    g_ada, d, mo, vo = _adamw_ada(c_all, dmod_cols, w_ada[0], m_w_ada[0], v_w_ada[0], "adamw_w_ada")
    grad_out["w_ada"], delta["w_ada"], new_m["w_ada"], new_v["w_ada"] = g_ada[None], d[None], mo[None], vo[None]
    d_small, mo, vo = _adamw(_pack({n: weights[n] for n in SMALL}), small_sum, _pack({n: m_in[n] for n in SMALL}),
                             _pack({n: v_in[n] for n in SMALL}), "adamw_small")
    for dst, slab_out in ((grad_out, small_sum), (delta, d_small), (new_m, mo), (new_v, vo)):
        dst.update(_unpack(slab_out, small_shapes))
    done = adamw(g_down, rd_down, after=d)
    done = adamw(g_ffn, rd_ffn, after=done)
    done = adamw(g_out, rd_out, after=done)
    hv_in = reduce_halves(g_in, gr_in, th_in, ot_in, after=done)
    adamw(g_in, share(g_in, hv_in, 17), after=done)

    return (loss, grad_x[None], *[grad_out[n] for n in ORDER], *[delta[n] for n in ORDER],
            *[new_m[n] for n in ORDER], *[new_v[n] for n in ORDER])
```

```python
import functools
import math

import jax
import jax.numpy as jnp
from jax import lax
from jax.experimental import pallas as pl
from jax.experimental.pallas import tpu as pltpu
from jax.experimental.pallas import tpu_sc as plsc

F32 = jnp.float32
MXU_DTYPE = jnp.bfloat16
WIRE_DTYPE = jnp.bfloat16
EPS = 1e-6
LANE = 128
V7X_VMEM_LIMIT = 56 * 1024 * 1024
MESH = pl.DeviceIdType.MESH
N_CHIPS = 4
FLIPS = (2, 1, 3)

ADAM_LR = 0.001
ADAM_B1 = 0.9
ADAM_B2 = 0.999
ADAM_EPS = 1e-08
ADAM_WD = 0.01
ADAM_STEP = 10


def _params(*semantics):
    return pltpu.CompilerParams(dimension_semantics=semantics or None, vmem_limit_bytes=V7X_VMEM_LIMIT)


def _tile(dim, pref, unit=LANE):
    best = None
    t = unit
    while t <= min(dim, pref):
        if dim % t == 0:
            best = t
        t += unit
    return best if best is not None else dim


def _then(first, second):
    return lax.optimization_barrier((first, second))[1]


def _sum0(v):
    return jnp.sum(v, axis=0, keepdims=True)


def _mean1(v):
    return jnp.mean(v, axis=-1, keepdims=True)


def _gelu(x):
    return 0.5 * x * (1.0 + lax.erf(x * (1.0 / math.sqrt(2.0))))


def _gelu_grad(x):
    cdf = 0.5 * (1.0 + lax.erf(x * (1.0 / math.sqrt(2.0))))
    return cdf + x * jnp.exp(-0.5 * x * x) * (1.0 / math.sqrt(2.0 * math.pi))


def _dot(a, b, dims):
    return lax.dot_general(a, b, (dims, ((), ())), preferred_element_type=F32)


NN = ((1,), (0,))
NT = ((1,), (1,))
TN = ((0,), (0,))


def _mm(kind, a, b, out_dtypes, name, tm=2048, tn=512, extras=(), epilogue=None):
    if kind == "nn":
        (M, K), N = a.shape, b.shape[1]
    elif kind == "nt":
        (M, K), N = a.shape, b.shape[0]
    else:
        (K, M), N = a.shape, b.shape[1]
    tm, tn = _tile(M, tm), _tile(N, tn)
    a_spec = pl.BlockSpec((K, tm), lambda i, j: (0, i)) if kind == "tn" else pl.BlockSpec((tm, K), lambda i, j: (i, 0))
    b_spec = pl.BlockSpec((tn, K), lambda i, j: (j, 0)) if kind == "nt" else pl.BlockSpec((K, tn), lambda i, j: (0, j))
    mn_spec = pl.BlockSpec((tm, tn), lambda i, j: (i, j))
    dims = {"nn": NN, "nt": NT, "tn": TN}[kind]
    n_extra = len(extras)

    def body(a_ref, b_ref, *rest):
        acc = _dot(a_ref[...], b_ref[...], dims)
        res = (acc,) if epilogue is None else epilogue(acc, *[e[...] for e in rest[:n_extra]])
        for o_ref, r in zip(rest[n_extra:], res):
            o_ref[...] = r.astype(o_ref.dtype)

    outs = pl.pallas_call(
        body, name=name, grid=(M // tm, N // tn),
        in_specs=[a_spec, b_spec] + [mn_spec] * n_extra,
        out_specs=[mn_spec] * len(out_dtypes),
        out_shape=[jax.ShapeDtypeStruct((M, N), d) for d in out_dtypes],
        compiler_params=_params("parallel", "arbitrary"),
    )(a, b, *extras)
    return outs


def _mm_ktiled(kind, pairs, name, tm=2048, tn=1024, tk=1408):
    a0, b0 = pairs[0]
    M, K = a0.shape
    N = b0.shape[1] if kind == "nn" else b0.shape[0]
    tm, tn, tk = _tile(M, tm), _tile(N, tn), _tile(K, tk)
    a_spec = pl.BlockSpec((tm, tk), lambda i, j, k: (i, k))
    b_spec = pl.BlockSpec((tk, tn), lambda i, j, k: (k, j)) if kind == "nn" else pl.BlockSpec((tn, tk), lambda i, j, k: (j, k))
    dims = NN if kind == "nn" else NT
    n_pairs = len(pairs)

    def body(*refs):
        o_ref = refs[2 * n_pairs]
        acc = _dot(refs[0][...], refs[1][...], dims)
        for p in range(1, n_pairs):
            acc = acc + _dot(refs[2 * p][...], refs[2 * p + 1][...], dims)

        @pl.when(pl.program_id(2) == 0)
        def _():
            o_ref[...] = acc

        @pl.when(pl.program_id(2) != 0)
        def _():
            o_ref[...] += acc

    return pl.pallas_call(
        body, name=name, grid=(M // tm, N // tn, K // tk),
        in_specs=[a_spec, b_spec] * n_pairs,
        out_specs=pl.BlockSpec((tm, tn), lambda i, j, k: (i, j)),
        out_shape=jax.ShapeDtypeStruct((M, N), F32),
        compiler_params=_params("parallel", "parallel", "arbitrary"),
    )(*[x for pair in pairs for x in pair])


def _gate_up(h, wg, wu, name):
    (M, K), N = h.shape, wg.shape[1]
    tm, tn = _tile(M, 2048), _tile(N, 512)

    def body(h_ref, wg_ref, wu_ref, ag_ref, au_ref, f_ref):
        hv = h_ref[...]
        ag = _dot(hv, wg_ref[...], NN)
        au = _dot(hv, wu_ref[...], NN)
        ag_ref[...] = ag.astype(ag_ref.dtype)
        au_ref[...] = au.astype(au_ref.dtype)
        f_ref[...] = (ag * jax.nn.sigmoid(ag) * au).astype(f_ref.dtype)

    w_spec = pl.BlockSpec((K, tn), lambda i, j: (0, j))
    mn_spec = pl.BlockSpec((tm, tn), lambda i, j: (i, j))
    return pl.pallas_call(
        body, name=name, grid=(M // tm, N // tn),
        in_specs=[pl.BlockSpec((tm, K), lambda i, j: (i, 0)), w_spec, w_spec],
        out_specs=[mn_spec] * 3,
        out_shape=[jax.ShapeDtypeStruct((M, N), MXU_DTYPE)] * 3,
        compiler_params=_params("parallel", "arbitrary"),
    )(h, wg, wu)


def _swiglu_bwd_epilogue(dfin, ag, au):
    ag, au = ag.astype(F32), au.astype(F32)
    sg = jax.nn.sigmoid(ag)
    d_au = dfin * (ag * sg)
    d_ag = dfin * au * (sg * (1.0 + ag * (1.0 - sg)))
    return d_ag, d_au


def _row_specs(ts, width):
    return pl.BlockSpec((ts, width), lambda i: (i, 0)), pl.BlockSpec((1, width), lambda i: (0, 0))


def _cast_into_full(place, shard, g, name):
    R, C = shard.shape
    tr = _tile(R, 256, 16)
    n_blk = R // tr
    out_map = (lambda i, p: (i, p[0])) if g.by_cols else (lambda i, p: (p[0] * n_blk + i, 0))

    def body(p_ref, a_ref, o_ref):
        o_ref[...] = a_ref[...].astype(o_ref.dtype)

    return pl.pallas_call(
        body, name=name,
        grid_spec=pltpu.PrefetchScalarGridSpec(
            num_scalar_prefetch=1, grid=(n_blk,),
            in_specs=[pl.BlockSpec((tr, C), lambda i, p: (i, 0))],
            out_specs=pl.BlockSpec((tr, C), out_map)),
        out_shape=jax.ShapeDtypeStruct(g.full, WIRE_DTYPE),
        compiler_params=_params("arbitrary"),
    )(place, shard)


def _norm_mod(x, g, scale, shift, name):
    S, D = x.shape
    ts = _tile(S, 256, 16)
    tile, vec = _row_specs(ts, D)

    def body(x_ref, g_ref, sc_ref, sh_ref, h_ref):
        xv = x_ref[...]
        r = lax.rsqrt(_mean1(xv * xv) + EPS)
        h_ref[...] = ((xv * r) * g_ref[...] * (1.0 + sc_ref[...]) + sh_ref[...]).astype(h_ref.dtype)

    return pl.pallas_call(body, name=name, grid=(S // ts,), in_specs=[tile, vec, vec, vec], out_specs=tile,
                          out_shape=jax.ShapeDtypeStruct((S, D), MXU_DTYPE), compiler_params=_params("parallel"))(x, g, scale, shift)


def _residual_norm_mod(x, attn, gate, g, scale, shift, name):
    S, D = x.shape
    ts = _tile(S, 256, 16)
    tile, vec = _row_specs(ts, D)

    def body(x_ref, a_ref, gate_ref, g_ref, sc_ref, sh_ref, x1_ref, h_ref):
        x1 = x_ref[...] + gate_ref[...] * a_ref[...]
        x1_ref[...] = x1
        r = lax.rsqrt(_mean1(x1 * x1) + EPS)
        h_ref[...] = ((x1 * r) * g_ref[...] * (1.0 + sc_ref[...]) + sh_ref[...]).astype(h_ref.dtype)

    return pl.pallas_call(body, name=name, grid=(S // ts,), in_specs=[tile, tile, vec, vec, vec, vec],
                          out_specs=[tile, tile],
                          out_shape=[jax.ShapeDtypeStruct((S, D), F32), jax.ShapeDtypeStruct((S, D), MXU_DTYPE)],
                          compiler_params=_params("parallel"))(x, attn, gate, g, scale, shift)


def _final_loss_bwd(x1, f, gate2, final_g, target, name):
    S, D = x1.shape
    ts = _tile(S, 256, 16)
    tile, vec = _row_specs(ts, D)
    loss_spec = pl.BlockSpec((1, LANE), lambda i: (0, 0))

    def body(x1_ref, f_ref, gate_ref, g_ref, t_ref, dx2_ref, df_ref, dgate_ref, dg_ref, loss_ref):
        @pl.when(pl.program_id(0) == 0)
        def _():
            dgate_ref[...] = jnp.zeros_like(dgate_ref)
            dg_ref[...] = jnp.zeros_like(dg_ref)
            loss_ref[...] = jnp.zeros_like(loss_ref)

        fv, gate, g = f_ref[...], gate_ref[...], g_ref[...]
        x2 = x1_ref[...] + gate * fv
        r = lax.rsqrt(_mean1(x2 * x2) + EPS)
        xn = x2 * r
        err = xn * g - t_ref[...]
        loss_ref[...] += jnp.broadcast_to(0.5 * _sum0(_mean1(err * err)), loss_ref.shape)
        dy = err * (1.0 / D)
        dg_ref[...] += _sum0(dy * xn)
        dxn = dy * g
        dx2 = r * (dxn - xn * _mean1(dxn * xn))
        dx2_ref[...] = dx2
        dgate_ref[...] += _sum0(dx2 * fv)
        df_ref[...] = (dx2 * gate).astype(df_ref.dtype)

    return pl.pallas_call(
        body, name=name, grid=(S // ts,), in_specs=[tile, tile, vec, vec, tile],
        out_specs=[tile, tile, vec, vec, loss_spec],
        out_shape=[jax.ShapeDtypeStruct((S, D), F32), jax.ShapeDtypeStruct((S, D), MXU_DTYPE),
                   jax.ShapeDtypeStruct((1, D), F32), jax.ShapeDtypeStruct((1, D), F32),
                   jax.ShapeDtypeStruct((1, LANE), F32)],
        compiler_params=_params("arbitrary"),
    )(x1, f, gate2, final_g, target)


def _norm_mod_bwd(dh, xin, dres, g, scale, name, branch=None, gate=None):
    S, D = xin.shape
    ts = _tile(S, 256, 16)
    tile, vec = _row_specs(ts, D)
    with_gate = branch is not None

    def body(*refs):
        if with_gate:
            dh_ref, x_ref, dres_ref, g_ref, sc_ref, br_ref, gate_ref, dx_ref, dshift_ref, dscale_ref, dg_ref, dgate_ref, dbr_ref = refs
            accs = (dshift_ref, dscale_ref, dg_ref, dgate_ref)
        else:
            dh_ref, x_ref, dres_ref, g_ref, sc_ref, dx_ref, dshift_ref, dscale_ref, dg_ref = refs
            accs = (dshift_ref, dscale_ref, dg_ref)

        @pl.when(pl.program_id(0) == 0)
        def _():
            for acc in accs:
                acc[...] = jnp.zeros_like(acc)

        dh_v, xv, g_v = dh_ref[...], x_ref[...], g_ref[...]
        one_sc = 1.0 + sc_ref[...]
        r = lax.rsqrt(_mean1(xv * xv) + EPS)
        xn = xv * r
        dshift_ref[...] += _sum0(dh_v)
        dscale_ref[...] += _sum0(dh_v * (xn * g_v))
        dg_ref[...] += _sum0(dh_v * one_sc * xn)
        dxn = dh_v * (g_v * one_sc)
        dx = dres_ref[...] + r * (dxn - xn * _mean1(dxn * xn))
        dx_ref[...] = dx
        if with_gate:
            dgate_ref[...] += _sum0(dx * br_ref[...])
            dbr_ref[...] = (dx * gate_ref[...]).astype(dbr_ref.dtype)

    ins = [dh, xin, dres, g, scale] + ([branch, gate] if with_gate else [])
    in_specs = [tile, tile, tile, vec, vec] + ([tile, vec] if with_gate else [])
    out_specs = [tile, vec, vec, vec] + ([vec, tile] if with_gate else [])
    out_shape = [jax.ShapeDtypeStruct((S, D), F32)] + [jax.ShapeDtypeStruct((1, D), F32)] * 3
    if with_gate:
        out_shape += [jax.ShapeDtypeStruct((1, D), F32), jax.ShapeDtypeStruct((S, D), MXU_DTYPE)]
    return pl.pallas_call(body, name=name, grid=(S // ts,), in_specs=in_specs, out_specs=out_specs,
                          out_shape=out_shape, compiler_params=_params("arbitrary"))(*ins)


def _causal_weights(ws_ref, wt_ref, n_g):
    row = lax.broadcasted_iota(jnp.int32, (LANE, LANE), 0)
    col = lax.broadcasted_iota(jnp.int32, (LANE, LANE), 1)
    for g in range(n_g):
        wt_ref[g] = jnp.where(col <= row, ws_ref[g], 0.0).astype(wt_ref.dtype)


def _group_layernorm(v):
    xc = v - _mean1(v)
    rstd = lax.rsqrt(_mean1(xc * xc) + EPS)
    return xc * rstd, rstd


def _gmlp_fwd(proj, v_gain, w_s, b_t, out_gain, n_g, name):
    S = proj.shape[0]
    GW = n_g * LANE

    def body(p_ref, vg_ref, ws_ref, bt_ref, og_ref, on_ref, wt_ref):
        @pl.when(pl.program_id(0) == 0)
        def _():
            _causal_weights(ws_ref, wt_ref, n_g)

        for g in range(n_g):
            cols = slice(g * LANE, (g + 1) * LANE)
            u = _gelu(p_ref[:, cols])
            v = _gelu(p_ref[:, GW + g * LANE:GW + (g + 1) * LANE])
            vhat, _ = _group_layernorm(v)
            vln = (vhat * vg_ref[:, cols]).astype(MXU_DTYPE)
            mixed = _dot(wt_ref[g], vln, NN) + bt_ref[:, g:g + 1]
            o = u * mixed
            r = lax.rsqrt(_mean1(o * o) + EPS)
            on_ref[:, cols] = (o * r * og_ref[:, cols]).astype(on_ref.dtype)

    return pl.pallas_call(
        body, name=name, grid=(S // LANE,),
        in_specs=[pl.BlockSpec((LANE, 2 * GW), lambda n: (n, 0)),
                  pl.BlockSpec((1, GW), lambda n: (0, 0)),
                  pl.BlockSpec((n_g, LANE, LANE), lambda n: (0, 0, 0)),
                  pl.BlockSpec((LANE, n_g), lambda n: (0, 0)),
                  pl.BlockSpec((1, GW), lambda n: (0, 0))],
        out_specs=pl.BlockSpec((LANE, GW), lambda n: (n, 0)),
        out_shape=jax.ShapeDtypeStruct((S, GW), MXU_DTYPE),
        scratch_shapes=[pltpu.VMEM((n_g, LANE, LANE), MXU_DTYPE)],
        compiler_params=_params("arbitrary"),
    )(proj, v_gain, w_s, b_t, out_gain)


def _gmlp_bwd(proj, d_on, v_gain, w_s, b_t, out_gain, n_g, name):
    S = proj.shape[0]
    GW = n_g * LANE

    def body(p_ref, dn_ref, vg_ref, ws_ref, bt_ref, og_ref, dp_ref, dws_ref, dbt_ref, dvg_ref, dog_ref, wt_ref):
        @pl.when(pl.program_id(0) == 0)
        def _():
            _causal_weights(ws_ref, wt_ref, n_g)
            dws_ref[...] = jnp.zeros_like(dws_ref)
            dbt_ref[...] = jnp.zeros_like(dbt_ref)
            dvg_ref[...] = jnp.zeros_like(dvg_ref)
            dog_ref[...] = jnp.zeros_like(dog_ref)

        row = lax.broadcasted_iota(jnp.int32, (LANE, LANE), 0)
        col = lax.broadcasted_iota(jnp.int32, (LANE, LANE), 1)
        for g in range(n_g):
            cols = slice(g * LANE, (g + 1) * LANE)
            vcols = slice(GW + g * LANE, GW + (g + 1) * LANE)
            pu, pv = p_ref[:, cols], p_ref[:, vcols]
            u, v = _gelu(pu), _gelu(pv)
            vhat, rstd = _group_layernorm(v)
            gain = vg_ref[:, cols]
            vln = (vhat * gain).astype(MXU_DTYPE)
            mixed = _dot(wt_ref[g], vln, NN) + bt_ref[:, g:g + 1]
            o = u * mixed
            r = lax.rsqrt(_mean1(o * o) + EPS)
            oh = o * r
            dn = dn_ref[:, cols]
            dog_ref[:, cols] += _sum0(dn * oh)
            dhn = dn * og_ref[:, cols]
            d_o = r * (dhn - oh * _mean1(dhn * oh))
            du = d_o * mixed
            dmix = d_o * u
            dbt_ref[:, g:g + 1] += jnp.sum(dmix, axis=1, keepdims=True)
            dmix_b = dmix.astype(MXU_DTYPE)
            dws_ref[g] += jnp.where(col <= row, _dot(dmix_b, vln, NT), 0.0)
            dvln = _dot(wt_ref[g], dmix_b, TN)
            dvg_ref[:, cols] += _sum0(dvln * vhat)
            dxh = dvln * gain
            dv = rstd * (dxh - _mean1(dxh) - vhat * _mean1(dxh * vhat))
            dp_ref[:, cols] = (du * _gelu_grad(pu)).astype(dp_ref.dtype)
            dp_ref[:, vcols] = (dv * _gelu_grad(pv)).astype(dp_ref.dtype)

    return pl.pallas_call(
        body, name=name, grid=(S // LANE,),
        in_specs=[pl.BlockSpec((LANE, 2 * GW), lambda n: (n, 0)),
                  pl.BlockSpec((LANE, GW), lambda n: (n, 0)),
                  pl.BlockSpec((1, GW), lambda n: (0, 0)),
                  pl.BlockSpec((n_g, LANE, LANE), lambda n: (0, 0, 0)),
                  pl.BlockSpec((LANE, n_g), lambda n: (0, 0)),
                  pl.BlockSpec((1, GW), lambda n: (0, 0))],
        out_specs=[pl.BlockSpec((LANE, 2 * GW), lambda n: (n, 0)),
                   pl.BlockSpec((n_g, LANE, LANE), lambda n: (0, 0, 0)),
                   pl.BlockSpec((LANE, n_g), lambda n: (0, 0)),
                   pl.BlockSpec((1, GW), lambda n: (0, 0)),
                   pl.BlockSpec((1, GW), lambda n: (0, 0))],
        out_shape=[jax.ShapeDtypeStruct((S, 2 * GW), MXU_DTYPE),
                   jax.ShapeDtypeStruct((n_g, LANE, LANE), F32),
                   jax.ShapeDtypeStruct((LANE, n_g), F32),
                   jax.ShapeDtypeStruct((1, GW), F32),
                   jax.ShapeDtypeStruct((1, GW), F32)],
        scratch_shapes=[pltpu.VMEM((n_g, LANE, LANE), MXU_DTYPE)],
        compiler_params=_params("arbitrary"),
    )(proj, d_on, v_gain, w_s, b_t, out_gain)


def _tri_sum(v, tri, exact=True):
    hi = v.astype(MXU_DTYPE)
    if not exact:
        return _dot(hi, tri, NN)
    lo = (v - hi.astype(F32)).astype(MXU_DTYPE)
    return _dot(hi, tri, NN) + _dot(lo, tri, NN)


def _log_sigmoids(z):
    sp = jnp.log1p(jnp.exp(-jnp.abs(z)))
    return jnp.minimum(z, 0.0) - sp, jnp.minimum(-z, 0.0) - sp


def _rows(i, size):
    return pl.ds(pl.multiple_of(i * size, size), size)


SB_QUERY_TILE = 512
SB_KEY_TILE = 256


def _sb_tiles(S):
    tq = _tile(S, SB_QUERY_TILE)
    tk = _tile(tq, SB_KEY_TILE)
    return tq, tk, S // tq, tq // tk


def _triangle(n, keep):
    row = lax.broadcasted_iota(jnp.int32, (n, n), 0)
    col = lax.broadcasted_iota(jnp.int32, (n, n), 1)
    return jnp.where(keep(row, col), 1.0, 0.0).astype(MXU_DTYPE)


def _strictly_before(tq, tk, key_offset):
    row = lax.broadcasted_iota(jnp.int32, (tq, tk), 0)
    col = lax.broadcasted_iota(jnp.int32, (tq, tk), 1)
    return col + key_offset < row


def _sb_specs(S, n_g, n_h):
    base = 2 * n_g
    q_spec = pl.BlockSpec((S, LANE), lambda h: (0, base + h))
    k_spec = pl.BlockSpec((S, LANE), lambda h: (0, base + n_h + h))
    v_spec = pl.BlockSpec((S, LANE), lambda h: (0, base + 2 * n_h + h))
    gain_spec = pl.BlockSpec((1, LANE), lambda h: (0, n_g + h))
    head_spec = pl.BlockSpec((S, LANE), lambda h: (0, h))
    return q_spec, k_spec, v_spec, gain_spec, head_spec


def _sb_fwd(proj, out_gain, n_g, n_h, name):
    S = proj.shape[0]
    TQ, TK, NQ, KPQ = _sb_tiles(S)
    scale = LANE ** -0.5
    q_spec, k_spec, v_spec, gain_spec, head_spec = _sb_specs(S, n_g, n_h)

    def body(q_ref, k_ref, v_ref, og_ref, o_ref, on_ref, ls_ref, qb, kb, vb):
        qb[...] = q_ref[...].astype(MXU_DTYPE)
        kb[...] = k_ref[...].astype(MXU_DTYPE)
        vb[...] = v_ref[...].astype(MXU_DTYPE)
        after = _triangle(TK, lambda r, c: r > c)

        def block(qi, j, ctail, acc, key_offset):
            z = _dot(qi, kb[_rows(j, TK), :], NT) * scale
            lb, l1m = _log_sigmoids(z)
            if key_offset is not None:
                strict = _strictly_before(TQ, TK, key_offset)
                l1m = jnp.where(strict, l1m, 0.0)
            a = jnp.exp(lb + ctail + _tri_sum(l1m, after))
            if key_offset is not None:
                a = jnp.where(strict, a, 0.0)
            acc = acc + _dot(a.astype(MXU_DTYPE), vb[_rows(j, TK), :], NN)
            return ctail + jnp.sum(l1m, axis=1, keepdims=True), acc

        def q_loop(i, carry):
            qi = qb[_rows(i, TQ), :]
            state = (jnp.zeros((TQ, 1), F32), jnp.zeros((TQ, LANE), F32))
            for d in reversed(range(KPQ)):
                state = block(qi, i * KPQ + d, state[0], state[1], d * TK)
            ctail, acc = lax.fori_loop(
                0, i * KPQ, lambda jj, st: block(qi, i * KPQ - 1 - jj, st[0], st[1], None), state)
            ls_ref[_rows(i, TQ), :] = jnp.broadcast_to(ctail, (TQ, LANE))
            o_ref[_rows(i, TQ), :] = acc
            r = lax.rsqrt(_mean1(acc * acc) + EPS)
            on_ref[_rows(i, TQ), :] = (acc * r * og_ref[...]).astype(on_ref.dtype)
            return carry

        lax.fori_loop(0, NQ, q_loop, 0)

    return pl.pallas_call(
        body, name=name, grid=(n_h,),
        in_specs=[q_spec, k_spec, v_spec, gain_spec],
        out_specs=[head_spec, head_spec, head_spec],
        out_shape=[jax.ShapeDtypeStruct((S, n_h * LANE), F32), jax.ShapeDtypeStruct((S, n_h * LANE), MXU_DTYPE),
                   jax.ShapeDtypeStruct((S, n_h * LANE), F32)],
        scratch_shapes=[pltpu.VMEM((S, LANE), MXU_DTYPE)] * 3,
        compiler_params=_params("parallel"),
    )(proj, proj, proj, out_gain)


def _sb_bwd(proj, o_sb, l_sum, d_on, out_gain, n_g, n_h, name):
    S = proj.shape[0]
    TQ, TK, NQ, KPQ = _sb_tiles(S)
    scale = LANE ** -0.5
    q_spec, k_spec, v_spec, gain_spec, head_spec = _sb_specs(S, n_g, n_h)
    dn_spec = pl.BlockSpec((S, LANE), lambda h: (0, n_g + h))
    dgain_spec = pl.BlockSpec((1, LANE), lambda h: (0, h))

    def body(q_ref, k_ref, v_ref, o_ref, ls_ref, dn_ref, og_ref, dq_ref, dk_ref, dv_ref, dog_ref,
             qb, kb, vb, dob, dk_acc, dv_acc):
        qb[...] = q_ref[...].astype(MXU_DTYPE)
        kb[...] = k_ref[...].astype(MXU_DTYPE)
        vb[...] = v_ref[...].astype(MXU_DTYPE)
        o, dn = o_ref[...], dn_ref[...]
        r = lax.rsqrt(_mean1(o * o) + EPS)
        oh = o * r
        dog_ref[...] = _sum0(dn * oh)
        dhn = dn * og_ref[...]
        dob[...] = (r * (dhn - oh * _mean1(dhn * oh))).astype(MXU_DTYPE)
        dk_acc[...] = jnp.zeros_like(dk_acc)
        dv_acc[...] = jnp.zeros_like(dv_acc)

        up_to = _triangle(TK, lambda r, c: r <= c)
        before = _triangle(TK, lambda r, c: r < c)

        def block(qi, doi, ltot, j, cl, cdl, dq, key_offset):
            kj, vj = kb[_rows(j, TK), :], vb[_rows(j, TK), :]
            z = _dot(qi, kj, NT) * scale
            lb, l1m_all = _log_sigmoids(z)
            l1m = l1m_all
            if key_offset is not None:
                strict = _strictly_before(TQ, TK, key_offset)
                l1m = jnp.where(strict, l1m_all, 0.0)
            a = jnp.exp(lb + (ltot - (cl + _tri_sum(l1m, up_to))))
            if key_offset is not None:
                a = jnp.where(strict, a, 0.0)
            dl = _dot(doi, vj, NT) * a
            d_l1m = cdl + _tri_sum(dl, before, exact=False)
            dz = dl * jnp.exp(l1m_all) - jnp.exp(lb) * d_l1m
            if key_offset is not None:
                dz = jnp.where(strict, dz, 0.0)
            dzs = (dz * scale).astype(MXU_DTYPE)
            dq = dq + _dot(dzs, kj, NN)
            dk_acc[_rows(j, TK), :] += _dot(dzs, qi, TN)
            dv_acc[_rows(j, TK), :] += _dot(a.astype(MXU_DTYPE), doi, TN)
            return (cl + jnp.sum(l1m, axis=1, keepdims=True), cdl + jnp.sum(dl, axis=1, keepdims=True), dq)

        def q_loop(i, carry):
            qi, doi = qb[_rows(i, TQ), :], dob[_rows(i, TQ), :]
            ltot = ls_ref[_rows(i, TQ), :][:, :1]
            zero_col = jnp.zeros((TQ, 1), F32)
            state = lax.fori_loop(
                0, i * KPQ, lambda j, st: block(qi, doi, ltot, j, st[0], st[1], st[2], None),
                (zero_col, zero_col, jnp.zeros((TQ, LANE), F32)))
            for d in range(KPQ):
                state = block(qi, doi, ltot, i * KPQ + d, state[0], state[1], state[2], d * TK)
            dq_ref[_rows(i, TQ), :] = state[2].astype(dq_ref.dtype)
            return carry

        lax.fori_loop(0, NQ, q_loop, 0)
        dk_ref[...] = dk_acc[...].astype(dk_ref.dtype)
        dv_ref[...] = dv_acc[...].astype(dv_ref.dtype)

    W = n_h * LANE
    return pl.pallas_call(
        body, name=name, grid=(n_h,),
        in_specs=[q_spec, k_spec, v_spec, head_spec, head_spec, dn_spec, gain_spec],
        out_specs=[head_spec, head_spec, head_spec, dgain_spec],
        out_shape=[jax.ShapeDtypeStruct((S, W), MXU_DTYPE)] * 3 + [jax.ShapeDtypeStruct((1, W), F32)],
        scratch_shapes=[pltpu.VMEM((S, LANE), MXU_DTYPE)] * 4 + [pltpu.VMEM((S, LANE), F32)] * 2,
        compiler_params=_params("parallel"),
    )(proj, proj, proj, o_sb, l_sum, d_on, out_gain)


def _mod_part(c_all, w_ada, b_ada_cols, name):
    B, K = c_all.shape
    N = w_ada.shape[1]
    tn = _tile(N, 512)

    def body(c_ref, w_ref, b_ref, o_ref):
        cv = c_ref[...]
        ca = (cv * jax.nn.sigmoid(cv)).astype(MXU_DTYPE)
        o_ref[...] = _dot(ca, w_ref[...].astype(MXU_DTYPE), NN) + b_ref[...]

    return pl.pallas_call(
        body, name=name, grid=(N // tn,),
        in_specs=[pl.BlockSpec((B, K), lambda j: (0, 0)), pl.BlockSpec((K, tn), lambda j: (0, j)),
                  pl.BlockSpec((1, tn), lambda j: (0, j))],
        out_specs=pl.BlockSpec((B, tn), lambda j: (0, j)),
        out_shape=jax.ShapeDtypeStruct((B, N), F32), compiler_params=_params("parallel"))(c_all, w_ada, b_ada_cols)


def _adamw_math(w, g, m, v):
    m = ADAM_B1 * m + (1.0 - ADAM_B1) * g
    v = ADAM_B2 * v + (1.0 - ADAM_B2) * (g * g)
    m_hat = m / (1.0 - ADAM_B1 ** ADAM_STEP)
    v_hat = v / (1.0 - ADAM_B2 ** ADAM_STEP)
    delta = -ADAM_LR * (m_hat / (jnp.sqrt(v_hat) + ADAM_EPS) + ADAM_WD * w)
    return delta, m, v


def _adamw(w, g, m, v, name):
    R, C = w.shape
    tr = _tile(R, max(8, (1 << 19) // C), 8)
    spec = pl.BlockSpec((tr, C), lambda i: (i, 0))

    def body(w_ref, g_ref, m_ref, v_ref, go_ref, d_ref, mo_ref, vo_ref):
        g = g_ref[...]
        go_ref[...] = g
        d_ref[...], mo_ref[...], vo_ref[...] = _adamw_math(w_ref[...], g, m_ref[...], v_ref[...])

    return pl.pallas_call(body, name=name, grid=(R // tr,), in_specs=[spec] * 4, out_specs=[spec] * 4,
                          out_shape=[jax.ShapeDtypeStruct((R, C), F32)] * 4, compiler_params=_params("parallel"))(w, g, m, v)


def _adamw_ada(c_all, dmod_cols, w, m, v, name):
    K, N = w.shape
    B = c_all.shape[0]
    tk, tn = _tile(K, 512), _tile(N, 1024)
    spec = pl.BlockSpec((tk, tn), lambda i, j: (i, j))

    def body(c_ref, dm_ref, w_ref, m_ref, v_ref, g_ref, d_ref, mo_ref, vo_ref):
        cv = c_ref[...]
        ca = (cv * jax.nn.sigmoid(cv)).astype(MXU_DTYPE)
        g = _dot(ca, dm_ref[...].astype(MXU_DTYPE), TN)
        g_ref[...] = g
        d_ref[...], mo_ref[...], vo_ref[...] = _adamw_math(w_ref[...], g, m_ref[...], v_ref[...])

    return pl.pallas_call(
        body, name=name, grid=(K // tk, N // tn),
        in_specs=[pl.BlockSpec((B, tk), lambda i, j: (0, i)), pl.BlockSpec((B, tn), lambda i, j: (0, j)), spec, spec, spec],
        out_specs=[spec] * 4, out_shape=[jax.ShapeDtypeStruct((K, N), F32)] * 4,
        compiler_params=_params("parallel", "parallel"))(c_all, dmod_cols, w, m, v)


def _sum_devices(gathered, n_dev, name):
    R = gathered.shape[0] // n_dev
    C = gathered.shape[1]
    tr = _tile(R, 512, 8)
    n_blk = R // tr

    def body(*refs):
        acc = refs[0][...]
        for r in refs[1:n_dev]:
            acc = acc + r[...]
        refs[n_dev][...] = acc

    in_specs = [pl.BlockSpec((tr, C), functools.partial(lambda i, d: (d * n_blk + i, 0), d=d)) for d in range(n_dev)]
    return pl.pallas_call(body, name=name, grid=(n_blk,), in_specs=in_specs,
                          out_specs=pl.BlockSpec((tr, C), lambda i: (i, 0)),
                          out_shape=jax.ShapeDtypeStruct((R, C), F32), compiler_params=_params("parallel"))(*([gathered] * n_dev))


def _place():
    x, y, c = lax.axis_index("x"), lax.axis_index("y"), lax.axis_index("c")
    return x, y, c


def _allgather8(blk, name):
    m_per, n = blk.shape

    def body(x_ref, out_ref, send_sems, recv_sems, local_sem):
        x, y, c = _place()
        me, sibling = (x, y, c), (x, y, 1 - c)
        chips = [(1 - x, y), (x, 1 - y), (1 - x, 1 - y)]

        def rows(px, py, pc):
            return out_ref.at[pl.ds((4 * px + 2 * py + pc) * m_per, m_per), :]

        def copy(k, block, to, src=None):
            return pltpu.make_async_remote_copy(
                src_ref=rows(*block) if src is None else src, dst_ref=rows(*block),
                send_sem=send_sems.at[k], recv_sem=recv_sems.at[k], device_id=to, device_id_type=MESH)

        mine = pltpu.make_async_copy(x_ref, rows(*me), local_sem)
        mine.start()
        first = [copy(0, me, sibling, src=x_ref)]
        first += [copy(1 + j, me, (*chip, c), src=x_ref) for j, chip in enumerate(chips)]
        for cp in first:
            cp.start()
        passed = [copy(4 + j, (*chip, c), sibling) for j, chip in enumerate(chips)]
        for j, chip in enumerate(chips):
            copy(1 + j, (*chip, c), me).wait_recv()
            passed[j].start()
        copy(0, sibling, me).wait_recv()
        for j, chip in enumerate(chips):
            copy(4 + j, (*chip, 1 - c), me).wait_recv()
        for cp in first + passed:
            cp.wait_send()
        mine.wait()

    return pl.pallas_call(
        body, name=name,
        out_shape=jax.ShapeDtypeStruct((8 * m_per, n), blk.dtype),
        in_specs=[pl.BlockSpec(memory_space=pltpu.VMEM)],
        out_specs=pl.BlockSpec(memory_space=pltpu.VMEM),
        scratch_shapes=[pltpu.SemaphoreType.DMA((7,)), pltpu.SemaphoreType.DMA((7,)), pltpu.SemaphoreType.DMA],
        compiler_params=pltpu.CompilerParams(vmem_limit_bytes=V7X_VMEM_LIMIT),
    )(blk)


class _Sharded:
    def __init__(self, shard_shape, by_cols):
        r, c = shard_shape
        self.by_cols = by_cols
        self.full = (r, N_CHIPS * c) if by_cols else (N_CHIPS * r, c)
        self.shard = (r, c)
        self.half_rows = r // 2
        self.half = (r // 2, c)

    def shard_of(self, ref, k):
        r, c = self.shard
        return ref.at[:, pl.ds(k * c, c)] if self.by_cols else ref.at[pl.ds(k * r, r), :]

    def half_of(self, ref, k, hc):
        r, c = self.shard
        h = self.half_rows
        if self.by_cols:
            return ref.at[pl.ds(hc * h, h), pl.ds(k * c, c)]
        return ref.at[pl.ds(k * r + hc * h, h), :]

    def chunk_of(self, ref, k, hc, ch, n):
        r, c = self.shard
        h = self.half_rows
        q = h // n
        if self.by_cols:
            return ref.at[pl.ds(hc * h + ch * q, q), pl.ds(k * c, c)]
        return ref.at[pl.ds(k * r + hc * h + ch * q, q), :]

    def half_of_shard(self, ref, hc):
        return ref.at[pl.ds(hc * self.half_rows, self.half_rows), :]

    def part_of_halves(self, ref, k):
        r, c = self.shard
        h = self.half_rows
        return ref.at[:, pl.ds(k * c, c)] if self.by_cols else ref.at[pl.ds(k * h, h), :]


def _on_each_place(x, y, c, fn, by_chip=True, by_core=True):
    q = 2 * x + y
    for k in range(N_CHIPS if by_chip else 1):
        for cc in range(2 if by_core else 1):
            cond = None
            if by_chip:
                cond = q == k
            if by_core:
                cond = (c == cc) if cond is None else jnp.logical_and(cond, c == cc)
            pl.when(cond)(functools.partial(fn, k, cc))


def _chip_id(k, c):
    return (k // 2, k % 2, c)


def _handshake(peers):
    barrier = pltpu.get_barrier_semaphore()
    for peer in peers:
        pl.semaphore_signal(barrier, inc=1, device_id=peer, device_id_type=MESH)
    pl.semaphore_wait(barrier, len(peers))


def _on_sequencer(body, inputs, out_structs, n_copies, peers_of, name, collective_id, return_inputs=False):
    in_refs = [jax.new_ref(a, memory_space=pltpu.MemorySpace.HBM) for a in inputs]
    out_refs = [jax.empty_ref(s, memory_space=pltpu.MemorySpace.HBM) for s in out_structs]

    @pl.kernel(mesh=plsc.ScalarSubcoreMesh(axis_name="sequencer", num_cores=1), name=name,
               scratch_types=(pltpu.SemaphoreType.DMA((n_copies,)), pltpu.SemaphoreType.DMA((n_copies,))),
               compiler_params=pltpu.CompilerParams(collective_id=collective_id))
    def launch(send_sems, recv_sems):
        x, y, c = _place()
        _handshake(peers_of(x, y, c))
        body(in_refs, out_refs, send_sems, recv_sems, x, y, c)

    launch()
    return [r[...] for r in (in_refs if return_inputs else out_refs)]


def _sibling(x, y, c):
    return [(x, y, 1 - c)]


def _same_core_of_other_chips(x, y, c):
    return [(1 - x, y, c), (x, 1 - y, c), (1 - x, 1 - y, c)]


GATHER_CHUNKS = 4
GATHER_COPIES = 6 * GATHER_CHUNKS


def _allgather8_on_sequencer(blk, name, collective_id):
    m_per, n = blk.shape
    x, y, c = _place()
    placed = lax.dynamic_update_slice(jnp.zeros((8 * m_per, n), blk.dtype), blk, ((4 * x + 2 * y + c) * m_per, 0))

    def body(refs, _, send_sems, recv_sems, x, y, c):
        out_ref, = refs

        def at_place(k, cc):
            def rows(kk, pc):
                return out_ref.at[pl.ds((2 * kk + pc) * m_per, m_per), :]

            def copy(slot, block, to):
                return pltpu.make_async_remote_copy(src_ref=rows(*block), dst_ref=rows(*block), send_sem=send_sems.at[slot],
                                                    recv_sem=recv_sems.at[slot], device_id=to, device_id_type=MESH)

            others = [k ^ flip for flip in FLIPS]
            sends = [copy(0, (k, cc), _chip_id(k, 1 - cc))] + [copy(1 + j, (k, cc), _chip_id(kk, cc)) for j, kk in enumerate(others)]
            for cp in sends:
                cp.start()
            for j, kk in enumerate(others):
                copy(1 + j, (kk, cc), _chip_id(k, cc)).wait_recv()
                cp = copy(4 + j, (kk, cc), _chip_id(k, 1 - cc))
                cp.start()
                sends.append(cp)
            copy(0, (k, 1 - cc), _chip_id(k, cc)).wait_recv()
            for j, kk in enumerate(others):
                copy(4 + j, (kk, 1 - cc), _chip_id(k, cc)).wait_recv()
            for cp in sends:
                cp.wait_send()

        _on_each_place(x, y, c, at_place)

    def peers(x, y, c):
        return _sibling(x, y, c) + _same_core_of_other_chips(x, y, c)

    return _on_sequencer(body, [placed], [], 7, peers, name, collective_id, return_inputs=True)[0]


def _gather_weights(fulls, geoms, name, collective_id):
    n_w = len(fulls)
    n_ch, n_relay = GATHER_CHUNKS, GATHER_CHUNKS // 2
    f_refs = [jax.new_ref(f, memory_space=pltpu.MemorySpace.HBM) for f in fulls]
    FLIP_X, FLIP_Y, FLIP_BOTH = FLIPS
    TO_X, TO_Y, RELAY_TO_Y, RELAY_TO_X, ON_X, ON_Y, ON_DIAG = 0, n_ch, 2 * n_ch, 2 * n_ch + n_relay, 3 * n_ch, 4 * n_ch, 5 * n_ch

    @pl.kernel(mesh=plsc.ScalarSubcoreMesh(axis_name="sequencer", num_cores=1), name=name,
               scratch_types=(pltpu.SemaphoreType.DMA((GATHER_COPIES * n_w,)), pltpu.SemaphoreType.DMA((GATHER_COPIES * n_w,))),
               compiler_params=pltpu.CompilerParams(collective_id=collective_id))
    def launch(send_sems, recv_sems):
        x, y, c = _place()
        _handshake([(x, y, 1 - c), (1 - x, y, c), (x, 1 - y, c)])

        def at_place(k, cc):
            kx, ky, kd = k ^ FLIP_X, k ^ FLIP_Y, k ^ FLIP_BOTH
            me, sibling = _chip_id(k, cc), _chip_id(k, 1 - cc)
            started = []

            def copy(i, slot, src, dst, to, start=True):
                cp = pltpu.make_async_remote_copy(src_ref=src, dst_ref=dst, send_sem=send_sems.at[GATHER_COPIES * i + slot],
                                                  recv_sem=recv_sems.at[GATHER_COPIES * i + slot], device_id=to, device_id_type=MESH)
                if start:
                    cp.start()
                    started.append(cp)
                return cp

            def pass_on(i, slot, ref, to):
                copy(i, slot, ref, ref, to)

            def landed(i, slot, ref):
                copy(i, slot, ref, ref, me, start=False).wait_recv()

            y_order = [(n_relay + s) % n_ch for s in range(n_ch)]
            for i, (g, f_ref) in enumerate(zip(geoms, f_refs)):
                for s in range(n_ch):
                    pass_on(i, TO_X + s, g.chunk_of(f_ref, k, cc, s, n_ch), _chip_id(kx, cc))
                    pass_on(i, TO_Y + y_order[s], g.chunk_of(f_ref, k, cc, y_order[s], n_ch), _chip_id(ky, cc))
            for i, (g, f_ref) in enumerate(zip(geoms, f_refs)):
                for s in range(n_ch):
                    from_x = g.chunk_of(f_ref, kx, cc, s, n_ch)
                    landed(i, TO_X + s, from_x)
                    if s < n_relay:
                        pass_on(i, RELAY_TO_Y + s, from_x, _chip_id(ky, cc))
                    pass_on(i, ON_X + s, from_x, sibling)
                    ch = y_order[s]
                    from_y = g.chunk_of(f_ref, ky, cc, ch, n_ch)
                    landed(i, TO_Y + ch, from_y)
                    if ch >= n_relay:
                        pass_on(i, RELAY_TO_X + ch - n_relay, from_y, _chip_id(kx, cc))
                    pass_on(i, ON_Y + ch, from_y, sibling)
                for r in range(n_relay):
                    via_y = g.chunk_of(f_ref, kd, cc, r, n_ch)
                    landed(i, RELAY_TO_Y + r, via_y)
                    pass_on(i, ON_DIAG + r, via_y, sibling)
                    via_x = g.chunk_of(f_ref, kd, cc, n_relay + r, n_ch)
                    landed(i, RELAY_TO_X + r, via_x)
                    pass_on(i, ON_DIAG + n_relay + r, via_x, sibling)
            for i, (g, f_ref) in enumerate(zip(geoms, f_refs)):
                for slot, kk in ((ON_X, kx), (ON_Y, ky), (ON_DIAG, kd)):
                    for ch in range(n_ch):
                        landed(i, slot + ch, g.chunk_of(f_ref, kk, 1 - cc, ch, n_ch))
            for cp in started:
                cp.wait_send()

        _on_each_place(x, y, c, at_place)

    launch()
    return [f_ref[...] for f_ref in f_refs]


def _swap_core_halves(grads, geoms, name, collective_id):
    n_cp = sum(1 if g.by_cols else N_CHIPS for g in geoms)

    def body(g_refs, t_refs, send_sems, recv_sems, x, y, c):

        def at_place(_, cc):
            def pairs(hc):
                out = []
                for g, g_ref, t_ref in zip(geoms, g_refs, t_refs):
                    if g.by_cols:
                        out.append((g_ref.at[pl.ds(hc * g.half_rows, g.half_rows), :], t_ref))
                    else:
                        out += [(g.half_of(g_ref, k, hc), g.part_of_halves(t_ref, k)) for k in range(N_CHIPS)]
                return out

            sends = [pltpu.make_async_remote_copy(src_ref=src, dst_ref=dst, send_sem=send_sems.at[n],
                                                  recv_sem=recv_sems.at[n], device_id=(x, y, 1 - cc), device_id_type=MESH)
                     for n, (src, dst) in enumerate(pairs(1 - cc))]
            for cp in sends:
                cp.start()
            for n, (src, dst) in enumerate(pairs(cc)):
                pltpu.make_async_remote_copy(src_ref=src, dst_ref=dst, send_sem=send_sems.at[n], recv_sem=recv_sems.at[n],
                                             device_id=(x, y, cc), device_id_type=MESH).wait_recv()
            for cp in sends:
                cp.wait_send()

        _on_each_place(x, y, c, at_place, by_chip=False)

    return _on_sequencer(body, grads, [jax.ShapeDtypeStruct((g.full[0] // 2, g.full[1]), F32) for g in geoms],
                         n_cp, _sibling, name, collective_id)


def _scatter_chip_sums(sums, geoms, name, collective_id):
    def body(s_refs, r_refs, send_sems, recv_sems, x, y, c):

        def at_place(k, _):
            sends = []
            for i, (g, s_ref, r_ref) in enumerate(zip(geoms, s_refs, r_refs)):
                for j, flip in enumerate(FLIPS):
                    kk = k ^ flip
                    cp = pltpu.make_async_remote_copy(
                        src_ref=g.part_of_halves(s_ref, kk), dst_ref=r_ref.at[j], send_sem=send_sems.at[3 * i + j],
                        recv_sem=recv_sems.at[3 * i + j], device_id=(kk // 2, kk % 2, c), device_id_type=MESH)
                    cp.start()
                    sends.append(cp)
            for i, (g, s_ref, r_ref) in enumerate(zip(geoms, s_refs, r_refs)):
                for j in range(len(FLIPS)):
                    pltpu.make_async_remote_copy(
                        src_ref=g.part_of_halves(s_ref, k), dst_ref=r_ref.at[j], send_sem=send_sems.at[3 * i + j],
                        recv_sem=recv_sems.at[3 * i + j], device_id=(x, y, c), device_id_type=MESH).wait_recv()
            for cp in sends:
                cp.wait_send()

        _on_each_place(x, y, c, at_place, by_core=False)

    return _on_sequencer(body, sums, [jax.ShapeDtypeStruct((len(FLIPS),) + g.half, WIRE_DTYPE) for g in geoms],
                         len(FLIPS) * len(sums), _same_core_of_other_chips, name, collective_id)


def _share_reduced_halves(reduced, geoms, name, collective_id):
    def body(out_refs, _, send_sems, recv_sems, x, y, c):

        def at_place(_, cc):
            sends = []
            for i, (g, ref) in enumerate(zip(geoms, out_refs)):
                mine = g.half_of_shard(ref, cc)
                cp = pltpu.make_async_remote_copy(src_ref=mine, dst_ref=mine, send_sem=send_sems.at[i],
                                                  recv_sem=recv_sems.at[i], device_id=(x, y, 1 - cc), device_id_type=MESH)
                cp.start()
                sends.append(cp)
            for i, (g, ref) in enumerate(zip(geoms, out_refs)):
                theirs = g.half_of_shard(ref, 1 - cc)
                pltpu.make_async_remote_copy(src_ref=theirs, dst_ref=theirs, send_sem=send_sems.at[i],
                                             recv_sem=recv_sems.at[i], device_id=(x, y, cc), device_id_type=MESH).wait_recv()
            for cp in sends:
                cp.wait_send()

        _on_each_place(x, y, c, at_place, by_chip=False)

    return _on_sequencer(body, reduced, [], len(reduced), _sibling, name, collective_id, return_inputs=True)


def _chip_sum(place, grad, theirs, g, name):
    RH, C = theirs.shape
    h = g.half_rows
    tr = _tile(h, 256, 16)
    tc = _tile(C, 2048)
    per_half = h // tr

    if g.by_cols:
        grad_map = lambda i, j, p: (p[1] * per_half + i, j)
    else:
        grad_map = lambda i, j, p: ((i // per_half) * 2 * per_half + p[1] * per_half + i % per_half, j)

    def body(p_ref, a_ref, b_ref, o_ref):
        o_ref[...] = (a_ref[...] + b_ref[...]).astype(o_ref.dtype)

    return pl.pallas_call(
        body, name=name,
        grid_spec=pltpu.PrefetchScalarGridSpec(
            num_scalar_prefetch=1, grid=(RH // tr, C // tc),
            in_specs=[pl.BlockSpec((tr, tc), grad_map), pl.BlockSpec((tr, tc), lambda i, j, p: (i, j))],
            out_specs=pl.BlockSpec((tr, tc), lambda i, j, p: (i, j))),
        out_shape=jax.ShapeDtypeStruct((RH, C), WIRE_DTYPE),
        compiler_params=_params("parallel", "parallel"),
    )(place, grad, theirs)


def _reduce_half(place, grad, theirs, others, g, name):
    h, wc = g.half
    tr = _tile(h, 256, 16)
    per_half = h // tr
    if g.by_cols:
        tc = wc
        grad_map = lambda i, p: (p[1] * per_half + i, p[0])
        theirs_map = lambda i, p: (i, p[0])
    else:
        tc = wc
        grad_map = lambda i, p: (p[0] * 2 * per_half + p[1] * per_half + i, 0)
        theirs_map = lambda i, p: (p[0] * per_half + i, 0)

    def body(p_ref, a_ref, b_ref, o0_ref, o1_ref, o2_ref, out_ref):
        acc = a_ref[...] + b_ref[...]
        for o_ref in (o0_ref, o1_ref, o2_ref):
            acc = acc + o_ref[...].astype(F32)
        out_ref[...] = acc

    other_specs = [pl.BlockSpec((None, tr, tc), functools.partial(lambda i, p, j: (j, i, 0), j=j)) for j in range(len(FLIPS))]
    return pl.pallas_call(
        body, name=name,
        grid_spec=pltpu.PrefetchScalarGridSpec(
            num_scalar_prefetch=1, grid=(per_half,),
            in_specs=[pl.BlockSpec((tr, tc), grad_map), pl.BlockSpec((tr, tc), theirs_map)] + other_specs,
            out_specs=pl.BlockSpec((tr, tc), lambda i, p: (p[1] * per_half + i, 0))),
        out_shape=jax.ShapeDtypeStruct(g.shard, F32),
        compiler_params=_params("arbitrary"),
    )(place, grad, theirs, others, others, others)


SMALL = ("b_ada", "norm1_g", "v_norm_g", "w_spatial", "b_spatial", "out_norm_g", "norm2_g", "final_g")
BIG = ("w_in", "w_out", "w_gate", "w_up", "w_down")
BY_COLS = {"w_in": True, "w_out": False, "w_gate": True, "w_up": True, "w_down": False}
ORDER = ("w_ada", "b_ada", "norm1_g", "w_in", "v_norm_g", "w_spatial", "b_spatial", "out_norm_g", "w_out",
         "norm2_g", "w_gate", "w_up", "w_down", "final_g")


def _pack(parts):
    return jnp.concatenate([parts[n].reshape(-1) for n in SMALL]).reshape(-1, LANE)


def _unpack(slab, shapes):
    flat = slab.reshape(-1)
    out, at = {}, 0
    for n in SMALL:
        size = math.prod(shapes[n])
        out[n] = flat[at:at + size].reshape(shapes[n])
        at += size
    return out


def kernel(x, c, w_ada, b_ada, norm1_g, w_in, v_norm_g, w_spatial, b_spatial, out_norm_g, w_out, norm2_g, w_gate, w_up, w_down, final_g, loss_target, m_w_ada, m_b_ada, m_norm1_g, m_w_in, m_v_norm_g, m_w_spatial, m_b_spatial, m_out_norm_g, m_w_out, m_norm2_g, m_w_gate, m_w_up, m_w_down, m_final_g, v_w_ada, v_b_ada, v_norm1_g, v_w_in, v_v_norm_g, v_w_spatial, v_b_spatial, v_out_norm_g, v_w_out, v_norm2_g, v_w_gate, v_w_up, v_w_down, v_final_g):
    weights = dict(w_ada=w_ada, b_ada=b_ada, norm1_g=norm1_g, w_in=w_in, v_norm_g=v_norm_g, w_spatial=w_spatial,
                   b_spatial=b_spatial, out_norm_g=out_norm_g, w_out=w_out, norm2_g=norm2_g, w_gate=w_gate, w_up=w_up,
                   w_down=w_down, final_g=final_g)
    m_in = dict(w_ada=m_w_ada, b_ada=m_b_ada, norm1_g=m_norm1_g, w_in=m_w_in, v_norm_g=m_v_norm_g, w_spatial=m_w_spatial,
                b_spatial=m_b_spatial, out_norm_g=m_out_norm_g, w_out=m_w_out, norm2_g=m_norm2_g, w_gate=m_w_gate,
                w_up=m_w_up, w_down=m_w_down, final_g=m_final_g)
    v_in = dict(w_ada=v_w_ada, b_ada=v_b_ada, norm1_g=v_norm1_g, w_in=v_w_in, v_norm_g=v_v_norm_g, w_spatial=v_w_spatial,
                b_spatial=v_b_spatial, out_norm_g=v_out_norm_g, w_out=v_w_out, norm2_g=v_norm2_g, w_gate=v_w_gate,
                w_up=v_w_up, w_down=v_w_down, final_g=v_final_g)

    S, D = x.shape[1], x.shape[2]
    n_g = v_norm_g.shape[-1] // LANE
    n_h = (D - n_g * LANE) // LANE
    GW = n_g * LANE
    xi, yi, ci = _place()
    chip = 2 * xi + yi
    me = 4 * xi + 2 * yi + ci
    place = jnp.stack([chip, ci]).astype(jnp.int32)

    xs, target = x[0], loss_target[0]
    geoms = [_Sharded(weights[n].shape[1:], BY_COLS[n]) for n in BIG]

    full = {}
    for i, group in enumerate((("w_in",), ("w_out",), ("w_gate", "w_up"), ("w_down",))):
        gg = [geoms[BIG.index(n)] for n in group]
        own = [_cast_into_full(place, weights[n][0], g, "cast_" + n) for n, g in zip(group, gg)]
        gathered = _gather_weights(own, gg, "gather_" + "_".join(group), 1 + i)
        full.update(zip(group, gathered))

    c_pad = jnp.concatenate([c, jnp.zeros((7, D), F32)], axis=0)
    c_all = _allgather8(c_pad, "gather_c")[::8]
    n_ada = w_ada.shape[2]
    b_cols = lax.dynamic_slice(b_ada, (0, chip * n_ada), (1, n_ada))
    mod_parts = _allgather8(_mod_part(c_all, w_ada[0], b_cols, "mod_part"), "gather_mod")
    mod_all = mod_parts.reshape(N_CHIPS, 2, 8, n_ada)[:, 0].transpose(1, 0, 2).reshape(8, N_CHIPS * n_ada)
    mod = lax.dynamic_slice(mod_all, (me, 0), (1, 6 * D))
    shift1, scale1, gate1, shift2, scale2, gate2 = [mod[:, i * D:(i + 1) * D] for i in range(6)]

    b_t = b_spatial[0].T
    h1 = _norm_mod(xs, norm1_g, scale1, shift1, "norm1")
    proj, = _mm("nn", h1, full["w_in"], [F32], "proj")
    on_gm = _gmlp_fwd(proj, v_norm_g, w_spatial[0], b_t, out_norm_g, n_g, "gmlp_fwd")
    o_sb, on_sb, l_sum = _sb_fwd(proj, out_norm_g, n_g, n_h, "sb_fwd")
    o_n = jnp.concatenate([on_gm, on_sb], axis=1)
    attn, = _mm("nn", o_n, full["w_out"], [F32], "attn_out")
    x1, h2 = _residual_norm_mod(xs, attn, gate1, norm2_g, scale2, shift2, "norm2")
    a_g, a_u, f_in = _gate_up(h2, full["w_gate"], full["w_up"], "gate_up")
    f, = _mm("nn", f_in, full["w_down"], [F32], "down", tm=1024)
    dx2, df, d_gate2, d_final_g, loss_part = _final_loss_bwd(x1, f, gate2, final_g.reshape(1, D), target, "final")
    loss = lax.psum(loss_part[0, 0], ("x", "y", "c"))

    geom_of = dict(zip(BIG, geoms))
    grad_out, delta, new_m, new_v = {}, {}, {}, {}

    def swap(group, grads, collective_id):
        return _swap_core_halves(grads, [geom_of[n] for n in group], "swap_" + "_".join(group), collective_id)

    def chip_sums(group, grads, theirs, after):
        return [_chip_sum(place, gr, _then(after, t), geom_of[n], "chip_sum_" + n) for n, gr, t in zip(group, grads, theirs)]

    def scatter(group, sums, collective_id):
        return _scatter_chip_sums(sums, [geom_of[n] for n in group], "scatter_" + "_".join(group), collective_id)

    def reduce_halves(group, grads, theirs, others, after):
        return [_reduce_half(place, gr, t, _then(after, o), geom_of[n], "reduce_" + n)
                for n, gr, t, o in zip(group, grads, theirs, others)]

    def share(group, halves, collective_id):
        return _share_reduced_halves(halves, [geom_of[n] for n in group], "share_" + "_".join(group), collective_id)

    def adamw(group, reduced, after):
        for n, r in zip(group, reduced):
            go, d, mo, vo = _adamw(weights[n][0], _then(after, r), m_in[n][0], v_in[n][0], "adamw_" + n)
            grad_out[n], delta[n], new_m[n], new_v[n] = go[None], d[None], mo[None], vo[None]
        return d

    g_down = ("w_down",)
    g_ffn = ("w_gate", "w_up")
    g_out = ("w_out",)
    g_in = ("w_in",)

    gr_down = _mm("tn", f_in, df, [F32], "d_w_down", tm=1408, tn=1024)
    th_down = swap(g_down, gr_down, 6)
    d_ag, d_au = _mm("nt", df, full["w_down"], [MXU_DTYPE, MXU_DTYPE], "d_ffn_in", extras=(a_g, a_u),
                     epilogue=_swiglu_bwd_epilogue)
    sm_down = chip_sums(g_down, gr_down, th_down, after=d_ag)
    ot_down = scatter(g_down, sm_down, 7)
    gr_ffn = [_mm("tn", h2, _then(sm_down, d_ag), [F32], "d_w_gate")[0], _mm("tn", h2, d_au, [F32], "d_w_up")[0]]
    th_ffn = swap(g_ffn, gr_ffn, 9)
    dh2 = _mm_ktiled("nt", [(_then(gr_ffn, d_ag), full["w_gate"]), (d_au, full["w_up"])], "d_h2", tn=512)
    sm_ffn = chip_sums(g_ffn, gr_ffn, th_ffn, after=dh2)
    ot_ffn = scatter(g_ffn, sm_ffn, 10)
    hv_down = reduce_halves(g_down, gr_down, th_down, ot_down, after=sm_ffn)
    rd_down = share(g_down, hv_down, 8)
    dx1, d_shift2, d_scale2, d_norm2_g, d_gate1, d_attn = _norm_mod_bwd(
        _then(hv_down, dh2), x1, dx2, norm2_g, scale2, "norm2_bwd", branch=attn, gate=gate1)
    gr_out = _mm("tn", o_n, d_attn, [F32], "d_w_out")
    th_out = swap(g_out, gr_out, 12)
    d_on, = _mm("nt", _then(gr_out, d_attn), full["w_out"], [F32], "d_o")
    dp_gm, d_w_spatial, d_b_t, d_v_norm_g, d_og_gm = _gmlp_bwd(proj, d_on, v_norm_g, w_spatial[0], b_t, out_norm_g, n_g, "gmlp_bwd")
    dq, dk, dv, d_og_sb = _sb_bwd(proj, o_sb, l_sum, _then(dp_gm, d_on), out_norm_g, n_g, n_h, "sb_bwd")
    sm_out = chip_sums(g_out, gr_out, th_out, after=dq)
    ot_out = scatter(g_out, sm_out, 13)
    hv_ffn = reduce_halves(g_ffn, gr_ffn, th_ffn, ot_ffn, after=sm_out)
    rd_ffn = share(g_ffn, hv_ffn, 11)
    dproj = jnp.concatenate([_then(hv_ffn, dp_gm), dq, dk, dv], axis=1)
    gr_in = _mm("tn", h1, dproj, [F32], "d_w_in")
    th_in = swap(g_in, gr_in, 15)
    hv_out = reduce_halves(g_out, gr_out, th_out, ot_out, after=gr_in)
    rd_out = share(g_out, hv_out, 14)
    dh1, = _mm("nt", _then(gr_in, dproj), full["w_in"], [F32], "d_h1", tm=1024)
    grad_x, d_shift1, d_scale1, d_norm1_g = _norm_mod_bwd(dh1, xs, dx1, norm1_g, scale1, "norm1_bwd")
    sm_in = chip_sums(g_in, gr_in, th_in, after=grad_x)
    ot_in = scatter(g_in, sm_in, 16)

    dmod = jnp.concatenate([d_shift1, d_scale1, d_gate1, d_shift2, d_scale2, d_gate2], axis=1)
    small_parts = dict(b_ada=dmod, norm1_g=d_norm1_g, v_norm_g=d_v_norm_g, w_spatial=d_w_spatial, b_spatial=d_b_t.T,
                       out_norm_g=jnp.concatenate([d_og_gm, d_og_sb], axis=1), norm2_g=d_norm2_g, final_g=d_final_g)
    slab = _then(sm_in, _pack(small_parts))
    rows = slab.shape[0]
    gathered = _allgather8_on_sequencer(slab, "gather_small", 18)
    done = adamw(g_down, rd_down, after=slab)
    done = adamw(g_ffn, rd_ffn, after=done)
    done = adamw(g_out, rd_out, after=done)
    gathered = _then(done, gathered)
    small_shapes = {n: weights[n].shape for n in SMALL}
    small_sum = _sum_devices(gathered, 8, "sum_small")
    dmod_all = gathered.reshape(8, rows * LANE)[:, :6 * D]
    dmod_cols = lax.dynamic_slice(dmod_all, (0, chip * n_ada), (8, n_ada))
    g_ada, d, mo, vo = _adamw_ada(c_all, dmod_cols, w_ada[0], m_w_ada[0], v_w_ada[0], "adamw_w_ada")
    grad_out["w_ada"], delta["w_ada"], new_m["w_ada"], new_v["w_ada"] = g_ada[None], d[None], mo[None], vo[None]
    gs_small, d_small, mo, vo = _adamw(_pack({n: weights[n] for n in SMALL}), small_sum, _pack({n: m_in[n] for n in SMALL}),
                                       _pack({n: v_in[n] for n in SMALL}), "adamw_small")
    for dst, slab_out in ((grad_out, gs_small), (delta, d_small), (new_m, mo), (new_v, vo)):
        dst.update(_unpack(slab_out, small_shapes))
    hv_in = reduce_halves(g_in, gr_in, th_in, ot_in, after=d)
    adamw(g_in, share(g_in, hv_in, 17), after=d)

    return (loss, grad_x[None], *[grad_out[n] for n in ORDER], *[delta[n] for n in ORDER],
            *[new_m[n] for n in ORDER], *[new_v[n] for n in ORDER])
```

```python
import functools
import math

import jax
import jax.numpy as jnp
from jax import lax
from jax.experimental import pallas as pl
from jax.experimental.pallas import tpu as pltpu
from jax.experimental.pallas import tpu_sc as plsc

F32 = jnp.float32
MXU_DTYPE = jnp.bfloat16
WIRE_DTYPE = jnp.bfloat16
EPS = 1e-6
LANE = 128
V7X_VMEM_LIMIT = 56 * 1024 * 1024
MESH = pl.DeviceIdType.MESH
N_CHIPS = 4
FLIPS = (2, 1, 3)

ADAM_LR = 0.001
ADAM_B1 = 0.9
ADAM_B2 = 0.999
ADAM_EPS = 1e-08
ADAM_WD = 0.01
ADAM_STEP = 10


def _params(*semantics):
    return pltpu.CompilerParams(dimension_semantics=semantics or None, vmem_limit_bytes=V7X_VMEM_LIMIT)


def _tile(dim, pref, unit=LANE):
    best = None
    t = unit
    while t <= min(dim, pref):
        if dim % t == 0:
            best = t
        t += unit
    return best if best is not None else dim


def _then(first, second):
    return lax.optimization_barrier((first, second))[1]


def _sum0(v):
    return jnp.sum(v, axis=0, keepdims=True)


def _mean1(v):
    return jnp.mean(v, axis=-1, keepdims=True)


def _gelu(x):
    return 0.5 * x * (1.0 + lax.erf(x * (1.0 / math.sqrt(2.0))))


def _gelu_grad(x):
    cdf = 0.5 * (1.0 + lax.erf(x * (1.0 / math.sqrt(2.0))))
    return cdf + x * jnp.exp(-0.5 * x * x) * (1.0 / math.sqrt(2.0 * math.pi))


def _dot(a, b, dims):
    return lax.dot_general(a, b, (dims, ((), ())), preferred_element_type=F32)


NN = ((1,), (0,))
NT = ((1,), (1,))
TN = ((0,), (0,))


def _mm(kind, a, b, out_dtypes, name, tm=2048, tn=512, extras=(), epilogue=None):
    if kind == "nn":
        (M, K), N = a.shape, b.shape[1]
    elif kind == "nt":
        (M, K), N = a.shape, b.shape[0]
    else:
        (K, M), N = a.shape, b.shape[1]
    tm, tn = _tile(M, tm), _tile(N, tn)
    a_spec = pl.BlockSpec((K, tm), lambda i, j: (0, i)) if kind == "tn" else pl.BlockSpec((tm, K), lambda i, j: (i, 0))
    b_spec = pl.BlockSpec((tn, K), lambda i, j: (j, 0)) if kind == "nt" else pl.BlockSpec((K, tn), lambda i, j: (0, j))
    mn_spec = pl.BlockSpec((tm, tn), lambda i, j: (i, j))
    dims = {"nn": NN, "nt": NT, "tn": TN}[kind]
    n_extra = len(extras)

    n_chunks = 1 if epilogue is None or kind == "tn" else max(1, tm // 512)
    rows_per = tm // n_chunks

    def body(a_ref, b_ref, *rest):
        for r in range(n_chunks):
            rows = slice(r * rows_per, (r + 1) * rows_per)
            acc = _dot(a_ref[...] if n_chunks == 1 else a_ref[rows, :], b_ref[...], dims)
            res = (acc,) if epilogue is None else epilogue(acc, *[e[rows, :] for e in rest[:n_extra]])
            for o_ref, val in zip(rest[n_extra:], res):
                o_ref[rows, :] = val.astype(o_ref.dtype)

    outs = pl.pallas_call(
        body, name=name, grid=(M // tm, N // tn),
        in_specs=[a_spec, b_spec] + [mn_spec] * n_extra,
        out_specs=[mn_spec] * len(out_dtypes),
        out_shape=[jax.ShapeDtypeStruct((M, N), d) for d in out_dtypes],
        compiler_params=_params("parallel", "arbitrary"),
    )(a, b, *extras)
    return outs


def _mm_ktiled(kind, pairs, name, tm=2048, tn=1024, tk=1408):
    a0, b0 = pairs[0]
    M, K = a0.shape
    N = b0.shape[1] if kind == "nn" else b0.shape[0]
    tm, tn, tk = _tile(M, tm), _tile(N, tn), _tile(K, tk)
    a_spec = pl.BlockSpec((tm, tk), lambda i, j, k: (i, k))
    b_spec = pl.BlockSpec((tk, tn), lambda i, j, k: (k, j)) if kind == "nn" else pl.BlockSpec((tn, tk), lambda i, j, k: (j, k))
    dims = NN if kind == "nn" else NT
    n_pairs = len(pairs)

    def body(*refs):
        o_ref = refs[2 * n_pairs]
        acc = _dot(refs[0][...], refs[1][...], dims)
        for p in range(1, n_pairs):
            acc = acc + _dot(refs[2 * p][...], refs[2 * p + 1][...], dims)

        @pl.when(pl.program_id(2) == 0)
        def _():
            o_ref[...] = acc

        @pl.when(pl.program_id(2) != 0)
        def _():
            o_ref[...] += acc

    return pl.pallas_call(
        body, name=name, grid=(M // tm, N // tn, K // tk),
        in_specs=[a_spec, b_spec] * n_pairs,
        out_specs=pl.BlockSpec((tm, tn), lambda i, j, k: (i, j)),
        out_shape=jax.ShapeDtypeStruct((M, N), F32),
        compiler_params=_params("parallel", "parallel", "arbitrary"),
    )(*[x for pair in pairs for x in pair])


def _gate_up(h, wg, wu, name):
    (M, K), N = h.shape, wg.shape[1]
    tm, tn = _tile(M, 2048), _tile(N, 512)

    n_chunks = max(1, tm // 512)
    rows_per = tm // n_chunks

    def body(h_ref, wg_ref, wu_ref, ag_ref, au_ref, f_ref):
        for r in range(n_chunks):
            rows = slice(r * rows_per, (r + 1) * rows_per)
            hv = h_ref[rows, :]
            ag = _dot(hv, wg_ref[...], NN)
            au = _dot(hv, wu_ref[...], NN)
            ag_ref[rows, :] = ag.astype(ag_ref.dtype)
            au_ref[rows, :] = au.astype(au_ref.dtype)
            f_ref[rows, :] = (ag * jax.nn.sigmoid(ag) * au).astype(f_ref.dtype)

    w_spec = pl.BlockSpec((K, tn), lambda i, j: (0, j))
    mn_spec = pl.BlockSpec((tm, tn), lambda i, j: (i, j))
    return pl.pallas_call(
        body, name=name, grid=(M // tm, N // tn),
        in_specs=[pl.BlockSpec((tm, K), lambda i, j: (i, 0)), w_spec, w_spec],
        out_specs=[mn_spec] * 3,
        out_shape=[jax.ShapeDtypeStruct((M, N), MXU_DTYPE)] * 3,
        compiler_params=_params("parallel", "arbitrary"),
    )(h, wg, wu)


def _swiglu_bwd_epilogue(dfin, ag, au):
    ag, au = ag.astype(F32), au.astype(F32)
    sg = jax.nn.sigmoid(ag)
    d_au = dfin * (ag * sg)
    d_ag = dfin * au * (sg * (1.0 + ag * (1.0 - sg)))
    return d_ag, d_au


def _row_specs(ts, width):
    return pl.BlockSpec((ts, width), lambda i: (i, 0)), pl.BlockSpec((1, width), lambda i: (0, 0))


def _cast_into_full(place, shard, g, name):
    R, C = shard.shape
    tr = _tile(R, 256, 16)
    n_blk = R // tr
    out_map = (lambda i, p: (i, p[0])) if g.by_cols else (lambda i, p: (p[0] * n_blk + i, 0))

    def body(p_ref, a_ref, o_ref):
        o_ref[...] = a_ref[...].astype(o_ref.dtype)

    return pl.pallas_call(
        body, name=name,
        grid_spec=pltpu.PrefetchScalarGridSpec(
            num_scalar_prefetch=1, grid=(n_blk,),
            in_specs=[pl.BlockSpec((tr, C), lambda i, p: (i, 0))],
            out_specs=pl.BlockSpec((tr, C), out_map)),
        out_shape=jax.ShapeDtypeStruct(g.full, WIRE_DTYPE),
        compiler_params=_params("arbitrary"),
    )(place, shard)


def _norm_mod(x, g, scale, shift, name):
    S, D = x.shape
    ts = _tile(S, 256, 16)
    tile, vec = _row_specs(ts, D)

    def body(x_ref, g_ref, sc_ref, sh_ref, h_ref):
        xv = x_ref[...]
        r = lax.rsqrt(_mean1(xv * xv) + EPS)
        h_ref[...] = ((xv * r) * g_ref[...] * (1.0 + sc_ref[...]) + sh_ref[...]).astype(h_ref.dtype)

    return pl.pallas_call(body, name=name, grid=(S // ts,), in_specs=[tile, vec, vec, vec], out_specs=tile,
                          out_shape=jax.ShapeDtypeStruct((S, D), MXU_DTYPE), compiler_params=_params("parallel"))(x, g, scale, shift)


def _residual_norm_mod(x, attn, gate, g, scale, shift, name):
    S, D = x.shape
    ts = _tile(S, 256, 16)
    tile, vec = _row_specs(ts, D)

    def body(x_ref, a_ref, gate_ref, g_ref, sc_ref, sh_ref, x1_ref, h_ref):
        x1 = x_ref[...] + gate_ref[...] * a_ref[...]
        x1_ref[...] = x1
        r = lax.rsqrt(_mean1(x1 * x1) + EPS)
        h_ref[...] = ((x1 * r) * g_ref[...] * (1.0 + sc_ref[...]) + sh_ref[...]).astype(h_ref.dtype)

    return pl.pallas_call(body, name=name, grid=(S // ts,), in_specs=[tile, tile, vec, vec, vec, vec],
                          out_specs=[tile, tile],
                          out_shape=[jax.ShapeDtypeStruct((S, D), F32), jax.ShapeDtypeStruct((S, D), MXU_DTYPE)],
                          compiler_params=_params("parallel"))(x, attn, gate, g, scale, shift)


def _final_loss_bwd(x1, f, gate2, final_g, target, name):
    S, D = x1.shape
    ts = _tile(S, 256, 16)
    tile, vec = _row_specs(ts, D)
    loss_spec = pl.BlockSpec((1, LANE), lambda i: (0, 0))

    def body(x1_ref, f_ref, gate_ref, g_ref, t_ref, dx2_ref, df_ref, dgate_ref, dg_ref, loss_ref):
        @pl.when(pl.program_id(0) == 0)
        def _():
            dgate_ref[...] = jnp.zeros_like(dgate_ref)
            dg_ref[...] = jnp.zeros_like(dg_ref)
            loss_ref[...] = jnp.zeros_like(loss_ref)

        fv, gate, g = f_ref[...], gate_ref[...], g_ref[...]
        x2 = x1_ref[...] + gate * fv
        r = lax.rsqrt(_mean1(x2 * x2) + EPS)
        xn = x2 * r
        err = xn * g - t_ref[...]
        loss_ref[...] += jnp.broadcast_to(0.5 * _sum0(_mean1(err * err)), loss_ref.shape)
        dy = err * (1.0 / D)
        dg_ref[...] += _sum0(dy * xn)
        dxn = dy * g
        dx2 = r * (dxn - xn * _mean1(dxn * xn))
        dx2_ref[...] = dx2
        dgate_ref[...] += _sum0(dx2 * fv)
        df_ref[...] = (dx2 * gate).astype(df_ref.dtype)

    return pl.pallas_call(
        body, name=name, grid=(S // ts,), in_specs=[tile, tile, vec, vec, tile],
        out_specs=[tile, tile, vec, vec, loss_spec],
        out_shape=[jax.ShapeDtypeStruct((S, D), F32), jax.ShapeDtypeStruct((S, D), MXU_DTYPE),
                   jax.ShapeDtypeStruct((1, D), F32), jax.ShapeDtypeStruct((1, D), F32),
                   jax.ShapeDtypeStruct((1, LANE), F32)],
        compiler_params=_params("arbitrary"),
    )(x1, f, gate2, final_g, target)


def _norm_mod_bwd(dh, xin, dres, g, scale, name, branch=None, gate=None):
    S, D = xin.shape
    ts = _tile(S, 256, 16)
    tile, vec = _row_specs(ts, D)
    with_gate = branch is not None

    def body(*refs):
        if with_gate:
            dh_ref, x_ref, dres_ref, g_ref, sc_ref, br_ref, gate_ref, dx_ref, dshift_ref, dscale_ref, dg_ref, dgate_ref, dbr_ref = refs
            accs = (dshift_ref, dscale_ref, dg_ref, dgate_ref)
        else:
            dh_ref, x_ref, dres_ref, g_ref, sc_ref, dx_ref, dshift_ref, dscale_ref, dg_ref = refs
            accs = (dshift_ref, dscale_ref, dg_ref)

        @pl.when(pl.program_id(0) == 0)
        def _():
            for acc in accs:
                acc[...] = jnp.zeros_like(acc)

        dh_v, xv, g_v = dh_ref[...], x_ref[...], g_ref[...]
        one_sc = 1.0 + sc_ref[...]
        r = lax.rsqrt(_mean1(xv * xv) + EPS)
        xn = xv * r
        dshift_ref[...] += _sum0(dh_v)
        dscale_ref[...] += _sum0(dh_v * (xn * g_v))
        dg_ref[...] += _sum0(dh_v * one_sc * xn)
        dxn = dh_v * (g_v * one_sc)
        dx = dres_ref[...] + r * (dxn - xn * _mean1(dxn * xn))
        dx_ref[...] = dx
        if with_gate:
            dgate_ref[...] += _sum0(dx * br_ref[...])
            dbr_ref[...] = (dx * gate_ref[...]).astype(dbr_ref.dtype)

    ins = [dh, xin, dres, g, scale] + ([branch, gate] if with_gate else [])
    in_specs = [tile, tile, tile, vec, vec] + ([tile, vec] if with_gate else [])
    out_specs = [tile, vec, vec, vec] + ([vec, tile] if with_gate else [])
    out_shape = [jax.ShapeDtypeStruct((S, D), F32)] + [jax.ShapeDtypeStruct((1, D), F32)] * 3
    if with_gate:
        out_shape += [jax.ShapeDtypeStruct((1, D), F32), jax.ShapeDtypeStruct((S, D), MXU_DTYPE)]
    return pl.pallas_call(body, name=name, grid=(S // ts,), in_specs=in_specs, out_specs=out_specs,
                          out_shape=out_shape, compiler_params=_params("arbitrary"))(*ins)


def _causal_weights(ws_ref, wt_ref, n_g):
    row = lax.broadcasted_iota(jnp.int32, (LANE, LANE), 0)
    col = lax.broadcasted_iota(jnp.int32, (LANE, LANE), 1)
    for g in range(n_g):
        wt_ref[g] = jnp.where(col <= row, ws_ref[g], 0.0).astype(wt_ref.dtype)


def _group_layernorm(v):
    xc = v - _mean1(v)
    rstd = lax.rsqrt(_mean1(xc * xc) + EPS)
    return xc * rstd, rstd


def _gmlp_fwd(proj, v_gain, w_s, b_t, out_gain, n_g, name):
    S = proj.shape[0]
    GW = n_g * LANE

    def body(p_ref, vg_ref, ws_ref, bt_ref, og_ref, on_ref, wt_ref):
        @pl.when(pl.program_id(0) == 0)
        def _():
            _causal_weights(ws_ref, wt_ref, n_g)

        for g in range(n_g):
            cols = slice(g * LANE, (g + 1) * LANE)
            u = _gelu(p_ref[:, cols])
            v = _gelu(p_ref[:, GW + g * LANE:GW + (g + 1) * LANE])
            vhat, _ = _group_layernorm(v)
            vln = (vhat * vg_ref[:, cols]).astype(MXU_DTYPE)
            mixed = _dot(wt_ref[g], vln, NN) + bt_ref[:, g:g + 1]
            o = u * mixed
            r = lax.rsqrt(_mean1(o * o) + EPS)
            on_ref[:, cols] = (o * r * og_ref[:, cols]).astype(on_ref.dtype)

    return pl.pallas_call(
        body, name=name, grid=(S // LANE,),
        in_specs=[pl.BlockSpec((LANE, 2 * GW), lambda n: (n, 0)),
                  pl.BlockSpec((1, GW), lambda n: (0, 0)),
                  pl.BlockSpec((n_g, LANE, LANE), lambda n: (0, 0, 0)),
                  pl.BlockSpec((LANE, n_g), lambda n: (0, 0)),
                  pl.BlockSpec((1, GW), lambda n: (0, 0))],
        out_specs=pl.BlockSpec((LANE, GW), lambda n: (n, 0)),
        out_shape=jax.ShapeDtypeStruct((S, GW), MXU_DTYPE),
        scratch_shapes=[pltpu.VMEM((n_g, LANE, LANE), MXU_DTYPE)],
        compiler_params=_params("arbitrary"),
    )(proj, v_gain, w_s, b_t, out_gain)


def _gmlp_bwd(proj, d_on, v_gain, w_s, b_t, out_gain, n_g, name):
    S = proj.shape[0]
    GW = n_g * LANE

    def body(p_ref, dn_ref, vg_ref, ws_ref, bt_ref, og_ref, dp_ref, dws_ref, dbt_ref, dvg_ref, dog_ref, wt_ref):
        @pl.when(pl.program_id(0) == 0)
        def _():
            _causal_weights(ws_ref, wt_ref, n_g)
            dws_ref[...] = jnp.zeros_like(dws_ref)
            dbt_ref[...] = jnp.zeros_like(dbt_ref)
            dvg_ref[...] = jnp.zeros_like(dvg_ref)
            dog_ref[...] = jnp.zeros_like(dog_ref)

        row = lax.broadcasted_iota(jnp.int32, (LANE, LANE), 0)
        col = lax.broadcasted_iota(jnp.int32, (LANE, LANE), 1)
        for g in range(n_g):
            cols = slice(g * LANE, (g + 1) * LANE)
            vcols = slice(GW + g * LANE, GW + (g + 1) * LANE)
            pu, pv = p_ref[:, cols], p_ref[:, vcols]
            u, v = _gelu(pu), _gelu(pv)
            vhat, rstd = _group_layernorm(v)
            gain = vg_ref[:, cols]
            vln = (vhat * gain).astype(MXU_DTYPE)
            mixed = _dot(wt_ref[g], vln, NN) + bt_ref[:, g:g + 1]
            o = u * mixed
            r = lax.rsqrt(_mean1(o * o) + EPS)
            oh = o * r
            dn = dn_ref[:, cols]
            dog_ref[:, cols] += _sum0(dn * oh)
            dhn = dn * og_ref[:, cols]
            d_o = r * (dhn - oh * _mean1(dhn * oh))
            du = d_o * mixed
            dmix = d_o * u
            dbt_ref[:, g:g + 1] += jnp.sum(dmix, axis=1, keepdims=True)
            dmix_b = dmix.astype(MXU_DTYPE)
            dws_ref[g] += jnp.where(col <= row, _dot(dmix_b, vln, NT), 0.0)
            dvln = _dot(wt_ref[g], dmix_b, TN)
            dvg_ref[:, cols] += _sum0(dvln * vhat)
            dxh = dvln * gain
            dv = rstd * (dxh - _mean1(dxh) - vhat * _mean1(dxh * vhat))
            dp_ref[:, cols] = (du * _gelu_grad(pu)).astype(dp_ref.dtype)
            dp_ref[:, vcols] = (dv * _gelu_grad(pv)).astype(dp_ref.dtype)

    return pl.pallas_call(
        body, name=name, grid=(S // LANE,),
        in_specs=[pl.BlockSpec((LANE, 2 * GW), lambda n: (n, 0)),
                  pl.BlockSpec((LANE, GW), lambda n: (n, 0)),
                  pl.BlockSpec((1, GW), lambda n: (0, 0)),
                  pl.BlockSpec((n_g, LANE, LANE), lambda n: (0, 0, 0)),
                  pl.BlockSpec((LANE, n_g), lambda n: (0, 0)),
                  pl.BlockSpec((1, GW), lambda n: (0, 0))],
        out_specs=[pl.BlockSpec((LANE, 2 * GW), lambda n: (n, 0)),
                   pl.BlockSpec((n_g, LANE, LANE), lambda n: (0, 0, 0)),
                   pl.BlockSpec((LANE, n_g), lambda n: (0, 0)),
                   pl.BlockSpec((1, GW), lambda n: (0, 0)),
                   pl.BlockSpec((1, GW), lambda n: (0, 0))],
        out_shape=[jax.ShapeDtypeStruct((S, 2 * GW), MXU_DTYPE),
                   jax.ShapeDtypeStruct((n_g, LANE, LANE), F32),
                   jax.ShapeDtypeStruct((LANE, n_g), F32),
                   jax.ShapeDtypeStruct((1, GW), F32),
                   jax.ShapeDtypeStruct((1, GW), F32)],
        scratch_shapes=[pltpu.VMEM((n_g, LANE, LANE), MXU_DTYPE)],
        compiler_params=_params("arbitrary"),
    )(proj, d_on, v_gain, w_s, b_t, out_gain)


def _tri_sum(v, tri, exact=True):
    hi = v.astype(MXU_DTYPE)
    if not exact:
        return _dot(hi, tri, NN)
    lo = (v - hi.astype(F32)).astype(MXU_DTYPE)
    return _dot(hi, tri, NN) + _dot(lo, tri, NN)


def _log_sigmoids(z):
    sp = jnp.log(1.0 + jnp.exp(-jnp.abs(z)))
    return jnp.minimum(z, 0.0) - sp, jnp.minimum(-z, 0.0) - sp


def _rows(i, size):
    return pl.ds(pl.multiple_of(i * size, size), size)


SB_QUERY_TILE = 512
SB_KEY_TILE = 256


def _sb_tiles(S):
    tq = _tile(S, SB_QUERY_TILE)
    tk = _tile(tq, SB_KEY_TILE)
    return tq, tk, S // tq, tq // tk


def _triangle(n, keep):
    row = lax.broadcasted_iota(jnp.int32, (n, n), 0)
    col = lax.broadcasted_iota(jnp.int32, (n, n), 1)
    return jnp.where(keep(row, col), 1.0, 0.0).astype(MXU_DTYPE)


def _strictly_before(tq, tk, key_offset):
    row = lax.broadcasted_iota(jnp.int32, (tq, tk), 0)
    col = lax.broadcasted_iota(jnp.int32, (tq, tk), 1)
    return col + key_offset < row


def _sb_specs(S, n_g, n_h):
    base = 2 * n_g
    q_spec = pl.BlockSpec((S, LANE), lambda h: (0, base + h))
    k_spec = pl.BlockSpec((S, LANE), lambda h: (0, base + n_h + h))
    v_spec = pl.BlockSpec((S, LANE), lambda h: (0, base + 2 * n_h + h))
    gain_spec = pl.BlockSpec((1, LANE), lambda h: (0, n_g + h))
    head_spec = pl.BlockSpec((S, LANE), lambda h: (0, h))
    return q_spec, k_spec, v_spec, gain_spec, head_spec


def _sb_fwd(proj, out_gain, n_g, n_h, name):
    S = proj.shape[0]
    TQ, TK, NQ, KPQ = _sb_tiles(S)
    scale = LANE ** -0.5
    q_spec, k_spec, v_spec, gain_spec, head_spec = _sb_specs(S, n_g, n_h)

    def body(q_ref, k_ref, v_ref, og_ref, o_ref, on_ref, ls_ref, qb, kb, vb):
        qb[...] = q_ref[...].astype(MXU_DTYPE)
        kb[...] = k_ref[...].astype(MXU_DTYPE)
        vb[...] = v_ref[...].astype(MXU_DTYPE)
        after = _triangle(TK, lambda r, c: r > c)

        def block(qi, j, ctail, acc, key_offset):
            z = _dot(qi, kb[_rows(j, TK), :], NT) * scale
            lb, l1m = _log_sigmoids(z)
            if key_offset is not None:
                strict = _strictly_before(TQ, TK, key_offset)
                l1m = jnp.where(strict, l1m, 0.0)
            a = jnp.exp(lb + ctail + _tri_sum(l1m, after))
            if key_offset is not None:
                a = jnp.where(strict, a, 0.0)
            acc = acc + _dot(a.astype(MXU_DTYPE), vb[_rows(j, TK), :], NN)
            return ctail + jnp.sum(l1m, axis=1, keepdims=True), acc

        def q_loop(i, carry):
            qi = qb[_rows(i, TQ), :]
            state = (jnp.zeros((TQ, 1), F32), jnp.zeros((TQ, LANE), F32))
            for d in reversed(range(KPQ)):
                state = block(qi, i * KPQ + d, state[0], state[1], d * TK)
            ctail, acc = lax.fori_loop(
                0, i * KPQ, lambda jj, st: block(qi, i * KPQ - 1 - jj, st[0], st[1], None), state)
            ls_ref[_rows(i, TQ), :] = jnp.broadcast_to(ctail, (TQ, LANE))
            o_ref[_rows(i, TQ), :] = acc
            r = lax.rsqrt(_mean1(acc * acc) + EPS)
            on_ref[_rows(i, TQ), :] = (acc * r * og_ref[...]).astype(on_ref.dtype)
            return carry

        lax.fori_loop(0, NQ, q_loop, 0)

    return pl.pallas_call(
        body, name=name, grid=(n_h,),
        in_specs=[q_spec, k_spec, v_spec, gain_spec],
        out_specs=[head_spec, head_spec, head_spec],
        out_shape=[jax.ShapeDtypeStruct((S, n_h * LANE), F32), jax.ShapeDtypeStruct((S, n_h * LANE), MXU_DTYPE),
                   jax.ShapeDtypeStruct((S, n_h * LANE), F32)],
        scratch_shapes=[pltpu.VMEM((S, LANE), MXU_DTYPE)] * 3,
        compiler_params=_params("parallel"),
    )(proj, proj, proj, out_gain)


def _sb_bwd(proj, o_sb, l_sum, d_on, out_gain, n_g, n_h, name):
    S = proj.shape[0]
    TQ, TK, NQ, KPQ = _sb_tiles(S)
    scale = LANE ** -0.5
    q_spec, k_spec, v_spec, gain_spec, head_spec = _sb_specs(S, n_g, n_h)
    dn_spec = pl.BlockSpec((S, LANE), lambda h: (0, n_g + h))
    dgain_spec = pl.BlockSpec((1, LANE), lambda h: (0, h))

    def body(q_ref, k_ref, v_ref, o_ref, ls_ref, dn_ref, og_ref, dq_ref, dk_ref, dv_ref, dog_ref,
             qb, kb, vb, dob, dk_acc, dv_acc):
        qb[...] = q_ref[...].astype(MXU_DTYPE)
        kb[...] = k_ref[...].astype(MXU_DTYPE)
        vb[...] = v_ref[...].astype(MXU_DTYPE)
        o, dn = o_ref[...], dn_ref[...]
        r = lax.rsqrt(_mean1(o * o) + EPS)
        oh = o * r
        dog_ref[...] = _sum0(dn * oh)
        dhn = dn * og_ref[...]
        dob[...] = (r * (dhn - oh * _mean1(dhn * oh))).astype(MXU_DTYPE)
        dk_acc[...] = jnp.zeros_like(dk_acc)
        dv_acc[...] = jnp.zeros_like(dv_acc)

        up_to = _triangle(TK, lambda r, c: r <= c)
        before = _triangle(TK, lambda r, c: r < c)

        def block(qi, doi, ltot, j, cl, cdl, dq, key_offset):
            kj, vj = kb[_rows(j, TK), :], vb[_rows(j, TK), :]
            z = _dot(qi, kj, NT) * scale
            lb, l1m_all = _log_sigmoids(z)
            l1m = l1m_all
            if key_offset is not None:
                strict = _strictly_before(TQ, TK, key_offset)
                l1m = jnp.where(strict, l1m_all, 0.0)
            a = jnp.exp(lb + (ltot - (cl + _tri_sum(l1m, up_to))))
            if key_offset is not None:
                a = jnp.where(strict, a, 0.0)
            dl = _dot(doi, vj, NT) * a
            d_l1m = cdl + _tri_sum(dl, before, exact=False)
            beta = jnp.exp(lb)
            dz = dl * (1.0 - beta) - beta * d_l1m
            if key_offset is not None:
                dz = jnp.where(strict, dz, 0.0)
            dzs = (dz * scale).astype(MXU_DTYPE)
            dq = dq + _dot(dzs, kj, NN)
            dk_acc[_rows(j, TK), :] += _dot(dzs, qi, TN)
            dv_acc[_rows(j, TK), :] += _dot(a.astype(MXU_DTYPE), doi, TN)
            return (cl + jnp.sum(l1m, axis=1, keepdims=True), cdl + jnp.sum(dl, axis=1, keepdims=True), dq)

        def q_loop(i, carry):
            qi, doi = qb[_rows(i, TQ), :], dob[_rows(i, TQ), :]
            ltot = ls_ref[_rows(i, TQ), :][:, :1]
            zero_col = jnp.zeros((TQ, 1), F32)
            state = lax.fori_loop(
                0, i * KPQ, lambda j, st: block(qi, doi, ltot, j, st[0], st[1], st[2], None),
                (zero_col, zero_col, jnp.zeros((TQ, LANE), F32)))
            for d in range(KPQ):
                state = block(qi, doi, ltot, i * KPQ + d, state[0], state[1], state[2], d * TK)
            dq_ref[_rows(i, TQ), :] = state[2].astype(dq_ref.dtype)
            return carry

        lax.fori_loop(0, NQ, q_loop, 0)
        dk_ref[...] = dk_acc[...].astype(dk_ref.dtype)
        dv_ref[...] = dv_acc[...].astype(dv_ref.dtype)

    W = n_h * LANE
    return pl.pallas_call(
        body, name=name, grid=(n_h,),
        in_specs=[q_spec, k_spec, v_spec, head_spec, head_spec, dn_spec, gain_spec],
        out_specs=[head_spec, head_spec, head_spec, dgain_spec],
        out_shape=[jax.ShapeDtypeStruct((S, W), MXU_DTYPE)] * 3 + [jax.ShapeDtypeStruct((1, W), F32)],
        scratch_shapes=[pltpu.VMEM((S, LANE), MXU_DTYPE)] * 4 + [pltpu.VMEM((S, LANE), F32)] * 2,
        compiler_params=_params("parallel"),
    )(proj, proj, proj, o_sb, l_sum, d_on, out_gain)


def _mod_part(c_all, w_ada, b_ada_cols, name):
    B, K = c_all.shape
    N = w_ada.shape[1]
    tn = _tile(N, 512)

    def body(c_ref, w_ref, b_ref, o_ref):
        cv = c_ref[...]
        ca = (cv * jax.nn.sigmoid(cv)).astype(MXU_DTYPE)
        o_ref[...] = _dot(ca, w_ref[...].astype(MXU_DTYPE), NN) + b_ref[...]

    return pl.pallas_call(
        body, name=name, grid=(N // tn,),
        in_specs=[pl.BlockSpec((B, K), lambda j: (0, 0)), pl.BlockSpec((K, tn), lambda j: (0, j)),
                  pl.BlockSpec((1, tn), lambda j: (0, j))],
        out_specs=pl.BlockSpec((B, tn), lambda j: (0, j)),
        out_shape=jax.ShapeDtypeStruct((B, N), F32), compiler_params=_params("parallel"))(c_all, w_ada, b_ada_cols)


def _adamw_math(w, g, m, v):
    m = ADAM_B1 * m + (1.0 - ADAM_B1) * g
    v = ADAM_B2 * v + (1.0 - ADAM_B2) * (g * g)
    m_hat = m / (1.0 - ADAM_B1 ** ADAM_STEP)
    v_hat = v / (1.0 - ADAM_B2 ** ADAM_STEP)
    delta = -ADAM_LR * (m_hat / (jnp.sqrt(v_hat) + ADAM_EPS) + ADAM_WD * w)
    return delta, m, v


def _adamw(w, g, m, v, name):
    R, C = w.shape
    tr = _tile(R, max(8, (1 << 19) // C), 8)
    spec = pl.BlockSpec((tr, C), lambda i: (i, 0))

    def body(w_ref, g_ref, m_ref, v_ref, go_ref, d_ref, mo_ref, vo_ref):
        g = g_ref[...]
        go_ref[...] = g
        d_ref[...], mo_ref[...], vo_ref[...] = _adamw_math(w_ref[...], g, m_ref[...], v_ref[...])

    return pl.pallas_call(body, name=name, grid=(R // tr,), in_specs=[spec] * 4, out_specs=[spec] * 4,
                          out_shape=[jax.ShapeDtypeStruct((R, C), F32)] * 4, compiler_params=_params("parallel"))(w, g, m, v)


def _adamw_ada(c_all, dmod_cols, w, m, v, name):
    K, N = w.shape
    B = c_all.shape[0]
    tk, tn = _tile(K, 512), _tile(N, 1024)
    spec = pl.BlockSpec((tk, tn), lambda i, j: (i, j))

    def body(c_ref, dm_ref, w_ref, m_ref, v_ref, g_ref, d_ref, mo_ref, vo_ref):
        cv = c_ref[...]
        ca = (cv * jax.nn.sigmoid(cv)).astype(MXU_DTYPE)
        g = _dot(ca, dm_ref[...].astype(MXU_DTYPE), TN)
        g_ref[...] = g
        d_ref[...], mo_ref[...], vo_ref[...] = _adamw_math(w_ref[...], g, m_ref[...], v_ref[...])

    return pl.pallas_call(
        body, name=name, grid=(K // tk, N // tn),
        in_specs=[pl.BlockSpec((B, tk), lambda i, j: (0, i)), pl.BlockSpec((B, tn), lambda i, j: (0, j)), spec, spec, spec],
        out_specs=[spec] * 4, out_shape=[jax.ShapeDtypeStruct((K, N), F32)] * 4,
        compiler_params=_params("parallel", "parallel"))(c_all, dmod_cols, w, m, v)


def _sum_devices(gathered, n_dev, name):
    R = gathered.shape[0] // n_dev
    C = gathered.shape[1]
    tr = _tile(R, 512, 8)
    n_blk = R // tr

    def body(*refs):
        acc = refs[0][...]
        for r in refs[1:n_dev]:
            acc = acc + r[...]
        refs[n_dev][...] = acc

    in_specs = [pl.BlockSpec((tr, C), functools.partial(lambda i, d: (d * n_blk + i, 0), d=d)) for d in range(n_dev)]
    return pl.pallas_call(body, name=name, grid=(n_blk,), in_specs=in_specs,
                          out_specs=pl.BlockSpec((tr, C), lambda i: (i, 0)),
                          out_shape=jax.ShapeDtypeStruct((R, C), F32), compiler_params=_params("parallel"))(*([gathered] * n_dev))


def _place():
    x, y, c = lax.axis_index("x"), lax.axis_index("y"), lax.axis_index("c")
    return x, y, c


def _allgather8(blk, name):
    m_per, n = blk.shape

    def body(x_ref, out_ref, send_sems, recv_sems, local_sem):
        x, y, c = _place()
        me, sibling = (x, y, c), (x, y, 1 - c)
        chips = [(1 - x, y), (x, 1 - y), (1 - x, 1 - y)]

        def rows(px, py, pc):
            return out_ref.at[pl.ds((4 * px + 2 * py + pc) * m_per, m_per), :]

        def copy(k, block, to, src=None):
            return pltpu.make_async_remote_copy(
                src_ref=rows(*block) if src is None else src, dst_ref=rows(*block),
                send_sem=send_sems.at[k], recv_sem=recv_sems.at[k], device_id=to, device_id_type=MESH)

        mine = pltpu.make_async_copy(x_ref, rows(*me), local_sem)
        mine.start()
        first = [copy(0, me, sibling, src=x_ref)]
        first += [copy(1 + j, me, (*chip, c), src=x_ref) for j, chip in enumerate(chips)]
        for cp in first:
            cp.start()
        passed = [copy(4 + j, (*chip, c), sibling) for j, chip in enumerate(chips)]
        for j, chip in enumerate(chips):
            copy(1 + j, (*chip, c), me).wait_recv()
            passed[j].start()
        copy(0, sibling, me).wait_recv()
        for j, chip in enumerate(chips):
            copy(4 + j, (*chip, 1 - c), me).wait_recv()
        for cp in first + passed:
            cp.wait_send()
        mine.wait()

    return pl.pallas_call(
        body, name=name,
        out_shape=jax.ShapeDtypeStruct((8 * m_per, n), blk.dtype),
        in_specs=[pl.BlockSpec(memory_space=pltpu.VMEM)],
        out_specs=pl.BlockSpec(memory_space=pltpu.VMEM),
        scratch_shapes=[pltpu.SemaphoreType.DMA((7,)), pltpu.SemaphoreType.DMA((7,)), pltpu.SemaphoreType.DMA],
        compiler_params=pltpu.CompilerParams(vmem_limit_bytes=V7X_VMEM_LIMIT),
    )(blk)


class _Sharded:
    def __init__(self, shard_shape, by_cols):
        r, c = shard_shape
        self.by_cols = by_cols
        self.full = (r, N_CHIPS * c) if by_cols else (N_CHIPS * r, c)
        self.shard = (r, c)
        self.half_rows = r // 2
        self.half = (r // 2, c)

    def shard_of(self, ref, k):
        r, c = self.shard
        return ref.at[:, pl.ds(k * c, c)] if self.by_cols else ref.at[pl.ds(k * r, r), :]

    def half_of(self, ref, k, hc):
        r, c = self.shard
        h = self.half_rows
        if self.by_cols:
            return ref.at[pl.ds(hc * h, h), pl.ds(k * c, c)]
        return ref.at[pl.ds(k * r + hc * h, h), :]

    def chunk_of(self, ref, k, hc, ch, n):
        r, c = self.shard
        h = self.half_rows
        q = h // n
        if self.by_cols:
            return ref.at[pl.ds(hc * h + ch * q, q), pl.ds(k * c, c)]
        return ref.at[pl.ds(k * r + hc * h + ch * q, q), :]

    def half_of_shard(self, ref, hc):
        return ref.at[pl.ds(hc * self.half_rows, self.half_rows), :]

    def part_of_halves(self, ref, k):
        r, c = self.shard
        h = self.half_rows
        return ref.at[:, pl.ds(k * c, c)] if self.by_cols else ref.at[pl.ds(k * h, h), :]


def _on_each_place(x, y, c, fn, by_chip=True, by_core=True):
    q = 2 * x + y
    for k in range(N_CHIPS if by_chip else 1):
        for cc in range(2 if by_core else 1):
            cond = None
            if by_chip:
                cond = q == k
            if by_core:
                cond = (c == cc) if cond is None else jnp.logical_and(cond, c == cc)
            pl.when(cond)(functools.partial(fn, k, cc))


def _chip_id(k, c):
    return (k // 2, k % 2, c)


def _handshake(peers):
    barrier = pltpu.get_barrier_semaphore()
    for peer in peers:
        pl.semaphore_signal(barrier, inc=1, device_id=peer, device_id_type=MESH)
    pl.semaphore_wait(barrier, len(peers))


def _on_sequencer(body, inputs, out_structs, n_copies, peers_of, name, collective_id, return_inputs=False):
    in_refs = [jax.new_ref(a, memory_space=pltpu.MemorySpace.HBM) for a in inputs]
    out_refs = [jax.empty_ref(s, memory_space=pltpu.MemorySpace.HBM) for s in out_structs]

    @pl.kernel(mesh=plsc.ScalarSubcoreMesh(axis_name="sequencer", num_cores=1), name=name,
               scratch_types=(pltpu.SemaphoreType.DMA((n_copies,)), pltpu.SemaphoreType.DMA((n_copies,))),
               compiler_params=pltpu.CompilerParams(collective_id=collective_id))
    def launch(send_sems, recv_sems):
        x, y, c = _place()
        _handshake(peers_of(x, y, c))
        body(in_refs, out_refs, send_sems, recv_sems, x, y, c)

    launch()
    return [r[...] for r in (in_refs if return_inputs else out_refs)]


def _sibling(x, y, c):
    return [(x, y, 1 - c)]


def _same_core_of_other_chips(x, y, c):
    return [(1 - x, y, c), (x, 1 - y, c), (1 - x, 1 - y, c)]


GATHER_CHUNKS = 4
GATHER_COPIES = 6 * GATHER_CHUNKS


def _allgather8_on_sequencer(blk, name, collective_id):
    m_per, n = blk.shape
    x, y, c = _place()
    placed = lax.dynamic_update_slice(jnp.zeros((8 * m_per, n), blk.dtype), blk, ((4 * x + 2 * y + c) * m_per, 0))

    def body(refs, _, send_sems, recv_sems, x, y, c):
        out_ref, = refs

        def at_place(k, cc):
            def rows(kk, pc):
                return out_ref.at[pl.ds((2 * kk + pc) * m_per, m_per), :]

            def copy(slot, block, to):
                return pltpu.make_async_remote_copy(src_ref=rows(*block), dst_ref=rows(*block), send_sem=send_sems.at[slot],
                                                    recv_sem=recv_sems.at[slot], device_id=to, device_id_type=MESH)

            others = [k ^ flip for flip in FLIPS]
            sends = [copy(0, (k, cc), _chip_id(k, 1 - cc))] + [copy(1 + j, (k, cc), _chip_id(kk, cc)) for j, kk in enumerate(others)]
            for cp in sends:
                cp.start()
            for j, kk in enumerate(others):
                copy(1 + j, (kk, cc), _chip_id(k, cc)).wait_recv()
                cp = copy(4 + j, (kk, cc), _chip_id(k, 1 - cc))
                cp.start()
                sends.append(cp)
            copy(0, (k, 1 - cc), _chip_id(k, cc)).wait_recv()
            for j, kk in enumerate(others):
                copy(4 + j, (kk, 1 - cc), _chip_id(k, cc)).wait_recv()
            for cp in sends:
                cp.wait_send()

        _on_each_place(x, y, c, at_place)

    def peers(x, y, c):
        return _sibling(x, y, c) + _same_core_of_other_chips(x, y, c)

    return _on_sequencer(body, [placed], [], 7, peers, name, collective_id, return_inputs=True)[0]


def _gather_weights(fulls, geoms, name, collective_id):
    n_w = len(fulls)
    n_ch, n_relay = GATHER_CHUNKS, GATHER_CHUNKS // 2
    f_refs = [jax.new_ref(f, memory_space=pltpu.MemorySpace.HBM) for f in fulls]
    FLIP_X, FLIP_Y, FLIP_BOTH = FLIPS
    TO_X, TO_Y, RELAY_TO_Y, RELAY_TO_X, ON_X, ON_Y, ON_DIAG = 0, n_ch, 2 * n_ch, 2 * n_ch + n_relay, 3 * n_ch, 4 * n_ch, 5 * n_ch

    @pl.kernel(mesh=plsc.ScalarSubcoreMesh(axis_name="sequencer", num_cores=1), name=name,
               scratch_types=(pltpu.SemaphoreType.DMA((GATHER_COPIES * n_w,)), pltpu.SemaphoreType.DMA((GATHER_COPIES * n_w,))),
               compiler_params=pltpu.CompilerParams(collective_id=collective_id))
    def launch(send_sems, recv_sems):
        x, y, c = _place()
        _handshake([(x, y, 1 - c), (1 - x, y, c), (x, 1 - y, c)])

        def at_place(k, cc):
            kx, ky, kd = k ^ FLIP_X, k ^ FLIP_Y, k ^ FLIP_BOTH
            me, sibling = _chip_id(k, cc), _chip_id(k, 1 - cc)
            started = []

            def copy(i, slot, src, dst, to, start=True):
                cp = pltpu.make_async_remote_copy(src_ref=src, dst_ref=dst, send_sem=send_sems.at[GATHER_COPIES * i + slot],
                                                  recv_sem=recv_sems.at[GATHER_COPIES * i + slot], device_id=to, device_id_type=MESH)
                if start:
                    cp.start()
                    started.append(cp)
                return cp

            def pass_on(i, slot, ref, to):
                copy(i, slot, ref, ref, to)

            def landed(i, slot, ref):
                copy(i, slot, ref, ref, me, start=False).wait_recv()

            y_order = [(n_relay + s) % n_ch for s in range(n_ch)]
            for i, (g, f_ref) in enumerate(zip(geoms, f_refs)):
                for s in range(n_ch):
                    pass_on(i, TO_X + s, g.chunk_of(f_ref, k, cc, s, n_ch), _chip_id(kx, cc))
                    pass_on(i, TO_Y + y_order[s], g.chunk_of(f_ref, k, cc, y_order[s], n_ch), _chip_id(ky, cc))
            for i, (g, f_ref) in enumerate(zip(geoms, f_refs)):
                for s in range(n_ch):
                    from_x = g.chunk_of(f_ref, kx, cc, s, n_ch)
                    landed(i, TO_X + s, from_x)
                    if s < n_relay:
                        pass_on(i, RELAY_TO_Y + s, from_x, _chip_id(ky, cc))
                    pass_on(i, ON_X + s, from_x, sibling)
                    ch = y_order[s]
                    from_y = g.chunk_of(f_ref, ky, cc, ch, n_ch)
                    landed(i, TO_Y + ch, from_y)
                    if ch >= n_relay:
                        pass_on(i, RELAY_TO_X + ch - n_relay, from_y, _chip_id(kx, cc))
                    pass_on(i, ON_Y + ch, from_y, sibling)
                for r in range(n_relay):
                    via_y = g.chunk_of(f_ref, kd, cc, r, n_ch)
                    landed(i, RELAY_TO_Y + r, via_y)
                    pass_on(i, ON_DIAG + r, via_y, sibling)
                    via_x = g.chunk_of(f_ref, kd, cc, n_relay + r, n_ch)
                    landed(i, RELAY_TO_X + r, via_x)
                    pass_on(i, ON_DIAG + n_relay + r, via_x, sibling)
            for i, (g, f_ref) in enumerate(zip(geoms, f_refs)):
                for slot, kk in ((ON_X, kx), (ON_Y, ky), (ON_DIAG, kd)):
                    for ch in range(n_ch):
                        landed(i, slot + ch, g.chunk_of(f_ref, kk, 1 - cc, ch, n_ch))
            for cp in started:
                cp.wait_send()

        _on_each_place(x, y, c, at_place)

    launch()
    return [f_ref[...] for f_ref in f_refs]


def _swap_core_halves(grads, geoms, name, collective_id):
    n_cp = sum(1 if g.by_cols else N_CHIPS for g in geoms)

    def body(g_refs, t_refs, send_sems, recv_sems, x, y, c):

        def at_place(_, cc):
            def pairs(hc):
                out = []
                for g, g_ref, t_ref in zip(geoms, g_refs, t_refs):
                    if g.by_cols:
                        out.append((g_ref.at[pl.ds(hc * g.half_rows, g.half_rows), :], t_ref))
                    else:
                        out += [(g.half_of(g_ref, k, hc), g.part_of_halves(t_ref, k)) for k in range(N_CHIPS)]
                return out

            sends = [pltpu.make_async_remote_copy(src_ref=src, dst_ref=dst, send_sem=send_sems.at[n],
                                                  recv_sem=recv_sems.at[n], device_id=(x, y, 1 - cc), device_id_type=MESH)
                     for n, (src, dst) in enumerate(pairs(1 - cc))]
            for cp in sends:
                cp.start()
            for n, (src, dst) in enumerate(pairs(cc)):
                pltpu.make_async_remote_copy(src_ref=src, dst_ref=dst, send_sem=send_sems.at[n], recv_sem=recv_sems.at[n],
                                             device_id=(x, y, cc), device_id_type=MESH).wait_recv()
            for cp in sends:
                cp.wait_send()

        _on_each_place(x, y, c, at_place, by_chip=False)

    return _on_sequencer(body, grads, [jax.ShapeDtypeStruct((g.full[0] // 2, g.full[1]), F32) for g in geoms],
                         n_cp, _sibling, name, collective_id)


def _scatter_chip_sums(sums, geoms, name, collective_id):
    def body(s_refs, r_refs, send_sems, recv_sems, x, y, c):

        def at_place(k, _):
            sends = []
            for i, (g, s_ref, r_ref) in enumerate(zip(geoms, s_refs, r_refs)):
                for j, flip in enumerate(FLIPS):
                    kk = k ^ flip
                    cp = pltpu.make_async_remote_copy(
                        src_ref=g.part_of_halves(s_ref, kk), dst_ref=r_ref.at[j], send_sem=send_sems.at[3 * i + j],
                        recv_sem=recv_sems.at[3 * i + j], device_id=(kk // 2, kk % 2, c), device_id_type=MESH)
                    cp.start()
                    sends.append(cp)
            for i, (g, s_ref, r_ref) in enumerate(zip(geoms, s_refs, r_refs)):
                for j in range(len(FLIPS)):
                    pltpu.make_async_remote_copy(
                        src_ref=g.part_of_halves(s_ref, k), dst_ref=r_ref.at[j], send_sem=send_sems.at[3 * i + j],
                        recv_sem=recv_sems.at[3 * i + j], device_id=(x, y, c), device_id_type=MESH).wait_recv()
            for cp in sends:
                cp.wait_send()

        _on_each_place(x, y, c, at_place, by_core=False)

    return _on_sequencer(body, sums, [jax.ShapeDtypeStruct((len(FLIPS),) + g.half, WIRE_DTYPE) for g in geoms],
                         len(FLIPS) * len(sums), _same_core_of_other_chips, name, collective_id)


def _share_reduced_halves(reduced, geoms, name, collective_id):
    def body(out_refs, _, send_sems, recv_sems, x, y, c):

        def at_place(_, cc):
            sends = []
            for i, (g, ref) in enumerate(zip(geoms, out_refs)):
                mine = g.half_of_shard(ref, cc)
                cp = pltpu.make_async_remote_copy(src_ref=mine, dst_ref=mine, send_sem=send_sems.at[i],
                                                  recv_sem=recv_sems.at[i], device_id=(x, y, 1 - cc), device_id_type=MESH)
                cp.start()
                sends.append(cp)
            for i, (g, ref) in enumerate(zip(geoms, out_refs)):
                theirs = g.half_of_shard(ref, 1 - cc)
                pltpu.make_async_remote_copy(src_ref=theirs, dst_ref=theirs, send_sem=send_sems.at[i],
                                             recv_sem=recv_sems.at[i], device_id=(x, y, cc), device_id_type=MESH).wait_recv()
            for cp in sends:
                cp.wait_send()

        _on_each_place(x, y, c, at_place, by_chip=False)

    return _on_sequencer(body, reduced, [], len(reduced), _sibling, name, collective_id, return_inputs=True)


def _chip_sum(place, grad, theirs, g, name):
    RH, C = theirs.shape
    h = g.half_rows
    tr = _tile(h, 256, 16)
    tc = _tile(C, 2048)
    per_half = h // tr

    if g.by_cols:
        grad_map = lambda i, j, p: (p[1] * per_half + i, j)
    else:
        grad_map = lambda i, j, p: ((i // per_half) * 2 * per_half + p[1] * per_half + i % per_half, j)

    def body(p_ref, a_ref, b_ref, o_ref):
        o_ref[...] = (a_ref[...] + b_ref[...]).astype(o_ref.dtype)

    return pl.pallas_call(
        body, name=name,
        grid_spec=pltpu.PrefetchScalarGridSpec(
            num_scalar_prefetch=1, grid=(RH // tr, C // tc),
            in_specs=[pl.BlockSpec((tr, tc), grad_map), pl.BlockSpec((tr, tc), lambda i, j, p: (i, j))],
            out_specs=pl.BlockSpec((tr, tc), lambda i, j, p: (i, j))),
        out_shape=jax.ShapeDtypeStruct((RH, C), WIRE_DTYPE),
        compiler_params=_params("parallel", "parallel"),
    )(place, grad, theirs)


def _reduce_half(place, grad, theirs, others, g, name):
    h, wc = g.half
    tr = _tile(h, 256, 16)
    per_half = h // tr
    if g.by_cols:
        tc = wc
        grad_map = lambda i, p: (p[1] * per_half + i, p[0])
        theirs_map = lambda i, p: (i, p[0])
    else:
        tc = wc
        grad_map = lambda i, p: (p[0] * 2 * per_half + p[1] * per_half + i, 0)
        theirs_map = lambda i, p: (p[0] * per_half + i, 0)

    def body(p_ref, a_ref, b_ref, o0_ref, o1_ref, o2_ref, out_ref):
        acc = a_ref[...] + b_ref[...]
        for o_ref in (o0_ref, o1_ref, o2_ref):
            acc = acc + o_ref[...].astype(F32)
        out_ref[...] = acc

    other_specs = [pl.BlockSpec((None, tr, tc), functools.partial(lambda i, p, j: (j, i, 0), j=j)) for j in range(len(FLIPS))]
    return pl.pallas_call(
        body, name=name,
        grid_spec=pltpu.PrefetchScalarGridSpec(
            num_scalar_prefetch=1, grid=(per_half,),
            in_specs=[pl.BlockSpec((tr, tc), grad_map), pl.BlockSpec((tr, tc), theirs_map)] + other_specs,
            out_specs=pl.BlockSpec((tr, tc), lambda i, p: (p[1] * per_half + i, 0))),
        out_shape=jax.ShapeDtypeStruct(g.shard, F32),
        compiler_params=_params("arbitrary"),
    )(place, grad, theirs, others, others, others)


SMALL = ("b_ada", "norm1_g", "v_norm_g", "w_spatial", "b_spatial", "out_norm_g", "norm2_g", "final_g")
BIG = ("w_in", "w_out", "w_gate", "w_up", "w_down")
BY_COLS = {"w_in": True, "w_out": False, "w_gate": True, "w_up": True, "w_down": False}
ORDER = ("w_ada", "b_ada", "norm1_g", "w_in", "v_norm_g", "w_spatial", "b_spatial", "out_norm_g", "w_out",
         "norm2_g", "w_gate", "w_up", "w_down", "final_g")


def _pack(parts):
    return jnp.concatenate([parts[n].reshape(-1) for n in SMALL]).reshape(-1, LANE)


def _unpack(slab, shapes):
    flat = slab.reshape(-1)
    out, at = {}, 0
    for n in SMALL:
        size = math.prod(shapes[n])
        out[n] = flat[at:at + size].reshape(shapes[n])
        at += size
    return out


def kernel(x, c, w_ada, b_ada, norm1_g, w_in, v_norm_g, w_spatial, b_spatial, out_norm_g, w_out, norm2_g, w_gate, w_up, w_down, final_g, loss_target, m_w_ada, m_b_ada, m_norm1_g, m_w_in, m_v_norm_g, m_w_spatial, m_b_spatial, m_out_norm_g, m_w_out, m_norm2_g, m_w_gate, m_w_up, m_w_down, m_final_g, v_w_ada, v_b_ada, v_norm1_g, v_w_in, v_v_norm_g, v_w_spatial, v_b_spatial, v_out_norm_g, v_w_out, v_norm2_g, v_w_gate, v_w_up, v_w_down, v_final_g):
    weights = dict(w_ada=w_ada, b_ada=b_ada, norm1_g=norm1_g, w_in=w_in, v_norm_g=v_norm_g, w_spatial=w_spatial,
                   b_spatial=b_spatial, out_norm_g=out_norm_g, w_out=w_out, norm2_g=norm2_g, w_gate=w_gate, w_up=w_up,
                   w_down=w_down, final_g=final_g)
    m_in = dict(w_ada=m_w_ada, b_ada=m_b_ada, norm1_g=m_norm1_g, w_in=m_w_in, v_norm_g=m_v_norm_g, w_spatial=m_w_spatial,
                b_spatial=m_b_spatial, out_norm_g=m_out_norm_g, w_out=m_w_out, norm2_g=m_norm2_g, w_gate=m_w_gate,
                w_up=m_w_up, w_down=m_w_down, final_g=m_final_g)
    v_in = dict(w_ada=v_w_ada, b_ada=v_b_ada, norm1_g=v_norm1_g, w_in=v_w_in, v_norm_g=v_v_norm_g, w_spatial=v_w_spatial,
                b_spatial=v_b_spatial, out_norm_g=v_out_norm_g, w_out=v_w_out, norm2_g=v_norm2_g, w_gate=v_w_gate,
                w_up=v_w_up, w_down=v_w_down, final_g=v_final_g)

    S, D = x.shape[1], x.shape[2]
    n_g = v_norm_g.shape[-1] // LANE
    n_h = (D - n_g * LANE) // LANE
    GW = n_g * LANE
    xi, yi, ci = _place()
    chip = 2 * xi + yi
    me = 4 * xi + 2 * yi + ci
    place = jnp.stack([chip, ci]).astype(jnp.int32)

    xs, target = x[0], loss_target[0]
    geoms = [_Sharded(weights[n].shape[1:], BY_COLS[n]) for n in BIG]

    full = {}
    for i, group in enumerate((("w_in",), ("w_out",), ("w_gate", "w_up"), ("w_down",))):
        gg = [geoms[BIG.index(n)] for n in group]
        own = [_cast_into_full(place, weights[n][0], g, "cast_" + n) for n, g in zip(group, gg)]
        gathered = _gather_weights(own, gg, "gather_" + "_".join(group), 1 + i)
        full.update(zip(group, gathered))

    c_pad = jnp.concatenate([c, jnp.zeros((7, D), F32)], axis=0)
    c_all = _allgather8(c_pad, "gather_c")[::8]
    n_ada = w_ada.shape[2]
    b_cols = lax.dynamic_slice(b_ada, (0, chip * n_ada), (1, n_ada))
    mod_parts = _allgather8(_mod_part(c_all, w_ada[0], b_cols, "mod_part"), "gather_mod")
    mod_all = mod_parts.reshape(N_CHIPS, 2, 8, n_ada)[:, 0].transpose(1, 0, 2).reshape(8, N_CHIPS * n_ada)
    mod = lax.dynamic_slice(mod_all, (me, 0), (1, 6 * D))
    shift1, scale1, gate1, shift2, scale2, gate2 = [mod[:, i * D:(i + 1) * D] for i in range(6)]

    b_t = b_spatial[0].T
    h1 = _norm_mod(xs, norm1_g, scale1, shift1, "norm1")
    proj, = _mm("nn", h1, full["w_in"], [F32], "proj")
    on_gm = _gmlp_fwd(proj, v_norm_g, w_spatial[0], b_t, out_norm_g, n_g, "gmlp_fwd")
    o_sb, on_sb, l_sum = _sb_fwd(proj, out_norm_g, n_g, n_h, "sb_fwd")
    o_n = jnp.concatenate([on_gm, on_sb], axis=1)
    attn, = _mm("nn", o_n, full["w_out"], [F32], "attn_out")
    x1, h2 = _residual_norm_mod(xs, attn, gate1, norm2_g, scale2, shift2, "norm2")
    a_g, a_u, f_in = _gate_up(h2, full["w_gate"], full["w_up"], "gate_up")
    f, = _mm("nn", f_in, full["w_down"], [F32], "down", tm=1024)
    dx2, df, d_gate2, d_final_g, loss_part = _final_loss_bwd(x1, f, gate2, final_g.reshape(1, D), target, "final")
    loss = lax.psum(loss_part[0, 0], ("x", "y", "c"))

    geom_of = dict(zip(BIG, geoms))
    grad_out, delta, new_m, new_v = {}, {}, {}, {}

    def swap(group, grads, collective_id):
        return _swap_core_halves(grads, [geom_of[n] for n in group], "swap_" + "_".join(group), collective_id)

    def chip_sums(group, grads, theirs, after):
        return [_chip_sum(place, gr, _then(after, t), geom_of[n], "chip_sum_" + n) for n, gr, t in zip(group, grads, theirs)]

    def scatter(group, sums, collective_id):
        return _scatter_chip_sums(sums, [geom_of[n] for n in group], "scatter_" + "_".join(group), collective_id)

    def reduce_halves(group, grads, theirs, others, after):
        return [_reduce_half(place, gr, t, _then(after, o), geom_of[n], "reduce_" + n)
                for n, gr, t, o in zip(group, grads, theirs, others)]

    def share(group, halves, collective_id):
        return _share_reduced_halves(halves, [geom_of[n] for n in group], "share_" + "_".join(group), collective_id)

    def adamw(group, reduced, after):
        for n, r in zip(group, reduced):
            go, d, mo, vo = _adamw(weights[n][0], _then(after, r), m_in[n][0], v_in[n][0], "adamw_" + n)
            grad_out[n], delta[n], new_m[n], new_v[n] = go[None], d[None], mo[None], vo[None]
        return d

    g_down = ("w_down",)
    g_ffn = ("w_gate", "w_up")
    g_out = ("w_out",)
    g_in = ("w_in",)

    gr_down = _mm("tn", f_in, df, [F32], "d_w_down", tm=1408, tn=1024)
    th_down = swap(g_down, gr_down, 6)
    d_ag, d_au = _mm("nt", df, full["w_down"], [MXU_DTYPE, MXU_DTYPE], "d_ffn_in", extras=(a_g, a_u),
                     epilogue=_swiglu_bwd_epilogue)
    sm_down = chip_sums(g_down, gr_down, th_down, after=d_ag)
    ot_down = scatter(g_down, sm_down, 7)
    gr_ffn = [_mm("tn", h2, _then(sm_down, d_ag), [F32], "d_w_gate")[0], _mm("tn", h2, d_au, [F32], "d_w_up")[0]]
    th_ffn = swap(g_ffn, gr_ffn, 9)
    dh2 = _mm_ktiled("nt", [(_then(gr_ffn, d_ag), full["w_gate"]), (d_au, full["w_up"])], "d_h2", tn=512)
    sm_ffn = chip_sums(g_ffn, gr_ffn, th_ffn, after=dh2)
    ot_ffn = scatter(g_ffn, sm_ffn, 10)
    hv_down = reduce_halves(g_down, gr_down, th_down, ot_down, after=sm_ffn)
    rd_down = share(g_down, hv_down, 8)
    dx1, d_shift2, d_scale2, d_norm2_g, d_gate1, d_attn = _norm_mod_bwd(
        _then(hv_down, dh2), x1, dx2, norm2_g, scale2, "norm2_bwd", branch=attn, gate=gate1)
    gr_out = _mm("tn", o_n, d_attn, [F32], "d_w_out")
    th_out = swap(g_out, gr_out, 12)
    d_on, = _mm("nt", _then(gr_out, d_attn), full["w_out"], [F32], "d_o")
    dp_gm, d_w_spatial, d_b_t, d_v_norm_g, d_og_gm = _gmlp_bwd(proj, d_on, v_norm_g, w_spatial[0], b_t, out_norm_g, n_g, "gmlp_bwd")
    dq, dk, dv, d_og_sb = _sb_bwd(proj, o_sb, l_sum, _then(dp_gm, d_on), out_norm_g, n_g, n_h, "sb_bwd")
    sm_out = chip_sums(g_out, gr_out, th_out, after=dq)
    ot_out = scatter(g_out, sm_out, 13)
    hv_ffn = reduce_halves(g_ffn, gr_ffn, th_ffn, ot_ffn, after=sm_out)
    rd_ffn = share(g_ffn, hv_ffn, 11)
    dproj = jnp.concatenate([_then(hv_ffn, dp_gm), dq, dk, dv], axis=1)
    gr_in = _mm("tn", h1, dproj, [F32], "d_w_in")
    th_in = swap(g_in, gr_in, 15)
    hv_out = reduce_halves(g_out, gr_out, th_out, ot_out, after=gr_in)
    rd_out = share(g_out, hv_out, 14)
    dh1, = _mm("nt", _then(gr_in, dproj), full["w_in"], [F32], "d_h1", tm=1024)
    grad_x, d_shift1, d_scale1, d_norm1_g = _norm_mod_bwd(dh1, xs, dx1, norm1_g, scale1, "norm1_bwd")
    sm_in = chip_sums(g_in, gr_in, th_in, after=grad_x)
    ot_in = scatter(g_in, sm_in, 16)

    dmod = jnp.concatenate([d_shift1, d_scale1, d_gate1, d_shift2, d_scale2, d_gate2], axis=1)
    small_parts = dict(b_ada=dmod, norm1_g=d_norm1_g, v_norm_g=d_v_norm_g, w_spatial=d_w_spatial, b_spatial=d_b_t.T,
                       out_norm_g=jnp.concatenate([d_og_gm, d_og_sb], axis=1), norm2_g=d_norm2_g, final_g=d_final_g)
    slab = _then(sm_in, _pack(small_parts))
    rows = slab.shape[0]
    gathered = _allgather8_on_sequencer(slab, "gather_small", 18)
    done = adamw(g_down, rd_down, after=slab)
    done = adamw(g_ffn, rd_ffn, after=done)
    done = adamw(g_out, rd_out, after=done)
    gathered = _then(done, gathered)
    small_shapes = {n: weights[n].shape for n in SMALL}
    small_sum = _sum_devices(gathered, 8, "sum_small")
    dmod_all = gathered.reshape(8, rows * LANE)[:, :6 * D]
    dmod_cols = lax.dynamic_slice(dmod_all, (0, chip * n_ada), (8, n_ada))
    g_ada, d, mo, vo = _adamw_ada(c_all, dmod_cols, w_ada[0], m_w_ada[0], v_w_ada[0], "adamw_w_ada")
    grad_out["w_ada"], delta["w_ada"], new_m["w_ada"], new_v["w_ada"] = g_ada[None], d[None], mo[None], vo[None]
    gs_small, d_small, mo, vo = _adamw(_pack({n: weights[n] for n in SMALL}), small_sum, _pack({n: m_in[n] for n in SMALL}),
                                       _pack({n: v_in[n] for n in SMALL}), "adamw_small")
    for dst, slab_out in ((grad_out, gs_small), (delta, d_small), (new_m, mo), (new_v, vo)):
        dst.update(_unpack(slab_out, small_shapes))
    hv_in = reduce_halves(g_in, gr_in, th_in, ot_in, after=d)
    adamw(g_in, share(g_in, hv_in, 17), after=d)

    return (loss, grad_x[None], *[grad_out[n] for n in ORDER], *[delta[n] for n in ORDER],
            *[new_m[n] for n in ORDER], *[new_v[n] for n in ORDER])
```

```python
import functools
import math

import jax
import jax.numpy as jnp
from jax import lax
from jax.experimental import pallas as pl
from jax.experimental.pallas import tpu as pltpu
from jax.experimental.pallas import tpu_sc as plsc

F32 = jnp.float32
MXU_DTYPE = jnp.bfloat16
WIRE_DTYPE = jnp.bfloat16
EPS = 1e-6
LANE = 128
V7X_VMEM_LIMIT = 56 * 1024 * 1024
MESH = pl.DeviceIdType.MESH
N_CHIPS = 4
FLIPS = (2, 1, 3)

ADAM_LR = 0.001
ADAM_B1 = 0.9
ADAM_B2 = 0.999
ADAM_EPS = 1e-08
ADAM_WD = 0.01
ADAM_STEP = 10


def _params(*semantics):
    return pltpu.CompilerParams(dimension_semantics=semantics or None, vmem_limit_bytes=V7X_VMEM_LIMIT)


def _tile(dim, pref, unit=LANE):
    best = None
    t = unit
    while t <= min(dim, pref):
        if dim % t == 0:
            best = t
        t += unit
    return best if best is not None else dim


def _then(first, second):
    return lax.optimization_barrier((first, second))[1]


def _sum0(v):
    return jnp.sum(v, axis=0, keepdims=True)


def _mean1(v):
    return jnp.mean(v, axis=-1, keepdims=True)


def _gelu(x):
    return 0.5 * x * (1.0 + lax.erf(x * (1.0 / math.sqrt(2.0))))


def _gelu_grad(x):
    cdf = 0.5 * (1.0 + lax.erf(x * (1.0 / math.sqrt(2.0))))
    return cdf + x * jnp.exp(-0.5 * x * x) * (1.0 / math.sqrt(2.0 * math.pi))


def _dot(a, b, dims):
    return lax.dot_general(a, b, (dims, ((), ())), preferred_element_type=F32)


NN = ((1,), (0,))
NT = ((1,), (1,))
TN = ((0,), (0,))


def _mm(kind, a, b, out_dtypes, name, tm=2048, tn=512, extras=(), epilogue=None):
    if kind == "nn":
        (M, K), N = a.shape, b.shape[1]
    elif kind == "nt":
        (M, K), N = a.shape, b.shape[0]
    else:
        (K, M), N = a.shape, b.shape[1]
    tm, tn = _tile(M, tm), _tile(N, tn)
    a_spec = pl.BlockSpec((K, tm), lambda i, j: (0, i)) if kind == "tn" else pl.BlockSpec((tm, K), lambda i, j: (i, 0))
    b_spec = pl.BlockSpec((tn, K), lambda i, j: (j, 0)) if kind == "nt" else pl.BlockSpec((K, tn), lambda i, j: (0, j))
    mn_spec = pl.BlockSpec((tm, tn), lambda i, j: (i, j))
    dims = {"nn": NN, "nt": NT, "tn": TN}[kind]
    n_extra = len(extras)

    n_chunks = 1 if epilogue is None or kind == "tn" else max(1, tm // 512)
    rows_per = tm // n_chunks

    def body(a_ref, b_ref, *rest):
        for r in range(n_chunks):
            rows = slice(r * rows_per, (r + 1) * rows_per)
            acc = _dot(a_ref[...] if n_chunks == 1 else a_ref[rows, :], b_ref[...], dims)
            res = (acc,) if epilogue is None else epilogue(acc, *[e[rows, :] for e in rest[:n_extra]])
            for o_ref, val in zip(rest[n_extra:], res):
                o_ref[rows, :] = val.astype(o_ref.dtype)

    outs = pl.pallas_call(
        body, name=name, grid=(M // tm, N // tn),
        in_specs=[a_spec, b_spec] + [mn_spec] * n_extra,
        out_specs=[mn_spec] * len(out_dtypes),
        out_shape=[jax.ShapeDtypeStruct((M, N), d) for d in out_dtypes],
        compiler_params=_params("parallel", "arbitrary"),
    )(a, b, *extras)
    return outs


def _mm_ktiled(kind, pairs, name, tm=2048, tn=1024, tk=1408):
    a0, b0 = pairs[0]
    M, K = a0.shape
    N = b0.shape[1] if kind == "nn" else b0.shape[0]
    tm, tn, tk = _tile(M, tm), _tile(N, tn), _tile(K, tk)
    a_spec = pl.BlockSpec((tm, tk), lambda i, j, k: (i, k))
    b_spec = pl.BlockSpec((tk, tn), lambda i, j, k: (k, j)) if kind == "nn" else pl.BlockSpec((tn, tk), lambda i, j, k: (j, k))
    dims = NN if kind == "nn" else NT
    n_pairs = len(pairs)

    def body(*refs):
        o_ref = refs[2 * n_pairs]
        acc = _dot(refs[0][...], refs[1][...], dims)
        for p in range(1, n_pairs):
            acc = acc + _dot(refs[2 * p][...], refs[2 * p + 1][...], dims)

        @pl.when(pl.program_id(2) == 0)
        def _():
            o_ref[...] = acc

        @pl.when(pl.program_id(2) != 0)
        def _():
            o_ref[...] += acc

    return pl.pallas_call(
        body, name=name, grid=(M // tm, N // tn, K // tk),
        in_specs=[a_spec, b_spec] * n_pairs,
        out_specs=pl.BlockSpec((tm, tn), lambda i, j, k: (i, j)),
        out_shape=jax.ShapeDtypeStruct((M, N), F32),
        compiler_params=_params("parallel", "parallel", "arbitrary"),
    )(*[x for pair in pairs for x in pair])


def _gate_up(h, wg, wu, name):
    (M, K), N = h.shape, wg.shape[1]
    tm, tn = _tile(M, 2048), _tile(N, 512)

    n_chunks = max(1, tm // 512)
    rows_per = tm // n_chunks

    def body(h_ref, wg_ref, wu_ref, ag_ref, au_ref, f_ref):
        for r in range(n_chunks):
            rows = slice(r * rows_per, (r + 1) * rows_per)
            hv = h_ref[rows, :]
            ag = _dot(hv, wg_ref[...], NN)
            au = _dot(hv, wu_ref[...], NN)
            ag_ref[rows, :] = ag.astype(ag_ref.dtype)
            au_ref[rows, :] = au.astype(au_ref.dtype)
            f_ref[rows, :] = (ag * jax.nn.sigmoid(ag) * au).astype(f_ref.dtype)

    w_spec = pl.BlockSpec((K, tn), lambda i, j: (0, j))
    mn_spec = pl.BlockSpec((tm, tn), lambda i, j: (i, j))
    return pl.pallas_call(
        body, name=name, grid=(M // tm, N // tn),
        in_specs=[pl.BlockSpec((tm, K), lambda i, j: (i, 0)), w_spec, w_spec],
        out_specs=[mn_spec] * 3,
        out_shape=[jax.ShapeDtypeStruct((M, N), MXU_DTYPE)] * 3,
        compiler_params=_params("parallel", "arbitrary"),
    )(h, wg, wu)


def _swiglu_bwd_epilogue(dfin, ag, au):
    ag, au = ag.astype(F32), au.astype(F32)
    sg = jax.nn.sigmoid(ag)
    d_au = dfin * (ag * sg)
    d_ag = dfin * au * (sg * (1.0 + ag * (1.0 - sg)))
    return d_ag, d_au


def _row_specs(ts, width):
    return pl.BlockSpec((ts, width), lambda i: (i, 0)), pl.BlockSpec((1, width), lambda i: (0, 0))


def _cast_into_full(place, shard, g, name):
    R, C = shard.shape
    tr = _tile(R, 256, 16)
    n_blk = R // tr
    out_map = (lambda i, p: (i, p[0])) if g.by_cols else (lambda i, p: (p[0] * n_blk + i, 0))

    def body(p_ref, a_ref, o_ref):
        o_ref[...] = a_ref[...].astype(o_ref.dtype)

    return pl.pallas_call(
        body, name=name,
        grid_spec=pltpu.PrefetchScalarGridSpec(
            num_scalar_prefetch=1, grid=(n_blk,),
            in_specs=[pl.BlockSpec((tr, C), lambda i, p: (i, 0))],
            out_specs=pl.BlockSpec((tr, C), out_map)),
        out_shape=jax.ShapeDtypeStruct(g.full, WIRE_DTYPE),
        compiler_params=_params("arbitrary"),
    )(place, shard)


def _norm_mod(x, g, scale, shift, name):
    S, D = x.shape
    ts = _tile(S, 256, 16)
    tile, vec = _row_specs(ts, D)

    def body(x_ref, g_ref, sc_ref, sh_ref, h_ref):
        xv = x_ref[...]
        r = lax.rsqrt(_mean1(xv * xv) + EPS)
        h_ref[...] = ((xv * r) * g_ref[...] * (1.0 + sc_ref[...]) + sh_ref[...]).astype(h_ref.dtype)

    return pl.pallas_call(body, name=name, grid=(S // ts,), in_specs=[tile, vec, vec, vec], out_specs=tile,
                          out_shape=jax.ShapeDtypeStruct((S, D), MXU_DTYPE), compiler_params=_params("parallel"))(x, g, scale, shift)


def _residual_norm_mod(x, attn, gate, g, scale, shift, name):
    S, D = x.shape
    ts = _tile(S, 256, 16)
    tile, vec = _row_specs(ts, D)

    def body(x_ref, a_ref, gate_ref, g_ref, sc_ref, sh_ref, x1_ref, h_ref):
        x1 = x_ref[...] + gate_ref[...] * a_ref[...]
        x1_ref[...] = x1
        r = lax.rsqrt(_mean1(x1 * x1) + EPS)
        h_ref[...] = ((x1 * r) * g_ref[...] * (1.0 + sc_ref[...]) + sh_ref[...]).astype(h_ref.dtype)

    return pl.pallas_call(body, name=name, grid=(S // ts,), in_specs=[tile, tile, vec, vec, vec, vec],
                          out_specs=[tile, tile],
                          out_shape=[jax.ShapeDtypeStruct((S, D), F32), jax.ShapeDtypeStruct((S, D), MXU_DTYPE)],
                          compiler_params=_params("parallel"))(x, attn, gate, g, scale, shift)


def _final_loss_bwd(x1, f, gate2, final_g, target, name):
    S, D = x1.shape
    ts = _tile(S, 256, 16)
    tile, vec = _row_specs(ts, D)
    loss_spec = pl.BlockSpec((1, LANE), lambda i: (0, 0))

    def body(x1_ref, f_ref, gate_ref, g_ref, t_ref, dx2_ref, df_ref, dgate_ref, dg_ref, loss_ref):
        @pl.when(pl.program_id(0) == 0)
        def _():
            dgate_ref[...] = jnp.zeros_like(dgate_ref)
            dg_ref[...] = jnp.zeros_like(dg_ref)
            loss_ref[...] = jnp.zeros_like(loss_ref)

        fv, gate, g = f_ref[...], gate_ref[...], g_ref[...]
        x2 = x1_ref[...] + gate * fv
        r = lax.rsqrt(_mean1(x2 * x2) + EPS)
        xn = x2 * r
        err = xn * g - t_ref[...]
        loss_ref[...] += jnp.broadcast_to(0.5 * _sum0(_mean1(err * err)), loss_ref.shape)
        dy = err * (1.0 / D)
        dg_ref[...] += _sum0(dy * xn)
        dxn = dy * g
        dx2 = r * (dxn - xn * _mean1(dxn * xn))
        dx2_ref[...] = dx2
        dgate_ref[...] += _sum0(dx2 * fv)
        df_ref[...] = (dx2 * gate).astype(df_ref.dtype)

    return pl.pallas_call(
        body, name=name, grid=(S // ts,), in_specs=[tile, tile, vec, vec, tile],
        out_specs=[tile, tile, vec, vec, loss_spec],
        out_shape=[jax.ShapeDtypeStruct((S, D), F32), jax.ShapeDtypeStruct((S, D), MXU_DTYPE),
                   jax.ShapeDtypeStruct((1, D), F32), jax.ShapeDtypeStruct((1, D), F32),
                   jax.ShapeDtypeStruct((1, LANE), F32)],
        compiler_params=_params("arbitrary"),
    )(x1, f, gate2, final_g, target)


def _norm_mod_bwd(dh, xin, dres, g, scale, name, branch=None, gate=None):
    S, D = xin.shape
    ts = _tile(S, 256, 16)
    tile, vec = _row_specs(ts, D)
    with_gate = branch is not None

    def body(*refs):
        if with_gate:
            dh_ref, x_ref, dres_ref, g_ref, sc_ref, br_ref, gate_ref, dx_ref, dshift_ref, dscale_ref, dg_ref, dgate_ref, dbr_ref = refs
            accs = (dshift_ref, dscale_ref, dg_ref, dgate_ref)
        else:
            dh_ref, x_ref, dres_ref, g_ref, sc_ref, dx_ref, dshift_ref, dscale_ref, dg_ref = refs
            accs = (dshift_ref, dscale_ref, dg_ref)

        @pl.when(pl.program_id(0) == 0)
        def _():
            for acc in accs:
                acc[...] = jnp.zeros_like(acc)

        dh_v, xv, g_v = dh_ref[...], x_ref[...], g_ref[...]
        one_sc = 1.0 + sc_ref[...]
        r = lax.rsqrt(_mean1(xv * xv) + EPS)
        xn = xv * r
        dshift_ref[...] += _sum0(dh_v)
        dscale_ref[...] += _sum0(dh_v * (xn * g_v))
        dg_ref[...] += _sum0(dh_v * one_sc * xn)
        dxn = dh_v * (g_v * one_sc)
        dx = dres_ref[...] + r * (dxn - xn * _mean1(dxn * xn))
        dx_ref[...] = dx
        if with_gate:
            dgate_ref[...] += _sum0(dx * br_ref[...])
            dbr_ref[...] = (dx * gate_ref[...]).astype(dbr_ref.dtype)

    ins = [dh, xin, dres, g, scale] + ([branch, gate] if with_gate else [])
    in_specs = [tile, tile, tile, vec, vec] + ([tile, vec] if with_gate else [])
    out_specs = [tile, vec, vec, vec] + ([vec, tile] if with_gate else [])
    out_shape = [jax.ShapeDtypeStruct((S, D), F32)] + [jax.ShapeDtypeStruct((1, D), F32)] * 3
    if with_gate:
        out_shape += [jax.ShapeDtypeStruct((1, D), F32), jax.ShapeDtypeStruct((S, D), MXU_DTYPE)]
    return pl.pallas_call(body, name=name, grid=(S // ts,), in_specs=in_specs, out_specs=out_specs,
                          out_shape=out_shape, compiler_params=_params("arbitrary"))(*ins)


def _causal_weights(ws_ref, wt_ref, n_g):
    row = lax.broadcasted_iota(jnp.int32, (LANE, LANE), 0)
    col = lax.broadcasted_iota(jnp.int32, (LANE, LANE), 1)
    for g in range(n_g):
        wt_ref[g] = jnp.where(col <= row, ws_ref[g], 0.0).astype(wt_ref.dtype)


def _group_layernorm(v):
    xc = v - _mean1(v)
    rstd = lax.rsqrt(_mean1(xc * xc) + EPS)
    return xc * rstd, rstd


def _gmlp_fwd(proj, v_gain, w_s, b_t, out_gain, n_g, name):
    S = proj.shape[0]
    GW = n_g * LANE

    def body(p_ref, vg_ref, ws_ref, bt_ref, og_ref, on_ref, wt_ref):
        @pl.when(pl.program_id(0) == 0)
        def _():
            _causal_weights(ws_ref, wt_ref, n_g)

        for g in range(n_g):
            cols = slice(g * LANE, (g + 1) * LANE)
            u = _gelu(p_ref[:, cols])
            v = _gelu(p_ref[:, GW + g * LANE:GW + (g + 1) * LANE])
            vhat, _ = _group_layernorm(v)
            vln = (vhat * vg_ref[:, cols]).astype(MXU_DTYPE)
            mixed = _dot(wt_ref[g], vln, NN) + bt_ref[:, g:g + 1]
            o = u * mixed
            r = lax.rsqrt(_mean1(o * o) + EPS)
            on_ref[:, cols] = (o * r * og_ref[:, cols]).astype(on_ref.dtype)

    return pl.pallas_call(
        body, name=name, grid=(S // LANE,),
        in_specs=[pl.BlockSpec((LANE, 2 * GW), lambda n: (n, 0)),
                  pl.BlockSpec((1, GW), lambda n: (0, 0)),
                  pl.BlockSpec((n_g, LANE, LANE), lambda n: (0, 0, 0)),
                  pl.BlockSpec((LANE, n_g), lambda n: (0, 0)),
                  pl.BlockSpec((1, GW), lambda n: (0, 0))],
        out_specs=pl.BlockSpec((LANE, GW), lambda n: (n, 0)),
        out_shape=jax.ShapeDtypeStruct((S, GW), MXU_DTYPE),
        scratch_shapes=[pltpu.VMEM((n_g, LANE, LANE), MXU_DTYPE)],
        compiler_params=_params("arbitrary"),
    )(proj, v_gain, w_s, b_t, out_gain)


def _gmlp_bwd(proj, d_on, v_gain, w_s, b_t, out_gain, n_g, name):
    S = proj.shape[0]
    GW = n_g * LANE

    def body(p_ref, dn_ref, vg_ref, ws_ref, bt_ref, og_ref, dp_ref, dws_ref, dbt_ref, dvg_ref, dog_ref, wt_ref):
        @pl.when(pl.program_id(0) == 0)
        def _():
            _causal_weights(ws_ref, wt_ref, n_g)
            dws_ref[...] = jnp.zeros_like(dws_ref)
            dbt_ref[...] = jnp.zeros_like(dbt_ref)
            dvg_ref[...] = jnp.zeros_like(dvg_ref)
            dog_ref[...] = jnp.zeros_like(dog_ref)

        row = lax.broadcasted_iota(jnp.int32, (LANE, LANE), 0)
        col = lax.broadcasted_iota(jnp.int32, (LANE, LANE), 1)
        for g in range(n_g):
            cols = slice(g * LANE, (g + 1) * LANE)
            vcols = slice(GW + g * LANE, GW + (g + 1) * LANE)
            pu, pv = p_ref[:, cols], p_ref[:, vcols]
            u, v = _gelu(pu), _gelu(pv)
            vhat, rstd = _group_layernorm(v)
            gain = vg_ref[:, cols]
            vln = (vhat * gain).astype(MXU_DTYPE)
            mixed = _dot(wt_ref[g], vln, NN) + bt_ref[:, g:g + 1]
            o = u * mixed
            r = lax.rsqrt(_mean1(o * o) + EPS)
            oh = o * r
            dn = dn_ref[:, cols]
            dog_ref[:, cols] += _sum0(dn * oh)
            dhn = dn * og_ref[:, cols]
            d_o = r * (dhn - oh * _mean1(dhn * oh))
            du = d_o * mixed
            dmix = d_o * u
            dbt_ref[:, g:g + 1] += jnp.sum(dmix, axis=1, keepdims=True)
            dmix_b = dmix.astype(MXU_DTYPE)
            dws_ref[g] += jnp.where(col <= row, _dot(dmix_b, vln, NT), 0.0)
            dvln = _dot(wt_ref[g], dmix_b, TN)
            dvg_ref[:, cols] += _sum0(dvln * vhat)
            dxh = dvln * gain
            dv = rstd * (dxh - _mean1(dxh) - vhat * _mean1(dxh * vhat))
            dp_ref[:, cols] = (du * _gelu_grad(pu)).astype(dp_ref.dtype)
            dp_ref[:, vcols] = (dv * _gelu_grad(pv)).astype(dp_ref.dtype)

    return pl.pallas_call(
        body, name=name, grid=(S // LANE,),
        in_specs=[pl.BlockSpec((LANE, 2 * GW), lambda n: (n, 0)),
                  pl.BlockSpec((LANE, GW), lambda n: (n, 0)),
                  pl.BlockSpec((1, GW), lambda n: (0, 0)),
                  pl.BlockSpec((n_g, LANE, LANE), lambda n: (0, 0, 0)),
                  pl.BlockSpec((LANE, n_g), lambda n: (0, 0)),
                  pl.BlockSpec((1, GW), lambda n: (0, 0))],
        out_specs=[pl.BlockSpec((LANE, 2 * GW), lambda n: (n, 0)),
                   pl.BlockSpec((n_g, LANE, LANE), lambda n: (0, 0, 0)),
                   pl.BlockSpec((LANE, n_g), lambda n: (0, 0)),
                   pl.BlockSpec((1, GW), lambda n: (0, 0)),
                   pl.BlockSpec((1, GW), lambda n: (0, 0))],
        out_shape=[jax.ShapeDtypeStruct((S, 2 * GW), MXU_DTYPE),
                   jax.ShapeDtypeStruct((n_g, LANE, LANE), F32),
                   jax.ShapeDtypeStruct((LANE, n_g), F32),
                   jax.ShapeDtypeStruct((1, GW), F32),
                   jax.ShapeDtypeStruct((1, GW), F32)],
        scratch_shapes=[pltpu.VMEM((n_g, LANE, LANE), MXU_DTYPE)],
        compiler_params=_params("arbitrary"),
    )(proj, d_on, v_gain, w_s, b_t, out_gain)


def _tri_sum(v, tri, exact=True):
    hi = v.astype(MXU_DTYPE)
    if not exact:
        return _dot(hi, tri, NN)
    lo = (v - hi.astype(F32)).astype(MXU_DTYPE)
    return _dot(hi, tri, NN) + _dot(lo, tri, NN)


def _log_sigmoids(z):
    sp = jnp.log(1.0 + jnp.exp(-jnp.abs(z)))
    return jnp.minimum(z, 0.0) - sp, jnp.minimum(-z, 0.0) - sp


def _rows(i, size):
    return pl.ds(pl.multiple_of(i * size, size), size)


SB_QUERY_TILE = 1024
SB_KEY_TILE = 256


def _sb_tiles(S):
    tq = _tile(S, SB_QUERY_TILE)
    tk = _tile(tq, SB_KEY_TILE)
    assert (tq // tk) % 2 == 0, "the key sweep takes two blocks a pass"
    return tq, tk, S // tq, tq // tk


def _triangle(n, keep):
    row = lax.broadcasted_iota(jnp.int32, (n, n), 0)
    col = lax.broadcasted_iota(jnp.int32, (n, n), 1)
    return jnp.where(keep(row, col), 1.0, 0.0).astype(MXU_DTYPE)


def _strictly_before(tq, tk, key_offset):
    row = lax.broadcasted_iota(jnp.int32, (tq, tk), 0)
    col = lax.broadcasted_iota(jnp.int32, (tq, tk), 1)
    return col + key_offset < row


def _sb_specs(S, n_g, n_h):
    base = 2 * n_g
    q_spec = pl.BlockSpec((S, LANE), lambda h: (0, base + h))
    k_spec = pl.BlockSpec((S, LANE), lambda h: (0, base + n_h + h))
    v_spec = pl.BlockSpec((S, LANE), lambda h: (0, base + 2 * n_h + h))
    gain_spec = pl.BlockSpec((1, LANE), lambda h: (0, n_g + h))
    head_spec = pl.BlockSpec((S, LANE), lambda h: (0, h))
    return q_spec, k_spec, v_spec, gain_spec, head_spec


def _sb_fwd(proj, out_gain, n_g, n_h, name):
    S = proj.shape[0]
    TQ, TK, NQ, KPQ = _sb_tiles(S)
    scale = LANE ** -0.5
    q_spec, k_spec, v_spec, gain_spec, head_spec = _sb_specs(S, n_g, n_h)

    def body(q_ref, k_ref, v_ref, og_ref, o_ref, on_ref, ls_ref, qb, kb, vb):
        qb[...] = q_ref[...].astype(MXU_DTYPE)
        kb[...] = k_ref[...].astype(MXU_DTYPE)
        vb[...] = v_ref[...].astype(MXU_DTYPE)
        after = _triangle(TK, lambda r, c: r > c)

        def block(qi, j, ctail, acc, key_offset):
            skip = key_offset or 0
            z = _dot(qi[skip:], kb[_rows(j, TK), :], NT) * scale
            lb, l1m = _log_sigmoids(z)
            if key_offset is not None:
                strict = _strictly_before(TQ - skip, TK, 0)
                l1m = jnp.where(strict, l1m, 0.0)
            a = jnp.exp(lb + ctail[skip:] + _tri_sum(l1m, after))
            if key_offset is not None:
                a = jnp.where(strict, a, 0.0)
            acc_new = acc[skip:] + _dot(a.astype(MXU_DTYPE), vb[_rows(j, TK), :], NN)
            ctail_new = ctail[skip:] + jnp.sum(l1m, axis=1, keepdims=True)
            if skip:
                ctail_new = jnp.concatenate([ctail[:skip], ctail_new], axis=0)
                acc_new = jnp.concatenate([acc[:skip], acc_new], axis=0)
            return ctail_new, acc_new

        def q_loop(i, carry):
            qi = qb[_rows(i, TQ), :]
            state = (jnp.zeros((TQ, 1), F32), jnp.zeros((TQ, LANE), F32))
            for d in reversed(range(KPQ)):
                state = block(qi, i * KPQ + d, state[0], state[1], d * TK)
            def pair(jj, st):
                st = block(qi, i * KPQ - 1 - 2 * jj, st[0], st[1], None)
                return block(qi, i * KPQ - 2 - 2 * jj, st[0], st[1], None)

            ctail, acc = lax.fori_loop(0, i * (KPQ // 2), pair, state)
            ls_ref[_rows(i, TQ), :] = jnp.broadcast_to(ctail, (TQ, LANE))
            o_ref[_rows(i, TQ), :] = acc
            r = lax.rsqrt(_mean1(acc * acc) + EPS)
            on_ref[_rows(i, TQ), :] = (acc * r * og_ref[...]).astype(on_ref.dtype)
            return carry

        lax.fori_loop(0, NQ, q_loop, 0)

    return pl.pallas_call(
        body, name=name, grid=(n_h,),
        in_specs=[q_spec, k_spec, v_spec, gain_spec],
        out_specs=[head_spec, head_spec, head_spec],
        out_shape=[jax.ShapeDtypeStruct((S, n_h * LANE), F32), jax.ShapeDtypeStruct((S, n_h * LANE), MXU_DTYPE),
                   jax.ShapeDtypeStruct((S, n_h * LANE), F32)],
        scratch_shapes=[pltpu.VMEM((S, LANE), MXU_DTYPE)] * 3,
        compiler_params=_params("parallel"),
    )(proj, proj, proj, out_gain)


def _sb_bwd(proj, o_sb, l_sum, d_on, out_gain, n_g, n_h, name):
    S = proj.shape[0]
    TQ, TK, NQ, KPQ = _sb_tiles(S)
    scale = LANE ** -0.5
    q_spec, k_spec, v_spec, gain_spec, head_spec = _sb_specs(S, n_g, n_h)
    dn_spec = pl.BlockSpec((S, LANE), lambda h: (0, n_g + h))
    dgain_spec = pl.BlockSpec((1, LANE), lambda h: (0, h))

    def body(q_ref, k_ref, v_ref, o_ref, ls_ref, dn_ref, og_ref, dq_ref, dk_ref, dv_ref, dog_ref,
             qb, kb, vb, dob, dk_acc, dv_acc):
        qb[...] = q_ref[...].astype(MXU_DTYPE)
        kb[...] = k_ref[...].astype(MXU_DTYPE)
        vb[...] = v_ref[...].astype(MXU_DTYPE)
        o, dn = o_ref[...], dn_ref[...]
        r = lax.rsqrt(_mean1(o * o) + EPS)
        oh = o * r
        dog_ref[...] = _sum0(dn * oh)
        dhn = dn * og_ref[...]
        dob[...] = (r * (dhn - oh * _mean1(dhn * oh))).astype(MXU_DTYPE)
        dk_acc[...] = jnp.zeros_like(dk_acc)
        dv_acc[...] = jnp.zeros_like(dv_acc)

        up_to = _triangle(TK, lambda r, c: r <= c)
        before = _triangle(TK, lambda r, c: r < c)

        def block(qi, doi, ltot, j, cl, cdl, dq, key_offset):
            skip = key_offset or 0
            q_in, do_in = qi[skip:], doi[skip:]
            kj, vj = kb[_rows(j, TK), :], vb[_rows(j, TK), :]
            z = _dot(q_in, kj, NT) * scale
            lb, l1m = _log_sigmoids(z)
            if key_offset is not None:
                strict = _strictly_before(TQ - skip, TK, 0)
                l1m = jnp.where(strict, l1m, 0.0)
            a = jnp.exp(lb + (ltot[skip:] - (cl[skip:] + _tri_sum(l1m, up_to))))
            if key_offset is not None:
                a = jnp.where(strict, a, 0.0)
            dl = _dot(do_in, vj, NT) * a
            d_l1m = cdl[skip:] + _tri_sum(dl, before, exact=False)
            beta = jnp.exp(lb)
            dz = dl * (1.0 - beta) - beta * d_l1m
            if key_offset is not None:
                dz = jnp.where(strict, dz, 0.0)
            dzs = (dz * scale).astype(MXU_DTYPE)
            dk_acc[_rows(j, TK), :] += _dot(dzs, q_in, TN)
            dv_acc[_rows(j, TK), :] += _dot(a.astype(MXU_DTYPE), do_in, TN)
            cl_new = cl[skip:] + jnp.sum(l1m, axis=1, keepdims=True)
            cdl_new = cdl[skip:] + jnp.sum(dl, axis=1, keepdims=True)
            dq_new = dq[skip:] + _dot(dzs, kj, NN)
            if skip:
                cl_new = jnp.concatenate([cl[:skip], cl_new], axis=0)
                cdl_new = jnp.concatenate([cdl[:skip], cdl_new], axis=0)
                dq_new = jnp.concatenate([dq[:skip], dq_new], axis=0)
            return cl_new, cdl_new, dq_new

        def q_loop(i, carry):
            qi, doi = qb[_rows(i, TQ), :], dob[_rows(i, TQ), :]
            ltot = ls_ref[_rows(i, TQ), :][:, :1]
            zero_col = jnp.zeros((TQ, 1), F32)
            def pair(jj, st):
                st = block(qi, doi, ltot, 2 * jj, st[0], st[1], st[2], None)
                return block(qi, doi, ltot, 2 * jj + 1, st[0], st[1], st[2], None)

            state = lax.fori_loop(0, i * (KPQ // 2), pair, (zero_col, zero_col, jnp.zeros((TQ, LANE), F32)))
            for d in range(KPQ):
                state = block(qi, doi, ltot, i * KPQ + d, state[0], state[1], state[2], d * TK)
            dq_ref[_rows(i, TQ), :] = state[2].astype(dq_ref.dtype)
            return carry

        lax.fori_loop(0, NQ, q_loop, 0)
        dk_ref[...] = dk_acc[...].astype(dk_ref.dtype)
        dv_ref[...] = dv_acc[...].astype(dv_ref.dtype)

    W = n_h * LANE
    return pl.pallas_call(
        body, name=name, grid=(n_h,),
        in_specs=[q_spec, k_spec, v_spec, head_spec, head_spec, dn_spec, gain_spec],
        out_specs=[head_spec, head_spec, head_spec, dgain_spec],
        out_shape=[jax.ShapeDtypeStruct((S, W), MXU_DTYPE)] * 3 + [jax.ShapeDtypeStruct((1, W), F32)],
        scratch_shapes=[pltpu.VMEM((S, LANE), MXU_DTYPE)] * 4 + [pltpu.VMEM((S, LANE), F32)] * 2,
        compiler_params=_params("parallel"),
    )(proj, proj, proj, o_sb, l_sum, d_on, out_gain)


def _mod_part(c_all, w_ada, b_ada_cols, name):
    B, K = c_all.shape
    N = w_ada.shape[1]
    tn = _tile(N, 512)

    def body(c_ref, w_ref, b_ref, o_ref):
        cv = c_ref[...]
        ca = (cv * jax.nn.sigmoid(cv)).astype(MXU_DTYPE)
        o_ref[...] = _dot(ca, w_ref[...].astype(MXU_DTYPE), NN) + b_ref[...]

    return pl.pallas_call(
        body, name=name, grid=(N // tn,),
        in_specs=[pl.BlockSpec((B, K), lambda j: (0, 0)), pl.BlockSpec((K, tn), lambda j: (0, j)),
                  pl.BlockSpec((1, tn), lambda j: (0, j))],
        out_specs=pl.BlockSpec((B, tn), lambda j: (0, j)),
        out_shape=jax.ShapeDtypeStruct((B, N), F32), compiler_params=_params("parallel"))(c_all, w_ada, b_ada_cols)


def _adamw_math(w, g, m, v):
    m = ADAM_B1 * m + (1.0 - ADAM_B1) * g
    v = ADAM_B2 * v + (1.0 - ADAM_B2) * (g * g)
    m_hat = m / (1.0 - ADAM_B1 ** ADAM_STEP)
    v_hat = v / (1.0 - ADAM_B2 ** ADAM_STEP)
    delta = -ADAM_LR * (m_hat / (jnp.sqrt(v_hat) + ADAM_EPS) + ADAM_WD * w)
    return delta, m, v


def _adamw(w, g, m, v, name):
    R, C = w.shape
    tr = _tile(R, max(8, (1 << 19) // C), 8)
    spec = pl.BlockSpec((tr, C), lambda i: (i, 0))

    def body(w_ref, g_ref, m_ref, v_ref, go_ref, d_ref, mo_ref, vo_ref):
        g = g_ref[...]
        go_ref[...] = g
        d_ref[...], mo_ref[...], vo_ref[...] = _adamw_math(w_ref[...], g, m_ref[...], v_ref[...])

    return pl.pallas_call(body, name=name, grid=(R // tr,), in_specs=[spec] * 4, out_specs=[spec] * 4,
                          out_shape=[jax.ShapeDtypeStruct((R, C), F32)] * 4, compiler_params=_params("parallel"))(w, g, m, v)


def _adamw_ada(c_all, dmod_cols, w, m, v, name):
    K, N = w.shape
    B = c_all.shape[0]
    tk, tn = _tile(K, 512), _tile(N, 1024)
    spec = pl.BlockSpec((tk, tn), lambda i, j: (i, j))

    def body(c_ref, dm_ref, w_ref, m_ref, v_ref, g_ref, d_ref, mo_ref, vo_ref):
        cv = c_ref[...]
        ca = (cv * jax.nn.sigmoid(cv)).astype(MXU_DTYPE)
        g = _dot(ca, dm_ref[...].astype(MXU_DTYPE), TN)
        g_ref[...] = g
        d_ref[...], mo_ref[...], vo_ref[...] = _adamw_math(w_ref[...], g, m_ref[...], v_ref[...])

    return pl.pallas_call(
        body, name=name, grid=(K // tk, N // tn),
        in_specs=[pl.BlockSpec((B, tk), lambda i, j: (0, i)), pl.BlockSpec((B, tn), lambda i, j: (0, j)), spec, spec, spec],
        out_specs=[spec] * 4, out_shape=[jax.ShapeDtypeStruct((K, N), F32)] * 4,
        compiler_params=_params("parallel", "parallel"))(c_all, dmod_cols, w, m, v)


def _sum_devices(gathered, n_dev, name):
    R = gathered.shape[0] // n_dev
    C = gathered.shape[1]
    tr = _tile(R, 512, 8)
    n_blk = R // tr

    def body(*refs):
        acc = refs[0][...]
        for r in refs[1:n_dev]:
            acc = acc + r[...]
        refs[n_dev][...] = acc

    in_specs = [pl.BlockSpec((tr, C), functools.partial(lambda i, d: (d * n_blk + i, 0), d=d)) for d in range(n_dev)]
    return pl.pallas_call(body, name=name, grid=(n_blk,), in_specs=in_specs,
                          out_specs=pl.BlockSpec((tr, C), lambda i: (i, 0)),
                          out_shape=jax.ShapeDtypeStruct((R, C), F32), compiler_params=_params("parallel"))(*([gathered] * n_dev))


def _place():
    x, y, c = lax.axis_index("x"), lax.axis_index("y"), lax.axis_index("c")
    return x, y, c


def _allgather8(blk, name):
    m_per, n = blk.shape

    def body(x_ref, out_ref, send_sems, recv_sems, local_sem):
        x, y, c = _place()
        me, sibling = (x, y, c), (x, y, 1 - c)
        chips = [(1 - x, y), (x, 1 - y), (1 - x, 1 - y)]

        def rows(px, py, pc):
            return out_ref.at[pl.ds((4 * px + 2 * py + pc) * m_per, m_per), :]

        def copy(k, block, to, src=None):
            return pltpu.make_async_remote_copy(
                src_ref=rows(*block) if src is None else src, dst_ref=rows(*block),
                send_sem=send_sems.at[k], recv_sem=recv_sems.at[k], device_id=to, device_id_type=MESH)

        mine = pltpu.make_async_copy(x_ref, rows(*me), local_sem)
        mine.start()
        first = [copy(0, me, sibling, src=x_ref)]
        first += [copy(1 + j, me, (*chip, c), src=x_ref) for j, chip in enumerate(chips)]
        for cp in first:
            cp.start()
        passed = [copy(4 + j, (*chip, c), sibling) for j, chip in enumerate(chips)]
        for j, chip in enumerate(chips):
            copy(1 + j, (*chip, c), me).wait_recv()
            passed[j].start()
        copy(0, sibling, me).wait_recv()
        for j, chip in enumerate(chips):
            copy(4 + j, (*chip, 1 - c), me).wait_recv()
        for cp in first + passed:
            cp.wait_send()
        mine.wait()

    return pl.pallas_call(
        body, name=name,
        out_shape=jax.ShapeDtypeStruct((8 * m_per, n), blk.dtype),
        in_specs=[pl.BlockSpec(memory_space=pltpu.VMEM)],
        out_specs=pl.BlockSpec(memory_space=pltpu.VMEM),
        scratch_shapes=[pltpu.SemaphoreType.DMA((7,)), pltpu.SemaphoreType.DMA((7,)), pltpu.SemaphoreType.DMA],
        compiler_params=pltpu.CompilerParams(vmem_limit_bytes=V7X_VMEM_LIMIT),
    )(blk)


class _Sharded:
    def __init__(self, shard_shape, by_cols):
        r, c = shard_shape
        self.by_cols = by_cols
        self.full = (r, N_CHIPS * c) if by_cols else (N_CHIPS * r, c)
        self.shard = (r, c)
        self.half_rows = r // 2
        self.half = (r // 2, c)

    def shard_of(self, ref, k):
        r, c = self.shard
        return ref.at[:, pl.ds(k * c, c)] if self.by_cols else ref.at[pl.ds(k * r, r), :]

    def half_of(self, ref, k, hc):
        r, c = self.shard
        h = self.half_rows
        if self.by_cols:
            return ref.at[pl.ds(hc * h, h), pl.ds(k * c, c)]
        return ref.at[pl.ds(k * r + hc * h, h), :]

    def chunk_of(self, ref, k, hc, ch, n):
        r, c = self.shard
        h = self.half_rows
        q = h // n
        if self.by_cols:
            return ref.at[pl.ds(hc * h + ch * q, q), pl.ds(k * c, c)]
        return ref.at[pl.ds(k * r + hc * h + ch * q, q), :]

    def half_of_shard(self, ref, hc):
        return ref.at[pl.ds(hc * self.half_rows, self.half_rows), :]

    def part_of_halves(self, ref, k):
        r, c = self.shard
        h = self.half_rows
        return ref.at[:, pl.ds(k * c, c)] if self.by_cols else ref.at[pl.ds(k * h, h), :]


def _on_each_place(x, y, c, fn, by_chip=True, by_core=True):
    q = 2 * x + y
    for k in range(N_CHIPS if by_chip else 1):
        for cc in range(2 if by_core else 1):
            cond = None
            if by_chip:
                cond = q == k
            if by_core:
                cond = (c == cc) if cond is None else jnp.logical_and(cond, c == cc)
            pl.when(cond)(functools.partial(fn, k, cc))


def _chip_id(k, c):
    return (k // 2, k % 2, c)


def _handshake(peers):
    barrier = pltpu.get_barrier_semaphore()
    for peer in peers:
        pl.semaphore_signal(barrier, inc=1, device_id=peer, device_id_type=MESH)
    pl.semaphore_wait(barrier, len(peers))


def _on_sequencer(body, inputs, out_structs, n_copies, peers_of, name, collective_id, return_inputs=False):
    in_refs = [jax.new_ref(a, memory_space=pltpu.MemorySpace.HBM) for a in inputs]
    out_refs = [jax.empty_ref(s, memory_space=pltpu.MemorySpace.HBM) for s in out_structs]

    @pl.kernel(mesh=plsc.ScalarSubcoreMesh(axis_name="sequencer", num_cores=1), name=name,
               scratch_types=(pltpu.SemaphoreType.DMA((n_copies,)), pltpu.SemaphoreType.DMA((n_copies,))),
               compiler_params=pltpu.CompilerParams(collective_id=collective_id))
    def launch(send_sems, recv_sems):
        x, y, c = _place()
        _handshake(peers_of(x, y, c))
        body(in_refs, out_refs, send_sems, recv_sems, x, y, c)

    launch()
    return [r[...] for r in (in_refs if return_inputs else out_refs)]


def _sibling(x, y, c):
    return [(x, y, 1 - c)]


def _same_core_of_other_chips(x, y, c):
    return [(1 - x, y, c), (x, 1 - y, c), (1 - x, 1 - y, c)]


GATHER_CHUNKS = 4
GATHER_COPIES = 6 * GATHER_CHUNKS


def _allgather8_on_sequencer(blk, name, collective_id):
    m_per, n = blk.shape
    x, y, c = _place()
    placed = lax.dynamic_update_slice(jnp.zeros((8 * m_per, n), blk.dtype), blk, ((4 * x + 2 * y + c) * m_per, 0))

    def body(refs, _, send_sems, recv_sems, x, y, c):
        out_ref, = refs

        def at_place(k, cc):
            def rows(kk, pc):
                return out_ref.at[pl.ds((2 * kk + pc) * m_per, m_per), :]

            def copy(slot, block, to):
                return pltpu.make_async_remote_copy(src_ref=rows(*block), dst_ref=rows(*block), send_sem=send_sems.at[slot],
                                                    recv_sem=recv_sems.at[slot], device_id=to, device_id_type=MESH)

            others = [k ^ flip for flip in FLIPS]
            sends = [copy(0, (k, cc), _chip_id(k, 1 - cc))] + [copy(1 + j, (k, cc), _chip_id(kk, cc)) for j, kk in enumerate(others)]
            for cp in sends:
                cp.start()
            for j, kk in enumerate(others):
                copy(1 + j, (kk, cc), _chip_id(k, cc)).wait_recv()
                cp = copy(4 + j, (kk, cc), _chip_id(k, 1 - cc))
                cp.start()
                sends.append(cp)
            copy(0, (k, 1 - cc), _chip_id(k, cc)).wait_recv()
            for j, kk in enumerate(others):
                copy(4 + j, (kk, 1 - cc), _chip_id(k, cc)).wait_recv()
            for cp in sends:
                cp.wait_send()

        _on_each_place(x, y, c, at_place)

    def peers(x, y, c):
        return _sibling(x, y, c) + _same_core_of_other_chips(x, y, c)

    return _on_sequencer(body, [placed], [], 7, peers, name, collective_id, return_inputs=True)[0]


def _gather_weights(fulls, geoms, name, collective_id):
    n_w = len(fulls)
    n_ch, n_relay = GATHER_CHUNKS, GATHER_CHUNKS // 2
    f_refs = [jax.new_ref(f, memory_space=pltpu.MemorySpace.HBM) for f in fulls]
    FLIP_X, FLIP_Y, FLIP_BOTH = FLIPS
    TO_X, TO_Y, RELAY_TO_Y, RELAY_TO_X, ON_X, ON_Y, ON_DIAG = 0, n_ch, 2 * n_ch, 2 * n_ch + n_relay, 3 * n_ch, 4 * n_ch, 5 * n_ch

    @pl.kernel(mesh=plsc.ScalarSubcoreMesh(axis_name="sequencer", num_cores=1), name=name,
               scratch_types=(pltpu.SemaphoreType.DMA((GATHER_COPIES * n_w,)), pltpu.SemaphoreType.DMA((GATHER_COPIES * n_w,))),
               compiler_params=pltpu.CompilerParams(collective_id=collective_id))
    def launch(send_sems, recv_sems):
        x, y, c = _place()
        _handshake([(x, y, 1 - c), (1 - x, y, c), (x, 1 - y, c)])

        def at_place(k, cc):
            kx, ky, kd = k ^ FLIP_X, k ^ FLIP_Y, k ^ FLIP_BOTH
            me, sibling = _chip_id(k, cc), _chip_id(k, 1 - cc)
            started = []

            def copy(i, slot, src, dst, to, start=True):
                cp = pltpu.make_async_remote_copy(src_ref=src, dst_ref=dst, send_sem=send_sems.at[GATHER_COPIES * i + slot],
                                                  recv_sem=recv_sems.at[GATHER_COPIES * i + slot], device_id=to, device_id_type=MESH)
                if start:
                    cp.start()
                    started.append(cp)
                return cp

            def pass_on(i, slot, ref, to):
                copy(i, slot, ref, ref, to)

            def landed(i, slot, ref):
                copy(i, slot, ref, ref, me, start=False).wait_recv()

            y_order = [(n_relay + s) % n_ch for s in range(n_ch)]
            for i, (g, f_ref) in enumerate(zip(geoms, f_refs)):
                for s in range(n_ch):
                    pass_on(i, TO_X + s, g.chunk_of(f_ref, k, cc, s, n_ch), _chip_id(kx, cc))
                    pass_on(i, TO_Y + y_order[s], g.chunk_of(f_ref, k, cc, y_order[s], n_ch), _chip_id(ky, cc))
            for i, (g, f_ref) in enumerate(zip(geoms, f_refs)):
                for s in range(n_ch):
                    from_x = g.chunk_of(f_ref, kx, cc, s, n_ch)
                    landed(i, TO_X + s, from_x)
                    if s < n_relay:
                        pass_on(i, RELAY_TO_Y + s, from_x, _chip_id(ky, cc))
                    pass_on(i, ON_X + s, from_x, sibling)
                    ch = y_order[s]
                    from_y = g.chunk_of(f_ref, ky, cc, ch, n_ch)
                    landed(i, TO_Y + ch, from_y)
                    if ch >= n_relay:
                        pass_on(i, RELAY_TO_X + ch - n_relay, from_y, _chip_id(kx, cc))
                    pass_on(i, ON_Y + ch, from_y, sibling)
                for r in range(n_relay):
                    via_y = g.chunk_of(f_ref, kd, cc, r, n_ch)
                    landed(i, RELAY_TO_Y + r, via_y)
                    pass_on(i, ON_DIAG + r, via_y, sibling)
                    via_x = g.chunk_of(f_ref, kd, cc, n_relay + r, n_ch)
                    landed(i, RELAY_TO_X + r, via_x)
                    pass_on(i, ON_DIAG + n_relay + r, via_x, sibling)
            for i, (g, f_ref) in enumerate(zip(geoms, f_refs)):
                for slot, kk in ((ON_X, kx), (ON_Y, ky), (ON_DIAG, kd)):
                    for ch in range(n_ch):
                        landed(i, slot + ch, g.chunk_of(f_ref, kk, 1 - cc, ch, n_ch))
            for cp in started:
                cp.wait_send()

        _on_each_place(x, y, c, at_place)

    launch()
    return [f_ref[...] for f_ref in f_refs]


def _swap_core_halves(grads, geoms, name, collective_id):
    n_cp = sum(1 if g.by_cols else N_CHIPS for g in geoms)

    def body(g_refs, t_refs, send_sems, recv_sems, x, y, c):

        def at_place(_, cc):
            def pairs(hc):
                out = []
                for g, g_ref, t_ref in zip(geoms, g_refs, t_refs):
                    if g.by_cols:
                        out.append((g_ref.at[pl.ds(hc * g.half_rows, g.half_rows), :], t_ref))
                    else:
                        out += [(g.half_of(g_ref, k, hc), g.part_of_halves(t_ref, k)) for k in range(N_CHIPS)]
                return out

            sends = [pltpu.make_async_remote_copy(src_ref=src, dst_ref=dst, send_sem=send_sems.at[n],
                                                  recv_sem=recv_sems.at[n], device_id=(x, y, 1 - cc), device_id_type=MESH)
                     for n, (src, dst) in enumerate(pairs(1 - cc))]
            for cp in sends:
                cp.start()
            for n, (src, dst) in enumerate(pairs(cc)):
                pltpu.make_async_remote_copy(src_ref=src, dst_ref=dst, send_sem=send_sems.at[n], recv_sem=recv_sems.at[n],
                                             device_id=(x, y, cc), device_id_type=MESH).wait_recv()
            for cp in sends:
                cp.wait_send()

        _on_each_place(x, y, c, at_place, by_chip=False)

    return _on_sequencer(body, grads, [jax.ShapeDtypeStruct((g.full[0] // 2, g.full[1]), F32) for g in geoms],
                         n_cp, _sibling, name, collective_id)


def _scatter_chip_sums(sums, geoms, name, collective_id):
    def body(s_refs, r_refs, send_sems, recv_sems, x, y, c):

        def at_place(k, _):
            sends = []
            for i, (g, s_ref, r_ref) in enumerate(zip(geoms, s_refs, r_refs)):
                for j, flip in enumerate(FLIPS):
                    kk = k ^ flip
                    cp = pltpu.make_async_remote_copy(
                        src_ref=g.part_of_halves(s_ref, kk), dst_ref=r_ref.at[j], send_sem=send_sems.at[3 * i + j],
                        recv_sem=recv_sems.at[3 * i + j], device_id=(kk // 2, kk % 2, c), device_id_type=MESH)
                    cp.start()
                    sends.append(cp)
            for i, (g, s_ref, r_ref) in enumerate(zip(geoms, s_refs, r_refs)):
                for j in range(len(FLIPS)):
                    pltpu.make_async_remote_copy(
                        src_ref=g.part_of_halves(s_ref, k), dst_ref=r_ref.at[j], send_sem=send_sems.at[3 * i + j],
                        recv_sem=recv_sems.at[3 * i + j], device_id=(x, y, c), device_id_type=MESH).wait_recv()
            for cp in sends:
                cp.wait_send()

        _on_each_place(x, y, c, at_place, by_core=False)

    return _on_sequencer(body, sums, [jax.ShapeDtypeStruct((len(FLIPS),) + g.half, WIRE_DTYPE) for g in geoms],
                         len(FLIPS) * len(sums), _same_core_of_other_chips, name, collective_id)


def _share_reduced_halves(reduced, geoms, name, collective_id):
    def body(out_refs, _, send_sems, recv_sems, x, y, c):

        def at_place(_, cc):
            sends = []
            for i, (g, ref) in enumerate(zip(geoms, out_refs)):
                mine = g.half_of_shard(ref, cc)
                cp = pltpu.make_async_remote_copy(src_ref=mine, dst_ref=mine, send_sem=send_sems.at[i],
                                                  recv_sem=recv_sems.at[i], device_id=(x, y, 1 - cc), device_id_type=MESH)
                cp.start()
                sends.append(cp)
            for i, (g, ref) in enumerate(zip(geoms, out_refs)):
                theirs = g.half_of_shard(ref, 1 - cc)
                pltpu.make_async_remote_copy(src_ref=theirs, dst_ref=theirs, send_sem=send_sems.at[i],
                                             recv_sem=recv_sems.at[i], device_id=(x, y, cc), device_id_type=MESH).wait_recv()
            for cp in sends:
                cp.wait_send()

        _on_each_place(x, y, c, at_place, by_chip=False)

    return _on_sequencer(body, reduced, [], len(reduced), _sibling, name, collective_id, return_inputs=True)


def _chip_sum(place, grad, theirs, g, name):
    RH, C = theirs.shape
    h = g.half_rows
    tr = _tile(h, 256, 16)
    tc = _tile(C, 2048)
    per_half = h // tr

    if g.by_cols:
        grad_map = lambda i, j, p: (p[1] * per_half + i, j)
    else:
        grad_map = lambda i, j, p: ((i // per_half) * 2 * per_half + p[1] * per_half + i % per_half, j)

    def body(p_ref, a_ref, b_ref, o_ref):
        o_ref[...] = (a_ref[...] + b_ref[...]).astype(o_ref.dtype)

    return pl.pallas_call(
        body, name=name,
        grid_spec=pltpu.PrefetchScalarGridSpec(
            num_scalar_prefetch=1, grid=(RH // tr, C // tc),
            in_specs=[pl.BlockSpec((tr, tc), grad_map), pl.BlockSpec((tr, tc), lambda i, j, p: (i, j))],
            out_specs=pl.BlockSpec((tr, tc), lambda i, j, p: (i, j))),
        out_shape=jax.ShapeDtypeStruct((RH, C), WIRE_DTYPE),
        compiler_params=_params("parallel", "parallel"),
    )(place, grad, theirs)


def _reduce_half(place, grad, theirs, others, g, name):
    h, wc = g.half
    tr = _tile(h, 256, 16)
    per_half = h // tr
    if g.by_cols:
        tc = wc
        grad_map = lambda i, p: (p[1] * per_half + i, p[0])
        theirs_map = lambda i, p: (i, p[0])
    else:
        tc = wc
        grad_map = lambda i, p: (p[0] * 2 * per_half + p[1] * per_half + i, 0)
        theirs_map = lambda i, p: (p[0] * per_half + i, 0)

    def body(p_ref, a_ref, b_ref, o0_ref, o1_ref, o2_ref, out_ref):
        acc = a_ref[...] + b_ref[...]
        for o_ref in (o0_ref, o1_ref, o2_ref):
            acc = acc + o_ref[...].astype(F32)
        out_ref[...] = acc

    other_specs = [pl.BlockSpec((None, tr, tc), functools.partial(lambda i, p, j: (j, i, 0), j=j)) for j in range(len(FLIPS))]
    return pl.pallas_call(
        body, name=name,
        grid_spec=pltpu.PrefetchScalarGridSpec(
            num_scalar_prefetch=1, grid=(per_half,),
            in_specs=[pl.BlockSpec((tr, tc), grad_map), pl.BlockSpec((tr, tc), theirs_map)] + other_specs,
            out_specs=pl.BlockSpec((tr, tc), lambda i, p: (p[1] * per_half + i, 0))),
        out_shape=jax.ShapeDtypeStruct(g.shard, F32),
        compiler_params=_params("arbitrary"),
    )(place, grad, theirs, others, others, others)


SMALL = ("b_ada", "norm1_g", "v_norm_g", "w_spatial", "b_spatial", "out_norm_g", "norm2_g", "final_g")
BIG = ("w_in", "w_out", "w_gate", "w_up", "w_down")
BY_COLS = {"w_in": True, "w_out": False, "w_gate": True, "w_up": True, "w_down": False}
ORDER = ("w_ada", "b_ada", "norm1_g", "w_in", "v_norm_g", "w_spatial", "b_spatial", "out_norm_g", "w_out",
         "norm2_g", "w_gate", "w_up", "w_down", "final_g")


def _pack(parts):
    return jnp.concatenate([parts[n].reshape(-1) for n in SMALL]).reshape(-1, LANE)


def _unpack(slab, shapes):
    flat = slab.reshape(-1)
    out, at = {}, 0
    for n in SMALL:
        size = math.prod(shapes[n])
        out[n] = flat[at:at + size].reshape(shapes[n])
        at += size
    return out


def kernel(x, c, w_ada, b_ada, norm1_g, w_in, v_norm_g, w_spatial, b_spatial, out_norm_g, w_out, norm2_g, w_gate, w_up, w_down, final_g, loss_target, m_w_ada, m_b_ada, m_norm1_g, m_w_in, m_v_norm_g, m_w_spatial, m_b_spatial, m_out_norm_g, m_w_out, m_norm2_g, m_w_gate, m_w_up, m_w_down, m_final_g, v_w_ada, v_b_ada, v_norm1_g, v_w_in, v_v_norm_g, v_w_spatial, v_b_spatial, v_out_norm_g, v_w_out, v_norm2_g, v_w_gate, v_w_up, v_w_down, v_final_g):
    weights = dict(w_ada=w_ada, b_ada=b_ada, norm1_g=norm1_g, w_in=w_in, v_norm_g=v_norm_g, w_spatial=w_spatial,
                   b_spatial=b_spatial, out_norm_g=out_norm_g, w_out=w_out, norm2_g=norm2_g, w_gate=w_gate, w_up=w_up,
                   w_down=w_down, final_g=final_g)
    m_in = dict(w_ada=m_w_ada, b_ada=m_b_ada, norm1_g=m_norm1_g, w_in=m_w_in, v_norm_g=m_v_norm_g, w_spatial=m_w_spatial,
                b_spatial=m_b_spatial, out_norm_g=m_out_norm_g, w_out=m_w_out, norm2_g=m_norm2_g, w_gate=m_w_gate,
                w_up=m_w_up, w_down=m_w_down, final_g=m_final_g)
    v_in = dict(w_ada=v_w_ada, b_ada=v_b_ada, norm1_g=v_norm1_g, w_in=v_w_in, v_norm_g=v_v_norm_g, w_spatial=v_w_spatial,
                b_spatial=v_b_spatial, out_norm_g=v_out_norm_g, w_out=v_w_out, norm2_g=v_norm2_g, w_gate=v_w_gate,
                w_up=v_w_up, w_down=v_w_down, final_g=v_final_g)

    S, D = x.shape[1], x.shape[2]
    n_g = v_norm_g.shape[-1] // LANE
    n_h = (D - n_g * LANE) // LANE
    GW = n_g * LANE
    xi, yi, ci = _place()
    chip = 2 * xi + yi
    me = 4 * xi + 2 * yi + ci
    place = jnp.stack([chip, ci]).astype(jnp.int32)

    xs, target = x[0], loss_target[0]
    geoms = [_Sharded(weights[n].shape[1:], BY_COLS[n]) for n in BIG]

    full = {}
    for i, group in enumerate((("w_in",), ("w_out",), ("w_gate", "w_up"), ("w_down",))):
        gg = [geoms[BIG.index(n)] for n in group]
        own = [_cast_into_full(place, weights[n][0], g, "cast_" + n) for n, g in zip(group, gg)]
        gathered = _gather_weights(own, gg, "gather_" + "_".join(group), 1 + i)
        full.update(zip(group, gathered))

    c_pad = jnp.concatenate([c, jnp.zeros((7, D), F32)], axis=0)
    c_all = _allgather8(c_pad, "gather_c")[::8]
    n_ada = w_ada.shape[2]
    b_cols = lax.dynamic_slice(b_ada, (0, chip * n_ada), (1, n_ada))
    mod_parts = _allgather8(_mod_part(c_all, w_ada[0], b_cols, "mod_part"), "gather_mod")
    mod_all = mod_parts.reshape(N_CHIPS, 2, 8, n_ada)[:, 0].transpose(1, 0, 2).reshape(8, N_CHIPS * n_ada)
    mod = lax.dynamic_slice(mod_all, (me, 0), (1, 6 * D))
    shift1, scale1, gate1, shift2, scale2, gate2 = [mod[:, i * D:(i + 1) * D] for i in range(6)]

    b_t = b_spatial[0].T
    h1 = _norm_mod(xs, norm1_g, scale1, shift1, "norm1")
    proj, = _mm("nn", h1, full["w_in"], [F32], "proj")
    on_gm = _gmlp_fwd(proj, v_norm_g, w_spatial[0], b_t, out_norm_g, n_g, "gmlp_fwd")
    o_sb, on_sb, l_sum = _sb_fwd(proj, out_norm_g, n_g, n_h, "sb_fwd")
    o_n = jnp.concatenate([on_gm, on_sb], axis=1)
    attn, = _mm("nn", o_n, full["w_out"], [F32], "attn_out")
    x1, h2 = _residual_norm_mod(xs, attn, gate1, norm2_g, scale2, shift2, "norm2")
    a_g, a_u, f_in = _gate_up(h2, full["w_gate"], full["w_up"], "gate_up")
    f, = _mm("nn", f_in, full["w_down"], [F32], "down", tm=1024)
    dx2, df, d_gate2, d_final_g, loss_part = _final_loss_bwd(x1, f, gate2, final_g.reshape(1, D), target, "final")
    loss = lax.psum(loss_part[0, 0], ("x", "y", "c"))

    geom_of = dict(zip(BIG, geoms))
    grad_out, delta, new_m, new_v = {}, {}, {}, {}

    def swap(group, grads, collective_id):
        return _swap_core_halves(grads, [geom_of[n] for n in group], "swap_" + "_".join(group), collective_id)

    def chip_sums(group, grads, theirs, after):
        return [_chip_sum(place, gr, _then(after, t), geom_of[n], "chip_sum_" + n) for n, gr, t in zip(group, grads, theirs)]

    def scatter(group, sums, collective_id):
        return _scatter_chip_sums(sums, [geom_of[n] for n in group], "scatter_" + "_".join(group), collective_id)

    def reduce_halves(group, grads, theirs, others, after):
        return [_reduce_half(place, gr, t, _then(after, o), geom_of[n], "reduce_" + n)
                for n, gr, t, o in zip(group, grads, theirs, others)]

    def share(group, halves, collective_id):
        return _share_reduced_halves(halves, [geom_of[n] for n in group], "share_" + "_".join(group), collective_id)

    def adamw(group, reduced, after):
        for n, r in zip(group, reduced):
            go, d, mo, vo = _adamw(weights[n][0], _then(after, r), m_in[n][0], v_in[n][0], "adamw_" + n)
            grad_out[n], delta[n], new_m[n], new_v[n] = go[None], d[None], mo[None], vo[None]
        return d

    g_down = ("w_down",)
    g_ffn = ("w_gate", "w_up")
    g_out = ("w_out",)
    g_in = ("w_in",)

    gr_down = _mm("tn", f_in, df, [F32], "d_w_down", tm=1408, tn=1024)
    th_down = swap(g_down, gr_down, 6)
    d_ag, d_au = _mm("nt", df, full["w_down"], [MXU_DTYPE, MXU_DTYPE], "d_ffn_in", extras=(a_g, a_u),
                     epilogue=_swiglu_bwd_epilogue)
    sm_down = chip_sums(g_down, gr_down, th_down, after=d_ag)
    ot_down = scatter(g_down, sm_down, 7)
    gr_ffn = [_mm("tn", h2, _then(sm_down, d_ag), [F32], "d_w_gate")[0], _mm("tn", h2, d_au, [F32], "d_w_up")[0]]
    th_ffn = swap(g_ffn, gr_ffn, 9)
    dh2 = _mm_ktiled("nt", [(_then(gr_ffn, d_ag), full["w_gate"]), (d_au, full["w_up"])], "d_h2", tn=512)
    sm_ffn = chip_sums(g_ffn, gr_ffn, th_ffn, after=dh2)
    ot_ffn = scatter(g_ffn, sm_ffn, 10)
    hv_down = reduce_halves(g_down, gr_down, th_down, ot_down, after=sm_ffn)
    rd_down = share(g_down, hv_down, 8)
    dx1, d_shift2, d_scale2, d_norm2_g, d_gate1, d_attn = _norm_mod_bwd(
        _then(hv_down, dh2), x1, dx2, norm2_g, scale2, "norm2_bwd", branch=attn, gate=gate1)
    gr_out = _mm("tn", o_n, d_attn, [F32], "d_w_out")
    th_out = swap(g_out, gr_out, 12)
    d_on, = _mm("nt", _then(gr_out, d_attn), full["w_out"], [F32], "d_o")
    dp_gm, d_w_spatial, d_b_t, d_v_norm_g, d_og_gm = _gmlp_bwd(proj, d_on, v_norm_g, w_spatial[0], b_t, out_norm_g, n_g, "gmlp_bwd")
    dq, dk, dv, d_og_sb = _sb_bwd(proj, o_sb, l_sum, _then(dp_gm, d_on), out_norm_g, n_g, n_h, "sb_bwd")
    sm_out = chip_sums(g_out, gr_out, th_out, after=dq)
    ot_out = scatter(g_out, sm_out, 13)
    hv_ffn = reduce_halves(g_ffn, gr_ffn, th_ffn, ot_ffn, after=sm_out)
    rd_ffn = share(g_ffn, hv_ffn, 11)
    dproj = jnp.concatenate([_then(hv_ffn, dp_gm), dq, dk, dv], axis=1)
    gr_in = _mm("tn", h1, dproj, [F32], "d_w_in")
    th_in = swap(g_in, gr_in, 15)
    hv_out = reduce_halves(g_out, gr_out, th_out, ot_out, after=gr_in)
    rd_out = share(g_out, hv_out, 14)
    dh1, = _mm("nt", _then(gr_in, dproj), full["w_in"], [F32], "d_h1", tm=1024)
    grad_x, d_shift1, d_scale1, d_norm1_g = _norm_mod_bwd(dh1, xs, dx1, norm1_g, scale1, "norm1_bwd")
    sm_in = chip_sums(g_in, gr_in, th_in, after=grad_x)
    ot_in = scatter(g_in, sm_in, 16)

    dmod = jnp.concatenate([d_shift1, d_scale1, d_gate1, d_shift2, d_scale2, d_gate2], axis=1)
    small_parts = dict(b_ada=dmod, norm1_g=d_norm1_g, v_norm_g=d_v_norm_g, w_spatial=d_w_spatial, b_spatial=d_b_t.T,
                       out_norm_g=jnp.concatenate([d_og_gm, d_og_sb], axis=1), norm2_g=d_norm2_g, final_g=d_final_g)
    slab = _then(sm_in, _pack(small_parts))
    rows = slab.shape[0]
    gathered = _allgather8_on_sequencer(slab, "gather_small", 18)
    done = adamw(g_down, rd_down, after=slab)
    done = adamw(g_ffn, rd_ffn, after=done)
    done = adamw(g_out, rd_out, after=done)
    gathered = _then(done, gathered)
    small_shapes = {n: weights[n].shape for n in SMALL}
    small_sum = _sum_devices(gathered, 8, "sum_small")
    dmod_all = gathered.reshape(8, rows * LANE)[:, :6 * D]
    dmod_cols = lax.dynamic_slice(dmod_all, (0, chip * n_ada), (8, n_ada))
    g_ada, d, mo, vo = _adamw_ada(c_all, dmod_cols, w_ada[0], m_w_ada[0], v_w_ada[0], "adamw_w_ada")
    grad_out["w_ada"], delta["w_ada"], new_m["w_ada"], new_v["w_ada"] = g_ada[None], d[None], mo[None], vo[None]
    gs_small, d_small, mo, vo = _adamw(_pack({n: weights[n] for n in SMALL}), small_sum, _pack({n: m_in[n] for n in SMALL}),
                                       _pack({n: v_in[n] for n in SMALL}), "adamw_small")
    for dst, slab_out in ((grad_out, gs_small), (delta, d_small), (new_m, mo), (new_v, vo)):
        dst.update(_unpack(slab_out, small_shapes))
    hv_in = reduce_halves(g_in, gr_in, th_in, ot_in, after=d)
    adamw(g_in, share(g_in, hv_in, 17), after=d)

    return (loss, grad_x[None], *[grad_out[n] for n in ORDER], *[delta[n] for n in ORDER],
            *[new_m[n] for n in ORDER], *[new_v[n] for n in ORDER])
```

```python
import functools
import math

import jax
import jax.numpy as jnp
from jax import lax
from jax.experimental import pallas as pl
from jax.experimental.pallas import tpu as pltpu
from jax.experimental.pallas import tpu_sc as plsc

F32 = jnp.float32
MXU_DTYPE = jnp.bfloat16
WIRE_DTYPE = jnp.bfloat16
EPS = 1e-6
LANE = 128
V7X_VMEM_LIMIT = 56 * 1024 * 1024
MESH = pl.DeviceIdType.MESH
N_CHIPS = 4
FLIPS = (2, 1, 3)

ADAM_LR = 0.001
ADAM_B1 = 0.9
ADAM_B2 = 0.999
ADAM_EPS = 1e-08
ADAM_WD = 0.01
ADAM_STEP = 10


def _params(*semantics):
    return pltpu.CompilerParams(dimension_semantics=semantics or None, vmem_limit_bytes=V7X_VMEM_LIMIT)


def _tile(dim, pref, unit=LANE):
    best = None
    t = unit
    while t <= min(dim, pref):
        if dim % t == 0:
            best = t
        t += unit
    return best if best is not None else dim


def _then(first, second):
    return lax.optimization_barrier((first, second))[1]


def _sum0(v):
    return jnp.sum(v, axis=0, keepdims=True)


def _mean1(v):
    return jnp.mean(v, axis=-1, keepdims=True)


def _gelu(x):
    return 0.5 * x * (1.0 + lax.erf(x * (1.0 / math.sqrt(2.0))))


def _gelu_grad(x):
    cdf = 0.5 * (1.0 + lax.erf(x * (1.0 / math.sqrt(2.0))))
    return cdf + x * jnp.exp(-0.5 * x * x) * (1.0 / math.sqrt(2.0 * math.pi))


def _dot(a, b, dims):
    return lax.dot_general(a, b, (dims, ((), ())), preferred_element_type=F32)


NN = ((1,), (0,))
NT = ((1,), (1,))
TN = ((0,), (0,))


def _mm(kind, a, b, out_dtypes, name, tm=2048, tn=512, extras=(), epilogue=None):
    if kind == "nn":
        (M, K), N = a.shape, b.shape[1]
    elif kind == "nt":
        (M, K), N = a.shape, b.shape[0]
    else:
        (K, M), N = a.shape, b.shape[1]
    tm, tn = _tile(M, tm), _tile(N, tn)
    a_spec = pl.BlockSpec((K, tm), lambda i, j: (0, i)) if kind == "tn" else pl.BlockSpec((tm, K), lambda i, j: (i, 0))
    b_spec = pl.BlockSpec((tn, K), lambda i, j: (j, 0)) if kind == "nt" else pl.BlockSpec((K, tn), lambda i, j: (0, j))
    mn_spec = pl.BlockSpec((tm, tn), lambda i, j: (i, j))
    dims = {"nn": NN, "nt": NT, "tn": TN}[kind]
    n_extra = len(extras)

    n_chunks = 1 if epilogue is None or kind == "tn" else max(1, tm // 512)
    rows_per = tm // n_chunks

    def body(a_ref, b_ref, *rest):
        for r in range(n_chunks):
            rows = slice(r * rows_per, (r + 1) * rows_per)
            acc = _dot(a_ref[...] if n_chunks == 1 else a_ref[rows, :], b_ref[...], dims)
            res = (acc,) if epilogue is None else epilogue(acc, *[e[rows, :] for e in rest[:n_extra]])
            for o_ref, val in zip(rest[n_extra:], res):
                o_ref[rows, :] = val.astype(o_ref.dtype)

    outs = pl.pallas_call(
        body, name=name, grid=(M // tm, N // tn),
        in_specs=[a_spec, b_spec] + [mn_spec] * n_extra,
        out_specs=[mn_spec] * len(out_dtypes),
        out_shape=[jax.ShapeDtypeStruct((M, N), d) for d in out_dtypes],
        compiler_params=_params("parallel", "arbitrary"),
    )(a, b, *extras)
    return outs


def _mm_ktiled(kind, pairs, name, tm=2048, tn=1024, tk=1408):
    a0, b0 = pairs[0]
    M, K = a0.shape
    N = b0.shape[1] if kind == "nn" else b0.shape[0]
    tm, tn, tk = _tile(M, tm), _tile(N, tn), _tile(K, tk)
    a_spec = pl.BlockSpec((tm, tk), lambda i, j, k: (i, k))
    b_spec = pl.BlockSpec((tk, tn), lambda i, j, k: (k, j)) if kind == "nn" else pl.BlockSpec((tn, tk), lambda i, j, k: (j, k))
    dims = NN if kind == "nn" else NT
    n_pairs = len(pairs)

    def body(*refs):
        o_ref = refs[2 * n_pairs]
        acc = _dot(refs[0][...], refs[1][...], dims)
        for p in range(1, n_pairs):
            acc = acc + _dot(refs[2 * p][...], refs[2 * p + 1][...], dims)

        @pl.when(pl.program_id(2) == 0)
        def _():
            o_ref[...] = acc

        @pl.when(pl.program_id(2) != 0)
        def _():
            o_ref[...] += acc

    return pl.pallas_call(
        body, name=name, grid=(M // tm, N // tn, K // tk),
        in_specs=[a_spec, b_spec] * n_pairs,
        out_specs=pl.BlockSpec((tm, tn), lambda i, j, k: (i, j)),
        out_shape=jax.ShapeDtypeStruct((M, N), F32),
        compiler_params=_params("parallel", "parallel", "arbitrary"),
    )(*[x for pair in pairs for x in pair])


def _gate_up(h, wg, wu, name):
    (M, K), N = h.shape, wg.shape[1]
    tm, tn = _tile(M, 2048), _tile(N, 512)

    n_chunks = max(1, tm // 512)
    rows_per = tm // n_chunks

    def body(h_ref, wg_ref, wu_ref, ag_ref, au_ref, f_ref):
        for r in range(n_chunks):
            rows = slice(r * rows_per, (r + 1) * rows_per)
            hv = h_ref[rows, :]
            ag = _dot(hv, wg_ref[...], NN)
            au = _dot(hv, wu_ref[...], NN)
            ag_ref[rows, :] = ag.astype(ag_ref.dtype)
            au_ref[rows, :] = au.astype(au_ref.dtype)
            f_ref[rows, :] = (ag * jax.nn.sigmoid(ag) * au).astype(f_ref.dtype)

    w_spec = pl.BlockSpec((K, tn), lambda i, j: (0, j))
    mn_spec = pl.BlockSpec((tm, tn), lambda i, j: (i, j))
    return pl.pallas_call(
        body, name=name, grid=(M // tm, N // tn),
        in_specs=[pl.BlockSpec((tm, K), lambda i, j: (i, 0)), w_spec, w_spec],
        out_specs=[mn_spec] * 3,
        out_shape=[jax.ShapeDtypeStruct((M, N), MXU_DTYPE)] * 3,
        compiler_params=_params("parallel", "arbitrary"),
    )(h, wg, wu)


def _swiglu_bwd_epilogue(dfin, ag, au):
    ag, au = ag.astype(F32), au.astype(F32)
    sg = jax.nn.sigmoid(ag)
    d_au = dfin * (ag * sg)
    d_ag = dfin * au * (sg * (1.0 + ag * (1.0 - sg)))
    return d_ag, d_au


def _row_specs(ts, width):
    return pl.BlockSpec((ts, width), lambda i: (i, 0)), pl.BlockSpec((1, width), lambda i: (0, 0))


def _cast_into_full(place, shard, g, name):
    R, C = shard.shape
    tr = _tile(R, 256, 16)
    n_blk = R // tr
    out_map = (lambda i, p: (i, p[0])) if g.by_cols else (lambda i, p: (p[0] * n_blk + i, 0))

    def body(p_ref, a_ref, o_ref):
        o_ref[...] = a_ref[...].astype(o_ref.dtype)

    return pl.pallas_call(
        body, name=name,
        grid_spec=pltpu.PrefetchScalarGridSpec(
            num_scalar_prefetch=1, grid=(n_blk,),
            in_specs=[pl.BlockSpec((tr, C), lambda i, p: (i, 0))],
            out_specs=pl.BlockSpec((tr, C), out_map)),
        out_shape=jax.ShapeDtypeStruct(g.full, WIRE_DTYPE),
        compiler_params=_params("arbitrary"),
    )(place, shard)


def _norm_mod(x, g, scale, shift, name):
    S, D = x.shape
    ts = _tile(S, 256, 16)
    tile, vec = _row_specs(ts, D)

    def body(x_ref, g_ref, sc_ref, sh_ref, h_ref):
        xv = x_ref[...]
        r = lax.rsqrt(_mean1(xv * xv) + EPS)
        h_ref[...] = ((xv * r) * g_ref[...] * (1.0 + sc_ref[...]) + sh_ref[...]).astype(h_ref.dtype)

    return pl.pallas_call(body, name=name, grid=(S // ts,), in_specs=[tile, vec, vec, vec], out_specs=tile,
                          out_shape=jax.ShapeDtypeStruct((S, D), MXU_DTYPE), compiler_params=_params("parallel"))(x, g, scale, shift)


def _residual_norm_mod(x, attn, gate, g, scale, shift, name):
    S, D = x.shape
    ts = _tile(S, 256, 16)
    tile, vec = _row_specs(ts, D)

    def body(x_ref, a_ref, gate_ref, g_ref, sc_ref, sh_ref, x1_ref, h_ref):
        x1 = x_ref[...] + gate_ref[...] * a_ref[...]
        x1_ref[...] = x1
        r = lax.rsqrt(_mean1(x1 * x1) + EPS)
        h_ref[...] = ((x1 * r) * g_ref[...] * (1.0 + sc_ref[...]) + sh_ref[...]).astype(h_ref.dtype)

    return pl.pallas_call(body, name=name, grid=(S // ts,), in_specs=[tile, tile, vec, vec, vec, vec],
                          out_specs=[tile, tile],
                          out_shape=[jax.ShapeDtypeStruct((S, D), F32), jax.ShapeDtypeStruct((S, D), MXU_DTYPE)],
                          compiler_params=_params("parallel"))(x, attn, gate, g, scale, shift)


def _final_loss_bwd(x1, f, gate2, final_g, target, name):
    S, D = x1.shape
    ts = _tile(S, 256, 16)
    tile, vec = _row_specs(ts, D)
    loss_spec = pl.BlockSpec((1, LANE), lambda i: (0, 0))

    def body(x1_ref, f_ref, gate_ref, g_ref, t_ref, dx2_ref, df_ref, dgate_ref, dg_ref, loss_ref):
        @pl.when(pl.program_id(0) == 0)
        def _():
            dgate_ref[...] = jnp.zeros_like(dgate_ref)
            dg_ref[...] = jnp.zeros_like(dg_ref)
            loss_ref[...] = jnp.zeros_like(loss_ref)

        fv, gate, g = f_ref[...], gate_ref[...], g_ref[...]
        x2 = x1_ref[...] + gate * fv
        r = lax.rsqrt(_mean1(x2 * x2) + EPS)
        xn = x2 * r
        err = xn * g - t_ref[...]
        loss_ref[...] += jnp.broadcast_to(0.5 * _sum0(_mean1(err * err)), loss_ref.shape)
        dy = err * (1.0 / D)
        dg_ref[...] += _sum0(dy * xn)
        dxn = dy * g
        dx2 = r * (dxn - xn * _mean1(dxn * xn))
        dx2_ref[...] = dx2
        dgate_ref[...] += _sum0(dx2 * fv)
        df_ref[...] = (dx2 * gate).astype(df_ref.dtype)

    return pl.pallas_call(
        body, name=name, grid=(S // ts,), in_specs=[tile, tile, vec, vec, tile],
        out_specs=[tile, tile, vec, vec, loss_spec],
        out_shape=[jax.ShapeDtypeStruct((S, D), F32), jax.ShapeDtypeStruct((S, D), MXU_DTYPE),
                   jax.ShapeDtypeStruct((1, D), F32), jax.ShapeDtypeStruct((1, D), F32),
                   jax.ShapeDtypeStruct((1, LANE), F32)],
        compiler_params=_params("arbitrary"),
    )(x1, f, gate2, final_g, target)


def _norm_mod_bwd(dh, xin, dres, g, scale, name, branch=None, gate=None):
    S, D = xin.shape
    ts = _tile(S, 256, 16)
    tile, vec = _row_specs(ts, D)
    with_gate = branch is not None

    def body(*refs):
        if with_gate:
            dh_ref, x_ref, dres_ref, g_ref, sc_ref, br_ref, gate_ref, dx_ref, dshift_ref, dscale_ref, dg_ref, dgate_ref, dbr_ref = refs
            accs = (dshift_ref, dscale_ref, dg_ref, dgate_ref)
        else:
            dh_ref, x_ref, dres_ref, g_ref, sc_ref, dx_ref, dshift_ref, dscale_ref, dg_ref = refs
            accs = (dshift_ref, dscale_ref, dg_ref)

        @pl.when(pl.program_id(0) == 0)
        def _():
            for acc in accs:
                acc[...] = jnp.zeros_like(acc)

        dh_v, xv, g_v = dh_ref[...], x_ref[...], g_ref[...]
        one_sc = 1.0 + sc_ref[...]
        r = lax.rsqrt(_mean1(xv * xv) + EPS)
        xn = xv * r
        dshift_ref[...] += _sum0(dh_v)
        dscale_ref[...] += _sum0(dh_v * (xn * g_v))
        dg_ref[...] += _sum0(dh_v * one_sc * xn)
        dxn = dh_v * (g_v * one_sc)
        dx = dres_ref[...] + r * (dxn - xn * _mean1(dxn * xn))
        dx_ref[...] = dx
        if with_gate:
            dgate_ref[...] += _sum0(dx * br_ref[...])
            dbr_ref[...] = (dx * gate_ref[...]).astype(dbr_ref.dtype)

    ins = [dh, xin, dres, g, scale] + ([branch, gate] if with_gate else [])
    in_specs = [tile, tile, tile, vec, vec] + ([tile, vec] if with_gate else [])
    out_specs = [tile, vec, vec, vec] + ([vec, tile] if with_gate else [])
    out_shape = [jax.ShapeDtypeStruct((S, D), F32)] + [jax.ShapeDtypeStruct((1, D), F32)] * 3
    if with_gate:
        out_shape += [jax.ShapeDtypeStruct((1, D), F32), jax.ShapeDtypeStruct((S, D), MXU_DTYPE)]
    return pl.pallas_call(body, name=name, grid=(S // ts,), in_specs=in_specs, out_specs=out_specs,
                          out_shape=out_shape, compiler_params=_params("arbitrary"))(*ins)


def _causal_weights(ws_ref, wt_ref, n_g):
    row = lax.broadcasted_iota(jnp.int32, (LANE, LANE), 0)
    col = lax.broadcasted_iota(jnp.int32, (LANE, LANE), 1)
    for g in range(n_g):
        wt_ref[g] = jnp.where(col <= row, ws_ref[g], 0.0).astype(wt_ref.dtype)


def _group_layernorm(v):
    xc = v - _mean1(v)
    rstd = lax.rsqrt(_mean1(xc * xc) + EPS)
    return xc * rstd, rstd


def _gmlp_fwd(proj, v_gain, w_s, b_t, out_gain, n_g, name):
    S = proj.shape[0]
    GW = n_g * LANE

    def body(p_ref, vg_ref, ws_ref, bt_ref, og_ref, on_ref, wt_ref):
        @pl.when(pl.program_id(0) == 0)
        def _():
            _causal_weights(ws_ref, wt_ref, n_g)

        for g in range(n_g):
            cols = slice(g * LANE, (g + 1) * LANE)
            u = _gelu(p_ref[:, cols])
            v = _gelu(p_ref[:, GW + g * LANE:GW + (g + 1) * LANE])
            vhat, _ = _group_layernorm(v)
            vln = (vhat * vg_ref[:, cols]).astype(MXU_DTYPE)
            mixed = _dot(wt_ref[g], vln, NN) + bt_ref[:, g:g + 1]
            o = u * mixed
            r = lax.rsqrt(_mean1(o * o) + EPS)
            on_ref[:, cols] = (o * r * og_ref[:, cols]).astype(on_ref.dtype)

    return pl.pallas_call(
        body, name=name, grid=(S // LANE,),
        in_specs=[pl.BlockSpec((LANE, 2 * GW), lambda n: (n, 0)),
                  pl.BlockSpec((1, GW), lambda n: (0, 0)),
                  pl.BlockSpec((n_g, LANE, LANE), lambda n: (0, 0, 0)),
                  pl.BlockSpec((LANE, n_g), lambda n: (0, 0)),
                  pl.BlockSpec((1, GW), lambda n: (0, 0))],
        out_specs=pl.BlockSpec((LANE, GW), lambda n: (n, 0)),
        out_shape=jax.ShapeDtypeStruct((S, GW), MXU_DTYPE),
        scratch_shapes=[pltpu.VMEM((n_g, LANE, LANE), MXU_DTYPE)],
        compiler_params=_params("arbitrary"),
    )(proj, v_gain, w_s, b_t, out_gain)


def _gmlp_bwd(proj, d_on, v_gain, w_s, b_t, out_gain, n_g, name):
    S = proj.shape[0]
    GW = n_g * LANE

    def body(p_ref, dn_ref, vg_ref, ws_ref, bt_ref, og_ref, dp_ref, dws_ref, dbt_ref, dvg_ref, dog_ref, wt_ref):
        @pl.when(pl.program_id(0) == 0)
        def _():
            _causal_weights(ws_ref, wt_ref, n_g)
            dws_ref[...] = jnp.zeros_like(dws_ref)
            dbt_ref[...] = jnp.zeros_like(dbt_ref)
            dvg_ref[...] = jnp.zeros_like(dvg_ref)
            dog_ref[...] = jnp.zeros_like(dog_ref)

        row = lax.broadcasted_iota(jnp.int32, (LANE, LANE), 0)
        col = lax.broadcasted_iota(jnp.int32, (LANE, LANE), 1)
        for g in range(n_g):
            cols = slice(g * LANE, (g + 1) * LANE)
            vcols = slice(GW + g * LANE, GW + (g + 1) * LANE)
            pu, pv = p_ref[:, cols], p_ref[:, vcols]
            u, v = _gelu(pu), _gelu(pv)
            vhat, rstd = _group_layernorm(v)
            gain = vg_ref[:, cols]
            vln = (vhat * gain).astype(MXU_DTYPE)
            mixed = _dot(wt_ref[g], vln, NN) + bt_ref[:, g:g + 1]
            o = u * mixed
            r = lax.rsqrt(_mean1(o * o) + EPS)
            oh = o * r
            dn = dn_ref[:, cols]
            dog_ref[:, cols] += _sum0(dn * oh)
            dhn = dn * og_ref[:, cols]
            d_o = r * (dhn - oh * _mean1(dhn * oh))
            du = d_o * mixed
            dmix = d_o * u
            dbt_ref[:, g:g + 1] += jnp.sum(dmix, axis=1, keepdims=True)
            dmix_b = dmix.astype(MXU_DTYPE)
            dws_ref[g] += jnp.where(col <= row, _dot(dmix_b, vln, NT), 0.0)
            dvln = _dot(wt_ref[g], dmix_b, TN)
            dvg_ref[:, cols] += _sum0(dvln * vhat)
            dxh = dvln * gain
            dv = rstd * (dxh - _mean1(dxh) - vhat * _mean1(dxh * vhat))
            dp_ref[:, cols] = (du * _gelu_grad(pu)).astype(dp_ref.dtype)
            dp_ref[:, vcols] = (dv * _gelu_grad(pv)).astype(dp_ref.dtype)

    return pl.pallas_call(
        body, name=name, grid=(S // LANE,),
        in_specs=[pl.BlockSpec((LANE, 2 * GW), lambda n: (n, 0)),
                  pl.BlockSpec((LANE, GW), lambda n: (n, 0)),
                  pl.BlockSpec((1, GW), lambda n: (0, 0)),
                  pl.BlockSpec((n_g, LANE, LANE), lambda n: (0, 0, 0)),
                  pl.BlockSpec((LANE, n_g), lambda n: (0, 0)),
                  pl.BlockSpec((1, GW), lambda n: (0, 0))],
        out_specs=[pl.BlockSpec((LANE, 2 * GW), lambda n: (n, 0)),
                   pl.BlockSpec((n_g, LANE, LANE), lambda n: (0, 0, 0)),
                   pl.BlockSpec((LANE, n_g), lambda n: (0, 0)),
                   pl.BlockSpec((1, GW), lambda n: (0, 0)),
                   pl.BlockSpec((1, GW), lambda n: (0, 0))],
        out_shape=[jax.ShapeDtypeStruct((S, 2 * GW), MXU_DTYPE),
                   jax.ShapeDtypeStruct((n_g, LANE, LANE), F32),
                   jax.ShapeDtypeStruct((LANE, n_g), F32),
                   jax.ShapeDtypeStruct((1, GW), F32),
                   jax.ShapeDtypeStruct((1, GW), F32)],
        scratch_shapes=[pltpu.VMEM((n_g, LANE, LANE), MXU_DTYPE)],
        compiler_params=_params("arbitrary"),
    )(proj, d_on, v_gain, w_s, b_t, out_gain)


def _tri_sum(v, tri, exact=True):
    hi = v.astype(MXU_DTYPE)
    if not exact:
        return _dot(hi, tri, NN)
    lo = (v - hi.astype(F32)).astype(MXU_DTYPE)
    return _dot(hi, tri, NN) + _dot(lo, tri, NN)


def _log_sigmoids(z):
    sp = jnp.log(1.0 + jnp.exp(-jnp.abs(z)))
    return jnp.minimum(z, 0.0) - sp, jnp.minimum(-z, 0.0) - sp


def _rows(i, size):
    return pl.ds(pl.multiple_of(i * size, size), size)


SB_QUERY_TILE = 1024
SB_KEY_TILE = 256


def _sb_tiles(S):
    tq = _tile(S, SB_QUERY_TILE)
    tk = _tile(tq, SB_KEY_TILE)
    assert (tq // tk) % 2 == 0, "the key sweep takes two blocks a pass"
    return tq, tk, S // tq, tq // tk


def _triangle(n, keep):
    row = lax.broadcasted_iota(jnp.int32, (n, n), 0)
    col = lax.broadcasted_iota(jnp.int32, (n, n), 1)
    return jnp.where(keep(row, col), 1.0, 0.0).astype(MXU_DTYPE)


def _strictly_before(tq, tk, key_offset):
    row = lax.broadcasted_iota(jnp.int32, (tq, tk), 0)
    col = lax.broadcasted_iota(jnp.int32, (tq, tk), 1)
    return col + key_offset < row


def _sb_specs(S, n_g, n_h):
    base = 2 * n_g
    q_spec = pl.BlockSpec((S, LANE), lambda h: (0, base + h))
    k_spec = pl.BlockSpec((S, LANE), lambda h: (0, base + n_h + h))
    v_spec = pl.BlockSpec((S, LANE), lambda h: (0, base + 2 * n_h + h))
    gain_spec = pl.BlockSpec((1, LANE), lambda h: (0, n_g + h))
    head_spec = pl.BlockSpec((S, LANE), lambda h: (0, h))
    return q_spec, k_spec, v_spec, gain_spec, head_spec


def _sb_fwd(proj, out_gain, n_g, n_h, name):
    S = proj.shape[0]
    TQ, TK, NQ, KPQ = _sb_tiles(S)
    scale = LANE ** -0.5
    q_spec, k_spec, v_spec, gain_spec, head_spec = _sb_specs(S, n_g, n_h)

    def body(q_ref, k_ref, v_ref, og_ref, o_ref, on_ref, ls_ref, qb, kb, vb):
        qb[...] = q_ref[...].astype(MXU_DTYPE)
        kb[...] = k_ref[...].astype(MXU_DTYPE)
        vb[...] = v_ref[...].astype(MXU_DTYPE)
        after = _triangle(TK, lambda r, c: r > c)

        def block(qi, j, ctail, acc, key_offset):
            skip = key_offset or 0
            z = _dot(qi[skip:], kb[_rows(j, TK), :], NT) * scale
            lb, l1m = _log_sigmoids(z)
            if key_offset is not None:
                strict = _strictly_before(TQ - skip, TK, 0)
                l1m = jnp.where(strict, l1m, 0.0)
            a = jnp.exp(lb + ctail[skip:] + _tri_sum(l1m, after))
            if key_offset is not None:
                a = jnp.where(strict, a, 0.0)
            acc_new = acc[skip:] + _dot(a.astype(MXU_DTYPE), vb[_rows(j, TK), :], NN)
            ctail_new = ctail[skip:] + jnp.sum(l1m, axis=1, keepdims=True)
            if skip:
                ctail_new = jnp.concatenate([ctail[:skip], ctail_new], axis=0)
                acc_new = jnp.concatenate([acc[:skip], acc_new], axis=0)
            return ctail_new, acc_new

        def q_loop(i, carry):
            qi = qb[_rows(i, TQ), :]
            state = (jnp.zeros((TQ, 1), F32), jnp.zeros((TQ, LANE), F32))
            for d in reversed(range(KPQ)):
                state = block(qi, i * KPQ + d, state[0], state[1], d * TK)
            def pair(jj, st):
                st = block(qi, i * KPQ - 1 - 2 * jj, st[0], st[1], None)
                return block(qi, i * KPQ - 2 - 2 * jj, st[0], st[1], None)

            ctail, acc = lax.fori_loop(0, i * (KPQ // 2), pair, state)
            ls_ref[_rows(i, TQ), :] = jnp.broadcast_to(ctail, (TQ, LANE))
            o_ref[_rows(i, TQ), :] = acc
            r = lax.rsqrt(_mean1(acc * acc) + EPS)
            on_ref[_rows(i, TQ), :] = (acc * r * og_ref[...]).astype(on_ref.dtype)
            return carry

        lax.fori_loop(0, NQ, q_loop, 0)

    return pl.pallas_call(
        body, name=name, grid=(n_h,),
        in_specs=[q_spec, k_spec, v_spec, gain_spec],
        out_specs=[head_spec, head_spec, head_spec],
        out_shape=[jax.ShapeDtypeStruct((S, n_h * LANE), F32), jax.ShapeDtypeStruct((S, n_h * LANE), MXU_DTYPE),
                   jax.ShapeDtypeStruct((S, n_h * LANE), F32)],
        scratch_shapes=[pltpu.VMEM((S, LANE), MXU_DTYPE)] * 3,
        compiler_params=_params("parallel"),
    )(proj, proj, proj, out_gain)


def _sb_bwd(proj, o_sb, l_sum, d_on, out_gain, n_g, n_h, name):
    S = proj.shape[0]
    TQ, TK, NQ, KPQ = _sb_tiles(S)
    scale = LANE ** -0.5
    q_spec, k_spec, v_spec, gain_spec, head_spec = _sb_specs(S, n_g, n_h)
    dn_spec = pl.BlockSpec((S, LANE), lambda h: (0, n_g + h))
    dgain_spec = pl.BlockSpec((1, LANE), lambda h: (0, h))

    def body(q_ref, k_ref, v_ref, o_ref, ls_ref, dn_ref, og_ref, dq_ref, dk_ref, dv_ref, dog_ref,
             qb, kb, vb, dob, dk_acc, dv_acc):
        qb[...] = q_ref[...].astype(MXU_DTYPE)
        kb[...] = k_ref[...].astype(MXU_DTYPE)
        vb[...] = v_ref[...].astype(MXU_DTYPE)
        o, dn = o_ref[...], dn_ref[...]
        r = lax.rsqrt(_mean1(o * o) + EPS)
        oh = o * r
        dog_ref[...] = _sum0(dn * oh)
        dhn = dn * og_ref[...]
        dob[...] = (r * (dhn - oh * _mean1(dhn * oh))).astype(MXU_DTYPE)
        dk_acc[...] = jnp.zeros_like(dk_acc)
        dv_acc[...] = jnp.zeros_like(dv_acc)

        up_to = _triangle(TK, lambda r, c: r <= c)
        before = _triangle(TK, lambda r, c: r < c)

        def block(qi, doi, ltot, j, cl, cdl, dq, key_offset):
            skip = key_offset or 0
            q_in, do_in = qi[skip:], doi[skip:]
            kj, vj = kb[_rows(j, TK), :], vb[_rows(j, TK), :]
            z = _dot(q_in, kj, NT) * scale
            lb, l1m = _log_sigmoids(z)
            if key_offset is not None:
                strict = _strictly_before(TQ - skip, TK, 0)
                l1m = jnp.where(strict, l1m, 0.0)
            a = jnp.exp(lb + (ltot[skip:] - (cl[skip:] + _tri_sum(l1m, up_to))))
            if key_offset is not None:
                a = jnp.where(strict, a, 0.0)
            dl = _dot(do_in, vj, NT) * a
            d_l1m = cdl[skip:] + _tri_sum(dl, before, exact=False)
            beta = jnp.exp(lb)
            dz = dl * (1.0 - beta) - beta * d_l1m
            if key_offset is not None:
                dz = jnp.where(strict, dz, 0.0)
            dzs = (dz * scale).astype(MXU_DTYPE)
            dk_acc[_rows(j, TK), :] += _dot(dzs, q_in, TN)
            dv_acc[_rows(j, TK), :] += _dot(a.astype(MXU_DTYPE), do_in, TN)
            cl_new = cl[skip:] + jnp.sum(l1m, axis=1, keepdims=True)
            cdl_new = cdl[skip:] + jnp.sum(dl, axis=1, keepdims=True)
            dq_new = dq[skip:] + _dot(dzs, kj, NN)
            if skip:
                cl_new = jnp.concatenate([cl[:skip], cl_new], axis=0)
                cdl_new = jnp.concatenate([cdl[:skip], cdl_new], axis=0)
                dq_new = jnp.concatenate([dq[:skip], dq_new], axis=0)
            return cl_new, cdl_new, dq_new

        def q_loop(i, carry):
            qi, doi = qb[_rows(i, TQ), :], dob[_rows(i, TQ), :]
            ltot = ls_ref[_rows(i, TQ), :][:, :1]
            zero_col = jnp.zeros((TQ, 1), F32)
            def pair(jj, st):
                st = block(qi, doi, ltot, 2 * jj, st[0], st[1], st[2], None)
                return block(qi, doi, ltot, 2 * jj + 1, st[0], st[1], st[2], None)

            state = lax.fori_loop(0, i * (KPQ // 2), pair, (zero_col, zero_col, jnp.zeros((TQ, LANE), F32)))
            for d in range(KPQ):
                state = block(qi, doi, ltot, i * KPQ + d, state[0], state[1], state[2], d * TK)
            dq_ref[_rows(i, TQ), :] = state[2].astype(dq_ref.dtype)
            return carry

        lax.fori_loop(0, NQ, q_loop, 0)
        dk_ref[...] = dk_acc[...].astype(dk_ref.dtype)
        dv_ref[...] = dv_acc[...].astype(dv_ref.dtype)

    W = n_h * LANE
    return pl.pallas_call(
        body, name=name, grid=(n_h,),
        in_specs=[q_spec, k_spec, v_spec, head_spec, head_spec, dn_spec, gain_spec],
        out_specs=[head_spec, head_spec, head_spec, dgain_spec],
        out_shape=[jax.ShapeDtypeStruct((S, W), MXU_DTYPE)] * 3 + [jax.ShapeDtypeStruct((1, W), F32)],
        scratch_shapes=[pltpu.VMEM((S, LANE), MXU_DTYPE)] * 4 + [pltpu.VMEM((S, LANE), F32)] * 2,
        compiler_params=_params("parallel"),
    )(proj, proj, proj, o_sb, l_sum, d_on, out_gain)


def _mod_part(c_all, w_ada, b_ada_cols, name):
    B, K = c_all.shape
    N = w_ada.shape[1]
    tn = _tile(N, 512)

    def body(c_ref, w_ref, b_ref, o_ref):
        cv = c_ref[...]
        ca = (cv * jax.nn.sigmoid(cv)).astype(MXU_DTYPE)
        o_ref[...] = _dot(ca, w_ref[...].astype(MXU_DTYPE), NN) + b_ref[...]

    return pl.pallas_call(
        body, name=name, grid=(N // tn,),
        in_specs=[pl.BlockSpec((B, K), lambda j: (0, 0)), pl.BlockSpec((K, tn), lambda j: (0, j)),
                  pl.BlockSpec((1, tn), lambda j: (0, j))],
        out_specs=pl.BlockSpec((B, tn), lambda j: (0, j)),
        out_shape=jax.ShapeDtypeStruct((B, N), F32), compiler_params=_params("parallel"))(c_all, w_ada, b_ada_cols)


def _adamw_math(w, g, m, v):
    m = ADAM_B1 * m + (1.0 - ADAM_B1) * g
    v = ADAM_B2 * v + (1.0 - ADAM_B2) * (g * g)
    m_hat = m / (1.0 - ADAM_B1 ** ADAM_STEP)
    v_hat = v / (1.0 - ADAM_B2 ** ADAM_STEP)
    delta = -ADAM_LR * (m_hat / (jnp.sqrt(v_hat) + ADAM_EPS) + ADAM_WD * w)
    return delta, m, v


def _adamw(w, g, m, v, name):
    R, C = w.shape
    tr = _tile(R, max(8, (1 << 19) // C), 8)
    spec = pl.BlockSpec((tr, C), lambda i: (i, 0))

    def body(w_ref, g_ref, m_ref, v_ref, go_ref, d_ref, mo_ref, vo_ref):
        g = g_ref[...]
        go_ref[...] = g
        d_ref[...], mo_ref[...], vo_ref[...] = _adamw_math(w_ref[...], g, m_ref[...], v_ref[...])

    return pl.pallas_call(body, name=name, grid=(R // tr,), in_specs=[spec] * 4, out_specs=[spec] * 4,
                          out_shape=[jax.ShapeDtypeStruct((R, C), F32)] * 4, compiler_params=_params("parallel"))(w, g, m, v)


def _adamw_ada(c_all, dmod_cols, w, m, v, name):
    K, N = w.shape
    B = c_all.shape[0]
    tk, tn = _tile(K, 512), _tile(N, 1024)
    spec = pl.BlockSpec((tk, tn), lambda i, j: (i, j))

    def body(c_ref, dm_ref, w_ref, m_ref, v_ref, g_ref, d_ref, mo_ref, vo_ref):
        cv = c_ref[...]
        ca = (cv * jax.nn.sigmoid(cv)).astype(MXU_DTYPE)
        g = _dot(ca, dm_ref[...].astype(MXU_DTYPE), TN)
        g_ref[...] = g
        d_ref[...], mo_ref[...], vo_ref[...] = _adamw_math(w_ref[...], g, m_ref[...], v_ref[...])

    return pl.pallas_call(
        body, name=name, grid=(K // tk, N // tn),
        in_specs=[pl.BlockSpec((B, tk), lambda i, j: (0, i)), pl.BlockSpec((B, tn), lambda i, j: (0, j)), spec, spec, spec],
        out_specs=[spec] * 4, out_shape=[jax.ShapeDtypeStruct((K, N), F32)] * 4,
        compiler_params=_params("parallel", "parallel"))(c_all, dmod_cols, w, m, v)


def _sum_devices(gathered, n_dev, name):
    R = gathered.shape[0] // n_dev
    C = gathered.shape[1]
    tr = _tile(R, 512, 8)
    n_blk = R // tr

    def body(*refs):
        acc = refs[0][...]
        for r in refs[1:n_dev]:
            acc = acc + r[...]
        refs[n_dev][...] = acc

    in_specs = [pl.BlockSpec((tr, C), functools.partial(lambda i, d: (d * n_blk + i, 0), d=d)) for d in range(n_dev)]
    return pl.pallas_call(body, name=name, grid=(n_blk,), in_specs=in_specs,
                          out_specs=pl.BlockSpec((tr, C), lambda i: (i, 0)),
                          out_shape=jax.ShapeDtypeStruct((R, C), F32), compiler_params=_params("parallel"))(*([gathered] * n_dev))


def _place():
    x, y, c = lax.axis_index("x"), lax.axis_index("y"), lax.axis_index("c")
    return x, y, c


def _allgather8(blk, name):
    m_per, n = blk.shape

    def body(x_ref, out_ref, send_sems, recv_sems, local_sem):
        x, y, c = _place()
        me, sibling = (x, y, c), (x, y, 1 - c)
        chips = [(1 - x, y), (x, 1 - y), (1 - x, 1 - y)]

        def rows(px, py, pc):
            return out_ref.at[pl.ds((4 * px + 2 * py + pc) * m_per, m_per), :]

        def copy(k, block, to, src=None):
            return pltpu.make_async_remote_copy(
                src_ref=rows(*block) if src is None else src, dst_ref=rows(*block),
                send_sem=send_sems.at[k], recv_sem=recv_sems.at[k], device_id=to, device_id_type=MESH)

        mine = pltpu.make_async_copy(x_ref, rows(*me), local_sem)
        mine.start()
        first = [copy(0, me, sibling, src=x_ref)]
        first += [copy(1 + j, me, (*chip, c), src=x_ref) for j, chip in enumerate(chips)]
        for cp in first:
            cp.start()
        passed = [copy(4 + j, (*chip, c), sibling) for j, chip in enumerate(chips)]
        for j, chip in enumerate(chips):
            copy(1 + j, (*chip, c), me).wait_recv()
            passed[j].start()
        copy(0, sibling, me).wait_recv()
        for j, chip in enumerate(chips):
            copy(4 + j, (*chip, 1 - c), me).wait_recv()
        for cp in first + passed:
            cp.wait_send()
        mine.wait()

    return pl.pallas_call(
        body, name=name,
        out_shape=jax.ShapeDtypeStruct((8 * m_per, n), blk.dtype),
        in_specs=[pl.BlockSpec(memory_space=pltpu.VMEM)],
        out_specs=pl.BlockSpec(memory_space=pltpu.VMEM),
        scratch_shapes=[pltpu.SemaphoreType.DMA((7,)), pltpu.SemaphoreType.DMA((7,)), pltpu.SemaphoreType.DMA],
        compiler_params=pltpu.CompilerParams(vmem_limit_bytes=V7X_VMEM_LIMIT),
    )(blk)


class _Sharded:
    def __init__(self, shard_shape, by_cols):
        r, c = shard_shape
        self.by_cols = by_cols
        self.full = (r, N_CHIPS * c) if by_cols else (N_CHIPS * r, c)
        self.shard = (r, c)
        self.half_rows = r // 2
        self.half = (r // 2, c)

    def shard_of(self, ref, k):
        r, c = self.shard
        return ref.at[:, pl.ds(k * c, c)] if self.by_cols else ref.at[pl.ds(k * r, r), :]

    def half_of(self, ref, k, hc):
        r, c = self.shard
        h = self.half_rows
        if self.by_cols:
            return ref.at[pl.ds(hc * h, h), pl.ds(k * c, c)]
        return ref.at[pl.ds(k * r + hc * h, h), :]

    def chunk_of(self, ref, k, hc, ch, n):
        r, c = self.shard
        h = self.half_rows
        q = h // n
        if self.by_cols:
            return ref.at[pl.ds(hc * h + ch * q, q), pl.ds(k * c, c)]
        return ref.at[pl.ds(k * r + hc * h + ch * q, q), :]

    def half_of_shard(self, ref, hc):
        return ref.at[pl.ds(hc * self.half_rows, self.half_rows), :]

    def part_of_halves(self, ref, k):
        r, c = self.shard
        h = self.half_rows
        return ref.at[:, pl.ds(k * c, c)] if self.by_cols else ref.at[pl.ds(k * h, h), :]


def _on_each_place(x, y, c, fn, by_chip=True, by_core=True):
    q = 2 * x + y
    for k in range(N_CHIPS if by_chip else 1):
        for cc in range(2 if by_core else 1):
            cond = None
            if by_chip:
                cond = q == k
            if by_core:
                cond = (c == cc) if cond is None else jnp.logical_and(cond, c == cc)
            pl.when(cond)(functools.partial(fn, k, cc))


def _chip_id(k, c):
    return (k // 2, k % 2, c)


def _handshake(peers):
    barrier = pltpu.get_barrier_semaphore()
    for peer in peers:
        pl.semaphore_signal(barrier, inc=1, device_id=peer, device_id_type=MESH)
    pl.semaphore_wait(barrier, len(peers))


def _on_sequencer(body, inputs, out_structs, n_copies, peers_of, name, collective_id, return_inputs=False):
    in_refs = [jax.new_ref(a, memory_space=pltpu.MemorySpace.HBM) for a in inputs]
    out_refs = [jax.empty_ref(s, memory_space=pltpu.MemorySpace.HBM) for s in out_structs]

    @pl.kernel(mesh=plsc.ScalarSubcoreMesh(axis_name="sequencer", num_cores=1), name=name,
               scratch_types=(pltpu.SemaphoreType.DMA((n_copies,)), pltpu.SemaphoreType.DMA((n_copies,))),
               compiler_params=pltpu.CompilerParams(collective_id=collective_id))
    def launch(send_sems, recv_sems):
        x, y, c = _place()
        _handshake(peers_of(x, y, c))
        body(in_refs, out_refs, send_sems, recv_sems, x, y, c)

    launch()
    return [r[...] for r in (in_refs if return_inputs else out_refs)]


def _sibling(x, y, c):
    return [(x, y, 1 - c)]


def _same_core_of_other_chips(x, y, c):
    return [(1 - x, y, c), (x, 1 - y, c), (1 - x, 1 - y, c)]


GATHER_CHUNKS = 4
GATHER_COPIES = 6 * GATHER_CHUNKS


def _allgather8_on_sequencer(blk, name, collective_id):
    m_per, n = blk.shape
    x, y, c = _place()
    placed = lax.dynamic_update_slice(jnp.zeros((8 * m_per, n), blk.dtype), blk, ((4 * x + 2 * y + c) * m_per, 0))

    def body(refs, _, send_sems, recv_sems, x, y, c):
        out_ref, = refs

        def at_place(k, cc):
            def rows(kk, pc):
                return out_ref.at[pl.ds((2 * kk + pc) * m_per, m_per), :]

            def copy(slot, block, to):
                return pltpu.make_async_remote_copy(src_ref=rows(*block), dst_ref=rows(*block), send_sem=send_sems.at[slot],
                                                    recv_sem=recv_sems.at[slot], device_id=to, device_id_type=MESH)

            others = [k ^ flip for flip in FLIPS]
            sends = [copy(0, (k, cc), _chip_id(k, 1 - cc))] + [copy(1 + j, (k, cc), _chip_id(kk, cc)) for j, kk in enumerate(others)]
            for cp in sends:
                cp.start()
            for j, kk in enumerate(others):
                copy(1 + j, (kk, cc), _chip_id(k, cc)).wait_recv()
                cp = copy(4 + j, (kk, cc), _chip_id(k, 1 - cc))
                cp.start()
                sends.append(cp)
            copy(0, (k, 1 - cc), _chip_id(k, cc)).wait_recv()
            for j, kk in enumerate(others):
                copy(4 + j, (kk, 1 - cc), _chip_id(k, cc)).wait_recv()
            for cp in sends:
                cp.wait_send()

        _on_each_place(x, y, c, at_place)

    def peers(x, y, c):
        return _sibling(x, y, c) + _same_core_of_other_chips(x, y, c)

    return _on_sequencer(body, [placed], [], 7, peers, name, collective_id, return_inputs=True)[0]


def _gather_weights(fulls, geoms, name, collective_id):
    n_w = len(fulls)
    n_ch, n_relay = GATHER_CHUNKS, GATHER_CHUNKS // 2
    f_refs = [jax.new_ref(f, memory_space=pltpu.MemorySpace.HBM) for f in fulls]
    FLIP_X, FLIP_Y, FLIP_BOTH = FLIPS
    TO_X, TO_Y, RELAY_TO_Y, RELAY_TO_X, ON_X, ON_Y, ON_DIAG = 0, n_ch, 2 * n_ch, 2 * n_ch + n_relay, 3 * n_ch, 4 * n_ch, 5 * n_ch

    @pl.kernel(mesh=plsc.ScalarSubcoreMesh(axis_name="sequencer", num_cores=1), name=name,
               scratch_types=(pltpu.SemaphoreType.DMA((GATHER_COPIES * n_w,)), pltpu.SemaphoreType.DMA((GATHER_COPIES * n_w,))),
               compiler_params=pltpu.CompilerParams(collective_id=collective_id))
    def launch(send_sems, recv_sems):
        x, y, c = _place()
        _handshake([(x, y, 1 - c), (1 - x, y, c), (x, 1 - y, c)])

        def at_place(k, cc):
            kx, ky, kd = k ^ FLIP_X, k ^ FLIP_Y, k ^ FLIP_BOTH
            me, sibling = _chip_id(k, cc), _chip_id(k, 1 - cc)
            started = []

            def copy(i, slot, src, dst, to, start=True):
                cp = pltpu.make_async_remote_copy(src_ref=src, dst_ref=dst, send_sem=send_sems.at[GATHER_COPIES * i + slot],
                                                  recv_sem=recv_sems.at[GATHER_COPIES * i + slot], device_id=to, device_id_type=MESH)
                if start:
                    cp.start()
                    started.append(cp)
                return cp

            def pass_on(i, slot, ref, to):
                copy(i, slot, ref, ref, to)

            def landed(i, slot, ref):
                copy(i, slot, ref, ref, me, start=False).wait_recv()

            y_order = [(n_relay + s) % n_ch for s in range(n_ch)]
            for i, (g, f_ref) in enumerate(zip(geoms, f_refs)):
                for s in range(n_ch):
                    pass_on(i, TO_X + s, g.chunk_of(f_ref, k, cc, s, n_ch), _chip_id(kx, cc))
                    pass_on(i, TO_Y + y_order[s], g.chunk_of(f_ref, k, cc, y_order[s], n_ch), _chip_id(ky, cc))
            for i, (g, f_ref) in enumerate(zip(geoms, f_refs)):
                for s in range(n_ch):
                    from_x = g.chunk_of(f_ref, kx, cc, s, n_ch)
                    landed(i, TO_X + s, from_x)
                    if s < n_relay:
                        pass_on(i, RELAY_TO_Y + s, from_x, _chip_id(ky, cc))
                    pass_on(i, ON_X + s, from_x, sibling)
                    ch = y_order[s]
                    from_y = g.chunk_of(f_ref, ky, cc, ch, n_ch)
                    landed(i, TO_Y + ch, from_y)
                    if ch >= n_relay:
                        pass_on(i, RELAY_TO_X + ch - n_relay, from_y, _chip_id(kx, cc))
                    pass_on(i, ON_Y + ch, from_y, sibling)
                for r in range(n_relay):
                    via_y = g.chunk_of(f_ref, kd, cc, r, n_ch)
                    landed(i, RELAY_TO_Y + r, via_y)
                    pass_on(i, ON_DIAG + r, via_y, sibling)
                    via_x = g.chunk_of(f_ref, kd, cc, n_relay + r, n_ch)
                    landed(i, RELAY_TO_X + r, via_x)
                    pass_on(i, ON_DIAG + n_relay + r, via_x, sibling)
            for i, (g, f_ref) in enumerate(zip(geoms, f_refs)):
                for slot, kk in ((ON_X, kx), (ON_Y, ky), (ON_DIAG, kd)):
                    for ch in range(n_ch):
                        landed(i, slot + ch, g.chunk_of(f_ref, kk, 1 - cc, ch, n_ch))
            for cp in started:
                cp.wait_send()

        _on_each_place(x, y, c, at_place)

    launch()
    return [f_ref[...] for f_ref in f_refs]


def _swap_core_halves(grads, geoms, name, collective_id):
    n_cp = sum(1 if g.by_cols else N_CHIPS for g in geoms)

    def body(g_refs, t_refs, send_sems, recv_sems, x, y, c):

        def at_place(_, cc):
            def pairs(hc):
                out = []
                for g, g_ref, t_ref in zip(geoms, g_refs, t_refs):
                    if g.by_cols:
                        out.append((g_ref.at[pl.ds(hc * g.half_rows, g.half_rows), :], t_ref))
                    else:
                        out += [(g.half_of(g_ref, k, hc), g.part_of_halves(t_ref, k)) for k in range(N_CHIPS)]
                return out

            sends = [pltpu.make_async_remote_copy(src_ref=src, dst_ref=dst, send_sem=send_sems.at[n],
                                                  recv_sem=recv_sems.at[n], device_id=(x, y, 1 - cc), device_id_type=MESH)
                     for n, (src, dst) in enumerate(pairs(1 - cc))]
            for cp in sends:
                cp.start()
            for n, (src, dst) in enumerate(pairs(cc)):
                pltpu.make_async_remote_copy(src_ref=src, dst_ref=dst, send_sem=send_sems.at[n], recv_sem=recv_sems.at[n],
                                             device_id=(x, y, cc), device_id_type=MESH).wait_recv()
            for cp in sends:
                cp.wait_send()

        _on_each_place(x, y, c, at_place, by_chip=False)

    return _on_sequencer(body, grads, [jax.ShapeDtypeStruct((g.full[0] // 2, g.full[1]), F32) for g in geoms],
                         n_cp, _sibling, name, collective_id)


def _send_to_sibling(buffers, name, collective_id):
    def body(src_refs, dst_refs, send_sems, recv_sems, x, y, c):
        def copy(i):
            return pltpu.make_async_remote_copy(src_ref=src_refs[i], dst_ref=dst_refs[i], send_sem=send_sems.at[i],
                                                recv_sem=recv_sems.at[i], device_id=(x, y, 1 - c), device_id_type=MESH)

        for i in range(len(buffers)):
            copy(i).start()
        for i in range(len(buffers)):
            copy(i).wait()

    return _on_sequencer(body, buffers, [jax.ShapeDtypeStruct(t.shape, t.dtype) for t in buffers], len(buffers),
                         _sibling, name, collective_id)


def _scatter_chip_sums(sums, geoms, name, collective_id):
    def body(s_refs, r_refs, send_sems, recv_sems, x, y, c):

        def at_place(k, _):
            sends = []
            for i, (g, s_ref, r_ref) in enumerate(zip(geoms, s_refs, r_refs)):
                for j, flip in enumerate(FLIPS):
                    kk = k ^ flip
                    cp = pltpu.make_async_remote_copy(
                        src_ref=g.part_of_halves(s_ref, kk), dst_ref=r_ref.at[j], send_sem=send_sems.at[3 * i + j],
                        recv_sem=recv_sems.at[3 * i + j], device_id=(kk // 2, kk % 2, c), device_id_type=MESH)
                    cp.start()
                    sends.append(cp)
            for i, (g, s_ref, r_ref) in enumerate(zip(geoms, s_refs, r_refs)):
                for j in range(len(FLIPS)):
                    pltpu.make_async_remote_copy(
                        src_ref=g.part_of_halves(s_ref, k), dst_ref=r_ref.at[j], send_sem=send_sems.at[3 * i + j],
                        recv_sem=recv_sems.at[3 * i + j], device_id=(x, y, c), device_id_type=MESH).wait_recv()
            for cp in sends:
                cp.wait_send()

        _on_each_place(x, y, c, at_place, by_core=False)

    return _on_sequencer(body, sums, [jax.ShapeDtypeStruct((len(FLIPS),) + g.half, WIRE_DTYPE) for g in geoms],
                         len(FLIPS) * len(sums), _same_core_of_other_chips, name, collective_id)


def _share_reduced_halves(reduced, geoms, name, collective_id):
    def body(out_refs, _, send_sems, recv_sems, x, y, c):

        def at_place(_, cc):
            sends = []
            for i, (g, ref) in enumerate(zip(geoms, out_refs)):
                mine = g.half_of_shard(ref, cc)
                cp = pltpu.make_async_remote_copy(src_ref=mine, dst_ref=mine, send_sem=send_sems.at[i],
                                                  recv_sem=recv_sems.at[i], device_id=(x, y, 1 - cc), device_id_type=MESH)
                cp.start()
                sends.append(cp)
            for i, (g, ref) in enumerate(zip(geoms, out_refs)):
                theirs = g.half_of_shard(ref, 1 - cc)
                pltpu.make_async_remote_copy(src_ref=theirs, dst_ref=theirs, send_sem=send_sems.at[i],
                                             recv_sem=recv_sems.at[i], device_id=(x, y, cc), device_id_type=MESH).wait_recv()
            for cp in sends:
                cp.wait_send()

        _on_each_place(x, y, c, at_place, by_chip=False)

    return _on_sequencer(body, reduced, [], len(reduced), _sibling, name, collective_id, return_inputs=True)


def _chip_sum(place, grad, theirs, g, name):
    RH, C = theirs.shape
    h = g.half_rows
    tr = _tile(h, 256, 16)
    tc = _tile(C, 2048)
    per_half = h // tr

    if g.by_cols:
        grad_map = lambda i, j, p: (p[1] * per_half + i, j)
    else:
        grad_map = lambda i, j, p: ((i // per_half) * 2 * per_half + p[1] * per_half + i % per_half, j)

    def body(p_ref, a_ref, b_ref, f_ref, o_ref):
        total = a_ref[...] + b_ref[...]
        f_ref[...] = total
        o_ref[...] = total.astype(o_ref.dtype)

    return pl.pallas_call(
        body, name=name,
        grid_spec=pltpu.PrefetchScalarGridSpec(
            num_scalar_prefetch=1, grid=(RH // tr, C // tc),
            in_specs=[pl.BlockSpec((tr, tc), grad_map), pl.BlockSpec((tr, tc), lambda i, j, p: (i, j))],
            out_specs=[pl.BlockSpec((tr, tc), lambda i, j, p: (i, j))] * 2),
        out_shape=[jax.ShapeDtypeStruct((RH, C), F32), jax.ShapeDtypeStruct((RH, C), WIRE_DTYPE)],
        compiler_params=_params("parallel", "parallel"),
    )(place, grad, theirs)


def _dw_half(place, a, b, g, mine, name, add=None):
    K, R = a.shape
    C = b.shape[1]
    h = g.half_rows
    tm, tn = _tile(h, 1024, 16), _tile(C, 512)
    per_half = h // tm
    n_i = (R // 2) // tm

    def a_map(i, j, p):
        hc = p[1] if mine else 1 - p[1]
        if g.by_cols:
            return 0, hc * n_i + i
        return 0, (i // per_half) * 2 * per_half + hc * per_half + i % per_half

    mn_spec = pl.BlockSpec((tm, tn), lambda i, j, p: (i, j))

    def body(p_ref, a_ref, b_ref, *rest):
        acc = _dot(a_ref[...], b_ref[...], TN)
        if add is None:
            rest[0][...] = acc
        else:
            total = acc + rest[0][...]
            rest[1][...] = total
            rest[2][...] = total.astype(rest[2].dtype)

    out_shape = [jax.ShapeDtypeStruct((R // 2, C), F32)] + ([] if add is None else [jax.ShapeDtypeStruct((R // 2, C), WIRE_DTYPE)])
    return pl.pallas_call(
        body, name=name,
        grid_spec=pltpu.PrefetchScalarGridSpec(
            num_scalar_prefetch=1, grid=(n_i, C // tn),
            in_specs=[pl.BlockSpec((K, tm), a_map), pl.BlockSpec((K, tn), lambda i, j, p: (0, j))] + ([] if add is None else [mn_spec]),
            out_specs=[mn_spec] * len(out_shape)),
        out_shape=out_shape,
        compiler_params=_params("parallel", "arbitrary"),
    )(place, a, b, *([] if add is None else [add]))


def _reduce_half(place, sums, others, g, name):
    h, tc = g.half
    tr = _tile(h, 256, 16)
    per_half = h // tr
    sums_map = (lambda i, p: (i, p[0])) if g.by_cols else (lambda i, p: (p[0] * per_half + i, 0))

    def body(p_ref, s_ref, o0_ref, o1_ref, o2_ref, out_ref):
        acc = s_ref[...]
        for o_ref in (o0_ref, o1_ref, o2_ref):
            acc = acc + o_ref[...].astype(F32)
        out_ref[...] = acc

    other_specs = [pl.BlockSpec((None, tr, tc), functools.partial(lambda i, p, j: (j, i, 0), j=j)) for j in range(len(FLIPS))]
    return pl.pallas_call(
        body, name=name,
        grid_spec=pltpu.PrefetchScalarGridSpec(
            num_scalar_prefetch=1, grid=(per_half,),
            in_specs=[pl.BlockSpec((tr, tc), sums_map)] + other_specs,
            out_specs=pl.BlockSpec((tr, tc), lambda i, p: (p[1] * per_half + i, 0))),
        out_shape=jax.ShapeDtypeStruct(g.shard, F32),
        compiler_params=_params("arbitrary"),
    )(place, sums, others, others, others)


SMALL = ("b_ada", "norm1_g", "v_norm_g", "w_spatial", "b_spatial", "out_norm_g", "norm2_g", "final_g")
BIG = ("w_in", "w_out", "w_gate", "w_up", "w_down")
BY_COLS = {"w_in": True, "w_out": False, "w_gate": True, "w_up": True, "w_down": False}
ORDER = ("w_ada", "b_ada", "norm1_g", "w_in", "v_norm_g", "w_spatial", "b_spatial", "out_norm_g", "w_out",
         "norm2_g", "w_gate", "w_up", "w_down", "final_g")


def _pack(parts):
    return jnp.concatenate([parts[n].reshape(-1) for n in SMALL]).reshape(-1, LANE)


def _unpack(slab, shapes):
    flat = slab.reshape(-1)
    out, at = {}, 0
    for n in SMALL:
        size = math.prod(shapes[n])
        out[n] = flat[at:at + size].reshape(shapes[n])
        at += size
    return out


def kernel(x, c, w_ada, b_ada, norm1_g, w_in, v_norm_g, w_spatial, b_spatial, out_norm_g, w_out, norm2_g, w_gate, w_up, w_down, final_g, loss_target, m_w_ada, m_b_ada, m_norm1_g, m_w_in, m_v_norm_g, m_w_spatial, m_b_spatial, m_out_norm_g, m_w_out, m_norm2_g, m_w_gate, m_w_up, m_w_down, m_final_g, v_w_ada, v_b_ada, v_norm1_g, v_w_in, v_v_norm_g, v_w_spatial, v_b_spatial, v_out_norm_g, v_w_out, v_norm2_g, v_w_gate, v_w_up, v_w_down, v_final_g):
    weights = dict(w_ada=w_ada, b_ada=b_ada, norm1_g=norm1_g, w_in=w_in, v_norm_g=v_norm_g, w_spatial=w_spatial,
                   b_spatial=b_spatial, out_norm_g=out_norm_g, w_out=w_out, norm2_g=norm2_g, w_gate=w_gate, w_up=w_up,
                   w_down=w_down, final_g=final_g)
    m_in = dict(w_ada=m_w_ada, b_ada=m_b_ada, norm1_g=m_norm1_g, w_in=m_w_in, v_norm_g=m_v_norm_g, w_spatial=m_w_spatial,
                b_spatial=m_b_spatial, out_norm_g=m_out_norm_g, w_out=m_w_out, norm2_g=m_norm2_g, w_gate=m_w_gate,
                w_up=m_w_up, w_down=m_w_down, final_g=m_final_g)
    v_in = dict(w_ada=v_w_ada, b_ada=v_b_ada, norm1_g=v_norm1_g, w_in=v_w_in, v_norm_g=v_v_norm_g, w_spatial=v_w_spatial,
                b_spatial=v_b_spatial, out_norm_g=v_out_norm_g, w_out=v_w_out, norm2_g=v_norm2_g, w_gate=v_w_gate,
                w_up=v_w_up, w_down=v_w_down, final_g=v_final_g)

    S, D = x.shape[1], x.shape[2]
    n_g = v_norm_g.shape[-1] // LANE
    n_h = (D - n_g * LANE) // LANE
    GW = n_g * LANE
    xi, yi, ci = _place()
    chip = 2 * xi + yi
    me = 4 * xi + 2 * yi + ci
    place = jnp.stack([chip, ci]).astype(jnp.int32)

    xs, target = x[0], loss_target[0]
    geoms = [_Sharded(weights[n].shape[1:], BY_COLS[n]) for n in BIG]

    full = {}
    for i, group in enumerate((("w_in",), ("w_out",), ("w_gate", "w_up"), ("w_down",))):
        gg = [geoms[BIG.index(n)] for n in group]
        own = [_cast_into_full(place, weights[n][0], g, "cast_" + n) for n, g in zip(group, gg)]
        gathered = _gather_weights(own, gg, "gather_" + "_".join(group), 1 + i)
        full.update(zip(group, gathered))

    c_pad = jnp.concatenate([c, jnp.zeros((7, D), F32)], axis=0)
    c_all = _allgather8(c_pad, "gather_c")[::8]
    n_ada = w_ada.shape[2]
    b_cols = lax.dynamic_slice(b_ada, (0, chip * n_ada), (1, n_ada))
    mod_parts = _allgather8(_mod_part(c_all, w_ada[0], b_cols, "mod_part"), "gather_mod")
    mod_all = mod_parts.reshape(N_CHIPS, 2, 8, n_ada)[:, 0].transpose(1, 0, 2).reshape(8, N_CHIPS * n_ada)
    mod = lax.dynamic_slice(mod_all, (me, 0), (1, 6 * D))
    shift1, scale1, gate1, shift2, scale2, gate2 = [mod[:, i * D:(i + 1) * D] for i in range(6)]

    b_t = b_spatial[0].T
    h1 = _norm_mod(xs, norm1_g, scale1, shift1, "norm1")
    proj, = _mm("nn", h1, full["w_in"], [F32], "proj")
    on_gm = _gmlp_fwd(proj, v_norm_g, w_spatial[0], b_t, out_norm_g, n_g, "gmlp_fwd")
    o_sb, on_sb, l_sum = _sb_fwd(proj, out_norm_g, n_g, n_h, "sb_fwd")
    o_n = jnp.concatenate([on_gm, on_sb], axis=1)
    attn, = _mm("nn", o_n, full["w_out"], [F32], "attn_out")
    x1, h2 = _residual_norm_mod(xs, attn, gate1, norm2_g, scale2, shift2, "norm2")
    a_g, a_u, f_in = _gate_up(h2, full["w_gate"], full["w_up"], "gate_up")
    f, = _mm("nn", f_in, full["w_down"], [F32], "down", tm=1024)
    dx2, df, d_gate2, d_final_g, loss_part = _final_loss_bwd(x1, f, gate2, final_g.reshape(1, D), target, "final")
    loss = lax.psum(loss_part[0, 0], ("x", "y", "c"))

    geom_of = dict(zip(BIG, geoms))
    grad_out, delta, new_m, new_v = {}, {}, {}, {}

    def theirs_first(group, operands, collective_id, after=None):
        outs = []
        for n, (a_op, b_op) in zip(group, operands):
            outs.append(_dw_half(place, a_op, b_op if after is None else _then(after, b_op), geom_of[n], False, "d_" + n + "_theirs")[0])
            after = outs[-1]
        return _send_to_sibling(outs, "swap_" + "_".join(group), collective_id)

    def chip_sums(group, operands, theirs, after):
        f32s, wires = [], []
        for n, (a_op, b_op), t in zip(group, operands, theirs):
            sf, sw = _dw_half(place, a_op, b_op, geom_of[n], True, "d_" + n + "_mine", add=_then(after, t))
            f32s.append(sf)
            wires.append(sw)
            after = sw
        return f32s, wires

    def scatter(group, sums, collective_id):
        return _scatter_chip_sums(sums, [geom_of[n] for n in group], "scatter_" + "_".join(group), collective_id)

    def reduce_halves(group, sums, others, after):
        return [_reduce_half(place, sf, _then(after, o), geom_of[n], "reduce_" + n) for n, sf, o in zip(group, sums, others)]

    def share(group, halves, collective_id):
        return _share_reduced_halves(halves, [geom_of[n] for n in group], "share_" + "_".join(group), collective_id)

    def adamw(group, reduced, after):
        for n, r in zip(group, reduced):
            go, d, mo, vo = _adamw(weights[n][0], _then(after, r), m_in[n][0], v_in[n][0], "adamw_" + n)
            grad_out[n], delta[n], new_m[n], new_v[n] = go[None], d[None], mo[None], vo[None]
        return d

    g_down = ("w_down",)
    g_ffn = ("w_gate", "w_up")
    g_out = ("w_out",)
    g_in = ("w_in",)

    gr_down, = _mm("tn", f_in, df, [F32], "d_w_down", tm=1408, tn=1024)
    th_down, = _swap_core_halves([gr_down], [geom_of["w_down"]], "swap_w_down", 6)
    d_ag, d_au = _mm("nt", df, full["w_down"], [MXU_DTYPE, MXU_DTYPE], "d_ffn_in", extras=(a_g, a_u),
                     epilogue=_swiglu_bwd_epilogue)
    sf_down, sw_down = [[t] for t in _chip_sum(place, gr_down, _then(d_ag, th_down), geom_of["w_down"], "chip_sum_w_down")]
    ot_down = scatter(g_down, sw_down, 7)
    th_ffn = theirs_first(g_ffn, [(h2, d_ag), (h2, d_au)], 9, after=sw_down)
    dh2 = _mm_ktiled("nt", [(_then(th_ffn, d_ag), full["w_gate"]), (d_au, full["w_up"])], "d_h2", tn=512)
    sf_ffn, sw_ffn = chip_sums(g_ffn, [(h2, d_ag), (h2, d_au)], th_ffn, after=dh2)
    ot_ffn = scatter(g_ffn, sw_ffn, 10)
    hv_down = reduce_halves(g_down, sf_down, ot_down, after=sw_ffn)
    rd_down = share(g_down, hv_down, 8)
    dx1, d_shift2, d_scale2, d_norm2_g, d_gate1, d_attn = _norm_mod_bwd(
        _then(hv_down, dh2), x1, dx2, norm2_g, scale2, "norm2_bwd", branch=attn, gate=gate1)
    th_out = theirs_first(g_out, [(o_n, d_attn)], 12)
    d_on, = _mm("nt", _then(th_out, d_attn), full["w_out"], [F32], "d_o")
    dp_gm, d_w_spatial, d_b_t, d_v_norm_g, d_og_gm = _gmlp_bwd(proj, d_on, v_norm_g, w_spatial[0], b_t, out_norm_g, n_g, "gmlp_bwd")
    sf_out, sw_out = chip_sums(g_out, [(o_n, d_attn)], th_out, after=dp_gm)
    ot_out = scatter(g_out, sw_out, 13)
    dq, dk, dv, d_og_sb = _sb_bwd(proj, o_sb, l_sum, _then(sw_out, d_on), out_norm_g, n_g, n_h, "sb_bwd")
    hv_ffn = reduce_halves(g_ffn, sf_ffn, ot_ffn, after=dq)
    rd_ffn = share(g_ffn, hv_ffn, 11)
    dproj = jnp.concatenate([_then(hv_ffn, dp_gm), dq, dk, dv], axis=1)
    th_in = theirs_first(g_in, [(h1, dproj)], 15)
    dh1, = _mm("nt", _then(th_in, dproj), full["w_in"], [F32], "d_h1", tm=1024)
    sf_in, sw_in = chip_sums(g_in, [(h1, dproj)], th_in, after=dh1)
    ot_in = scatter(g_in, sw_in, 16)
    hv_out = reduce_halves(g_out, sf_out, ot_out, after=sw_in)
    rd_out = share(g_out, hv_out, 14)
    grad_x, d_shift1, d_scale1, d_norm1_g = _norm_mod_bwd(_then(hv_out, dh1), xs, dx1, norm1_g, scale1, "norm1_bwd")

    dmod = jnp.concatenate([d_shift1, d_scale1, d_gate1, d_shift2, d_scale2, d_gate2], axis=1)
    small_parts = dict(b_ada=dmod, norm1_g=d_norm1_g, v_norm_g=d_v_norm_g, w_spatial=d_w_spatial, b_spatial=d_b_t.T,
                       out_norm_g=jnp.concatenate([d_og_gm, d_og_sb], axis=1), norm2_g=d_norm2_g, final_g=d_final_g)
    slab = _then(grad_x, _pack(small_parts))
    rows = slab.shape[0]
    gathered = _allgather8_on_sequencer(slab, "gather_small", 18)
    done = adamw(g_down, rd_down, after=slab)
    done = adamw(g_ffn, rd_ffn, after=done)
    done = adamw(g_out, rd_out, after=done)
    gathered = _then(done, gathered)
    small_shapes = {n: weights[n].shape for n in SMALL}
    small_sum = _sum_devices(gathered, 8, "sum_small")
    dmod_all = gathered.reshape(8, rows * LANE)[:, :6 * D]
    dmod_cols = lax.dynamic_slice(dmod_all, (0, chip * n_ada), (8, n_ada))
    g_ada, d, mo, vo = _adamw_ada(c_all, dmod_cols, w_ada[0], m_w_ada[0], v_w_ada[0], "adamw_w_ada")
    grad_out["w_ada"], delta["w_ada"], new_m["w_ada"], new_v["w_ada"] = g_ada[None], d[None], mo[None], vo[None]
    gs_small, d_small, mo, vo = _adamw(_pack({n: weights[n] for n in SMALL}), small_sum, _pack({n: m_in[n] for n in SMALL}),
                                       _pack({n: v_in[n] for n in SMALL}), "adamw_small")
    for dst, slab_out in ((grad_out, gs_small), (delta, d_small), (new_m, mo), (new_v, vo)):
        dst.update(_unpack(slab_out, small_shapes))
    hv_in = reduce_halves(g_in, sf_in, ot_in, after=d)
    adamw(g_in, share(g_in, hv_in, 17), after=d)

    return (loss, grad_x[None], *[grad_out[n] for n in ORDER], *[delta[n] for n in ORDER],
            *[new_m[n] for n in ORDER], *[new_v[n] for n in ORDER])
```

```python
import functools
import math

import jax
import jax.numpy as jnp
from jax import lax
from jax.experimental import pallas as pl
from jax.experimental.pallas import tpu as pltpu
from jax.experimental.pallas import tpu_sc as plsc

F32 = jnp.float32
MXU_DTYPE = jnp.bfloat16
WIRE_DTYPE = jnp.bfloat16
EPS = 1e-6
LANE = 128
V7X_VMEM_LIMIT = 56 * 1024 * 1024
MESH = pl.DeviceIdType.MESH
N_CHIPS = 4
FLIPS = (2, 1, 3)

ADAM_LR = 0.001
ADAM_B1 = 0.9
ADAM_B2 = 0.999
ADAM_EPS = 1e-08
ADAM_WD = 0.01
ADAM_STEP = 10


def _params(*semantics):
    return pltpu.CompilerParams(dimension_semantics=semantics or None, vmem_limit_bytes=V7X_VMEM_LIMIT)


def _tile(dim, pref, unit=LANE):
    best = None
    t = unit
    while t <= min(dim, pref):
        if dim % t == 0:
            best = t
        t += unit
    return best if best is not None else dim


def _then(first, second):
    return lax.optimization_barrier((first, second))[1]


def _sum0(v):
    return jnp.sum(v, axis=0, keepdims=True)


def _mean1(v):
    return jnp.mean(v, axis=-1, keepdims=True)


def _gelu(x):
    return 0.5 * x * (1.0 + lax.erf(x * (1.0 / math.sqrt(2.0))))


def _gelu_grad(x):
    cdf = 0.5 * (1.0 + lax.erf(x * (1.0 / math.sqrt(2.0))))
    return cdf + x * jnp.exp(-0.5 * x * x) * (1.0 / math.sqrt(2.0 * math.pi))


def _dot(a, b, dims):
    return lax.dot_general(a, b, (dims, ((), ())), preferred_element_type=F32)


NN = ((1,), (0,))
NT = ((1,), (1,))
TN = ((0,), (0,))


def _mm(kind, a, b, out_dtypes, name, tm=2048, tn=512, extras=(), epilogue=None):
    if kind == "nn":
        (M, K), N = a.shape, b.shape[1]
    elif kind == "nt":
        (M, K), N = a.shape, b.shape[0]
    else:
        (K, M), N = a.shape, b.shape[1]
    tm, tn = _tile(M, tm), _tile(N, tn)
    a_spec = pl.BlockSpec((K, tm), lambda i, j: (0, i)) if kind == "tn" else pl.BlockSpec((tm, K), lambda i, j: (i, 0))
    b_spec = pl.BlockSpec((tn, K), lambda i, j: (j, 0)) if kind == "nt" else pl.BlockSpec((K, tn), lambda i, j: (0, j))
    mn_spec = pl.BlockSpec((tm, tn), lambda i, j: (i, j))
    dims = {"nn": NN, "nt": NT, "tn": TN}[kind]
    n_extra = len(extras)

    n_chunks = 1 if epilogue is None or kind == "tn" else max(1, tm // 512)
    rows_per = tm // n_chunks

    def body(a_ref, b_ref, *rest):
        for r in range(n_chunks):
            rows = slice(r * rows_per, (r + 1) * rows_per)
            acc = _dot(a_ref[...] if n_chunks == 1 else a_ref[rows, :], b_ref[...], dims)
            res = (acc,) if epilogue is None else epilogue(acc, *[e[rows, :] for e in rest[:n_extra]])
            for o_ref, val in zip(rest[n_extra:], res):
                o_ref[rows, :] = val.astype(o_ref.dtype)

    outs = pl.pallas_call(
        body, name=name, grid=(M // tm, N // tn),
        in_specs=[a_spec, b_spec] + [mn_spec] * n_extra,
        out_specs=[mn_spec] * len(out_dtypes),
        out_shape=[jax.ShapeDtypeStruct((M, N), d) for d in out_dtypes],
        compiler_params=_params("parallel", "arbitrary"),
    )(a, b, *extras)
    return outs


def _mm_ktiled(kind, pairs, name, tm=2048, tn=1024, tk=1408):
    a0, b0 = pairs[0]
    M, K = a0.shape
    N = b0.shape[1] if kind == "nn" else b0.shape[0]
    tm, tn, tk = _tile(M, tm), _tile(N, tn), _tile(K, tk)
    a_spec = pl.BlockSpec((tm, tk), lambda i, j, k: (i, k))
    b_spec = pl.BlockSpec((tk, tn), lambda i, j, k: (k, j)) if kind == "nn" else pl.BlockSpec((tn, tk), lambda i, j, k: (j, k))
    dims = NN if kind == "nn" else NT
    n_pairs = len(pairs)

    def body(*refs):
        o_ref = refs[2 * n_pairs]
        acc = _dot(refs[0][...], refs[1][...], dims)
        for p in range(1, n_pairs):
            acc = acc + _dot(refs[2 * p][...], refs[2 * p + 1][...], dims)

        @pl.when(pl.program_id(2) == 0)
        def _():
            o_ref[...] = acc

        @pl.when(pl.program_id(2) != 0)
        def _():
            o_ref[...] += acc

    return pl.pallas_call(
        body, name=name, grid=(M // tm, N // tn, K // tk),
        in_specs=[a_spec, b_spec] * n_pairs,
        out_specs=pl.BlockSpec((tm, tn), lambda i, j, k: (i, j)),
        out_shape=jax.ShapeDtypeStruct((M, N), F32),
        compiler_params=_params("parallel", "parallel", "arbitrary"),
    )(*[x for pair in pairs for x in pair])


def _gate_up(h, wg, wu, name):
    (M, K), N = h.shape, wg.shape[1]
    tm, tn = _tile(M, 2048), _tile(N, 512)

    n_chunks = max(1, tm // 512)
    rows_per = tm // n_chunks

    def body(h_ref, wg_ref, wu_ref, ag_ref, au_ref, f_ref):
        for r in range(n_chunks):
            rows = slice(r * rows_per, (r + 1) * rows_per)
            hv = h_ref[rows, :]
            ag = _dot(hv, wg_ref[...], NN)
            au = _dot(hv, wu_ref[...], NN)
            ag_ref[rows, :] = ag.astype(ag_ref.dtype)
            au_ref[rows, :] = au.astype(au_ref.dtype)
            f_ref[rows, :] = (ag * jax.nn.sigmoid(ag) * au).astype(f_ref.dtype)

    w_spec = pl.BlockSpec((K, tn), lambda i, j: (0, j))
    mn_spec = pl.BlockSpec((tm, tn), lambda i, j: (i, j))
    return pl.pallas_call(
        body, name=name, grid=(M // tm, N // tn),
        in_specs=[pl.BlockSpec((tm, K), lambda i, j: (i, 0)), w_spec, w_spec],
        out_specs=[mn_spec] * 3,
        out_shape=[jax.ShapeDtypeStruct((M, N), MXU_DTYPE)] * 3,
        compiler_params=_params("parallel", "arbitrary"),
    )(h, wg, wu)


def _swiglu_bwd_epilogue(dfin, ag, au):
    ag, au = ag.astype(F32), au.astype(F32)
    sg = jax.nn.sigmoid(ag)
    d_au = dfin * (ag * sg)
    d_ag = dfin * au * (sg * (1.0 + ag * (1.0 - sg)))
    return d_ag, d_au


def _row_specs(ts, width):
    return pl.BlockSpec((ts, width), lambda i: (i, 0)), pl.BlockSpec((1, width), lambda i: (0, 0))


def _cast_into_full(place, shard, g, name):
    R, C = shard.shape
    tr = _tile(R, 256, 16)
    n_blk = R // tr
    out_map = (lambda i, p: (i, p[0])) if g.by_cols else (lambda i, p: (p[0] * n_blk + i, 0))

    def body(p_ref, a_ref, o_ref):
        o_ref[...] = a_ref[...].astype(o_ref.dtype)

    return pl.pallas_call(
        body, name=name,
        grid_spec=pltpu.PrefetchScalarGridSpec(
            num_scalar_prefetch=1, grid=(n_blk,),
            in_specs=[pl.BlockSpec((tr, C), lambda i, p: (i, 0))],
            out_specs=pl.BlockSpec((tr, C), out_map)),
        out_shape=jax.ShapeDtypeStruct(g.full, WIRE_DTYPE),
        compiler_params=_params("arbitrary"),
    )(place, shard)


def _norm_mod(x, g, scale, shift, name):
    S, D = x.shape
    ts = _tile(S, 256, 16)
    tile, vec = _row_specs(ts, D)

    def body(x_ref, g_ref, sc_ref, sh_ref, h_ref):
        xv = x_ref[...]
        r = lax.rsqrt(_mean1(xv * xv) + EPS)
        h_ref[...] = ((xv * r) * g_ref[...] * (1.0 + sc_ref[...]) + sh_ref[...]).astype(h_ref.dtype)

    return pl.pallas_call(body, name=name, grid=(S // ts,), in_specs=[tile, vec, vec, vec], out_specs=tile,
                          out_shape=jax.ShapeDtypeStruct((S, D), MXU_DTYPE), compiler_params=_params("parallel"))(x, g, scale, shift)


def _residual_norm_mod(x, attn, gate, g, scale, shift, name):
    S, D = x.shape
    ts = _tile(S, 256, 16)
    tile, vec = _row_specs(ts, D)

    def body(x_ref, a_ref, gate_ref, g_ref, sc_ref, sh_ref, x1_ref, h_ref):
        x1 = x_ref[...] + gate_ref[...] * a_ref[...]
        x1_ref[...] = x1
        r = lax.rsqrt(_mean1(x1 * x1) + EPS)
        h_ref[...] = ((x1 * r) * g_ref[...] * (1.0 + sc_ref[...]) + sh_ref[...]).astype(h_ref.dtype)

    return pl.pallas_call(body, name=name, grid=(S // ts,), in_specs=[tile, tile, vec, vec, vec, vec],
                          out_specs=[tile, tile],
                          out_shape=[jax.ShapeDtypeStruct((S, D), F32), jax.ShapeDtypeStruct((S, D), MXU_DTYPE)],
                          compiler_params=_params("parallel"))(x, attn, gate, g, scale, shift)


def _final_loss_bwd(x1, f, gate2, final_g, target, name):
    S, D = x1.shape
    ts = _tile(S, 256, 16)
    tile, vec = _row_specs(ts, D)
    loss_spec = pl.BlockSpec((1, LANE), lambda i: (0, 0))

    def body(x1_ref, f_ref, gate_ref, g_ref, t_ref, dx2_ref, df_ref, dgate_ref, dg_ref, loss_ref):
        @pl.when(pl.program_id(0) == 0)
        def _():
            dgate_ref[...] = jnp.zeros_like(dgate_ref)
            dg_ref[...] = jnp.zeros_like(dg_ref)
            loss_ref[...] = jnp.zeros_like(loss_ref)

        fv, gate, g = f_ref[...], gate_ref[...], g_ref[...]
        x2 = x1_ref[...] + gate * fv
        r = lax.rsqrt(_mean1(x2 * x2) + EPS)
        xn = x2 * r
        err = xn * g - t_ref[...]
        loss_ref[...] += jnp.broadcast_to(0.5 * _sum0(_mean1(err * err)), loss_ref.shape)
        dy = err * (1.0 / D)
        dg_ref[...] += _sum0(dy * xn)
        dxn = dy * g
        dx2 = r * (dxn - xn * _mean1(dxn * xn))
        dx2_ref[...] = dx2
        dgate_ref[...] += _sum0(dx2 * fv)
        df_ref[...] = (dx2 * gate).astype(df_ref.dtype)

    return pl.pallas_call(
        body, name=name, grid=(S // ts,), in_specs=[tile, tile, vec, vec, tile],
        out_specs=[tile, tile, vec, vec, loss_spec],
        out_shape=[jax.ShapeDtypeStruct((S, D), F32), jax.ShapeDtypeStruct((S, D), MXU_DTYPE),
                   jax.ShapeDtypeStruct((1, D), F32), jax.ShapeDtypeStruct((1, D), F32),
                   jax.ShapeDtypeStruct((1, LANE), F32)],
        compiler_params=_params("arbitrary"),
    )(x1, f, gate2, final_g, target)


def _norm_mod_bwd(dh, xin, dres, g, scale, name, branch=None, gate=None):
    S, D = xin.shape
    ts = _tile(S, 256, 16)
    tile, vec = _row_specs(ts, D)
    with_gate = branch is not None

    def body(*refs):
        if with_gate:
            dh_ref, x_ref, dres_ref, g_ref, sc_ref, br_ref, gate_ref, dx_ref, dshift_ref, dscale_ref, dg_ref, dgate_ref, dbr_ref = refs
            accs = (dshift_ref, dscale_ref, dg_ref, dgate_ref)
        else:
            dh_ref, x_ref, dres_ref, g_ref, sc_ref, dx_ref, dshift_ref, dscale_ref, dg_ref = refs
            accs = (dshift_ref, dscale_ref, dg_ref)

        @pl.when(pl.program_id(0) == 0)
        def _():
            for acc in accs:
                acc[...] = jnp.zeros_like(acc)

        dh_v, xv, g_v = dh_ref[...], x_ref[...], g_ref[...]
        one_sc = 1.0 + sc_ref[...]
        r = lax.rsqrt(_mean1(xv * xv) + EPS)
        xn = xv * r
        dshift_ref[...] += _sum0(dh_v)
        dscale_ref[...] += _sum0(dh_v * (xn * g_v))
        dg_ref[...] += _sum0(dh_v * one_sc * xn)
        dxn = dh_v * (g_v * one_sc)
        dx = dres_ref[...] + r * (dxn - xn * _mean1(dxn * xn))
        dx_ref[...] = dx
        if with_gate:
            dgate_ref[...] += _sum0(dx * br_ref[...])
            dbr_ref[...] = (dx * gate_ref[...]).astype(dbr_ref.dtype)

    ins = [dh, xin, dres, g, scale] + ([branch, gate] if with_gate else [])
    in_specs = [tile, tile, tile, vec, vec] + ([tile, vec] if with_gate else [])
    out_specs = [tile, vec, vec, vec] + ([vec, tile] if with_gate else [])
    out_shape = [jax.ShapeDtypeStruct((S, D), F32)] + [jax.ShapeDtypeStruct((1, D), F32)] * 3
    if with_gate:
        out_shape += [jax.ShapeDtypeStruct((1, D), F32), jax.ShapeDtypeStruct((S, D), MXU_DTYPE)]
    return pl.pallas_call(body, name=name, grid=(S // ts,), in_specs=in_specs, out_specs=out_specs,
                          out_shape=out_shape, compiler_params=_params("arbitrary"))(*ins)


def _causal_weights(ws_ref, wt_ref, n_g):
    row = lax.broadcasted_iota(jnp.int32, (LANE, LANE), 0)
    col = lax.broadcasted_iota(jnp.int32, (LANE, LANE), 1)
    for g in range(n_g):
        wt_ref[g] = jnp.where(col <= row, ws_ref[g], 0.0).astype(wt_ref.dtype)


def _group_layernorm(v):
    xc = v - _mean1(v)
    rstd = lax.rsqrt(_mean1(xc * xc) + EPS)
    return xc * rstd, rstd


def _gmlp_fwd(proj, v_gain, w_s, b_t, out_gain, n_g, name):
    S = proj.shape[0]
    GW = n_g * LANE

    def body(p_ref, vg_ref, ws_ref, bt_ref, og_ref, on_ref, wt_ref):
        @pl.when(pl.program_id(0) == 0)
        def _():
            _causal_weights(ws_ref, wt_ref, n_g)

        for g in range(n_g):
            cols = slice(g * LANE, (g + 1) * LANE)
            u = _gelu(p_ref[:, cols])
            v = _gelu(p_ref[:, GW + g * LANE:GW + (g + 1) * LANE])
            vhat, _ = _group_layernorm(v)
            vln = (vhat * vg_ref[:, cols]).astype(MXU_DTYPE)
            mixed = _dot(wt_ref[g], vln, NN) + bt_ref[:, g:g + 1]
            o = u * mixed
            r = lax.rsqrt(_mean1(o * o) + EPS)
            on_ref[:, cols] = (o * r * og_ref[:, cols]).astype(on_ref.dtype)

    return pl.pallas_call(
        body, name=name, grid=(S // LANE,),
        in_specs=[pl.BlockSpec((LANE, 2 * GW), lambda n: (n, 0)),
                  pl.BlockSpec((1, GW), lambda n: (0, 0)),
                  pl.BlockSpec((n_g, LANE, LANE), lambda n: (0, 0, 0)),
                  pl.BlockSpec((LANE, n_g), lambda n: (0, 0)),
                  pl.BlockSpec((1, GW), lambda n: (0, 0))],
        out_specs=pl.BlockSpec((LANE, GW), lambda n: (n, 0)),
        out_shape=jax.ShapeDtypeStruct((S, GW), MXU_DTYPE),
        scratch_shapes=[pltpu.VMEM((n_g, LANE, LANE), MXU_DTYPE)],
        compiler_params=_params("arbitrary"),
    )(proj, v_gain, w_s, b_t, out_gain)


def _gmlp_bwd(proj, d_on, v_gain, w_s, b_t, out_gain, n_g, name):
    S = proj.shape[0]
    GW = n_g * LANE

    def body(p_ref, dn_ref, vg_ref, ws_ref, bt_ref, og_ref, dp_ref, dws_ref, dbt_ref, dvg_ref, dog_ref, wt_ref):
        @pl.when(pl.program_id(0) == 0)
        def _():
            _causal_weights(ws_ref, wt_ref, n_g)
            dws_ref[...] = jnp.zeros_like(dws_ref)
            dbt_ref[...] = jnp.zeros_like(dbt_ref)
            dvg_ref[...] = jnp.zeros_like(dvg_ref)
            dog_ref[...] = jnp.zeros_like(dog_ref)

        row = lax.broadcasted_iota(jnp.int32, (LANE, LANE), 0)
        col = lax.broadcasted_iota(jnp.int32, (LANE, LANE), 1)
        for g in range(n_g):
            cols = slice(g * LANE, (g + 1) * LANE)
            vcols = slice(GW + g * LANE, GW + (g + 1) * LANE)
            pu, pv = p_ref[:, cols], p_ref[:, vcols]
            u, v = _gelu(pu), _gelu(pv)
            vhat, rstd = _group_layernorm(v)
            gain = vg_ref[:, cols]
            vln = (vhat * gain).astype(MXU_DTYPE)
            mixed = _dot(wt_ref[g], vln, NN) + bt_ref[:, g:g + 1]
            o = u * mixed
            r = lax.rsqrt(_mean1(o * o) + EPS)
            oh = o * r
            dn = dn_ref[:, cols]
            dog_ref[:, cols] += _sum0(dn * oh)
            dhn = dn * og_ref[:, cols]
            d_o = r * (dhn - oh * _mean1(dhn * oh))
            du = d_o * mixed
            dmix = d_o * u
            dbt_ref[:, g:g + 1] += jnp.sum(dmix, axis=1, keepdims=True)
            dmix_b = dmix.astype(MXU_DTYPE)
            dws_ref[g] += jnp.where(col <= row, _dot(dmix_b, vln, NT), 0.0)
            dvln = _dot(wt_ref[g], dmix_b, TN)
            dvg_ref[:, cols] += _sum0(dvln * vhat)
            dxh = dvln * gain
            dv = rstd * (dxh - _mean1(dxh) - vhat * _mean1(dxh * vhat))
            dp_ref[:, cols] = (du * _gelu_grad(pu)).astype(dp_ref.dtype)
            dp_ref[:, vcols] = (dv * _gelu_grad(pv)).astype(dp_ref.dtype)

    return pl.pallas_call(
        body, name=name, grid=(S // LANE,),
        in_specs=[pl.BlockSpec((LANE, 2 * GW), lambda n: (n, 0)),
                  pl.BlockSpec((LANE, GW), lambda n: (n, 0)),
                  pl.BlockSpec((1, GW), lambda n: (0, 0)),
                  pl.BlockSpec((n_g, LANE, LANE), lambda n: (0, 0, 0)),
                  pl.BlockSpec((LANE, n_g), lambda n: (0, 0)),
                  pl.BlockSpec((1, GW), lambda n: (0, 0))],
        out_specs=[pl.BlockSpec((LANE, 2 * GW), lambda n: (n, 0)),
                   pl.BlockSpec((n_g, LANE, LANE), lambda n: (0, 0, 0)),
                   pl.BlockSpec((LANE, n_g), lambda n: (0, 0)),
                   pl.BlockSpec((1, GW), lambda n: (0, 0)),
                   pl.BlockSpec((1, GW), lambda n: (0, 0))],
        out_shape=[jax.ShapeDtypeStruct((S, 2 * GW), MXU_DTYPE),
                   jax.ShapeDtypeStruct((n_g, LANE, LANE), F32),
                   jax.ShapeDtypeStruct((LANE, n_g), F32),
                   jax.ShapeDtypeStruct((1, GW), F32),
                   jax.ShapeDtypeStruct((1, GW), F32)],
        scratch_shapes=[pltpu.VMEM((n_g, LANE, LANE), MXU_DTYPE)],
        compiler_params=_params("arbitrary"),
    )(proj, d_on, v_gain, w_s, b_t, out_gain)


def _tri_sum(v, tri, exact=True):
    hi = v.astype(MXU_DTYPE)
    if not exact:
        return _dot(hi, tri, NN)
    lo = (v - hi.astype(F32)).astype(MXU_DTYPE)
    return _dot(hi, tri, NN) + _dot(lo, tri, NN)


def _log_sigmoids(z):
    sp = jnp.log(1.0 + jnp.exp(-jnp.abs(z)))
    return jnp.minimum(z, 0.0) - sp, jnp.minimum(-z, 0.0) - sp


def _rows(i, size):
    return pl.ds(pl.multiple_of(i * size, size), size)


SB_QUERY_TILE = 1024
SB_KEY_TILE = 256


def _sb_tiles(S):
    tq = _tile(S, SB_QUERY_TILE)
    tk = _tile(tq, SB_KEY_TILE)
    assert (tq // tk) % 2 == 0, "the key sweep takes two blocks a pass"
    return tq, tk, S // tq, tq // tk


def _triangle(n, keep):
    row = lax.broadcasted_iota(jnp.int32, (n, n), 0)
    col = lax.broadcasted_iota(jnp.int32, (n, n), 1)
    return jnp.where(keep(row, col), 1.0, 0.0).astype(MXU_DTYPE)


def _strictly_before(tq, tk, key_offset):
    row = lax.broadcasted_iota(jnp.int32, (tq, tk), 0)
    col = lax.broadcasted_iota(jnp.int32, (tq, tk), 1)
    return col + key_offset < row


def _sb_specs(S, n_g, n_h):
    base = 2 * n_g
    q_spec = pl.BlockSpec((S, LANE), lambda h: (0, base + h))
    k_spec = pl.BlockSpec((S, LANE), lambda h: (0, base + n_h + h))
    v_spec = pl.BlockSpec((S, LANE), lambda h: (0, base + 2 * n_h + h))
    gain_spec = pl.BlockSpec((1, LANE), lambda h: (0, n_g + h))
    head_spec = pl.BlockSpec((S, LANE), lambda h: (0, h))
    return q_spec, k_spec, v_spec, gain_spec, head_spec


def _sb_fwd(proj, out_gain, n_g, n_h, name):
    S = proj.shape[0]
    TQ, TK, NQ, KPQ = _sb_tiles(S)
    scale = LANE ** -0.5
    q_spec, k_spec, v_spec, gain_spec, head_spec = _sb_specs(S, n_g, n_h)

    def body(q_ref, k_ref, v_ref, og_ref, o_ref, on_ref, ls_ref, qb, kb, vb):
        qb[...] = q_ref[...].astype(MXU_DTYPE)
        kb[...] = k_ref[...].astype(MXU_DTYPE)
        vb[...] = v_ref[...].astype(MXU_DTYPE)
        after = _triangle(TK, lambda r, c: r > c)

        def block(qi, j, ctail, acc, key_offset):
            skip = key_offset or 0
            z = _dot(qi[skip:], kb[_rows(j, TK), :], NT) * scale
            lb, l1m = _log_sigmoids(z)
            if key_offset is not None:
                strict = _strictly_before(TQ - skip, TK, 0)
                l1m = jnp.where(strict, l1m, 0.0)
            a = jnp.exp(lb + ctail[skip:] + _tri_sum(l1m, after))
            if key_offset is not None:
                a = jnp.where(strict, a, 0.0)
            acc_new = acc[skip:] + _dot(a.astype(MXU_DTYPE), vb[_rows(j, TK), :], NN)
            ctail_new = ctail[skip:] + jnp.sum(l1m, axis=1, keepdims=True)
            if skip:
                ctail_new = jnp.concatenate([ctail[:skip], ctail_new], axis=0)
                acc_new = jnp.concatenate([acc[:skip], acc_new], axis=0)
            return ctail_new, acc_new

        def q_loop(i, carry):
            qi = qb[_rows(i, TQ), :]
            state = (jnp.zeros((TQ, 1), F32), jnp.zeros((TQ, LANE), F32))
            for d in reversed(range(KPQ)):
                state = block(qi, i * KPQ + d, state[0], state[1], d * TK)
            def pair(jj, st):
                st = block(qi, i * KPQ - 1 - 2 * jj, st[0], st[1], None)
                return block(qi, i * KPQ - 2 - 2 * jj, st[0], st[1], None)

            ctail, acc = lax.fori_loop(0, i * (KPQ // 2), pair, state)
            ls_ref[_rows(i, TQ), :] = jnp.broadcast_to(ctail, (TQ, LANE))
            o_ref[_rows(i, TQ), :] = acc
            r = lax.rsqrt(_mean1(acc * acc) + EPS)
            on_ref[_rows(i, TQ), :] = (acc * r * og_ref[...]).astype(on_ref.dtype)
            return carry

        lax.fori_loop(0, NQ, q_loop, 0)

    return pl.pallas_call(
        body, name=name, grid=(n_h,),
        in_specs=[q_spec, k_spec, v_spec, gain_spec],
        out_specs=[head_spec, head_spec, head_spec],
        out_shape=[jax.ShapeDtypeStruct((S, n_h * LANE), F32), jax.ShapeDtypeStruct((S, n_h * LANE), MXU_DTYPE),
                   jax.ShapeDtypeStruct((S, n_h * LANE), F32)],
        scratch_shapes=[pltpu.VMEM((S, LANE), MXU_DTYPE)] * 3,
        compiler_params=_params("parallel"),
    )(proj, proj, proj, out_gain)


def _sb_bwd(proj, o_sb, l_sum, d_on, out_gain, n_g, n_h, name):
    S = proj.shape[0]
    TQ, TK, NQ, KPQ = _sb_tiles(S)
    scale = LANE ** -0.5
    q_spec, k_spec, v_spec, gain_spec, head_spec = _sb_specs(S, n_g, n_h)
    dn_spec = pl.BlockSpec((S, LANE), lambda h: (0, n_g + h))
    dgain_spec = pl.BlockSpec((1, LANE), lambda h: (0, h))

    def body(q_ref, k_ref, v_ref, o_ref, ls_ref, dn_ref, og_ref, dq_ref, dk_ref, dv_ref, dog_ref,
             qb, kb, vb, dob, dk_acc, dv_acc):
        qb[...] = q_ref[...].astype(MXU_DTYPE)
        kb[...] = k_ref[...].astype(MXU_DTYPE)
        vb[...] = v_ref[...].astype(MXU_DTYPE)
        o, dn = o_ref[...], dn_ref[...]
        r = lax.rsqrt(_mean1(o * o) + EPS)
        oh = o * r
        dog_ref[...] = _sum0(dn * oh)
        dhn = dn * og_ref[...]
        dob[...] = (r * (dhn - oh * _mean1(dhn * oh))).astype(MXU_DTYPE)
        dk_acc[...] = jnp.zeros_like(dk_acc)
        dv_acc[...] = jnp.zeros_like(dv_acc)

        up_to = _triangle(TK, lambda r, c: r <= c)
        before = _triangle(TK, lambda r, c: r < c)

        def block(qi, doi, ltot, j, cl, cdl, dq, key_offset):
            skip = key_offset or 0
            q_in, do_in = qi[skip:], doi[skip:]
            kj, vj = kb[_rows(j, TK), :], vb[_rows(j, TK), :]
            z = _dot(q_in, kj, NT) * scale
            lb, l1m = _log_sigmoids(z)
            if key_offset is not None:
                strict = _strictly_before(TQ - skip, TK, 0)
                l1m = jnp.where(strict, l1m, 0.0)
            a = jnp.exp(lb + (ltot[skip:] - (cl[skip:] + _tri_sum(l1m, up_to))))
            if key_offset is not None:
                a = jnp.where(strict, a, 0.0)
            dl = _dot(do_in, vj, NT) * a
            d_l1m = cdl[skip:] + _tri_sum(dl, before, exact=False)
            beta = jnp.exp(lb)
            dz = dl * (1.0 - beta) - beta * d_l1m
            if key_offset is not None:
                dz = jnp.where(strict, dz, 0.0)
            dzs = (dz * scale).astype(MXU_DTYPE)
            dk_acc[_rows(j, TK), :] += _dot(dzs, q_in, TN)
            dv_acc[_rows(j, TK), :] += _dot(a.astype(MXU_DTYPE), do_in, TN)
            cl_new = cl[skip:] + jnp.sum(l1m, axis=1, keepdims=True)
            cdl_new = cdl[skip:] + jnp.sum(dl, axis=1, keepdims=True)
            dq_new = dq[skip:] + _dot(dzs, kj, NN)
            if skip:
                cl_new = jnp.concatenate([cl[:skip], cl_new], axis=0)
                cdl_new = jnp.concatenate([cdl[:skip], cdl_new], axis=0)
                dq_new = jnp.concatenate([dq[:skip], dq_new], axis=0)
            return cl_new, cdl_new, dq_new

        def q_loop(i, carry):
            qi, doi = qb[_rows(i, TQ), :], dob[_rows(i, TQ), :]
            ltot = ls_ref[_rows(i, TQ), :][:, :1]
            zero_col = jnp.zeros((TQ, 1), F32)
            def pair(jj, st):
                st = block(qi, doi, ltot, 2 * jj, st[0], st[1], st[2], None)
                return block(qi, doi, ltot, 2 * jj + 1, st[0], st[1], st[2], None)

            state = lax.fori_loop(0, i * (KPQ // 2), pair, (zero_col, zero_col, jnp.zeros((TQ, LANE), F32)))
            for d in range(KPQ):
                state = block(qi, doi, ltot, i * KPQ + d, state[0], state[1], state[2], d * TK)
            dq_ref[_rows(i, TQ), :] = state[2].astype(dq_ref.dtype)
            return carry

        lax.fori_loop(0, NQ, q_loop, 0)
        dk_ref[...] = dk_acc[...].astype(dk_ref.dtype)
        dv_ref[...] = dv_acc[...].astype(dv_ref.dtype)

    W = n_h * LANE
    return pl.pallas_call(
        body, name=name, grid=(n_h,),
        in_specs=[q_spec, k_spec, v_spec, head_spec, head_spec, dn_spec, gain_spec],
        out_specs=[head_spec, head_spec, head_spec, dgain_spec],
        out_shape=[jax.ShapeDtypeStruct((S, W), MXU_DTYPE)] * 3 + [jax.ShapeDtypeStruct((1, W), F32)],
        scratch_shapes=[pltpu.VMEM((S, LANE), MXU_DTYPE)] * 4 + [pltpu.VMEM((S, LANE), F32)] * 2,
        compiler_params=_params("parallel"),
    )(proj, proj, proj, o_sb, l_sum, d_on, out_gain)


def _mod_part(c_all, w_ada, b_ada_cols, name):
    B, K = c_all.shape
    N = w_ada.shape[1]
    tn = _tile(N, 512)

    def body(c_ref, w_ref, b_ref, o_ref):
        cv = c_ref[...]
        ca = (cv * jax.nn.sigmoid(cv)).astype(MXU_DTYPE)
        o_ref[...] = _dot(ca, w_ref[...].astype(MXU_DTYPE), NN) + b_ref[...]

    return pl.pallas_call(
        body, name=name, grid=(N // tn,),
        in_specs=[pl.BlockSpec((B, K), lambda j: (0, 0)), pl.BlockSpec((K, tn), lambda j: (0, j)),
                  pl.BlockSpec((1, tn), lambda j: (0, j))],
        out_specs=pl.BlockSpec((B, tn), lambda j: (0, j)),
        out_shape=jax.ShapeDtypeStruct((B, N), F32), compiler_params=_params("parallel"))(c_all, w_ada, b_ada_cols)


def _adamw_math(w, g, m, v):
    m = ADAM_B1 * m + (1.0 - ADAM_B1) * g
    v = ADAM_B2 * v + (1.0 - ADAM_B2) * (g * g)
    m_hat = m / (1.0 - ADAM_B1 ** ADAM_STEP)
    v_hat = v / (1.0 - ADAM_B2 ** ADAM_STEP)
    delta = -ADAM_LR * (m_hat / (jnp.sqrt(v_hat) + ADAM_EPS) + ADAM_WD * w)
    return delta, m, v


def _adamw(w, g, m, v, name):
    R, C = w.shape
    tr = _tile(R, max(8, (1 << 19) // C), 8)
    spec = pl.BlockSpec((tr, C), lambda i: (i, 0))

    def body(w_ref, g_ref, m_ref, v_ref, go_ref, d_ref, mo_ref, vo_ref):
        g = g_ref[...]
        go_ref[...] = g
        d_ref[...], mo_ref[...], vo_ref[...] = _adamw_math(w_ref[...], g, m_ref[...], v_ref[...])

    return pl.pallas_call(body, name=name, grid=(R // tr,), in_specs=[spec] * 4, out_specs=[spec] * 4,
                          out_shape=[jax.ShapeDtypeStruct((R, C), F32)] * 4, compiler_params=_params("parallel"))(w, g, m, v)


def _adamw_ada(c_all, dmod_cols, w, m, v, name):
    K, N = w.shape
    B = c_all.shape[0]
    tk, tn = _tile(K, 512), _tile(N, 1024)
    spec = pl.BlockSpec((tk, tn), lambda i, j: (i, j))

    def body(c_ref, dm_ref, w_ref, m_ref, v_ref, g_ref, d_ref, mo_ref, vo_ref):
        cv = c_ref[...]
        ca = (cv * jax.nn.sigmoid(cv)).astype(MXU_DTYPE)
        g = _dot(ca, dm_ref[...].astype(MXU_DTYPE), TN)
        g_ref[...] = g
        d_ref[...], mo_ref[...], vo_ref[...] = _adamw_math(w_ref[...], g, m_ref[...], v_ref[...])

    return pl.pallas_call(
        body, name=name, grid=(K // tk, N // tn),
        in_specs=[pl.BlockSpec((B, tk), lambda i, j: (0, i)), pl.BlockSpec((B, tn), lambda i, j: (0, j)), spec, spec, spec],
        out_specs=[spec] * 4, out_shape=[jax.ShapeDtypeStruct((K, N), F32)] * 4,
        compiler_params=_params("parallel", "parallel"))(c_all, dmod_cols, w, m, v)


def _sum_devices(gathered, n_dev, name):
    R = gathered.shape[0] // n_dev
    C = gathered.shape[1]
    tr = _tile(R, 512, 8)
    n_blk = R // tr

    def body(*refs):
        acc = refs[0][...]
        for r in refs[1:n_dev]:
            acc = acc + r[...]
        refs[n_dev][...] = acc

    in_specs = [pl.BlockSpec((tr, C), functools.partial(lambda i, d: (d * n_blk + i, 0), d=d)) for d in range(n_dev)]
    return pl.pallas_call(body, name=name, grid=(n_blk,), in_specs=in_specs,
                          out_specs=pl.BlockSpec((tr, C), lambda i: (i, 0)),
                          out_shape=jax.ShapeDtypeStruct((R, C), F32), compiler_params=_params("parallel"))(*([gathered] * n_dev))


def _place():
    x, y, c = lax.axis_index("x"), lax.axis_index("y"), lax.axis_index("c")
    return x, y, c


def _allgather8(blk, name):
    m_per, n = blk.shape

    def body(x_ref, out_ref, send_sems, recv_sems, local_sem):
        x, y, c = _place()
        me, sibling = (x, y, c), (x, y, 1 - c)
        chips = [(1 - x, y), (x, 1 - y), (1 - x, 1 - y)]

        def rows(px, py, pc):
            return out_ref.at[pl.ds((4 * px + 2 * py + pc) * m_per, m_per), :]

        def copy(k, block, to, src=None):
            return pltpu.make_async_remote_copy(
                src_ref=rows(*block) if src is None else src, dst_ref=rows(*block),
                send_sem=send_sems.at[k], recv_sem=recv_sems.at[k], device_id=to, device_id_type=MESH)

        mine = pltpu.make_async_copy(x_ref, rows(*me), local_sem)
        mine.start()
        first = [copy(0, me, sibling, src=x_ref)]
        first += [copy(1 + j, me, (*chip, c), src=x_ref) for j, chip in enumerate(chips)]
        for cp in first:
            cp.start()
        passed = [copy(4 + j, (*chip, c), sibling) for j, chip in enumerate(chips)]
        for j, chip in enumerate(chips):
            copy(1 + j, (*chip, c), me).wait_recv()
            passed[j].start()
        copy(0, sibling, me).wait_recv()
        for j, chip in enumerate(chips):
            copy(4 + j, (*chip, 1 - c), me).wait_recv()
        for cp in first + passed:
            cp.wait_send()
        mine.wait()

    return pl.pallas_call(
        body, name=name,
        out_shape=jax.ShapeDtypeStruct((8 * m_per, n), blk.dtype),
        in_specs=[pl.BlockSpec(memory_space=pltpu.VMEM)],
        out_specs=pl.BlockSpec(memory_space=pltpu.VMEM),
        scratch_shapes=[pltpu.SemaphoreType.DMA((7,)), pltpu.SemaphoreType.DMA((7,)), pltpu.SemaphoreType.DMA],
        compiler_params=pltpu.CompilerParams(vmem_limit_bytes=V7X_VMEM_LIMIT),
    )(blk)


class _Sharded:
    def __init__(self, shard_shape, by_cols):
        r, c = shard_shape
        self.by_cols = by_cols
        self.full = (r, N_CHIPS * c) if by_cols else (N_CHIPS * r, c)
        self.shard = (r, c)
        self.half_rows = r // 2
        self.half = (r // 2, c)

    def shard_of(self, ref, k):
        r, c = self.shard
        return ref.at[:, pl.ds(k * c, c)] if self.by_cols else ref.at[pl.ds(k * r, r), :]

    def half_of(self, ref, k, hc):
        r, c = self.shard
        h = self.half_rows
        if self.by_cols:
            return ref.at[pl.ds(hc * h, h), pl.ds(k * c, c)]
        return ref.at[pl.ds(k * r + hc * h, h), :]

    def chunk_of(self, ref, k, hc, ch, n):
        r, c = self.shard
        h = self.half_rows
        q = h // n
        if self.by_cols:
            return ref.at[pl.ds(hc * h + ch * q, q), pl.ds(k * c, c)]
        return ref.at[pl.ds(k * r + hc * h + ch * q, q), :]

    def half_of_shard(self, ref, hc):
        return ref.at[pl.ds(hc * self.half_rows, self.half_rows), :]

    def part_of_halves(self, ref, k):
        r, c = self.shard
        h = self.half_rows
        return ref.at[:, pl.ds(k * c, c)] if self.by_cols else ref.at[pl.ds(k * h, h), :]


def _on_each_place(x, y, c, fn, by_chip=True, by_core=True):
    q = 2 * x + y
    for k in range(N_CHIPS if by_chip else 1):
        for cc in range(2 if by_core else 1):
            cond = None
            if by_chip:
                cond = q == k
            if by_core:
                cond = (c == cc) if cond is None else jnp.logical_and(cond, c == cc)
            pl.when(cond)(functools.partial(fn, k, cc))


def _chip_id(k, c):
    return (k // 2, k % 2, c)


def _handshake(peers):
    barrier = pltpu.get_barrier_semaphore()
    for peer in peers:
        pl.semaphore_signal(barrier, inc=1, device_id=peer, device_id_type=MESH)
    pl.semaphore_wait(barrier, len(peers))


def _on_sequencer(body, inputs, out_structs, n_copies, peers_of, name, collective_id, return_inputs=False):
    in_refs = [jax.new_ref(a, memory_space=pltpu.MemorySpace.HBM) for a in inputs]
    out_refs = [jax.empty_ref(s, memory_space=pltpu.MemorySpace.HBM) for s in out_structs]

    @pl.kernel(mesh=plsc.ScalarSubcoreMesh(axis_name="sequencer", num_cores=1), name=name,
               scratch_types=(pltpu.SemaphoreType.DMA((n_copies,)), pltpu.SemaphoreType.DMA((n_copies,))),
               compiler_params=pltpu.CompilerParams(collective_id=collective_id))
    def launch(send_sems, recv_sems):
        x, y, c = _place()
        _handshake(peers_of(x, y, c))
        body(in_refs, out_refs, send_sems, recv_sems, x, y, c)

    launch()
    return [r[...] for r in (in_refs if return_inputs else out_refs)]


def _sibling(x, y, c):
    return [(x, y, 1 - c)]


def _same_core_of_other_chips(x, y, c):
    return [(1 - x, y, c), (x, 1 - y, c), (1 - x, 1 - y, c)]


GATHER_CHUNKS = 4
GATHER_COPIES = 6 * GATHER_CHUNKS


def _allgather8_on_sequencer(blk, name, collective_id):
    m_per, n = blk.shape
    x, y, c = _place()
    placed = lax.dynamic_update_slice(jnp.zeros((8 * m_per, n), blk.dtype), blk, ((4 * x + 2 * y + c) * m_per, 0))

    def body(refs, _, send_sems, recv_sems, x, y, c):
        out_ref, = refs

        def at_place(k, cc):
            def rows(kk, pc):
                return out_ref.at[pl.ds((2 * kk + pc) * m_per, m_per), :]

            def copy(slot, block, to):
                return pltpu.make_async_remote_copy(src_ref=rows(*block), dst_ref=rows(*block), send_sem=send_sems.at[slot],
                                                    recv_sem=recv_sems.at[slot], device_id=to, device_id_type=MESH)

            others = [k ^ flip for flip in FLIPS]
            sends = [copy(0, (k, cc), _chip_id(k, 1 - cc))] + [copy(1 + j, (k, cc), _chip_id(kk, cc)) for j, kk in enumerate(others)]
            for cp in sends:
                cp.start()
            for j, kk in enumerate(others):
                copy(1 + j, (kk, cc), _chip_id(k, cc)).wait_recv()
                cp = copy(4 + j, (kk, cc), _chip_id(k, 1 - cc))
                cp.start()
                sends.append(cp)
            copy(0, (k, 1 - cc), _chip_id(k, cc)).wait_recv()
            for j, kk in enumerate(others):
                copy(4 + j, (kk, 1 - cc), _chip_id(k, cc)).wait_recv()
            for cp in sends:
                cp.wait_send()

        _on_each_place(x, y, c, at_place)

    def peers(x, y, c):
        return _sibling(x, y, c) + _same_core_of_other_chips(x, y, c)

    return _on_sequencer(body, [placed], [], 7, peers, name, collective_id, return_inputs=True)[0]


def _gather_weights(fulls, geoms, name, collective_id):
    n_w = len(fulls)
    n_ch, n_relay = GATHER_CHUNKS, GATHER_CHUNKS // 2
    f_refs = [jax.new_ref(f, memory_space=pltpu.MemorySpace.HBM) for f in fulls]
    FLIP_X, FLIP_Y, FLIP_BOTH = FLIPS
    TO_X, TO_Y, RELAY_TO_Y, RELAY_TO_X, ON_X, ON_Y, ON_DIAG = 0, n_ch, 2 * n_ch, 2 * n_ch + n_relay, 3 * n_ch, 4 * n_ch, 5 * n_ch

    @pl.kernel(mesh=plsc.ScalarSubcoreMesh(axis_name="sequencer", num_cores=1), name=name,
               scratch_types=(pltpu.SemaphoreType.DMA((GATHER_COPIES * n_w,)), pltpu.SemaphoreType.DMA((GATHER_COPIES * n_w,))),
               compiler_params=pltpu.CompilerParams(collective_id=collective_id))
    def launch(send_sems, recv_sems):
        x, y, c = _place()
        _handshake([(x, y, 1 - c), (1 - x, y, c), (x, 1 - y, c)])

        def at_place(k, cc):
            kx, ky, kd = k ^ FLIP_X, k ^ FLIP_Y, k ^ FLIP_BOTH
            me, sibling = _chip_id(k, cc), _chip_id(k, 1 - cc)
            started = []

            def copy(i, slot, src, dst, to, start=True):
                cp = pltpu.make_async_remote_copy(src_ref=src, dst_ref=dst, send_sem=send_sems.at[GATHER_COPIES * i + slot],
                                                  recv_sem=recv_sems.at[GATHER_COPIES * i + slot], device_id=to, device_id_type=MESH)
                if start:
                    cp.start()
                    started.append(cp)
                return cp

            def pass_on(i, slot, ref, to):
                copy(i, slot, ref, ref, to)

            def landed(i, slot, ref):
                copy(i, slot, ref, ref, me, start=False).wait_recv()

            y_order = [(n_relay + s) % n_ch for s in range(n_ch)]
            for i, (g, f_ref) in enumerate(zip(geoms, f_refs)):
                for s in range(n_ch):
                    pass_on(i, TO_X + s, g.chunk_of(f_ref, k, cc, s, n_ch), _chip_id(kx, cc))
                    pass_on(i, TO_Y + y_order[s], g.chunk_of(f_ref, k, cc, y_order[s], n_ch), _chip_id(ky, cc))
            for i, (g, f_ref) in enumerate(zip(geoms, f_refs)):
                for s in range(n_ch):
                    from_x = g.chunk_of(f_ref, kx, cc, s, n_ch)
                    landed(i, TO_X + s, from_x)
                    if s < n_relay:
                        pass_on(i, RELAY_TO_Y + s, from_x, _chip_id(ky, cc))
                    pass_on(i, ON_X + s, from_x, sibling)
                    ch = y_order[s]
                    from_y = g.chunk_of(f_ref, ky, cc, ch, n_ch)
                    landed(i, TO_Y + ch, from_y)
                    if ch >= n_relay:
                        pass_on(i, RELAY_TO_X + ch - n_relay, from_y, _chip_id(kx, cc))
                    pass_on(i, ON_Y + ch, from_y, sibling)
                for r in range(n_relay):
                    via_y = g.chunk_of(f_ref, kd, cc, r, n_ch)
                    landed(i, RELAY_TO_Y + r, via_y)
                    pass_on(i, ON_DIAG + r, via_y, sibling)
                    via_x = g.chunk_of(f_ref, kd, cc, n_relay + r, n_ch)
                    landed(i, RELAY_TO_X + r, via_x)
                    pass_on(i, ON_DIAG + n_relay + r, via_x, sibling)
            for i, (g, f_ref) in enumerate(zip(geoms, f_refs)):
                for slot, kk in ((ON_X, kx), (ON_Y, ky), (ON_DIAG, kd)):
                    for ch in range(n_ch):
                        landed(i, slot + ch, g.chunk_of(f_ref, kk, 1 - cc, ch, n_ch))
            for cp in started:
                cp.wait_send()

        _on_each_place(x, y, c, at_place)

    launch()
    return [f_ref[...] for f_ref in f_refs]


def _swap_core_halves(grads, geoms, name, collective_id):
    n_cp = sum(1 if g.by_cols else N_CHIPS for g in geoms)

    def body(g_refs, t_refs, send_sems, recv_sems, x, y, c):

        def at_place(_, cc):
            def pairs(hc):
                out = []
                for g, g_ref, t_ref in zip(geoms, g_refs, t_refs):
                    if g.by_cols:
                        out.append((g_ref.at[pl.ds(hc * g.half_rows, g.half_rows), :], t_ref))
                    else:
                        out += [(g.half_of(g_ref, k, hc), g.part_of_halves(t_ref, k)) for k in range(N_CHIPS)]
                return out

            sends = [pltpu.make_async_remote_copy(src_ref=src, dst_ref=dst, send_sem=send_sems.at[n],
                                                  recv_sem=recv_sems.at[n], device_id=(x, y, 1 - cc), device_id_type=MESH)
                     for n, (src, dst) in enumerate(pairs(1 - cc))]
            for cp in sends:
                cp.start()
            for n, (src, dst) in enumerate(pairs(cc)):
                pltpu.make_async_remote_copy(src_ref=src, dst_ref=dst, send_sem=send_sems.at[n], recv_sem=recv_sems.at[n],
                                             device_id=(x, y, cc), device_id_type=MESH).wait_recv()
            for cp in sends:
                cp.wait_send()

        _on_each_place(x, y, c, at_place, by_chip=False)

    return _on_sequencer(body, grads, [jax.ShapeDtypeStruct((g.full[0] // 2, g.full[1]), F32) for g in geoms],
                         n_cp, _sibling, name, collective_id)


def _send_to_sibling(buffers, name, collective_id):
    def body(src_refs, dst_refs, send_sems, recv_sems, x, y, c):
        def copy(i):
            return pltpu.make_async_remote_copy(src_ref=src_refs[i], dst_ref=dst_refs[i], send_sem=send_sems.at[i],
                                                recv_sem=recv_sems.at[i], device_id=(x, y, 1 - c), device_id_type=MESH)

        for i in range(len(buffers)):
            copy(i).start()
        for i in range(len(buffers)):
            copy(i).wait()

    return _on_sequencer(body, buffers, [jax.ShapeDtypeStruct(t.shape, t.dtype) for t in buffers], len(buffers),
                         _sibling, name, collective_id)


def _scatter_chip_sums(sums, geoms, name, collective_id):
    def body(s_refs, r_refs, send_sems, recv_sems, x, y, c):

        def at_place(k, _):
            sends = []
            for i, (g, s_ref, r_ref) in enumerate(zip(geoms, s_refs, r_refs)):
                for j, flip in enumerate(FLIPS):
                    kk = k ^ flip
                    cp = pltpu.make_async_remote_copy(
                        src_ref=g.part_of_halves(s_ref, kk), dst_ref=r_ref.at[j], send_sem=send_sems.at[3 * i + j],
                        recv_sem=recv_sems.at[3 * i + j], device_id=(kk // 2, kk % 2, c), device_id_type=MESH)
                    cp.start()
                    sends.append(cp)
            for i, (g, s_ref, r_ref) in enumerate(zip(geoms, s_refs, r_refs)):
                for j in range(len(FLIPS)):
                    pltpu.make_async_remote_copy(
                        src_ref=g.part_of_halves(s_ref, k), dst_ref=r_ref.at[j], send_sem=send_sems.at[3 * i + j],
                        recv_sem=recv_sems.at[3 * i + j], device_id=(x, y, c), device_id_type=MESH).wait_recv()
            for cp in sends:
                cp.wait_send()

        _on_each_place(x, y, c, at_place, by_core=False)

    return _on_sequencer(body, sums, [jax.ShapeDtypeStruct((len(FLIPS),) + g.half, WIRE_DTYPE) for g in geoms],
                         len(FLIPS) * len(sums), _same_core_of_other_chips, name, collective_id)


def _share_reduced_halves(reduced, geoms, name, collective_id):
    def body(out_refs, _, send_sems, recv_sems, x, y, c):

        def at_place(_, cc):
            sends = []
            for i, (g, ref) in enumerate(zip(geoms, out_refs)):
                mine = g.half_of_shard(ref, cc)
                cp = pltpu.make_async_remote_copy(src_ref=mine, dst_ref=mine, send_sem=send_sems.at[i],
                                                  recv_sem=recv_sems.at[i], device_id=(x, y, 1 - cc), device_id_type=MESH)
                cp.start()
                sends.append(cp)
            for i, (g, ref) in enumerate(zip(geoms, out_refs)):
                theirs = g.half_of_shard(ref, 1 - cc)
                pltpu.make_async_remote_copy(src_ref=theirs, dst_ref=theirs, send_sem=send_sems.at[i],
                                             recv_sem=recv_sems.at[i], device_id=(x, y, cc), device_id_type=MESH).wait_recv()
            for cp in sends:
                cp.wait_send()

        _on_each_place(x, y, c, at_place, by_chip=False)

    return _on_sequencer(body, reduced, [], len(reduced), _sibling, name, collective_id, return_inputs=True)


def _chip_sum(place, grad, theirs, g, name):
    RH, C = theirs.shape
    h = g.half_rows
    tr = _tile(h, 256, 16)
    tc = _tile(C, 2048)
    per_half = h // tr

    if g.by_cols:
        grad_map = lambda i, j, p: (p[1] * per_half + i, j)
    else:
        grad_map = lambda i, j, p: ((i // per_half) * 2 * per_half + p[1] * per_half + i % per_half, j)

    def body(p_ref, a_ref, b_ref, f_ref, o_ref):
        total = a_ref[...] + b_ref[...]
        f_ref[...] = total
        o_ref[...] = total.astype(o_ref.dtype)

    return pl.pallas_call(
        body, name=name,
        grid_spec=pltpu.PrefetchScalarGridSpec(
            num_scalar_prefetch=1, grid=(RH // tr, C // tc),
            in_specs=[pl.BlockSpec((tr, tc), grad_map), pl.BlockSpec((tr, tc), lambda i, j, p: (i, j))],
            out_specs=[pl.BlockSpec((tr, tc), lambda i, j, p: (i, j))] * 2),
        out_shape=[jax.ShapeDtypeStruct((RH, C), F32), jax.ShapeDtypeStruct((RH, C), WIRE_DTYPE)],
        compiler_params=_params("parallel", "parallel"),
    )(place, grad, theirs)


def _dw_half(place, a, b, g, mine, name, add=None):
    K, R = a.shape
    C = b.shape[1]
    h = g.half_rows
    tm, tn = _tile(h, 1024, 16), _tile(C, 512)
    per_half = h // tm
    n_i = (R // 2) // tm

    def a_map(i, j, p):
        hc = p[1] if mine else 1 - p[1]
        if g.by_cols:
            return 0, hc * n_i + i
        return 0, (i // per_half) * 2 * per_half + hc * per_half + i % per_half

    mn_spec = pl.BlockSpec((tm, tn), lambda i, j, p: (i, j))

    def body(p_ref, a_ref, b_ref, *rest):
        acc = _dot(a_ref[...], b_ref[...], TN)
        if add is None:
            rest[0][...] = acc
        else:
            total = acc + rest[0][...]
            rest[1][...] = total
            rest[2][...] = total.astype(rest[2].dtype)

    out_shape = [jax.ShapeDtypeStruct((R // 2, C), F32)] + ([] if add is None else [jax.ShapeDtypeStruct((R // 2, C), WIRE_DTYPE)])
    return pl.pallas_call(
        body, name=name,
        grid_spec=pltpu.PrefetchScalarGridSpec(
            num_scalar_prefetch=1, grid=(n_i, C // tn),
            in_specs=[pl.BlockSpec((K, tm), a_map), pl.BlockSpec((K, tn), lambda i, j, p: (0, j))] + ([] if add is None else [mn_spec]),
            out_specs=[mn_spec] * len(out_shape)),
        out_shape=out_shape,
        compiler_params=_params("parallel", "arbitrary"),
    )(place, a, b, *([] if add is None else [add]))


def _reduce_half(place, sums, others, g, name):
    h, tc = g.half
    tr = _tile(h, 256, 16)
    per_half = h // tr
    sums_map = (lambda i, p: (i, p[0])) if g.by_cols else (lambda i, p: (p[0] * per_half + i, 0))

    def body(p_ref, s_ref, o0_ref, o1_ref, o2_ref, out_ref):
        acc = s_ref[...]
        for o_ref in (o0_ref, o1_ref, o2_ref):
            acc = acc + o_ref[...].astype(F32)
        out_ref[...] = acc

    other_specs = [pl.BlockSpec((None, tr, tc), functools.partial(lambda i, p, j: (j, i, 0), j=j)) for j in range(len(FLIPS))]
    return pl.pallas_call(
        body, name=name,
        grid_spec=pltpu.PrefetchScalarGridSpec(
            num_scalar_prefetch=1, grid=(per_half,),
            in_specs=[pl.BlockSpec((tr, tc), sums_map)] + other_specs,
            out_specs=pl.BlockSpec((tr, tc), lambda i, p: (p[1] * per_half + i, 0))),
        out_shape=jax.ShapeDtypeStruct(g.shard, F32),
        compiler_params=_params("arbitrary"),
    )(place, sums, others, others, others)


SMALL = ("b_ada", "norm1_g", "v_norm_g", "w_spatial", "b_spatial", "out_norm_g", "norm2_g", "final_g")
BIG = ("w_in", "w_out", "w_gate", "w_up", "w_down")
BY_COLS = {"w_in": True, "w_out": False, "w_gate": True, "w_up": True, "w_down": False}
ORDER = ("w_ada", "b_ada", "norm1_g", "w_in", "v_norm_g", "w_spatial", "b_spatial", "out_norm_g", "w_out",
         "norm2_g", "w_gate", "w_up", "w_down", "final_g")


def _pack(parts):
    return jnp.concatenate([parts[n].reshape(-1) for n in SMALL]).reshape(-1, LANE)


def _unpack(slab, shapes):
    flat = slab.reshape(-1)
    out, at = {}, 0
    for n in SMALL:
        size = math.prod(shapes[n])
        out[n] = flat[at:at + size].reshape(shapes[n])
        at += size
    return out


def kernel(x, c, w_ada, b_ada, norm1_g, w_in, v_norm_g, w_spatial, b_spatial, out_norm_g, w_out, norm2_g, w_gate, w_up, w_down, final_g, loss_target, m_w_ada, m_b_ada, m_norm1_g, m_w_in, m_v_norm_g, m_w_spatial, m_b_spatial, m_out_norm_g, m_w_out, m_norm2_g, m_w_gate, m_w_up, m_w_down, m_final_g, v_w_ada, v_b_ada, v_norm1_g, v_w_in, v_v_norm_g, v_w_spatial, v_b_spatial, v_out_norm_g, v_w_out, v_norm2_g, v_w_gate, v_w_up, v_w_down, v_final_g):
    weights = dict(w_ada=w_ada, b_ada=b_ada, norm1_g=norm1_g, w_in=w_in, v_norm_g=v_norm_g, w_spatial=w_spatial,
                   b_spatial=b_spatial, out_norm_g=out_norm_g, w_out=w_out, norm2_g=norm2_g, w_gate=w_gate, w_up=w_up,
                   w_down=w_down, final_g=final_g)
    m_in = dict(w_ada=m_w_ada, b_ada=m_b_ada, norm1_g=m_norm1_g, w_in=m_w_in, v_norm_g=m_v_norm_g, w_spatial=m_w_spatial,
                b_spatial=m_b_spatial, out_norm_g=m_out_norm_g, w_out=m_w_out, norm2_g=m_norm2_g, w_gate=m_w_gate,
                w_up=m_w_up, w_down=m_w_down, final_g=m_final_g)
    v_in = dict(w_ada=v_w_ada, b_ada=v_b_ada, norm1_g=v_norm1_g, w_in=v_w_in, v_norm_g=v_v_norm_g, w_spatial=v_w_spatial,
                b_spatial=v_b_spatial, out_norm_g=v_out_norm_g, w_out=v_w_out, norm2_g=v_norm2_g, w_gate=v_w_gate,
                w_up=v_w_up, w_down=v_w_down, final_g=v_final_g)

    S, D = x.shape[1], x.shape[2]
    n_g = v_norm_g.shape[-1] // LANE
    n_h = (D - n_g * LANE) // LANE
    GW = n_g * LANE
    xi, yi, ci = _place()
    chip = 2 * xi + yi
    me = 4 * xi + 2 * yi + ci
    place = jnp.stack([chip, ci]).astype(jnp.int32)

    xs, target = x[0], loss_target[0]
    geoms = [_Sharded(weights[n].shape[1:], BY_COLS[n]) for n in BIG]

    full = {}
    for i, group in enumerate((("w_in",), ("w_out",), ("w_gate", "w_up"), ("w_down",))):
        gg = [geoms[BIG.index(n)] for n in group]
        own = [_cast_into_full(place, weights[n][0], g, "cast_" + n) for n, g in zip(group, gg)]
        gathered = _gather_weights(own, gg, "gather_" + "_".join(group), 1 + i)
        full.update(zip(group, gathered))

    c_pad = jnp.concatenate([c, jnp.zeros((7, D), F32)], axis=0)
    c_all = _allgather8(c_pad, "gather_c")[::8]
    n_ada = w_ada.shape[2]
    b_cols = lax.dynamic_slice(b_ada, (0, chip * n_ada), (1, n_ada))
    mod_parts = _allgather8(_mod_part(c_all, w_ada[0], b_cols, "mod_part"), "gather_mod")
    mod_all = mod_parts.reshape(N_CHIPS, 2, 8, n_ada)[:, 0].transpose(1, 0, 2).reshape(8, N_CHIPS * n_ada)
    mod = lax.dynamic_slice(mod_all, (me, 0), (1, 6 * D))
    shift1, scale1, gate1, shift2, scale2, gate2 = [mod[:, i * D:(i + 1) * D] for i in range(6)]

    b_t = b_spatial[0].T
    h1 = _norm_mod(xs, norm1_g, scale1, shift1, "norm1")
    proj, = _mm("nn", h1, full["w_in"], [F32], "proj")
    on_gm = _gmlp_fwd(proj, v_norm_g, w_spatial[0], b_t, out_norm_g, n_g, "gmlp_fwd")
    o_sb, on_sb, l_sum = _sb_fwd(proj, out_norm_g, n_g, n_h, "sb_fwd")
    o_n = jnp.concatenate([on_gm, on_sb], axis=1)
    attn, = _mm("nn", o_n, full["w_out"], [F32], "attn_out")
    x1, h2 = _residual_norm_mod(xs, attn, gate1, norm2_g, scale2, shift2, "norm2")
    a_g, a_u, f_in = _gate_up(h2, full["w_gate"], full["w_up"], "gate_up")
    f, = _mm("nn", f_in, full["w_down"], [F32], "down", tm=1024)
    dx2, df, d_gate2, d_final_g, loss_part = _final_loss_bwd(x1, f, gate2, final_g.reshape(1, D), target, "final")
    loss = lax.psum(loss_part[0, 0], ("x", "y", "c"))

    geom_of = dict(zip(BIG, geoms))
    grad_out, delta, new_m, new_v = {}, {}, {}, {}

    def theirs_first(group, operands, collective_id, after=None):
        outs = []
        for n, (a_op, b_op) in zip(group, operands):
            outs.append(_dw_half(place, a_op, b_op if after is None else _then(after, b_op), geom_of[n], False, "d_" + n + "_theirs")[0])
            after = outs[-1]
        return outs, _send_to_sibling(outs, "swap_" + "_".join(group), collective_id)

    def chip_sums(group, operands, theirs, after):
        f32s, wires = [], []
        for n, (a_op, b_op), t in zip(group, operands, theirs):
            sf, sw = _dw_half(place, a_op, b_op, geom_of[n], True, "d_" + n + "_mine", add=_then(after, t))
            f32s.append(sf)
            wires.append(sw)
            after = sw
        return f32s, wires

    def scatter(group, sums, collective_id):
        return _scatter_chip_sums(sums, [geom_of[n] for n in group], "scatter_" + "_".join(group), collective_id)

    def reduce_halves(group, sums, others, after):
        return [_reduce_half(place, sf, _then(after, o), geom_of[n], "reduce_" + n) for n, sf, o in zip(group, sums, others)]

    def share(group, halves, collective_id):
        return _share_reduced_halves(halves, [geom_of[n] for n in group], "share_" + "_".join(group), collective_id)

    def adamw(group, reduced, after):
        for n, r in zip(group, reduced):
            go, d, mo, vo = _adamw(weights[n][0], _then(after, r), m_in[n][0], v_in[n][0], "adamw_" + n)
            grad_out[n], delta[n], new_m[n], new_v[n] = go[None], d[None], mo[None], vo[None]
        return d

    g_down = ("w_down",)
    g_ffn = ("w_gate", "w_up")
    g_out = ("w_out",)
    g_in = ("w_in",)

    gr_down, = _mm("tn", f_in, df, [F32], "d_w_down", tm=1408, tn=1024)
    th_down, = _swap_core_halves([gr_down], [geom_of["w_down"]], "swap_w_down", 6)
    d_ag, d_au = _mm("nt", df, full["w_down"], [MXU_DTYPE, MXU_DTYPE], "d_ffn_in", extras=(a_g, a_u),
                     epilogue=_swiglu_bwd_epilogue)
    sf_down, sw_down = [[t] for t in _chip_sum(place, gr_down, _then(d_ag, th_down), geom_of["w_down"], "chip_sum_w_down")]
    ot_down = scatter(g_down, sw_down, 7)
    sent, th_ffn = theirs_first(g_ffn, [(h2, d_ag), (h2, d_au)], 9, after=sw_down)
    dh2 = _mm_ktiled("nt", [(_then(sent, d_ag), full["w_gate"]), (d_au, full["w_up"])], "d_h2", tn=512)
    sf_ffn, sw_ffn = chip_sums(g_ffn, [(h2, d_ag), (h2, d_au)], th_ffn, after=dh2)
    ot_ffn = scatter(g_ffn, sw_ffn, 10)
    hv_down = reduce_halves(g_down, sf_down, ot_down, after=sw_ffn)
    rd_down = share(g_down, hv_down, 8)
    dx1, d_shift2, d_scale2, d_norm2_g, d_gate1, d_attn = _norm_mod_bwd(
        _then(hv_down, dh2), x1, dx2, norm2_g, scale2, "norm2_bwd", branch=attn, gate=gate1)
    gr_out, = _mm("tn", o_n, d_attn, [F32], "d_w_out")
    th_out, = _swap_core_halves([gr_out], [geom_of["w_out"]], "swap_w_out", 12)
    d_on, = _mm("nt", _then(gr_out, d_attn), full["w_out"], [F32], "d_o")
    dp_gm, d_w_spatial, d_b_t, d_v_norm_g, d_og_gm = _gmlp_bwd(proj, d_on, v_norm_g, w_spatial[0], b_t, out_norm_g, n_g, "gmlp_bwd")
    sf_out, sw_out = [[t] for t in _chip_sum(place, gr_out, _then(dp_gm, th_out), geom_of["w_out"], "chip_sum_w_out")]
    ot_out = scatter(g_out, sw_out, 13)
    dq, dk, dv, d_og_sb = _sb_bwd(proj, o_sb, l_sum, _then(sw_out, d_on), out_norm_g, n_g, n_h, "sb_bwd")
    hv_ffn = reduce_halves(g_ffn, sf_ffn, ot_ffn, after=dq)
    rd_ffn = share(g_ffn, hv_ffn, 11)
    dproj = jnp.concatenate([_then(hv_ffn, dp_gm), dq, dk, dv], axis=1)
    sent, th_in = theirs_first(g_in, [(h1, dproj)], 15)
    dh1, = _mm("nt", _then(sent, dproj), full["w_in"], [F32], "d_h1", tm=1024)
    sf_in, sw_in = chip_sums(g_in, [(h1, dproj)], th_in, after=dh1)
    ot_in = scatter(g_in, sw_in, 16)
    hv_out = reduce_halves(g_out, sf_out, ot_out, after=sw_in)
    rd_out = share(g_out, hv_out, 14)
    grad_x, d_shift1, d_scale1, d_norm1_g = _norm_mod_bwd(_then(hv_out, dh1), xs, dx1, norm1_g, scale1, "norm1_bwd")

    dmod = jnp.concatenate([d_shift1, d_scale1, d_gate1, d_shift2, d_scale2, d_gate2], axis=1)
    small_parts = dict(b_ada=dmod, norm1_g=d_norm1_g, v_norm_g=d_v_norm_g, w_spatial=d_w_spatial, b_spatial=d_b_t.T,
                       out_norm_g=jnp.concatenate([d_og_gm, d_og_sb], axis=1), norm2_g=d_norm2_g, final_g=d_final_g)
    slab = _then(grad_x, _pack(small_parts))
    rows = slab.shape[0]
    gathered = _allgather8_on_sequencer(slab, "gather_small", 18)
    done = adamw(g_down, rd_down, after=slab)
    done = adamw(g_ffn, rd_ffn, after=done)
    done = adamw(g_out, rd_out, after=done)
    gathered = _then(done, gathered)
    small_shapes = {n: weights[n].shape for n in SMALL}
    small_sum = _sum_devices(gathered, 8, "sum_small")
    dmod_all = gathered.reshape(8, rows * LANE)[:, :6 * D]
    dmod_cols = lax.dynamic_slice(dmod_all, (0, chip * n_ada), (8, n_ada))
    g_ada, d, mo, vo = _adamw_ada(c_all, dmod_cols, w_ada[0], m_w_ada[0], v_w_ada[0], "adamw_w_ada")
    grad_out["w_ada"], delta["w_ada"], new_m["w_ada"], new_v["w_ada"] = g_ada[None], d[None], mo[None], vo[None]
    gs_small, d_small, mo, vo = _adamw(_pack({n: weights[n] for n in SMALL}), small_sum, _pack({n: m_in[n] for n in SMALL}),
                                       _pack({n: v_in[n] for n in SMALL}), "adamw_small")
    for dst, slab_out in ((grad_out, gs_small), (delta, d_small), (new_m, mo), (new_v, vo)):
        dst.update(_unpack(slab_out, small_shapes))
    hv_in = reduce_halves(g_in, sf_in, ot_in, after=d)
    adamw(g_in, share(g_in, hv_in, 17), after=d)

    return (loss, grad_x[None], *[grad_out[n] for n in ORDER], *[delta[n] for n in ORDER],
            *[new_m[n] for n in ORDER], *[new_v[n] for n in ORDER])
```

```python
import functools
import math

import jax
import jax.numpy as jnp
from jax import lax
from jax.experimental import pallas as pl
from jax.experimental.pallas import tpu as pltpu
from jax.experimental.pallas import tpu_sc as plsc

F32 = jnp.float32
MXU_DTYPE = jnp.bfloat16
WIRE_DTYPE = jnp.bfloat16
EPS = 1e-6
LANE = 128
V7X_VMEM_LIMIT = 56 * 1024 * 1024
MESH = pl.DeviceIdType.MESH
N_CHIPS = 4
FLIPS = (2, 1, 3)

ADAM_LR = 0.001
ADAM_B1 = 0.9
ADAM_B2 = 0.999
ADAM_EPS = 1e-08
ADAM_WD = 0.01
ADAM_STEP = 10


def _params(*semantics):
    return pltpu.CompilerParams(dimension_semantics=semantics or None, vmem_limit_bytes=V7X_VMEM_LIMIT)


def _tile(dim, pref, unit=LANE):
    best = None
    t = unit
    while t <= min(dim, pref):
        if dim % t == 0:
            best = t
        t += unit
    return best if best is not None else dim


def _then(first, second):
    return lax.optimization_barrier((first, second))[1]


def _sum0(v):
    return jnp.sum(v, axis=0, keepdims=True)


def _mean1(v):
    return jnp.mean(v, axis=-1, keepdims=True)


def _gelu(x):
    return 0.5 * x * (1.0 + lax.erf(x * (1.0 / math.sqrt(2.0))))


def _gelu_grad(x):
    cdf = 0.5 * (1.0 + lax.erf(x * (1.0 / math.sqrt(2.0))))
    return cdf + x * jnp.exp(-0.5 * x * x) * (1.0 / math.sqrt(2.0 * math.pi))


def _dot(a, b, dims):
    return lax.dot_general(a, b, (dims, ((), ())), preferred_element_type=F32)


NN = ((1,), (0,))
NT = ((1,), (1,))
TN = ((0,), (0,))


def _mm(kind, a, b, out_dtypes, name, tm=2048, tn=512, extras=(), epilogue=None):
    if kind == "nn":
        (M, K), N = a.shape, b.shape[1]
    elif kind == "nt":
        (M, K), N = a.shape, b.shape[0]
    else:
        (K, M), N = a.shape, b.shape[1]
    tm, tn = _tile(M, tm), _tile(N, tn)
    a_spec = pl.BlockSpec((K, tm), lambda i, j: (0, i)) if kind == "tn" else pl.BlockSpec((tm, K), lambda i, j: (i, 0))
    b_spec = pl.BlockSpec((tn, K), lambda i, j: (j, 0)) if kind == "nt" else pl.BlockSpec((K, tn), lambda i, j: (0, j))
    mn_spec = pl.BlockSpec((tm, tn), lambda i, j: (i, j))
    dims = {"nn": NN, "nt": NT, "tn": TN}[kind]
    n_extra = len(extras)

    n_chunks = 1 if epilogue is None or kind == "tn" else max(1, tm // 512)
    rows_per = tm // n_chunks

    def body(a_ref, b_ref, *rest):
        for r in range(n_chunks):
            rows = slice(r * rows_per, (r + 1) * rows_per)
            acc = _dot(a_ref[...] if n_chunks == 1 else a_ref[rows, :], b_ref[...], dims)
            res = (acc,) if epilogue is None else epilogue(acc, *[e[rows, :] for e in rest[:n_extra]])
            for o_ref, val in zip(rest[n_extra:], res):
                o_ref[rows, :] = val.astype(o_ref.dtype)

    outs = pl.pallas_call(
        body, name=name, grid=(M // tm, N // tn),
        in_specs=[a_spec, b_spec] + [mn_spec] * n_extra,
        out_specs=[mn_spec] * len(out_dtypes),
        out_shape=[jax.ShapeDtypeStruct((M, N), d) for d in out_dtypes],
        compiler_params=_params("parallel", "arbitrary"),
    )(a, b, *extras)
    return outs


def _mm_ktiled(kind, pairs, name, tm=2048, tn=1024, tk=1408):
    a0, b0 = pairs[0]
    M, K = a0.shape
    N = b0.shape[1] if kind == "nn" else b0.shape[0]
    tm, tn, tk = _tile(M, tm), _tile(N, tn), _tile(K, tk)
    a_spec = pl.BlockSpec((tm, tk), lambda i, j, k: (i, k))
    b_spec = pl.BlockSpec((tk, tn), lambda i, j, k: (k, j)) if kind == "nn" else pl.BlockSpec((tn, tk), lambda i, j, k: (j, k))
    dims = NN if kind == "nn" else NT
    n_pairs = len(pairs)

    def body(*refs):
        o_ref = refs[2 * n_pairs]
        acc = _dot(refs[0][...], refs[1][...], dims)
        for p in range(1, n_pairs):
            acc = acc + _dot(refs[2 * p][...], refs[2 * p + 1][...], dims)

        @pl.when(pl.program_id(2) == 0)
        def _():
            o_ref[...] = acc

        @pl.when(pl.program_id(2) != 0)
        def _():
            o_ref[...] += acc

    return pl.pallas_call(
        body, name=name, grid=(M // tm, N // tn, K // tk),
        in_specs=[a_spec, b_spec] * n_pairs,
        out_specs=pl.BlockSpec((tm, tn), lambda i, j, k: (i, j)),
        out_shape=jax.ShapeDtypeStruct((M, N), F32),
        compiler_params=_params("parallel", "parallel", "arbitrary"),
    )(*[x for pair in pairs for x in pair])


def _gate_up(h, wg, wu, name):
    (M, K), N = h.shape, wg.shape[1]
    tm, tn = _tile(M, 2048), _tile(N, 512)

    n_chunks = max(1, tm // 512)
    rows_per = tm // n_chunks

    def body(h_ref, wg_ref, wu_ref, ag_ref, au_ref, f_ref):
        for r in range(n_chunks):
            rows = slice(r * rows_per, (r + 1) * rows_per)
            hv = h_ref[rows, :]
            ag = _dot(hv, wg_ref[...], NN)
            au = _dot(hv, wu_ref[...], NN)
            ag_ref[rows, :] = ag.astype(ag_ref.dtype)
            au_ref[rows, :] = au.astype(au_ref.dtype)
            f_ref[rows, :] = (ag * jax.nn.sigmoid(ag) * au).astype(f_ref.dtype)

    w_spec = pl.BlockSpec((K, tn), lambda i, j: (0, j))
    mn_spec = pl.BlockSpec((tm, tn), lambda i, j: (i, j))
    return pl.pallas_call(
        body, name=name, grid=(M // tm, N // tn),
        in_specs=[pl.BlockSpec((tm, K), lambda i, j: (i, 0)), w_spec, w_spec],
        out_specs=[mn_spec] * 3,
        out_shape=[jax.ShapeDtypeStruct((M, N), MXU_DTYPE)] * 3,
        compiler_params=_params("parallel", "arbitrary"),
    )(h, wg, wu)


def _swiglu_bwd_epilogue(dfin, ag, au):
    ag, au = ag.astype(F32), au.astype(F32)
    sg = jax.nn.sigmoid(ag)
    d_au = dfin * (ag * sg)
    d_ag = dfin * au * (sg * (1.0 + ag * (1.0 - sg)))
    return d_ag, d_au


def _row_specs(ts, width):
    return pl.BlockSpec((ts, width), lambda i: (i, 0)), pl.BlockSpec((1, width), lambda i: (0, 0))


def _cast_into_full(place, shard, g, name):
    R, C = shard.shape
    tr = _tile(R, 256, 16)
    n_blk = R // tr
    out_map = (lambda i, p: (i, p[0])) if g.by_cols else (lambda i, p: (p[0] * n_blk + i, 0))

    def body(p_ref, a_ref, o_ref):
        o_ref[...] = a_ref[...].astype(o_ref.dtype)

    return pl.pallas_call(
        body, name=name,
        grid_spec=pltpu.PrefetchScalarGridSpec(
            num_scalar_prefetch=1, grid=(n_blk,),
            in_specs=[pl.BlockSpec((tr, C), lambda i, p: (i, 0))],
            out_specs=pl.BlockSpec((tr, C), out_map)),
        out_shape=jax.ShapeDtypeStruct(g.full, WIRE_DTYPE),
        compiler_params=_params("arbitrary"),
    )(place, shard)


def _norm_mod(x, g, scale, shift, name):
    S, D = x.shape
    ts = _tile(S, 256, 16)
    tile, vec = _row_specs(ts, D)

    def body(x_ref, g_ref, sc_ref, sh_ref, h_ref):
        xv = x_ref[...]
        r = lax.rsqrt(_mean1(xv * xv) + EPS)
        h_ref[...] = ((xv * r) * g_ref[...] * (1.0 + sc_ref[...]) + sh_ref[...]).astype(h_ref.dtype)

    return pl.pallas_call(body, name=name, grid=(S // ts,), in_specs=[tile, vec, vec, vec], out_specs=tile,
                          out_shape=jax.ShapeDtypeStruct((S, D), MXU_DTYPE), compiler_params=_params("parallel"))(x, g, scale, shift)


def _residual_norm_mod(x, attn, gate, g, scale, shift, name):
    S, D = x.shape
    ts = _tile(S, 256, 16)
    tile, vec = _row_specs(ts, D)

    def body(x_ref, a_ref, gate_ref, g_ref, sc_ref, sh_ref, x1_ref, h_ref):
        x1 = x_ref[...] + gate_ref[...] * a_ref[...]
        x1_ref[...] = x1
        r = lax.rsqrt(_mean1(x1 * x1) + EPS)
        h_ref[...] = ((x1 * r) * g_ref[...] * (1.0 + sc_ref[...]) + sh_ref[...]).astype(h_ref.dtype)

    return pl.pallas_call(body, name=name, grid=(S // ts,), in_specs=[tile, tile, vec, vec, vec, vec],
                          out_specs=[tile, tile],
                          out_shape=[jax.ShapeDtypeStruct((S, D), F32), jax.ShapeDtypeStruct((S, D), MXU_DTYPE)],
                          compiler_params=_params("parallel"))(x, attn, gate, g, scale, shift)


def _final_loss_bwd(x1, f, gate2, final_g, target, name):
    S, D = x1.shape
    ts = _tile(S, 256, 16)
    tile, vec = _row_specs(ts, D)
    loss_spec = pl.BlockSpec((1, LANE), lambda i: (0, 0))

    def body(x1_ref, f_ref, gate_ref, g_ref, t_ref, dx2_ref, df_ref, dgate_ref, dg_ref, loss_ref):
        @pl.when(pl.program_id(0) == 0)
        def _():
            dgate_ref[...] = jnp.zeros_like(dgate_ref)
            dg_ref[...] = jnp.zeros_like(dg_ref)
            loss_ref[...] = jnp.zeros_like(loss_ref)

        fv, gate, g = f_ref[...], gate_ref[...], g_ref[...]
        x2 = x1_ref[...] + gate * fv
        r = lax.rsqrt(_mean1(x2 * x2) + EPS)
        xn = x2 * r
        err = xn * g - t_ref[...]
        loss_ref[...] += jnp.broadcast_to(0.5 * _sum0(_mean1(err * err)), loss_ref.shape)
        dy = err * (1.0 / D)
        dg_ref[...] += _sum0(dy * xn)
        dxn = dy * g
        dx2 = r * (dxn - xn * _mean1(dxn * xn))
        dx2_ref[...] = dx2
        dgate_ref[...] += _sum0(dx2 * fv)
        df_ref[...] = (dx2 * gate).astype(df_ref.dtype)

    return pl.pallas_call(
        body, name=name, grid=(S // ts,), in_specs=[tile, tile, vec, vec, tile],
        out_specs=[tile, tile, vec, vec, loss_spec],
        out_shape=[jax.ShapeDtypeStruct((S, D), F32), jax.ShapeDtypeStruct((S, D), MXU_DTYPE),
                   jax.ShapeDtypeStruct((1, D), F32), jax.ShapeDtypeStruct((1, D), F32),
                   jax.ShapeDtypeStruct((1, LANE), F32)],
        compiler_params=_params("arbitrary"),
    )(x1, f, gate2, final_g, target)


def _norm_mod_bwd(dh, xin, dres, g, scale, name, branch=None, gate=None):
    S, D = xin.shape
    ts = _tile(S, 256, 16)
    tile, vec = _row_specs(ts, D)
    with_gate = branch is not None

    def body(*refs):
        if with_gate:
            dh_ref, x_ref, dres_ref, g_ref, sc_ref, br_ref, gate_ref, dx_ref, dshift_ref, dscale_ref, dg_ref, dgate_ref, dbr_ref = refs
            accs = (dshift_ref, dscale_ref, dg_ref, dgate_ref)
        else:
            dh_ref, x_ref, dres_ref, g_ref, sc_ref, dx_ref, dshift_ref, dscale_ref, dg_ref = refs
            accs = (dshift_ref, dscale_ref, dg_ref)

        @pl.when(pl.program_id(0) == 0)
        def _():
            for acc in accs:
                acc[...] = jnp.zeros_like(acc)

        dh_v, xv, g_v = dh_ref[...], x_ref[...], g_ref[...]
        one_sc = 1.0 + sc_ref[...]
        r = lax.rsqrt(_mean1(xv * xv) + EPS)
        xn = xv * r
        dshift_ref[...] += _sum0(dh_v)
        dscale_ref[...] += _sum0(dh_v * (xn * g_v))
        dg_ref[...] += _sum0(dh_v * one_sc * xn)
        dxn = dh_v * (g_v * one_sc)
        dx = dres_ref[...] + r * (dxn - xn * _mean1(dxn * xn))
        dx_ref[...] = dx
        if with_gate:
            dgate_ref[...] += _sum0(dx * br_ref[...])
            dbr_ref[...] = (dx * gate_ref[...]).astype(dbr_ref.dtype)

    ins = [dh, xin, dres, g, scale] + ([branch, gate] if with_gate else [])
    in_specs = [tile, tile, tile, vec, vec] + ([tile, vec] if with_gate else [])
    out_specs = [tile, vec, vec, vec] + ([vec, tile] if with_gate else [])
    out_shape = [jax.ShapeDtypeStruct((S, D), F32)] + [jax.ShapeDtypeStruct((1, D), F32)] * 3
    if with_gate:
        out_shape += [jax.ShapeDtypeStruct((1, D), F32), jax.ShapeDtypeStruct((S, D), MXU_DTYPE)]
    return pl.pallas_call(body, name=name, grid=(S // ts,), in_specs=in_specs, out_specs=out_specs,
                          out_shape=out_shape, compiler_params=_params("arbitrary"))(*ins)


def _causal_weights(ws_ref, wt_ref, n_g):
    row = lax.broadcasted_iota(jnp.int32, (LANE, LANE), 0)
    col = lax.broadcasted_iota(jnp.int32, (LANE, LANE), 1)
    for g in range(n_g):
        wt_ref[g] = jnp.where(col <= row, ws_ref[g], 0.0).astype(wt_ref.dtype)


def _group_layernorm(v):
    xc = v - _mean1(v)
    rstd = lax.rsqrt(_mean1(xc * xc) + EPS)
    return xc * rstd, rstd


def _gmlp_fwd(proj, v_gain, w_s, b_t, out_gain, n_g, name):
    S = proj.shape[0]
    GW = n_g * LANE

    def body(p_ref, vg_ref, ws_ref, bt_ref, og_ref, on_ref, wt_ref):
        @pl.when(pl.program_id(0) == 0)
        def _():
            _causal_weights(ws_ref, wt_ref, n_g)

        for g in range(n_g):
            cols = slice(g * LANE, (g + 1) * LANE)
            u = _gelu(p_ref[:, cols])
            v = _gelu(p_ref[:, GW + g * LANE:GW + (g + 1) * LANE])
            vhat, _ = _group_layernorm(v)
            vln = (vhat * vg_ref[:, cols]).astype(MXU_DTYPE)
            mixed = _dot(wt_ref[g], vln, NN) + bt_ref[:, g:g + 1]
            o = u * mixed
            r = lax.rsqrt(_mean1(o * o) + EPS)
            on_ref[:, cols] = (o * r * og_ref[:, cols]).astype(on_ref.dtype)

    return pl.pallas_call(
        body, name=name, grid=(S // LANE,),
        in_specs=[pl.BlockSpec((LANE, 2 * GW), lambda n: (n, 0)),
                  pl.BlockSpec((1, GW), lambda n: (0, 0)),
                  pl.BlockSpec((n_g, LANE, LANE), lambda n: (0, 0, 0)),
                  pl.BlockSpec((LANE, n_g), lambda n: (0, 0)),
                  pl.BlockSpec((1, GW), lambda n: (0, 0))],
        out_specs=pl.BlockSpec((LANE, GW), lambda n: (n, 0)),
        out_shape=jax.ShapeDtypeStruct((S, GW), MXU_DTYPE),
        scratch_shapes=[pltpu.VMEM((n_g, LANE, LANE), MXU_DTYPE)],
        compiler_params=_params("arbitrary"),
    )(proj, v_gain, w_s, b_t, out_gain)


def _gmlp_bwd(proj, d_on, v_gain, w_s, b_t, out_gain, n_g, name):
    S = proj.shape[0]
    GW = n_g * LANE

    def body(p_ref, dn_ref, vg_ref, ws_ref, bt_ref, og_ref, dp_ref, dws_ref, dbt_ref, dvg_ref, dog_ref, wt_ref):
        @pl.when(pl.program_id(0) == 0)
        def _():
            _causal_weights(ws_ref, wt_ref, n_g)
            dws_ref[...] = jnp.zeros_like(dws_ref)
            dbt_ref[...] = jnp.zeros_like(dbt_ref)
            dvg_ref[...] = jnp.zeros_like(dvg_ref)
            dog_ref[...] = jnp.zeros_like(dog_ref)

        row = lax.broadcasted_iota(jnp.int32, (LANE, LANE), 0)
        col = lax.broadcasted_iota(jnp.int32, (LANE, LANE), 1)
        for g in range(n_g):
            cols = slice(g * LANE, (g + 1) * LANE)
            vcols = slice(GW + g * LANE, GW + (g + 1) * LANE)
            pu, pv = p_ref[:, cols], p_ref[:, vcols]
            u, v = _gelu(pu), _gelu(pv)
            vhat, rstd = _group_layernorm(v)
            gain = vg_ref[:, cols]
            vln = (vhat * gain).astype(MXU_DTYPE)
            mixed = _dot(wt_ref[g], vln, NN) + bt_ref[:, g:g + 1]
            o = u * mixed
            r = lax.rsqrt(_mean1(o * o) + EPS)
            oh = o * r
            dn = dn_ref[:, cols]
            dog_ref[:, cols] += _sum0(dn * oh)
            dhn = dn * og_ref[:, cols]
            d_o = r * (dhn - oh * _mean1(dhn * oh))
            du = d_o * mixed
            dmix = d_o * u
            dbt_ref[:, g:g + 1] += jnp.sum(dmix, axis=1, keepdims=True)
            dmix_b = dmix.astype(MXU_DTYPE)
            dws_ref[g] += jnp.where(col <= row, _dot(dmix_b, vln, NT), 0.0)
            dvln = _dot(wt_ref[g], dmix_b, TN)
            dvg_ref[:, cols] += _sum0(dvln * vhat)
            dxh = dvln * gain
            dv = rstd * (dxh - _mean1(dxh) - vhat * _mean1(dxh * vhat))
            dp_ref[:, cols] = (du * _gelu_grad(pu)).astype(dp_ref.dtype)
            dp_ref[:, vcols] = (dv * _gelu_grad(pv)).astype(dp_ref.dtype)

    return pl.pallas_call(
        body, name=name, grid=(S // LANE,),
        in_specs=[pl.BlockSpec((LANE, 2 * GW), lambda n: (n, 0)),
                  pl.BlockSpec((LANE, GW), lambda n: (n, 0)),
                  pl.BlockSpec((1, GW), lambda n: (0, 0)),
                  pl.BlockSpec((n_g, LANE, LANE), lambda n: (0, 0, 0)),
                  pl.BlockSpec((LANE, n_g), lambda n: (0, 0)),
                  pl.BlockSpec((1, GW), lambda n: (0, 0))],
        out_specs=[pl.BlockSpec((LANE, 2 * GW), lambda n: (n, 0)),
                   pl.BlockSpec((n_g, LANE, LANE), lambda n: (0, 0, 0)),
                   pl.BlockSpec((LANE, n_g), lambda n: (0, 0)),
                   pl.BlockSpec((1, GW), lambda n: (0, 0)),
                   pl.BlockSpec((1, GW), lambda n: (0, 0))],
        out_shape=[jax.ShapeDtypeStruct((S, 2 * GW), MXU_DTYPE),
                   jax.ShapeDtypeStruct((n_g, LANE, LANE), F32),
                   jax.ShapeDtypeStruct((LANE, n_g), F32),
                   jax.ShapeDtypeStruct((1, GW), F32),
                   jax.ShapeDtypeStruct((1, GW), F32)],
        scratch_shapes=[pltpu.VMEM((n_g, LANE, LANE), MXU_DTYPE)],
        compiler_params=_params("arbitrary"),
    )(proj, d_on, v_gain, w_s, b_t, out_gain)


def _tri_sum(v, tri, exact=True):
    hi = v.astype(MXU_DTYPE)
    if not exact:
        return _dot(hi, tri, NN)
    lo = (v - hi.astype(F32)).astype(MXU_DTYPE)
    return _dot(hi, tri, NN) + _dot(lo, tri, NN)


def _log_sigmoids(z):
    sp = jnp.log(1.0 + jnp.exp(-jnp.abs(z)))
    return jnp.minimum(z, 0.0) - sp, jnp.minimum(-z, 0.0) - sp


def _rows(i, size):
    return pl.ds(pl.multiple_of(i * size, size), size)


SB_QUERY_TILE = 1024
SB_KEY_TILE = 256


def _sb_tiles(S):
    tq = _tile(S, SB_QUERY_TILE)
    tk = _tile(tq, SB_KEY_TILE)
    assert (tq // tk) % 2 == 0, "the key sweep takes two blocks a pass"
    return tq, tk, S // tq, tq // tk


def _triangle(n, keep):
    row = lax.broadcasted_iota(jnp.int32, (n, n), 0)
    col = lax.broadcasted_iota(jnp.int32, (n, n), 1)
    return jnp.where(keep(row, col), 1.0, 0.0).astype(MXU_DTYPE)


def _strictly_before(tq, tk, key_offset):
    row = lax.broadcasted_iota(jnp.int32, (tq, tk), 0)
    col = lax.broadcasted_iota(jnp.int32, (tq, tk), 1)
    return col + key_offset < row


def _sb_specs(S, n_g, n_h):
    base = 2 * n_g
    q_spec = pl.BlockSpec((S, LANE), lambda h: (0, base + h))
    k_spec = pl.BlockSpec((S, LANE), lambda h: (0, base + n_h + h))
    v_spec = pl.BlockSpec((S, LANE), lambda h: (0, base + 2 * n_h + h))
    gain_spec = pl.BlockSpec((1, LANE), lambda h: (0, n_g + h))
    head_spec = pl.BlockSpec((S, LANE), lambda h: (0, h))
    return q_spec, k_spec, v_spec, gain_spec, head_spec


def _sb_fwd(proj, out_gain, n_g, n_h, name):
    S = proj.shape[0]
    TQ, TK, NQ, KPQ = _sb_tiles(S)
    scale = LANE ** -0.5
    q_spec, k_spec, v_spec, gain_spec, head_spec = _sb_specs(S, n_g, n_h)

    def body(q_ref, k_ref, v_ref, og_ref, o_ref, on_ref, ls_ref, qb, kb, vb):
        qb[...] = q_ref[...].astype(MXU_DTYPE)
        kb[...] = k_ref[...].astype(MXU_DTYPE)
        vb[...] = v_ref[...].astype(MXU_DTYPE)
        after = _triangle(TK, lambda r, c: r > c)

        def block(qi, j, ctail, acc, key_offset):
            skip = key_offset or 0
            z = _dot(qi[skip:], kb[_rows(j, TK), :], NT) * scale
            lb, l1m = _log_sigmoids(z)
            if key_offset is not None:
                strict = _strictly_before(TQ - skip, TK, 0)
                l1m = jnp.where(strict, l1m, 0.0)
            a = jnp.exp(lb + ctail[skip:] + _tri_sum(l1m, after))
            if key_offset is not None:
                a = jnp.where(strict, a, 0.0)
            acc_new = acc[skip:] + _dot(a.astype(MXU_DTYPE), vb[_rows(j, TK), :], NN)
            ctail_new = ctail[skip:] + jnp.sum(l1m, axis=1, keepdims=True)
            if skip:
                ctail_new = jnp.concatenate([ctail[:skip], ctail_new], axis=0)
                acc_new = jnp.concatenate([acc[:skip], acc_new], axis=0)
            return ctail_new, acc_new

        def q_loop(i, carry):
            qi = qb[_rows(i, TQ), :]
            state = (jnp.zeros((TQ, 1), F32), jnp.zeros((TQ, LANE), F32))
            for d in reversed(range(KPQ)):
                state = block(qi, i * KPQ + d, state[0], state[1], d * TK)
            def pair(jj, st):
                st = block(qi, i * KPQ - 1 - 2 * jj, st[0], st[1], None)
                return block(qi, i * KPQ - 2 - 2 * jj, st[0], st[1], None)

            ctail, acc = lax.fori_loop(0, i * (KPQ // 2), pair, state)
            ls_ref[_rows(i, TQ), :] = jnp.broadcast_to(ctail, (TQ, LANE))
            o_ref[_rows(i, TQ), :] = acc
            r = lax.rsqrt(_mean1(acc * acc) + EPS)
            on_ref[_rows(i, TQ), :] = (acc * r * og_ref[...]).astype(on_ref.dtype)
            return carry

        lax.fori_loop(0, NQ, q_loop, 0)

    return pl.pallas_call(
        body, name=name, grid=(n_h,),
        in_specs=[q_spec, k_spec, v_spec, gain_spec],
        out_specs=[head_spec, head_spec, head_spec],
        out_shape=[jax.ShapeDtypeStruct((S, n_h * LANE), F32), jax.ShapeDtypeStruct((S, n_h * LANE), MXU_DTYPE),
                   jax.ShapeDtypeStruct((S, n_h * LANE), F32)],
        scratch_shapes=[pltpu.VMEM((S, LANE), MXU_DTYPE)] * 3,
        compiler_params=_params("parallel"),
    )(proj, proj, proj, out_gain)


def _sb_bwd(proj, o_sb, l_sum, d_on, out_gain, n_g, n_h, name):
    S = proj.shape[0]
    TQ, TK, NQ, KPQ = _sb_tiles(S)
    scale = LANE ** -0.5
    q_spec, k_spec, v_spec, gain_spec, head_spec = _sb_specs(S, n_g, n_h)
    dn_spec = pl.BlockSpec((S, LANE), lambda h: (0, n_g + h))
    dgain_spec = pl.BlockSpec((1, LANE), lambda h: (0, h))

    def body(q_ref, k_ref, v_ref, o_ref, ls_ref, dn_ref, og_ref, dq_ref, dk_ref, dv_ref, dog_ref,
             qb, kb, vb, dob, dk_acc, dv_acc):
        qb[...] = q_ref[...].astype(MXU_DTYPE)
        kb[...] = k_ref[...].astype(MXU_DTYPE)
        vb[...] = v_ref[...].astype(MXU_DTYPE)
        o, dn = o_ref[...], dn_ref[...]
        r = lax.rsqrt(_mean1(o * o) + EPS)
        oh = o * r
        dog_ref[...] = _sum0(dn * oh)
        dhn = dn * og_ref[...]
        dob[...] = (r * (dhn - oh * _mean1(dhn * oh))).astype(MXU_DTYPE)
        dk_acc[...] = jnp.zeros_like(dk_acc)
        dv_acc[...] = jnp.zeros_like(dv_acc)

        up_to = _triangle(TK, lambda r, c: r <= c)
        before = _triangle(TK, lambda r, c: r < c)

        def block(qi, doi, ltot, j, cl, cdl, dq, key_offset):
            skip = key_offset or 0
            q_in, do_in = qi[skip:], doi[skip:]
            kj, vj = kb[_rows(j, TK), :], vb[_rows(j, TK), :]
            z = _dot(q_in, kj, NT) * scale
            lb, l1m = _log_sigmoids(z)
            if key_offset is not None:
                strict = _strictly_before(TQ - skip, TK, 0)
                l1m = jnp.where(strict, l1m, 0.0)
            a = jnp.exp(lb + (ltot[skip:] - (cl[skip:] + _tri_sum(l1m, up_to))))
            if key_offset is not None:
                a = jnp.where(strict, a, 0.0)
            dl = _dot(do_in, vj, NT) * a
            d_l1m = cdl[skip:] + _tri_sum(dl, before, exact=False)
            beta = jnp.exp(lb)
            dz = dl * (1.0 - beta) - beta * d_l1m
            if key_offset is not None:
                dz = jnp.where(strict, dz, 0.0)
            dzs = (dz * scale).astype(MXU_DTYPE)
            dk_acc[_rows(j, TK), :] += _dot(dzs, q_in, TN)
            dv_acc[_rows(j, TK), :] += _dot(a.astype(MXU_DTYPE), do_in, TN)
            cl_new = cl[skip:] + jnp.sum(l1m, axis=1, keepdims=True)
            cdl_new = cdl[skip:] + jnp.sum(dl, axis=1, keepdims=True)
            dq_new = dq[skip:] + _dot(dzs, kj, NN)
            if skip:
                cl_new = jnp.concatenate([cl[:skip], cl_new], axis=0)
                cdl_new = jnp.concatenate([cdl[:skip], cdl_new], axis=0)
                dq_new = jnp.concatenate([dq[:skip], dq_new], axis=0)
            return cl_new, cdl_new, dq_new

        def q_loop(i, carry):
            qi, doi = qb[_rows(i, TQ), :], dob[_rows(i, TQ), :]
            ltot = ls_ref[_rows(i, TQ), :][:, :1]
            zero_col = jnp.zeros((TQ, 1), F32)
            def pair(jj, st):
                st = block(qi, doi, ltot, 2 * jj, st[0], st[1], st[2], None)
                return block(qi, doi, ltot, 2 * jj + 1, st[0], st[1], st[2], None)

            state = lax.fori_loop(0, i * (KPQ // 2), pair, (zero_col, zero_col, jnp.zeros((TQ, LANE), F32)))
            for d in range(KPQ):
                state = block(qi, doi, ltot, i * KPQ + d, state[0], state[1], state[2], d * TK)
            dq_ref[_rows(i, TQ), :] = state[2].astype(dq_ref.dtype)
            return carry

        lax.fori_loop(0, NQ, q_loop, 0)
        dk_ref[...] = dk_acc[...].astype(dk_ref.dtype)
        dv_ref[...] = dv_acc[...].astype(dv_ref.dtype)

    W = n_h * LANE
    return pl.pallas_call(
        body, name=name, grid=(n_h,),
        in_specs=[q_spec, k_spec, v_spec, head_spec, head_spec, dn_spec, gain_spec],
        out_specs=[head_spec, head_spec, head_spec, dgain_spec],
        out_shape=[jax.ShapeDtypeStruct((S, W), MXU_DTYPE)] * 3 + [jax.ShapeDtypeStruct((1, W), F32)],
        scratch_shapes=[pltpu.VMEM((S, LANE), MXU_DTYPE)] * 4 + [pltpu.VMEM((S, LANE), F32)] * 2,
        compiler_params=_params("parallel"),
    )(proj, proj, proj, o_sb, l_sum, d_on, out_gain)


def _mod_part(c_all, w_ada, b_ada_cols, name):
    B, K = c_all.shape
    N = w_ada.shape[1]
    tn = _tile(N, 512)

    def body(c_ref, w_ref, b_ref, o_ref):
        cv = c_ref[...]
        ca = (cv * jax.nn.sigmoid(cv)).astype(MXU_DTYPE)
        o_ref[...] = _dot(ca, w_ref[...].astype(MXU_DTYPE), NN) + b_ref[...]

    return pl.pallas_call(
        body, name=name, grid=(N // tn,),
        in_specs=[pl.BlockSpec((B, K), lambda j: (0, 0)), pl.BlockSpec((K, tn), lambda j: (0, j)),
                  pl.BlockSpec((1, tn), lambda j: (0, j))],
        out_specs=pl.BlockSpec((B, tn), lambda j: (0, j)),
        out_shape=jax.ShapeDtypeStruct((B, N), F32), compiler_params=_params("parallel"))(c_all, w_ada, b_ada_cols)


def _adamw_math(w, g, m, v):
    m = ADAM_B1 * m + (1.0 - ADAM_B1) * g
    v = ADAM_B2 * v + (1.0 - ADAM_B2) * (g * g)
    m_hat = m / (1.0 - ADAM_B1 ** ADAM_STEP)
    v_hat = v / (1.0 - ADAM_B2 ** ADAM_STEP)
    delta = -ADAM_LR * (m_hat / (jnp.sqrt(v_hat) + ADAM_EPS) + ADAM_WD * w)
    return delta, m, v


def _adamw(w, g, m, v, name):
    R, C = w.shape
    tr = _tile(R, max(8, (1 << 19) // C), 8)
    spec = pl.BlockSpec((tr, C), lambda i: (i, 0))

    def body(w_ref, g_ref, m_ref, v_ref, go_ref, d_ref, mo_ref, vo_ref):
        g = g_ref[...]
        go_ref[...] = g
        d_ref[...], mo_ref[...], vo_ref[...] = _adamw_math(w_ref[...], g, m_ref[...], v_ref[...])

    return pl.pallas_call(body, name=name, grid=(R // tr,), in_specs=[spec] * 4, out_specs=[spec] * 4,
                          out_shape=[jax.ShapeDtypeStruct((R, C), F32)] * 4, compiler_params=_params("parallel"))(w, g, m, v)


def _adamw_ada(c_all, dmod_cols, w, m, v, name):
    K, N = w.shape
    B = c_all.shape[0]
    tk, tn = _tile(K, 512), _tile(N, 1024)
    spec = pl.BlockSpec((tk, tn), lambda i, j: (i, j))

    def body(c_ref, dm_ref, w_ref, m_ref, v_ref, g_ref, d_ref, mo_ref, vo_ref):
        cv = c_ref[...]
        ca = (cv * jax.nn.sigmoid(cv)).astype(MXU_DTYPE)
        g = _dot(ca, dm_ref[...].astype(MXU_DTYPE), TN)
        g_ref[...] = g
        d_ref[...], mo_ref[...], vo_ref[...] = _adamw_math(w_ref[...], g, m_ref[...], v_ref[...])

    return pl.pallas_call(
        body, name=name, grid=(K // tk, N // tn),
        in_specs=[pl.BlockSpec((B, tk), lambda i, j: (0, i)), pl.BlockSpec((B, tn), lambda i, j: (0, j)), spec, spec, spec],
        out_specs=[spec] * 4, out_shape=[jax.ShapeDtypeStruct((K, N), F32)] * 4,
        compiler_params=_params("parallel", "parallel"))(c_all, dmod_cols, w, m, v)


def _sum_devices(gathered, n_dev, name):
    R = gathered.shape[0] // n_dev
    C = gathered.shape[1]
    tr = _tile(R, 512, 8)
    n_blk = R // tr

    def body(*refs):
        acc = refs[0][...]
        for r in refs[1:n_dev]:
            acc = acc + r[...]
        refs[n_dev][...] = acc

    in_specs = [pl.BlockSpec((tr, C), functools.partial(lambda i, d: (d * n_blk + i, 0), d=d)) for d in range(n_dev)]
    return pl.pallas_call(body, name=name, grid=(n_blk,), in_specs=in_specs,
                          out_specs=pl.BlockSpec((tr, C), lambda i: (i, 0)),
                          out_shape=jax.ShapeDtypeStruct((R, C), F32), compiler_params=_params("parallel"))(*([gathered] * n_dev))


def _place():
    x, y, c = lax.axis_index("x"), lax.axis_index("y"), lax.axis_index("c")
    return x, y, c


def _allgather8(blk, name):
    m_per, n = blk.shape

    def body(x_ref, out_ref, send_sems, recv_sems, local_sem):
        x, y, c = _place()
        me, sibling = (x, y, c), (x, y, 1 - c)
        chips = [(1 - x, y), (x, 1 - y), (1 - x, 1 - y)]

        def rows(px, py, pc):
            return out_ref.at[pl.ds((4 * px + 2 * py + pc) * m_per, m_per), :]

        def copy(k, block, to, src=None):
            return pltpu.make_async_remote_copy(
                src_ref=rows(*block) if src is None else src, dst_ref=rows(*block),
                send_sem=send_sems.at[k], recv_sem=recv_sems.at[k], device_id=to, device_id_type=MESH)

        mine = pltpu.make_async_copy(x_ref, rows(*me), local_sem)
        mine.start()
        first = [copy(0, me, sibling, src=x_ref)]
        first += [copy(1 + j, me, (*chip, c), src=x_ref) for j, chip in enumerate(chips)]
        for cp in first:
            cp.start()
        passed = [copy(4 + j, (*chip, c), sibling) for j, chip in enumerate(chips)]
        for j, chip in enumerate(chips):
            copy(1 + j, (*chip, c), me).wait_recv()
            passed[j].start()
        copy(0, sibling, me).wait_recv()
        for j, chip in enumerate(chips):
            copy(4 + j, (*chip, 1 - c), me).wait_recv()
        for cp in first + passed:
            cp.wait_send()
        mine.wait()

    return pl.pallas_call(
        body, name=name,
        out_shape=jax.ShapeDtypeStruct((8 * m_per, n), blk.dtype),
        in_specs=[pl.BlockSpec(memory_space=pltpu.VMEM)],
        out_specs=pl.BlockSpec(memory_space=pltpu.VMEM),
        scratch_shapes=[pltpu.SemaphoreType.DMA((7,)), pltpu.SemaphoreType.DMA((7,)), pltpu.SemaphoreType.DMA],
        compiler_params=pltpu.CompilerParams(vmem_limit_bytes=V7X_VMEM_LIMIT),
    )(blk)


class _Sharded:
    def __init__(self, shard_shape, by_cols):
        r, c = shard_shape
        self.by_cols = by_cols
        self.full = (r, N_CHIPS * c) if by_cols else (N_CHIPS * r, c)
        self.shard = (r, c)
        self.half_rows = r // 2
        self.half = (r // 2, c)

    def shard_of(self, ref, k):
        r, c = self.shard
        return ref.at[:, pl.ds(k * c, c)] if self.by_cols else ref.at[pl.ds(k * r, r), :]

    def half_of(self, ref, k, hc):
        r, c = self.shard
        h = self.half_rows
        if self.by_cols:
            return ref.at[pl.ds(hc * h, h), pl.ds(k * c, c)]
        return ref.at[pl.ds(k * r + hc * h, h), :]

    def chunk_of(self, ref, k, hc, ch, n):
        r, c = self.shard
        h = self.half_rows
        q = h // n
        if self.by_cols:
            return ref.at[pl.ds(hc * h + ch * q, q), pl.ds(k * c, c)]
        return ref.at[pl.ds(k * r + hc * h + ch * q, q), :]

    def half_of_shard(self, ref, hc):
        return ref.at[pl.ds(hc * self.half_rows, self.half_rows), :]

    def part_of_halves(self, ref, k):
        r, c = self.shard
        h = self.half_rows
        return ref.at[:, pl.ds(k * c, c)] if self.by_cols else ref.at[pl.ds(k * h, h), :]


def _on_each_place(x, y, c, fn, by_chip=True, by_core=True):
    q = 2 * x + y
    for k in range(N_CHIPS if by_chip else 1):
        for cc in range(2 if by_core else 1):
            cond = None
            if by_chip:
                cond = q == k
            if by_core:
                cond = (c == cc) if cond is None else jnp.logical_and(cond, c == cc)
            pl.when(cond)(functools.partial(fn, k, cc))


def _chip_id(k, c):
    return (k // 2, k % 2, c)


def _handshake(peers):
    barrier = pltpu.get_barrier_semaphore()
    for peer in peers:
        pl.semaphore_signal(barrier, inc=1, device_id=peer, device_id_type=MESH)
    pl.semaphore_wait(barrier, len(peers))


def _on_sequencer(body, inputs, out_structs, n_copies, peers_of, name, collective_id, return_inputs=False):
    in_refs = [jax.new_ref(a, memory_space=pltpu.MemorySpace.HBM) for a in inputs]
    out_refs = [jax.empty_ref(s, memory_space=pltpu.MemorySpace.HBM) for s in out_structs]

    @pl.kernel(mesh=plsc.ScalarSubcoreMesh(axis_name="sequencer", num_cores=1), name=name,
               scratch_types=(pltpu.SemaphoreType.DMA((n_copies,)), pltpu.SemaphoreType.DMA((n_copies,))),
               compiler_params=pltpu.CompilerParams(collective_id=collective_id))
    def launch(send_sems, recv_sems):
        x, y, c = _place()
        _handshake(peers_of(x, y, c))
        body(in_refs, out_refs, send_sems, recv_sems, x, y, c)

    launch()
    return [r[...] for r in (in_refs if return_inputs else out_refs)]


def _sibling(x, y, c):
    return [(x, y, 1 - c)]


def _same_core_of_other_chips(x, y, c):
    return [(1 - x, y, c), (x, 1 - y, c), (1 - x, 1 - y, c)]


GATHER_CHUNKS = 4
GATHER_COPIES = 6 * GATHER_CHUNKS


def _allgather8_on_sequencer(blk, name, collective_id):
    m_per, n = blk.shape
    x, y, c = _place()
    placed = lax.dynamic_update_slice(jnp.zeros((8 * m_per, n), blk.dtype), blk, ((4 * x + 2 * y + c) * m_per, 0))

    def body(refs, _, send_sems, recv_sems, x, y, c):
        out_ref, = refs

        def at_place(k, cc):
            def rows(kk, pc):
                return out_ref.at[pl.ds((2 * kk + pc) * m_per, m_per), :]

            def copy(slot, block, to):
                return pltpu.make_async_remote_copy(src_ref=rows(*block), dst_ref=rows(*block), send_sem=send_sems.at[slot],
                                                    recv_sem=recv_sems.at[slot], device_id=to, device_id_type=MESH)

            others = [k ^ flip for flip in FLIPS]
            sends = [copy(0, (k, cc), _chip_id(k, 1 - cc))] + [copy(1 + j, (k, cc), _chip_id(kk, cc)) for j, kk in enumerate(others)]
            for cp in sends:
                cp.start()
            for j, kk in enumerate(others):
                copy(1 + j, (kk, cc), _chip_id(k, cc)).wait_recv()
                cp = copy(4 + j, (kk, cc), _chip_id(k, 1 - cc))
                cp.start()
                sends.append(cp)
            copy(0, (k, 1 - cc), _chip_id(k, cc)).wait_recv()
            for j, kk in enumerate(others):
                copy(4 + j, (kk, 1 - cc), _chip_id(k, cc)).wait_recv()
            for cp in sends:
                cp.wait_send()

        _on_each_place(x, y, c, at_place)

    def peers(x, y, c):
        return _sibling(x, y, c) + _same_core_of_other_chips(x, y, c)

    return _on_sequencer(body, [placed], [], 7, peers, name, collective_id, return_inputs=True)[0]


def _gather_weights(fulls, geoms, name, collective_id):
    n_w = len(fulls)
    n_ch, n_relay = GATHER_CHUNKS, GATHER_CHUNKS // 2
    f_refs = [jax.new_ref(f, memory_space=pltpu.MemorySpace.HBM) for f in fulls]
    FLIP_X, FLIP_Y, FLIP_BOTH = FLIPS
    TO_X, TO_Y, RELAY_TO_Y, RELAY_TO_X, ON_X, ON_Y, ON_DIAG = 0, n_ch, 2 * n_ch, 2 * n_ch + n_relay, 3 * n_ch, 4 * n_ch, 5 * n_ch

    @pl.kernel(mesh=plsc.ScalarSubcoreMesh(axis_name="sequencer", num_cores=1), name=name,
               scratch_types=(pltpu.SemaphoreType.DMA((GATHER_COPIES * n_w,)), pltpu.SemaphoreType.DMA((GATHER_COPIES * n_w,))),
               compiler_params=pltpu.CompilerParams(collective_id=collective_id))
    def launch(send_sems, recv_sems):
        x, y, c = _place()
        _handshake([(x, y, 1 - c), (1 - x, y, c), (x, 1 - y, c)])

        def at_place(k, cc):
            kx, ky, kd = k ^ FLIP_X, k ^ FLIP_Y, k ^ FLIP_BOTH
            me, sibling = _chip_id(k, cc), _chip_id(k, 1 - cc)
            started = []

            def copy(i, slot, src, dst, to, start=True):
                cp = pltpu.make_async_remote_copy(src_ref=src, dst_ref=dst, send_sem=send_sems.at[GATHER_COPIES * i + slot],
                                                  recv_sem=recv_sems.at[GATHER_COPIES * i + slot], device_id=to, device_id_type=MESH)
                if start:
                    cp.start()
                    started.append(cp)
                return cp

            def pass_on(i, slot, ref, to):
                copy(i, slot, ref, ref, to)

            def landed(i, slot, ref):
                copy(i, slot, ref, ref, me, start=False).wait_recv()

            y_order = [(n_relay + s) % n_ch for s in range(n_ch)]
            for i, (g, f_ref) in enumerate(zip(geoms, f_refs)):
                for s in range(n_ch):
                    pass_on(i, TO_X + s, g.chunk_of(f_ref, k, cc, s, n_ch), _chip_id(kx, cc))
                    pass_on(i, TO_Y + y_order[s], g.chunk_of(f_ref, k, cc, y_order[s], n_ch), _chip_id(ky, cc))
            for i, (g, f_ref) in enumerate(zip(geoms, f_refs)):
                for s in range(n_ch):
                    from_x = g.chunk_of(f_ref, kx, cc, s, n_ch)
                    landed(i, TO_X + s, from_x)
                    if s < n_relay:
                        pass_on(i, RELAY_TO_Y + s, from_x, _chip_id(ky, cc))
                    pass_on(i, ON_X + s, from_x, sibling)
                    ch = y_order[s]
                    from_y = g.chunk_of(f_ref, ky, cc, ch, n_ch)
                    landed(i, TO_Y + ch, from_y)
                    if ch >= n_relay:
                        pass_on(i, RELAY_TO_X + ch - n_relay, from_y, _chip_id(kx, cc))
                    pass_on(i, ON_Y + ch, from_y, sibling)
                for r in range(n_relay):
                    via_y = g.chunk_of(f_ref, kd, cc, r, n_ch)
                    landed(i, RELAY_TO_Y + r, via_y)
                    pass_on(i, ON_DIAG + r, via_y, sibling)
                    via_x = g.chunk_of(f_ref, kd, cc, n_relay + r, n_ch)
                    landed(i, RELAY_TO_X + r, via_x)
                    pass_on(i, ON_DIAG + n_relay + r, via_x, sibling)
            for i, (g, f_ref) in enumerate(zip(geoms, f_refs)):
                for slot, kk in ((ON_X, kx), (ON_Y, ky), (ON_DIAG, kd)):
                    for ch in range(n_ch):
                        landed(i, slot + ch, g.chunk_of(f_ref, kk, 1 - cc, ch, n_ch))
            for cp in started:
                cp.wait_send()

        _on_each_place(x, y, c, at_place)

    launch()
    return [f_ref[...] for f_ref in f_refs]


def _swap_core_halves(grads, geoms, name, collective_id):
    n_cp = sum(1 if g.by_cols else N_CHIPS for g in geoms)

    def body(g_refs, t_refs, send_sems, recv_sems, x, y, c):

        def at_place(_, cc):
            def pairs(hc):
                out = []
                for g, g_ref, t_ref in zip(geoms, g_refs, t_refs):
                    if g.by_cols:
                        out.append((g_ref.at[pl.ds(hc * g.half_rows, g.half_rows), :], t_ref))
                    else:
                        out += [(g.half_of(g_ref, k, hc), g.part_of_halves(t_ref, k)) for k in range(N_CHIPS)]
                return out

            sends = [pltpu.make_async_remote_copy(src_ref=src, dst_ref=dst, send_sem=send_sems.at[n],
                                                  recv_sem=recv_sems.at[n], device_id=(x, y, 1 - cc), device_id_type=MESH)
                     for n, (src, dst) in enumerate(pairs(1 - cc))]
            for cp in sends:
                cp.start()
            for n, (src, dst) in enumerate(pairs(cc)):
                pltpu.make_async_remote_copy(src_ref=src, dst_ref=dst, send_sem=send_sems.at[n], recv_sem=recv_sems.at[n],
                                             device_id=(x, y, cc), device_id_type=MESH).wait_recv()
            for cp in sends:
                cp.wait_send()

        _on_each_place(x, y, c, at_place, by_chip=False)

    return _on_sequencer(body, grads, [jax.ShapeDtypeStruct((g.full[0] // 2, g.full[1]), F32) for g in geoms],
                         n_cp, _sibling, name, collective_id)


def _send_to_sibling(buffers, name, collective_id):
    def body(src_refs, dst_refs, send_sems, recv_sems, x, y, c):
        def copy(i):
            return pltpu.make_async_remote_copy(src_ref=src_refs[i], dst_ref=dst_refs[i], send_sem=send_sems.at[i],
                                                recv_sem=recv_sems.at[i], device_id=(x, y, 1 - c), device_id_type=MESH)

        for i in range(len(buffers)):
            copy(i).start()
        for i in range(len(buffers)):
            copy(i).wait()

    return _on_sequencer(body, buffers, [jax.ShapeDtypeStruct(t.shape, t.dtype) for t in buffers], len(buffers),
                         _sibling, name, collective_id)


def _scatter_chip_sums(sums, geoms, name, collective_id):
    def body(s_refs, r_refs, send_sems, recv_sems, x, y, c):

        def at_place(k, _):
            sends = []
            for i, (g, s_ref, r_ref) in enumerate(zip(geoms, s_refs, r_refs)):
                for j, flip in enumerate(FLIPS):
                    kk = k ^ flip
                    cp = pltpu.make_async_remote_copy(
                        src_ref=g.part_of_halves(s_ref, kk), dst_ref=r_ref.at[j], send_sem=send_sems.at[3 * i + j],
                        recv_sem=recv_sems.at[3 * i + j], device_id=(kk // 2, kk % 2, c), device_id_type=MESH)
                    cp.start()
                    sends.append(cp)
            for i, (g, s_ref, r_ref) in enumerate(zip(geoms, s_refs, r_refs)):
                for j in range(len(FLIPS)):
                    pltpu.make_async_remote_copy(
                        src_ref=g.part_of_halves(s_ref, k), dst_ref=r_ref.at[j], send_sem=send_sems.at[3 * i + j],
                        recv_sem=recv_sems.at[3 * i + j], device_id=(x, y, c), device_id_type=MESH).wait_recv()
            for cp in sends:
                cp.wait_send()

        _on_each_place(x, y, c, at_place, by_core=False)

    return _on_sequencer(body, sums, [jax.ShapeDtypeStruct((len(FLIPS),) + g.half, WIRE_DTYPE) for g in geoms],
                         len(FLIPS) * len(sums), _same_core_of_other_chips, name, collective_id)


def _share_reduced_halves(reduced, geoms, name, collective_id):
    def body(out_refs, _, send_sems, recv_sems, x, y, c):

        def at_place(_, cc):
            sends = []
            for i, (g, ref) in enumerate(zip(geoms, out_refs)):
                mine = g.half_of_shard(ref, cc)
                cp = pltpu.make_async_remote_copy(src_ref=mine, dst_ref=mine, send_sem=send_sems.at[i],
                                                  recv_sem=recv_sems.at[i], device_id=(x, y, 1 - cc), device_id_type=MESH)
                cp.start()
                sends.append(cp)
            for i, (g, ref) in enumerate(zip(geoms, out_refs)):
                theirs = g.half_of_shard(ref, 1 - cc)
                pltpu.make_async_remote_copy(src_ref=theirs, dst_ref=theirs, send_sem=send_sems.at[i],
                                             recv_sem=recv_sems.at[i], device_id=(x, y, cc), device_id_type=MESH).wait_recv()
            for cp in sends:
                cp.wait_send()

        _on_each_place(x, y, c, at_place, by_chip=False)

    return _on_sequencer(body, reduced, [], len(reduced), _sibling, name, collective_id, return_inputs=True)


def _chip_sum(place, grad, theirs, g, name):
    RH, C = theirs.shape
    h = g.half_rows
    tr = _tile(h, 256, 16)
    tc = _tile(C, 2048)
    per_half = h // tr

    if g.by_cols:
        grad_map = lambda i, j, p: (p[1] * per_half + i, j)
    else:
        grad_map = lambda i, j, p: ((i // per_half) * 2 * per_half + p[1] * per_half + i % per_half, j)

    def body(p_ref, a_ref, b_ref, f_ref, o_ref):
        total = a_ref[...] + b_ref[...]
        f_ref[...] = total
        o_ref[...] = total.astype(o_ref.dtype)

    return pl.pallas_call(
        body, name=name,
        grid_spec=pltpu.PrefetchScalarGridSpec(
            num_scalar_prefetch=1, grid=(RH // tr, C // tc),
            in_specs=[pl.BlockSpec((tr, tc), grad_map), pl.BlockSpec((tr, tc), lambda i, j, p: (i, j))],
            out_specs=[pl.BlockSpec((tr, tc), lambda i, j, p: (i, j))] * 2),
        out_shape=[jax.ShapeDtypeStruct((RH, C), F32), jax.ShapeDtypeStruct((RH, C), WIRE_DTYPE)],
        compiler_params=_params("parallel", "parallel"),
    )(place, grad, theirs)


def _dw_half(place, a, b, g, mine, name, add=None):
    K, R = a.shape
    C = b.shape[1]
    h = g.half_rows
    tm, tn = _tile(h, 1024, 16), _tile(C, 512)
    per_half = h // tm
    n_i = (R // 2) // tm

    def a_map(i, j, p):
        hc = p[1] if mine else 1 - p[1]
        if g.by_cols:
            return 0, hc * n_i + i
        return 0, (i // per_half) * 2 * per_half + hc * per_half + i % per_half

    mn_spec = pl.BlockSpec((tm, tn), lambda i, j, p: (i, j))

    def body(p_ref, a_ref, b_ref, *rest):
        acc = _dot(a_ref[...], b_ref[...], TN)
        if add is None:
            rest[0][...] = acc
        else:
            total = acc + rest[0][...]
            rest[1][...] = total
            rest[2][...] = total.astype(rest[2].dtype)

    out_shape = [jax.ShapeDtypeStruct((R // 2, C), F32)] + ([] if add is None else [jax.ShapeDtypeStruct((R // 2, C), WIRE_DTYPE)])
    return pl.pallas_call(
        body, name=name,
        grid_spec=pltpu.PrefetchScalarGridSpec(
            num_scalar_prefetch=1, grid=(n_i, C // tn),
            in_specs=[pl.BlockSpec((K, tm), a_map), pl.BlockSpec((K, tn), lambda i, j, p: (0, j))] + ([] if add is None else [mn_spec]),
            out_specs=[mn_spec] * len(out_shape)),
        out_shape=out_shape,
        compiler_params=_params("parallel", "arbitrary"),
    )(place, a, b, *([] if add is None else [add]))


def _reduce_half(place, sums, others, g, name):
    h, tc = g.half
    tr = _tile(h, 256, 16)
    per_half = h // tr
    sums_map = (lambda i, p: (i, p[0])) if g.by_cols else (lambda i, p: (p[0] * per_half + i, 0))

    def body(p_ref, s_ref, o0_ref, o1_ref, o2_ref, out_ref):
        acc = s_ref[...]
        for o_ref in (o0_ref, o1_ref, o2_ref):
            acc = acc + o_ref[...].astype(F32)
        out_ref[...] = acc

    other_specs = [pl.BlockSpec((None, tr, tc), functools.partial(lambda i, p, j: (j, i, 0), j=j)) for j in range(len(FLIPS))]
    return pl.pallas_call(
        body, name=name,
        grid_spec=pltpu.PrefetchScalarGridSpec(
            num_scalar_prefetch=1, grid=(per_half,),
            in_specs=[pl.BlockSpec((tr, tc), sums_map)] + other_specs,
            out_specs=pl.BlockSpec((tr, tc), lambda i, p: (p[1] * per_half + i, 0))),
        out_shape=jax.ShapeDtypeStruct(g.shard, F32),
        compiler_params=_params("arbitrary"),
    )(place, sums, others, others, others)


SMALL = ("b_ada", "norm1_g", "v_norm_g", "w_spatial", "b_spatial", "out_norm_g", "norm2_g", "final_g")
BIG = ("w_in", "w_out", "w_gate", "w_up", "w_down")
BY_COLS = {"w_in": True, "w_out": False, "w_gate": True, "w_up": True, "w_down": False}
ORDER = ("w_ada", "b_ada", "norm1_g", "w_in", "v_norm_g", "w_spatial", "b_spatial", "out_norm_g", "w_out",
         "norm2_g", "w_gate", "w_up", "w_down", "final_g")


def _pack(parts):
    return jnp.concatenate([parts[n].reshape(-1) for n in SMALL]).reshape(-1, LANE)


def _unpack(slab, shapes):
    flat = slab.reshape(-1)
    out, at = {}, 0
    for n in SMALL:
        size = math.prod(shapes[n])
        out[n] = flat[at:at + size].reshape(shapes[n])
        at += size
    return out


def kernel(x, c, w_ada, b_ada, norm1_g, w_in, v_norm_g, w_spatial, b_spatial, out_norm_g, w_out, norm2_g, w_gate, w_up, w_down, final_g, loss_target, m_w_ada, m_b_ada, m_norm1_g, m_w_in, m_v_norm_g, m_w_spatial, m_b_spatial, m_out_norm_g, m_w_out, m_norm2_g, m_w_gate, m_w_up, m_w_down, m_final_g, v_w_ada, v_b_ada, v_norm1_g, v_w_in, v_v_norm_g, v_w_spatial, v_b_spatial, v_out_norm_g, v_w_out, v_norm2_g, v_w_gate, v_w_up, v_w_down, v_final_g):
    weights = dict(w_ada=w_ada, b_ada=b_ada, norm1_g=norm1_g, w_in=w_in, v_norm_g=v_norm_g, w_spatial=w_spatial,
                   b_spatial=b_spatial, out_norm_g=out_norm_g, w_out=w_out, norm2_g=norm2_g, w_gate=w_gate, w_up=w_up,
                   w_down=w_down, final_g=final_g)
    m_in = dict(w_ada=m_w_ada, b_ada=m_b_ada, norm1_g=m_norm1_g, w_in=m_w_in, v_norm_g=m_v_norm_g, w_spatial=m_w_spatial,
                b_spatial=m_b_spatial, out_norm_g=m_out_norm_g, w_out=m_w_out, norm2_g=m_norm2_g, w_gate=m_w_gate,
                w_up=m_w_up, w_down=m_w_down, final_g=m_final_g)
    v_in = dict(w_ada=v_w_ada, b_ada=v_b_ada, norm1_g=v_norm1_g, w_in=v_w_in, v_norm_g=v_v_norm_g, w_spatial=v_w_spatial,
                b_spatial=v_b_spatial, out_norm_g=v_out_norm_g, w_out=v_w_out, norm2_g=v_norm2_g, w_gate=v_w_gate,
                w_up=v_w_up, w_down=v_w_down, final_g=v_final_g)

    S, D = x.shape[1], x.shape[2]
    n_g = v_norm_g.shape[-1] // LANE
    n_h = (D - n_g * LANE) // LANE
    GW = n_g * LANE
    xi, yi, ci = _place()
    chip = 2 * xi + yi
    me = 4 * xi + 2 * yi + ci
    place = jnp.stack([chip, ci]).astype(jnp.int32)

    xs, target = x[0], loss_target[0]
    geoms = [_Sharded(weights[n].shape[1:], BY_COLS[n]) for n in BIG]

    full = {}
    for i, group in enumerate((("w_in",), ("w_out",), ("w_gate", "w_up"), ("w_down",))):
        gg = [geoms[BIG.index(n)] for n in group]
        own = [_cast_into_full(place, weights[n][0], g, "cast_" + n) for n, g in zip(group, gg)]
        gathered = _gather_weights(own, gg, "gather_" + "_".join(group), 1 + i)
        full.update(zip(group, gathered))

    c_pad = jnp.concatenate([c, jnp.zeros((7, D), F32)], axis=0)
    c_all = _allgather8(c_pad, "gather_c")[::8]
    n_ada = w_ada.shape[2]
    b_cols = lax.dynamic_slice(b_ada, (0, chip * n_ada), (1, n_ada))
    mod_parts = _allgather8(_mod_part(c_all, w_ada[0], b_cols, "mod_part"), "gather_mod")
    mod_all = mod_parts.reshape(N_CHIPS, 2, 8, n_ada)[:, 0].transpose(1, 0, 2).reshape(8, N_CHIPS * n_ada)
    mod = lax.dynamic_slice(mod_all, (me, 0), (1, 6 * D))
    shift1, scale1, gate1, shift2, scale2, gate2 = [mod[:, i * D:(i + 1) * D] for i in range(6)]

    b_t = b_spatial[0].T
    h1 = _norm_mod(xs, norm1_g, scale1, shift1, "norm1")
    proj, = _mm("nn", h1, full["w_in"], [F32], "proj")
    on_gm = _gmlp_fwd(proj, v_norm_g, w_spatial[0], b_t, out_norm_g, n_g, "gmlp_fwd")
    o_sb, on_sb, l_sum = _sb_fwd(proj, out_norm_g, n_g, n_h, "sb_fwd")
    o_n = jnp.concatenate([on_gm, on_sb], axis=1)
    attn, = _mm("nn", o_n, full["w_out"], [F32], "attn_out")
    x1, h2 = _residual_norm_mod(xs, attn, gate1, norm2_g, scale2, shift2, "norm2")
    a_g, a_u, f_in = _gate_up(h2, full["w_gate"], full["w_up"], "gate_up")
    f, = _mm("nn", f_in, full["w_down"], [F32], "down", tm=1024)
    dx2, df, d_gate2, d_final_g, loss_part = _final_loss_bwd(x1, f, gate2, final_g.reshape(1, D), target, "final")
    loss = lax.psum(loss_part[0, 0], ("x", "y", "c"))

    geom_of = dict(zip(BIG, geoms))
    grad_out, delta, new_m, new_v = {}, {}, {}, {}

    def theirs_first(group, operands, collective_id, after=None):
        outs = []
        for n, (a_op, b_op) in zip(group, operands):
            outs.append(_dw_half(place, a_op, b_op if after is None else _then(after, b_op), geom_of[n], False, "d_" + n + "_theirs")[0])
            after = outs[-1]
        return outs, _send_to_sibling(outs, "swap_" + "_".join(group), collective_id)

    def chip_sums(group, operands, theirs, after):
        f32s, wires = [], []
        for n, (a_op, b_op), t in zip(group, operands, theirs):
            sf, sw = _dw_half(place, a_op, b_op, geom_of[n], True, "d_" + n + "_mine", add=_then(after, t))
            f32s.append(sf)
            wires.append(sw)
            after = sw
        return f32s, wires

    def scatter(group, sums, collective_id):
        return _scatter_chip_sums(sums, [geom_of[n] for n in group], "scatter_" + "_".join(group), collective_id)

    def reduce_halves(group, sums, others, after):
        return [_reduce_half(place, sf, _then(after, o), geom_of[n], "reduce_" + n) for n, sf, o in zip(group, sums, others)]

    def share(group, halves, collective_id):
        return _share_reduced_halves(halves, [geom_of[n] for n in group], "share_" + "_".join(group), collective_id)

    def adamw(group, reduced, after):
        for n, r in zip(group, reduced):
            go, d, mo, vo = _adamw(weights[n][0], _then(after, r), m_in[n][0], v_in[n][0], "adamw_" + n)
            grad_out[n], delta[n], new_m[n], new_v[n] = go[None], d[None], mo[None], vo[None]
        return d

    g_down = ("w_down",)
    g_ffn = ("w_gate", "w_up")
    g_out = ("w_out",)
    g_in = ("w_in",)

    gr_down, = _mm("tn", f_in, df, [F32], "d_w_down", tm=1408, tn=1024)
    th_down, = _swap_core_halves([gr_down], [geom_of["w_down"]], "swap_w_down", 6)
    d_ag, d_au = _mm("nt", df, full["w_down"], [MXU_DTYPE, MXU_DTYPE], "d_ffn_in", extras=(a_g, a_u),
                     epilogue=_swiglu_bwd_epilogue)
    sf_down, sw_down = [[t] for t in _chip_sum(place, gr_down, _then(d_ag, th_down), geom_of["w_down"], "chip_sum_w_down")]
    ot_down = scatter(g_down, sw_down, 7)
    sent, th_ffn = theirs_first(g_ffn, [(h2, d_ag), (h2, d_au)], 9, after=sw_down)
    dh2 = _mm_ktiled("nt", [(_then(sent, d_ag), full["w_gate"]), (d_au, full["w_up"])], "d_h2", tn=512)
    sf_ffn, sw_ffn = chip_sums(g_ffn, [(h2, d_ag), (h2, d_au)], th_ffn, after=dh2)
    ot_ffn = scatter(g_ffn, sw_ffn, 10)
    hv_down = reduce_halves(g_down, sf_down, ot_down, after=sw_ffn)
    rd_down = share(g_down, hv_down, 8)
    dx1, d_shift2, d_scale2, d_norm2_g, d_gate1, d_attn = _norm_mod_bwd(
        _then(hv_down, dh2), x1, dx2, norm2_g, scale2, "norm2_bwd", branch=attn, gate=gate1)
    gr_out, = _mm("tn", o_n, d_attn, [F32], "d_w_out")
    th_out, = _swap_core_halves([gr_out], [geom_of["w_out"]], "swap_w_out", 12)
    d_on, = _mm("nt", _then(gr_out, d_attn), full["w_out"], [F32], "d_o")
    dp_gm, d_w_spatial, d_b_t, d_v_norm_g, d_og_gm = _gmlp_bwd(proj, d_on, v_norm_g, w_spatial[0], b_t, out_norm_g, n_g, "gmlp_bwd")
    sf_out, sw_out = [[t] for t in _chip_sum(place, gr_out, _then(dp_gm, th_out), geom_of["w_out"], "chip_sum_w_out")]
    ot_out = scatter(g_out, sw_out, 13)
    dq, dk, dv, d_og_sb = _sb_bwd(proj, o_sb, l_sum, _then(sw_out, d_on), out_norm_g, n_g, n_h, "sb_bwd")
    hv_ffn = reduce_halves(g_ffn, sf_ffn, ot_ffn, after=dq)
    rd_ffn = share(g_ffn, hv_ffn, 11)
    dproj = jnp.concatenate([_then(hv_ffn, dp_gm), dq, dk, dv], axis=1)
    sent, th_in = theirs_first(g_in, [(h1, dproj)], 15)
    dh1, = _mm("nt", _then(sent, dproj), full["w_in"], [F32], "d_h1", tm=1024)
    sf_in, sw_in = chip_sums(g_in, [(h1, dproj)], th_in, after=dh1)
    ot_in = scatter(g_in, sw_in, 16)
    hv_out = reduce_halves(g_out, sf_out, ot_out, after=sw_in)
    rd_out = share(g_out, hv_out, 14)
    grad_x, d_shift1, d_scale1, d_norm1_g = _norm_mod_bwd(_then(hv_out, dh1), xs, dx1, norm1_g, scale1, "norm1_bwd")

    dmod = jnp.concatenate([d_shift1, d_scale1, d_gate1, d_shift2, d_scale2, d_gate2], axis=1)
    small_parts = dict(b_ada=dmod, norm1_g=d_norm1_g, v_norm_g=d_v_norm_g, w_spatial=d_w_spatial, b_spatial=d_b_t.T,
                       out_norm_g=jnp.concatenate([d_og_gm, d_og_sb], axis=1), norm2_g=d_norm2_g, final_g=d_final_g)
    slab = _then(grad_x, _pack(small_parts))
    rows = slab.shape[0]
    gathered = _allgather8_on_sequencer(slab, "gather_small", 18)
    done = adamw(g_down, rd_down, after=slab)
    done = adamw(("w_up",), rd_ffn[1:], after=done)
    hv_in = reduce_halves(g_in, sf_in, ot_in, after=done)
    rd_in = share(g_in, hv_in, 17)
    done = adamw(g_out, rd_out, after=hv_in)
    gathered = _then(done, gathered)
    small_shapes = {n: weights[n].shape for n in SMALL}
    small_sum = _sum_devices(gathered, 8, "sum_small")
    dmod_all = gathered.reshape(8, rows * LANE)[:, :6 * D]
    dmod_cols = lax.dynamic_slice(dmod_all, (0, chip * n_ada), (8, n_ada))
    g_ada, d, mo, vo = _adamw_ada(c_all, dmod_cols, w_ada[0], m_w_ada[0], v_w_ada[0], "adamw_w_ada")
    grad_out["w_ada"], delta["w_ada"], new_m["w_ada"], new_v["w_ada"] = g_ada[None], d[None], mo[None], vo[None]
    gs_small, d_small, mo, vo = _adamw(_pack({n: weights[n] for n in SMALL}), small_sum, _pack({n: m_in[n] for n in SMALL}),
                                       _pack({n: v_in[n] for n in SMALL}), "adamw_small")
    for dst, slab_out in ((grad_out, gs_small), (delta, d_small), (new_m, mo), (new_v, vo)):
        dst.update(_unpack(slab_out, small_shapes))
    done = adamw(("w_gate",), rd_ffn[:1], after=d)
    adamw(g_in, rd_in, after=done)

    return (loss, grad_x[None], *[grad_out[n] for n in ORDER], *[delta[n] for n in ORDER],
            *[new_m[n] for n in ORDER], *[new_v[n] for n in ORDER])
```

```python
import functools
import math

import jax
import jax.numpy as jnp
from jax import lax
from jax.experimental import pallas as pl
from jax.experimental.pallas import tpu as pltpu
from jax.experimental.pallas import tpu_sc as plsc

F32 = jnp.float32
MXU_DTYPE = jnp.bfloat16
WIRE_DTYPE = jnp.bfloat16
EPS = 1e-6
LANE = 128
V7X_VMEM_LIMIT = 56 * 1024 * 1024
MESH = pl.DeviceIdType.MESH
N_CHIPS = 4
FLIPS = (2, 1, 3)

ADAM_LR = 0.001
ADAM_B1 = 0.9
ADAM_B2 = 0.999
ADAM_EPS = 1e-08
ADAM_WD = 0.01
ADAM_STEP = 10


def _params(*semantics):
    return pltpu.CompilerParams(dimension_semantics=semantics or None, vmem_limit_bytes=V7X_VMEM_LIMIT)


def _tile(dim, pref, unit=LANE):
    best = None
    t = unit
    while t <= min(dim, pref):
        if dim % t == 0:
            best = t
        t += unit
    return best if best is not None else dim


def _then(first, second):
    return lax.optimization_barrier((first, second))[1]


def _sum0(v):
    return jnp.sum(v, axis=0, keepdims=True)


def _mean1(v):
    return jnp.mean(v, axis=-1, keepdims=True)


def _gelu(x):
    return 0.5 * x * (1.0 + lax.erf(x * (1.0 / math.sqrt(2.0))))


def _gelu_grad(x):
    cdf = 0.5 * (1.0 + lax.erf(x * (1.0 / math.sqrt(2.0))))
    return cdf + x * jnp.exp(-0.5 * x * x) * (1.0 / math.sqrt(2.0 * math.pi))


def _dot(a, b, dims):
    return lax.dot_general(a, b, (dims, ((), ())), preferred_element_type=F32)


NN = ((1,), (0,))
NT = ((1,), (1,))
TN = ((0,), (0,))


def _mm(kind, a, b, out_dtypes, name, tm=2048, tn=512, extras=(), epilogue=None):
    if kind == "nn":
        (M, K), N = a.shape, b.shape[1]
    elif kind == "nt":
        (M, K), N = a.shape, b.shape[0]
    else:
        (K, M), N = a.shape, b.shape[1]
    tm, tn = _tile(M, tm), _tile(N, tn)
    a_spec = pl.BlockSpec((K, tm), lambda i, j: (0, i)) if kind == "tn" else pl.BlockSpec((tm, K), lambda i, j: (i, 0))
    b_spec = pl.BlockSpec((tn, K), lambda i, j: (j, 0)) if kind == "nt" else pl.BlockSpec((K, tn), lambda i, j: (0, j))
    mn_spec = pl.BlockSpec((tm, tn), lambda i, j: (i, j))
    dims = {"nn": NN, "nt": NT, "tn": TN}[kind]
    n_extra = len(extras)

    n_chunks = 1 if epilogue is None or kind == "tn" else max(1, tm // 512)
    rows_per = tm // n_chunks

    def body(a_ref, b_ref, *rest):
        for r in range(n_chunks):
            rows = slice(r * rows_per, (r + 1) * rows_per)
            acc = _dot(a_ref[...] if n_chunks == 1 else a_ref[rows, :], b_ref[...], dims)
            res = (acc,) if epilogue is None else epilogue(acc, *[e[rows, :] for e in rest[:n_extra]])
            for o_ref, val in zip(rest[n_extra:], res):
                o_ref[rows, :] = val.astype(o_ref.dtype)

    outs = pl.pallas_call(
        body, name=name, grid=(M // tm, N // tn),
        in_specs=[a_spec, b_spec] + [mn_spec] * n_extra,
        out_specs=[mn_spec] * len(out_dtypes),
        out_shape=[jax.ShapeDtypeStruct((M, N), d) for d in out_dtypes],
        compiler_params=_params("parallel", "arbitrary"),
    )(a, b, *extras)
    return outs


def _mm_ktiled(kind, pairs, name, tm=2048, tn=1024, tk=1408):
    a0, b0 = pairs[0]
    M, K = a0.shape
    N = b0.shape[1] if kind == "nn" else b0.shape[0]
    tm, tn, tk = _tile(M, tm), _tile(N, tn), _tile(K, tk)
    a_spec = pl.BlockSpec((tm, tk), lambda i, j, k: (i, k))
    b_spec = pl.BlockSpec((tk, tn), lambda i, j, k: (k, j)) if kind == "nn" else pl.BlockSpec((tn, tk), lambda i, j, k: (j, k))
    dims = NN if kind == "nn" else NT
    n_pairs = len(pairs)

    def body(*refs):
        o_ref = refs[2 * n_pairs]
        acc = _dot(refs[0][...], refs[1][...], dims)
        for p in range(1, n_pairs):
            acc = acc + _dot(refs[2 * p][...], refs[2 * p + 1][...], dims)

        @pl.when(pl.program_id(2) == 0)
        def _():
            o_ref[...] = acc

        @pl.when(pl.program_id(2) != 0)
        def _():
            o_ref[...] += acc

    return pl.pallas_call(
        body, name=name, grid=(M // tm, N // tn, K // tk),
        in_specs=[a_spec, b_spec] * n_pairs,
        out_specs=pl.BlockSpec((tm, tn), lambda i, j, k: (i, j)),
        out_shape=jax.ShapeDtypeStruct((M, N), F32),
        compiler_params=_params("parallel", "parallel", "arbitrary"),
    )(*[x for pair in pairs for x in pair])


def _gate_up(h, wg, wu, name):
    (M, K), N = h.shape, wg.shape[1]
    tm, tn = _tile(M, 2048), _tile(N, 512)

    n_chunks = max(1, tm // 512)
    rows_per = tm // n_chunks

    def body(h_ref, wg_ref, wu_ref, ag_ref, au_ref, f_ref):
        for r in range(n_chunks):
            rows = slice(r * rows_per, (r + 1) * rows_per)
            hv = h_ref[rows, :]
            ag = _dot(hv, wg_ref[...], NN)
            au = _dot(hv, wu_ref[...], NN)
            ag_ref[rows, :] = ag.astype(ag_ref.dtype)
            au_ref[rows, :] = au.astype(au_ref.dtype)
            f_ref[rows, :] = (ag * jax.nn.sigmoid(ag) * au).astype(f_ref.dtype)

    w_spec = pl.BlockSpec((K, tn), lambda i, j: (0, j))
    mn_spec = pl.BlockSpec((tm, tn), lambda i, j: (i, j))
    return pl.pallas_call(
        body, name=name, grid=(M // tm, N // tn),
        in_specs=[pl.BlockSpec((tm, K), lambda i, j: (i, 0)), w_spec, w_spec],
        out_specs=[mn_spec] * 3,
        out_shape=[jax.ShapeDtypeStruct((M, N), MXU_DTYPE)] * 3,
        compiler_params=_params("parallel", "arbitrary"),
    )(h, wg, wu)


def _swiglu_bwd_epilogue(dfin, ag, au):
    ag, au = ag.astype(F32), au.astype(F32)
    sg = jax.nn.sigmoid(ag)
    d_au = dfin * (ag * sg)
    d_ag = dfin * au * (sg * (1.0 + ag * (1.0 - sg)))
    return d_ag, d_au


def _row_specs(ts, width):
    return pl.BlockSpec((ts, width), lambda i: (i, 0)), pl.BlockSpec((1, width), lambda i: (0, 0))


def _cast_into_full(place, shard, g, name):
    R, C = shard.shape
    tr = _tile(R, 256, 16)
    n_blk = R // tr
    out_map = (lambda i, p: (i, p[0])) if g.by_cols else (lambda i, p: (p[0] * n_blk + i, 0))

    def body(p_ref, a_ref, o_ref):
        o_ref[...] = a_ref[...].astype(o_ref.dtype)

    return pl.pallas_call(
        body, name=name,
        grid_spec=pltpu.PrefetchScalarGridSpec(
            num_scalar_prefetch=1, grid=(n_blk,),
            in_specs=[pl.BlockSpec((tr, C), lambda i, p: (i, 0))],
            out_specs=pl.BlockSpec((tr, C), out_map)),
        out_shape=jax.ShapeDtypeStruct(g.full, WIRE_DTYPE),
        compiler_params=_params("arbitrary"),
    )(place, shard)


def _norm_mod(x, g, scale, shift, name):
    S, D = x.shape
    ts = _tile(S, 256, 16)
    tile, vec = _row_specs(ts, D)

    def body(x_ref, g_ref, sc_ref, sh_ref, h_ref):
        xv = x_ref[...]
        r = lax.rsqrt(_mean1(xv * xv) + EPS)
        h_ref[...] = ((xv * r) * g_ref[...] * (1.0 + sc_ref[...]) + sh_ref[...]).astype(h_ref.dtype)

    return pl.pallas_call(body, name=name, grid=(S // ts,), in_specs=[tile, vec, vec, vec], out_specs=tile,
                          out_shape=jax.ShapeDtypeStruct((S, D), MXU_DTYPE), compiler_params=_params("parallel"))(x, g, scale, shift)


def _residual_norm_mod(x, attn, gate, g, scale, shift, name):
    S, D = x.shape
    ts = _tile(S, 256, 16)
    tile, vec = _row_specs(ts, D)

    def body(x_ref, a_ref, gate_ref, g_ref, sc_ref, sh_ref, x1_ref, h_ref):
        x1 = x_ref[...] + gate_ref[...] * a_ref[...]
        x1_ref[...] = x1
        r = lax.rsqrt(_mean1(x1 * x1) + EPS)
        h_ref[...] = ((x1 * r) * g_ref[...] * (1.0 + sc_ref[...]) + sh_ref[...]).astype(h_ref.dtype)

    return pl.pallas_call(body, name=name, grid=(S // ts,), in_specs=[tile, tile, vec, vec, vec, vec],
                          out_specs=[tile, tile],
                          out_shape=[jax.ShapeDtypeStruct((S, D), F32), jax.ShapeDtypeStruct((S, D), MXU_DTYPE)],
                          compiler_params=_params("parallel"))(x, attn, gate, g, scale, shift)


def _final_loss_bwd(x1, f, gate2, final_g, target, name):
    S, D = x1.shape
    ts = _tile(S, 256, 16)
    tile, vec = _row_specs(ts, D)
    loss_spec = pl.BlockSpec((1, LANE), lambda i: (0, 0))

    def body(x1_ref, f_ref, gate_ref, g_ref, t_ref, dx2_ref, df_ref, dgate_ref, dg_ref, loss_ref):
        @pl.when(pl.program_id(0) == 0)
        def _():
            dgate_ref[...] = jnp.zeros_like(dgate_ref)
            dg_ref[...] = jnp.zeros_like(dg_ref)
            loss_ref[...] = jnp.zeros_like(loss_ref)

        fv, gate, g = f_ref[...], gate_ref[...], g_ref[...]
        x2 = x1_ref[...] + gate * fv
        r = lax.rsqrt(_mean1(x2 * x2) + EPS)
        xn = x2 * r
        err = xn * g - t_ref[...]
        loss_ref[...] += jnp.broadcast_to(0.5 * _sum0(_mean1(err * err)), loss_ref.shape)
        dy = err * (1.0 / D)
        dg_ref[...] += _sum0(dy * xn)
        dxn = dy * g
        dx2 = r * (dxn - xn * _mean1(dxn * xn))
        dx2_ref[...] = dx2
        dgate_ref[...] += _sum0(dx2 * fv)
        df_ref[...] = (dx2 * gate).astype(df_ref.dtype)

    return pl.pallas_call(
        body, name=name, grid=(S // ts,), in_specs=[tile, tile, vec, vec, tile],
        out_specs=[tile, tile, vec, vec, loss_spec],
        out_shape=[jax.ShapeDtypeStruct((S, D), F32), jax.ShapeDtypeStruct((S, D), MXU_DTYPE),
                   jax.ShapeDtypeStruct((1, D), F32), jax.ShapeDtypeStruct((1, D), F32),
                   jax.ShapeDtypeStruct((1, LANE), F32)],
        compiler_params=_params("arbitrary"),
    )(x1, f, gate2, final_g, target)


def _norm_mod_bwd(dh, xin, dres, g, scale, name, branch=None, gate=None):
    S, D = xin.shape
    ts = _tile(S, 256, 16)
    tile, vec = _row_specs(ts, D)
    with_gate = branch is not None

    def body(*refs):
        if with_gate:
            dh_ref, x_ref, dres_ref, g_ref, sc_ref, br_ref, gate_ref, dx_ref, dshift_ref, dscale_ref, dg_ref, dgate_ref, dbr_ref = refs
            accs = (dshift_ref, dscale_ref, dg_ref, dgate_ref)
        else:
            dh_ref, x_ref, dres_ref, g_ref, sc_ref, dx_ref, dshift_ref, dscale_ref, dg_ref = refs
            accs = (dshift_ref, dscale_ref, dg_ref)

        @pl.when(pl.program_id(0) == 0)
        def _():
            for acc in accs:
                acc[...] = jnp.zeros_like(acc)

        dh_v, xv, g_v = dh_ref[...], x_ref[...], g_ref[...]
        one_sc = 1.0 + sc_ref[...]
        r = lax.rsqrt(_mean1(xv * xv) + EPS)
        xn = xv * r
        dshift_ref[...] += _sum0(dh_v)
        dscale_ref[...] += _sum0(dh_v * (xn * g_v))
        dg_ref[...] += _sum0(dh_v * one_sc * xn)
        dxn = dh_v * (g_v * one_sc)
        dx = dres_ref[...] + r * (dxn - xn * _mean1(dxn * xn))
        dx_ref[...] = dx
        if with_gate:
            dgate_ref[...] += _sum0(dx * br_ref[...])
            dbr_ref[...] = (dx * gate_ref[...]).astype(dbr_ref.dtype)

    ins = [dh, xin, dres, g, scale] + ([branch, gate] if with_gate else [])
    in_specs = [tile, tile, tile, vec, vec] + ([tile, vec] if with_gate else [])
    out_specs = [tile, vec, vec, vec] + ([vec, tile] if with_gate else [])
    out_shape = [jax.ShapeDtypeStruct((S, D), F32)] + [jax.ShapeDtypeStruct((1, D), F32)] * 3
    if with_gate:
        out_shape += [jax.ShapeDtypeStruct((1, D), F32), jax.ShapeDtypeStruct((S, D), MXU_DTYPE)]
    return pl.pallas_call(body, name=name, grid=(S // ts,), in_specs=in_specs, out_specs=out_specs,
                          out_shape=out_shape, compiler_params=_params("arbitrary"))(*ins)


def _causal_weights(ws_ref, wt_ref, n_g):
    row = lax.broadcasted_iota(jnp.int32, (LANE, LANE), 0)
    col = lax.broadcasted_iota(jnp.int32, (LANE, LANE), 1)
    for g in range(n_g):
        wt_ref[g] = jnp.where(col <= row, ws_ref[g], 0.0).astype(wt_ref.dtype)


def _group_layernorm(v):
    xc = v - _mean1(v)
    rstd = lax.rsqrt(_mean1(xc * xc) + EPS)
    return xc * rstd, rstd


def _gmlp_fwd(proj, v_gain, w_s, b_t, out_gain, n_g, name):
    S = proj.shape[0]
    GW = n_g * LANE

    def body(p_ref, vg_ref, ws_ref, bt_ref, og_ref, on_ref, wt_ref):
        @pl.when(pl.program_id(0) == 0)
        def _():
            _causal_weights(ws_ref, wt_ref, n_g)

        for g in range(n_g):
            cols = slice(g * LANE, (g + 1) * LANE)
            u = _gelu(p_ref[:, cols])
            v = _gelu(p_ref[:, GW + g * LANE:GW + (g + 1) * LANE])
            vhat, _ = _group_layernorm(v)
            vln = (vhat * vg_ref[:, cols]).astype(MXU_DTYPE)
            mixed = _dot(wt_ref[g], vln, NN) + bt_ref[:, g:g + 1]
            o = u * mixed
            r = lax.rsqrt(_mean1(o * o) + EPS)
            on_ref[:, cols] = (o * r * og_ref[:, cols]).astype(on_ref.dtype)

    return pl.pallas_call(
        body, name=name, grid=(S // LANE,),
        in_specs=[pl.BlockSpec((LANE, 2 * GW), lambda n: (n, 0)),
                  pl.BlockSpec((1, GW), lambda n: (0, 0)),
                  pl.BlockSpec((n_g, LANE, LANE), lambda n: (0, 0, 0)),
                  pl.BlockSpec((LANE, n_g), lambda n: (0, 0)),
                  pl.BlockSpec((1, GW), lambda n: (0, 0))],
        out_specs=pl.BlockSpec((LANE, GW), lambda n: (n, 0)),
        out_shape=jax.ShapeDtypeStruct((S, GW), MXU_DTYPE),
        scratch_shapes=[pltpu.VMEM((n_g, LANE, LANE), MXU_DTYPE)],
        compiler_params=_params("arbitrary"),
    )(proj, v_gain, w_s, b_t, out_gain)


def _gmlp_bwd(proj, d_on, v_gain, w_s, b_t, out_gain, n_g, name):
    S = proj.shape[0]
    GW = n_g * LANE

    def body(p_ref, dn_ref, vg_ref, ws_ref, bt_ref, og_ref, dp_ref, dws_ref, dbt_ref, dvg_ref, dog_ref, wt_ref):
        @pl.when(pl.program_id(0) == 0)
        def _():
            _causal_weights(ws_ref, wt_ref, n_g)
            dws_ref[...] = jnp.zeros_like(dws_ref)
            dbt_ref[...] = jnp.zeros_like(dbt_ref)
            dvg_ref[...] = jnp.zeros_like(dvg_ref)
            dog_ref[...] = jnp.zeros_like(dog_ref)

        row = lax.broadcasted_iota(jnp.int32, (LANE, LANE), 0)
        col = lax.broadcasted_iota(jnp.int32, (LANE, LANE), 1)
        for g in range(n_g):
            cols = slice(g * LANE, (g + 1) * LANE)
            vcols = slice(GW + g * LANE, GW + (g + 1) * LANE)
            pu, pv = p_ref[:, cols], p_ref[:, vcols]
            u, v = _gelu(pu), _gelu(pv)
            vhat, rstd = _group_layernorm(v)
            gain = vg_ref[:, cols]
            vln = (vhat * gain).astype(MXU_DTYPE)
            mixed = _dot(wt_ref[g], vln, NN) + bt_ref[:, g:g + 1]
            o = u * mixed
            r = lax.rsqrt(_mean1(o * o) + EPS)
            oh = o * r
            dn = dn_ref[:, cols]
            dog_ref[:, cols] += _sum0(dn * oh)
            dhn = dn * og_ref[:, cols]
            d_o = r * (dhn - oh * _mean1(dhn * oh))
            du = d_o * mixed
            dmix = d_o * u
            dbt_ref[:, g:g + 1] += jnp.sum(dmix, axis=1, keepdims=True)
            dmix_b = dmix.astype(MXU_DTYPE)
            dws_ref[g] += jnp.where(col <= row, _dot(dmix_b, vln, NT), 0.0)
            dvln = _dot(wt_ref[g], dmix_b, TN)
            dvg_ref[:, cols] += _sum0(dvln * vhat)
            dxh = dvln * gain
            dv = rstd * (dxh - _mean1(dxh) - vhat * _mean1(dxh * vhat))
            dp_ref[:, cols] = (du * _gelu_grad(pu)).astype(dp_ref.dtype)
            dp_ref[:, vcols] = (dv * _gelu_grad(pv)).astype(dp_ref.dtype)

    return pl.pallas_call(
        body, name=name, grid=(S // LANE,),
        in_specs=[pl.BlockSpec((LANE, 2 * GW), lambda n: (n, 0)),
                  pl.BlockSpec((LANE, GW), lambda n: (n, 0)),
                  pl.BlockSpec((1, GW), lambda n: (0, 0)),
                  pl.BlockSpec((n_g, LANE, LANE), lambda n: (0, 0, 0)),
                  pl.BlockSpec((LANE, n_g), lambda n: (0, 0)),
                  pl.BlockSpec((1, GW), lambda n: (0, 0))],
        out_specs=[pl.BlockSpec((LANE, 2 * GW), lambda n: (n, 0)),
                   pl.BlockSpec((n_g, LANE, LANE), lambda n: (0, 0, 0)),
                   pl.BlockSpec((LANE, n_g), lambda n: (0, 0)),
                   pl.BlockSpec((1, GW), lambda n: (0, 0)),
                   pl.BlockSpec((1, GW), lambda n: (0, 0))],
        out_shape=[jax.ShapeDtypeStruct((S, 2 * GW), MXU_DTYPE),
                   jax.ShapeDtypeStruct((n_g, LANE, LANE), F32),
                   jax.ShapeDtypeStruct((LANE, n_g), F32),
                   jax.ShapeDtypeStruct((1, GW), F32),
                   jax.ShapeDtypeStruct((1, GW), F32)],
        scratch_shapes=[pltpu.VMEM((n_g, LANE, LANE), MXU_DTYPE)],
        compiler_params=_params("arbitrary"),
    )(proj, d_on, v_gain, w_s, b_t, out_gain)


def _tri_sum(v, tri, exact=True):
    hi = v.astype(MXU_DTYPE)
    if not exact:
        return _dot(hi, tri, NN)
    lo = (v - hi.astype(F32)).astype(MXU_DTYPE)
    return _dot(hi, tri, NN) + _dot(lo, tri, NN)


def _log_sigmoids(z):
    sp = jnp.log(1.0 + jnp.exp(-jnp.abs(z)))
    return jnp.minimum(z, 0.0) - sp, jnp.minimum(-z, 0.0) - sp


def _rows(i, size):
    return pl.ds(pl.multiple_of(i * size, size), size)


SB_QUERY_TILE = 1024
SB_KEY_TILE = 256


def _sb_tiles(S):
    tq = _tile(S, SB_QUERY_TILE)
    tk = _tile(tq, SB_KEY_TILE)
    assert (tq // tk) % 2 == 0, "the key sweep takes two blocks a pass"
    return tq, tk, S // tq, tq // tk


def _triangle(n, keep):
    row = lax.broadcasted_iota(jnp.int32, (n, n), 0)
    col = lax.broadcasted_iota(jnp.int32, (n, n), 1)
    return jnp.where(keep(row, col), 1.0, 0.0).astype(MXU_DTYPE)


def _strictly_before(tq, tk, key_offset):
    row = lax.broadcasted_iota(jnp.int32, (tq, tk), 0)
    col = lax.broadcasted_iota(jnp.int32, (tq, tk), 1)
    return col + key_offset < row


def _sb_specs(S, n_g, n_h):
    base = 2 * n_g
    q_spec = pl.BlockSpec((S, LANE), lambda h: (0, base + h))
    k_spec = pl.BlockSpec((S, LANE), lambda h: (0, base + n_h + h))
    v_spec = pl.BlockSpec((S, LANE), lambda h: (0, base + 2 * n_h + h))
    gain_spec = pl.BlockSpec((1, LANE), lambda h: (0, n_g + h))
    head_spec = pl.BlockSpec((S, LANE), lambda h: (0, h))
    return q_spec, k_spec, v_spec, gain_spec, head_spec


def _sb_fwd(proj, out_gain, n_g, n_h, name):
    S = proj.shape[0]
    TQ, TK, NQ, KPQ = _sb_tiles(S)
    scale = LANE ** -0.5
    q_spec, k_spec, v_spec, gain_spec, head_spec = _sb_specs(S, n_g, n_h)

    def body(q_ref, k_ref, v_ref, og_ref, o_ref, on_ref, ls_ref, qb, kb, vb):
        qb[...] = q_ref[...].astype(MXU_DTYPE)
        kb[...] = k_ref[...].astype(MXU_DTYPE)
        vb[...] = v_ref[...].astype(MXU_DTYPE)
        after = _triangle(TK, lambda r, c: r > c)

        def block(qi, j, ctail, acc, key_offset):
            skip = key_offset or 0
            z = _dot(qi[skip:], kb[_rows(j, TK), :], NT) * scale
            lb, l1m = _log_sigmoids(z)
            if key_offset is not None:
                strict = _strictly_before(TQ - skip, TK, 0)
                l1m = jnp.where(strict, l1m, 0.0)
            a = jnp.exp(lb + ctail[skip:] + _tri_sum(l1m, after))
            if key_offset is not None:
                a = jnp.where(strict, a, 0.0)
            acc_new = acc[skip:] + _dot(a.astype(MXU_DTYPE), vb[_rows(j, TK), :], NN)
            ctail_new = ctail[skip:] + jnp.sum(l1m, axis=1, keepdims=True)
            if skip:
                ctail_new = jnp.concatenate([ctail[:skip], ctail_new], axis=0)
                acc_new = jnp.concatenate([acc[:skip], acc_new], axis=0)
            return ctail_new, acc_new

        def q_loop(i, carry):
            qi = qb[_rows(i, TQ), :]
            state = (jnp.zeros((TQ, 1), F32), jnp.zeros((TQ, LANE), F32))
            for d in reversed(range(KPQ)):
                state = block(qi, i * KPQ + d, state[0], state[1], d * TK)
            def pair(jj, st):
                st = block(qi, i * KPQ - 1 - 2 * jj, st[0], st[1], None)
                return block(qi, i * KPQ - 2 - 2 * jj, st[0], st[1], None)

            ctail, acc = lax.fori_loop(0, i * (KPQ // 2), pair, state)
            ls_ref[_rows(i, TQ), :] = jnp.broadcast_to(ctail, (TQ, LANE))
            o_ref[_rows(i, TQ), :] = acc
            r = lax.rsqrt(_mean1(acc * acc) + EPS)
            on_ref[_rows(i, TQ), :] = (acc * r * og_ref[...]).astype(on_ref.dtype)
            return carry

        lax.fori_loop(0, NQ, q_loop, 0)

    return pl.pallas_call(
        body, name=name, grid=(n_h,),
        in_specs=[q_spec, k_spec, v_spec, gain_spec],
        out_specs=[head_spec, head_spec, head_spec],
        out_shape=[jax.ShapeDtypeStruct((S, n_h * LANE), F32), jax.ShapeDtypeStruct((S, n_h * LANE), MXU_DTYPE),
                   jax.ShapeDtypeStruct((S, n_h * LANE), F32)],
        scratch_shapes=[pltpu.VMEM((S, LANE), MXU_DTYPE)] * 3,
        compiler_params=_params("parallel"),
    )(proj, proj, proj, out_gain)


def _sb_bwd(proj, o_sb, l_sum, d_on, out_gain, n_g, n_h, name):
    S = proj.shape[0]
    TQ, TK, NQ, KPQ = _sb_tiles(S)
    scale = LANE ** -0.5
    q_spec, k_spec, v_spec, gain_spec, head_spec = _sb_specs(S, n_g, n_h)
    dn_spec = pl.BlockSpec((S, LANE), lambda h: (0, n_g + h))
    dgain_spec = pl.BlockSpec((1, LANE), lambda h: (0, h))

    def body(q_ref, k_ref, v_ref, o_ref, ls_ref, dn_ref, og_ref, dq_ref, dk_ref, dv_ref, dog_ref,
             qb, kb, vb, dob, dk_acc, dv_acc):
        qb[...] = q_ref[...].astype(MXU_DTYPE)
        kb[...] = k_ref[...].astype(MXU_DTYPE)
        vb[...] = v_ref[...].astype(MXU_DTYPE)
        o, dn = o_ref[...], dn_ref[...]
        r = lax.rsqrt(_mean1(o * o) + EPS)
        oh = o * r
        dog_ref[...] = _sum0(dn * oh)
        dhn = dn * og_ref[...]
        dob[...] = (r * (dhn - oh * _mean1(dhn * oh))).astype(MXU_DTYPE)
        dk_acc[...] = jnp.zeros_like(dk_acc)
        dv_acc[...] = jnp.zeros_like(dv_acc)

        up_to = _triangle(TK, lambda r, c: r <= c)
        before = _triangle(TK, lambda r, c: r < c)

        def block(qi, doi, ltot, j, cl, cdl, dq, key_offset):
            skip = key_offset or 0
            q_in, do_in = qi[skip:], doi[skip:]
            kj, vj = kb[_rows(j, TK), :], vb[_rows(j, TK), :]
            z = _dot(q_in, kj, NT) * scale
            lb, l1m = _log_sigmoids(z)
            if key_offset is not None:
                strict = _strictly_before(TQ - skip, TK, 0)
                l1m = jnp.where(strict, l1m, 0.0)
            a = jnp.exp(lb + (ltot[skip:] - (cl[skip:] + _tri_sum(l1m, up_to))))
            if key_offset is not None:
                a = jnp.where(strict, a, 0.0)
            dl = _dot(do_in, vj, NT) * a
            d_l1m = cdl[skip:] + _tri_sum(dl, before, exact=False)
            beta = jnp.exp(lb)
            dz = dl * (1.0 - beta) - beta * d_l1m
            if key_offset is not None:
                dz = jnp.where(strict, dz, 0.0)
            dzs = (dz * scale).astype(MXU_DTYPE)
            dk_acc[_rows(j, TK), :] += _dot(dzs, q_in, TN)
            dv_acc[_rows(j, TK), :] += _dot(a.astype(MXU_DTYPE), do_in, TN)
            cl_new = cl[skip:] + jnp.sum(l1m, axis=1, keepdims=True)
            cdl_new = cdl[skip:] + jnp.sum(dl, axis=1, keepdims=True)
            dq_new = dq[skip:] + _dot(dzs, kj, NN)
            if skip:
                cl_new = jnp.concatenate([cl[:skip], cl_new], axis=0)
                cdl_new = jnp.concatenate([cdl[:skip], cdl_new], axis=0)
                dq_new = jnp.concatenate([dq[:skip], dq_new], axis=0)
            return cl_new, cdl_new, dq_new

        def q_loop(i, carry):
            qi, doi = qb[_rows(i, TQ), :], dob[_rows(i, TQ), :]
            ltot = ls_ref[_rows(i, TQ), :][:, :1]
            zero_col = jnp.zeros((TQ, 1), F32)
            def pair(jj, st):
                st = block(qi, doi, ltot, 2 * jj, st[0], st[1], st[2], None)
                return block(qi, doi, ltot, 2 * jj + 1, st[0], st[1], st[2], None)

            state = lax.fori_loop(0, i * (KPQ // 2), pair, (zero_col, zero_col, jnp.zeros((TQ, LANE), F32)))
            for d in range(KPQ):
                state = block(qi, doi, ltot, i * KPQ + d, state[0], state[1], state[2], d * TK)
            dq_ref[_rows(i, TQ), :] = state[2].astype(dq_ref.dtype)
            return carry

        lax.fori_loop(0, NQ, q_loop, 0)
        dk_ref[...] = dk_acc[...].astype(dk_ref.dtype)
        dv_ref[...] = dv_acc[...].astype(dv_ref.dtype)

    W = n_h * LANE
    return pl.pallas_call(
        body, name=name, grid=(n_h,),
        in_specs=[q_spec, k_spec, v_spec, head_spec, head_spec, dn_spec, gain_spec],
        out_specs=[head_spec, head_spec, head_spec, dgain_spec],
        out_shape=[jax.ShapeDtypeStruct((S, W), MXU_DTYPE)] * 3 + [jax.ShapeDtypeStruct((1, W), F32)],
        scratch_shapes=[pltpu.VMEM((S, LANE), MXU_DTYPE)] * 4 + [pltpu.VMEM((S, LANE), F32)] * 2,
        compiler_params=_params("parallel"),
    )(proj, proj, proj, o_sb, l_sum, d_on, out_gain)


def _mod_part(c_all, w_ada, b_ada_cols, name):
    B, K = c_all.shape
    N = w_ada.shape[1]
    tn = _tile(N, 512)

    def body(c_ref, w_ref, b_ref, o_ref):
        cv = c_ref[...]
        ca = (cv * jax.nn.sigmoid(cv)).astype(MXU_DTYPE)
        o_ref[...] = _dot(ca, w_ref[...].astype(MXU_DTYPE), NN) + b_ref[...]

    return pl.pallas_call(
        body, name=name, grid=(N // tn,),
        in_specs=[pl.BlockSpec((B, K), lambda j: (0, 0)), pl.BlockSpec((K, tn), lambda j: (0, j)),
                  pl.BlockSpec((1, tn), lambda j: (0, j))],
        out_specs=pl.BlockSpec((B, tn), lambda j: (0, j)),
        out_shape=jax.ShapeDtypeStruct((B, N), F32), compiler_params=_params("parallel"))(c_all, w_ada, b_ada_cols)


def _adamw_math(w, g, m, v):
    m = ADAM_B1 * m + (1.0 - ADAM_B1) * g
    v = ADAM_B2 * v + (1.0 - ADAM_B2) * (g * g)
    m_hat = m / (1.0 - ADAM_B1 ** ADAM_STEP)
    v_hat = v / (1.0 - ADAM_B2 ** ADAM_STEP)
    delta = -ADAM_LR * (m_hat / (jnp.sqrt(v_hat) + ADAM_EPS) + ADAM_WD * w)
    return delta, m, v


def _adamw(w, g, m, v, name):
    R, C = w.shape
    tr = _tile(R, max(8, (1 << 19) // C), 8)
    spec = pl.BlockSpec((tr, C), lambda i: (i, 0))

    def body(w_ref, g_ref, m_ref, v_ref, go_ref, d_ref, mo_ref, vo_ref):
        g = g_ref[...]
        go_ref[...] = g
        d_ref[...], mo_ref[...], vo_ref[...] = _adamw_math(w_ref[...], g, m_ref[...], v_ref[...])

    return pl.pallas_call(body, name=name, grid=(R // tr,), in_specs=[spec] * 4, out_specs=[spec] * 4,
                          out_shape=[jax.ShapeDtypeStruct((R, C), F32)] * 4, compiler_params=_params("parallel"))(w, g, m, v)


def _adamw_ada(c_all, dmod_cols, w, m, v, name):
    K, N = w.shape
    B = c_all.shape[0]
    tk, tn = _tile(K, 512), _tile(N, 1024)
    spec = pl.BlockSpec((tk, tn), lambda i, j: (i, j))

    def body(c_ref, dm_ref, w_ref, m_ref, v_ref, g_ref, d_ref, mo_ref, vo_ref):
        cv = c_ref[...]
        ca = (cv * jax.nn.sigmoid(cv)).astype(MXU_DTYPE)
        g = _dot(ca, dm_ref[...].astype(MXU_DTYPE), TN)
        g_ref[...] = g
        d_ref[...], mo_ref[...], vo_ref[...] = _adamw_math(w_ref[...], g, m_ref[...], v_ref[...])

    return pl.pallas_call(
        body, name=name, grid=(K // tk, N // tn),
        in_specs=[pl.BlockSpec((B, tk), lambda i, j: (0, i)), pl.BlockSpec((B, tn), lambda i, j: (0, j)), spec, spec, spec],
        out_specs=[spec] * 4, out_shape=[jax.ShapeDtypeStruct((K, N), F32)] * 4,
        compiler_params=_params("parallel", "parallel"))(c_all, dmod_cols, w, m, v)


def _sum_devices(gathered, n_dev, name):
    R = gathered.shape[0] // n_dev
    C = gathered.shape[1]
    tr = _tile(R, 512, 8)
    n_blk = R // tr

    def body(*refs):
        acc = refs[0][...]
        for r in refs[1:n_dev]:
            acc = acc + r[...]
        refs[n_dev][...] = acc

    in_specs = [pl.BlockSpec((tr, C), functools.partial(lambda i, d: (d * n_blk + i, 0), d=d)) for d in range(n_dev)]
    return pl.pallas_call(body, name=name, grid=(n_blk,), in_specs=in_specs,
                          out_specs=pl.BlockSpec((tr, C), lambda i: (i, 0)),
                          out_shape=jax.ShapeDtypeStruct((R, C), F32), compiler_params=_params("parallel"))(*([gathered] * n_dev))


def _place():
    x, y, c = lax.axis_index("x"), lax.axis_index("y"), lax.axis_index("c")
    return x, y, c


def _allgather8(blk, name):
    m_per, n = blk.shape

    def body(x_ref, out_ref, send_sems, recv_sems, local_sem):
        x, y, c = _place()
        me, sibling = (x, y, c), (x, y, 1 - c)
        chips = [(1 - x, y), (x, 1 - y), (1 - x, 1 - y)]

        def rows(px, py, pc):
            return out_ref.at[pl.ds((4 * px + 2 * py + pc) * m_per, m_per), :]

        def copy(k, block, to, src=None):
            return pltpu.make_async_remote_copy(
                src_ref=rows(*block) if src is None else src, dst_ref=rows(*block),
                send_sem=send_sems.at[k], recv_sem=recv_sems.at[k], device_id=to, device_id_type=MESH)

        mine = pltpu.make_async_copy(x_ref, rows(*me), local_sem)
        mine.start()
        first = [copy(0, me, sibling, src=x_ref)]
        first += [copy(1 + j, me, (*chip, c), src=x_ref) for j, chip in enumerate(chips)]
        for cp in first:
            cp.start()
        passed = [copy(4 + j, (*chip, c), sibling) for j, chip in enumerate(chips)]
        for j, chip in enumerate(chips):
            copy(1 + j, (*chip, c), me).wait_recv()
            passed[j].start()
        copy(0, sibling, me).wait_recv()
        for j, chip in enumerate(chips):
            copy(4 + j, (*chip, 1 - c), me).wait_recv()
        for cp in first + passed:
            cp.wait_send()
        mine.wait()

    return pl.pallas_call(
        body, name=name,
        out_shape=jax.ShapeDtypeStruct((8 * m_per, n), blk.dtype),
        in_specs=[pl.BlockSpec(memory_space=pltpu.VMEM)],
        out_specs=pl.BlockSpec(memory_space=pltpu.VMEM),
        scratch_shapes=[pltpu.SemaphoreType.DMA((7,)), pltpu.SemaphoreType.DMA((7,)), pltpu.SemaphoreType.DMA],
        compiler_params=pltpu.CompilerParams(vmem_limit_bytes=V7X_VMEM_LIMIT),
    )(blk)


class _Sharded:
    def __init__(self, shard_shape, by_cols):
        r, c = shard_shape
        self.by_cols = by_cols
        self.full = (r, N_CHIPS * c) if by_cols else (N_CHIPS * r, c)
        self.shard = (r, c)
        self.half_rows = r // 2
        self.half = (r // 2, c)

    def shard_of(self, ref, k):
        r, c = self.shard
        return ref.at[:, pl.ds(k * c, c)] if self.by_cols else ref.at[pl.ds(k * r, r), :]

    def half_of(self, ref, k, hc):
        r, c = self.shard
        h = self.half_rows
        if self.by_cols:
            return ref.at[pl.ds(hc * h, h), pl.ds(k * c, c)]
        return ref.at[pl.ds(k * r + hc * h, h), :]

    def chunk_of(self, ref, k, hc, ch, n):
        r, c = self.shard
        h = self.half_rows
        q = h // n
        if self.by_cols:
            return ref.at[pl.ds(hc * h + ch * q, q), pl.ds(k * c, c)]
        return ref.at[pl.ds(k * r + hc * h + ch * q, q), :]

    def half_of_shard(self, ref, hc):
        return ref.at[pl.ds(hc * self.half_rows, self.half_rows), :]

    def part_of_halves(self, ref, k):
        r, c = self.shard
        h = self.half_rows
        return ref.at[:, pl.ds(k * c, c)] if self.by_cols else ref.at[pl.ds(k * h, h), :]


def _on_each_place(x, y, c, fn, by_chip=True, by_core=True):
    q = 2 * x + y
    for k in range(N_CHIPS if by_chip else 1):
        for cc in range(2 if by_core else 1):
            cond = None
            if by_chip:
                cond = q == k
            if by_core:
                cond = (c == cc) if cond is None else jnp.logical_and(cond, c == cc)
            pl.when(cond)(functools.partial(fn, k, cc))


def _chip_id(k, c):
    return (k // 2, k % 2, c)


def _handshake(peers):
    barrier = pltpu.get_barrier_semaphore()
    for peer in peers:
        pl.semaphore_signal(barrier, inc=1, device_id=peer, device_id_type=MESH)
    pl.semaphore_wait(barrier, len(peers))


def _on_sequencer(body, inputs, out_structs, n_copies, peers_of, name, collective_id, return_inputs=False):
    in_refs = [jax.new_ref(a, memory_space=pltpu.MemorySpace.HBM) for a in inputs]
    out_refs = [jax.empty_ref(s, memory_space=pltpu.MemorySpace.HBM) for s in out_structs]

    @pl.kernel(mesh=plsc.ScalarSubcoreMesh(axis_name="sequencer", num_cores=1), name=name,
               scratch_types=(pltpu.SemaphoreType.DMA((n_copies,)), pltpu.SemaphoreType.DMA((n_copies,))),
               compiler_params=pltpu.CompilerParams(collective_id=collective_id))
    def launch(send_sems, recv_sems):
        x, y, c = _place()
        _handshake(peers_of(x, y, c))
        body(in_refs, out_refs, send_sems, recv_sems, x, y, c)

    launch()
    return [r[...] for r in (in_refs if return_inputs else out_refs)]


def _sibling(x, y, c):
    return [(x, y, 1 - c)]


def _same_core_of_other_chips(x, y, c):
    return [(1 - x, y, c), (x, 1 - y, c), (1 - x, 1 - y, c)]


GATHER_CHUNKS = 4
GATHER_COPIES = 6 * GATHER_CHUNKS


def _allgather8_on_sequencer(blk, name, collective_id):
    m_per, n = blk.shape
    x, y, c = _place()
    placed = lax.dynamic_update_slice(jnp.zeros((8 * m_per, n), blk.dtype), blk, ((4 * x + 2 * y + c) * m_per, 0))

    def body(refs, _, send_sems, recv_sems, x, y, c):
        out_ref, = refs

        def at_place(k, cc):
            def rows(kk, pc):
                return out_ref.at[pl.ds((2 * kk + pc) * m_per, m_per), :]

            def copy(slot, block, to):
                return pltpu.make_async_remote_copy(src_ref=rows(*block), dst_ref=rows(*block), send_sem=send_sems.at[slot],
                                                    recv_sem=recv_sems.at[slot], device_id=to, device_id_type=MESH)

            others = [k ^ flip for flip in FLIPS]
            sends = [copy(0, (k, cc), _chip_id(k, 1 - cc))] + [copy(1 + j, (k, cc), _chip_id(kk, cc)) for j, kk in enumerate(others)]
            for cp in sends:
                cp.start()
            for j, kk in enumerate(others):
                copy(1 + j, (kk, cc), _chip_id(k, cc)).wait_recv()
                cp = copy(4 + j, (kk, cc), _chip_id(k, 1 - cc))
                cp.start()
                sends.append(cp)
            copy(0, (k, 1 - cc), _chip_id(k, cc)).wait_recv()
            for j, kk in enumerate(others):
                copy(4 + j, (kk, 1 - cc), _chip_id(k, cc)).wait_recv()
            for cp in sends:
                cp.wait_send()

        _on_each_place(x, y, c, at_place)

    def peers(x, y, c):
        return _sibling(x, y, c) + _same_core_of_other_chips(x, y, c)

    return _on_sequencer(body, [placed], [], 7, peers, name, collective_id, return_inputs=True)[0]


def _gather_weights(fulls, geoms, name, collective_id):
    n_w = len(fulls)
    n_ch, n_relay = GATHER_CHUNKS, GATHER_CHUNKS // 2
    f_refs = [jax.new_ref(f, memory_space=pltpu.MemorySpace.HBM) for f in fulls]
    FLIP_X, FLIP_Y, FLIP_BOTH = FLIPS
    TO_X, TO_Y, RELAY_TO_Y, RELAY_TO_X, ON_X, ON_Y, ON_DIAG = 0, n_ch, 2 * n_ch, 2 * n_ch + n_relay, 3 * n_ch, 4 * n_ch, 5 * n_ch

    @pl.kernel(mesh=plsc.ScalarSubcoreMesh(axis_name="sequencer", num_cores=1), name=name,
               scratch_types=(pltpu.SemaphoreType.DMA((GATHER_COPIES * n_w,)), pltpu.SemaphoreType.DMA((GATHER_COPIES * n_w,))),
               compiler_params=pltpu.CompilerParams(collective_id=collective_id))
    def launch(send_sems, recv_sems):
        x, y, c = _place()
        _handshake([(x, y, 1 - c), (1 - x, y, c), (x, 1 - y, c)])

        def at_place(k, cc):
            kx, ky, kd = k ^ FLIP_X, k ^ FLIP_Y, k ^ FLIP_BOTH
            me, sibling = _chip_id(k, cc), _chip_id(k, 1 - cc)
            started = []

            def copy(i, slot, src, dst, to, start=True):
                cp = pltpu.make_async_remote_copy(src_ref=src, dst_ref=dst, send_sem=send_sems.at[GATHER_COPIES * i + slot],
                                                  recv_sem=recv_sems.at[GATHER_COPIES * i + slot], device_id=to, device_id_type=MESH)
                if start:
                    cp.start()
                    started.append(cp)
                return cp

            def pass_on(i, slot, ref, to):
                copy(i, slot, ref, ref, to)

            def landed(i, slot, ref):
                copy(i, slot, ref, ref, me, start=False).wait_recv()

            y_order = [(n_relay + s) % n_ch for s in range(n_ch)]
            for i, (g, f_ref) in enumerate(zip(geoms, f_refs)):
                for s in range(n_ch):
                    pass_on(i, TO_X + s, g.chunk_of(f_ref, k, cc, s, n_ch), _chip_id(kx, cc))
                    pass_on(i, TO_Y + y_order[s], g.chunk_of(f_ref, k, cc, y_order[s], n_ch), _chip_id(ky, cc))
            for i, (g, f_ref) in enumerate(zip(geoms, f_refs)):
                for s in range(n_ch):
                    from_x = g.chunk_of(f_ref, kx, cc, s, n_ch)
                    landed(i, TO_X + s, from_x)
                    if s < n_relay:
                        pass_on(i, RELAY_TO_Y + s, from_x, _chip_id(ky, cc))
                    pass_on(i, ON_X + s, from_x, sibling)
                    ch = y_order[s]
                    from_y = g.chunk_of(f_ref, ky, cc, ch, n_ch)
                    landed(i, TO_Y + ch, from_y)
                    if ch >= n_relay:
                        pass_on(i, RELAY_TO_X + ch - n_relay, from_y, _chip_id(kx, cc))
                    pass_on(i, ON_Y + ch, from_y, sibling)
                for r in range(n_relay):
                    via_y = g.chunk_of(f_ref, kd, cc, r, n_ch)
                    landed(i, RELAY_TO_Y + r, via_y)
                    pass_on(i, ON_DIAG + r, via_y, sibling)
                    via_x = g.chunk_of(f_ref, kd, cc, n_relay + r, n_ch)
                    landed(i, RELAY_TO_X + r, via_x)
                    pass_on(i, ON_DIAG + n_relay + r, via_x, sibling)
            for i, (g, f_ref) in enumerate(zip(geoms, f_refs)):
                for slot, kk in ((ON_X, kx), (ON_Y, ky), (ON_DIAG, kd)):
                    for ch in range(n_ch):
                        landed(i, slot + ch, g.chunk_of(f_ref, kk, 1 - cc, ch, n_ch))
            for cp in started:
                cp.wait_send()

        _on_each_place(x, y, c, at_place)

    launch()
    return [f_ref[...] for f_ref in f_refs]


def _swap_core_halves(grads, geoms, name, collective_id):
    n_cp = sum(1 if g.by_cols else N_CHIPS for g in geoms)

    def body(g_refs, t_refs, send_sems, recv_sems, x, y, c):

        def at_place(_, cc):
            def pairs(hc):
                out = []
                for g, g_ref, t_ref in zip(geoms, g_refs, t_refs):
                    if g.by_cols:
                        out.append((g_ref.at[pl.ds(hc * g.half_rows, g.half_rows), :], t_ref))
                    else:
                        out += [(g.half_of(g_ref, k, hc), g.part_of_halves(t_ref, k)) for k in range(N_CHIPS)]
                return out

            sends = [pltpu.make_async_remote_copy(src_ref=src, dst_ref=dst, send_sem=send_sems.at[n],
                                                  recv_sem=recv_sems.at[n], device_id=(x, y, 1 - cc), device_id_type=MESH)
                     for n, (src, dst) in enumerate(pairs(1 - cc))]
            for cp in sends:
                cp.start()
            for n, (src, dst) in enumerate(pairs(cc)):
                pltpu.make_async_remote_copy(src_ref=src, dst_ref=dst, send_sem=send_sems.at[n], recv_sem=recv_sems.at[n],
                                             device_id=(x, y, cc), device_id_type=MESH).wait_recv()
            for cp in sends:
                cp.wait_send()

        _on_each_place(x, y, c, at_place, by_chip=False)

    return _on_sequencer(body, grads, [jax.ShapeDtypeStruct((g.full[0] // 2, g.full[1]), F32) for g in geoms],
                         n_cp, _sibling, name, collective_id)


def _send_to_sibling(buffers, name, collective_id):
    def body(src_refs, dst_refs, send_sems, recv_sems, x, y, c):
        def copy(i):
            return pltpu.make_async_remote_copy(src_ref=src_refs[i], dst_ref=dst_refs[i], send_sem=send_sems.at[i],
                                                recv_sem=recv_sems.at[i], device_id=(x, y, 1 - c), device_id_type=MESH)

        for i in range(len(buffers)):
            copy(i).start()
        for i in range(len(buffers)):
            copy(i).wait()

    return _on_sequencer(body, buffers, [jax.ShapeDtypeStruct(t.shape, t.dtype) for t in buffers], len(buffers),
                         _sibling, name, collective_id)


def _scatter_chip_sums(sums, geoms, name, collective_id):
    def body(s_refs, r_refs, send_sems, recv_sems, x, y, c):

        def at_place(k, _):
            sends = []
            for i, (g, s_ref, r_ref) in enumerate(zip(geoms, s_refs, r_refs)):
                for j, flip in enumerate(FLIPS):
                    kk = k ^ flip
                    cp = pltpu.make_async_remote_copy(
                        src_ref=g.part_of_halves(s_ref, kk), dst_ref=r_ref.at[j], send_sem=send_sems.at[3 * i + j],
                        recv_sem=recv_sems.at[3 * i + j], device_id=(kk // 2, kk % 2, c), device_id_type=MESH)
                    cp.start()
                    sends.append(cp)
            for i, (g, s_ref, r_ref) in enumerate(zip(geoms, s_refs, r_refs)):
                for j in range(len(FLIPS)):
                    pltpu.make_async_remote_copy(
                        src_ref=g.part_of_halves(s_ref, k), dst_ref=r_ref.at[j], send_sem=send_sems.at[3 * i + j],
                        recv_sem=recv_sems.at[3 * i + j], device_id=(x, y, c), device_id_type=MESH).wait_recv()
            for cp in sends:
                cp.wait_send()

        _on_each_place(x, y, c, at_place, by_core=False)

    return _on_sequencer(body, sums, [jax.ShapeDtypeStruct((len(FLIPS),) + g.half, WIRE_DTYPE) for g in geoms],
                         len(FLIPS) * len(sums), _same_core_of_other_chips, name, collective_id)


def _share_reduced_halves(reduced, geoms, name, collective_id):
    def body(out_refs, _, send_sems, recv_sems, x, y, c):

        def at_place(_, cc):
            sends = []
            for i, (g, ref) in enumerate(zip(geoms, out_refs)):
                mine = g.half_of_shard(ref, cc)
                cp = pltpu.make_async_remote_copy(src_ref=mine, dst_ref=mine, send_sem=send_sems.at[i],
                                                  recv_sem=recv_sems.at[i], device_id=(x, y, 1 - cc), device_id_type=MESH)
                cp.start()
                sends.append(cp)
            for i, (g, ref) in enumerate(zip(geoms, out_refs)):
                theirs = g.half_of_shard(ref, 1 - cc)
                pltpu.make_async_remote_copy(src_ref=theirs, dst_ref=theirs, send_sem=send_sems.at[i],
                                             recv_sem=recv_sems.at[i], device_id=(x, y, cc), device_id_type=MESH).wait_recv()
            for cp in sends:
                cp.wait_send()

        _on_each_place(x, y, c, at_place, by_chip=False)

    return _on_sequencer(body, reduced, [], len(reduced), _sibling, name, collective_id, return_inputs=True)


def _chip_sum(place, grad, theirs, g, name):
    RH, C = theirs.shape
    h = g.half_rows
    tr = _tile(h, 256, 16)
    tc = _tile(C, 2048)
    per_half = h // tr

    if g.by_cols:
        grad_map = lambda i, j, p: (p[1] * per_half + i, j)
    else:
        grad_map = lambda i, j, p: ((i // per_half) * 2 * per_half + p[1] * per_half + i % per_half, j)

    def body(p_ref, a_ref, b_ref, f_ref, o_ref):
        total = a_ref[...] + b_ref[...]
        f_ref[...] = total
        o_ref[...] = total.astype(o_ref.dtype)

    return pl.pallas_call(
        body, name=name,
        grid_spec=pltpu.PrefetchScalarGridSpec(
            num_scalar_prefetch=1, grid=(RH // tr, C // tc),
            in_specs=[pl.BlockSpec((tr, tc), grad_map), pl.BlockSpec((tr, tc), lambda i, j, p: (i, j))],
            out_specs=[pl.BlockSpec((tr, tc), lambda i, j, p: (i, j))] * 2),
        out_shape=[jax.ShapeDtypeStruct((RH, C), F32), jax.ShapeDtypeStruct((RH, C), WIRE_DTYPE)],
        compiler_params=_params("parallel", "parallel"),
    )(place, grad, theirs)


def _dw_half(place, a, b, g, mine, name, add=None):
    K, R = a.shape
    C = b.shape[1]
    h = g.half_rows
    tm, tn = _tile(h, 1024, 16), _tile(C, 512)
    per_half = h // tm
    n_i = (R // 2) // tm

    def a_map(i, j, p):
        hc = p[1] if mine else 1 - p[1]
        if g.by_cols:
            return 0, hc * n_i + i
        return 0, (i // per_half) * 2 * per_half + hc * per_half + i % per_half

    mn_spec = pl.BlockSpec((tm, tn), lambda i, j, p: (i, j))

    def body(p_ref, a_ref, b_ref, *rest):
        acc = _dot(a_ref[...], b_ref[...], TN)
        if add is None:
            rest[0][...] = acc
        else:
            total = acc + rest[0][...]
            rest[1][...] = total
            rest[2][...] = total.astype(rest[2].dtype)

    out_shape = [jax.ShapeDtypeStruct((R // 2, C), F32)] + ([] if add is None else [jax.ShapeDtypeStruct((R // 2, C), WIRE_DTYPE)])
    return pl.pallas_call(
        body, name=name,
        grid_spec=pltpu.PrefetchScalarGridSpec(
            num_scalar_prefetch=1, grid=(n_i, C // tn),
            in_specs=[pl.BlockSpec((K, tm), a_map), pl.BlockSpec((K, tn), lambda i, j, p: (0, j))] + ([] if add is None else [mn_spec]),
            out_specs=[mn_spec] * len(out_shape)),
        out_shape=out_shape,
        compiler_params=_params("parallel", "arbitrary"),
    )(place, a, b, *([] if add is None else [add]))


def _reduce_half(place, sums, others, g, name):
    h, tc = g.half
    tr = _tile(h, 256, 16)
    per_half = h // tr
    sums_map = (lambda i, p: (i, p[0])) if g.by_cols else (lambda i, p: (p[0] * per_half + i, 0))

    def body(p_ref, s_ref, o0_ref, o1_ref, o2_ref, out_ref):
        acc = s_ref[...]
        for o_ref in (o0_ref, o1_ref, o2_ref):
            acc = acc + o_ref[...].astype(F32)
        out_ref[...] = acc

    other_specs = [pl.BlockSpec((None, tr, tc), functools.partial(lambda i, p, j: (j, i, 0), j=j)) for j in range(len(FLIPS))]
    return pl.pallas_call(
        body, name=name,
        grid_spec=pltpu.PrefetchScalarGridSpec(
            num_scalar_prefetch=1, grid=(per_half,),
            in_specs=[pl.BlockSpec((tr, tc), sums_map)] + other_specs,
            out_specs=pl.BlockSpec((tr, tc), lambda i, p: (p[1] * per_half + i, 0))),
        out_shape=jax.ShapeDtypeStruct(g.shard, F32),
        compiler_params=_params("arbitrary"),
    )(place, sums, others, others, others)


SMALL = ("b_ada", "norm1_g", "v_norm_g", "w_spatial", "b_spatial", "out_norm_g", "norm2_g", "final_g")
BIG = ("w_in", "w_out", "w_gate", "w_up", "w_down")
BY_COLS = {"w_in": True, "w_out": False, "w_gate": True, "w_up": True, "w_down": False}
ORDER = ("w_ada", "b_ada", "norm1_g", "w_in", "v_norm_g", "w_spatial", "b_spatial", "out_norm_g", "w_out",
         "norm2_g", "w_gate", "w_up", "w_down", "final_g")


def _pack(parts):
    return jnp.concatenate([parts[n].reshape(-1) for n in SMALL]).reshape(-1, LANE)


def _unpack(slab, shapes):
    flat = slab.reshape(-1)
    out, at = {}, 0
    for n in SMALL:
        size = math.prod(shapes[n])
        out[n] = flat[at:at + size].reshape(shapes[n])
        at += size
    return out


def kernel(x, c, w_ada, b_ada, norm1_g, w_in, v_norm_g, w_spatial, b_spatial, out_norm_g, w_out, norm2_g, w_gate, w_up, w_down, final_g, loss_target, m_w_ada, m_b_ada, m_norm1_g, m_w_in, m_v_norm_g, m_w_spatial, m_b_spatial, m_out_norm_g, m_w_out, m_norm2_g, m_w_gate, m_w_up, m_w_down, m_final_g, v_w_ada, v_b_ada, v_norm1_g, v_w_in, v_v_norm_g, v_w_spatial, v_b_spatial, v_out_norm_g, v_w_out, v_norm2_g, v_w_gate, v_w_up, v_w_down, v_final_g):
    weights = dict(w_ada=w_ada, b_ada=b_ada, norm1_g=norm1_g, w_in=w_in, v_norm_g=v_norm_g, w_spatial=w_spatial,
                   b_spatial=b_spatial, out_norm_g=out_norm_g, w_out=w_out, norm2_g=norm2_g, w_gate=w_gate, w_up=w_up,
                   w_down=w_down, final_g=final_g)
    m_in = dict(w_ada=m_w_ada, b_ada=m_b_ada, norm1_g=m_norm1_g, w_in=m_w_in, v_norm_g=m_v_norm_g, w_spatial=m_w_spatial,
                b_spatial=m_b_spatial, out_norm_g=m_out_norm_g, w_out=m_w_out, norm2_g=m_norm2_g, w_gate=m_w_gate,
                w_up=m_w_up, w_down=m_w_down, final_g=m_final_g)
    v_in = dict(w_ada=v_w_ada, b_ada=v_b_ada, norm1_g=v_norm1_g, w_in=v_w_in, v_norm_g=v_v_norm_g, w_spatial=v_w_spatial,
                b_spatial=v_b_spatial, out_norm_g=v_out_norm_g, w_out=v_w_out, norm2_g=v_norm2_g, w_gate=v_w_gate,
                w_up=v_w_up, w_down=v_w_down, final_g=v_final_g)

    S, D = x.shape[1], x.shape[2]
    n_g = v_norm_g.shape[-1] // LANE
    n_h = (D - n_g * LANE) // LANE
    GW = n_g * LANE
    xi, yi, ci = _place()
    chip = 2 * xi + yi
    me = 4 * xi + 2 * yi + ci
    place = jnp.stack([chip, ci]).astype(jnp.int32)

    xs, target = x[0], loss_target[0]
    geoms = [_Sharded(weights[n].shape[1:], BY_COLS[n]) for n in BIG]

    full = {}
    for i, group in enumerate((("w_in",), ("w_out",), ("w_gate", "w_up"), ("w_down",))):
        gg = [geoms[BIG.index(n)] for n in group]
        own = [_cast_into_full(place, weights[n][0], g, "cast_" + n) for n, g in zip(group, gg)]
        gathered = _gather_weights(own, gg, "gather_" + "_".join(group), 1 + i)
        full.update(zip(group, gathered))

    c_pad = jnp.concatenate([c, jnp.zeros((7, D), F32)], axis=0)
    c_all = _allgather8(c_pad, "gather_c")[::8]
    n_ada = w_ada.shape[2]
    b_cols = lax.dynamic_slice(b_ada, (0, chip * n_ada), (1, n_ada))
    mod_parts = _allgather8(_mod_part(c_all, w_ada[0], b_cols, "mod_part"), "gather_mod")
    mod_all = mod_parts.reshape(N_CHIPS, 2, 8, n_ada)[:, 0].transpose(1, 0, 2).reshape(8, N_CHIPS * n_ada)
    mod = lax.dynamic_slice(mod_all, (me, 0), (1, 6 * D))
    shift1, scale1, gate1, shift2, scale2, gate2 = [mod[:, i * D:(i + 1) * D] for i in range(6)]

    b_t = b_spatial[0].T
    h1 = _norm_mod(xs, norm1_g, scale1, shift1, "norm1")
    proj, = _mm("nn", h1, full["w_in"], [F32], "proj")
    on_gm = _gmlp_fwd(proj, v_norm_g, w_spatial[0], b_t, out_norm_g, n_g, "gmlp_fwd")
    o_sb, on_sb, l_sum = _sb_fwd(proj, out_norm_g, n_g, n_h, "sb_fwd")
    o_n = jnp.concatenate([on_gm, on_sb], axis=1)
    attn, = _mm("nn", o_n, full["w_out"], [F32], "attn_out")
    x1, h2 = _residual_norm_mod(xs, attn, gate1, norm2_g, scale2, shift2, "norm2")
    a_g, a_u, f_in = _gate_up(h2, full["w_gate"], full["w_up"], "gate_up")
    f, = _mm("nn", f_in, full["w_down"], [F32], "down", tm=1024)
    dx2, df, d_gate2, d_final_g, loss_part = _final_loss_bwd(x1, f, gate2, final_g.reshape(1, D), target, "final")
    loss = lax.psum(loss_part[0, 0], ("x", "y", "c"))

    geom_of = dict(zip(BIG, geoms))
    grad_out, delta, new_m, new_v = {}, {}, {}, {}

    def theirs_first(group, operands, collective_id, after=None):
        outs = []
        for n, (a_op, b_op) in zip(group, operands):
            outs.append(_dw_half(place, a_op, b_op if after is None else _then(after, b_op), geom_of[n], False, "d_" + n + "_theirs")[0])
            after = outs[-1]
        return outs, _send_to_sibling(outs, "swap_" + "_".join(group), collective_id)

    def chip_sums(group, operands, theirs, after):
        f32s, wires = [], []
        for n, (a_op, b_op), t in zip(group, operands, theirs):
            sf, sw = _dw_half(place, a_op, b_op, geom_of[n], True, "d_" + n + "_mine", add=_then(after, t))
            f32s.append(sf)
            wires.append(sw)
            after = sw
        return f32s, wires

    def scatter(group, sums, collective_id):
        return _scatter_chip_sums(sums, [geom_of[n] for n in group], "scatter_" + "_".join(group), collective_id)

    def reduce_halves(group, sums, others, after):
        return [_reduce_half(place, sf, _then(after, o), geom_of[n], "reduce_" + n) for n, sf, o in zip(group, sums, others)]

    def share(group, halves, collective_id):
        return _share_reduced_halves(halves, [geom_of[n] for n in group], "share_" + "_".join(group), collective_id)

    def adamw(group, reduced, after):
        for n, r in zip(group, reduced):
            go, d, mo, vo = _adamw(weights[n][0], _then(after, r), m_in[n][0], v_in[n][0], "adamw_" + n)
            grad_out[n], delta[n], new_m[n], new_v[n] = go[None], d[None], mo[None], vo[None]
        return d

    g_down = ("w_down",)
    g_ffn = ("w_gate", "w_up")
    g_out = ("w_out",)
    g_in = ("w_in",)

    gr_down, = _mm("tn", f_in, df, [F32], "d_w_down", tm=1408, tn=1024)
    th_down, = _swap_core_halves([gr_down], [geom_of["w_down"]], "swap_w_down", 6)
    d_ag, d_au = _mm("nt", df, full["w_down"], [MXU_DTYPE, MXU_DTYPE], "d_ffn_in", extras=(a_g, a_u),
                     epilogue=_swiglu_bwd_epilogue)
    sf_down, sw_down = [[t] for t in _chip_sum(place, gr_down, _then(d_ag, th_down), geom_of["w_down"], "chip_sum_w_down")]
    ot_down = scatter(g_down, sw_down, 7)
    sent, th_ffn = theirs_first(g_ffn, [(h2, d_ag), (h2, d_au)], 9, after=sw_down)
    dh2 = _mm_ktiled("nt", [(_then(sent, d_ag), full["w_gate"]), (d_au, full["w_up"])], "d_h2", tn=512)
    sf_ffn, sw_ffn = chip_sums(g_ffn, [(h2, d_ag), (h2, d_au)], th_ffn, after=dh2)
    ot_ffn = scatter(g_ffn, sw_ffn, 10)
    hv_down = reduce_halves(g_down, sf_down, ot_down, after=sw_ffn)
    rd_down = share(g_down, hv_down, 8)
    dx1, d_shift2, d_scale2, d_norm2_g, d_gate1, d_attn = _norm_mod_bwd(
        _then(hv_down, dh2), x1, dx2, norm2_g, scale2, "norm2_bwd", branch=attn, gate=gate1)
    gr_out, = _mm("tn", o_n, d_attn, [F32], "d_w_out")
    th_out, = _swap_core_halves([gr_out], [geom_of["w_out"]], "swap_w_out", 12)
    d_on, = _mm("nt", _then(gr_out, d_attn), full["w_out"], [F32], "d_o")
    dp_gm, d_w_spatial, d_b_t, d_v_norm_g, d_og_gm = _gmlp_bwd(proj, d_on, v_norm_g, w_spatial[0], b_t, out_norm_g, n_g, "gmlp_bwd")
    sf_out, sw_out = [[t] for t in _chip_sum(place, gr_out, _then(dp_gm, th_out), geom_of["w_out"], "chip_sum_w_out")]
    ot_out = scatter(g_out, sw_out, 13)
    dq, dk, dv, d_og_sb = _sb_bwd(proj, o_sb, l_sum, _then(sw_out, d_on), out_norm_g, n_g, n_h, "sb_bwd")
    hv_ffn = reduce_halves(g_ffn, sf_ffn, ot_ffn, after=dq)
    rd_ffn = share(g_ffn, hv_ffn, 11)
    dproj = jnp.concatenate([_then(hv_ffn, dp_gm), dq, dk, dv], axis=1)
    sent, th_in = theirs_first(g_in, [(h1, dproj)], 15)
    dh1, = _mm("nt", _then(sent, dproj), full["w_in"], [F32], "d_h1", tm=1024)
    hv_out = reduce_halves(g_out, sf_out, ot_out, after=dh1)
    rd_out = share(g_out, hv_out, 14)
    grad_x, d_shift1, d_scale1, d_norm1_g = _norm_mod_bwd(_then(hv_out, dh1), xs, dx1, norm1_g, scale1, "norm1_bwd")

    dmod = jnp.concatenate([d_shift1, d_scale1, d_gate1, d_shift2, d_scale2, d_gate2], axis=1)
    small_parts = dict(b_ada=dmod, norm1_g=d_norm1_g, v_norm_g=d_v_norm_g, w_spatial=d_w_spatial, b_spatial=d_b_t.T,
                       out_norm_g=jnp.concatenate([d_og_gm, d_og_sb], axis=1), norm2_g=d_norm2_g, final_g=d_final_g)
    slab = _then(grad_x, _pack(small_parts))
    rows = slab.shape[0]
    gathered = _allgather8_on_sequencer(slab, "gather_small", 18)
    sf_in, sw_in = chip_sums(g_in, [(h1, dproj)], th_in, after=slab)
    ot_in = scatter(g_in, sw_in, 16)
    done = adamw(g_down, rd_down, after=sw_in)
    done = adamw(g_ffn, rd_ffn, after=done)
    done = adamw(g_out, rd_out, after=done)
    gathered = _then(done, gathered)
    small_shapes = {n: weights[n].shape for n in SMALL}
    small_sum = _sum_devices(gathered, 8, "sum_small")
    dmod_all = gathered.reshape(8, rows * LANE)[:, :6 * D]
    dmod_cols = lax.dynamic_slice(dmod_all, (0, chip * n_ada), (8, n_ada))
    g_ada, d, mo, vo = _adamw_ada(c_all, dmod_cols, w_ada[0], m_w_ada[0], v_w_ada[0], "adamw_w_ada")
    grad_out["w_ada"], delta["w_ada"], new_m["w_ada"], new_v["w_ada"] = g_ada[None], d[None], mo[None], vo[None]
    gs_small, d_small, mo, vo = _adamw(_pack({n: weights[n] for n in SMALL}), small_sum, _pack({n: m_in[n] for n in SMALL}),
                                       _pack({n: v_in[n] for n in SMALL}), "adamw_small")
    for dst, slab_out in ((grad_out, gs_small), (delta, d_small), (new_m, mo), (new_v, vo)):
        dst.update(_unpack(slab_out, small_shapes))
    hv_in = reduce_halves(g_in, sf_in, ot_in, after=d)
    adamw(g_in, share(g_in, hv_in, 17), after=d)

    return (loss, grad_x[None], *[grad_out[n] for n in ORDER], *[delta[n] for n in ORDER],
            *[new_m[n] for n in ORDER], *[new_v[n] for n in ORDER])
```

```python
import functools
import math

import jax
import jax.numpy as jnp
from jax import lax
from jax.experimental import pallas as pl
from jax.experimental.pallas import tpu as pltpu
from jax.experimental.pallas import tpu_sc as plsc

F32 = jnp.float32
MXU_DTYPE = jnp.bfloat16
WIRE_DTYPE = jnp.bfloat16
EPS = 1e-6
LANE = 128
V7X_VMEM_LIMIT = 56 * 1024 * 1024
MESH = pl.DeviceIdType.MESH
N_CHIPS = 4
FLIPS = (2, 1, 3)

ADAM_LR = 0.001
ADAM_B1 = 0.9
ADAM_B2 = 0.999
ADAM_EPS = 1e-08
ADAM_WD = 0.01
ADAM_STEP = 10


def _params(*semantics):
    return pltpu.CompilerParams(dimension_semantics=semantics or None, vmem_limit_bytes=V7X_VMEM_LIMIT)


def _tile(dim, pref, unit=LANE):
    best = None
    t = unit
    while t <= min(dim, pref):
        if dim % t == 0:
            best = t
        t += unit
    return best if best is not None else dim


def _then(first, second):
    return lax.optimization_barrier((first, second))[1]


def _sum0(v):
    return jnp.sum(v, axis=0, keepdims=True)


def _mean1(v):
    return jnp.mean(v, axis=-1, keepdims=True)


def _gelu(x):
    return 0.5 * x * (1.0 + lax.erf(x * (1.0 / math.sqrt(2.0))))


def _gelu_grad(x):
    cdf = 0.5 * (1.0 + lax.erf(x * (1.0 / math.sqrt(2.0))))
    return cdf + x * jnp.exp(-0.5 * x * x) * (1.0 / math.sqrt(2.0 * math.pi))


def _dot(a, b, dims):
    return lax.dot_general(a, b, (dims, ((), ())), preferred_element_type=F32)


NN = ((1,), (0,))
NT = ((1,), (1,))
TN = ((0,), (0,))


def _mm(kind, a, b, out_dtypes, name, tm=2048, tn=512, extras=(), epilogue=None):
    if kind == "nn":
        (M, K), N = a.shape, b.shape[1]
    elif kind == "nt":
        (M, K), N = a.shape, b.shape[0]
    else:
        (K, M), N = a.shape, b.shape[1]
    tm, tn = _tile(M, tm), _tile(N, tn)
    a_spec = pl.BlockSpec((K, tm), lambda i, j: (0, i)) if kind == "tn" else pl.BlockSpec((tm, K), lambda i, j: (i, 0))
    b_spec = pl.BlockSpec((tn, K), lambda i, j: (j, 0)) if kind == "nt" else pl.BlockSpec((K, tn), lambda i, j: (0, j))
    mn_spec = pl.BlockSpec((tm, tn), lambda i, j: (i, j))
    dims = {"nn": NN, "nt": NT, "tn": TN}[kind]
    n_extra = len(extras)

    n_chunks = 1 if epilogue is None or kind == "tn" else max(1, tm // 512)
    rows_per = tm // n_chunks

    def body(a_ref, b_ref, *rest):
        for r in range(n_chunks):
            rows = slice(r * rows_per, (r + 1) * rows_per)
            acc = _dot(a_ref[...] if n_chunks == 1 else a_ref[rows, :], b_ref[...], dims)
            res = (acc,) if epilogue is None else epilogue(acc, *[e[rows, :] for e in rest[:n_extra]])
            for o_ref, val in zip(rest[n_extra:], res):
                o_ref[rows, :] = val.astype(o_ref.dtype)

    outs = pl.pallas_call(
        body, name=name, grid=(M // tm, N // tn),
        in_specs=[a_spec, b_spec] + [mn_spec] * n_extra,
        out_specs=[mn_spec] * len(out_dtypes),
        out_shape=[jax.ShapeDtypeStruct((M, N), d) for d in out_dtypes],
        compiler_params=_params("parallel", "arbitrary"),
    )(a, b, *extras)
    return outs


def _mm_ktiled(kind, pairs, name, tm=2048, tn=1024, tk=1408):
    a0, b0 = pairs[0]
    M, K = a0.shape
    N = b0.shape[1] if kind == "nn" else b0.shape[0]
    tm, tn, tk = _tile(M, tm), _tile(N, tn), _tile(K, tk)
    a_spec = pl.BlockSpec((tm, tk), lambda i, j, k: (i, k))
    b_spec = pl.BlockSpec((tk, tn), lambda i, j, k: (k, j)) if kind == "nn" else pl.BlockSpec((tn, tk), lambda i, j, k: (j, k))
    dims = NN if kind == "nn" else NT
    n_pairs = len(pairs)

    def body(*refs):
        o_ref = refs[2 * n_pairs]
        acc = _dot(refs[0][...], refs[1][...], dims)
        for p in range(1, n_pairs):
            acc = acc + _dot(refs[2 * p][...], refs[2 * p + 1][...], dims)

        @pl.when(pl.program_id(2) == 0)
        def _():
            o_ref[...] = acc

        @pl.when(pl.program_id(2) != 0)
        def _():
            o_ref[...] += acc

    return pl.pallas_call(
        body, name=name, grid=(M // tm, N // tn, K // tk),
        in_specs=[a_spec, b_spec] * n_pairs,
        out_specs=pl.BlockSpec((tm, tn), lambda i, j, k: (i, j)),
        out_shape=jax.ShapeDtypeStruct((M, N), F32),
        compiler_params=_params("parallel", "parallel", "arbitrary"),
    )(*[x for pair in pairs for x in pair])


def _gate_up(h, wg, wu, name):
    (M, K), N = h.shape, wg.shape[1]
    tm, tn = _tile(M, 2048), _tile(N, 512)

    n_chunks = max(1, tm // 512)
    rows_per = tm // n_chunks

    def body(h_ref, wg_ref, wu_ref, ag_ref, au_ref, f_ref):
        for r in range(n_chunks):
            rows = slice(r * rows_per, (r + 1) * rows_per)
            hv = h_ref[rows, :]
            ag = _dot(hv, wg_ref[...], NN)
            au = _dot(hv, wu_ref[...], NN)
            ag_ref[rows, :] = ag.astype(ag_ref.dtype)
            au_ref[rows, :] = au.astype(au_ref.dtype)
            f_ref[rows, :] = (ag * jax.nn.sigmoid(ag) * au).astype(f_ref.dtype)

    w_spec = pl.BlockSpec((K, tn), lambda i, j: (0, j))
    mn_spec = pl.BlockSpec((tm, tn), lambda i, j: (i, j))
    return pl.pallas_call(
        body, name=name, grid=(M // tm, N // tn),
        in_specs=[pl.BlockSpec((tm, K), lambda i, j: (i, 0)), w_spec, w_spec],
        out_specs=[mn_spec] * 3,
        out_shape=[jax.ShapeDtypeStruct((M, N), MXU_DTYPE)] * 3,
        compiler_params=_params("parallel", "arbitrary"),
    )(h, wg, wu)


def _swiglu_bwd_epilogue(dfin, ag, au):
    ag, au = ag.astype(F32), au.astype(F32)
    sg = jax.nn.sigmoid(ag)
    d_au = dfin * (ag * sg)
    d_ag = dfin * au * (sg * (1.0 + ag * (1.0 - sg)))
    return d_ag, d_au


def _row_specs(ts, width):
    return pl.BlockSpec((ts, width), lambda i: (i, 0)), pl.BlockSpec((1, width), lambda i: (0, 0))


def _cast_into_full(place, shard, g, name):
    R, C = shard.shape
    tr = _tile(R, 256, 16)
    n_blk = R // tr
    out_map = (lambda i, p: (i, p[0])) if g.by_cols else (lambda i, p: (p[0] * n_blk + i, 0))

    def body(p_ref, a_ref, o_ref):
        o_ref[...] = a_ref[...].astype(o_ref.dtype)

    return pl.pallas_call(
        body, name=name,
        grid_spec=pltpu.PrefetchScalarGridSpec(
            num_scalar_prefetch=1, grid=(n_blk,),
            in_specs=[pl.BlockSpec((tr, C), lambda i, p: (i, 0))],
            out_specs=pl.BlockSpec((tr, C), out_map)),
        out_shape=jax.ShapeDtypeStruct(g.full, WIRE_DTYPE),
        compiler_params=_params("arbitrary"),
    )(place, shard)


def _norm_mod(x, g, scale, shift, name):
    S, D = x.shape
    ts = _tile(S, 256, 16)
    tile, vec = _row_specs(ts, D)

    def body(x_ref, g_ref, sc_ref, sh_ref, h_ref):
        xv = x_ref[...]
        r = lax.rsqrt(_mean1(xv * xv) + EPS)
        h_ref[...] = ((xv * r) * g_ref[...] * (1.0 + sc_ref[...]) + sh_ref[...]).astype(h_ref.dtype)

    return pl.pallas_call(body, name=name, grid=(S // ts,), in_specs=[tile, vec, vec, vec], out_specs=tile,
                          out_shape=jax.ShapeDtypeStruct((S, D), MXU_DTYPE), compiler_params=_params("parallel"))(x, g, scale, shift)


def _residual_norm_mod(x, attn, gate, g, scale, shift, name):
    S, D = x.shape
    ts = _tile(S, 256, 16)
    tile, vec = _row_specs(ts, D)

    def body(x_ref, a_ref, gate_ref, g_ref, sc_ref, sh_ref, x1_ref, h_ref):
        x1 = x_ref[...] + gate_ref[...] * a_ref[...]
        x1_ref[...] = x1
        r = lax.rsqrt(_mean1(x1 * x1) + EPS)
        h_ref[...] = ((x1 * r) * g_ref[...] * (1.0 + sc_ref[...]) + sh_ref[...]).astype(h_ref.dtype)

    return pl.pallas_call(body, name=name, grid=(S // ts,), in_specs=[tile, tile, vec, vec, vec, vec],
                          out_specs=[tile, tile],
                          out_shape=[jax.ShapeDtypeStruct((S, D), F32), jax.ShapeDtypeStruct((S, D), MXU_DTYPE)],
                          compiler_params=_params("parallel"))(x, attn, gate, g, scale, shift)


def _final_loss_bwd(x1, f, gate2, final_g, target, name):
    S, D = x1.shape
    ts = _tile(S, 256, 16)
    tile, vec = _row_specs(ts, D)
    loss_spec = pl.BlockSpec((1, LANE), lambda i: (0, 0))

    def body(x1_ref, f_ref, gate_ref, g_ref, t_ref, dx2_ref, df_ref, dgate_ref, dg_ref, loss_ref):
        @pl.when(pl.program_id(0) == 0)
        def _():
            dgate_ref[...] = jnp.zeros_like(dgate_ref)
            dg_ref[...] = jnp.zeros_like(dg_ref)
            loss_ref[...] = jnp.zeros_like(loss_ref)

        fv, gate, g = f_ref[...], gate_ref[...], g_ref[...]
        x2 = x1_ref[...] + gate * fv
        r = lax.rsqrt(_mean1(x2 * x2) + EPS)
        xn = x2 * r
        err = xn * g - t_ref[...]
        loss_ref[...] += jnp.broadcast_to(0.5 * _sum0(_mean1(err * err)), loss_ref.shape)
        dy = err * (1.0 / D)
        dg_ref[...] += _sum0(dy * xn)
        dxn = dy * g
        dx2 = r * (dxn - xn * _mean1(dxn * xn))
        dx2_ref[...] = dx2
        dgate_ref[...] += _sum0(dx2 * fv)
        df_ref[...] = (dx2 * gate).astype(df_ref.dtype)

    return pl.pallas_call(
        body, name=name, grid=(S // ts,), in_specs=[tile, tile, vec, vec, tile],
        out_specs=[tile, tile, vec, vec, loss_spec],
        out_shape=[jax.ShapeDtypeStruct((S, D), F32), jax.ShapeDtypeStruct((S, D), MXU_DTYPE),
                   jax.ShapeDtypeStruct((1, D), F32), jax.ShapeDtypeStruct((1, D), F32),
                   jax.ShapeDtypeStruct((1, LANE), F32)],
        compiler_params=_params("arbitrary"),
    )(x1, f, gate2, final_g, target)


def _norm_mod_bwd(dh, xin, dres, g, scale, name, branch=None, gate=None):
    S, D = xin.shape
    ts = _tile(S, 256, 16)
    tile, vec = _row_specs(ts, D)
    with_gate = branch is not None

    def body(*refs):
        if with_gate:
            dh_ref, x_ref, dres_ref, g_ref, sc_ref, br_ref, gate_ref, dx_ref, dshift_ref, dscale_ref, dg_ref, dgate_ref, dbr_ref = refs
            accs = (dshift_ref, dscale_ref, dg_ref, dgate_ref)
        else:
            dh_ref, x_ref, dres_ref, g_ref, sc_ref, dx_ref, dshift_ref, dscale_ref, dg_ref = refs
            accs = (dshift_ref, dscale_ref, dg_ref)

        @pl.when(pl.program_id(0) == 0)
        def _():
            for acc in accs:
                acc[...] = jnp.zeros_like(acc)

        dh_v, xv, g_v = dh_ref[...], x_ref[...], g_ref[...]
        one_sc = 1.0 + sc_ref[...]
        r = lax.rsqrt(_mean1(xv * xv) + EPS)
        xn = xv * r
        dshift_ref[...] += _sum0(dh_v)
        dscale_ref[...] += _sum0(dh_v * (xn * g_v))
        dg_ref[...] += _sum0(dh_v * one_sc * xn)
        dxn = dh_v * (g_v * one_sc)
        dx = dres_ref[...] + r * (dxn - xn * _mean1(dxn * xn))
        dx_ref[...] = dx
        if with_gate:
            dgate_ref[...] += _sum0(dx * br_ref[...])
            dbr_ref[...] = (dx * gate_ref[...]).astype(dbr_ref.dtype)

    ins = [dh, xin, dres, g, scale] + ([branch, gate] if with_gate else [])
    in_specs = [tile, tile, tile, vec, vec] + ([tile, vec] if with_gate else [])
    out_specs = [tile, vec, vec, vec] + ([vec, tile] if with_gate else [])
    out_shape = [jax.ShapeDtypeStruct((S, D), F32)] + [jax.ShapeDtypeStruct((1, D), F32)] * 3
    if with_gate:
        out_shape += [jax.ShapeDtypeStruct((1, D), F32), jax.ShapeDtypeStruct((S, D), MXU_DTYPE)]
    return pl.pallas_call(body, name=name, grid=(S // ts,), in_specs=in_specs, out_specs=out_specs,
                          out_shape=out_shape, compiler_params=_params("arbitrary"))(*ins)


def _causal_weights(ws_ref, wt_ref, n_g):
    row = lax.broadcasted_iota(jnp.int32, (LANE, LANE), 0)
    col = lax.broadcasted_iota(jnp.int32, (LANE, LANE), 1)
    for g in range(n_g):
        wt_ref[g] = jnp.where(col <= row, ws_ref[g], 0.0).astype(wt_ref.dtype)


def _group_layernorm(v):
    xc = v - _mean1(v)
    rstd = lax.rsqrt(_mean1(xc * xc) + EPS)
    return xc * rstd, rstd


def _gmlp_fwd(proj, v_gain, w_s, b_t, out_gain, n_g, name):
    S = proj.shape[0]
    GW = n_g * LANE

    def body(p_ref, vg_ref, ws_ref, bt_ref, og_ref, on_ref, wt_ref):
        @pl.when(pl.program_id(0) == 0)
        def _():
            _causal_weights(ws_ref, wt_ref, n_g)

        for g in range(n_g):
            cols = slice(g * LANE, (g + 1) * LANE)
            u = _gelu(p_ref[:, cols])
            v = _gelu(p_ref[:, GW + g * LANE:GW + (g + 1) * LANE])
            vhat, _ = _group_layernorm(v)
            vln = (vhat * vg_ref[:, cols]).astype(MXU_DTYPE)
            mixed = _dot(wt_ref[g], vln, NN) + bt_ref[:, g:g + 1]
            o = u * mixed
            r = lax.rsqrt(_mean1(o * o) + EPS)
            on_ref[:, cols] = (o * r * og_ref[:, cols]).astype(on_ref.dtype)

    return pl.pallas_call(
        body, name=name, grid=(S // LANE,),
        in_specs=[pl.BlockSpec((LANE, 2 * GW), lambda n: (n, 0)),
                  pl.BlockSpec((1, GW), lambda n: (0, 0)),
                  pl.BlockSpec((n_g, LANE, LANE), lambda n: (0, 0, 0)),
                  pl.BlockSpec((LANE, n_g), lambda n: (0, 0)),
                  pl.BlockSpec((1, GW), lambda n: (0, 0))],
        out_specs=pl.BlockSpec((LANE, GW), lambda n: (n, 0)),
        out_shape=jax.ShapeDtypeStruct((S, GW), MXU_DTYPE),
        scratch_shapes=[pltpu.VMEM((n_g, LANE, LANE), MXU_DTYPE)],
        compiler_params=_params("arbitrary"),
    )(proj, v_gain, w_s, b_t, out_gain)


def _gmlp_bwd(proj, d_on, v_gain, w_s, b_t, out_gain, n_g, name):
    S = proj.shape[0]
    GW = n_g * LANE

    def body(p_ref, dn_ref, vg_ref, ws_ref, bt_ref, og_ref, dp_ref, dws_ref, dbt_ref, dvg_ref, dog_ref, wt_ref):
        @pl.when(pl.program_id(0) == 0)
        def _():
            _causal_weights(ws_ref, wt_ref, n_g)
            dws_ref[...] = jnp.zeros_like(dws_ref)
            dbt_ref[...] = jnp.zeros_like(dbt_ref)
            dvg_ref[...] = jnp.zeros_like(dvg_ref)
            dog_ref[...] = jnp.zeros_like(dog_ref)

        row = lax.broadcasted_iota(jnp.int32, (LANE, LANE), 0)
        col = lax.broadcasted_iota(jnp.int32, (LANE, LANE), 1)
        for g in range(n_g):
            cols = slice(g * LANE, (g + 1) * LANE)
            vcols = slice(GW + g * LANE, GW + (g + 1) * LANE)
            pu, pv = p_ref[:, cols], p_ref[:, vcols]
            u, v = _gelu(pu), _gelu(pv)
            vhat, rstd = _group_layernorm(v)
            gain = vg_ref[:, cols]
            vln = (vhat * gain).astype(MXU_DTYPE)
            mixed = _dot(wt_ref[g], vln, NN) + bt_ref[:, g:g + 1]
            o = u * mixed
            r = lax.rsqrt(_mean1(o * o) + EPS)
            oh = o * r
            dn = dn_ref[:, cols]
            dog_ref[:, cols] += _sum0(dn * oh)
            dhn = dn * og_ref[:, cols]
            d_o = r * (dhn - oh * _mean1(dhn * oh))
            du = d_o * mixed
            dmix = d_o * u
            dbt_ref[:, g:g + 1] += jnp.sum(dmix, axis=1, keepdims=True)
            dmix_b = dmix.astype(MXU_DTYPE)
            dws_ref[g] += jnp.where(col <= row, _dot(dmix_b, vln, NT), 0.0)
            dvln = _dot(wt_ref[g], dmix_b, TN)
            dvg_ref[:, cols] += _sum0(dvln * vhat)
            dxh = dvln * gain
            dv = rstd * (dxh - _mean1(dxh) - vhat * _mean1(dxh * vhat))
            dp_ref[:, cols] = (du * _gelu_grad(pu)).astype(dp_ref.dtype)
            dp_ref[:, vcols] = (dv * _gelu_grad(pv)).astype(dp_ref.dtype)

    return pl.pallas_call(
        body, name=name, grid=(S // LANE,),
        in_specs=[pl.BlockSpec((LANE, 2 * GW), lambda n: (n, 0)),
                  pl.BlockSpec((LANE, GW), lambda n: (n, 0)),
                  pl.BlockSpec((1, GW), lambda n: (0, 0)),
                  pl.BlockSpec((n_g, LANE, LANE), lambda n: (0, 0, 0)),
                  pl.BlockSpec((LANE, n_g), lambda n: (0, 0)),
                  pl.BlockSpec((1, GW), lambda n: (0, 0))],
        out_specs=[pl.BlockSpec((LANE, 2 * GW), lambda n: (n, 0)),
                   pl.BlockSpec((n_g, LANE, LANE), lambda n: (0, 0, 0)),
                   pl.BlockSpec((LANE, n_g), lambda n: (0, 0)),
                   pl.BlockSpec((1, GW), lambda n: (0, 0)),
                   pl.BlockSpec((1, GW), lambda n: (0, 0))],
        out_shape=[jax.ShapeDtypeStruct((S, 2 * GW), MXU_DTYPE),
                   jax.ShapeDtypeStruct((n_g, LANE, LANE), F32),
                   jax.ShapeDtypeStruct((LANE, n_g), F32),
                   jax.ShapeDtypeStruct((1, GW), F32),
                   jax.ShapeDtypeStruct((1, GW), F32)],
        scratch_shapes=[pltpu.VMEM((n_g, LANE, LANE), MXU_DTYPE)],
        compiler_params=_params("arbitrary"),
    )(proj, d_on, v_gain, w_s, b_t, out_gain)


def _tri_sum(v, tri, exact=True):
    hi = v.astype(MXU_DTYPE)
    if not exact:
        return _dot(hi, tri, NN)
    lo = (v - hi.astype(F32)).astype(MXU_DTYPE)
    return _dot(hi, tri, NN) + _dot(lo, tri, NN)


def _log_sigmoids(z):
    sp = jnp.log(1.0 + jnp.exp(-jnp.abs(z)))
    return jnp.minimum(z, 0.0) - sp, jnp.minimum(-z, 0.0) - sp


def _rows(i, size):
    return pl.ds(pl.multiple_of(i * size, size), size)


SB_QUERY_TILE = 2048
SB_KEY_TILE = 256


def _sb_tiles(S):
    tq = _tile(S, SB_QUERY_TILE)
    tk = _tile(tq, SB_KEY_TILE)
    assert (tq // tk) % 2 == 0, "the key sweep takes two blocks a pass"
    return tq, tk, S // tq, tq // tk


def _triangle(n, keep):
    row = lax.broadcasted_iota(jnp.int32, (n, n), 0)
    col = lax.broadcasted_iota(jnp.int32, (n, n), 1)
    return jnp.where(keep(row, col), 1.0, 0.0).astype(MXU_DTYPE)


def _strictly_before(tq, tk, key_offset):
    row = lax.broadcasted_iota(jnp.int32, (tq, tk), 0)
    col = lax.broadcasted_iota(jnp.int32, (tq, tk), 1)
    return col + key_offset < row


def _sb_specs(S, n_g, n_h):
    base = 2 * n_g
    q_spec = pl.BlockSpec((S, LANE), lambda h: (0, base + h))
    k_spec = pl.BlockSpec((S, LANE), lambda h: (0, base + n_h + h))
    v_spec = pl.BlockSpec((S, LANE), lambda h: (0, base + 2 * n_h + h))
    gain_spec = pl.BlockSpec((1, LANE), lambda h: (0, n_g + h))
    head_spec = pl.BlockSpec((S, LANE), lambda h: (0, h))
    return q_spec, k_spec, v_spec, gain_spec, head_spec


def _sb_fwd(proj, out_gain, n_g, n_h, name):
    S = proj.shape[0]
    TQ, TK, NQ, KPQ = _sb_tiles(S)
    scale = LANE ** -0.5
    q_spec, k_spec, v_spec, gain_spec, head_spec = _sb_specs(S, n_g, n_h)

    def body(q_ref, k_ref, v_ref, og_ref, o_ref, on_ref, ls_ref, qb, kb, vb):
        qb[...] = q_ref[...].astype(MXU_DTYPE)
        kb[...] = k_ref[...].astype(MXU_DTYPE)
        vb[...] = v_ref[...].astype(MXU_DTYPE)
        after = _triangle(TK, lambda r, c: r > c)

        def block(qi, j, ctail, acc, key_offset):
            skip = key_offset or 0
            z = _dot(qi[skip:], kb[_rows(j, TK), :], NT) * scale
            lb, l1m = _log_sigmoids(z)
            if key_offset is not None:
                strict = _strictly_before(TQ - skip, TK, 0)
                l1m = jnp.where(strict, l1m, 0.0)
            a = jnp.exp(lb + ctail[skip:] + _tri_sum(l1m, after))
            if key_offset is not None:
                a = jnp.where(strict, a, 0.0)
            acc_new = acc[skip:] + _dot(a.astype(MXU_DTYPE), vb[_rows(j, TK), :], NN)
            ctail_new = ctail[skip:] + jnp.sum(l1m, axis=1, keepdims=True)
            if skip:
                ctail_new = jnp.concatenate([ctail[:skip], ctail_new], axis=0)
                acc_new = jnp.concatenate([acc[:skip], acc_new], axis=0)
            return ctail_new, acc_new

        def q_loop(i, carry):
            qi = qb[_rows(i, TQ), :]
            state = (jnp.zeros((TQ, 1), F32), jnp.zeros((TQ, LANE), F32))
            for d in reversed(range(KPQ)):
                state = block(qi, i * KPQ + d, state[0], state[1], d * TK)
            def pair(jj, st):
                st = block(qi, i * KPQ - 1 - 2 * jj, st[0], st[1], None)
                return block(qi, i * KPQ - 2 - 2 * jj, st[0], st[1], None)

            ctail, acc = lax.fori_loop(0, i * (KPQ // 2), pair, state)
            ls_ref[_rows(i, TQ), :] = jnp.broadcast_to(ctail, (TQ, LANE))
            o_ref[_rows(i, TQ), :] = acc
            r = lax.rsqrt(_mean1(acc * acc) + EPS)
            on_ref[_rows(i, TQ), :] = (acc * r * og_ref[...]).astype(on_ref.dtype)
            return carry

        lax.fori_loop(0, NQ, q_loop, 0)

    return pl.pallas_call(
        body, name=name, grid=(n_h,),
        in_specs=[q_spec, k_spec, v_spec, gain_spec],
        out_specs=[head_spec, head_spec, head_spec],
        out_shape=[jax.ShapeDtypeStruct((S, n_h * LANE), F32), jax.ShapeDtypeStruct((S, n_h * LANE), MXU_DTYPE),
                   jax.ShapeDtypeStruct((S, n_h * LANE), F32)],
        scratch_shapes=[pltpu.VMEM((S, LANE), MXU_DTYPE)] * 3,
        compiler_params=_params("parallel"),
    )(proj, proj, proj, out_gain)


def _sb_bwd(proj, o_sb, l_sum, d_on, out_gain, n_g, n_h, name):
    S = proj.shape[0]
    TQ, TK, NQ, KPQ = _sb_tiles(S)
    scale = LANE ** -0.5
    q_spec, k_spec, v_spec, gain_spec, head_spec = _sb_specs(S, n_g, n_h)
    dn_spec = pl.BlockSpec((S, LANE), lambda h: (0, n_g + h))
    dgain_spec = pl.BlockSpec((1, LANE), lambda h: (0, h))

    def body(q_ref, k_ref, v_ref, o_ref, ls_ref, dn_ref, og_ref, dq_ref, dk_ref, dv_ref, dog_ref,
             qb, kb, vb, dob, dk_acc, dv_acc):
        qb[...] = q_ref[...].astype(MXU_DTYPE)
        kb[...] = k_ref[...].astype(MXU_DTYPE)
        vb[...] = v_ref[...].astype(MXU_DTYPE)
        o, dn = o_ref[...], dn_ref[...]
        r = lax.rsqrt(_mean1(o * o) + EPS)
        oh = o * r
        dog_ref[...] = _sum0(dn * oh)
        dhn = dn * og_ref[...]
        dob[...] = (r * (dhn - oh * _mean1(dhn * oh))).astype(MXU_DTYPE)
        dk_acc[...] = jnp.zeros_like(dk_acc)
        dv_acc[...] = jnp.zeros_like(dv_acc)

        up_to = _triangle(TK, lambda r, c: r <= c)
        before = _triangle(TK, lambda r, c: r < c)

        def block(qi, doi, ltot, j, cl, cdl, dq, key_offset):
            skip = key_offset or 0
            q_in, do_in = qi[skip:], doi[skip:]
            kj, vj = kb[_rows(j, TK), :], vb[_rows(j, TK), :]
            z = _dot(q_in, kj, NT) * scale
            lb, l1m = _log_sigmoids(z)
            if key_offset is not None:
                strict = _strictly_before(TQ - skip, TK, 0)
                l1m = jnp.where(strict, l1m, 0.0)
            a = jnp.exp(lb + (ltot[skip:] - (cl[skip:] + _tri_sum(l1m, up_to))))
            if key_offset is not None:
                a = jnp.where(strict, a, 0.0)
            dl = _dot(do_in, vj, NT) * a
            d_l1m = cdl[skip:] + _tri_sum(dl, before, exact=False)
            beta = jnp.exp(lb)
            dz = dl * (1.0 - beta) - beta * d_l1m
            if key_offset is not None:
                dz = jnp.where(strict, dz, 0.0)
            dzs = (dz * scale).astype(MXU_DTYPE)
            dk_acc[_rows(j, TK), :] += _dot(dzs, q_in, TN)
            dv_acc[_rows(j, TK), :] += _dot(a.astype(MXU_DTYPE), do_in, TN)
            cl_new = cl[skip:] + jnp.sum(l1m, axis=1, keepdims=True)
            cdl_new = cdl[skip:] + jnp.sum(dl, axis=1, keepdims=True)
            dq_new = dq[skip:] + _dot(dzs, kj, NN)
            if skip:
                cl_new = jnp.concatenate([cl[:skip], cl_new], axis=0)
                cdl_new = jnp.concatenate([cdl[:skip], cdl_new], axis=0)
                dq_new = jnp.concatenate([dq[:skip], dq_new], axis=0)
            return cl_new, cdl_new, dq_new

        def q_loop(i, carry):
            qi, doi = qb[_rows(i, TQ), :], dob[_rows(i, TQ), :]
            ltot = ls_ref[_rows(i, TQ), :][:, :1]
            zero_col = jnp.zeros((TQ, 1), F32)
            def pair(jj, st):
                st = block(qi, doi, ltot, 2 * jj, st[0], st[1], st[2], None)
                return block(qi, doi, ltot, 2 * jj + 1, st[0], st[1], st[2], None)

            state = lax.fori_loop(0, i * (KPQ // 2), pair, (zero_col, zero_col, jnp.zeros((TQ, LANE), F32)))
            for d in range(KPQ):
                state = block(qi, doi, ltot, i * KPQ + d, state[0], state[1], state[2], d * TK)
            dq_ref[_rows(i, TQ), :] = state[2].astype(dq_ref.dtype)
            return carry

        lax.fori_loop(0, NQ, q_loop, 0)
        dk_ref[...] = dk_acc[...].astype(dk_ref.dtype)
        dv_ref[...] = dv_acc[...].astype(dv_ref.dtype)

    W = n_h * LANE
    return pl.pallas_call(
        body, name=name, grid=(n_h,),
        in_specs=[q_spec, k_spec, v_spec, head_spec, head_spec, dn_spec, gain_spec],
        out_specs=[head_spec, head_spec, head_spec, dgain_spec],
        out_shape=[jax.ShapeDtypeStruct((S, W), MXU_DTYPE)] * 3 + [jax.ShapeDtypeStruct((1, W), F32)],
        scratch_shapes=[pltpu.VMEM((S, LANE), MXU_DTYPE)] * 4 + [pltpu.VMEM((S, LANE), F32)] * 2,
        compiler_params=_params("parallel"),
    )(proj, proj, proj, o_sb, l_sum, d_on, out_gain)


def _mod_part(c_all, w_ada, b_ada_cols, name):
    B, K = c_all.shape
    N = w_ada.shape[1]
    tn = _tile(N, 512)

    def body(c_ref, w_ref, b_ref, o_ref):
        cv = c_ref[...]
        ca = (cv * jax.nn.sigmoid(cv)).astype(MXU_DTYPE)
        o_ref[...] = _dot(ca, w_ref[...].astype(MXU_DTYPE), NN) + b_ref[...]

    return pl.pallas_call(
        body, name=name, grid=(N // tn,),
        in_specs=[pl.BlockSpec((B, K), lambda j: (0, 0)), pl.BlockSpec((K, tn), lambda j: (0, j)),
                  pl.BlockSpec((1, tn), lambda j: (0, j))],
        out_specs=pl.BlockSpec((B, tn), lambda j: (0, j)),
        out_shape=jax.ShapeDtypeStruct((B, N), F32), compiler_params=_params("parallel"))(c_all, w_ada, b_ada_cols)


def _adamw_math(w, g, m, v):
    m = ADAM_B1 * m + (1.0 - ADAM_B1) * g
    v = ADAM_B2 * v + (1.0 - ADAM_B2) * (g * g)
    m_hat = m / (1.0 - ADAM_B1 ** ADAM_STEP)
    v_hat = v / (1.0 - ADAM_B2 ** ADAM_STEP)
    delta = -ADAM_LR * (m_hat / (jnp.sqrt(v_hat) + ADAM_EPS) + ADAM_WD * w)
    return delta, m, v


def _adamw(w, g, m, v, name):
    R, C = w.shape
    tr = _tile(R, max(8, (1 << 19) // C), 8)
    spec = pl.BlockSpec((tr, C), lambda i: (i, 0))

    def body(w_ref, g_ref, m_ref, v_ref, go_ref, d_ref, mo_ref, vo_ref):
        g = g_ref[...]
        go_ref[...] = g
        d_ref[...], mo_ref[...], vo_ref[...] = _adamw_math(w_ref[...], g, m_ref[...], v_ref[...])

    return pl.pallas_call(body, name=name, grid=(R // tr,), in_specs=[spec] * 4, out_specs=[spec] * 4,
                          out_shape=[jax.ShapeDtypeStruct((R, C), F32)] * 4, compiler_params=_params("parallel"))(w, g, m, v)


def _adamw_ada(c_all, dmod_cols, w, m, v, name):
    K, N = w.shape
    B = c_all.shape[0]
    tk, tn = _tile(K, 512), _tile(N, 1024)
    spec = pl.BlockSpec((tk, tn), lambda i, j: (i, j))

    def body(c_ref, dm_ref, w_ref, m_ref, v_ref, g_ref, d_ref, mo_ref, vo_ref):
        cv = c_ref[...]
        ca = (cv * jax.nn.sigmoid(cv)).astype(MXU_DTYPE)
        g = _dot(ca, dm_ref[...].astype(MXU_DTYPE), TN)
        g_ref[...] = g
        d_ref[...], mo_ref[...], vo_ref[...] = _adamw_math(w_ref[...], g, m_ref[...], v_ref[...])

    return pl.pallas_call(
        body, name=name, grid=(K // tk, N // tn),
        in_specs=[pl.BlockSpec((B, tk), lambda i, j: (0, i)), pl.BlockSpec((B, tn), lambda i, j: (0, j)), spec, spec, spec],
        out_specs=[spec] * 4, out_shape=[jax.ShapeDtypeStruct((K, N), F32)] * 4,
        compiler_params=_params("parallel", "parallel"))(c_all, dmod_cols, w, m, v)


def _sum_devices(gathered, n_dev, name):
    R = gathered.shape[0] // n_dev
    C = gathered.shape[1]
    tr = _tile(R, 512, 8)
    n_blk = R // tr

    def body(*refs):
        acc = refs[0][...]
        for r in refs[1:n_dev]:
            acc = acc + r[...]
        refs[n_dev][...] = acc

    in_specs = [pl.BlockSpec((tr, C), functools.partial(lambda i, d: (d * n_blk + i, 0), d=d)) for d in range(n_dev)]
    return pl.pallas_call(body, name=name, grid=(n_blk,), in_specs=in_specs,
                          out_specs=pl.BlockSpec((tr, C), lambda i: (i, 0)),
                          out_shape=jax.ShapeDtypeStruct((R, C), F32), compiler_params=_params("parallel"))(*([gathered] * n_dev))


def _place():
    x, y, c = lax.axis_index("x"), lax.axis_index("y"), lax.axis_index("c")
    return x, y, c


def _allgather8(blk, name):
    m_per, n = blk.shape

    def body(x_ref, out_ref, send_sems, recv_sems, local_sem):
        x, y, c = _place()
        me, sibling = (x, y, c), (x, y, 1 - c)
        chips = [(1 - x, y), (x, 1 - y), (1 - x, 1 - y)]

        def rows(px, py, pc):
            return out_ref.at[pl.ds((4 * px + 2 * py + pc) * m_per, m_per), :]

        def copy(k, block, to, src=None):
            return pltpu.make_async_remote_copy(
                src_ref=rows(*block) if src is None else src, dst_ref=rows(*block),
                send_sem=send_sems.at[k], recv_sem=recv_sems.at[k], device_id=to, device_id_type=MESH)

        mine = pltpu.make_async_copy(x_ref, rows(*me), local_sem)
        mine.start()
        first = [copy(0, me, sibling, src=x_ref)]
        first += [copy(1 + j, me, (*chip, c), src=x_ref) for j, chip in enumerate(chips)]
        for cp in first:
            cp.start()
        passed = [copy(4 + j, (*chip, c), sibling) for j, chip in enumerate(chips)]
        for j, chip in enumerate(chips):
            copy(1 + j, (*chip, c), me).wait_recv()
            passed[j].start()
        copy(0, sibling, me).wait_recv()
        for j, chip in enumerate(chips):
            copy(4 + j, (*chip, 1 - c), me).wait_recv()
        for cp in first + passed:
            cp.wait_send()
        mine.wait()

    return pl.pallas_call(
        body, name=name,
        out_shape=jax.ShapeDtypeStruct((8 * m_per, n), blk.dtype),
        in_specs=[pl.BlockSpec(memory_space=pltpu.VMEM)],
        out_specs=pl.BlockSpec(memory_space=pltpu.VMEM),
        scratch_shapes=[pltpu.SemaphoreType.DMA((7,)), pltpu.SemaphoreType.DMA((7,)), pltpu.SemaphoreType.DMA],
        compiler_params=pltpu.CompilerParams(vmem_limit_bytes=V7X_VMEM_LIMIT),
    )(blk)


class _Sharded:
    def __init__(self, shard_shape, by_cols):
        r, c = shard_shape
        self.by_cols = by_cols
        self.full = (r, N_CHIPS * c) if by_cols else (N_CHIPS * r, c)
        self.shard = (r, c)
        self.half_rows = r // 2
        self.half = (r // 2, c)

    def shard_of(self, ref, k):
        r, c = self.shard
        return ref.at[:, pl.ds(k * c, c)] if self.by_cols else ref.at[pl.ds(k * r, r), :]

    def half_of(self, ref, k, hc):
        r, c = self.shard
        h = self.half_rows
        if self.by_cols:
            return ref.at[pl.ds(hc * h, h), pl.ds(k * c, c)]
        return ref.at[pl.ds(k * r + hc * h, h), :]

    def chunk_of(self, ref, k, hc, ch, n):
        r, c = self.shard
        h = self.half_rows
        q = h // n
        if self.by_cols:
            return ref.at[pl.ds(hc * h + ch * q, q), pl.ds(k * c, c)]
        return ref.at[pl.ds(k * r + hc * h + ch * q, q), :]

    def half_of_shard(self, ref, hc):
        return ref.at[pl.ds(hc * self.half_rows, self.half_rows), :]

    def part_of_halves(self, ref, k):
        r, c = self.shard
        h = self.half_rows
        return ref.at[:, pl.ds(k * c, c)] if self.by_cols else ref.at[pl.ds(k * h, h), :]


def _on_each_place(x, y, c, fn, by_chip=True, by_core=True):
    q = 2 * x + y
    for k in range(N_CHIPS if by_chip else 1):
        for cc in range(2 if by_core else 1):
            cond = None
            if by_chip:
                cond = q == k
            if by_core:
                cond = (c == cc) if cond is None else jnp.logical_and(cond, c == cc)
            pl.when(cond)(functools.partial(fn, k, cc))


def _chip_id(k, c):
    return (k // 2, k % 2, c)


def _handshake(peers):
    barrier = pltpu.get_barrier_semaphore()
    for peer in peers:
        pl.semaphore_signal(barrier, inc=1, device_id=peer, device_id_type=MESH)
    pl.semaphore_wait(barrier, len(peers))


def _on_sequencer(body, inputs, out_structs, n_copies, peers_of, name, collective_id, return_inputs=False):
    in_refs = [jax.new_ref(a, memory_space=pltpu.MemorySpace.HBM) for a in inputs]
    out_refs = [jax.empty_ref(s, memory_space=pltpu.MemorySpace.HBM) for s in out_structs]

    @pl.kernel(mesh=plsc.ScalarSubcoreMesh(axis_name="sequencer", num_cores=1), name=name,
               scratch_types=(pltpu.SemaphoreType.DMA((n_copies,)), pltpu.SemaphoreType.DMA((n_copies,))),
               compiler_params=pltpu.CompilerParams(collective_id=collective_id))
    def launch(send_sems, recv_sems):
        x, y, c = _place()
        _handshake(peers_of(x, y, c))
        body(in_refs, out_refs, send_sems, recv_sems, x, y, c)

    launch()
    return [r[...] for r in (in_refs if return_inputs else out_refs)]


def _sibling(x, y, c):
    return [(x, y, 1 - c)]


def _same_core_of_other_chips(x, y, c):
    return [(1 - x, y, c), (x, 1 - y, c), (1 - x, 1 - y, c)]


GATHER_CHUNKS = 4
GATHER_COPIES = 6 * GATHER_CHUNKS


def _allgather8_on_sequencer(blk, name, collective_id):
    m_per, n = blk.shape
    x, y, c = _place()
    placed = lax.dynamic_update_slice(jnp.zeros((8 * m_per, n), blk.dtype), blk, ((4 * x + 2 * y + c) * m_per, 0))

    def body(refs, _, send_sems, recv_sems, x, y, c):
        out_ref, = refs

        def at_place(k, cc):
            def rows(kk, pc):
                return out_ref.at[pl.ds((2 * kk + pc) * m_per, m_per), :]

            def copy(slot, block, to):
                return pltpu.make_async_remote_copy(src_ref=rows(*block), dst_ref=rows(*block), send_sem=send_sems.at[slot],
                                                    recv_sem=recv_sems.at[slot], device_id=to, device_id_type=MESH)

            others = [k ^ flip for flip in FLIPS]
            sends = [copy(0, (k, cc), _chip_id(k, 1 - cc))] + [copy(1 + j, (k, cc), _chip_id(kk, cc)) for j, kk in enumerate(others)]
            for cp in sends:
                cp.start()
            for j, kk in enumerate(others):
                copy(1 + j, (kk, cc), _chip_id(k, cc)).wait_recv()
                cp = copy(4 + j, (kk, cc), _chip_id(k, 1 - cc))
                cp.start()
                sends.append(cp)
            copy(0, (k, 1 - cc), _chip_id(k, cc)).wait_recv()
            for j, kk in enumerate(others):
                copy(4 + j, (kk, 1 - cc), _chip_id(k, cc)).wait_recv()
            for cp in sends:
                cp.wait_send()

        _on_each_place(x, y, c, at_place)

    def peers(x, y, c):
        return _sibling(x, y, c) + _same_core_of_other_chips(x, y, c)

    return _on_sequencer(body, [placed], [], 7, peers, name, collective_id, return_inputs=True)[0]


def _gather_weights(fulls, geoms, name, collective_id):
    n_w = len(fulls)
    n_ch, n_relay = GATHER_CHUNKS, GATHER_CHUNKS // 2
    f_refs = [jax.new_ref(f, memory_space=pltpu.MemorySpace.HBM) for f in fulls]
    FLIP_X, FLIP_Y, FLIP_BOTH = FLIPS
    TO_X, TO_Y, RELAY_TO_Y, RELAY_TO_X, ON_X, ON_Y, ON_DIAG = 0, n_ch, 2 * n_ch, 2 * n_ch + n_relay, 3 * n_ch, 4 * n_ch, 5 * n_ch

    @pl.kernel(mesh=plsc.ScalarSubcoreMesh(axis_name="sequencer", num_cores=1), name=name,
               scratch_types=(pltpu.SemaphoreType.DMA((GATHER_COPIES * n_w,)), pltpu.SemaphoreType.DMA((GATHER_COPIES * n_w,))),
               compiler_params=pltpu.CompilerParams(collective_id=collective_id))
    def launch(send_sems, recv_sems):
        x, y, c = _place()
        _handshake([(x, y, 1 - c), (1 - x, y, c), (x, 1 - y, c)])

        def at_place(k, cc):
            kx, ky, kd = k ^ FLIP_X, k ^ FLIP_Y, k ^ FLIP_BOTH
            me, sibling = _chip_id(k, cc), _chip_id(k, 1 - cc)
            started = []

            def copy(i, slot, src, dst, to, start=True):
                cp = pltpu.make_async_remote_copy(src_ref=src, dst_ref=dst, send_sem=send_sems.at[GATHER_COPIES * i + slot],
                                                  recv_sem=recv_sems.at[GATHER_COPIES * i + slot], device_id=to, device_id_type=MESH)
                if start:
                    cp.start()
                    started.append(cp)
                return cp

            def pass_on(i, slot, ref, to):
                copy(i, slot, ref, ref, to)

            def landed(i, slot, ref):
                copy(i, slot, ref, ref, me, start=False).wait_recv()

            y_order = [(n_relay + s) % n_ch for s in range(n_ch)]
            for i, (g, f_ref) in enumerate(zip(geoms, f_refs)):
                for s in range(n_ch):
                    pass_on(i, TO_X + s, g.chunk_of(f_ref, k, cc, s, n_ch), _chip_id(kx, cc))
                    pass_on(i, TO_Y + y_order[s], g.chunk_of(f_ref, k, cc, y_order[s], n_ch), _chip_id(ky, cc))
            for i, (g, f_ref) in enumerate(zip(geoms, f_refs)):
                for s in range(n_ch):
                    from_x = g.chunk_of(f_ref, kx, cc, s, n_ch)
                    landed(i, TO_X + s, from_x)
                    if s < n_relay:
                        pass_on(i, RELAY_TO_Y + s, from_x, _chip_id(ky, cc))
                    pass_on(i, ON_X + s, from_x, sibling)
                    ch = y_order[s]
                    from_y = g.chunk_of(f_ref, ky, cc, ch, n_ch)
                    landed(i, TO_Y + ch, from_y)
                    if ch >= n_relay:
                        pass_on(i, RELAY_TO_X + ch - n_relay, from_y, _chip_id(kx, cc))
                    pass_on(i, ON_Y + ch, from_y, sibling)
                for r in range(n_relay):
                    via_y = g.chunk_of(f_ref, kd, cc, r, n_ch)
                    landed(i, RELAY_TO_Y + r, via_y)
                    pass_on(i, ON_DIAG + r, via_y, sibling)
                    via_x = g.chunk_of(f_ref, kd, cc, n_relay + r, n_ch)
                    landed(i, RELAY_TO_X + r, via_x)
                    pass_on(i, ON_DIAG + n_relay + r, via_x, sibling)
            for i, (g, f_ref) in enumerate(zip(geoms, f_refs)):
                for slot, kk in ((ON_X, kx), (ON_Y, ky), (ON_DIAG, kd)):
                    for ch in range(n_ch):
                        landed(i, slot + ch, g.chunk_of(f_ref, kk, 1 - cc, ch, n_ch))
            for cp in started:
                cp.wait_send()

        _on_each_place(x, y, c, at_place)

    launch()
    return [f_ref[...] for f_ref in f_refs]


def _swap_core_halves(grads, geoms, name, collective_id):
    n_cp = sum(1 if g.by_cols else N_CHIPS for g in geoms)

    def body(g_refs, t_refs, send_sems, recv_sems, x, y, c):

        def at_place(_, cc):
            def pairs(hc):
                out = []
                for g, g_ref, t_ref in zip(geoms, g_refs, t_refs):
                    if g.by_cols:
                        out.append((g_ref.at[pl.ds(hc * g.half_rows, g.half_rows), :], t_ref))
                    else:
                        out += [(g.half_of(g_ref, k, hc), g.part_of_halves(t_ref, k)) for k in range(N_CHIPS)]
                return out

            sends = [pltpu.make_async_remote_copy(src_ref=src, dst_ref=dst, send_sem=send_sems.at[n],
                                                  recv_sem=recv_sems.at[n], device_id=(x, y, 1 - cc), device_id_type=MESH)
                     for n, (src, dst) in enumerate(pairs(1 - cc))]
            for cp in sends:
                cp.start()
            for n, (src, dst) in enumerate(pairs(cc)):
                pltpu.make_async_remote_copy(src_ref=src, dst_ref=dst, send_sem=send_sems.at[n], recv_sem=recv_sems.at[n],
                                             device_id=(x, y, cc), device_id_type=MESH).wait_recv()
            for cp in sends:
                cp.wait_send()

        _on_each_place(x, y, c, at_place, by_chip=False)

    return _on_sequencer(body, grads, [jax.ShapeDtypeStruct((g.full[0] // 2, g.full[1]), F32) for g in geoms],
                         n_cp, _sibling, name, collective_id)


def _send_to_sibling(buffers, name, collective_id):
    def body(src_refs, dst_refs, send_sems, recv_sems, x, y, c):
        def copy(i):
            return pltpu.make_async_remote_copy(src_ref=src_refs[i], dst_ref=dst_refs[i], send_sem=send_sems.at[i],
                                                recv_sem=recv_sems.at[i], device_id=(x, y, 1 - c), device_id_type=MESH)

        for i in range(len(buffers)):
            copy(i).start()
        for i in range(len(buffers)):
            copy(i).wait()

    return _on_sequencer(body, buffers, [jax.ShapeDtypeStruct(t.shape, t.dtype) for t in buffers], len(buffers),
                         _sibling, name, collective_id)


def _scatter_chip_sums(sums, geoms, name, collective_id):
    def body(s_refs, r_refs, send_sems, recv_sems, x, y, c):

        def at_place(k, _):
            sends = []
            for i, (g, s_ref, r_ref) in enumerate(zip(geoms, s_refs, r_refs)):
                for j, flip in enumerate(FLIPS):
                    kk = k ^ flip
                    cp = pltpu.make_async_remote_copy(
                        src_ref=g.part_of_halves(s_ref, kk), dst_ref=r_ref.at[j], send_sem=send_sems.at[3 * i + j],
                        recv_sem=recv_sems.at[3 * i + j], device_id=(kk // 2, kk % 2, c), device_id_type=MESH)
                    cp.start()
                    sends.append(cp)
            for i, (g, s_ref, r_ref) in enumerate(zip(geoms, s_refs, r_refs)):
                for j in range(len(FLIPS)):
                    pltpu.make_async_remote_copy(
                        src_ref=g.part_of_halves(s_ref, k), dst_ref=r_ref.at[j], send_sem=send_sems.at[3 * i + j],
                        recv_sem=recv_sems.at[3 * i + j], device_id=(x, y, c), device_id_type=MESH).wait_recv()
            for cp in sends:
                cp.wait_send()

        _on_each_place(x, y, c, at_place, by_core=False)

    return _on_sequencer(body, sums, [jax.ShapeDtypeStruct((len(FLIPS),) + g.half, WIRE_DTYPE) for g in geoms],
                         len(FLIPS) * len(sums), _same_core_of_other_chips, name, collective_id)


def _share_reduced_halves(reduced, geoms, name, collective_id):
    def body(out_refs, _, send_sems, recv_sems, x, y, c):

        def at_place(_, cc):
            sends = []
            for i, (g, ref) in enumerate(zip(geoms, out_refs)):
                mine = g.half_of_shard(ref, cc)
                cp = pltpu.make_async_remote_copy(src_ref=mine, dst_ref=mine, send_sem=send_sems.at[i],
                                                  recv_sem=recv_sems.at[i], device_id=(x, y, 1 - cc), device_id_type=MESH)
                cp.start()
                sends.append(cp)
            for i, (g, ref) in enumerate(zip(geoms, out_refs)):
                theirs = g.half_of_shard(ref, 1 - cc)
                pltpu.make_async_remote_copy(src_ref=theirs, dst_ref=theirs, send_sem=send_sems.at[i],
                                             recv_sem=recv_sems.at[i], device_id=(x, y, cc), device_id_type=MESH).wait_recv()
            for cp in sends:
                cp.wait_send()

        _on_each_place(x, y, c, at_place, by_chip=False)

    return _on_sequencer(body, reduced, [], len(reduced), _sibling, name, collective_id, return_inputs=True)


def _chip_sum(place, grad, theirs, g, name):
    RH, C = theirs.shape
    h = g.half_rows
    tr = _tile(h, 256, 16)
    tc = _tile(C, 2048)
    per_half = h // tr

    if g.by_cols:
        grad_map = lambda i, j, p: (p[1] * per_half + i, j)
    else:
        grad_map = lambda i, j, p: ((i // per_half) * 2 * per_half + p[1] * per_half + i % per_half, j)

    def body(p_ref, a_ref, b_ref, f_ref, o_ref):
        total = a_ref[...] + b_ref[...]
        f_ref[...] = total
        o_ref[...] = total.astype(o_ref.dtype)

    return pl.pallas_call(
        body, name=name,
        grid_spec=pltpu.PrefetchScalarGridSpec(
            num_scalar_prefetch=1, grid=(RH // tr, C // tc),
            in_specs=[pl.BlockSpec((tr, tc), grad_map), pl.BlockSpec((tr, tc), lambda i, j, p: (i, j))],
            out_specs=[pl.BlockSpec((tr, tc), lambda i, j, p: (i, j))] * 2),
        out_shape=[jax.ShapeDtypeStruct((RH, C), F32), jax.ShapeDtypeStruct((RH, C), WIRE_DTYPE)],
        compiler_params=_params("parallel", "parallel"),
    )(place, grad, theirs)


def _dw_half(place, a, b, g, mine, name, add=None):
    K, R = a.shape
    C = b.shape[1]
    h = g.half_rows
    tm, tn = _tile(h, 1024, 16), _tile(C, 512)
    per_half = h // tm
    n_i = (R // 2) // tm

    def a_map(i, j, p):
        hc = p[1] if mine else 1 - p[1]
        if g.by_cols:
            return 0, hc * n_i + i
        return 0, (i // per_half) * 2 * per_half + hc * per_half + i % per_half

    mn_spec = pl.BlockSpec((tm, tn), lambda i, j, p: (i, j))

    def body(p_ref, a_ref, b_ref, *rest):
        acc = _dot(a_ref[...], b_ref[...], TN)
        if add is None:
            rest[0][...] = acc
        else:
            total = acc + rest[0][...]
            rest[1][...] = total
            rest[2][...] = total.astype(rest[2].dtype)

    out_shape = [jax.ShapeDtypeStruct((R // 2, C), F32)] + ([] if add is None else [jax.ShapeDtypeStruct((R // 2, C), WIRE_DTYPE)])
    return pl.pallas_call(
        body, name=name,
        grid_spec=pltpu.PrefetchScalarGridSpec(
            num_scalar_prefetch=1, grid=(n_i, C // tn),
            in_specs=[pl.BlockSpec((K, tm), a_map), pl.BlockSpec((K, tn), lambda i, j, p: (0, j))] + ([] if add is None else [mn_spec]),
            out_specs=[mn_spec] * len(out_shape)),
        out_shape=out_shape,
        compiler_params=_params("parallel", "arbitrary"),
    )(place, a, b, *([] if add is None else [add]))


def _reduce_half(place, sums, others, g, name):
    h, tc = g.half
    tr = _tile(h, 256, 16)
    per_half = h // tr
    sums_map = (lambda i, p: (i, p[0])) if g.by_cols else (lambda i, p: (p[0] * per_half + i, 0))

    def body(p_ref, s_ref, o0_ref, o1_ref, o2_ref, out_ref):
        acc = s_ref[...]
        for o_ref in (o0_ref, o1_ref, o2_ref):
            acc = acc + o_ref[...].astype(F32)
        out_ref[...] = acc

    other_specs = [pl.BlockSpec((None, tr, tc), functools.partial(lambda i, p, j: (j, i, 0), j=j)) for j in range(len(FLIPS))]
    return pl.pallas_call(
        body, name=name,
        grid_spec=pltpu.PrefetchScalarGridSpec(
            num_scalar_prefetch=1, grid=(per_half,),
            in_specs=[pl.BlockSpec((tr, tc), sums_map)] + other_specs,
            out_specs=pl.BlockSpec((tr, tc), lambda i, p: (p[1] * per_half + i, 0))),
        out_shape=jax.ShapeDtypeStruct(g.shard, F32),
        compiler_params=_params("arbitrary"),
    )(place, sums, others, others, others)


SLAB_ROW_UNIT = 256
SMALL = ("b_ada", "norm1_g", "v_norm_g", "w_spatial", "b_spatial", "out_norm_g", "norm2_g", "final_g")
BIG = ("w_in", "w_out", "w_gate", "w_up", "w_down")
BY_COLS = {"w_in": True, "w_out": False, "w_gate": True, "w_up": True, "w_down": False}
ORDER = ("w_ada", "b_ada", "norm1_g", "w_in", "v_norm_g", "w_spatial", "b_spatial", "out_norm_g", "w_out",
         "norm2_g", "w_gate", "w_up", "w_down", "final_g")


def _pack(parts):
    return jnp.concatenate([parts[n].reshape(-1) for n in SMALL]).reshape(-1, LANE)


def _unpack(slab, shapes):
    flat = slab.reshape(-1)
    out, at = {}, 0
    for n in SMALL:
        size = math.prod(shapes[n])
        out[n] = flat[at:at + size].reshape(shapes[n])
        at += size
    return out


def kernel(x, c, w_ada, b_ada, norm1_g, w_in, v_norm_g, w_spatial, b_spatial, out_norm_g, w_out, norm2_g, w_gate, w_up, w_down, final_g, loss_target, m_w_ada, m_b_ada, m_norm1_g, m_w_in, m_v_norm_g, m_w_spatial, m_b_spatial, m_out_norm_g, m_w_out, m_norm2_g, m_w_gate, m_w_up, m_w_down, m_final_g, v_w_ada, v_b_ada, v_norm1_g, v_w_in, v_v_norm_g, v_w_spatial, v_b_spatial, v_out_norm_g, v_w_out, v_norm2_g, v_w_gate, v_w_up, v_w_down, v_final_g):
    weights = dict(w_ada=w_ada, b_ada=b_ada, norm1_g=norm1_g, w_in=w_in, v_norm_g=v_norm_g, w_spatial=w_spatial,
                   b_spatial=b_spatial, out_norm_g=out_norm_g, w_out=w_out, norm2_g=norm2_g, w_gate=w_gate, w_up=w_up,
                   w_down=w_down, final_g=final_g)
    m_in = dict(w_ada=m_w_ada, b_ada=m_b_ada, norm1_g=m_norm1_g, w_in=m_w_in, v_norm_g=m_v_norm_g, w_spatial=m_w_spatial,
                b_spatial=m_b_spatial, out_norm_g=m_out_norm_g, w_out=m_w_out, norm2_g=m_norm2_g, w_gate=m_w_gate,
                w_up=m_w_up, w_down=m_w_down, final_g=m_final_g)
    v_in = dict(w_ada=v_w_ada, b_ada=v_b_ada, norm1_g=v_norm1_g, w_in=v_w_in, v_norm_g=v_v_norm_g, w_spatial=v_w_spatial,
                b_spatial=v_b_spatial, out_norm_g=v_out_norm_g, w_out=v_w_out, norm2_g=v_norm2_g, w_gate=v_w_gate,
                w_up=v_w_up, w_down=v_w_down, final_g=v_final_g)

    S, D = x.shape[1], x.shape[2]
    n_g = v_norm_g.shape[-1] // LANE
    n_h = (D - n_g * LANE) // LANE
    GW = n_g * LANE
    xi, yi, ci = _place()
    chip = 2 * xi + yi
    me = 4 * xi + 2 * yi + ci
    place = jnp.stack([chip, ci]).astype(jnp.int32)

    xs, target = x[0], loss_target[0]
    geoms = [_Sharded(weights[n].shape[1:], BY_COLS[n]) for n in BIG]

    full = {}
    for i, group in enumerate((("w_in",), ("w_out",), ("w_gate", "w_up"), ("w_down",))):
        gg = [geoms[BIG.index(n)] for n in group]
        own = [_cast_into_full(place, weights[n][0], g, "cast_" + n) for n, g in zip(group, gg)]
        gathered = _gather_weights(own, gg, "gather_" + "_".join(group), 1 + i)
        full.update(zip(group, gathered))

    c_pad = jnp.concatenate([c, jnp.zeros((7, D), F32)], axis=0)
    c_all = _allgather8(c_pad, "gather_c")[::8]
    n_ada = w_ada.shape[2]
    b_cols = lax.dynamic_slice(b_ada, (0, chip * n_ada), (1, n_ada))
    mod_parts = _allgather8(_mod_part(c_all, w_ada[0], b_cols, "mod_part"), "gather_mod")
    mod_all = mod_parts.reshape(N_CHIPS, 2, 8, n_ada)[:, 0].transpose(1, 0, 2).reshape(8, N_CHIPS * n_ada)
    mod = lax.dynamic_slice(mod_all, (me, 0), (1, 6 * D))
    shift1, scale1, gate1, shift2, scale2, gate2 = [mod[:, i * D:(i + 1) * D] for i in range(6)]

    b_t = b_spatial[0].T
    h1 = _norm_mod(xs, norm1_g, scale1, shift1, "norm1")
    proj, = _mm("nn", h1, full["w_in"], [F32], "proj")
    on_gm = _gmlp_fwd(proj, v_norm_g, w_spatial[0], b_t, out_norm_g, n_g, "gmlp_fwd")
    o_sb, on_sb, l_sum = _sb_fwd(proj, out_norm_g, n_g, n_h, "sb_fwd")
    o_n = jnp.concatenate([on_gm, on_sb], axis=1)
    attn, = _mm("nn", o_n, full["w_out"], [F32], "attn_out")
    x1, h2 = _residual_norm_mod(xs, attn, gate1, norm2_g, scale2, shift2, "norm2")
    a_g, a_u, f_in = _gate_up(h2, full["w_gate"], full["w_up"], "gate_up")
    f, = _mm("nn", f_in, full["w_down"], [F32], "down", tm=1024)
    dx2, df, d_gate2, d_final_g, loss_part = _final_loss_bwd(x1, f, gate2, final_g.reshape(1, D), target, "final")

    geom_of = dict(zip(BIG, geoms))
    grad_out, delta, new_m, new_v = {}, {}, {}, {}

    def theirs_first(group, operands, collective_id, after=None):
        outs = []
        for n, (a_op, b_op) in zip(group, operands):
            outs.append(_dw_half(place, a_op, b_op if after is None else _then(after, b_op), geom_of[n], False, "d_" + n + "_theirs")[0])
            after = outs[-1]
        return outs, _send_to_sibling(outs, "swap_" + "_".join(group), collective_id)

    def chip_sums(group, operands, theirs, after):
        f32s, wires = [], []
        for n, (a_op, b_op), t in zip(group, operands, theirs):
            sf, sw = _dw_half(place, a_op, b_op, geom_of[n], True, "d_" + n + "_mine", add=_then(after, t))
            f32s.append(sf)
            wires.append(sw)
            after = sw
        return f32s, wires

    def scatter(group, sums, collective_id):
        return _scatter_chip_sums(sums, [geom_of[n] for n in group], "scatter_" + "_".join(group), collective_id)

    def reduce_halves(group, sums, others, after):
        return [_reduce_half(place, sf, _then(after, o), geom_of[n], "reduce_" + n) for n, sf, o in zip(group, sums, others)]

    def share(group, halves, collective_id):
        return _share_reduced_halves(halves, [geom_of[n] for n in group], "share_" + "_".join(group), collective_id)

    def adamw(group, reduced, after):
        for n, r in zip(group, reduced):
            go, d, mo, vo = _adamw(weights[n][0], _then(after, r), m_in[n][0], v_in[n][0], "adamw_" + n)
            grad_out[n], delta[n], new_m[n], new_v[n] = go[None], d[None], mo[None], vo[None]
        return d

    g_down = ("w_down",)
    g_ffn = ("w_gate", "w_up")
    g_out = ("w_out",)
    g_in = ("w_in",)

    gr_down, = _mm("tn", f_in, df, [F32], "d_w_down", tm=1408, tn=1024)
    th_down, = _swap_core_halves([gr_down], [geom_of["w_down"]], "swap_w_down", 6)
    d_ag, d_au = _mm("nt", df, full["w_down"], [MXU_DTYPE, MXU_DTYPE], "d_ffn_in", extras=(a_g, a_u),
                     epilogue=_swiglu_bwd_epilogue)
    sf_down, sw_down = [[t] for t in _chip_sum(place, gr_down, _then(d_ag, th_down), geom_of["w_down"], "chip_sum_w_down")]
    ot_down = scatter(g_down, sw_down, 7)
    sent, th_ffn = theirs_first(g_ffn, [(h2, d_ag), (h2, d_au)], 9, after=sw_down)
    dh2 = _mm_ktiled("nt", [(_then(sent, d_ag), full["w_gate"]), (d_au, full["w_up"])], "d_h2", tn=512)
    sf_ffn, sw_ffn = chip_sums(g_ffn, [(h2, d_ag), (h2, d_au)], th_ffn, after=dh2)
    ot_ffn = scatter(g_ffn, sw_ffn, 10)
    hv_down = reduce_halves(g_down, sf_down, ot_down, after=sw_ffn)
    rd_down = share(g_down, hv_down, 8)
    dx1, d_shift2, d_scale2, d_norm2_g, d_gate1, d_attn = _norm_mod_bwd(
        _then(hv_down, dh2), x1, dx2, norm2_g, scale2, "norm2_bwd", branch=attn, gate=gate1)
    gr_out, = _mm("tn", o_n, d_attn, [F32], "d_w_out")
    th_out, = _swap_core_halves([gr_out], [geom_of["w_out"]], "swap_w_out", 12)
    d_on, = _mm("nt", _then(gr_out, d_attn), full["w_out"], [F32], "d_o")
    dp_gm, d_w_spatial, d_b_t, d_v_norm_g, d_og_gm = _gmlp_bwd(proj, d_on, v_norm_g, w_spatial[0], b_t, out_norm_g, n_g, "gmlp_bwd")
    sf_out, sw_out = [[t] for t in _chip_sum(place, gr_out, _then(dp_gm, th_out), geom_of["w_out"], "chip_sum_w_out")]
    ot_out = scatter(g_out, sw_out, 13)
    dq, dk, dv, d_og_sb = _sb_bwd(proj, o_sb, l_sum, _then(sw_out, d_on), out_norm_g, n_g, n_h, "sb_bwd")
    hv_ffn = reduce_halves(g_ffn, sf_ffn, ot_ffn, after=dq)
    rd_ffn = share(g_ffn, hv_ffn, 11)
    dproj = jnp.concatenate([_then(hv_ffn, dp_gm), dq, dk, dv], axis=1)
    sent, th_in = theirs_first(g_in, [(h1, dproj)], 15)
    dh1, = _mm("nt", _then(sent, dproj), full["w_in"], [F32], "d_h1", tm=1024)
    hv_out = reduce_halves(g_out, sf_out, ot_out, after=dh1)
    rd_out = share(g_out, hv_out, 14)
    grad_x, d_shift1, d_scale1, d_norm1_g = _norm_mod_bwd(_then(hv_out, dh1), xs, dx1, norm1_g, scale1, "norm1_bwd")

    dmod = jnp.concatenate([d_shift1, d_scale1, d_gate1, d_shift2, d_scale2, d_gate2], axis=1)
    small_parts = dict(b_ada=dmod, norm1_g=d_norm1_g, v_norm_g=d_v_norm_g, w_spatial=d_w_spatial, b_spatial=d_b_t.T,
                       out_norm_g=jnp.concatenate([d_og_gm, d_og_sb], axis=1), norm2_g=d_norm2_g, final_g=d_final_g)
    packed = _pack(small_parts)
    small_rows = packed.shape[0]
    rows = -(-(small_rows + 8) // SLAB_ROW_UNIT) * SLAB_ROW_UNIT
    slab = _then(grad_x, jnp.concatenate([packed, jnp.broadcast_to(loss_part, (8, LANE)),
                                          jnp.zeros((rows - small_rows - 8, LANE), F32)], axis=0))
    gathered = _allgather8_on_sequencer(slab, "gather_small", 18)
    sf_in, sw_in = chip_sums(g_in, [(h1, dproj)], th_in, after=slab)
    ot_in = scatter(g_in, sw_in, 16)
    done = adamw(g_down, rd_down, after=sw_in)
    done = adamw(g_ffn, rd_ffn, after=done)
    done = adamw(g_out, rd_out, after=done)
    gathered = _then(done, gathered)
    small_shapes = {n: weights[n].shape for n in SMALL}
    slab_sum = _sum_devices(gathered, 8, "sum_small")
    small_sum, loss = slab_sum[:small_rows], slab_sum[small_rows, 0]
    dmod_all = gathered.reshape(8, rows * LANE)[:, :6 * D]
    dmod_cols = lax.dynamic_slice(dmod_all, (0, chip * n_ada), (8, n_ada))
    g_ada, d, mo, vo = _adamw_ada(c_all, dmod_cols, w_ada[0], m_w_ada[0], v_w_ada[0], "adamw_w_ada")
    grad_out["w_ada"], delta["w_ada"], new_m["w_ada"], new_v["w_ada"] = g_ada[None], d[None], mo[None], vo[None]
    gs_small, d_small, mo, vo = _adamw(_pack({n: weights[n] for n in SMALL}), small_sum, _pack({n: m_in[n] for n in SMALL}),
                                       _pack({n: v_in[n] for n in SMALL}), "adamw_small")
    for dst, slab_out in ((grad_out, gs_small), (delta, d_small), (new_m, mo), (new_v, vo)):
        dst.update(_unpack(slab_out, small_shapes))
    hv_in = reduce_halves(g_in, sf_in, ot_in, after=d)
    adamw(g_in, share(g_in, hv_in, 17), after=d)

    return (loss, grad_x[None], *[grad_out[n] for n in ORDER], *[delta[n] for n in ORDER],
            *[new_m[n] for n in ORDER], *[new_v[n] for n in ORDER])
```

```python
import functools
import math

import jax
import jax.numpy as jnp
from jax import lax
from jax.experimental import pallas as pl
from jax.experimental.pallas import tpu as pltpu
from jax.experimental.pallas import tpu_sc as plsc

F32 = jnp.float32
MXU_DTYPE = jnp.bfloat16
WIRE_DTYPE = jnp.bfloat16
EPS = 1e-6
LANE = 128
V7X_VMEM_LIMIT = 56 * 1024 * 1024
MESH = pl.DeviceIdType.MESH
N_CHIPS = 4
FLIPS = (2, 1, 3)

ADAM_LR = 0.001
ADAM_B1 = 0.9
ADAM_B2 = 0.999
ADAM_EPS = 1e-08
ADAM_WD = 0.01
ADAM_STEP = 10


def _params(*semantics):
    return pltpu.CompilerParams(dimension_semantics=semantics or None, vmem_limit_bytes=V7X_VMEM_LIMIT)


def _tile(dim, pref, unit=LANE):
    best = None
    t = unit
    while t <= min(dim, pref):
        if dim % t == 0:
            best = t
        t += unit
    return best if best is not None else dim


def _then(first, second):
    return lax.optimization_barrier((first, second))[1]


def _sum0(v):
    return jnp.sum(v, axis=0, keepdims=True)


def _mean1(v):
    return jnp.mean(v, axis=-1, keepdims=True)


def _gelu(x):
    return 0.5 * x * (1.0 + lax.erf(x * (1.0 / math.sqrt(2.0))))


def _gelu_grad(x):
    cdf = 0.5 * (1.0 + lax.erf(x * (1.0 / math.sqrt(2.0))))
    return cdf + x * jnp.exp(-0.5 * x * x) * (1.0 / math.sqrt(2.0 * math.pi))


def _dot(a, b, dims):
    return lax.dot_general(a, b, (dims, ((), ())), preferred_element_type=F32)


NN = ((1,), (0,))
NT = ((1,), (1,))
TN = ((0,), (0,))


def _mm(kind, a, b, out_dtypes, name, tm=2048, tn=512, extras=(), epilogue=None):
    if kind == "nn":
        (M, K), N = a.shape, b.shape[1]
    elif kind == "nt":
        (M, K), N = a.shape, b.shape[0]
    else:
        (K, M), N = a.shape, b.shape[1]
    tm, tn = _tile(M, tm), _tile(N, tn)
    a_spec = pl.BlockSpec((K, tm), lambda i, j: (0, i)) if kind == "tn" else pl.BlockSpec((tm, K), lambda i, j: (i, 0))
    b_spec = pl.BlockSpec((tn, K), lambda i, j: (j, 0)) if kind == "nt" else pl.BlockSpec((K, tn), lambda i, j: (0, j))
    mn_spec = pl.BlockSpec((tm, tn), lambda i, j: (i, j))
    dims = {"nn": NN, "nt": NT, "tn": TN}[kind]
    n_extra = len(extras)

    n_chunks = 1 if epilogue is None or kind == "tn" else max(1, tm // 256)
    rows_per = tm // n_chunks

    def body(a_ref, b_ref, *rest):
        for r in range(n_chunks):
            rows = slice(r * rows_per, (r + 1) * rows_per)
            acc = _dot(a_ref[...] if n_chunks == 1 else a_ref[rows, :], b_ref[...], dims)
            res = (acc,) if epilogue is None else epilogue(acc, *[e[rows, :] for e in rest[:n_extra]])
            for o_ref, val in zip(rest[n_extra:], res):
                o_ref[rows, :] = val.astype(o_ref.dtype)

    outs = pl.pallas_call(
        body, name=name, grid=(M // tm, N // tn),
        in_specs=[a_spec, b_spec] + [mn_spec] * n_extra,
        out_specs=[mn_spec] * len(out_dtypes),
        out_shape=[jax.ShapeDtypeStruct((M, N), d) for d in out_dtypes],
        compiler_params=_params("parallel", "arbitrary"),
    )(a, b, *extras)
    return outs


def _mm_ktiled(kind, pairs, name, tm=2048, tn=1024, tk=1408):
    a0, b0 = pairs[0]
    M, K = a0.shape
    N = b0.shape[1] if kind == "nn" else b0.shape[0]
    tm, tn, tk = _tile(M, tm), _tile(N, tn), _tile(K, tk)
    a_spec = pl.BlockSpec((tm, tk), lambda i, j, k: (i, k))
    b_spec = pl.BlockSpec((tk, tn), lambda i, j, k: (k, j)) if kind == "nn" else pl.BlockSpec((tn, tk), lambda i, j, k: (j, k))
    dims = NN if kind == "nn" else NT
    n_pairs = len(pairs)

    def body(*refs):
        o_ref = refs[2 * n_pairs]
        acc = _dot(refs[0][...], refs[1][...], dims)
        for p in range(1, n_pairs):
            acc = acc + _dot(refs[2 * p][...], refs[2 * p + 1][...], dims)

        @pl.when(pl.program_id(2) == 0)
        def _():
            o_ref[...] = acc

        @pl.when(pl.program_id(2) != 0)
        def _():
            o_ref[...] += acc

    return pl.pallas_call(
        body, name=name, grid=(M // tm, N // tn, K // tk),
        in_specs=[a_spec, b_spec] * n_pairs,
        out_specs=pl.BlockSpec((tm, tn), lambda i, j, k: (i, j)),
        out_shape=jax.ShapeDtypeStruct((M, N), F32),
        compiler_params=_params("parallel", "parallel", "arbitrary"),
    )(*[x for pair in pairs for x in pair])


def _gate_up(h, wg, wu, name):
    (M, K), N = h.shape, wg.shape[1]
    tm, tn = _tile(M, 2048), _tile(N, 512)

    n_chunks = max(1, tm // 256)
    rows_per = tm // n_chunks

    def body(h_ref, wg_ref, wu_ref, ag_ref, au_ref, f_ref):
        for r in range(n_chunks):
            rows = slice(r * rows_per, (r + 1) * rows_per)
            hv = h_ref[rows, :]
            ag = _dot(hv, wg_ref[...], NN)
            au = _dot(hv, wu_ref[...], NN)
            ag_ref[rows, :] = ag.astype(ag_ref.dtype)
            au_ref[rows, :] = au.astype(au_ref.dtype)
            f_ref[rows, :] = (ag * jax.nn.sigmoid(ag) * au).astype(f_ref.dtype)

    w_spec = pl.BlockSpec((K, tn), lambda i, j: (0, j))
    mn_spec = pl.BlockSpec((tm, tn), lambda i, j: (i, j))
    return pl.pallas_call(
        body, name=name, grid=(M // tm, N // tn),
        in_specs=[pl.BlockSpec((tm, K), lambda i, j: (i, 0)), w_spec, w_spec],
        out_specs=[mn_spec] * 3,
        out_shape=[jax.ShapeDtypeStruct((M, N), MXU_DTYPE)] * 3,
        compiler_params=_params("parallel", "arbitrary"),
    )(h, wg, wu)


def _swiglu_bwd_epilogue(dfin, ag, au):
    ag, au = ag.astype(F32), au.astype(F32)
    sg = jax.nn.sigmoid(ag)
    d_au = dfin * (ag * sg)
    d_ag = dfin * au * (sg * (1.0 + ag * (1.0 - sg)))
    return d_ag, d_au


def _row_specs(ts, width):
    return pl.BlockSpec((ts, width), lambda i: (i, 0)), pl.BlockSpec((1, width), lambda i: (0, 0))


def _cast_into_full(place, shard, g, name):
    R, C = shard.shape
    tr = _tile(R, 256, 16)
    n_blk = R // tr
    out_map = (lambda i, p: (i, p[0])) if g.by_cols else (lambda i, p: (p[0] * n_blk + i, 0))

    def body(p_ref, a_ref, o_ref):
        o_ref[...] = a_ref[...].astype(o_ref.dtype)

    return pl.pallas_call(
        body, name=name,
        grid_spec=pltpu.PrefetchScalarGridSpec(
            num_scalar_prefetch=1, grid=(n_blk,),
            in_specs=[pl.BlockSpec((tr, C), lambda i, p: (i, 0))],
            out_specs=pl.BlockSpec((tr, C), out_map)),
        out_shape=jax.ShapeDtypeStruct(g.full, WIRE_DTYPE),
        compiler_params=_params("arbitrary"),
    )(place, shard)


def _norm_mod(x, g, scale, shift, name):
    S, D = x.shape
    ts = _tile(S, 256, 16)
    tile, vec = _row_specs(ts, D)

    def body(x_ref, g_ref, sc_ref, sh_ref, h_ref):
        xv = x_ref[...]
        r = lax.rsqrt(_mean1(xv * xv) + EPS)
        h_ref[...] = ((xv * r) * g_ref[...] * (1.0 + sc_ref[...]) + sh_ref[...]).astype(h_ref.dtype)

    return pl.pallas_call(body, name=name, grid=(S // ts,), in_specs=[tile, vec, vec, vec], out_specs=tile,
                          out_shape=jax.ShapeDtypeStruct((S, D), MXU_DTYPE), compiler_params=_params("parallel"))(x, g, scale, shift)


def _residual_norm_mod(x, attn, gate, g, scale, shift, name):
    S, D = x.shape
    ts = _tile(S, 256, 16)
    tile, vec = _row_specs(ts, D)

    def body(x_ref, a_ref, gate_ref, g_ref, sc_ref, sh_ref, x1_ref, h_ref):
        x1 = x_ref[...] + gate_ref[...] * a_ref[...]
        x1_ref[...] = x1
        r = lax.rsqrt(_mean1(x1 * x1) + EPS)
        h_ref[...] = ((x1 * r) * g_ref[...] * (1.0 + sc_ref[...]) + sh_ref[...]).astype(h_ref.dtype)

    return pl.pallas_call(body, name=name, grid=(S // ts,), in_specs=[tile, tile, vec, vec, vec, vec],
                          out_specs=[tile, tile],
                          out_shape=[jax.ShapeDtypeStruct((S, D), F32), jax.ShapeDtypeStruct((S, D), MXU_DTYPE)],
                          compiler_params=_params("parallel"))(x, attn, gate, g, scale, shift)


def _final_loss_bwd(x1, f, gate2, final_g, target, name):
    S, D = x1.shape
    ts = _tile(S, 256, 16)
    tile, vec = _row_specs(ts, D)
    loss_spec = pl.BlockSpec((1, LANE), lambda i: (0, 0))

    def body(x1_ref, f_ref, gate_ref, g_ref, t_ref, dx2_ref, df_ref, dgate_ref, dg_ref, loss_ref):
        @pl.when(pl.program_id(0) == 0)
        def _():
            dgate_ref[...] = jnp.zeros_like(dgate_ref)
            dg_ref[...] = jnp.zeros_like(dg_ref)
            loss_ref[...] = jnp.zeros_like(loss_ref)

        fv, gate, g = f_ref[...], gate_ref[...], g_ref[...]
        x2 = x1_ref[...] + gate * fv
        r = lax.rsqrt(_mean1(x2 * x2) + EPS)
        xn = x2 * r
        err = xn * g - t_ref[...]
        loss_ref[...] += jnp.broadcast_to(0.5 * _sum0(_mean1(err * err)), loss_ref.shape)
        dy = err * (1.0 / D)
        dg_ref[...] += _sum0(dy * xn)
        dxn = dy * g
        dx2 = r * (dxn - xn * _mean1(dxn * xn))
        dx2_ref[...] = dx2
        dgate_ref[...] += _sum0(dx2 * fv)
        df_ref[...] = (dx2 * gate).astype(df_ref.dtype)

    return pl.pallas_call(
        body, name=name, grid=(S // ts,), in_specs=[tile, tile, vec, vec, tile],
        out_specs=[tile, tile, vec, vec, loss_spec],
        out_shape=[jax.ShapeDtypeStruct((S, D), F32), jax.ShapeDtypeStruct((S, D), MXU_DTYPE),
                   jax.ShapeDtypeStruct((1, D), F32), jax.ShapeDtypeStruct((1, D), F32),
                   jax.ShapeDtypeStruct((1, LANE), F32)],
        compiler_params=_params("arbitrary"),
    )(x1, f, gate2, final_g, target)


def _norm_mod_bwd(dh, xin, dres, g, scale, name, branch=None, gate=None):
    S, D = xin.shape
    ts = _tile(S, 256, 16)
    tile, vec = _row_specs(ts, D)
    with_gate = branch is not None

    def body(*refs):
        if with_gate:
            dh_ref, x_ref, dres_ref, g_ref, sc_ref, br_ref, gate_ref, dx_ref, dshift_ref, dscale_ref, dg_ref, dgate_ref, dbr_ref = refs
            accs = (dshift_ref, dscale_ref, dg_ref, dgate_ref)
        else:
            dh_ref, x_ref, dres_ref, g_ref, sc_ref, dx_ref, dshift_ref, dscale_ref, dg_ref = refs
            accs = (dshift_ref, dscale_ref, dg_ref)

        @pl.when(pl.program_id(0) == 0)
        def _():
            for acc in accs:
                acc[...] = jnp.zeros_like(acc)

        dh_v, xv, g_v = dh_ref[...], x_ref[...], g_ref[...]
        one_sc = 1.0 + sc_ref[...]
        r = lax.rsqrt(_mean1(xv * xv) + EPS)
        xn = xv * r
        dshift_ref[...] += _sum0(dh_v)
        dscale_ref[...] += _sum0(dh_v * (xn * g_v))
        dg_ref[...] += _sum0(dh_v * one_sc * xn)
        dxn = dh_v * (g_v * one_sc)
        dx = dres_ref[...] + r * (dxn - xn * _mean1(dxn * xn))
        dx_ref[...] = dx
        if with_gate:
            dgate_ref[...] += _sum0(dx * br_ref[...])
            dbr_ref[...] = (dx * gate_ref[...]).astype(dbr_ref.dtype)

    ins = [dh, xin, dres, g, scale] + ([branch, gate] if with_gate else [])
    in_specs = [tile, tile, tile, vec, vec] + ([tile, vec] if with_gate else [])
    out_specs = [tile, vec, vec, vec] + ([vec, tile] if with_gate else [])
    out_shape = [jax.ShapeDtypeStruct((S, D), F32)] + [jax.ShapeDtypeStruct((1, D), F32)] * 3
    if with_gate:
        out_shape += [jax.ShapeDtypeStruct((1, D), F32), jax.ShapeDtypeStruct((S, D), MXU_DTYPE)]
    return pl.pallas_call(body, name=name, grid=(S // ts,), in_specs=in_specs, out_specs=out_specs,
                          out_shape=out_shape, compiler_params=_params("arbitrary"))(*ins)


def _causal_weights(ws_ref, wt_ref, n_g):
    row = lax.broadcasted_iota(jnp.int32, (LANE, LANE), 0)
    col = lax.broadcasted_iota(jnp.int32, (LANE, LANE), 1)
    for g in range(n_g):
        wt_ref[g] = jnp.where(col <= row, ws_ref[g], 0.0).astype(wt_ref.dtype)


def _group_layernorm(v):
    xc = v - _mean1(v)
    rstd = lax.rsqrt(_mean1(xc * xc) + EPS)
    return xc * rstd, rstd


def _gmlp_fwd(proj, v_gain, w_s, b_t, out_gain, n_g, name):
    S = proj.shape[0]
    GW = n_g * LANE

    def body(p_ref, vg_ref, ws_ref, bt_ref, og_ref, on_ref, wt_ref):
        @pl.when(pl.program_id(0) == 0)
        def _():
            _causal_weights(ws_ref, wt_ref, n_g)

        for g in range(n_g):
            cols = slice(g * LANE, (g + 1) * LANE)
            u = _gelu(p_ref[:, cols])
            v = _gelu(p_ref[:, GW + g * LANE:GW + (g + 1) * LANE])
            vhat, _ = _group_layernorm(v)
            vln = (vhat * vg_ref[:, cols]).astype(MXU_DTYPE)
            mixed = _dot(wt_ref[g], vln, NN) + bt_ref[:, g:g + 1]
            o = u * mixed
            r = lax.rsqrt(_mean1(o * o) + EPS)
            on_ref[:, cols] = (o * r * og_ref[:, cols]).astype(on_ref.dtype)

    return pl.pallas_call(
        body, name=name, grid=(S // LANE,),
        in_specs=[pl.BlockSpec((LANE, 2 * GW), lambda n: (n, 0)),
                  pl.BlockSpec((1, GW), lambda n: (0, 0)),
                  pl.BlockSpec((n_g, LANE, LANE), lambda n: (0, 0, 0)),
                  pl.BlockSpec((LANE, n_g), lambda n: (0, 0)),
                  pl.BlockSpec((1, GW), lambda n: (0, 0))],
        out_specs=pl.BlockSpec((LANE, GW), lambda n: (n, 0)),
        out_shape=jax.ShapeDtypeStruct((S, GW), MXU_DTYPE),
        scratch_shapes=[pltpu.VMEM((n_g, LANE, LANE), MXU_DTYPE)],
        compiler_params=_params("arbitrary"),
    )(proj, v_gain, w_s, b_t, out_gain)


def _gmlp_bwd(proj, d_on, v_gain, w_s, b_t, out_gain, n_g, name):
    S = proj.shape[0]
    GW = n_g * LANE

    def body(p_ref, dn_ref, vg_ref, ws_ref, bt_ref, og_ref, dp_ref, dws_ref, dbt_ref, dvg_ref, dog_ref, wt_ref):
        @pl.when(pl.program_id(0) == 0)
        def _():
            _causal_weights(ws_ref, wt_ref, n_g)
            dws_ref[...] = jnp.zeros_like(dws_ref)
            dbt_ref[...] = jnp.zeros_like(dbt_ref)
            dvg_ref[...] = jnp.zeros_like(dvg_ref)
            dog_ref[...] = jnp.zeros_like(dog_ref)

        row = lax.broadcasted_iota(jnp.int32, (LANE, LANE), 0)
        col = lax.broadcasted_iota(jnp.int32, (LANE, LANE), 1)
        for g in range(n_g):
            cols = slice(g * LANE, (g + 1) * LANE)
            vcols = slice(GW + g * LANE, GW + (g + 1) * LANE)
            pu, pv = p_ref[:, cols], p_ref[:, vcols]
            u, v = _gelu(pu), _gelu(pv)
            vhat, rstd = _group_layernorm(v)
            gain = vg_ref[:, cols]
            vln = (vhat * gain).astype(MXU_DTYPE)
            mixed = _dot(wt_ref[g], vln, NN) + bt_ref[:, g:g + 1]
            o = u * mixed
            r = lax.rsqrt(_mean1(o * o) + EPS)
            oh = o * r
            dn = dn_ref[:, cols]
            dog_ref[:, cols] += _sum0(dn * oh)
            dhn = dn * og_ref[:, cols]
            d_o = r * (dhn - oh * _mean1(dhn * oh))
            du = d_o * mixed
            dmix = d_o * u
            dbt_ref[:, g:g + 1] += jnp.sum(dmix, axis=1, keepdims=True)
            dmix_b = dmix.astype(MXU_DTYPE)
            dws_ref[g] += jnp.where(col <= row, _dot(dmix_b, vln, NT), 0.0)
            dvln = _dot(wt_ref[g], dmix_b, TN)
            dvg_ref[:, cols] += _sum0(dvln * vhat)
            dxh = dvln * gain
            dv = rstd * (dxh - _mean1(dxh) - vhat * _mean1(dxh * vhat))
            dp_ref[:, cols] = (du * _gelu_grad(pu)).astype(dp_ref.dtype)
            dp_ref[:, vcols] = (dv * _gelu_grad(pv)).astype(dp_ref.dtype)

    return pl.pallas_call(
        body, name=name, grid=(S // LANE,),
        in_specs=[pl.BlockSpec((LANE, 2 * GW), lambda n: (n, 0)),
                  pl.BlockSpec((LANE, GW), lambda n: (n, 0)),
                  pl.BlockSpec((1, GW), lambda n: (0, 0)),
                  pl.BlockSpec((n_g, LANE, LANE), lambda n: (0, 0, 0)),
                  pl.BlockSpec((LANE, n_g), lambda n: (0, 0)),
                  pl.BlockSpec((1, GW), lambda n: (0, 0))],
        out_specs=[pl.BlockSpec((LANE, 2 * GW), lambda n: (n, 0)),
                   pl.BlockSpec((n_g, LANE, LANE), lambda n: (0, 0, 0)),
                   pl.BlockSpec((LANE, n_g), lambda n: (0, 0)),
                   pl.BlockSpec((1, GW), lambda n: (0, 0)),
                   pl.BlockSpec((1, GW), lambda n: (0, 0))],
        out_shape=[jax.ShapeDtypeStruct((S, 2 * GW), MXU_DTYPE),
                   jax.ShapeDtypeStruct((n_g, LANE, LANE), F32),
                   jax.ShapeDtypeStruct((LANE, n_g), F32),
                   jax.ShapeDtypeStruct((1, GW), F32),
                   jax.ShapeDtypeStruct((1, GW), F32)],
        scratch_shapes=[pltpu.VMEM((n_g, LANE, LANE), MXU_DTYPE)],
        compiler_params=_params("arbitrary"),
    )(proj, d_on, v_gain, w_s, b_t, out_gain)


def _tri_sum(v, tri, exact=True):
    hi = v.astype(MXU_DTYPE)
    if not exact:
        return _dot(hi, tri, NN)
    lo = (v - hi.astype(F32)).astype(MXU_DTYPE)
    return _dot(hi, tri, NN) + _dot(lo, tri, NN)


def _log_sigmoids(z):
    sp = jnp.log(1.0 + jnp.exp(-jnp.abs(z)))
    return jnp.minimum(z, 0.0) - sp, jnp.minimum(-z, 0.0) - sp


def _rows(i, size):
    return pl.ds(pl.multiple_of(i * size, size), size)


SB_QUERY_TILE = 2048
SB_KEY_TILE = 256


def _sb_tiles(S):
    tq = _tile(S, SB_QUERY_TILE)
    tk = _tile(tq, SB_KEY_TILE)
    assert (tq // tk) % 2 == 0, "the key sweep takes two blocks a pass"
    return tq, tk, S // tq, tq // tk


def _triangle(n, keep):
    row = lax.broadcasted_iota(jnp.int32, (n, n), 0)
    col = lax.broadcasted_iota(jnp.int32, (n, n), 1)
    return jnp.where(keep(row, col), 1.0, 0.0).astype(MXU_DTYPE)


def _strictly_before(tq, tk, key_offset):
    row = lax.broadcasted_iota(jnp.int32, (tq, tk), 0)
    col = lax.broadcasted_iota(jnp.int32, (tq, tk), 1)
    return col + key_offset < row


def _sb_specs(S, n_g, n_h):
    base = 2 * n_g
    q_spec = pl.BlockSpec((S, LANE), lambda h: (0, base + h))
    k_spec = pl.BlockSpec((S, LANE), lambda h: (0, base + n_h + h))
    v_spec = pl.BlockSpec((S, LANE), lambda h: (0, base + 2 * n_h + h))
    gain_spec = pl.BlockSpec((1, LANE), lambda h: (0, n_g + h))
    head_spec = pl.BlockSpec((S, LANE), lambda h: (0, h))
    return q_spec, k_spec, v_spec, gain_spec, head_spec


def _sb_fwd(proj, out_gain, n_g, n_h, name):
    S = proj.shape[0]
    TQ, TK, NQ, KPQ = _sb_tiles(S)
    scale = LANE ** -0.5
    q_spec, k_spec, v_spec, gain_spec, head_spec = _sb_specs(S, n_g, n_h)

    def body(q_ref, k_ref, v_ref, og_ref, o_ref, on_ref, ls_ref, qb, kb, vb):
        qb[...] = q_ref[...].astype(MXU_DTYPE)
        kb[...] = k_ref[...].astype(MXU_DTYPE)
        vb[...] = v_ref[...].astype(MXU_DTYPE)
        after = _triangle(TK, lambda r, c: r > c)

        def block(qi, j, ctail, acc, key_offset):
            skip = key_offset or 0
            z = _dot(qi[skip:], kb[_rows(j, TK), :], NT) * scale
            lb, l1m = _log_sigmoids(z)
            if key_offset is not None:
                strict = _strictly_before(TQ - skip, TK, 0)
                l1m = jnp.where(strict, l1m, 0.0)
            a = jnp.exp(lb + ctail[skip:] + _tri_sum(l1m, after))
            if key_offset is not None:
                a = jnp.where(strict, a, 0.0)
            acc_new = acc[skip:] + _dot(a.astype(MXU_DTYPE), vb[_rows(j, TK), :], NN)
            ctail_new = ctail[skip:] + jnp.sum(l1m, axis=1, keepdims=True)
            if skip:
                ctail_new = jnp.concatenate([ctail[:skip], ctail_new], axis=0)
                acc_new = jnp.concatenate([acc[:skip], acc_new], axis=0)
            return ctail_new, acc_new

        def q_loop(i, carry):
            qi = qb[_rows(i, TQ), :]
            state = (jnp.zeros((TQ, 1), F32), jnp.zeros((TQ, LANE), F32))
            for d in reversed(range(KPQ)):
                state = block(qi, i * KPQ + d, state[0], state[1], d * TK)
            def pair(jj, st):
                st = block(qi, i * KPQ - 1 - 2 * jj, st[0], st[1], None)
                return block(qi, i * KPQ - 2 - 2 * jj, st[0], st[1], None)

            ctail, acc = lax.fori_loop(0, i * (KPQ // 2), pair, state)
            ls_ref[_rows(i, TQ), :] = jnp.broadcast_to(ctail, (TQ, LANE))
            o_ref[_rows(i, TQ), :] = acc
            r = lax.rsqrt(_mean1(acc * acc) + EPS)
            on_ref[_rows(i, TQ), :] = (acc * r * og_ref[...]).astype(on_ref.dtype)
            return carry

        lax.fori_loop(0, NQ, q_loop, 0)

    return pl.pallas_call(
        body, name=name, grid=(n_h,),
        in_specs=[q_spec, k_spec, v_spec, gain_spec],
        out_specs=[head_spec, head_spec, head_spec],
        out_shape=[jax.ShapeDtypeStruct((S, n_h * LANE), F32), jax.ShapeDtypeStruct((S, n_h * LANE), MXU_DTYPE),
                   jax.ShapeDtypeStruct((S, n_h * LANE), F32)],
        scratch_shapes=[pltpu.VMEM((S, LANE), MXU_DTYPE)] * 3,
        compiler_params=_params("parallel"),
    )(proj, proj, proj, out_gain)


def _sb_bwd(proj, o_sb, l_sum, d_on, out_gain, n_g, n_h, name):
    S = proj.shape[0]
    TQ, TK, NQ, KPQ = _sb_tiles(S)
    scale = LANE ** -0.5
    q_spec, k_spec, v_spec, gain_spec, head_spec = _sb_specs(S, n_g, n_h)
    dn_spec = pl.BlockSpec((S, LANE), lambda h: (0, n_g + h))
    dgain_spec = pl.BlockSpec((1, LANE), lambda h: (0, h))

    def body(q_ref, k_ref, v_ref, o_ref, ls_ref, dn_ref, og_ref, dq_ref, dk_ref, dv_ref, dog_ref,
             qb, kb, vb, dob, dk_acc, dv_acc):
        qb[...] = q_ref[...].astype(MXU_DTYPE)
        kb[...] = k_ref[...].astype(MXU_DTYPE)
        vb[...] = v_ref[...].astype(MXU_DTYPE)
        o, dn = o_ref[...], dn_ref[...]
        r = lax.rsqrt(_mean1(o * o) + EPS)
        oh = o * r
        dog_ref[...] = _sum0(dn * oh)
        dhn = dn * og_ref[...]
        dob[...] = (r * (dhn - oh * _mean1(dhn * oh))).astype(MXU_DTYPE)
        dk_acc[...] = jnp.zeros_like(dk_acc)
        dv_acc[...] = jnp.zeros_like(dv_acc)

        up_to = _triangle(TK, lambda r, c: r <= c)
        before = _triangle(TK, lambda r, c: r < c)

        def block(qi, doi, ltot, j, cl, cdl, dq, key_offset):
            skip = key_offset or 0
            q_in, do_in = qi[skip:], doi[skip:]
            kj, vj = kb[_rows(j, TK), :], vb[_rows(j, TK), :]
            z = _dot(q_in, kj, NT) * scale
            lb, l1m = _log_sigmoids(z)
            if key_offset is not None:
                strict = _strictly_before(TQ - skip, TK, 0)
                l1m = jnp.where(strict, l1m, 0.0)
            a = jnp.exp(lb + (ltot[skip:] - (cl[skip:] + _tri_sum(l1m, up_to))))
            if key_offset is not None:
                a = jnp.where(strict, a, 0.0)
            dl = _dot(do_in, vj, NT) * a
            d_l1m = cdl[skip:] + _tri_sum(dl, before, exact=False)
            beta = jnp.exp(lb)
            dz = dl * (1.0 - beta) - beta * d_l1m
            if key_offset is not None:
                dz = jnp.where(strict, dz, 0.0)
            dzs = (dz * scale).astype(MXU_DTYPE)
            dk_acc[_rows(j, TK), :] += _dot(dzs, q_in, TN)
            dv_acc[_rows(j, TK), :] += _dot(a.astype(MXU_DTYPE), do_in, TN)
            cl_new = cl[skip:] + jnp.sum(l1m, axis=1, keepdims=True)
            cdl_new = cdl[skip:] + jnp.sum(dl, axis=1, keepdims=True)
            dq_new = dq[skip:] + _dot(dzs, kj, NN)
            if skip:
                cl_new = jnp.concatenate([cl[:skip], cl_new], axis=0)
                cdl_new = jnp.concatenate([cdl[:skip], cdl_new], axis=0)
                dq_new = jnp.concatenate([dq[:skip], dq_new], axis=0)
            return cl_new, cdl_new, dq_new

        def q_loop(i, carry):
            qi, doi = qb[_rows(i, TQ), :], dob[_rows(i, TQ), :]
            ltot = ls_ref[_rows(i, TQ), :][:, :1]
            zero_col = jnp.zeros((TQ, 1), F32)
            def pair(jj, st):
                st = block(qi, doi, ltot, 2 * jj, st[0], st[1], st[2], None)
                return block(qi, doi, ltot, 2 * jj + 1, st[0], st[1], st[2], None)

            state = lax.fori_loop(0, i * (KPQ // 2), pair, (zero_col, zero_col, jnp.zeros((TQ, LANE), F32)))
            for d in range(KPQ):
                state = block(qi, doi, ltot, i * KPQ + d, state[0], state[1], state[2], d * TK)
            dq_ref[_rows(i, TQ), :] = state[2].astype(dq_ref.dtype)
            return carry

        lax.fori_loop(0, NQ, q_loop, 0)
        dk_ref[...] = dk_acc[...].astype(dk_ref.dtype)
        dv_ref[...] = dv_acc[...].astype(dv_ref.dtype)

    W = n_h * LANE
    return pl.pallas_call(
        body, name=name, grid=(n_h,),
        in_specs=[q_spec, k_spec, v_spec, head_spec, head_spec, dn_spec, gain_spec],
        out_specs=[head_spec, head_spec, head_spec, dgain_spec],
        out_shape=[jax.ShapeDtypeStruct((S, W), MXU_DTYPE)] * 3 + [jax.ShapeDtypeStruct((1, W), F32)],
        scratch_shapes=[pltpu.VMEM((S, LANE), MXU_DTYPE)] * 4 + [pltpu.VMEM((S, LANE), F32)] * 2,
        compiler_params=_params("parallel"),
    )(proj, proj, proj, o_sb, l_sum, d_on, out_gain)


def _mod_part(c_all, w_ada, b_ada_cols, name):
    B, K = c_all.shape
    N = w_ada.shape[1]
    tn = _tile(N, 512)

    def body(c_ref, w_ref, b_ref, o_ref):
        cv = c_ref[...]
        ca = (cv * jax.nn.sigmoid(cv)).astype(MXU_DTYPE)
        o_ref[...] = _dot(ca, w_ref[...].astype(MXU_DTYPE), NN) + b_ref[...]

    return pl.pallas_call(
        body, name=name, grid=(N // tn,),
        in_specs=[pl.BlockSpec((B, K), lambda j: (0, 0)), pl.BlockSpec((K, tn), lambda j: (0, j)),
                  pl.BlockSpec((1, tn), lambda j: (0, j))],
        out_specs=pl.BlockSpec((B, tn), lambda j: (0, j)),
        out_shape=jax.ShapeDtypeStruct((B, N), F32), compiler_params=_params("parallel"))(c_all, w_ada, b_ada_cols)


def _adamw_math(w, g, m, v):
    m = ADAM_B1 * m + (1.0 - ADAM_B1) * g
    v = ADAM_B2 * v + (1.0 - ADAM_B2) * (g * g)
    m_hat = m / (1.0 - ADAM_B1 ** ADAM_STEP)
    v_hat = v / (1.0 - ADAM_B2 ** ADAM_STEP)
    delta = -ADAM_LR * (m_hat / (jnp.sqrt(v_hat) + ADAM_EPS) + ADAM_WD * w)
    return delta, m, v


def _adamw(w, g, m, v, name):
    R, C = w.shape
    tr = _tile(R, max(8, (1 << 19) // C), 8)
    spec = pl.BlockSpec((tr, C), lambda i: (i, 0))

    def body(w_ref, g_ref, m_ref, v_ref, go_ref, d_ref, mo_ref, vo_ref):
        g = g_ref[...]
        go_ref[...] = g
        d_ref[...], mo_ref[...], vo_ref[...] = _adamw_math(w_ref[...], g, m_ref[...], v_ref[...])

    return pl.pallas_call(body, name=name, grid=(R // tr,), in_specs=[spec] * 4, out_specs=[spec] * 4,
                          out_shape=[jax.ShapeDtypeStruct((R, C), F32)] * 4, compiler_params=_params("parallel"))(w, g, m, v)


def _adamw_ada(c_all, dmod_cols, w, m, v, name):
    K, N = w.shape
    B = c_all.shape[0]
    tk, tn = _tile(K, 512), _tile(N, 1024)
    spec = pl.BlockSpec((tk, tn), lambda i, j: (i, j))

    def body(c_ref, dm_ref, w_ref, m_ref, v_ref, g_ref, d_ref, mo_ref, vo_ref):
        cv = c_ref[...]
        ca = (cv * jax.nn.sigmoid(cv)).astype(MXU_DTYPE)
        g = _dot(ca, dm_ref[...].astype(MXU_DTYPE), TN)
        g_ref[...] = g
        d_ref[...], mo_ref[...], vo_ref[...] = _adamw_math(w_ref[...], g, m_ref[...], v_ref[...])

    return pl.pallas_call(
        body, name=name, grid=(K // tk, N // tn),
        in_specs=[pl.BlockSpec((B, tk), lambda i, j: (0, i)), pl.BlockSpec((B, tn), lambda i, j: (0, j)), spec, spec, spec],
        out_specs=[spec] * 4, out_shape=[jax.ShapeDtypeStruct((K, N), F32)] * 4,
        compiler_params=_params("parallel", "parallel"))(c_all, dmod_cols, w, m, v)


def _sum_devices(gathered, n_dev, name):
    R = gathered.shape[0] // n_dev
    C = gathered.shape[1]
    tr = _tile(R, 512, 8)
    n_blk = R // tr

    def body(*refs):
        acc = refs[0][...]
        for r in refs[1:n_dev]:
            acc = acc + r[...]
        refs[n_dev][...] = acc

    in_specs = [pl.BlockSpec((tr, C), functools.partial(lambda i, d: (d * n_blk + i, 0), d=d)) for d in range(n_dev)]
    return pl.pallas_call(body, name=name, grid=(n_blk,), in_specs=in_specs,
                          out_specs=pl.BlockSpec((tr, C), lambda i: (i, 0)),
                          out_shape=jax.ShapeDtypeStruct((R, C), F32), compiler_params=_params("parallel"))(*([gathered] * n_dev))


def _place():
    x, y, c = lax.axis_index("x"), lax.axis_index("y"), lax.axis_index("c")
    return x, y, c


def _allgather8(blk, name):
    m_per, n = blk.shape

    def body(x_ref, out_ref, send_sems, recv_sems, local_sem):
        x, y, c = _place()
        me, sibling = (x, y, c), (x, y, 1 - c)
        chips = [(1 - x, y), (x, 1 - y), (1 - x, 1 - y)]

        def rows(px, py, pc):
            return out_ref.at[pl.ds((4 * px + 2 * py + pc) * m_per, m_per), :]

        def copy(k, block, to, src=None):
            return pltpu.make_async_remote_copy(
                src_ref=rows(*block) if src is None else src, dst_ref=rows(*block),
                send_sem=send_sems.at[k], recv_sem=recv_sems.at[k], device_id=to, device_id_type=MESH)

        mine = pltpu.make_async_copy(x_ref, rows(*me), local_sem)
        mine.start()
        first = [copy(0, me, sibling, src=x_ref)]
        first += [copy(1 + j, me, (*chip, c), src=x_ref) for j, chip in enumerate(chips)]
        for cp in first:
            cp.start()
        passed = [copy(4 + j, (*chip, c), sibling) for j, chip in enumerate(chips)]
        for j, chip in enumerate(chips):
            copy(1 + j, (*chip, c), me).wait_recv()
            passed[j].start()
        copy(0, sibling, me).wait_recv()
        for j, chip in enumerate(chips):
            copy(4 + j, (*chip, 1 - c), me).wait_recv()
        for cp in first + passed:
            cp.wait_send()
        mine.wait()

    return pl.pallas_call(
        body, name=name,
        out_shape=jax.ShapeDtypeStruct((8 * m_per, n), blk.dtype),
        in_specs=[pl.BlockSpec(memory_space=pltpu.VMEM)],
        out_specs=pl.BlockSpec(memory_space=pltpu.VMEM),
        scratch_shapes=[pltpu.SemaphoreType.DMA((7,)), pltpu.SemaphoreType.DMA((7,)), pltpu.SemaphoreType.DMA],
        compiler_params=pltpu.CompilerParams(vmem_limit_bytes=V7X_VMEM_LIMIT),
    )(blk)


class _Sharded:
    def __init__(self, shard_shape, by_cols):
        r, c = shard_shape
        self.by_cols = by_cols
        self.full = (r, N_CHIPS * c) if by_cols else (N_CHIPS * r, c)
        self.shard = (r, c)
        self.half_rows = r // 2
        self.half = (r // 2, c)

    def shard_of(self, ref, k):
        r, c = self.shard
        return ref.at[:, pl.ds(k * c, c)] if self.by_cols else ref.at[pl.ds(k * r, r), :]

    def half_of(self, ref, k, hc):
        r, c = self.shard
        h = self.half_rows
        if self.by_cols:
            return ref.at[pl.ds(hc * h, h), pl.ds(k * c, c)]
        return ref.at[pl.ds(k * r + hc * h, h), :]

    def chunk_of(self, ref, k, hc, ch, n):
        r, c = self.shard
        h = self.half_rows
        q = h // n
        if self.by_cols:
            return ref.at[pl.ds(hc * h + ch * q, q), pl.ds(k * c, c)]
        return ref.at[pl.ds(k * r + hc * h + ch * q, q), :]

    def half_of_shard(self, ref, hc):
        return ref.at[pl.ds(hc * self.half_rows, self.half_rows), :]

    def part_of_halves(self, ref, k):
        r, c = self.shard
        h = self.half_rows
        return ref.at[:, pl.ds(k * c, c)] if self.by_cols else ref.at[pl.ds(k * h, h), :]


def _on_each_place(x, y, c, fn, by_chip=True, by_core=True):
    q = 2 * x + y
    for k in range(N_CHIPS if by_chip else 1):
        for cc in range(2 if by_core else 1):
            cond = None
            if by_chip:
                cond = q == k
            if by_core:
                cond = (c == cc) if cond is None else jnp.logical_and(cond, c == cc)
            pl.when(cond)(functools.partial(fn, k, cc))


def _chip_id(k, c):
    return (k // 2, k % 2, c)


def _handshake(peers):
    barrier = pltpu.get_barrier_semaphore()
    for peer in peers:
        pl.semaphore_signal(barrier, inc=1, device_id=peer, device_id_type=MESH)
    pl.semaphore_wait(barrier, len(peers))


def _on_sequencer(body, inputs, out_structs, n_copies, peers_of, name, collective_id, return_inputs=False):
    in_refs = [jax.new_ref(a, memory_space=pltpu.MemorySpace.HBM) for a in inputs]
    out_refs = [jax.empty_ref(s, memory_space=pltpu.MemorySpace.HBM) for s in out_structs]

    @pl.kernel(mesh=plsc.ScalarSubcoreMesh(axis_name="sequencer", num_cores=1), name=name,
               scratch_types=(pltpu.SemaphoreType.DMA((n_copies,)), pltpu.SemaphoreType.DMA((n_copies,))),
               compiler_params=pltpu.CompilerParams(collective_id=collective_id))
    def launch(send_sems, recv_sems):
        x, y, c = _place()
        _handshake(peers_of(x, y, c))
        body(in_refs, out_refs, send_sems, recv_sems, x, y, c)

    launch()
    return [r[...] for r in (in_refs if return_inputs else out_refs)]


def _sibling(x, y, c):
    return [(x, y, 1 - c)]


def _same_core_of_other_chips(x, y, c):
    return [(1 - x, y, c), (x, 1 - y, c), (1 - x, 1 - y, c)]


GATHER_CHUNKS = 4
GATHER_COPIES = 6 * GATHER_CHUNKS


def _allgather8_on_sequencer(blk, name, collective_id):
    m_per, n = blk.shape
    x, y, c = _place()
    placed = lax.dynamic_update_slice(jnp.zeros((8 * m_per, n), blk.dtype), blk, ((4 * x + 2 * y + c) * m_per, 0))

    def body(refs, _, send_sems, recv_sems, x, y, c):
        out_ref, = refs

        def at_place(k, cc):
            def rows(kk, pc):
                return out_ref.at[pl.ds((2 * kk + pc) * m_per, m_per), :]

            def copy(slot, block, to):
                return pltpu.make_async_remote_copy(src_ref=rows(*block), dst_ref=rows(*block), send_sem=send_sems.at[slot],
                                                    recv_sem=recv_sems.at[slot], device_id=to, device_id_type=MESH)

            others = [k ^ flip for flip in FLIPS]
            sends = [copy(0, (k, cc), _chip_id(k, 1 - cc))] + [copy(1 + j, (k, cc), _chip_id(kk, cc)) for j, kk in enumerate(others)]
            for cp in sends:
                cp.start()
            for j, kk in enumerate(others):
                copy(1 + j, (kk, cc), _chip_id(k, cc)).wait_recv()
                cp = copy(4 + j, (kk, cc), _chip_id(k, 1 - cc))
                cp.start()
                sends.append(cp)
            copy(0, (k, 1 - cc), _chip_id(k, cc)).wait_recv()
            for j, kk in enumerate(others):
                copy(4 + j, (kk, 1 - cc), _chip_id(k, cc)).wait_recv()
            for cp in sends:
                cp.wait_send()

        _on_each_place(x, y, c, at_place)

    def peers(x, y, c):
        return _sibling(x, y, c) + _same_core_of_other_chips(x, y, c)

    return _on_sequencer(body, [placed], [], 7, peers, name, collective_id, return_inputs=True)[0]


def _gather_weights(fulls, geoms, name, collective_id):
    n_w = len(fulls)
    n_ch, n_relay = GATHER_CHUNKS, GATHER_CHUNKS // 2
    f_refs = [jax.new_ref(f, memory_space=pltpu.MemorySpace.HBM) for f in fulls]
    FLIP_X, FLIP_Y, FLIP_BOTH = FLIPS
    TO_X, TO_Y, RELAY_TO_Y, RELAY_TO_X, ON_X, ON_Y, ON_DIAG = 0, n_ch, 2 * n_ch, 2 * n_ch + n_relay, 3 * n_ch, 4 * n_ch, 5 * n_ch

    @pl.kernel(mesh=plsc.ScalarSubcoreMesh(axis_name="sequencer", num_cores=1), name=name,
               scratch_types=(pltpu.SemaphoreType.DMA((GATHER_COPIES * n_w,)), pltpu.SemaphoreType.DMA((GATHER_COPIES * n_w,))),
               compiler_params=pltpu.CompilerParams(collective_id=collective_id))
    def launch(send_sems, recv_sems):
        x, y, c = _place()
        _handshake([(x, y, 1 - c), (1 - x, y, c), (x, 1 - y, c)])

        def at_place(k, cc):
            kx, ky, kd = k ^ FLIP_X, k ^ FLIP_Y, k ^ FLIP_BOTH
            me, sibling = _chip_id(k, cc), _chip_id(k, 1 - cc)
            started = []

            def copy(i, slot, src, dst, to, start=True):
                cp = pltpu.make_async_remote_copy(src_ref=src, dst_ref=dst, send_sem=send_sems.at[GATHER_COPIES * i + slot],
                                                  recv_sem=recv_sems.at[GATHER_COPIES * i + slot], device_id=to, device_id_type=MESH)
                if start:
                    cp.start()
                    started.append(cp)
                return cp

            def pass_on(i, slot, ref, to):
                copy(i, slot, ref, ref, to)

            def landed(i, slot, ref):
                copy(i, slot, ref, ref, me, start=False).wait_recv()

            y_order = [(n_relay + s) % n_ch for s in range(n_ch)]
            for i, (g, f_ref) in enumerate(zip(geoms, f_refs)):
                for s in range(n_ch):
                    pass_on(i, TO_X + s, g.chunk_of(f_ref, k, cc, s, n_ch), _chip_id(kx, cc))
                    pass_on(i, TO_Y + y_order[s], g.chunk_of(f_ref, k, cc, y_order[s], n_ch), _chip_id(ky, cc))
            for i, (g, f_ref) in enumerate(zip(geoms, f_refs)):
                for s in range(n_ch):
                    from_x = g.chunk_of(f_ref, kx, cc, s, n_ch)
                    landed(i, TO_X + s, from_x)
                    if s < n_relay:
                        pass_on(i, RELAY_TO_Y + s, from_x, _chip_id(ky, cc))
                    pass_on(i, ON_X + s, from_x, sibling)
                    ch = y_order[s]
                    from_y = g.chunk_of(f_ref, ky, cc, ch, n_ch)
                    landed(i, TO_Y + ch, from_y)
                    if ch >= n_relay:
                        pass_on(i, RELAY_TO_X + ch - n_relay, from_y, _chip_id(kx, cc))
                    pass_on(i, ON_Y + ch, from_y, sibling)
                for r in range(n_relay):
                    via_y = g.chunk_of(f_ref, kd, cc, r, n_ch)
                    landed(i, RELAY_TO_Y + r, via_y)
                    pass_on(i, ON_DIAG + r, via_y, sibling)
                    via_x = g.chunk_of(f_ref, kd, cc, n_relay + r, n_ch)
                    landed(i, RELAY_TO_X + r, via_x)
                    pass_on(i, ON_DIAG + n_relay + r, via_x, sibling)
            for i, (g, f_ref) in enumerate(zip(geoms, f_refs)):
                for slot, kk in ((ON_X, kx), (ON_Y, ky), (ON_DIAG, kd)):
                    for ch in range(n_ch):
                        landed(i, slot + ch, g.chunk_of(f_ref, kk, 1 - cc, ch, n_ch))
            for cp in started:
                cp.wait_send()

        _on_each_place(x, y, c, at_place)

    launch()
    return [f_ref[...] for f_ref in f_refs]


def _swap_core_halves(grads, geoms, name, collective_id):
    n_cp = sum(1 if g.by_cols else N_CHIPS for g in geoms)

    def body(g_refs, t_refs, send_sems, recv_sems, x, y, c):

        def at_place(_, cc):
            def pairs(hc):
                out = []
                for g, g_ref, t_ref in zip(geoms, g_refs, t_refs):
                    if g.by_cols:
                        out.append((g_ref.at[pl.ds(hc * g.half_rows, g.half_rows), :], t_ref))
                    else:
                        out += [(g.half_of(g_ref, k, hc), g.part_of_halves(t_ref, k)) for k in range(N_CHIPS)]
                return out

            sends = [pltpu.make_async_remote_copy(src_ref=src, dst_ref=dst, send_sem=send_sems.at[n],
                                                  recv_sem=recv_sems.at[n], device_id=(x, y, 1 - cc), device_id_type=MESH)
                     for n, (src, dst) in enumerate(pairs(1 - cc))]
            for cp in sends:
                cp.start()
            for n, (src, dst) in enumerate(pairs(cc)):
                pltpu.make_async_remote_copy(src_ref=src, dst_ref=dst, send_sem=send_sems.at[n], recv_sem=recv_sems.at[n],
                                             device_id=(x, y, cc), device_id_type=MESH).wait_recv()
            for cp in sends:
                cp.wait_send()

        _on_each_place(x, y, c, at_place, by_chip=False)

    return _on_sequencer(body, grads, [jax.ShapeDtypeStruct((g.full[0] // 2, g.full[1]), F32) for g in geoms],
                         n_cp, _sibling, name, collective_id)


def _send_to_sibling(buffers, name, collective_id):
    def body(src_refs, dst_refs, send_sems, recv_sems, x, y, c):
        def copy(i):
            return pltpu.make_async_remote_copy(src_ref=src_refs[i], dst_ref=dst_refs[i], send_sem=send_sems.at[i],
                                                recv_sem=recv_sems.at[i], device_id=(x, y, 1 - c), device_id_type=MESH)

        for i in range(len(buffers)):
            copy(i).start()
        for i in range(len(buffers)):
            copy(i).wait()

    return _on_sequencer(body, buffers, [jax.ShapeDtypeStruct(t.shape, t.dtype) for t in buffers], len(buffers),
                         _sibling, name, collective_id)


def _scatter_chip_sums(sums, geoms, name, collective_id):
    def body(s_refs, r_refs, send_sems, recv_sems, x, y, c):

        def at_place(k, _):
            sends = []
            for i, (g, s_ref, r_ref) in enumerate(zip(geoms, s_refs, r_refs)):
                for j, flip in enumerate(FLIPS):
                    kk = k ^ flip
                    cp = pltpu.make_async_remote_copy(
                        src_ref=g.part_of_halves(s_ref, kk), dst_ref=r_ref.at[j], send_sem=send_sems.at[3 * i + j],
                        recv_sem=recv_sems.at[3 * i + j], device_id=(kk // 2, kk % 2, c), device_id_type=MESH)
                    cp.start()
                    sends.append(cp)
            for i, (g, s_ref, r_ref) in enumerate(zip(geoms, s_refs, r_refs)):
                for j in range(len(FLIPS)):
                    pltpu.make_async_remote_copy(
                        src_ref=g.part_of_halves(s_ref, k), dst_ref=r_ref.at[j], send_sem=send_sems.at[3 * i + j],
                        recv_sem=recv_sems.at[3 * i + j], device_id=(x, y, c), device_id_type=MESH).wait_recv()
            for cp in sends:
                cp.wait_send()

        _on_each_place(x, y, c, at_place, by_core=False)

    return _on_sequencer(body, sums, [jax.ShapeDtypeStruct((len(FLIPS),) + g.half, WIRE_DTYPE) for g in geoms],
                         len(FLIPS) * len(sums), _same_core_of_other_chips, name, collective_id)


def _share_reduced_halves(reduced, geoms, name, collective_id):
    def body(out_refs, _, send_sems, recv_sems, x, y, c):

        def at_place(_, cc):
            sends = []
            for i, (g, ref) in enumerate(zip(geoms, out_refs)):
                mine = g.half_of_shard(ref, cc)
                cp = pltpu.make_async_remote_copy(src_ref=mine, dst_ref=mine, send_sem=send_sems.at[i],
                                                  recv_sem=recv_sems.at[i], device_id=(x, y, 1 - cc), device_id_type=MESH)
                cp.start()
                sends.append(cp)
            for i, (g, ref) in enumerate(zip(geoms, out_refs)):
                theirs = g.half_of_shard(ref, 1 - cc)
                pltpu.make_async_remote_copy(src_ref=theirs, dst_ref=theirs, send_sem=send_sems.at[i],
                                             recv_sem=recv_sems.at[i], device_id=(x, y, cc), device_id_type=MESH).wait_recv()
            for cp in sends:
                cp.wait_send()

        _on_each_place(x, y, c, at_place, by_chip=False)

    return _on_sequencer(body, reduced, [], len(reduced), _sibling, name, collective_id, return_inputs=True)


def _chip_sum(place, grad, theirs, g, name):
    RH, C = theirs.shape
    h = g.half_rows
    tr = _tile(h, 256, 16)
    tc = _tile(C, 2048)
    per_half = h // tr

    if g.by_cols:
        grad_map = lambda i, j, p: (p[1] * per_half + i, j)
    else:
        grad_map = lambda i, j, p: ((i // per_half) * 2 * per_half + p[1] * per_half + i % per_half, j)

    def body(p_ref, a_ref, b_ref, f_ref, o_ref):
        total = a_ref[...] + b_ref[...]
        f_ref[...] = total
        o_ref[...] = total.astype(o_ref.dtype)

    return pl.pallas_call(
        body, name=name,
        grid_spec=pltpu.PrefetchScalarGridSpec(
            num_scalar_prefetch=1, grid=(RH // tr, C // tc),
            in_specs=[pl.BlockSpec((tr, tc), grad_map), pl.BlockSpec((tr, tc), lambda i, j, p: (i, j))],
            out_specs=[pl.BlockSpec((tr, tc), lambda i, j, p: (i, j))] * 2),
        out_shape=[jax.ShapeDtypeStruct((RH, C), F32), jax.ShapeDtypeStruct((RH, C), WIRE_DTYPE)],
        compiler_params=_params("parallel", "parallel"),
    )(place, grad, theirs)


def _dw_half(place, a, b, g, mine, name, add=None):
    K, R = a.shape
    C = b.shape[1]
    h = g.half_rows
    tm, tn = _tile(h, 1024, 16), _tile(C, 512)
    per_half = h // tm
    n_i = (R // 2) // tm

    def a_map(i, j, p):
        hc = p[1] if mine else 1 - p[1]
        if g.by_cols:
            return 0, hc * n_i + i
        return 0, (i // per_half) * 2 * per_half + hc * per_half + i % per_half

    mn_spec = pl.BlockSpec((tm, tn), lambda i, j, p: (i, j))

    def body(p_ref, a_ref, b_ref, *rest):
        acc = _dot(a_ref[...], b_ref[...], TN)
        if add is None:
            rest[0][...] = acc
        else:
            total = acc + rest[0][...]
            rest[1][...] = total
            rest[2][...] = total.astype(rest[2].dtype)

    out_shape = [jax.ShapeDtypeStruct((R // 2, C), F32)] + ([] if add is None else [jax.ShapeDtypeStruct((R // 2, C), WIRE_DTYPE)])
    return pl.pallas_call(
        body, name=name,
        grid_spec=pltpu.PrefetchScalarGridSpec(
            num_scalar_prefetch=1, grid=(n_i, C // tn),
            in_specs=[pl.BlockSpec((K, tm), a_map), pl.BlockSpec((K, tn), lambda i, j, p: (0, j))] + ([] if add is None else [mn_spec]),
            out_specs=[mn_spec] * len(out_shape)),
        out_shape=out_shape,
        compiler_params=_params("parallel", "arbitrary"),
    )(place, a, b, *([] if add is None else [add]))


def _reduce_half(place, sums, others, g, name):
    h, tc = g.half
    tr = _tile(h, 256, 16)
    per_half = h // tr
    sums_map = (lambda i, p: (i, p[0])) if g.by_cols else (lambda i, p: (p[0] * per_half + i, 0))

    def body(p_ref, s_ref, o0_ref, o1_ref, o2_ref, out_ref):
        acc = s_ref[...]
        for o_ref in (o0_ref, o1_ref, o2_ref):
            acc = acc + o_ref[...].astype(F32)
        out_ref[...] = acc

    other_specs = [pl.BlockSpec((None, tr, tc), functools.partial(lambda i, p, j: (j, i, 0), j=j)) for j in range(len(FLIPS))]
    return pl.pallas_call(
        body, name=name,
        grid_spec=pltpu.PrefetchScalarGridSpec(
            num_scalar_prefetch=1, grid=(per_half,),
            in_specs=[pl.BlockSpec((tr, tc), sums_map)] + other_specs,
            out_specs=pl.BlockSpec((tr, tc), lambda i, p: (p[1] * per_half + i, 0))),
        out_shape=jax.ShapeDtypeStruct(g.shard, F32),
        compiler_params=_params("arbitrary"),
    )(place, sums, others, others, others)


SLAB_ROW_UNIT = 256
SMALL = ("b_ada", "norm1_g", "v_norm_g", "w_spatial", "b_spatial", "out_norm_g", "norm2_g", "final_g")
BIG = ("w_in", "w_out", "w_gate", "w_up", "w_down")
BY_COLS = {"w_in": True, "w_out": False, "w_gate": True, "w_up": True, "w_down": False}
ORDER = ("w_ada", "b_ada", "norm1_g", "w_in", "v_norm_g", "w_spatial", "b_spatial", "out_norm_g", "w_out",
         "norm2_g", "w_gate", "w_up", "w_down", "final_g")


def _pack(parts):
    return jnp.concatenate([parts[n].reshape(-1) for n in SMALL]).reshape(-1, LANE)


def _adamw_small(w, g, m, v, shapes, name):
    R = w.shape[0]
    slab_spec = pl.BlockSpec((R, LANE), lambda: (0, 0))
    out_shapes = [shapes[n] if len(shapes[n]) > 1 else (1,) + tuple(shapes[n]) for n in SMALL]

    def body(w_ref, g_ref, m_ref, v_ref, *out_refs):
        gv = g_ref[...]
        results = (gv,) + _adamw_math(w_ref[...], gv, m_ref[...], v_ref[...])
        for kind, val in enumerate(results):
            at = 0
            for i, shp in enumerate(out_shapes):
                o_ref = out_refs[kind * len(SMALL) + i]
                n_rows = math.prod(shp) // LANE
                if len(shp) == 2:
                    for r in range(n_rows):
                        o_ref[:, r * LANE:(r + 1) * LANE] = val[at + r:at + r + 1, :]
                else:
                    o_ref[0] = val[at:at + n_rows, :].reshape(shp[1:])
                at += n_rows

    outs = pl.pallas_call(
        body, name=name, in_specs=[slab_spec] * 4,
        out_specs=[pl.BlockSpec(shp, functools.partial(lambda nd: (0,) * nd, len(shp))) for shp in out_shapes] * 4,
        out_shape=[jax.ShapeDtypeStruct(shp, F32) for shp in out_shapes] * 4,
        compiler_params=_params(),
    )(w, g, m, v)
    dicts = []
    for kind in range(4):
        part = outs[kind * len(SMALL):(kind + 1) * len(SMALL)]
        dicts.append({n: a.reshape(shapes[n]) for n, a in zip(SMALL, part)})
    return dicts


def kernel(x, c, w_ada, b_ada, norm1_g, w_in, v_norm_g, w_spatial, b_spatial, out_norm_g, w_out, norm2_g, w_gate, w_up, w_down, final_g, loss_target, m_w_ada, m_b_ada, m_norm1_g, m_w_in, m_v_norm_g, m_w_spatial, m_b_spatial, m_out_norm_g, m_w_out, m_norm2_g, m_w_gate, m_w_up, m_w_down, m_final_g, v_w_ada, v_b_ada, v_norm1_g, v_w_in, v_v_norm_g, v_w_spatial, v_b_spatial, v_out_norm_g, v_w_out, v_norm2_g, v_w_gate, v_w_up, v_w_down, v_final_g):
    weights = dict(w_ada=w_ada, b_ada=b_ada, norm1_g=norm1_g, w_in=w_in, v_norm_g=v_norm_g, w_spatial=w_spatial,
                   b_spatial=b_spatial, out_norm_g=out_norm_g, w_out=w_out, norm2_g=norm2_g, w_gate=w_gate, w_up=w_up,
                   w_down=w_down, final_g=final_g)
    m_in = dict(w_ada=m_w_ada, b_ada=m_b_ada, norm1_g=m_norm1_g, w_in=m_w_in, v_norm_g=m_v_norm_g, w_spatial=m_w_spatial,
                b_spatial=m_b_spatial, out_norm_g=m_out_norm_g, w_out=m_w_out, norm2_g=m_norm2_g, w_gate=m_w_gate,
                w_up=m_w_up, w_down=m_w_down, final_g=m_final_g)
    v_in = dict(w_ada=v_w_ada, b_ada=v_b_ada, norm1_g=v_norm1_g, w_in=v_w_in, v_norm_g=v_v_norm_g, w_spatial=v_w_spatial,
                b_spatial=v_b_spatial, out_norm_g=v_out_norm_g, w_out=v_w_out, norm2_g=v_norm2_g, w_gate=v_w_gate,
                w_up=v_w_up, w_down=v_w_down, final_g=v_final_g)

    S, D = x.shape[1], x.shape[2]
    n_g = v_norm_g.shape[-1] // LANE
    n_h = (D - n_g * LANE) // LANE
    GW = n_g * LANE
    xi, yi, ci = _place()
    chip = 2 * xi + yi
    me = 4 * xi + 2 * yi + ci
    place = jnp.stack([chip, ci]).astype(jnp.int32)

    xs, target = x[0], loss_target[0]
    geoms = [_Sharded(weights[n].shape[1:], BY_COLS[n]) for n in BIG]

    full = {}
    for i, group in enumerate((("w_in",), ("w_out",), ("w_gate", "w_up"), ("w_down",))):
        gg = [geoms[BIG.index(n)] for n in group]
        own = [_cast_into_full(place, weights[n][0], g, "cast_" + n) for n, g in zip(group, gg)]
        gathered = _gather_weights(own, gg, "gather_" + "_".join(group), 1 + i)
        full.update(zip(group, gathered))

    c_pad = jnp.concatenate([c, jnp.zeros((7, D), F32)], axis=0)
    c_all = _allgather8(c_pad, "gather_c")[::8]
    n_ada = w_ada.shape[2]
    b_cols = lax.dynamic_slice(b_ada, (0, chip * n_ada), (1, n_ada))
    mod_parts = _allgather8(_mod_part(c_all, w_ada[0], b_cols, "mod_part"), "gather_mod")
    mod_all = mod_parts.reshape(N_CHIPS, 2, 8, n_ada)[:, 0].transpose(1, 0, 2).reshape(8, N_CHIPS * n_ada)
    mod = lax.dynamic_slice(mod_all, (me, 0), (1, 6 * D))
    shift1, scale1, gate1, shift2, scale2, gate2 = [mod[:, i * D:(i + 1) * D] for i in range(6)]

    b_t = b_spatial[0].T
    h1 = _norm_mod(xs, norm1_g, scale1, shift1, "norm1")
    proj, = _mm("nn", h1, full["w_in"], [F32], "proj")
    on_gm = _gmlp_fwd(proj, v_norm_g, w_spatial[0], b_t, out_norm_g, n_g, "gmlp_fwd")
    o_sb, on_sb, l_sum = _sb_fwd(proj, out_norm_g, n_g, n_h, "sb_fwd")
    o_n = jnp.concatenate([on_gm, on_sb], axis=1)
    attn, = _mm("nn", o_n, full["w_out"], [F32], "attn_out")
    x1, h2 = _residual_norm_mod(xs, attn, gate1, norm2_g, scale2, shift2, "norm2")
    a_g, a_u, f_in = _gate_up(h2, full["w_gate"], full["w_up"], "gate_up")
    f, = _mm("nn", f_in, full["w_down"], [F32], "down", tm=1024)
    dx2, df, d_gate2, d_final_g, loss_part = _final_loss_bwd(x1, f, gate2, final_g.reshape(1, D), target, "final")

    geom_of = dict(zip(BIG, geoms))
    grad_out, delta, new_m, new_v = {}, {}, {}, {}

    def theirs_first(group, operands, collective_id, after=None):
        outs = []
        for n, (a_op, b_op) in zip(group, operands):
            outs.append(_dw_half(place, a_op, b_op if after is None else _then(after, b_op), geom_of[n], False, "d_" + n + "_theirs")[0])
            after = outs[-1]
        return outs, _send_to_sibling(outs, "swap_" + "_".join(group), collective_id)

    def chip_sums(group, operands, theirs, after):
        f32s, wires = [], []
        for n, (a_op, b_op), t in zip(group, operands, theirs):
            sf, sw = _dw_half(place, a_op, b_op, geom_of[n], True, "d_" + n + "_mine", add=_then(after, t))
            f32s.append(sf)
            wires.append(sw)
            after = sw
        return f32s, wires

    def scatter(group, sums, collective_id):
        return _scatter_chip_sums(sums, [geom_of[n] for n in group], "scatter_" + "_".join(group), collective_id)

    def reduce_halves(group, sums, others, after):
        return [_reduce_half(place, sf, _then(after, o), geom_of[n], "reduce_" + n) for n, sf, o in zip(group, sums, others)]

    def share(group, halves, collective_id):
        return _share_reduced_halves(halves, [geom_of[n] for n in group], "share_" + "_".join(group), collective_id)

    def adamw(group, reduced, after):
        for n, r in zip(group, reduced):
            go, d, mo, vo = _adamw(weights[n][0], _then(after, r), m_in[n][0], v_in[n][0], "adamw_" + n)
            grad_out[n], delta[n], new_m[n], new_v[n] = go[None], d[None], mo[None], vo[None]
        return d

    g_down = ("w_down",)
    g_ffn = ("w_gate", "w_up")
    g_out = ("w_out",)
    g_in = ("w_in",)

    gr_down, = _mm("tn", f_in, df, [F32], "d_w_down", tm=1408, tn=1024)
    th_down, = _swap_core_halves([gr_down], [geom_of["w_down"]], "swap_w_down", 6)
    d_ag, d_au = _mm("nt", df, full["w_down"], [MXU_DTYPE, MXU_DTYPE], "d_ffn_in", extras=(a_g, a_u),
                     epilogue=_swiglu_bwd_epilogue)
    sf_down, sw_down = [[t] for t in _chip_sum(place, gr_down, _then(d_ag, th_down), geom_of["w_down"], "chip_sum_w_down")]
    ot_down = scatter(g_down, sw_down, 7)
    sent, th_ffn = theirs_first(g_ffn, [(h2, d_ag), (h2, d_au)], 9, after=sw_down)
    dh2 = _mm_ktiled("nt", [(_then(sent, d_ag), full["w_gate"]), (d_au, full["w_up"])], "d_h2", tn=512)
    sf_ffn, sw_ffn = chip_sums(g_ffn, [(h2, d_ag), (h2, d_au)], th_ffn, after=dh2)
    ot_ffn = scatter(g_ffn, sw_ffn, 10)
    hv_down = reduce_halves(g_down, sf_down, ot_down, after=sw_ffn)
    rd_down = share(g_down, hv_down, 8)
    dx1, d_shift2, d_scale2, d_norm2_g, d_gate1, d_attn = _norm_mod_bwd(
        _then(hv_down, dh2), x1, dx2, norm2_g, scale2, "norm2_bwd", branch=attn, gate=gate1)
    gr_out, = _mm("tn", o_n, d_attn, [F32], "d_w_out")
    th_out, = _swap_core_halves([gr_out], [geom_of["w_out"]], "swap_w_out", 12)
    d_on, = _mm("nt", _then(gr_out, d_attn), full["w_out"], [F32], "d_o")
    dp_gm, d_w_spatial, d_b_t, d_v_norm_g, d_og_gm = _gmlp_bwd(proj, d_on, v_norm_g, w_spatial[0], b_t, out_norm_g, n_g, "gmlp_bwd")
    sf_out, sw_out = [[t] for t in _chip_sum(place, gr_out, _then(dp_gm, th_out), geom_of["w_out"], "chip_sum_w_out")]
    ot_out = scatter(g_out, sw_out, 13)
    dq, dk, dv, d_og_sb = _sb_bwd(proj, o_sb, l_sum, _then(sw_out, d_on), out_norm_g, n_g, n_h, "sb_bwd")
    dproj = jnp.concatenate([dp_gm, dq, dk, dv], axis=1)
    sent, th_in = theirs_first(g_in, [(h1, dproj)], 15)
    hv_ffn = reduce_halves(g_ffn, sf_ffn, ot_ffn, after=sent)
    rd_ffn = share(g_ffn, hv_ffn, 11)
    dh1, = _mm("nt", _then(sent, dproj), full["w_in"], [F32], "d_h1", tm=1024)
    hv_out = reduce_halves(g_out, sf_out, ot_out, after=dh1)
    rd_out = share(g_out, hv_out, 14)
    grad_x, d_shift1, d_scale1, d_norm1_g = _norm_mod_bwd(_then(hv_out, dh1), xs, dx1, norm1_g, scale1, "norm1_bwd")

    dmod = jnp.concatenate([d_shift1, d_scale1, d_gate1, d_shift2, d_scale2, d_gate2], axis=1)
    small_parts = dict(b_ada=dmod, norm1_g=d_norm1_g, v_norm_g=d_v_norm_g, w_spatial=d_w_spatial, b_spatial=d_b_t.T,
                       out_norm_g=jnp.concatenate([d_og_gm, d_og_sb], axis=1), norm2_g=d_norm2_g, final_g=d_final_g)
    packed = _pack(small_parts)
    small_rows = packed.shape[0]
    rows = -(-(small_rows + 8) // SLAB_ROW_UNIT) * SLAB_ROW_UNIT
    slab = _then(grad_x, jnp.concatenate([packed, jnp.broadcast_to(loss_part, (8, LANE)),
                                          jnp.zeros((rows - small_rows - 8, LANE), F32)], axis=0))
    gathered = _allgather8_on_sequencer(slab, "gather_small", 18)
    sf_in, sw_in = chip_sums(g_in, [(h1, dproj)], th_in, after=slab)
    ot_in = scatter(g_in, sw_in, 16)
    done = adamw(g_down, rd_down, after=sw_in)
    done = adamw(g_ffn, rd_ffn, after=done)
    done = adamw(g_out, rd_out, after=done)
    gathered = _then(done, gathered)
    small_shapes = {n: weights[n].shape for n in SMALL}
    slab_sum = _sum_devices(gathered, 8, "sum_small")
    small_sum, loss = slab_sum[:small_rows], slab_sum[small_rows, 0]
    dmod_all = gathered.reshape(8, rows * LANE)[:, :6 * D]
    dmod_cols = lax.dynamic_slice(dmod_all, (0, chip * n_ada), (8, n_ada))
    g_ada, d, mo, vo = _adamw_ada(c_all, dmod_cols, w_ada[0], m_w_ada[0], v_w_ada[0], "adamw_w_ada")
    grad_out["w_ada"], delta["w_ada"], new_m["w_ada"], new_v["w_ada"] = g_ada[None], d[None], mo[None], vo[None]
    small_out = _adamw_small(_pack({n: weights[n] for n in SMALL}), small_sum, _pack({n: m_in[n] for n in SMALL}),
                             _pack({n: v_in[n] for n in SMALL}), small_shapes, "adamw_small")
    for dst, part in zip((grad_out, delta, new_m, new_v), small_out):
        dst.update(part)
    hv_in = reduce_halves(g_in, sf_in, ot_in, after=d)
    adamw(g_in, share(g_in, hv_in, 17), after=d)

    return (loss, grad_x[None], *[grad_out[n] for n in ORDER], *[delta[n] for n in ORDER],
            *[new_m[n] for n in ORDER], *[new_v[n] for n in ORDER])
```

```python
import functools
import math

import jax
import jax.numpy as jnp
from jax import lax
from jax.experimental import pallas as pl
from jax.experimental.pallas import tpu as pltpu
from jax.experimental.pallas import tpu_sc as plsc

F32 = jnp.float32
MXU_DTYPE = jnp.bfloat16
WIRE_DTYPE = jnp.bfloat16
EPS = 1e-6
LANE = 128
V7X_VMEM_LIMIT = 56 * 1024 * 1024
MESH = pl.DeviceIdType.MESH
N_CHIPS = 4
FLIPS = (2, 1, 3)

ADAM_LR = 0.001
ADAM_B1 = 0.9
ADAM_B2 = 0.999
ADAM_EPS = 1e-08
ADAM_WD = 0.01
ADAM_STEP = 10


def _params(*semantics):
    return pltpu.CompilerParams(dimension_semantics=semantics or None, vmem_limit_bytes=V7X_VMEM_LIMIT)


def _tile(dim, pref, unit=LANE):
    best = None
    t = unit
    while t <= min(dim, pref):
        if dim % t == 0:
            best = t
        t += unit
    return best if best is not None else dim


def _then(first, second):
    return lax.optimization_barrier((first, second))[1]


def _sum0(v):
    return jnp.sum(v, axis=0, keepdims=True)


def _mean1(v):
    return jnp.mean(v, axis=-1, keepdims=True)


def _gelu(x):
    return 0.5 * x * (1.0 + lax.erf(x * (1.0 / math.sqrt(2.0))))


def _gelu_grad(x):
    cdf = 0.5 * (1.0 + lax.erf(x * (1.0 / math.sqrt(2.0))))
    return cdf + x * jnp.exp(-0.5 * x * x) * (1.0 / math.sqrt(2.0 * math.pi))


def _dot(a, b, dims):
    return lax.dot_general(a, b, (dims, ((), ())), preferred_element_type=F32)


NN = ((1,), (0,))
NT = ((1,), (1,))
TN = ((0,), (0,))


def _mm(kind, a, b, out_dtypes, name, tm=2048, tn=512, extras=(), epilogue=None):
    if kind == "nn":
        (M, K), N = a.shape, b.shape[1]
    elif kind == "nt":
        (M, K), N = a.shape, b.shape[0]
    else:
        (K, M), N = a.shape, b.shape[1]
    tm, tn = _tile(M, tm), _tile(N, tn)
    a_spec = pl.BlockSpec((K, tm), lambda i, j: (0, i)) if kind == "tn" else pl.BlockSpec((tm, K), lambda i, j: (i, 0))
    b_spec = pl.BlockSpec((tn, K), lambda i, j: (j, 0)) if kind == "nt" else pl.BlockSpec((K, tn), lambda i, j: (0, j))
    mn_spec = pl.BlockSpec((tm, tn), lambda i, j: (i, j))
    dims = {"nn": NN, "nt": NT, "tn": TN}[kind]
    n_extra = len(extras)

    n_chunks = 1 if epilogue is None or kind == "tn" else max(1, tm // 256)
    rows_per = tm // n_chunks

    def body(a_ref, b_ref, *rest):
        for r in range(n_chunks):
            rows = slice(r * rows_per, (r + 1) * rows_per)
            acc = _dot(a_ref[...] if n_chunks == 1 else a_ref[rows, :], b_ref[...], dims)
            res = (acc,) if epilogue is None else epilogue(acc, *[e[rows, :] for e in rest[:n_extra]])
            for o_ref, val in zip(rest[n_extra:], res):
                o_ref[rows, :] = val.astype(o_ref.dtype)

    outs = pl.pallas_call(
        body, name=name, grid=(M // tm, N // tn),
        in_specs=[a_spec, b_spec] + [mn_spec] * n_extra,
        out_specs=[mn_spec] * len(out_dtypes),
        out_shape=[jax.ShapeDtypeStruct((M, N), d) for d in out_dtypes],
        compiler_params=_params("parallel", "arbitrary"),
    )(a, b, *extras)
    return outs


def _mm_ktiled(kind, pairs, name, tm=2048, tn=1024, tk=1408):
    a0, b0 = pairs[0]
    M, K = a0.shape
    N = b0.shape[1] if kind == "nn" else b0.shape[0]
    tm, tn, tk = _tile(M, tm), _tile(N, tn), _tile(K, tk)
    a_spec = pl.BlockSpec((tm, tk), lambda i, j, k: (i, k))
    b_spec = pl.BlockSpec((tk, tn), lambda i, j, k: (k, j)) if kind == "nn" else pl.BlockSpec((tn, tk), lambda i, j, k: (j, k))
    dims = NN if kind == "nn" else NT
    n_pairs = len(pairs)

    def body(*refs):
        o_ref = refs[2 * n_pairs]
        acc = _dot(refs[0][...], refs[1][...], dims)
        for p in range(1, n_pairs):
            acc = acc + _dot(refs[2 * p][...], refs[2 * p + 1][...], dims)

        @pl.when(pl.program_id(2) == 0)
        def _():
            o_ref[...] = acc

        @pl.when(pl.program_id(2) != 0)
        def _():
            o_ref[...] += acc

    return pl.pallas_call(
        body, name=name, grid=(M // tm, N // tn, K // tk),
        in_specs=[a_spec, b_spec] * n_pairs,
        out_specs=pl.BlockSpec((tm, tn), lambda i, j, k: (i, j)),
        out_shape=jax.ShapeDtypeStruct((M, N), F32),
        compiler_params=_params("parallel", "parallel", "arbitrary"),
    )(*[x for pair in pairs for x in pair])


def _gate_up(h, wg, wu, name):
    (M, K), N = h.shape, wg.shape[1]
    tm, tn = _tile(M, 2048), _tile(N, 512)

    n_chunks = max(1, tm // 256)
    rows_per = tm // n_chunks

    def body(h_ref, wg_ref, wu_ref, ag_ref, au_ref, f_ref):
        for r in range(n_chunks):
            rows = slice(r * rows_per, (r + 1) * rows_per)
            hv = h_ref[rows, :]
            ag = _dot(hv, wg_ref[...], NN)
            au = _dot(hv, wu_ref[...], NN)
            ag_ref[rows, :] = ag.astype(ag_ref.dtype)
            au_ref[rows, :] = au.astype(au_ref.dtype)
            f_ref[rows, :] = (ag * jax.nn.sigmoid(ag) * au).astype(f_ref.dtype)

    w_spec = pl.BlockSpec((K, tn), lambda i, j: (0, j))
    mn_spec = pl.BlockSpec((tm, tn), lambda i, j: (i, j))
    return pl.pallas_call(
        body, name=name, grid=(M // tm, N // tn),
        in_specs=[pl.BlockSpec((tm, K), lambda i, j: (i, 0)), w_spec, w_spec],
        out_specs=[mn_spec] * 3,
        out_shape=[jax.ShapeDtypeStruct((M, N), MXU_DTYPE)] * 3,
        compiler_params=_params("parallel", "arbitrary"),
    )(h, wg, wu)


def _swiglu_bwd_epilogue(dfin, ag, au):
    ag, au = ag.astype(F32), au.astype(F32)
    sg = jax.nn.sigmoid(ag)
    d_au = dfin * (ag * sg)
    d_ag = dfin * au * (sg * (1.0 + ag * (1.0 - sg)))
    return d_ag, d_au


def _row_specs(ts, width):
    return pl.BlockSpec((ts, width), lambda i: (i, 0)), pl.BlockSpec((1, width), lambda i: (0, 0))


def _cast_into_full(place, shard, g, name):
    R, C = shard.shape
    tr = _tile(R, 256, 16)
    n_blk = R // tr
    out_map = (lambda i, p: (i, p[0])) if g.by_cols else (lambda i, p: (p[0] * n_blk + i, 0))

    def body(p_ref, a_ref, o_ref):
        o_ref[...] = a_ref[...].astype(o_ref.dtype)

    return pl.pallas_call(
        body, name=name,
        grid_spec=pltpu.PrefetchScalarGridSpec(
            num_scalar_prefetch=1, grid=(n_blk,),
            in_specs=[pl.BlockSpec((tr, C), lambda i, p: (i, 0))],
            out_specs=pl.BlockSpec((tr, C), out_map)),
        out_shape=jax.ShapeDtypeStruct(g.full, WIRE_DTYPE),
        compiler_params=_params("arbitrary"),
    )(place, shard)


def _norm_mod(x, g, scale, shift, name):
    S, D = x.shape
    ts = _tile(S, 256, 16)
    tile, vec = _row_specs(ts, D)

    def body(x_ref, g_ref, sc_ref, sh_ref, h_ref):
        xv = x_ref[...]
        r = lax.rsqrt(_mean1(xv * xv) + EPS)
        h_ref[...] = ((xv * r) * g_ref[...] * (1.0 + sc_ref[...]) + sh_ref[...]).astype(h_ref.dtype)

    return pl.pallas_call(body, name=name, grid=(S // ts,), in_specs=[tile, vec, vec, vec], out_specs=tile,
                          out_shape=jax.ShapeDtypeStruct((S, D), MXU_DTYPE), compiler_params=_params("parallel"))(x, g, scale, shift)


def _residual_norm_mod(x, attn, gate, g, scale, shift, name):
    S, D = x.shape
    ts = _tile(S, 256, 16)
    tile, vec = _row_specs(ts, D)

    def body(x_ref, a_ref, gate_ref, g_ref, sc_ref, sh_ref, x1_ref, h_ref):
        x1 = x_ref[...] + gate_ref[...] * a_ref[...]
        x1_ref[...] = x1
        r = lax.rsqrt(_mean1(x1 * x1) + EPS)
        h_ref[...] = ((x1 * r) * g_ref[...] * (1.0 + sc_ref[...]) + sh_ref[...]).astype(h_ref.dtype)

    return pl.pallas_call(body, name=name, grid=(S // ts,), in_specs=[tile, tile, vec, vec, vec, vec],
                          out_specs=[tile, tile],
                          out_shape=[jax.ShapeDtypeStruct((S, D), F32), jax.ShapeDtypeStruct((S, D), MXU_DTYPE)],
                          compiler_params=_params("parallel"))(x, attn, gate, g, scale, shift)


def _final_loss_bwd(x1, f, gate2, final_g, target, name):
    S, D = x1.shape
    ts = _tile(S, 256, 16)
    tile, vec = _row_specs(ts, D)
    loss_spec = pl.BlockSpec((1, LANE), lambda i: (0, 0))

    def body(x1_ref, f_ref, gate_ref, g_ref, t_ref, dx2_ref, df_ref, dgate_ref, dg_ref, loss_ref):
        @pl.when(pl.program_id(0) == 0)
        def _():
            dgate_ref[...] = jnp.zeros_like(dgate_ref)
            dg_ref[...] = jnp.zeros_like(dg_ref)
            loss_ref[...] = jnp.zeros_like(loss_ref)

        fv, gate, g = f_ref[...], gate_ref[...], g_ref[...]
        x2 = x1_ref[...] + gate * fv
        r = lax.rsqrt(_mean1(x2 * x2) + EPS)
        xn = x2 * r
        err = xn * g - t_ref[...]
        loss_ref[...] += jnp.broadcast_to(0.5 * _sum0(_mean1(err * err)), loss_ref.shape)
        dy = err * (1.0 / D)
        dg_ref[...] += _sum0(dy * xn)
        dxn = dy * g
        dx2 = r * (dxn - xn * _mean1(dxn * xn))
        dx2_ref[...] = dx2
        dgate_ref[...] += _sum0(dx2 * fv)
        df_ref[...] = (dx2 * gate).astype(df_ref.dtype)

    return pl.pallas_call(
        body, name=name, grid=(S // ts,), in_specs=[tile, tile, vec, vec, tile],
        out_specs=[tile, tile, vec, vec, loss_spec],
        out_shape=[jax.ShapeDtypeStruct((S, D), F32), jax.ShapeDtypeStruct((S, D), MXU_DTYPE),
                   jax.ShapeDtypeStruct((1, D), F32), jax.ShapeDtypeStruct((1, D), F32),
                   jax.ShapeDtypeStruct((1, LANE), F32)],
        compiler_params=_params("arbitrary"),
    )(x1, f, gate2, final_g, target)


def _norm_mod_bwd(dh, xin, dres, g, scale, name, branch=None, gate=None):
    S, D = xin.shape
    ts = _tile(S, 256, 16)
    tile, vec = _row_specs(ts, D)
    with_gate = branch is not None

    def body(*refs):
        if with_gate:
            dh_ref, x_ref, dres_ref, g_ref, sc_ref, br_ref, gate_ref, dx_ref, dshift_ref, dscale_ref, dg_ref, dgate_ref, dbr_ref = refs
            accs = (dshift_ref, dscale_ref, dg_ref, dgate_ref)
        else:
            dh_ref, x_ref, dres_ref, g_ref, sc_ref, dx_ref, dshift_ref, dscale_ref, dg_ref = refs
            accs = (dshift_ref, dscale_ref, dg_ref)

        @pl.when(pl.program_id(0) == 0)
        def _():
            for acc in accs:
                acc[...] = jnp.zeros_like(acc)

        dh_v, xv, g_v = dh_ref[...], x_ref[...], g_ref[...]
        one_sc = 1.0 + sc_ref[...]
        r = lax.rsqrt(_mean1(xv * xv) + EPS)
        xn = xv * r
        dshift_ref[...] += _sum0(dh_v)
        dscale_ref[...] += _sum0(dh_v * (xn * g_v))
        dg_ref[...] += _sum0(dh_v * one_sc * xn)
        dxn = dh_v * (g_v * one_sc)
        dx = dres_ref[...] + r * (dxn - xn * _mean1(dxn * xn))
        dx_ref[...] = dx
        if with_gate:
            dgate_ref[...] += _sum0(dx * br_ref[...])
            dbr_ref[...] = (dx * gate_ref[...]).astype(dbr_ref.dtype)

    ins = [dh, xin, dres, g, scale] + ([branch, gate] if with_gate else [])
    in_specs = [tile, tile, tile, vec, vec] + ([tile, vec] if with_gate else [])
    out_specs = [tile, vec, vec, vec] + ([vec, tile] if with_gate else [])
    out_shape = [jax.ShapeDtypeStruct((S, D), F32)] + [jax.ShapeDtypeStruct((1, D), F32)] * 3
    if with_gate:
        out_shape += [jax.ShapeDtypeStruct((1, D), F32), jax.ShapeDtypeStruct((S, D), MXU_DTYPE)]
    return pl.pallas_call(body, name=name, grid=(S // ts,), in_specs=in_specs, out_specs=out_specs,
                          out_shape=out_shape, compiler_params=_params("arbitrary"))(*ins)


def _causal_weights(ws_ref, wt_ref, n_g):
    row = lax.broadcasted_iota(jnp.int32, (LANE, LANE), 0)
    col = lax.broadcasted_iota(jnp.int32, (LANE, LANE), 1)
    for g in range(n_g):
        wt_ref[g] = jnp.where(col <= row, ws_ref[g], 0.0).astype(wt_ref.dtype)


def _group_layernorm(v):
    xc = v - _mean1(v)
    rstd = lax.rsqrt(_mean1(xc * xc) + EPS)
    return xc * rstd, rstd


def _gmlp_fwd(proj, v_gain, w_s, b_t, out_gain, n_g, name):
    S = proj.shape[0]
    GW = n_g * LANE
    D = out_gain.shape[1]

    def body(p_ref, vg_ref, ws_ref, bt_ref, og_ref, on_ref, wt_ref):
        @pl.when(pl.program_id(0) == 0)
        def _():
            _causal_weights(ws_ref, wt_ref, n_g)

        for g in range(n_g):
            cols = slice(g * LANE, (g + 1) * LANE)
            u = _gelu(p_ref[:, cols])
            v = _gelu(p_ref[:, GW + g * LANE:GW + (g + 1) * LANE])
            vhat, _ = _group_layernorm(v)
            vln = (vhat * vg_ref[:, cols]).astype(MXU_DTYPE)
            mixed = _dot(wt_ref[g], vln, NN) + bt_ref[:, g:g + 1]
            o = u * mixed
            r = lax.rsqrt(_mean1(o * o) + EPS)
            on_ref[:, cols] = (o * r * og_ref[:, cols]).astype(on_ref.dtype)

    return pl.pallas_call(
        body, name=name, grid=(S // LANE,),
        in_specs=[pl.BlockSpec((LANE, 2 * GW), lambda n: (n, 0)),
                  pl.BlockSpec((1, GW), lambda n: (0, 0)),
                  pl.BlockSpec((n_g, LANE, LANE), lambda n: (0, 0, 0)),
                  pl.BlockSpec((LANE, n_g), lambda n: (0, 0)),
                  pl.BlockSpec((1, GW), lambda n: (0, 0))],
        out_specs=pl.BlockSpec((LANE, GW), lambda n: (n, 0)),
        out_shape=jax.ShapeDtypeStruct((S, D), MXU_DTYPE),
        scratch_shapes=[pltpu.VMEM((n_g, LANE, LANE), MXU_DTYPE)],
        compiler_params=_params("arbitrary"),
    )(proj, v_gain, w_s, b_t, out_gain)


def _gmlp_bwd(proj, d_on, v_gain, w_s, b_t, out_gain, n_g, name):
    S, N_IN = proj.shape
    GW = n_g * LANE

    def body(p_ref, dn_ref, vg_ref, ws_ref, bt_ref, og_ref, dp_ref, dws_ref, dbt_ref, dvg_ref, dog_ref, wt_ref):
        @pl.when(pl.program_id(0) == 0)
        def _():
            _causal_weights(ws_ref, wt_ref, n_g)
            dws_ref[...] = jnp.zeros_like(dws_ref)
            dbt_ref[...] = jnp.zeros_like(dbt_ref)
            dvg_ref[...] = jnp.zeros_like(dvg_ref)
            dog_ref[...] = jnp.zeros_like(dog_ref)

        row = lax.broadcasted_iota(jnp.int32, (LANE, LANE), 0)
        col = lax.broadcasted_iota(jnp.int32, (LANE, LANE), 1)
        for g in range(n_g):
            cols = slice(g * LANE, (g + 1) * LANE)
            vcols = slice(GW + g * LANE, GW + (g + 1) * LANE)
            pu, pv = p_ref[:, cols], p_ref[:, vcols]
            u, v = _gelu(pu), _gelu(pv)
            vhat, rstd = _group_layernorm(v)
            gain = vg_ref[:, cols]
            vln = (vhat * gain).astype(MXU_DTYPE)
            mixed = _dot(wt_ref[g], vln, NN) + bt_ref[:, g:g + 1]
            o = u * mixed
            r = lax.rsqrt(_mean1(o * o) + EPS)
            oh = o * r
            dn = dn_ref[:, cols]
            dog_ref[:, cols] += _sum0(dn * oh)
            dhn = dn * og_ref[:, cols]
            d_o = r * (dhn - oh * _mean1(dhn * oh))
            du = d_o * mixed
            dmix = d_o * u
            dbt_ref[:, g:g + 1] += jnp.sum(dmix, axis=1, keepdims=True)
            dmix_b = dmix.astype(MXU_DTYPE)
            dws_ref[g] += jnp.where(col <= row, _dot(dmix_b, vln, NT), 0.0)
            dvln = _dot(wt_ref[g], dmix_b, TN)
            dvg_ref[:, cols] += _sum0(dvln * vhat)
            dxh = dvln * gain
            dv = rstd * (dxh - _mean1(dxh) - vhat * _mean1(dxh * vhat))
            dp_ref[:, cols] = (du * _gelu_grad(pu)).astype(dp_ref.dtype)
            dp_ref[:, vcols] = (dv * _gelu_grad(pv)).astype(dp_ref.dtype)

    return pl.pallas_call(
        body, name=name, grid=(S // LANE,),
        in_specs=[pl.BlockSpec((LANE, 2 * GW), lambda n: (n, 0)),
                  pl.BlockSpec((LANE, GW), lambda n: (n, 0)),
                  pl.BlockSpec((1, GW), lambda n: (0, 0)),
                  pl.BlockSpec((n_g, LANE, LANE), lambda n: (0, 0, 0)),
                  pl.BlockSpec((LANE, n_g), lambda n: (0, 0)),
                  pl.BlockSpec((1, GW), lambda n: (0, 0))],
        out_specs=[pl.BlockSpec((LANE, 2 * GW), lambda n: (n, 0)),
                   pl.BlockSpec((n_g, LANE, LANE), lambda n: (0, 0, 0)),
                   pl.BlockSpec((LANE, n_g), lambda n: (0, 0)),
                   pl.BlockSpec((1, GW), lambda n: (0, 0)),
                   pl.BlockSpec((1, GW), lambda n: (0, 0))],
        out_shape=[jax.ShapeDtypeStruct((S, N_IN), MXU_DTYPE),
                   jax.ShapeDtypeStruct((n_g, LANE, LANE), F32),
                   jax.ShapeDtypeStruct((LANE, n_g), F32),
                   jax.ShapeDtypeStruct((1, GW), F32),
                   jax.ShapeDtypeStruct((1, GW), F32)],
        scratch_shapes=[pltpu.VMEM((n_g, LANE, LANE), MXU_DTYPE)],
        compiler_params=_params("arbitrary"),
    )(proj, d_on, v_gain, w_s, b_t, out_gain)


def _tri_sum(v, tri, exact=True):
    hi = v.astype(MXU_DTYPE)
    if not exact:
        return _dot(hi, tri, NN)
    lo = (v - hi.astype(F32)).astype(MXU_DTYPE)
    return _dot(hi, tri, NN) + _dot(lo, tri, NN)


def _log_sigmoids(z):
    sp = jnp.log(1.0 + jnp.exp(-jnp.abs(z)))
    return jnp.minimum(z, 0.0) - sp, jnp.minimum(-z, 0.0) - sp


def _rows(i, size):
    return pl.ds(pl.multiple_of(i * size, size), size)


SB_QUERY_TILE = 2048
SB_KEY_TILE = 256


def _sb_tiles(S):
    tq = _tile(S, SB_QUERY_TILE)
    tk = _tile(tq, SB_KEY_TILE)
    assert (tq // tk) % 2 == 0, "the key sweep takes two blocks a pass"
    return tq, tk, S // tq, tq // tk


def _triangle(n, keep):
    row = lax.broadcasted_iota(jnp.int32, (n, n), 0)
    col = lax.broadcasted_iota(jnp.int32, (n, n), 1)
    return jnp.where(keep(row, col), 1.0, 0.0).astype(MXU_DTYPE)


def _strictly_before(tq, tk, key_offset):
    row = lax.broadcasted_iota(jnp.int32, (tq, tk), 0)
    col = lax.broadcasted_iota(jnp.int32, (tq, tk), 1)
    return col + key_offset < row


def _sb_specs(S, n_g, n_h):
    base = 2 * n_g
    q_spec = pl.BlockSpec((S, LANE), lambda h: (0, base + h))
    k_spec = pl.BlockSpec((S, LANE), lambda h: (0, base + n_h + h))
    v_spec = pl.BlockSpec((S, LANE), lambda h: (0, base + 2 * n_h + h))
    gain_spec = pl.BlockSpec((1, LANE), lambda h: (0, n_g + h))
    head_spec = pl.BlockSpec((S, LANE), lambda h: (0, h))
    return q_spec, k_spec, v_spec, gain_spec, head_spec


def _sb_fwd(proj, out_gain, on_buffer, n_g, n_h, name):
    S = proj.shape[0]
    TQ, TK, NQ, KPQ = _sb_tiles(S)
    scale = LANE ** -0.5
    q_spec, k_spec, v_spec, gain_spec, head_spec = _sb_specs(S, n_g, n_h)

    def body(q_ref, k_ref, v_ref, og_ref, _, o_ref, on_ref, ls_ref, qb, kb, vb):
        qb[...] = q_ref[...].astype(MXU_DTYPE)
        kb[...] = k_ref[...].astype(MXU_DTYPE)
        vb[...] = v_ref[...].astype(MXU_DTYPE)
        after = _triangle(TK, lambda r, c: r > c)

        def block(qi, j, ctail, acc, key_offset):
            skip = key_offset or 0
            z = _dot(qi[skip:], kb[_rows(j, TK), :], NT) * scale
            lb, l1m = _log_sigmoids(z)
            if key_offset is not None:
                strict = _strictly_before(TQ - skip, TK, 0)
                l1m = jnp.where(strict, l1m, 0.0)
            a = jnp.exp(lb + ctail[skip:] + _tri_sum(l1m, after))
            if key_offset is not None:
                a = jnp.where(strict, a, 0.0)
            acc_new = acc[skip:] + _dot(a.astype(MXU_DTYPE), vb[_rows(j, TK), :], NN)
            ctail_new = ctail[skip:] + jnp.sum(l1m, axis=1, keepdims=True)
            if skip:
                ctail_new = jnp.concatenate([ctail[:skip], ctail_new], axis=0)
                acc_new = jnp.concatenate([acc[:skip], acc_new], axis=0)
            return ctail_new, acc_new

        def q_loop(i, carry):
            qi = qb[_rows(i, TQ), :]
            state = (jnp.zeros((TQ, 1), F32), jnp.zeros((TQ, LANE), F32))
            for d in reversed(range(KPQ)):
                state = block(qi, i * KPQ + d, state[0], state[1], d * TK)
            def pair(jj, st):
                st = block(qi, i * KPQ - 1 - 2 * jj, st[0], st[1], None)
                return block(qi, i * KPQ - 2 - 2 * jj, st[0], st[1], None)

            ctail, acc = lax.fori_loop(0, i * (KPQ // 2), pair, state)
            ls_ref[_rows(i, TQ), :] = jnp.broadcast_to(ctail, (TQ, LANE))
            o_ref[_rows(i, TQ), :] = acc
            r = lax.rsqrt(_mean1(acc * acc) + EPS)
            on_ref[_rows(i, TQ), :] = (acc * r * og_ref[...]).astype(on_ref.dtype)
            return carry

        lax.fori_loop(0, NQ, q_loop, 0)

    return pl.pallas_call(
        body, name=name, grid=(n_h,),
        in_specs=[q_spec, k_spec, v_spec, gain_spec, pl.BlockSpec(memory_space=pl.ANY)],
        out_specs=[head_spec, pl.BlockSpec((S, LANE), lambda h: (0, n_g + h)), head_spec],
        out_shape=[jax.ShapeDtypeStruct((S, n_h * LANE), F32), jax.ShapeDtypeStruct(on_buffer.shape, MXU_DTYPE),
                   jax.ShapeDtypeStruct((S, n_h * LANE), F32)],
        input_output_aliases={4: 1},
        scratch_shapes=[pltpu.VMEM((S, LANE), MXU_DTYPE)] * 3,
        compiler_params=_params("parallel"),
    )(proj, proj, proj, out_gain, on_buffer)


def _sb_bwd(proj, o_sb, l_sum, d_on, out_gain, d_proj, n_g, n_h, name):
    S = proj.shape[0]
    TQ, TK, NQ, KPQ = _sb_tiles(S)
    scale = LANE ** -0.5
    base = 2 * n_g

    def col(block):
        return pl.BlockSpec((S, LANE), lambda h, part: (0, block(h, part)))

    def body(q_ref, k_ref, v_ref, o_ref, ls_ref, dn_ref, og_ref, _, out_ref, dog_ref, qb, kb, vb, dob, dk_acc, dv_acc):
        part = pl.program_id(1)
        pl.when(part == 0)(functools.partial(sweep, q_ref, k_ref, v_ref, o_ref, ls_ref, dn_ref, og_ref, out_ref, dog_ref,
                                             qb, kb, vb, dob, dk_acc, dv_acc))

        @pl.when(part == 1)
        def _():
            out_ref[...] = dk_acc[...].astype(out_ref.dtype)

        @pl.when(part == 2)
        def _():
            out_ref[...] = dv_acc[...].astype(out_ref.dtype)

    def sweep(q_ref, k_ref, v_ref, o_ref, ls_ref, dn_ref, og_ref, dq_ref, dog_ref, qb, kb, vb, dob, dk_acc, dv_acc):
        qb[...] = q_ref[...].astype(MXU_DTYPE)
        kb[...] = k_ref[...].astype(MXU_DTYPE)
        vb[...] = v_ref[...].astype(MXU_DTYPE)
        o, dn = o_ref[...], dn_ref[...]
        r = lax.rsqrt(_mean1(o * o) + EPS)
        oh = o * r
        dog_ref[...] = _sum0(dn * oh)
        dhn = dn * og_ref[...]
        dob[...] = (r * (dhn - oh * _mean1(dhn * oh))).astype(MXU_DTYPE)
        dk_acc[...] = jnp.zeros_like(dk_acc)
        dv_acc[...] = jnp.zeros_like(dv_acc)

        up_to = _triangle(TK, lambda r, c: r <= c)
        before = _triangle(TK, lambda r, c: r < c)

        def block(qi, doi, ltot, j, cl, cdl, dq, key_offset):
            skip = key_offset or 0
            q_in, do_in = qi[skip:], doi[skip:]
            kj, vj = kb[_rows(j, TK), :], vb[_rows(j, TK), :]
            z = _dot(q_in, kj, NT) * scale
            lb, l1m = _log_sigmoids(z)
            if key_offset is not None:
                strict = _strictly_before(TQ - skip, TK, 0)
                l1m = jnp.where(strict, l1m, 0.0)
            a = jnp.exp(lb + (ltot[skip:] - (cl[skip:] + _tri_sum(l1m, up_to))))
            if key_offset is not None:
                a = jnp.where(strict, a, 0.0)
            dl = _dot(do_in, vj, NT) * a
            d_l1m = cdl[skip:] + _tri_sum(dl, before, exact=False)
            beta = jnp.exp(lb)
            dz = dl * (1.0 - beta) - beta * d_l1m
            if key_offset is not None:
                dz = jnp.where(strict, dz, 0.0)
            dzs = (dz * scale).astype(MXU_DTYPE)
            dk_acc[_rows(j, TK), :] += _dot(dzs, q_in, TN)
            dv_acc[_rows(j, TK), :] += _dot(a.astype(MXU_DTYPE), do_in, TN)
            cl_new = cl[skip:] + jnp.sum(l1m, axis=1, keepdims=True)
            cdl_new = cdl[skip:] + jnp.sum(dl, axis=1, keepdims=True)
            dq_new = dq[skip:] + _dot(dzs, kj, NN)
            if skip:
                cl_new = jnp.concatenate([cl[:skip], cl_new], axis=0)
                cdl_new = jnp.concatenate([cdl[:skip], cdl_new], axis=0)
                dq_new = jnp.concatenate([dq[:skip], dq_new], axis=0)
            return cl_new, cdl_new, dq_new

        def q_loop(i, carry):
            qi, doi = qb[_rows(i, TQ), :], dob[_rows(i, TQ), :]
            ltot = ls_ref[_rows(i, TQ), :][:, :1]
            zero_col = jnp.zeros((TQ, 1), F32)
            def pair(jj, st):
                st = block(qi, doi, ltot, 2 * jj, st[0], st[1], st[2], None)
                return block(qi, doi, ltot, 2 * jj + 1, st[0], st[1], st[2], None)

            state = lax.fori_loop(0, i * (KPQ // 2), pair, (zero_col, zero_col, jnp.zeros((TQ, LANE), F32)))
            for d in range(KPQ):
                state = block(qi, doi, ltot, i * KPQ + d, state[0], state[1], state[2], d * TK)
            dq_ref[_rows(i, TQ), :] = state[2].astype(dq_ref.dtype)
            return carry

        lax.fori_loop(0, NQ, q_loop, 0)

    return pl.pallas_call(
        body, name=name, grid=(n_h, 3),
        in_specs=[col(lambda h, p: base + h), col(lambda h, p: base + n_h + h), col(lambda h, p: base + 2 * n_h + h),
                  col(lambda h, p: h), col(lambda h, p: h), col(lambda h, p: n_g + h),
                  pl.BlockSpec((1, LANE), lambda h, part: (0, n_g + h)), pl.BlockSpec(memory_space=pl.ANY)],
        out_specs=[col(lambda h, p: base + p * n_h + h), pl.BlockSpec((1, LANE), lambda h, part: (0, h))],
        out_shape=[jax.ShapeDtypeStruct(d_proj.shape, MXU_DTYPE), jax.ShapeDtypeStruct((1, n_h * LANE), F32)],
        input_output_aliases={7: 0},
        scratch_shapes=[pltpu.VMEM((S, LANE), MXU_DTYPE)] * 4 + [pltpu.VMEM((S, LANE), F32)] * 2,
        compiler_params=_params("parallel", "arbitrary"),
    )(proj, proj, proj, o_sb, l_sum, d_on, out_gain, d_proj)


def _mod_part(c_all, w_ada, b_ada_cols, name):
    B, K = c_all.shape
    N = w_ada.shape[1]
    tn = _tile(N, 512)

    def body(c_ref, w_ref, b_ref, o_ref):
        cv = c_ref[...]
        ca = (cv * jax.nn.sigmoid(cv)).astype(MXU_DTYPE)
        o_ref[...] = _dot(ca, w_ref[...].astype(MXU_DTYPE), NN) + b_ref[...]

    return pl.pallas_call(
        body, name=name, grid=(N // tn,),
        in_specs=[pl.BlockSpec((B, K), lambda j: (0, 0)), pl.BlockSpec((K, tn), lambda j: (0, j)),
                  pl.BlockSpec((1, tn), lambda j: (0, j))],
        out_specs=pl.BlockSpec((B, tn), lambda j: (0, j)),
        out_shape=jax.ShapeDtypeStruct((B, N), F32), compiler_params=_params("parallel"))(c_all, w_ada, b_ada_cols)


def _adamw_math(w, g, m, v):
    m = ADAM_B1 * m + (1.0 - ADAM_B1) * g
    v = ADAM_B2 * v + (1.0 - ADAM_B2) * (g * g)
    m_hat = m / (1.0 - ADAM_B1 ** ADAM_STEP)
    v_hat = v / (1.0 - ADAM_B2 ** ADAM_STEP)
    delta = -ADAM_LR * (m_hat / (jnp.sqrt(v_hat) + ADAM_EPS) + ADAM_WD * w)
    return delta, m, v


def _adamw(w, g, m, v, name):
    R, C = w.shape
    tr = _tile(R, max(8, (1 << 19) // C), 8)
    spec = pl.BlockSpec((tr, C), lambda i: (i, 0))

    def body(w_ref, g_ref, m_ref, v_ref, go_ref, d_ref, mo_ref, vo_ref):
        g = g_ref[...]
        go_ref[...] = g
        d_ref[...], mo_ref[...], vo_ref[...] = _adamw_math(w_ref[...], g, m_ref[...], v_ref[...])

    return pl.pallas_call(body, name=name, grid=(R // tr,), in_specs=[spec] * 4, out_specs=[spec] * 4,
                          out_shape=[jax.ShapeDtypeStruct((R, C), F32)] * 4, compiler_params=_params("parallel"))(w, g, m, v)


def _adamw_ada(c_all, dmod_cols, w, m, v, name):
    K, N = w.shape
    B = c_all.shape[0]
    tk, tn = _tile(K, 512), _tile(N, 1024)
    spec = pl.BlockSpec((tk, tn), lambda i, j: (i, j))

    def body(c_ref, dm_ref, w_ref, m_ref, v_ref, g_ref, d_ref, mo_ref, vo_ref):
        cv = c_ref[...]
        ca = (cv * jax.nn.sigmoid(cv)).astype(MXU_DTYPE)
        g = _dot(ca, dm_ref[...].astype(MXU_DTYPE), TN)
        g_ref[...] = g
        d_ref[...], mo_ref[...], vo_ref[...] = _adamw_math(w_ref[...], g, m_ref[...], v_ref[...])

    return pl.pallas_call(
        body, name=name, grid=(K // tk, N // tn),
        in_specs=[pl.BlockSpec((B, tk), lambda i, j: (0, i)), pl.BlockSpec((B, tn), lambda i, j: (0, j)), spec, spec, spec],
        out_specs=[spec] * 4, out_shape=[jax.ShapeDtypeStruct((K, N), F32)] * 4,
        compiler_params=_params("parallel", "parallel"))(c_all, dmod_cols, w, m, v)


def _sum_devices(gathered, n_dev, name):
    R = gathered.shape[0] // n_dev
    C = gathered.shape[1]
    tr = _tile(R, 512, 8)
    n_blk = R // tr

    def body(*refs):
        acc = refs[0][...]
        for r in refs[1:n_dev]:
            acc = acc + r[...]
        refs[n_dev][...] = acc

    in_specs = [pl.BlockSpec((tr, C), functools.partial(lambda i, d: (d * n_blk + i, 0), d=d)) for d in range(n_dev)]
    return pl.pallas_call(body, name=name, grid=(n_blk,), in_specs=in_specs,
                          out_specs=pl.BlockSpec((tr, C), lambda i: (i, 0)),
                          out_shape=jax.ShapeDtypeStruct((R, C), F32), compiler_params=_params("parallel"))(*([gathered] * n_dev))


def _place():
    x, y, c = lax.axis_index("x"), lax.axis_index("y"), lax.axis_index("c")
    return x, y, c


def _allgather8(blk, name):
    m_per, n = blk.shape

    def body(x_ref, out_ref, send_sems, recv_sems, local_sem):
        x, y, c = _place()
        me, sibling = (x, y, c), (x, y, 1 - c)
        chips = [(1 - x, y), (x, 1 - y), (1 - x, 1 - y)]

        def rows(px, py, pc):
            return out_ref.at[pl.ds((4 * px + 2 * py + pc) * m_per, m_per), :]

        def copy(k, block, to, src=None):
            return pltpu.make_async_remote_copy(
                src_ref=rows(*block) if src is None else src, dst_ref=rows(*block),
                send_sem=send_sems.at[k], recv_sem=recv_sems.at[k], device_id=to, device_id_type=MESH)

        mine = pltpu.make_async_copy(x_ref, rows(*me), local_sem)
        mine.start()
        first = [copy(0, me, sibling, src=x_ref)]
        first += [copy(1 + j, me, (*chip, c), src=x_ref) for j, chip in enumerate(chips)]
        for cp in first:
            cp.start()
        passed = [copy(4 + j, (*chip, c), sibling) for j, chip in enumerate(chips)]
        for j, chip in enumerate(chips):
            copy(1 + j, (*chip, c), me).wait_recv()
            passed[j].start()
        copy(0, sibling, me).wait_recv()
        for j, chip in enumerate(chips):
            copy(4 + j, (*chip, 1 - c), me).wait_recv()
        for cp in first + passed:
            cp.wait_send()
        mine.wait()

    return pl.pallas_call(
        body, name=name,
        out_shape=jax.ShapeDtypeStruct((8 * m_per, n), blk.dtype),
        in_specs=[pl.BlockSpec(memory_space=pltpu.VMEM)],
        out_specs=pl.BlockSpec(memory_space=pltpu.VMEM),
        scratch_shapes=[pltpu.SemaphoreType.DMA((7,)), pltpu.SemaphoreType.DMA((7,)), pltpu.SemaphoreType.DMA],
        compiler_params=pltpu.CompilerParams(vmem_limit_bytes=V7X_VMEM_LIMIT),
    )(blk)


class _Sharded:
    def __init__(self, shard_shape, by_cols):
        r, c = shard_shape
        self.by_cols = by_cols
        self.full = (r, N_CHIPS * c) if by_cols else (N_CHIPS * r, c)
        self.shard = (r, c)
        self.half_rows = r // 2
        self.half = (r // 2, c)

    def shard_of(self, ref, k):
        r, c = self.shard
        return ref.at[:, pl.ds(k * c, c)] if self.by_cols else ref.at[pl.ds(k * r, r), :]

    def half_of(self, ref, k, hc):
        r, c = self.shard
        h = self.half_rows
        if self.by_cols:
            return ref.at[pl.ds(hc * h, h), pl.ds(k * c, c)]
        return ref.at[pl.ds(k * r + hc * h, h), :]

    def chunk_of(self, ref, k, hc, ch, n):
        r, c = self.shard
        h = self.half_rows
        q = h // n
        if self.by_cols:
            return ref.at[pl.ds(hc * h + ch * q, q), pl.ds(k * c, c)]
        return ref.at[pl.ds(k * r + hc * h + ch * q, q), :]

    def half_of_shard(self, ref, hc):
        return ref.at[pl.ds(hc * self.half_rows, self.half_rows), :]

    def part_of_halves(self, ref, k):
        r, c = self.shard
        h = self.half_rows
        return ref.at[:, pl.ds(k * c, c)] if self.by_cols else ref.at[pl.ds(k * h, h), :]


def _on_each_place(x, y, c, fn, by_chip=True, by_core=True):
    q = 2 * x + y
    for k in range(N_CHIPS if by_chip else 1):
        for cc in range(2 if by_core else 1):
            cond = None
            if by_chip:
                cond = q == k
            if by_core:
                cond = (c == cc) if cond is None else jnp.logical_and(cond, c == cc)
            pl.when(cond)(functools.partial(fn, k, cc))


def _chip_id(k, c):
    return (k // 2, k % 2, c)


def _handshake(peers):
    barrier = pltpu.get_barrier_semaphore()
    for peer in peers:
        pl.semaphore_signal(barrier, inc=1, device_id=peer, device_id_type=MESH)
    pl.semaphore_wait(barrier, len(peers))


def _on_sequencer(body, inputs, out_structs, n_copies, peers_of, name, collective_id, return_inputs=False):
    in_refs = [jax.new_ref(a, memory_space=pltpu.MemorySpace.HBM) for a in inputs]
    out_refs = [jax.empty_ref(s, memory_space=pltpu.MemorySpace.HBM) for s in out_structs]

    @pl.kernel(mesh=plsc.ScalarSubcoreMesh(axis_name="sequencer", num_cores=1), name=name,
               scratch_types=(pltpu.SemaphoreType.DMA((n_copies,)), pltpu.SemaphoreType.DMA((n_copies,))),
               compiler_params=pltpu.CompilerParams(collective_id=collective_id))
    def launch(send_sems, recv_sems):
        x, y, c = _place()
        _handshake(peers_of(x, y, c))
        body(in_refs, out_refs, send_sems, recv_sems, x, y, c)

    launch()
    return [r[...] for r in (in_refs if return_inputs else out_refs)]


def _sibling(x, y, c):
    return [(x, y, 1 - c)]


def _same_core_of_other_chips(x, y, c):
    return [(1 - x, y, c), (x, 1 - y, c), (1 - x, 1 - y, c)]


GATHER_CHUNKS = 4
GATHER_COPIES = 6 * GATHER_CHUNKS


def _allgather8_on_sequencer(blk, name, collective_id):
    m_per, n = blk.shape
    x, y, c = _place()
    placed = lax.dynamic_update_slice(jnp.zeros((8 * m_per, n), blk.dtype), blk, ((4 * x + 2 * y + c) * m_per, 0))

    def body(refs, _, send_sems, recv_sems, x, y, c):
        out_ref, = refs

        def at_place(k, cc):
            def rows(kk, pc):
                return out_ref.at[pl.ds((2 * kk + pc) * m_per, m_per), :]

            def copy(slot, block, to):
                return pltpu.make_async_remote_copy(src_ref=rows(*block), dst_ref=rows(*block), send_sem=send_sems.at[slot],
                                                    recv_sem=recv_sems.at[slot], device_id=to, device_id_type=MESH)

            others = [k ^ flip for flip in FLIPS]
            sends = [copy(0, (k, cc), _chip_id(k, 1 - cc))] + [copy(1 + j, (k, cc), _chip_id(kk, cc)) for j, kk in enumerate(others)]
            for cp in sends:
                cp.start()
            for j, kk in enumerate(others):
                copy(1 + j, (kk, cc), _chip_id(k, cc)).wait_recv()
                cp = copy(4 + j, (kk, cc), _chip_id(k, 1 - cc))
                cp.start()
                sends.append(cp)
            copy(0, (k, 1 - cc), _chip_id(k, cc)).wait_recv()
            for j, kk in enumerate(others):
                copy(4 + j, (kk, 1 - cc), _chip_id(k, cc)).wait_recv()
            for cp in sends:
                cp.wait_send()

        _on_each_place(x, y, c, at_place)

    def peers(x, y, c):
        return _sibling(x, y, c) + _same_core_of_other_chips(x, y, c)

    return _on_sequencer(body, [placed], [], 7, peers, name, collective_id, return_inputs=True)[0]


def _gather_weights(fulls, geoms, name, collective_id):
    n_w = len(fulls)
    n_ch, n_relay = GATHER_CHUNKS, GATHER_CHUNKS // 2
    f_refs = [jax.new_ref(f, memory_space=pltpu.MemorySpace.HBM) for f in fulls]
    FLIP_X, FLIP_Y, FLIP_BOTH = FLIPS
    TO_X, TO_Y, RELAY_TO_Y, RELAY_TO_X, ON_X, ON_Y, ON_DIAG = 0, n_ch, 2 * n_ch, 2 * n_ch + n_relay, 3 * n_ch, 4 * n_ch, 5 * n_ch

    @pl.kernel(mesh=plsc.ScalarSubcoreMesh(axis_name="sequencer", num_cores=1), name=name,
               scratch_types=(pltpu.SemaphoreType.DMA((GATHER_COPIES * n_w,)), pltpu.SemaphoreType.DMA((GATHER_COPIES * n_w,))),
               compiler_params=pltpu.CompilerParams(collective_id=collective_id))
    def launch(send_sems, recv_sems):
        x, y, c = _place()
        _handshake([(x, y, 1 - c), (1 - x, y, c), (x, 1 - y, c)])

        def at_place(k, cc):
            kx, ky, kd = k ^ FLIP_X, k ^ FLIP_Y, k ^ FLIP_BOTH
            me, sibling = _chip_id(k, cc), _chip_id(k, 1 - cc)
            started = []

            def copy(i, slot, src, dst, to, start=True):
                cp = pltpu.make_async_remote_copy(src_ref=src, dst_ref=dst, send_sem=send_sems.at[GATHER_COPIES * i + slot],
                                                  recv_sem=recv_sems.at[GATHER_COPIES * i + slot], device_id=to, device_id_type=MESH)
                if start:
                    cp.start()
                    started.append(cp)
                return cp

            def pass_on(i, slot, ref, to):
                copy(i, slot, ref, ref, to)

            def landed(i, slot, ref):
                copy(i, slot, ref, ref, me, start=False).wait_recv()

            y_order = [(n_relay + s) % n_ch for s in range(n_ch)]
            for i, (g, f_ref) in enumerate(zip(geoms, f_refs)):
                for s in range(n_ch):
                    pass_on(i, TO_X + s, g.chunk_of(f_ref, k, cc, s, n_ch), _chip_id(kx, cc))
                    pass_on(i, TO_Y + y_order[s], g.chunk_of(f_ref, k, cc, y_order[s], n_ch), _chip_id(ky, cc))
            for i, (g, f_ref) in enumerate(zip(geoms, f_refs)):
                for s in range(n_ch):
                    from_x = g.chunk_of(f_ref, kx, cc, s, n_ch)
                    landed(i, TO_X + s, from_x)
                    if s < n_relay:
                        pass_on(i, RELAY_TO_Y + s, from_x, _chip_id(ky, cc))
                    pass_on(i, ON_X + s, from_x, sibling)
                    ch = y_order[s]
                    from_y = g.chunk_of(f_ref, ky, cc, ch, n_ch)
                    landed(i, TO_Y + ch, from_y)
                    if ch >= n_relay:
                        pass_on(i, RELAY_TO_X + ch - n_relay, from_y, _chip_id(kx, cc))
                    pass_on(i, ON_Y + ch, from_y, sibling)
                for r in range(n_relay):
                    via_y = g.chunk_of(f_ref, kd, cc, r, n_ch)
                    landed(i, RELAY_TO_Y + r, via_y)
                    pass_on(i, ON_DIAG + r, via_y, sibling)
                    via_x = g.chunk_of(f_ref, kd, cc, n_relay + r, n_ch)
                    landed(i, RELAY_TO_X + r, via_x)
                    pass_on(i, ON_DIAG + n_relay + r, via_x, sibling)
            for i, (g, f_ref) in enumerate(zip(geoms, f_refs)):
                for slot, kk in ((ON_X, kx), (ON_Y, ky), (ON_DIAG, kd)):
                    for ch in range(n_ch):
                        landed(i, slot + ch, g.chunk_of(f_ref, kk, 1 - cc, ch, n_ch))
            for cp in started:
                cp.wait_send()

        _on_each_place(x, y, c, at_place)

    launch()
    return [f_ref[...] for f_ref in f_refs]


def _swap_core_halves(grads, geoms, name, collective_id):
    n_cp = sum(1 if g.by_cols else N_CHIPS for g in geoms)

    def body(g_refs, t_refs, send_sems, recv_sems, x, y, c):

        def at_place(_, cc):
            def pairs(hc):
                out = []
                for g, g_ref, t_ref in zip(geoms, g_refs, t_refs):
                    if g.by_cols:
                        out.append((g_ref.at[pl.ds(hc * g.half_rows, g.half_rows), :], t_ref))
                    else:
                        out += [(g.half_of(g_ref, k, hc), g.part_of_halves(t_ref, k)) for k in range(N_CHIPS)]
                return out

            sends = [pltpu.make_async_remote_copy(src_ref=src, dst_ref=dst, send_sem=send_sems.at[n],
                                                  recv_sem=recv_sems.at[n], device_id=(x, y, 1 - cc), device_id_type=MESH)
                     for n, (src, dst) in enumerate(pairs(1 - cc))]
            for cp in sends:
                cp.start()
            for n, (src, dst) in enumerate(pairs(cc)):
                pltpu.make_async_remote_copy(src_ref=src, dst_ref=dst, send_sem=send_sems.at[n], recv_sem=recv_sems.at[n],
                                             device_id=(x, y, cc), device_id_type=MESH).wait_recv()
            for cp in sends:
                cp.wait_send()

        _on_each_place(x, y, c, at_place, by_chip=False)

    return _on_sequencer(body, grads, [jax.ShapeDtypeStruct((g.full[0] // 2, g.full[1]), F32) for g in geoms],
                         n_cp, _sibling, name, collective_id)


def _send_to_sibling(buffers, name, collective_id):
    def body(src_refs, dst_refs, send_sems, recv_sems, x, y, c):
        def copy(i):
            return pltpu.make_async_remote_copy(src_ref=src_refs[i], dst_ref=dst_refs[i], send_sem=send_sems.at[i],
                                                recv_sem=recv_sems.at[i], device_id=(x, y, 1 - c), device_id_type=MESH)

        for i in range(len(buffers)):
            copy(i).start()
        for i in range(len(buffers)):
            copy(i).wait()

    return _on_sequencer(body, buffers, [jax.ShapeDtypeStruct(t.shape, t.dtype) for t in buffers], len(buffers),
                         _sibling, name, collective_id)


def _scatter_chip_sums(sums, geoms, name, collective_id):
    def body(s_refs, r_refs, send_sems, recv_sems, x, y, c):

        def at_place(k, _):
            sends = []
            for i, (g, s_ref, r_ref) in enumerate(zip(geoms, s_refs, r_refs)):
                for j, flip in enumerate(FLIPS):
                    kk = k ^ flip
                    cp = pltpu.make_async_remote_copy(
                        src_ref=g.part_of_halves(s_ref, kk), dst_ref=r_ref.at[j], send_sem=send_sems.at[3 * i + j],
                        recv_sem=recv_sems.at[3 * i + j], device_id=(kk // 2, kk % 2, c), device_id_type=MESH)
                    cp.start()
                    sends.append(cp)
            for i, (g, s_ref, r_ref) in enumerate(zip(geoms, s_refs, r_refs)):
                for j in range(len(FLIPS)):
                    pltpu.make_async_remote_copy(
                        src_ref=g.part_of_halves(s_ref, k), dst_ref=r_ref.at[j], send_sem=send_sems.at[3 * i + j],
                        recv_sem=recv_sems.at[3 * i + j], device_id=(x, y, c), device_id_type=MESH).wait_recv()
            for cp in sends:
                cp.wait_send()

        _on_each_place(x, y, c, at_place, by_core=False)

    return _on_sequencer(body, sums, [jax.ShapeDtypeStruct((len(FLIPS),) + g.half, WIRE_DTYPE) for g in geoms],
                         len(FLIPS) * len(sums), _same_core_of_other_chips, name, collective_id)


def _share_reduced_halves(reduced, geoms, name, collective_id):
    def body(out_refs, _, send_sems, recv_sems, x, y, c):

        def at_place(_, cc):
            sends = []
            for i, (g, ref) in enumerate(zip(geoms, out_refs)):
                mine = g.half_of_shard(ref, cc)
                cp = pltpu.make_async_remote_copy(src_ref=mine, dst_ref=mine, send_sem=send_sems.at[i],
                                                  recv_sem=recv_sems.at[i], device_id=(x, y, 1 - cc), device_id_type=MESH)
                cp.start()
                sends.append(cp)
            for i, (g, ref) in enumerate(zip(geoms, out_refs)):
                theirs = g.half_of_shard(ref, 1 - cc)
                pltpu.make_async_remote_copy(src_ref=theirs, dst_ref=theirs, send_sem=send_sems.at[i],
                                             recv_sem=recv_sems.at[i], device_id=(x, y, cc), device_id_type=MESH).wait_recv()
            for cp in sends:
                cp.wait_send()

        _on_each_place(x, y, c, at_place, by_chip=False)

    return _on_sequencer(body, reduced, [], len(reduced), _sibling, name, collective_id, return_inputs=True)


def _chip_sum(place, grad, theirs, g, name):
    RH, C = theirs.shape
    h = g.half_rows
    tr = _tile(h, 256, 16)
    tc = _tile(C, 2048)
    per_half = h // tr

    if g.by_cols:
        grad_map = lambda i, j, p: (p[1] * per_half + i, j)
    else:
        grad_map = lambda i, j, p: ((i // per_half) * 2 * per_half + p[1] * per_half + i % per_half, j)

    def body(p_ref, a_ref, b_ref, f_ref, o_ref):
        total = a_ref[...] + b_ref[...]
        f_ref[...] = total
        o_ref[...] = total.astype(o_ref.dtype)

    return pl.pallas_call(
        body, name=name,
        grid_spec=pltpu.PrefetchScalarGridSpec(
            num_scalar_prefetch=1, grid=(RH // tr, C // tc),
            in_specs=[pl.BlockSpec((tr, tc), grad_map), pl.BlockSpec((tr, tc), lambda i, j, p: (i, j))],
            out_specs=[pl.BlockSpec((tr, tc), lambda i, j, p: (i, j))] * 2),
        out_shape=[jax.ShapeDtypeStruct((RH, C), F32), jax.ShapeDtypeStruct((RH, C), WIRE_DTYPE)],
        compiler_params=_params("parallel", "parallel"),
    )(place, grad, theirs)


def _dw_half(place, a, b, g, mine, name, add=None):
    K, R = a.shape
    C = b.shape[1]
    h = g.half_rows
    tm, tn = _tile(h, 1024, 16), _tile(C, 512)
    per_half = h // tm
    n_i = (R // 2) // tm

    def a_map(i, j, p):
        hc = p[1] if mine else 1 - p[1]
        if g.by_cols:
            return 0, hc * n_i + i
        return 0, (i // per_half) * 2 * per_half + hc * per_half + i % per_half

    mn_spec = pl.BlockSpec((tm, tn), lambda i, j, p: (i, j))

    def body(p_ref, a_ref, b_ref, *rest):
        acc = _dot(a_ref[...], b_ref[...], TN)
        if add is None:
            rest[0][...] = acc
        else:
            total = acc + rest[0][...]
            rest[1][...] = total
            rest[2][...] = total.astype(rest[2].dtype)

    out_shape = [jax.ShapeDtypeStruct((R // 2, C), F32)] + ([] if add is None else [jax.ShapeDtypeStruct((R // 2, C), WIRE_DTYPE)])
    return pl.pallas_call(
        body, name=name,
        grid_spec=pltpu.PrefetchScalarGridSpec(
            num_scalar_prefetch=1, grid=(n_i, C // tn),
            in_specs=[pl.BlockSpec((K, tm), a_map), pl.BlockSpec((K, tn), lambda i, j, p: (0, j))] + ([] if add is None else [mn_spec]),
            out_specs=[mn_spec] * len(out_shape)),
        out_shape=out_shape,
        compiler_params=_params("parallel", "arbitrary"),
    )(place, a, b, *([] if add is None else [add]))


def _reduce_half(place, sums, others, g, name):
    h, tc = g.half
    tr = _tile(h, 256, 16)
    per_half = h // tr
    sums_map = (lambda i, p: (i, p[0])) if g.by_cols else (lambda i, p: (p[0] * per_half + i, 0))

    def body(p_ref, s_ref, o0_ref, o1_ref, o2_ref, out_ref):
        acc = s_ref[...]
        for o_ref in (o0_ref, o1_ref, o2_ref):
            acc = acc + o_ref[...].astype(F32)
        out_ref[...] = acc

    other_specs = [pl.BlockSpec((None, tr, tc), functools.partial(lambda i, p, j: (j, i, 0), j=j)) for j in range(len(FLIPS))]
    return pl.pallas_call(
        body, name=name,
        grid_spec=pltpu.PrefetchScalarGridSpec(
            num_scalar_prefetch=1, grid=(per_half,),
            in_specs=[pl.BlockSpec((tr, tc), sums_map)] + other_specs,
            out_specs=pl.BlockSpec((tr, tc), lambda i, p: (p[1] * per_half + i, 0))),
        out_shape=jax.ShapeDtypeStruct(g.shard, F32),
        compiler_params=_params("arbitrary"),
    )(place, sums, others, others, others)


SLAB_ROW_UNIT = 256
SMALL = ("b_ada", "norm1_g", "v_norm_g", "w_spatial", "b_spatial", "out_norm_g", "norm2_g", "final_g")
BIG = ("w_in", "w_out", "w_gate", "w_up", "w_down")
BY_COLS = {"w_in": True, "w_out": False, "w_gate": True, "w_up": True, "w_down": False}
ORDER = ("w_ada", "b_ada", "norm1_g", "w_in", "v_norm_g", "w_spatial", "b_spatial", "out_norm_g", "w_out",
         "norm2_g", "w_gate", "w_up", "w_down", "final_g")


def _pack(parts):
    return jnp.concatenate([parts[n].reshape(-1) for n in SMALL]).reshape(-1, LANE)


def _adamw_small(w, g, m, v, shapes, name):
    R = w.shape[0]
    slab_spec = pl.BlockSpec((R, LANE), lambda: (0, 0))
    out_shapes = [shapes[n] if len(shapes[n]) > 1 else (1,) + tuple(shapes[n]) for n in SMALL]

    def body(w_ref, g_ref, m_ref, v_ref, *out_refs):
        gv = g_ref[...]
        results = (gv,) + _adamw_math(w_ref[...], gv, m_ref[...], v_ref[...])
        for kind, val in enumerate(results):
            at = 0
            for i, shp in enumerate(out_shapes):
                o_ref = out_refs[kind * len(SMALL) + i]
                n_rows = math.prod(shp) // LANE
                if len(shp) == 2:
                    for r in range(n_rows):
                        o_ref[:, r * LANE:(r + 1) * LANE] = val[at + r:at + r + 1, :]
                else:
                    o_ref[0] = val[at:at + n_rows, :].reshape(shp[1:])
                at += n_rows

    outs = pl.pallas_call(
        body, name=name, in_specs=[slab_spec] * 4,
        out_specs=[pl.BlockSpec(shp, functools.partial(lambda nd: (0,) * nd, len(shp))) for shp in out_shapes] * 4,
        out_shape=[jax.ShapeDtypeStruct(shp, F32) for shp in out_shapes] * 4,
        compiler_params=_params(),
    )(w, g, m, v)
    dicts = []
    for kind in range(4):
        part = outs[kind * len(SMALL):(kind + 1) * len(SMALL)]
        dicts.append({n: a.reshape(shapes[n]) for n, a in zip(SMALL, part)})
    return dicts


def kernel(x, c, w_ada, b_ada, norm1_g, w_in, v_norm_g, w_spatial, b_spatial, out_norm_g, w_out, norm2_g, w_gate, w_up, w_down, final_g, loss_target, m_w_ada, m_b_ada, m_norm1_g, m_w_in, m_v_norm_g, m_w_spatial, m_b_spatial, m_out_norm_g, m_w_out, m_norm2_g, m_w_gate, m_w_up, m_w_down, m_final_g, v_w_ada, v_b_ada, v_norm1_g, v_w_in, v_v_norm_g, v_w_spatial, v_b_spatial, v_out_norm_g, v_w_out, v_norm2_g, v_w_gate, v_w_up, v_w_down, v_final_g):
    weights = dict(w_ada=w_ada, b_ada=b_ada, norm1_g=norm1_g, w_in=w_in, v_norm_g=v_norm_g, w_spatial=w_spatial,
                   b_spatial=b_spatial, out_norm_g=out_norm_g, w_out=w_out, norm2_g=norm2_g, w_gate=w_gate, w_up=w_up,
                   w_down=w_down, final_g=final_g)
    m_in = dict(w_ada=m_w_ada, b_ada=m_b_ada, norm1_g=m_norm1_g, w_in=m_w_in, v_norm_g=m_v_norm_g, w_spatial=m_w_spatial,
                b_spatial=m_b_spatial, out_norm_g=m_out_norm_g, w_out=m_w_out, norm2_g=m_norm2_g, w_gate=m_w_gate,
                w_up=m_w_up, w_down=m_w_down, final_g=m_final_g)
    v_in = dict(w_ada=v_w_ada, b_ada=v_b_ada, norm1_g=v_norm1_g, w_in=v_w_in, v_norm_g=v_v_norm_g, w_spatial=v_w_spatial,
                b_spatial=v_b_spatial, out_norm_g=v_out_norm_g, w_out=v_w_out, norm2_g=v_norm2_g, w_gate=v_w_gate,
                w_up=v_w_up, w_down=v_w_down, final_g=v_final_g)

    S, D = x.shape[1], x.shape[2]
    n_g = v_norm_g.shape[-1] // LANE
    n_h = (D - n_g * LANE) // LANE
    GW = n_g * LANE
    xi, yi, ci = _place()
    chip = 2 * xi + yi
    me = 4 * xi + 2 * yi + ci
    place = jnp.stack([chip, ci]).astype(jnp.int32)

    xs, target = x[0], loss_target[0]
    geoms = [_Sharded(weights[n].shape[1:], BY_COLS[n]) for n in BIG]

    full = {}
    for i, group in enumerate((("w_in",), ("w_out",), ("w_gate", "w_up"), ("w_down",))):
        gg = [geoms[BIG.index(n)] for n in group]
        own = [_cast_into_full(place, weights[n][0], g, "cast_" + n) for n, g in zip(group, gg)]
        gathered = _gather_weights(own, gg, "gather_" + "_".join(group), 1 + i)
        full.update(zip(group, gathered))

    c_pad = jnp.concatenate([c, jnp.zeros((7, D), F32)], axis=0)
    c_all = _allgather8(c_pad, "gather_c")[::8]
    n_ada = w_ada.shape[2]
    b_cols = lax.dynamic_slice(b_ada, (0, chip * n_ada), (1, n_ada))
    mod_parts = _allgather8(_mod_part(c_all, w_ada[0], b_cols, "mod_part"), "gather_mod")
    mod_all = mod_parts.reshape(N_CHIPS, 2, 8, n_ada)[:, 0].transpose(1, 0, 2).reshape(8, N_CHIPS * n_ada)
    mod = lax.dynamic_slice(mod_all, (me, 0), (1, 6 * D))
    shift1, scale1, gate1, shift2, scale2, gate2 = [mod[:, i * D:(i + 1) * D] for i in range(6)]

    b_t = b_spatial[0].T
    h1 = _norm_mod(xs, norm1_g, scale1, shift1, "norm1")
    proj, = _mm("nn", h1, full["w_in"], [F32], "proj")
    on_gm = _gmlp_fwd(proj, v_norm_g, w_spatial[0], b_t, out_norm_g, n_g, "gmlp_fwd")
    o_sb, o_n, l_sum = _sb_fwd(proj, out_norm_g, on_gm, n_g, n_h, "sb_fwd")
    attn, = _mm("nn", o_n, full["w_out"], [F32], "attn_out")
    x1, h2 = _residual_norm_mod(xs, attn, gate1, norm2_g, scale2, shift2, "norm2")
    a_g, a_u, f_in = _gate_up(h2, full["w_gate"], full["w_up"], "gate_up")
    f, = _mm("nn", f_in, full["w_down"], [F32], "down", tm=1024)
    dx2, df, d_gate2, d_final_g, loss_part = _final_loss_bwd(x1, f, gate2, final_g.reshape(1, D), target, "final")

    geom_of = dict(zip(BIG, geoms))
    grad_out, delta, new_m, new_v = {}, {}, {}, {}

    def theirs_first(group, operands, collective_id, after=None):
        outs = []
        for n, (a_op, b_op) in zip(group, operands):
            outs.append(_dw_half(place, a_op, b_op if after is None else _then(after, b_op), geom_of[n], False, "d_" + n + "_theirs")[0])
            after = outs[-1]
        return outs, _send_to_sibling(outs, "swap_" + "_".join(group), collective_id)

    def chip_sums(group, operands, theirs, after):
        f32s, wires = [], []
        for n, (a_op, b_op), t in zip(group, operands, theirs):
            sf, sw = _dw_half(place, a_op, b_op, geom_of[n], True, "d_" + n + "_mine", add=_then(after, t))
            f32s.append(sf)
            wires.append(sw)
            after = sw
        return f32s, wires

    def scatter(group, sums, collective_id):
        return _scatter_chip_sums(sums, [geom_of[n] for n in group], "scatter_" + "_".join(group), collective_id)

    def reduce_halves(group, sums, others, after):
        return [_reduce_half(place, sf, _then(after, o), geom_of[n], "reduce_" + n) for n, sf, o in zip(group, sums, others)]

    def share(group, halves, collective_id):
        return _share_reduced_halves(halves, [geom_of[n] for n in group], "share_" + "_".join(group), collective_id)

    def adamw(group, reduced, after):
        for n, r in zip(group, reduced):
            go, d, mo, vo = _adamw(weights[n][0], _then(after, r), m_in[n][0], v_in[n][0], "adamw_" + n)
            grad_out[n], delta[n], new_m[n], new_v[n] = go[None], d[None], mo[None], vo[None]
        return d

    g_down = ("w_down",)
    g_ffn = ("w_gate", "w_up")
    g_out = ("w_out",)
    g_in = ("w_in",)

    gr_down, = _mm("tn", f_in, df, [F32], "d_w_down", tm=1408, tn=1024)
    th_down, = _swap_core_halves([gr_down], [geom_of["w_down"]], "swap_w_down", 6)
    d_ag, d_au = _mm("nt", df, full["w_down"], [MXU_DTYPE, MXU_DTYPE], "d_ffn_in", extras=(a_g, a_u),
                     epilogue=_swiglu_bwd_epilogue)
    sf_down, sw_down = [[t] for t in _chip_sum(place, gr_down, _then(d_ag, th_down), geom_of["w_down"], "chip_sum_w_down")]
    ot_down = scatter(g_down, sw_down, 7)
    sent, th_ffn = theirs_first(g_ffn, [(h2, d_ag), (h2, d_au)], 9, after=sw_down)
    dh2 = _mm_ktiled("nt", [(_then(sent, d_ag), full["w_gate"]), (d_au, full["w_up"])], "d_h2", tn=512)
    sf_ffn, sw_ffn = chip_sums(g_ffn, [(h2, d_ag), (h2, d_au)], th_ffn, after=dh2)
    ot_ffn = scatter(g_ffn, sw_ffn, 10)
    hv_down = reduce_halves(g_down, sf_down, ot_down, after=sw_ffn)
    rd_down = share(g_down, hv_down, 8)
    dx1, d_shift2, d_scale2, d_norm2_g, d_gate1, d_attn = _norm_mod_bwd(
        _then(hv_down, dh2), x1, dx2, norm2_g, scale2, "norm2_bwd", branch=attn, gate=gate1)
    gr_out, = _mm("tn", o_n, d_attn, [F32], "d_w_out")
    th_out, = _swap_core_halves([gr_out], [geom_of["w_out"]], "swap_w_out", 12)
    d_on, = _mm("nt", _then(gr_out, d_attn), full["w_out"], [F32], "d_o")
    dp_gm, d_w_spatial, d_b_t, d_v_norm_g, d_og_gm = _gmlp_bwd(proj, d_on, v_norm_g, w_spatial[0], b_t, out_norm_g, n_g, "gmlp_bwd")
    sf_out, sw_out = [[t] for t in _chip_sum(place, gr_out, _then(dp_gm, th_out), geom_of["w_out"], "chip_sum_w_out")]
    ot_out = scatter(g_out, sw_out, 13)
    dproj, d_og_sb = _sb_bwd(proj, o_sb, l_sum, _then(sw_out, d_on), out_norm_g, dp_gm, n_g, n_h, "sb_bwd")
    sent, th_in = theirs_first(g_in, [(h1, dproj)], 15)
    hv_ffn = reduce_halves(g_ffn, sf_ffn, ot_ffn, after=sent)
    rd_ffn = share(g_ffn, hv_ffn, 11)
    dh1, = _mm("nt", _then(sent, dproj), full["w_in"], [F32], "d_h1", tm=1024)
    hv_out = reduce_halves(g_out, sf_out, ot_out, after=dh1)
    rd_out = share(g_out, hv_out, 14)
    grad_x, d_shift1, d_scale1, d_norm1_g = _norm_mod_bwd(_then(hv_out, dh1), xs, dx1, norm1_g, scale1, "norm1_bwd")

    dmod = jnp.concatenate([d_shift1, d_scale1, d_gate1, d_shift2, d_scale2, d_gate2], axis=1)
    small_parts = dict(b_ada=dmod, norm1_g=d_norm1_g, v_norm_g=d_v_norm_g, w_spatial=d_w_spatial, b_spatial=d_b_t.T,
                       out_norm_g=jnp.concatenate([d_og_gm, d_og_sb], axis=1), norm2_g=d_norm2_g, final_g=d_final_g)
    packed = _pack(small_parts)
    small_rows = packed.shape[0]
    rows = -(-(small_rows + 8) // SLAB_ROW_UNIT) * SLAB_ROW_UNIT
    slab = _then(grad_x, jnp.concatenate([packed, jnp.broadcast_to(loss_part, (8, LANE)),
                                          jnp.zeros((rows - small_rows - 8, LANE), F32)], axis=0))
    gathered = _allgather8_on_sequencer(slab, "gather_small", 18)
    sf_in, sw_in = chip_sums(g_in, [(h1, dproj)], th_in, after=slab)
    ot_in = scatter(g_in, sw_in, 16)
    done = adamw(g_down, rd_down, after=sw_in)
    done = adamw(g_ffn, rd_ffn, after=done)
    done = adamw(g_out, rd_out, after=done)
    gathered = _then(done, gathered)
    small_shapes = {n: weights[n].shape for n in SMALL}
    slab_sum = _sum_devices(gathered, 8, "sum_small")
    small_sum, loss = slab_sum[:small_rows], slab_sum[small_rows, 0]
    dmod_all = gathered.reshape(8, rows * LANE)[:, :6 * D]
    dmod_cols = lax.dynamic_slice(dmod_all, (0, chip * n_ada), (8, n_ada))
    g_ada, d, mo, vo = _adamw_ada(c_all, dmod_cols, w_ada[0], m_w_ada[0], v_w_ada[0], "adamw_w_ada")
    grad_out["w_ada"], delta["w_ada"], new_m["w_ada"], new_v["w_ada"] = g_ada[None], d[None], mo[None], vo[None]
    small_out = _adamw_small(_pack({n: weights[n] for n in SMALL}), small_sum, _pack({n: m_in[n] for n in SMALL}),
                             _pack({n: v_in[n] for n in SMALL}), small_shapes, "adamw_small")
    for dst, part in zip((grad_out, delta, new_m, new_v), small_out):
        dst.update(part)
    hv_in = reduce_halves(g_in, sf_in, ot_in, after=d)
    adamw(g_in, share(g_in, hv_in, 17), after=d)

    return (loss, grad_x[None], *[grad_out[n] for n in ORDER], *[delta[n] for n in ORDER],
            *[new_m[n] for n in ORDER], *[new_v[n] for n in ORDER])
```

```python
import functools
import math

import jax
import jax.numpy as jnp
from jax import lax
from jax.experimental import pallas as pl
from jax.experimental.pallas import tpu as pltpu
from jax.experimental.pallas import tpu_sc as plsc

F32 = jnp.float32
MXU_DTYPE = jnp.bfloat16
WIRE_DTYPE = jnp.bfloat16
EPS = 1e-6
LANE = 128
V7X_VMEM_LIMIT = 56 * 1024 * 1024
MESH = pl.DeviceIdType.MESH
N_CHIPS = 4
FLIPS = (2, 1, 3)

ADAM_LR = 0.001
ADAM_B1 = 0.9
ADAM_B2 = 0.999
ADAM_EPS = 1e-08
ADAM_WD = 0.01
ADAM_STEP = 10


def _params(*semantics):
    return pltpu.CompilerParams(dimension_semantics=semantics or None, vmem_limit_bytes=V7X_VMEM_LIMIT)


def _tile(dim, pref, unit=LANE):
    best = None
    t = unit
    while t <= min(dim, pref):
        if dim % t == 0:
            best = t
        t += unit
    return best if best is not None else dim


def _then(first, second):
    return lax.optimization_barrier((first, second))[1]


def _sum0(v):
    return jnp.sum(v, axis=0, keepdims=True)


def _mean1(v):
    return jnp.mean(v, axis=-1, keepdims=True)


def _gelu(x):
    return 0.5 * x * (1.0 + lax.erf(x * (1.0 / math.sqrt(2.0))))


def _gelu_grad(x):
    cdf = 0.5 * (1.0 + lax.erf(x * (1.0 / math.sqrt(2.0))))
    return cdf + x * jnp.exp(-0.5 * x * x) * (1.0 / math.sqrt(2.0 * math.pi))


def _dot(a, b, dims):
    return lax.dot_general(a, b, (dims, ((), ())), preferred_element_type=F32)


NN = ((1,), (0,))
NT = ((1,), (1,))
TN = ((0,), (0,))


def _mm(kind, a, b, out_dtypes, name, tm=2048, tn=512, extras=(), epilogue=None):
    if kind == "nn":
        (M, K), N = a.shape, b.shape[1]
    elif kind == "nt":
        (M, K), N = a.shape, b.shape[0]
    else:
        (K, M), N = a.shape, b.shape[1]
    tm, tn = _tile(M, tm), _tile(N, tn)
    a_spec = pl.BlockSpec((K, tm), lambda i, j: (0, i)) if kind == "tn" else pl.BlockSpec((tm, K), lambda i, j: (i, 0))
    b_spec = pl.BlockSpec((tn, K), lambda i, j: (j, 0)) if kind == "nt" else pl.BlockSpec((K, tn), lambda i, j: (0, j))
    mn_spec = pl.BlockSpec((tm, tn), lambda i, j: (i, j))
    dims = {"nn": NN, "nt": NT, "tn": TN}[kind]
    n_extra = len(extras)

    n_chunks = 1 if epilogue is None or kind == "tn" else max(1, tm // 256)
    rows_per = tm // n_chunks

    def body(a_ref, b_ref, *rest):
        for r in range(n_chunks):
            rows = slice(r * rows_per, (r + 1) * rows_per)
            acc = _dot(a_ref[...] if n_chunks == 1 else a_ref[rows, :], b_ref[...], dims)
            res = (acc,) if epilogue is None else epilogue(acc, *[e[rows, :] for e in rest[:n_extra]])
            for o_ref, val in zip(rest[n_extra:], res):
                o_ref[rows, :] = val.astype(o_ref.dtype)

    outs = pl.pallas_call(
        body, name=name, grid=(M // tm, N // tn),
        in_specs=[a_spec, b_spec] + [mn_spec] * n_extra,
        out_specs=[mn_spec] * len(out_dtypes),
        out_shape=[jax.ShapeDtypeStruct((M, N), d) for d in out_dtypes],
        compiler_params=_params("parallel", "arbitrary"),
    )(a, b, *extras)
    return outs


def _mm_ktiled(kind, pairs, name, tm=2048, tn=1024, tk=1408):
    a0, b0 = pairs[0]
    M, K = a0.shape
    N = b0.shape[1] if kind == "nn" else b0.shape[0]
    tm, tn, tk = _tile(M, tm), _tile(N, tn), _tile(K, tk)
    a_spec = pl.BlockSpec((tm, tk), lambda i, j, k: (i, k))
    b_spec = pl.BlockSpec((tk, tn), lambda i, j, k: (k, j)) if kind == "nn" else pl.BlockSpec((tn, tk), lambda i, j, k: (j, k))
    dims = NN if kind == "nn" else NT
    n_pairs = len(pairs)

    def body(*refs):
        o_ref = refs[2 * n_pairs]
        acc = _dot(refs[0][...], refs[1][...], dims)
        for p in range(1, n_pairs):
            acc = acc + _dot(refs[2 * p][...], refs[2 * p + 1][...], dims)

        @pl.when(pl.program_id(2) == 0)
        def _():
            o_ref[...] = acc

        @pl.when(pl.program_id(2) != 0)
        def _():
            o_ref[...] += acc

    return pl.pallas_call(
        body, name=name, grid=(M // tm, N // tn, K // tk),
        in_specs=[a_spec, b_spec] * n_pairs,
        out_specs=pl.BlockSpec((tm, tn), lambda i, j, k: (i, j)),
        out_shape=jax.ShapeDtypeStruct((M, N), F32),
        compiler_params=_params("parallel", "parallel", "arbitrary"),
    )(*[x for pair in pairs for x in pair])


def _gate_up(h, wg, wu, name):
    (M, K), N = h.shape, wg.shape[1]
    tm, tn = _tile(M, 2048), _tile(N, 512)

    n_chunks = max(1, tm // 256)
    rows_per = tm // n_chunks

    def body(h_ref, wg_ref, wu_ref, ag_ref, au_ref, f_ref):
        for r in range(n_chunks):
            rows = slice(r * rows_per, (r + 1) * rows_per)
            hv = h_ref[rows, :]
            ag = _dot(hv, wg_ref[...], NN)
            au = _dot(hv, wu_ref[...], NN)
            ag_ref[rows, :] = ag.astype(ag_ref.dtype)
            au_ref[rows, :] = au.astype(au_ref.dtype)
            f_ref[rows, :] = (ag * jax.nn.sigmoid(ag) * au).astype(f_ref.dtype)

    w_spec = pl.BlockSpec((K, tn), lambda i, j: (0, j))
    mn_spec = pl.BlockSpec((tm, tn), lambda i, j: (i, j))
    return pl.pallas_call(
        body, name=name, grid=(M // tm, N // tn),
        in_specs=[pl.BlockSpec((tm, K), lambda i, j: (i, 0)), w_spec, w_spec],
        out_specs=[mn_spec] * 3,
        out_shape=[jax.ShapeDtypeStruct((M, N), MXU_DTYPE)] * 3,
        compiler_params=_params("parallel", "arbitrary"),
    )(h, wg, wu)


def _swiglu_bwd_epilogue(dfin, ag, au):
    ag, au = ag.astype(F32), au.astype(F32)
    sg = jax.nn.sigmoid(ag)
    d_au = dfin * (ag * sg)
    d_ag = dfin * au * (sg * (1.0 + ag * (1.0 - sg)))
    return d_ag, d_au


def _row_specs(ts, width):
    return pl.BlockSpec((ts, width), lambda i: (i, 0)), pl.BlockSpec((1, width), lambda i: (0, 0))


def _cast_into_full(place, shard, g, name):
    R, C = shard.shape
    tr = _tile(R, 256, 16)
    n_blk = R // tr
    out_map = (lambda i, p: (i, p[0])) if g.by_cols else (lambda i, p: (p[0] * n_blk + i, 0))

    def body(p_ref, a_ref, o_ref):
        o_ref[...] = a_ref[...].astype(o_ref.dtype)

    return pl.pallas_call(
        body, name=name,
        grid_spec=pltpu.PrefetchScalarGridSpec(
            num_scalar_prefetch=1, grid=(n_blk,),
            in_specs=[pl.BlockSpec((tr, C), lambda i, p: (i, 0))],
            out_specs=pl.BlockSpec((tr, C), out_map)),
        out_shape=jax.ShapeDtypeStruct(g.full, WIRE_DTYPE),
        compiler_params=_params("arbitrary"),
    )(place, shard)


def _norm_mod(x, g, scale, shift, name):
    S, D = x.shape
    ts = _tile(S, 256, 16)
    tile, vec = _row_specs(ts, D)

    def body(x_ref, g_ref, sc_ref, sh_ref, h_ref):
        xv = x_ref[...]
        r = lax.rsqrt(_mean1(xv * xv) + EPS)
        h_ref[...] = ((xv * r) * g_ref[...] * (1.0 + sc_ref[...]) + sh_ref[...]).astype(h_ref.dtype)

    return pl.pallas_call(body, name=name, grid=(S // ts,), in_specs=[tile, vec, vec, vec], out_specs=tile,
                          out_shape=jax.ShapeDtypeStruct((S, D), MXU_DTYPE), compiler_params=_params("parallel"))(x, g, scale, shift)


def _residual_norm_mod(x, attn, gate, g, scale, shift, name):
    S, D = x.shape
    ts = _tile(S, 256, 16)
    tile, vec = _row_specs(ts, D)

    def body(x_ref, a_ref, gate_ref, g_ref, sc_ref, sh_ref, x1_ref, h_ref):
        x1 = x_ref[...] + gate_ref[...] * a_ref[...]
        x1_ref[...] = x1
        r = lax.rsqrt(_mean1(x1 * x1) + EPS)
        h_ref[...] = ((x1 * r) * g_ref[...] * (1.0 + sc_ref[...]) + sh_ref[...]).astype(h_ref.dtype)

    return pl.pallas_call(body, name=name, grid=(S // ts,), in_specs=[tile, tile, vec, vec, vec, vec],
                          out_specs=[tile, tile],
                          out_shape=[jax.ShapeDtypeStruct((S, D), F32), jax.ShapeDtypeStruct((S, D), MXU_DTYPE)],
                          compiler_params=_params("parallel"))(x, attn, gate, g, scale, shift)


def _final_loss_bwd(x1, f, gate2, final_g, target, name):
    S, D = x1.shape
    ts = _tile(S, 256, 16)
    tile, vec = _row_specs(ts, D)
    loss_spec = pl.BlockSpec((1, LANE), lambda i: (0, 0))

    def body(x1_ref, f_ref, gate_ref, g_ref, t_ref, dx2_ref, df_ref, dgate_ref, dg_ref, loss_ref):
        @pl.when(pl.program_id(0) == 0)
        def _():
            dgate_ref[...] = jnp.zeros_like(dgate_ref)
            dg_ref[...] = jnp.zeros_like(dg_ref)
            loss_ref[...] = jnp.zeros_like(loss_ref)

        fv, gate, g = f_ref[...], gate_ref[...], g_ref[...]
        x2 = x1_ref[...] + gate * fv
        r = lax.rsqrt(_mean1(x2 * x2) + EPS)
        xn = x2 * r
        err = xn * g - t_ref[...]
        loss_ref[...] += jnp.broadcast_to(0.5 * _sum0(_mean1(err * err)), loss_ref.shape)
        dy = err * (1.0 / D)
        dg_ref[...] += _sum0(dy * xn)
        dxn = dy * g
        dx2 = r * (dxn - xn * _mean1(dxn * xn))
        dx2_ref[...] = dx2
        dgate_ref[...] += _sum0(dx2 * fv)
        df_ref[...] = (dx2 * gate).astype(df_ref.dtype)

    return pl.pallas_call(
        body, name=name, grid=(S // ts,), in_specs=[tile, tile, vec, vec, tile],
        out_specs=[tile, tile, vec, vec, loss_spec],
        out_shape=[jax.ShapeDtypeStruct((S, D), F32), jax.ShapeDtypeStruct((S, D), MXU_DTYPE),
                   jax.ShapeDtypeStruct((1, D), F32), jax.ShapeDtypeStruct((1, D), F32),
                   jax.ShapeDtypeStruct((1, LANE), F32)],
        compiler_params=_params("arbitrary"),
    )(x1, f, gate2, final_g, target)


def _norm_mod_bwd(dh, xin, dres, g, scale, name, branch=None, gate=None):
    S, D = xin.shape
    ts = _tile(S, 256, 16)
    tile, vec = _row_specs(ts, D)
    with_gate = branch is not None

    def body(*refs):
        if with_gate:
            dh_ref, x_ref, dres_ref, g_ref, sc_ref, br_ref, gate_ref, dx_ref, dshift_ref, dscale_ref, dg_ref, dgate_ref, dbr_ref = refs
            accs = (dshift_ref, dscale_ref, dg_ref, dgate_ref)
        else:
            dh_ref, x_ref, dres_ref, g_ref, sc_ref, dx_ref, dshift_ref, dscale_ref, dg_ref = refs
            accs = (dshift_ref, dscale_ref, dg_ref)

        @pl.when(pl.program_id(0) == 0)
        def _():
            for acc in accs:
                acc[...] = jnp.zeros_like(acc)

        dh_v, xv, g_v = dh_ref[...], x_ref[...], g_ref[...]
        one_sc = 1.0 + sc_ref[...]
        r = lax.rsqrt(_mean1(xv * xv) + EPS)
        xn = xv * r
        dshift_ref[...] += _sum0(dh_v)
        dscale_ref[...] += _sum0(dh_v * (xn * g_v))
        dg_ref[...] += _sum0(dh_v * one_sc * xn)
        dxn = dh_v * (g_v * one_sc)
        dx = dres_ref[...] + r * (dxn - xn * _mean1(dxn * xn))
        dx_ref[...] = dx
        if with_gate:
            dgate_ref[...] += _sum0(dx * br_ref[...])
            dbr_ref[...] = (dx * gate_ref[...]).astype(dbr_ref.dtype)

    ins = [dh, xin, dres, g, scale] + ([branch, gate] if with_gate else [])
    in_specs = [tile, tile, tile, vec, vec] + ([tile, vec] if with_gate else [])
    out_specs = [tile, vec, vec, vec] + ([vec, tile] if with_gate else [])
    out_shape = [jax.ShapeDtypeStruct((S, D), F32)] + [jax.ShapeDtypeStruct((1, D), F32)] * 3
    if with_gate:
        out_shape += [jax.ShapeDtypeStruct((1, D), F32), jax.ShapeDtypeStruct((S, D), MXU_DTYPE)]
    return pl.pallas_call(body, name=name, grid=(S // ts,), in_specs=in_specs, out_specs=out_specs,
                          out_shape=out_shape, compiler_params=_params("arbitrary"))(*ins)


def _causal_weights(ws_ref, wt_ref, n_g):
    row = lax.broadcasted_iota(jnp.int32, (LANE, LANE), 0)
    col = lax.broadcasted_iota(jnp.int32, (LANE, LANE), 1)
    for g in range(n_g):
        wt_ref[g] = jnp.where(col <= row, ws_ref[g], 0.0).astype(wt_ref.dtype)


def _group_layernorm(v):
    xc = v - _mean1(v)
    rstd = lax.rsqrt(_mean1(xc * xc) + EPS)
    return xc * rstd, rstd


def _gmlp_fwd(proj, v_gain, w_s, b_t, out_gain, n_g, name):
    S = proj.shape[0]
    GW = n_g * LANE
    D = out_gain.shape[1]

    def body(p_ref, vg_ref, ws_ref, bt_ref, og_ref, on_ref, wt_ref):
        @pl.when(pl.program_id(0) == 0)
        def _():
            _causal_weights(ws_ref, wt_ref, n_g)

        for g in range(n_g):
            cols = slice(g * LANE, (g + 1) * LANE)
            u = _gelu(p_ref[:, cols])
            v = _gelu(p_ref[:, GW + g * LANE:GW + (g + 1) * LANE])
            vhat, _ = _group_layernorm(v)
            vln = (vhat * vg_ref[:, cols]).astype(MXU_DTYPE)
            mixed = _dot(wt_ref[g], vln, NN) + bt_ref[:, g:g + 1]
            o = u * mixed
            r = lax.rsqrt(_mean1(o * o) + EPS)
            on_ref[:, cols] = (o * r * og_ref[:, cols]).astype(on_ref.dtype)

    return pl.pallas_call(
        body, name=name, grid=(S // LANE,),
        in_specs=[pl.BlockSpec((LANE, 2 * GW), lambda n: (n, 0)),
                  pl.BlockSpec((1, GW), lambda n: (0, 0)),
                  pl.BlockSpec((n_g, LANE, LANE), lambda n: (0, 0, 0)),
                  pl.BlockSpec((LANE, n_g), lambda n: (0, 0)),
                  pl.BlockSpec((1, GW), lambda n: (0, 0))],
        out_specs=pl.BlockSpec((LANE, GW), lambda n: (n, 0)),
        out_shape=jax.ShapeDtypeStruct((S, D), MXU_DTYPE),
        scratch_shapes=[pltpu.VMEM((n_g, LANE, LANE), MXU_DTYPE)],
        compiler_params=_params("arbitrary"),
    )(proj, v_gain, w_s, b_t, out_gain)


def _gmlp_bwd(proj, d_on, v_gain, w_s, b_t, out_gain, n_g, name):
    S = proj.shape[0]
    GW = n_g * LANE

    def body(p_ref, dn_ref, vg_ref, ws_ref, bt_ref, og_ref, dp_ref, dws_ref, dbt_ref, dvg_ref, dog_ref, wt_ref):
        @pl.when(pl.program_id(0) == 0)
        def _():
            _causal_weights(ws_ref, wt_ref, n_g)
            dws_ref[...] = jnp.zeros_like(dws_ref)
            dbt_ref[...] = jnp.zeros_like(dbt_ref)
            dvg_ref[...] = jnp.zeros_like(dvg_ref)
            dog_ref[...] = jnp.zeros_like(dog_ref)

        row = lax.broadcasted_iota(jnp.int32, (LANE, LANE), 0)
        col = lax.broadcasted_iota(jnp.int32, (LANE, LANE), 1)
        for g in range(n_g):
            cols = slice(g * LANE, (g + 1) * LANE)
            vcols = slice(GW + g * LANE, GW + (g + 1) * LANE)
            pu, pv = p_ref[:, cols], p_ref[:, vcols]
            u, v = _gelu(pu), _gelu(pv)
            vhat, rstd = _group_layernorm(v)
            gain = vg_ref[:, cols]
            vln = (vhat * gain).astype(MXU_DTYPE)
            mixed = _dot(wt_ref[g], vln, NN) + bt_ref[:, g:g + 1]
            o = u * mixed
            r = lax.rsqrt(_mean1(o * o) + EPS)
            oh = o * r
            dn = dn_ref[:, cols]
            dog_ref[:, cols] += _sum0(dn * oh)
            dhn = dn * og_ref[:, cols]
            d_o = r * (dhn - oh * _mean1(dhn * oh))
            du = d_o * mixed
            dmix = d_o * u
            dbt_ref[:, g:g + 1] += jnp.sum(dmix, axis=1, keepdims=True)
            dmix_b = dmix.astype(MXU_DTYPE)
            dws_ref[g] += jnp.where(col <= row, _dot(dmix_b, vln, NT), 0.0)
            dvln = _dot(wt_ref[g], dmix_b, TN)
            dvg_ref[:, cols] += _sum0(dvln * vhat)
            dxh = dvln * gain
            dv = rstd * (dxh - _mean1(dxh) - vhat * _mean1(dxh * vhat))
            dp_ref[:, cols] = (du * _gelu_grad(pu)).astype(dp_ref.dtype)
            dp_ref[:, vcols] = (dv * _gelu_grad(pv)).astype(dp_ref.dtype)

    return pl.pallas_call(
        body, name=name, grid=(S // LANE,),
        in_specs=[pl.BlockSpec((LANE, 2 * GW), lambda n: (n, 0)),
                  pl.BlockSpec((LANE, GW), lambda n: (n, 0)),
                  pl.BlockSpec((1, GW), lambda n: (0, 0)),
                  pl.BlockSpec((n_g, LANE, LANE), lambda n: (0, 0, 0)),
                  pl.BlockSpec((LANE, n_g), lambda n: (0, 0)),
                  pl.BlockSpec((1, GW), lambda n: (0, 0))],
        out_specs=[pl.BlockSpec((LANE, 2 * GW), lambda n: (n, 0)),
                   pl.BlockSpec((n_g, LANE, LANE), lambda n: (0, 0, 0)),
                   pl.BlockSpec((LANE, n_g), lambda n: (0, 0)),
                   pl.BlockSpec((1, GW), lambda n: (0, 0)),
                   pl.BlockSpec((1, GW), lambda n: (0, 0))],
        out_shape=[jax.ShapeDtypeStruct((S, 2 * GW), MXU_DTYPE),
                   jax.ShapeDtypeStruct((n_g, LANE, LANE), F32),
                   jax.ShapeDtypeStruct((LANE, n_g), F32),
                   jax.ShapeDtypeStruct((1, GW), F32),
                   jax.ShapeDtypeStruct((1, GW), F32)],
        scratch_shapes=[pltpu.VMEM((n_g, LANE, LANE), MXU_DTYPE)],
        compiler_params=_params("arbitrary"),
    )(proj, d_on, v_gain, w_s, b_t, out_gain)


def _tri_sum(v, tri, exact=True):
    hi = v.astype(MXU_DTYPE)
    if not exact:
        return _dot(hi, tri, NN)
    lo = (v - hi.astype(F32)).astype(MXU_DTYPE)
    return _dot(hi, tri, NN) + _dot(lo, tri, NN)


def _log_sigmoids(z):
    sp = jnp.log(1.0 + jnp.exp(-jnp.abs(z)))
    return jnp.minimum(z, 0.0) - sp, jnp.minimum(-z, 0.0) - sp


def _rows(i, size):
    return pl.ds(pl.multiple_of(i * size, size), size)


SB_QUERY_TILE = 2048
SB_KEY_TILE = 256


def _sb_tiles(S):
    tq = _tile(S, SB_QUERY_TILE)
    tk = _tile(tq, SB_KEY_TILE)
    assert (tq // tk) % 2 == 0, "the key sweep takes two blocks a pass"
    return tq, tk, S // tq, tq // tk


def _triangle(n, keep):
    row = lax.broadcasted_iota(jnp.int32, (n, n), 0)
    col = lax.broadcasted_iota(jnp.int32, (n, n), 1)
    return jnp.where(keep(row, col), 1.0, 0.0).astype(MXU_DTYPE)


def _strictly_before(tq, tk, key_offset):
    row = lax.broadcasted_iota(jnp.int32, (tq, tk), 0)
    col = lax.broadcasted_iota(jnp.int32, (tq, tk), 1)
    return col + key_offset < row


def _sb_specs(S, n_g, n_h):
    base = 2 * n_g
    q_spec = pl.BlockSpec((S, LANE), lambda h: (0, base + h))
    k_spec = pl.BlockSpec((S, LANE), lambda h: (0, base + n_h + h))
    v_spec = pl.BlockSpec((S, LANE), lambda h: (0, base + 2 * n_h + h))
    gain_spec = pl.BlockSpec((1, LANE), lambda h: (0, n_g + h))
    head_spec = pl.BlockSpec((S, LANE), lambda h: (0, h))
    return q_spec, k_spec, v_spec, gain_spec, head_spec


def _sb_fwd(proj, out_gain, on_buffer, n_g, n_h, name):
    S = proj.shape[0]
    TQ, TK, NQ, KPQ = _sb_tiles(S)
    scale = LANE ** -0.5
    q_spec, k_spec, v_spec, gain_spec, head_spec = _sb_specs(S, n_g, n_h)

    def body(q_ref, k_ref, v_ref, og_ref, _, o_ref, on_ref, ls_ref, qb, kb, vb):
        qb[...] = q_ref[...].astype(MXU_DTYPE)
        kb[...] = k_ref[...].astype(MXU_DTYPE)
        vb[...] = v_ref[...].astype(MXU_DTYPE)
        after = _triangle(TK, lambda r, c: r > c)

        def block(qi, j, ctail, acc, key_offset):
            skip = key_offset or 0
            z = _dot(qi[skip:], kb[_rows(j, TK), :], NT) * scale
            lb, l1m = _log_sigmoids(z)
            if key_offset is not None:
                strict = _strictly_before(TQ - skip, TK, 0)
                l1m = jnp.where(strict, l1m, 0.0)
            a = jnp.exp(lb + ctail[skip:] + _tri_sum(l1m, after))
            if key_offset is not None:
                a = jnp.where(strict, a, 0.0)
            acc_new = acc[skip:] + _dot(a.astype(MXU_DTYPE), vb[_rows(j, TK), :], NN)
            ctail_new = ctail[skip:] + jnp.sum(l1m, axis=1, keepdims=True)
            if skip:
                ctail_new = jnp.concatenate([ctail[:skip], ctail_new], axis=0)
                acc_new = jnp.concatenate([acc[:skip], acc_new], axis=0)
            return ctail_new, acc_new

        def q_loop(i, carry):
            qi = qb[_rows(i, TQ), :]
            state = (jnp.zeros((TQ, 1), F32), jnp.zeros((TQ, LANE), F32))
            for d in reversed(range(KPQ)):
                state = block(qi, i * KPQ + d, state[0], state[1], d * TK)
            def pair(jj, st):
                st = block(qi, i * KPQ - 1 - 2 * jj, st[0], st[1], None)
                return block(qi, i * KPQ - 2 - 2 * jj, st[0], st[1], None)

            ctail, acc = lax.fori_loop(0, i * (KPQ // 2), pair, state)
            ls_ref[_rows(i, TQ), :] = jnp.broadcast_to(ctail, (TQ, LANE))
            o_ref[_rows(i, TQ), :] = acc
            r = lax.rsqrt(_mean1(acc * acc) + EPS)
            on_ref[_rows(i, TQ), :] = (acc * r * og_ref[...]).astype(on_ref.dtype)
            return carry

        lax.fori_loop(0, NQ, q_loop, 0)

    return pl.pallas_call(
        body, name=name, grid=(n_h,),
        in_specs=[q_spec, k_spec, v_spec, gain_spec, pl.BlockSpec(memory_space=pl.ANY)],
        out_specs=[head_spec, pl.BlockSpec((S, LANE), lambda h: (0, n_g + h)), head_spec],
        out_shape=[jax.ShapeDtypeStruct((S, n_h * LANE), F32), jax.ShapeDtypeStruct(on_buffer.shape, MXU_DTYPE),
                   jax.ShapeDtypeStruct((S, n_h * LANE), F32)],
        input_output_aliases={4: 1},
        scratch_shapes=[pltpu.VMEM((S, LANE), MXU_DTYPE)] * 3,
        compiler_params=_params("parallel"),
    )(proj, proj, proj, out_gain, on_buffer)


def _sb_bwd(proj, o_sb, l_sum, d_on, out_gain, n_g, n_h, name):
    S = proj.shape[0]
    TQ, TK, NQ, KPQ = _sb_tiles(S)
    scale = LANE ** -0.5
    q_spec, k_spec, v_spec, gain_spec, head_spec = _sb_specs(S, n_g, n_h)
    dn_spec = pl.BlockSpec((S, LANE), lambda h: (0, n_g + h))
    dgain_spec = pl.BlockSpec((1, LANE), lambda h: (0, h))

    def body(q_ref, k_ref, v_ref, o_ref, ls_ref, dn_ref, og_ref, dq_ref, dk_ref, dv_ref, dog_ref,
             qb, kb, vb, dob, dk_acc, dv_acc):
        qb[...] = q_ref[...].astype(MXU_DTYPE)
        kb[...] = k_ref[...].astype(MXU_DTYPE)
        vb[...] = v_ref[...].astype(MXU_DTYPE)
        o, dn = o_ref[...], dn_ref[...]
        r = lax.rsqrt(_mean1(o * o) + EPS)
        oh = o * r
        dog_ref[...] = _sum0(dn * oh)
        dhn = dn * og_ref[...]
        dob[...] = (r * (dhn - oh * _mean1(dhn * oh))).astype(MXU_DTYPE)
        dk_acc[...] = jnp.zeros_like(dk_acc)
        dv_acc[...] = jnp.zeros_like(dv_acc)

        up_to = _triangle(TK, lambda r, c: r <= c)
        before = _triangle(TK, lambda r, c: r < c)

        def block(qi, doi, ltot, j, cl, cdl, dq, key_offset):
            skip = key_offset or 0
            q_in, do_in = qi[skip:], doi[skip:]
            kj, vj = kb[_rows(j, TK), :], vb[_rows(j, TK), :]
            z = _dot(q_in, kj, NT) * scale
            lb, l1m = _log_sigmoids(z)
            if key_offset is not None:
                strict = _strictly_before(TQ - skip, TK, 0)
                l1m = jnp.where(strict, l1m, 0.0)
            a = jnp.exp(lb + (ltot[skip:] - (cl[skip:] + _tri_sum(l1m, up_to))))
            if key_offset is not None:
                a = jnp.where(strict, a, 0.0)
            dl = _dot(do_in, vj, NT) * a
            d_l1m = cdl[skip:] + _tri_sum(dl, before, exact=False)
            beta = jnp.exp(lb)
            dz = dl * (1.0 - beta) - beta * d_l1m
            if key_offset is not None:
                dz = jnp.where(strict, dz, 0.0)
            dzs = (dz * scale).astype(MXU_DTYPE)
            dk_acc[_rows(j, TK), :] += _dot(dzs, q_in, TN)
            dv_acc[_rows(j, TK), :] += _dot(a.astype(MXU_DTYPE), do_in, TN)
            cl_new = cl[skip:] + jnp.sum(l1m, axis=1, keepdims=True)
            cdl_new = cdl[skip:] + jnp.sum(dl, axis=1, keepdims=True)
            dq_new = dq[skip:] + _dot(dzs, kj, NN)
            if skip:
                cl_new = jnp.concatenate([cl[:skip], cl_new], axis=0)
                cdl_new = jnp.concatenate([cdl[:skip], cdl_new], axis=0)
                dq_new = jnp.concatenate([dq[:skip], dq_new], axis=0)
            return cl_new, cdl_new, dq_new

        def q_loop(i, carry):
            qi, doi = qb[_rows(i, TQ), :], dob[_rows(i, TQ), :]
            ltot = ls_ref[_rows(i, TQ), :][:, :1]
            zero_col = jnp.zeros((TQ, 1), F32)
            def pair(jj, st):
                st = block(qi, doi, ltot, 2 * jj, st[0], st[1], st[2], None)
                return block(qi, doi, ltot, 2 * jj + 1, st[0], st[1], st[2], None)

            state = lax.fori_loop(0, i * (KPQ // 2), pair, (zero_col, zero_col, jnp.zeros((TQ, LANE), F32)))
            for d in range(KPQ):
                state = block(qi, doi, ltot, i * KPQ + d, state[0], state[1], state[2], d * TK)
            dq_ref[_rows(i, TQ), :] = state[2].astype(dq_ref.dtype)
            return carry

        lax.fori_loop(0, NQ, q_loop, 0)
        dk_ref[...] = dk_acc[...].astype(dk_ref.dtype)
        dv_ref[...] = dv_acc[...].astype(dv_ref.dtype)

    W = n_h * LANE
    return pl.pallas_call(
        body, name=name, grid=(n_h,),
        in_specs=[q_spec, k_spec, v_spec, head_spec, head_spec, dn_spec, gain_spec],
        out_specs=[head_spec, head_spec, head_spec, dgain_spec],
        out_shape=[jax.ShapeDtypeStruct((S, W), MXU_DTYPE)] * 3 + [jax.ShapeDtypeStruct((1, W), F32)],
        scratch_shapes=[pltpu.VMEM((S, LANE), MXU_DTYPE)] * 4 + [pltpu.VMEM((S, LANE), F32)] * 2,
        compiler_params=_params("parallel"),
    )(proj, proj, proj, o_sb, l_sum, d_on, out_gain)


def _mod_part(c_all, w_ada, b_ada_cols, name):
    B, K = c_all.shape
    N = w_ada.shape[1]
    tn = _tile(N, 512)

    def body(c_ref, w_ref, b_ref, o_ref):
        cv = c_ref[...]
        ca = (cv * jax.nn.sigmoid(cv)).astype(MXU_DTYPE)
        o_ref[...] = _dot(ca, w_ref[...].astype(MXU_DTYPE), NN) + b_ref[...]

    return pl.pallas_call(
        body, name=name, grid=(N // tn,),
        in_specs=[pl.BlockSpec((B, K), lambda j: (0, 0)), pl.BlockSpec((K, tn), lambda j: (0, j)),
                  pl.BlockSpec((1, tn), lambda j: (0, j))],
        out_specs=pl.BlockSpec((B, tn), lambda j: (0, j)),
        out_shape=jax.ShapeDtypeStruct((B, N), F32), compiler_params=_params("parallel"))(c_all, w_ada, b_ada_cols)


def _adamw_math(w, g, m, v):
    m = ADAM_B1 * m + (1.0 - ADAM_B1) * g
    v = ADAM_B2 * v + (1.0 - ADAM_B2) * (g * g)
    m_hat = m / (1.0 - ADAM_B1 ** ADAM_STEP)
    v_hat = v / (1.0 - ADAM_B2 ** ADAM_STEP)
    delta = -ADAM_LR * (m_hat / (jnp.sqrt(v_hat) + ADAM_EPS) + ADAM_WD * w)
    return delta, m, v


def _adamw(w, g, m, v, name):
    R, C = w.shape
    tr = _tile(R, max(8, (1 << 19) // C), 8)
    spec = pl.BlockSpec((tr, C), lambda i: (i, 0))

    def body(w_ref, g_ref, m_ref, v_ref, go_ref, d_ref, mo_ref, vo_ref):
        g = g_ref[...]
        go_ref[...] = g
        d_ref[...], mo_ref[...], vo_ref[...] = _adamw_math(w_ref[...], g, m_ref[...], v_ref[...])

    return pl.pallas_call(body, name=name, grid=(R // tr,), in_specs=[spec] * 4, out_specs=[spec] * 4,
                          out_shape=[jax.ShapeDtypeStruct((R, C), F32)] * 4, compiler_params=_params("parallel"))(w, g, m, v)


def _adamw_ada(c_all, dmod_cols, w, m, v, name):
    K, N = w.shape
    B = c_all.shape[0]
    tk, tn = _tile(K, 512), _tile(N, 1024)
    spec = pl.BlockSpec((tk, tn), lambda i, j: (i, j))

    def body(c_ref, dm_ref, w_ref, m_ref, v_ref, g_ref, d_ref, mo_ref, vo_ref):
        cv = c_ref[...]
        ca = (cv * jax.nn.sigmoid(cv)).astype(MXU_DTYPE)
        g = _dot(ca, dm_ref[...].astype(MXU_DTYPE), TN)
        g_ref[...] = g
        d_ref[...], mo_ref[...], vo_ref[...] = _adamw_math(w_ref[...], g, m_ref[...], v_ref[...])

    return pl.pallas_call(
        body, name=name, grid=(K // tk, N // tn),
        in_specs=[pl.BlockSpec((B, tk), lambda i, j: (0, i)), pl.BlockSpec((B, tn), lambda i, j: (0, j)), spec, spec, spec],
        out_specs=[spec] * 4, out_shape=[jax.ShapeDtypeStruct((K, N), F32)] * 4,
        compiler_params=_params("parallel", "parallel"))(c_all, dmod_cols, w, m, v)


def _sum_devices(gathered, n_dev, name):
    R = gathered.shape[0] // n_dev
    C = gathered.shape[1]
    tr = _tile(R, 512, 8)
    n_blk = R // tr

    def body(*refs):
        acc = refs[0][...]
        for r in refs[1:n_dev]:
            acc = acc + r[...]
        refs[n_dev][...] = acc

    in_specs = [pl.BlockSpec((tr, C), functools.partial(lambda i, d: (d * n_blk + i, 0), d=d)) for d in range(n_dev)]
    return pl.pallas_call(body, name=name, grid=(n_blk,), in_specs=in_specs,
                          out_specs=pl.BlockSpec((tr, C), lambda i: (i, 0)),
                          out_shape=jax.ShapeDtypeStruct((R, C), F32), compiler_params=_params("parallel"))(*([gathered] * n_dev))


def _place():
    x, y, c = lax.axis_index("x"), lax.axis_index("y"), lax.axis_index("c")
    return x, y, c


def _allgather8(blk, name):
    m_per, n = blk.shape

    def body(x_ref, out_ref, send_sems, recv_sems, local_sem):
        x, y, c = _place()
        me, sibling = (x, y, c), (x, y, 1 - c)
        chips = [(1 - x, y), (x, 1 - y), (1 - x, 1 - y)]

        def rows(px, py, pc):
            return out_ref.at[pl.ds((4 * px + 2 * py + pc) * m_per, m_per), :]

        def copy(k, block, to, src=None):
            return pltpu.make_async_remote_copy(
                src_ref=rows(*block) if src is None else src, dst_ref=rows(*block),
                send_sem=send_sems.at[k], recv_sem=recv_sems.at[k], device_id=to, device_id_type=MESH)

        mine = pltpu.make_async_copy(x_ref, rows(*me), local_sem)
        mine.start()
        first = [copy(0, me, sibling, src=x_ref)]
        first += [copy(1 + j, me, (*chip, c), src=x_ref) for j, chip in enumerate(chips)]
        for cp in first:
            cp.start()
        passed = [copy(4 + j, (*chip, c), sibling) for j, chip in enumerate(chips)]
        for j, chip in enumerate(chips):
            copy(1 + j, (*chip, c), me).wait_recv()
            passed[j].start()
        copy(0, sibling, me).wait_recv()
        for j, chip in enumerate(chips):
            copy(4 + j, (*chip, 1 - c), me).wait_recv()
        for cp in first + passed:
            cp.wait_send()
        mine.wait()

    return pl.pallas_call(
        body, name=name,
        out_shape=jax.ShapeDtypeStruct((8 * m_per, n), blk.dtype),
        in_specs=[pl.BlockSpec(memory_space=pltpu.VMEM)],
        out_specs=pl.BlockSpec(memory_space=pltpu.VMEM),
        scratch_shapes=[pltpu.SemaphoreType.DMA((7,)), pltpu.SemaphoreType.DMA((7,)), pltpu.SemaphoreType.DMA],
        compiler_params=pltpu.CompilerParams(vmem_limit_bytes=V7X_VMEM_LIMIT),
    )(blk)


class _Sharded:
    def __init__(self, shard_shape, by_cols):
        r, c = shard_shape
        self.by_cols = by_cols
        self.full = (r, N_CHIPS * c) if by_cols else (N_CHIPS * r, c)
        self.shard = (r, c)
        self.half_rows = r // 2
        self.half = (r // 2, c)

    def shard_of(self, ref, k):
        r, c = self.shard
        return ref.at[:, pl.ds(k * c, c)] if self.by_cols else ref.at[pl.ds(k * r, r), :]

    def half_of(self, ref, k, hc):
        r, c = self.shard
        h = self.half_rows
        if self.by_cols:
            return ref.at[pl.ds(hc * h, h), pl.ds(k * c, c)]
        return ref.at[pl.ds(k * r + hc * h, h), :]

    def chunk_of(self, ref, k, hc, ch, n):
        r, c = self.shard
        h = self.half_rows
        q = h // n
        if self.by_cols:
            return ref.at[pl.ds(hc * h + ch * q, q), pl.ds(k * c, c)]
        return ref.at[pl.ds(k * r + hc * h + ch * q, q), :]

    def half_of_shard(self, ref, hc):
        return ref.at[pl.ds(hc * self.half_rows, self.half_rows), :]

    def part_of_halves(self, ref, k):
        r, c = self.shard
        h = self.half_rows
        return ref.at[:, pl.ds(k * c, c)] if self.by_cols else ref.at[pl.ds(k * h, h), :]


def _on_each_place(x, y, c, fn, by_chip=True, by_core=True):
    q = 2 * x + y
    for k in range(N_CHIPS if by_chip else 1):
        for cc in range(2 if by_core else 1):
            cond = None
            if by_chip:
                cond = q == k
            if by_core:
                cond = (c == cc) if cond is None else jnp.logical_and(cond, c == cc)
            pl.when(cond)(functools.partial(fn, k, cc))


def _chip_id(k, c):
    return (k // 2, k % 2, c)


def _handshake(peers):
    barrier = pltpu.get_barrier_semaphore()
    for peer in peers:
        pl.semaphore_signal(barrier, inc=1, device_id=peer, device_id_type=MESH)
    pl.semaphore_wait(barrier, len(peers))


def _on_sequencer(body, inputs, out_structs, n_copies, peers_of, name, collective_id, return_inputs=False):
    in_refs = [jax.new_ref(a, memory_space=pltpu.MemorySpace.HBM) for a in inputs]
    out_refs = [jax.empty_ref(s, memory_space=pltpu.MemorySpace.HBM) for s in out_structs]

    @pl.kernel(mesh=plsc.ScalarSubcoreMesh(axis_name="sequencer", num_cores=1), name=name,
               scratch_types=(pltpu.SemaphoreType.DMA((n_copies,)), pltpu.SemaphoreType.DMA((n_copies,))),
               compiler_params=pltpu.CompilerParams(collective_id=collective_id))
    def launch(send_sems, recv_sems):
        x, y, c = _place()
        _handshake(peers_of(x, y, c))
        body(in_refs, out_refs, send_sems, recv_sems, x, y, c)

    launch()
    return [r[...] for r in (in_refs if return_inputs else out_refs)]


def _sibling(x, y, c):
    return [(x, y, 1 - c)]


def _same_core_of_other_chips(x, y, c):
    return [(1 - x, y, c), (x, 1 - y, c), (1 - x, 1 - y, c)]


GATHER_CHUNKS = 4
GATHER_COPIES = 6 * GATHER_CHUNKS


def _place_slab(place, parts, loss_part, rows, name):
    n_parts = len(parts)

    def body(p_ref, *refs):
        loss_ref, out_ref = refs[n_parts], refs[n_parts + 1]
        at = 0
        for ref in refs[:n_parts]:
            if len(ref.shape) == 2 and ref.shape[0] == 1:
                for r in range(ref.shape[1] // LANE):
                    out_ref[at + r:at + r + 1, :] = ref[:, r * LANE:(r + 1) * LANE]
                at += ref.shape[1] // LANE
            else:
                n_rows = math.prod(ref.shape) // LANE
                out_ref[at:at + n_rows, :] = ref[...].reshape(n_rows, LANE)
                at += n_rows
        out_ref[at:at + 8, :] = jnp.broadcast_to(loss_ref[...], (8, LANE))
        out_ref[at + 8:, :] = jnp.zeros((rows - at - 8, LANE), F32)

    def whole(shape):
        return pl.BlockSpec(shape, functools.partial(lambda i, p, nd: (0,) * nd, nd=len(shape)))

    return pl.pallas_call(
        body, name=name,
        grid_spec=pltpu.PrefetchScalarGridSpec(
            num_scalar_prefetch=1, grid=(1,),
            in_specs=[whole(a.shape) for a in parts] + [whole(loss_part.shape)],
            out_specs=pl.BlockSpec((rows, LANE), lambda i, p: (2 * p[0] + p[1], 0))),
        out_shape=jax.ShapeDtypeStruct((8 * rows, LANE), F32),
        compiler_params=_params(),
    )(place, *parts, loss_part)


def _allgather8_on_sequencer(placed, m_per, name, collective_id):
    def body(refs, _, send_sems, recv_sems, x, y, c):
        out_ref, = refs

        def at_place(k, cc):
            def rows(kk, pc):
                return out_ref.at[pl.ds((2 * kk + pc) * m_per, m_per), :]

            def copy(slot, block, to):
                return pltpu.make_async_remote_copy(src_ref=rows(*block), dst_ref=rows(*block), send_sem=send_sems.at[slot],
                                                    recv_sem=recv_sems.at[slot], device_id=to, device_id_type=MESH)

            others = [k ^ flip for flip in FLIPS]
            sends = [copy(0, (k, cc), _chip_id(k, 1 - cc))] + [copy(1 + j, (k, cc), _chip_id(kk, cc)) for j, kk in enumerate(others)]
            for cp in sends:
                cp.start()
            for j, kk in enumerate(others):
                copy(1 + j, (kk, cc), _chip_id(k, cc)).wait_recv()
                cp = copy(4 + j, (kk, cc), _chip_id(k, 1 - cc))
                cp.start()
                sends.append(cp)
            copy(0, (k, 1 - cc), _chip_id(k, cc)).wait_recv()
            for j, kk in enumerate(others):
                copy(4 + j, (kk, 1 - cc), _chip_id(k, cc)).wait_recv()
            for cp in sends:
                cp.wait_send()

        _on_each_place(x, y, c, at_place)

    def peers(x, y, c):
        return _sibling(x, y, c) + _same_core_of_other_chips(x, y, c)

    return _on_sequencer(body, [placed], [], 7, peers, name, collective_id, return_inputs=True)[0]


def _gather_weights(fulls, geoms, name, collective_id):
    n_w = len(fulls)
    n_ch, n_relay = GATHER_CHUNKS, GATHER_CHUNKS // 2
    f_refs = [jax.new_ref(f, memory_space=pltpu.MemorySpace.HBM) for f in fulls]
    FLIP_X, FLIP_Y, FLIP_BOTH = FLIPS
    TO_X, TO_Y, RELAY_TO_Y, RELAY_TO_X, ON_X, ON_Y, ON_DIAG = 0, n_ch, 2 * n_ch, 2 * n_ch + n_relay, 3 * n_ch, 4 * n_ch, 5 * n_ch

    @pl.kernel(mesh=plsc.ScalarSubcoreMesh(axis_name="sequencer", num_cores=1), name=name,
               scratch_types=(pltpu.SemaphoreType.DMA((GATHER_COPIES * n_w,)), pltpu.SemaphoreType.DMA((GATHER_COPIES * n_w,))),
               compiler_params=pltpu.CompilerParams(collective_id=collective_id))
    def launch(send_sems, recv_sems):
        x, y, c = _place()
        _handshake([(x, y, 1 - c), (1 - x, y, c), (x, 1 - y, c)])

        def at_place(k, cc):
            kx, ky, kd = k ^ FLIP_X, k ^ FLIP_Y, k ^ FLIP_BOTH
            me, sibling = _chip_id(k, cc), _chip_id(k, 1 - cc)
            started = []

            def copy(i, slot, src, dst, to, start=True):
                cp = pltpu.make_async_remote_copy(src_ref=src, dst_ref=dst, send_sem=send_sems.at[GATHER_COPIES * i + slot],
                                                  recv_sem=recv_sems.at[GATHER_COPIES * i + slot], device_id=to, device_id_type=MESH)
                if start:
                    cp.start()
                    started.append(cp)
                return cp

            def pass_on(i, slot, ref, to):
                copy(i, slot, ref, ref, to)

            def landed(i, slot, ref):
                copy(i, slot, ref, ref, me, start=False).wait_recv()

            y_order = [(n_relay + s) % n_ch for s in range(n_ch)]
            for i, (g, f_ref) in enumerate(zip(geoms, f_refs)):
                for s in range(n_ch):
                    pass_on(i, TO_X + s, g.chunk_of(f_ref, k, cc, s, n_ch), _chip_id(kx, cc))
                    pass_on(i, TO_Y + y_order[s], g.chunk_of(f_ref, k, cc, y_order[s], n_ch), _chip_id(ky, cc))
            for i, (g, f_ref) in enumerate(zip(geoms, f_refs)):
                for s in range(n_ch):
                    from_x = g.chunk_of(f_ref, kx, cc, s, n_ch)
                    landed(i, TO_X + s, from_x)
                    if s < n_relay:
                        pass_on(i, RELAY_TO_Y + s, from_x, _chip_id(ky, cc))
                    pass_on(i, ON_X + s, from_x, sibling)
                    ch = y_order[s]
                    from_y = g.chunk_of(f_ref, ky, cc, ch, n_ch)
                    landed(i, TO_Y + ch, from_y)
                    if ch >= n_relay:
                        pass_on(i, RELAY_TO_X + ch - n_relay, from_y, _chip_id(kx, cc))
                    pass_on(i, ON_Y + ch, from_y, sibling)
                for r in range(n_relay):
                    via_y = g.chunk_of(f_ref, kd, cc, r, n_ch)
                    landed(i, RELAY_TO_Y + r, via_y)
                    pass_on(i, ON_DIAG + r, via_y, sibling)
                    via_x = g.chunk_of(f_ref, kd, cc, n_relay + r, n_ch)
                    landed(i, RELAY_TO_X + r, via_x)
                    pass_on(i, ON_DIAG + n_relay + r, via_x, sibling)
            for i, (g, f_ref) in enumerate(zip(geoms, f_refs)):
                for slot, kk in ((ON_X, kx), (ON_Y, ky), (ON_DIAG, kd)):
                    for ch in range(n_ch):
                        landed(i, slot + ch, g.chunk_of(f_ref, kk, 1 - cc, ch, n_ch))
            for cp in started:
                cp.wait_send()

        _on_each_place(x, y, c, at_place)

    launch()
    return [f_ref[...] for f_ref in f_refs]


def _swap_core_halves(grads, geoms, name, collective_id):
    n_cp = sum(1 if g.by_cols else N_CHIPS for g in geoms)

    def body(g_refs, t_refs, send_sems, recv_sems, x, y, c):

        def at_place(_, cc):
            def pairs(hc):
                out = []
                for g, g_ref, t_ref in zip(geoms, g_refs, t_refs):
                    if g.by_cols:
                        out.append((g_ref.at[pl.ds(hc * g.half_rows, g.half_rows), :], t_ref))
                    else:
                        out += [(g.half_of(g_ref, k, hc), g.part_of_halves(t_ref, k)) for k in range(N_CHIPS)]
                return out

            sends = [pltpu.make_async_remote_copy(src_ref=src, dst_ref=dst, send_sem=send_sems.at[n],
                                                  recv_sem=recv_sems.at[n], device_id=(x, y, 1 - cc), device_id_type=MESH)
                     for n, (src, dst) in enumerate(pairs(1 - cc))]
            for cp in sends:
                cp.start()
            for n, (src, dst) in enumerate(pairs(cc)):
                pltpu.make_async_remote_copy(src_ref=src, dst_ref=dst, send_sem=send_sems.at[n], recv_sem=recv_sems.at[n],
                                             device_id=(x, y, cc), device_id_type=MESH).wait_recv()
            for cp in sends:
                cp.wait_send()

        _on_each_place(x, y, c, at_place, by_chip=False)

    return _on_sequencer(body, grads, [jax.ShapeDtypeStruct((g.full[0] // 2, g.full[1]), F32) for g in geoms],
                         n_cp, _sibling, name, collective_id)


def _send_to_sibling(buffers, name, collective_id):
    def body(src_refs, dst_refs, send_sems, recv_sems, x, y, c):
        def copy(i):
            return pltpu.make_async_remote_copy(src_ref=src_refs[i], dst_ref=dst_refs[i], send_sem=send_sems.at[i],
                                                recv_sem=recv_sems.at[i], device_id=(x, y, 1 - c), device_id_type=MESH)

        for i in range(len(buffers)):
            copy(i).start()
        for i in range(len(buffers)):
            copy(i).wait()

    return _on_sequencer(body, buffers, [jax.ShapeDtypeStruct(t.shape, t.dtype) for t in buffers], len(buffers),
                         _sibling, name, collective_id)


def _scatter_chip_sums(sums, geoms, name, collective_id):
    def body(s_refs, r_refs, send_sems, recv_sems, x, y, c):

        def at_place(k, _):
            sends = []
            for i, (g, s_ref, r_ref) in enumerate(zip(geoms, s_refs, r_refs)):
                for j, flip in enumerate(FLIPS):
                    kk = k ^ flip
                    cp = pltpu.make_async_remote_copy(
                        src_ref=g.part_of_halves(s_ref, kk), dst_ref=r_ref.at[j], send_sem=send_sems.at[3 * i + j],
                        recv_sem=recv_sems.at[3 * i + j], device_id=(kk // 2, kk % 2, c), device_id_type=MESH)
                    cp.start()
                    sends.append(cp)
            for i, (g, s_ref, r_ref) in enumerate(zip(geoms, s_refs, r_refs)):
                for j in range(len(FLIPS)):
                    pltpu.make_async_remote_copy(
                        src_ref=g.part_of_halves(s_ref, k), dst_ref=r_ref.at[j], send_sem=send_sems.at[3 * i + j],
                        recv_sem=recv_sems.at[3 * i + j], device_id=(x, y, c), device_id_type=MESH).wait_recv()
            for cp in sends:
                cp.wait_send()

        _on_each_place(x, y, c, at_place, by_core=False)

    return _on_sequencer(body, sums, [jax.ShapeDtypeStruct((len(FLIPS),) + g.half, WIRE_DTYPE) for g in geoms],
                         len(FLIPS) * len(sums), _same_core_of_other_chips, name, collective_id)


def _share_reduced_halves(reduced, geoms, name, collective_id):
    def body(out_refs, _, send_sems, recv_sems, x, y, c):

        def at_place(_, cc):
            sends = []
            for i, (g, ref) in enumerate(zip(geoms, out_refs)):
                mine = g.half_of_shard(ref, cc)
                cp = pltpu.make_async_remote_copy(src_ref=mine, dst_ref=mine, send_sem=send_sems.at[i],
                                                  recv_sem=recv_sems.at[i], device_id=(x, y, 1 - cc), device_id_type=MESH)
                cp.start()
                sends.append(cp)
            for i, (g, ref) in enumerate(zip(geoms, out_refs)):
                theirs = g.half_of_shard(ref, 1 - cc)
                pltpu.make_async_remote_copy(src_ref=theirs, dst_ref=theirs, send_sem=send_sems.at[i],
                                             recv_sem=recv_sems.at[i], device_id=(x, y, cc), device_id_type=MESH).wait_recv()
            for cp in sends:
                cp.wait_send()

        _on_each_place(x, y, c, at_place, by_chip=False)

    return _on_sequencer(body, reduced, [], len(reduced), _sibling, name, collective_id, return_inputs=True)


def _chip_sum(place, grad, theirs, g, name):
    RH, C = theirs.shape
    h = g.half_rows
    tr = _tile(h, 256, 16)
    tc = _tile(C, 2048)
    per_half = h // tr

    if g.by_cols:
        grad_map = lambda i, j, p: (p[1] * per_half + i, j)
    else:
        grad_map = lambda i, j, p: ((i // per_half) * 2 * per_half + p[1] * per_half + i % per_half, j)

    def body(p_ref, a_ref, b_ref, f_ref, o_ref):
        total = a_ref[...] + b_ref[...]
        f_ref[...] = total
        o_ref[...] = total.astype(o_ref.dtype)

    return pl.pallas_call(
        body, name=name,
        grid_spec=pltpu.PrefetchScalarGridSpec(
            num_scalar_prefetch=1, grid=(RH // tr, C // tc),
            in_specs=[pl.BlockSpec((tr, tc), grad_map), pl.BlockSpec((tr, tc), lambda i, j, p: (i, j))],
            out_specs=[pl.BlockSpec((tr, tc), lambda i, j, p: (i, j))] * 2),
        out_shape=[jax.ShapeDtypeStruct((RH, C), F32), jax.ShapeDtypeStruct((RH, C), WIRE_DTYPE)],
        compiler_params=_params("parallel", "parallel"),
    )(place, grad, theirs)


def _dw_half(place, a, b, g, mine, name, add=None):
    K, R = a.shape
    C = b.shape[1]
    h = g.half_rows
    tm, tn = _tile(h, 1024, 16), _tile(C, 512)
    per_half = h // tm
    n_i = (R // 2) // tm

    def a_map(i, j, p):
        hc = p[1] if mine else 1 - p[1]
        if g.by_cols:
            return 0, hc * n_i + i
        return 0, (i // per_half) * 2 * per_half + hc * per_half + i % per_half

    mn_spec = pl.BlockSpec((tm, tn), lambda i, j, p: (i, j))

    def body(p_ref, a_ref, b_ref, *rest):
        acc = _dot(a_ref[...], b_ref[...], TN)
        if add is None:
            rest[0][...] = acc
        else:
            total = acc + rest[0][...]
            rest[1][...] = total
            rest[2][...] = total.astype(rest[2].dtype)

    out_shape = [jax.ShapeDtypeStruct((R // 2, C), F32)] + ([] if add is None else [jax.ShapeDtypeStruct((R // 2, C), WIRE_DTYPE)])
    return pl.pallas_call(
        body, name=name,
        grid_spec=pltpu.PrefetchScalarGridSpec(
            num_scalar_prefetch=1, grid=(n_i, C // tn),
            in_specs=[pl.BlockSpec((K, tm), a_map), pl.BlockSpec((K, tn), lambda i, j, p: (0, j))] + ([] if add is None else [mn_spec]),
            out_specs=[mn_spec] * len(out_shape)),
        out_shape=out_shape,
        compiler_params=_params("parallel", "arbitrary"),
    )(place, a, b, *([] if add is None else [add]))


def _reduce_half(place, sums, others, g, name):
    h, tc = g.half
    tr = _tile(h, 256, 16)
    per_half = h // tr
    sums_map = (lambda i, p: (i, p[0])) if g.by_cols else (lambda i, p: (p[0] * per_half + i, 0))

    def body(p_ref, s_ref, o0_ref, o1_ref, o2_ref, out_ref):
        acc = s_ref[...]
        for o_ref in (o0_ref, o1_ref, o2_ref):
            acc = acc + o_ref[...].astype(F32)
        out_ref[...] = acc

    other_specs = [pl.BlockSpec((None, tr, tc), functools.partial(lambda i, p, j: (j, i, 0), j=j)) for j in range(len(FLIPS))]
    return pl.pallas_call(
        body, name=name,
        grid_spec=pltpu.PrefetchScalarGridSpec(
            num_scalar_prefetch=1, grid=(per_half,),
            in_specs=[pl.BlockSpec((tr, tc), sums_map)] + other_specs,
            out_specs=pl.BlockSpec((tr, tc), lambda i, p: (p[1] * per_half + i, 0))),
        out_shape=jax.ShapeDtypeStruct(g.shard, F32),
        compiler_params=_params("arbitrary"),
    )(place, sums, others, others, others)


SLAB_ROW_UNIT = 256
SMALL = ("b_ada", "norm1_g", "v_norm_g", "w_spatial", "b_spatial", "out_norm_g", "norm2_g", "final_g")
BIG = ("w_in", "w_out", "w_gate", "w_up", "w_down")
BY_COLS = {"w_in": True, "w_out": False, "w_gate": True, "w_up": True, "w_down": False}
ORDER = ("w_ada", "b_ada", "norm1_g", "w_in", "v_norm_g", "w_spatial", "b_spatial", "out_norm_g", "w_out",
         "norm2_g", "w_gate", "w_up", "w_down", "final_g")


def _pack(parts):
    return jnp.concatenate([parts[n].reshape(-1) for n in SMALL]).reshape(-1, LANE)


def _adamw_small(w, g, m, v, shapes, name):
    R = w.shape[0]
    slab_spec = pl.BlockSpec((R, LANE), lambda: (0, 0))
    out_shapes = [shapes[n] if len(shapes[n]) > 1 else (1,) + tuple(shapes[n]) for n in SMALL]

    def body(w_ref, g_ref, m_ref, v_ref, *out_refs):
        gv = g_ref[...]
        results = (gv,) + _adamw_math(w_ref[...], gv, m_ref[...], v_ref[...])
        for kind, val in enumerate(results):
            at = 0
            for i, shp in enumerate(out_shapes):
                o_ref = out_refs[kind * len(SMALL) + i]
                n_rows = math.prod(shp) // LANE
                if len(shp) == 2:
                    for r in range(n_rows):
                        o_ref[:, r * LANE:(r + 1) * LANE] = val[at + r:at + r + 1, :]
                else:
                    o_ref[0] = val[at:at + n_rows, :].reshape(shp[1:])
                at += n_rows

    outs = pl.pallas_call(
        body, name=name, in_specs=[slab_spec] * 4,
        out_specs=[pl.BlockSpec(shp, functools.partial(lambda nd: (0,) * nd, len(shp))) for shp in out_shapes] * 4,
        out_shape=[jax.ShapeDtypeStruct(shp, F32) for shp in out_shapes] * 4,
        compiler_params=_params(),
    )(w, g, m, v)
    dicts = []
    for kind in range(4):
        part = outs[kind * len(SMALL):(kind + 1) * len(SMALL)]
        dicts.append({n: a.reshape(shapes[n]) for n, a in zip(SMALL, part)})
    return dicts


def kernel(x, c, w_ada, b_ada, norm1_g, w_in, v_norm_g, w_spatial, b_spatial, out_norm_g, w_out, norm2_g, w_gate, w_up, w_down, final_g, loss_target, m_w_ada, m_b_ada, m_norm1_g, m_w_in, m_v_norm_g, m_w_spatial, m_b_spatial, m_out_norm_g, m_w_out, m_norm2_g, m_w_gate, m_w_up, m_w_down, m_final_g, v_w_ada, v_b_ada, v_norm1_g, v_w_in, v_v_norm_g, v_w_spatial, v_b_spatial, v_out_norm_g, v_w_out, v_norm2_g, v_w_gate, v_w_up, v_w_down, v_final_g):
    weights = dict(w_ada=w_ada, b_ada=b_ada, norm1_g=norm1_g, w_in=w_in, v_norm_g=v_norm_g, w_spatial=w_spatial,
                   b_spatial=b_spatial, out_norm_g=out_norm_g, w_out=w_out, norm2_g=norm2_g, w_gate=w_gate, w_up=w_up,
                   w_down=w_down, final_g=final_g)
    m_in = dict(w_ada=m_w_ada, b_ada=m_b_ada, norm1_g=m_norm1_g, w_in=m_w_in, v_norm_g=m_v_norm_g, w_spatial=m_w_spatial,
                b_spatial=m_b_spatial, out_norm_g=m_out_norm_g, w_out=m_w_out, norm2_g=m_norm2_g, w_gate=m_w_gate,
                w_up=m_w_up, w_down=m_w_down, final_g=m_final_g)
    v_in = dict(w_ada=v_w_ada, b_ada=v_b_ada, norm1_g=v_norm1_g, w_in=v_w_in, v_norm_g=v_v_norm_g, w_spatial=v_w_spatial,
                b_spatial=v_b_spatial, out_norm_g=v_out_norm_g, w_out=v_w_out, norm2_g=v_norm2_g, w_gate=v_w_gate,
                w_up=v_w_up, w_down=v_w_down, final_g=v_final_g)

    S, D = x.shape[1], x.shape[2]
    n_g = v_norm_g.shape[-1] // LANE
    n_h = (D - n_g * LANE) // LANE
    GW = n_g * LANE
    xi, yi, ci = _place()
    chip = 2 * xi + yi
    me = 4 * xi + 2 * yi + ci
    place = jnp.stack([chip, ci]).astype(jnp.int32)

    xs, target = x[0], loss_target[0]
    geoms = [_Sharded(weights[n].shape[1:], BY_COLS[n]) for n in BIG]

    full = {}
    for i, group in enumerate((("w_in",), ("w_out",), ("w_gate", "w_up"), ("w_down",))):
        gg = [geoms[BIG.index(n)] for n in group]
        own = [_cast_into_full(place, weights[n][0], g, "cast_" + n) for n, g in zip(group, gg)]
        gathered = _gather_weights(own, gg, "gather_" + "_".join(group), 1 + i)
        full.update(zip(group, gathered))

    c_pad = jnp.concatenate([c, jnp.zeros((7, D), F32)], axis=0)
    c_all = _allgather8(c_pad, "gather_c")[::8]
    n_ada = w_ada.shape[2]
    b_cols = lax.dynamic_slice(b_ada, (0, chip * n_ada), (1, n_ada))
    mod_parts = _allgather8(_mod_part(c_all, w_ada[0], b_cols, "mod_part"), "gather_mod")
    mod_all = mod_parts.reshape(N_CHIPS, 2, 8, n_ada)[:, 0].transpose(1, 0, 2).reshape(8, N_CHIPS * n_ada)
    mod = lax.dynamic_slice(mod_all, (me, 0), (1, 6 * D))
    shift1, scale1, gate1, shift2, scale2, gate2 = [mod[:, i * D:(i + 1) * D] for i in range(6)]

    b_t = b_spatial[0].T
    h1 = _norm_mod(xs, norm1_g, scale1, shift1, "norm1")
    proj, = _mm("nn", h1, full["w_in"], [F32], "proj")
    on_gm = _gmlp_fwd(proj, v_norm_g, w_spatial[0], b_t, out_norm_g, n_g, "gmlp_fwd")
    o_sb, o_n, l_sum = _sb_fwd(proj, out_norm_g, on_gm, n_g, n_h, "sb_fwd")
    attn, = _mm("nn", o_n, full["w_out"], [F32], "attn_out")
    x1, h2 = _residual_norm_mod(xs, attn, gate1, norm2_g, scale2, shift2, "norm2")
    a_g, a_u, f_in = _gate_up(h2, full["w_gate"], full["w_up"], "gate_up")
    f, = _mm("nn", f_in, full["w_down"], [F32], "down", tm=1024)
    dx2, df, d_gate2, d_final_g, loss_part = _final_loss_bwd(x1, f, gate2, final_g.reshape(1, D), target, "final")

    geom_of = dict(zip(BIG, geoms))
    grad_out, delta, new_m, new_v = {}, {}, {}, {}

    def theirs_first(group, operands, collective_id, after=None):
        outs = []
        for n, (a_op, b_op) in zip(group, operands):
            outs.append(_dw_half(place, a_op, b_op if after is None else _then(after, b_op), geom_of[n], False, "d_" + n + "_theirs")[0])
            after = outs[-1]
        return outs, _send_to_sibling(outs, "swap_" + "_".join(group), collective_id)

    def chip_sums(group, operands, theirs, after):
        f32s, wires = [], []
        for n, (a_op, b_op), t in zip(group, operands, theirs):
            sf, sw = _dw_half(place, a_op, b_op, geom_of[n], True, "d_" + n + "_mine", add=_then(after, t))
            f32s.append(sf)
            wires.append(sw)
            after = sw
        return f32s, wires

    def scatter(group, sums, collective_id):
        return _scatter_chip_sums(sums, [geom_of[n] for n in group], "scatter_" + "_".join(group), collective_id)

    def reduce_halves(group, sums, others, after):
        return [_reduce_half(place, sf, _then(after, o), geom_of[n], "reduce_" + n) for n, sf, o in zip(group, sums, others)]

    def share(group, halves, collective_id):
        return _share_reduced_halves(halves, [geom_of[n] for n in group], "share_" + "_".join(group), collective_id)

    def adamw(group, reduced, after):
        for n, r in zip(group, reduced):
            go, d, mo, vo = _adamw(weights[n][0], _then(after, r), m_in[n][0], v_in[n][0], "adamw_" + n)
            grad_out[n], delta[n], new_m[n], new_v[n] = go[None], d[None], mo[None], vo[None]
        return d

    g_down = ("w_down",)
    g_ffn = ("w_gate", "w_up")
    g_out = ("w_out",)
    g_in = ("w_in",)

    gr_down, = _mm("tn", f_in, df, [F32], "d_w_down", tm=1408, tn=1024)
    th_down, = _swap_core_halves([gr_down], [geom_of["w_down"]], "swap_w_down", 6)
    d_ag, d_au = _mm("nt", df, full["w_down"], [MXU_DTYPE, MXU_DTYPE], "d_ffn_in", extras=(a_g, a_u),
                     epilogue=_swiglu_bwd_epilogue)
    sf_down, sw_down = [[t] for t in _chip_sum(place, gr_down, _then(d_ag, th_down), geom_of["w_down"], "chip_sum_w_down")]
    ot_down = scatter(g_down, sw_down, 7)
    sent, th_ffn = theirs_first(g_ffn, [(h2, d_ag), (h2, d_au)], 9, after=sw_down)
    dh2 = _mm_ktiled("nt", [(_then(sent, d_ag), full["w_gate"]), (d_au, full["w_up"])], "d_h2", tn=512)
    sf_ffn, sw_ffn = chip_sums(g_ffn, [(h2, d_ag), (h2, d_au)], th_ffn, after=dh2)
    ot_ffn = scatter(g_ffn, sw_ffn, 10)
    hv_down = reduce_halves(g_down, sf_down, ot_down, after=sw_ffn)
    rd_down = share(g_down, hv_down, 8)
    dx1, d_shift2, d_scale2, d_norm2_g, d_gate1, d_attn = _norm_mod_bwd(
        _then(hv_down, dh2), x1, dx2, norm2_g, scale2, "norm2_bwd", branch=attn, gate=gate1)
    gr_out, = _mm("tn", o_n, d_attn, [F32], "d_w_out")
    th_out, = _swap_core_halves([gr_out], [geom_of["w_out"]], "swap_w_out", 12)
    d_on, = _mm("nt", _then(gr_out, d_attn), full["w_out"], [F32], "d_o")
    dp_gm, d_w_spatial, d_b_t, d_v_norm_g, d_og_gm = _gmlp_bwd(proj, d_on, v_norm_g, w_spatial[0], b_t, out_norm_g, n_g, "gmlp_bwd")
    sf_out, sw_out = [[t] for t in _chip_sum(place, gr_out, _then(dp_gm, th_out), geom_of["w_out"], "chip_sum_w_out")]
    ot_out = scatter(g_out, sw_out, 13)
    dq, dk, dv, d_og_sb = _sb_bwd(proj, o_sb, l_sum, _then(sw_out, d_on), out_norm_g, n_g, n_h, "sb_bwd")
    dproj = jnp.concatenate([dp_gm, dq, dk, dv], axis=1)
    sent, th_in = theirs_first(g_in, [(h1, dproj)], 15)
    hv_ffn = reduce_halves(g_ffn, sf_ffn, ot_ffn, after=sent)
    rd_ffn = share(g_ffn, hv_ffn, 11)
    dh1, = _mm("nt", _then(sent, dproj), full["w_in"], [F32], "d_h1", tm=1024)
    hv_out = reduce_halves(g_out, sf_out, ot_out, after=dh1)
    rd_out = share(g_out, hv_out, 14)
    grad_x, d_shift1, d_scale1, d_norm1_g = _norm_mod_bwd(_then(hv_out, dh1), xs, dx1, norm1_g, scale1, "norm1_bwd")

    small_parts = [d_shift1, d_scale1, d_gate1, d_shift2, d_scale2, d_gate2,
                   d_norm1_g, d_v_norm_g, d_w_spatial, d_b_t.T, d_og_gm, d_og_sb, d_norm2_g, d_final_g]
    small_rows = sum(math.prod(p.shape) for p in small_parts) // LANE
    rows = -(-(small_rows + 8) // SLAB_ROW_UNIT) * SLAB_ROW_UNIT
    slab = _place_slab(place, small_parts, loss_part, rows, "place_slab")
    gathered = _allgather8_on_sequencer(slab, rows, "gather_small", 18)
    sf_in, sw_in = chip_sums(g_in, [(h1, dproj)], th_in, after=slab)
    ot_in = scatter(g_in, sw_in, 16)
    done = adamw(g_down, rd_down, after=sw_in)
    done = adamw(g_ffn, rd_ffn, after=done)
    done = adamw(g_out, rd_out, after=done)
    gathered = _then(done, gathered)
    small_shapes = {n: weights[n].shape for n in SMALL}
    slab_sum = _sum_devices(gathered, 8, "sum_small")
    small_sum, loss = slab_sum[:small_rows], slab_sum[small_rows, 0]
    dmod_all = gathered.reshape(8, rows * LANE)[:, :6 * D]
    dmod_cols = lax.dynamic_slice(dmod_all, (0, chip * n_ada), (8, n_ada))
    g_ada, d, mo, vo = _adamw_ada(c_all, dmod_cols, w_ada[0], m_w_ada[0], v_w_ada[0], "adamw_w_ada")
    grad_out["w_ada"], delta["w_ada"], new_m["w_ada"], new_v["w_ada"] = g_ada[None], d[None], mo[None], vo[None]
    small_out = _adamw_small(_pack({n: weights[n] for n in SMALL}), small_sum, _pack({n: m_in[n] for n in SMALL}),
                             _pack({n: v_in[n] for n in SMALL}), small_shapes, "adamw_small")
    for dst, part in zip((grad_out, delta, new_m, new_v), small_out):
        dst.update(part)
    hv_in = reduce_halves(g_in, sf_in, ot_in, after=d)
    adamw(g_in, share(g_in, hv_in, 17), after=d)

    return (loss, grad_x[None], *[grad_out[n] for n in ORDER], *[delta[n] for n in ORDER],
            *[new_m[n] for n in ORDER], *[new_v[n] for n in ORDER])
```

```python
import functools
import math

import jax
import jax.numpy as jnp
from jax import lax
from jax.experimental import pallas as pl
from jax.experimental.pallas import tpu as pltpu
from jax.experimental.pallas import tpu_sc as plsc

F32 = jnp.float32
MXU_DTYPE = jnp.bfloat16
WIRE_DTYPE = jnp.bfloat16
EPS = 1e-6
LANE = 128
V7X_VMEM_LIMIT = 56 * 1024 * 1024
MESH = pl.DeviceIdType.MESH
N_CHIPS = 4
FLIPS = (2, 1, 3)

ADAM_LR = 0.001
ADAM_B1 = 0.9
ADAM_B2 = 0.999
ADAM_EPS = 1e-08
ADAM_WD = 0.01
ADAM_STEP = 10


def _params(*semantics):
    return pltpu.CompilerParams(dimension_semantics=semantics or None, vmem_limit_bytes=V7X_VMEM_LIMIT)


def _tile(dim, pref, unit=LANE):
    best = None
    t = unit
    while t <= min(dim, pref):
        if dim % t == 0:
            best = t
        t += unit
    return best if best is not None else dim


def _then(first, second):
    return lax.optimization_barrier((first, second))[1]


def _sum0(v):
    return jnp.sum(v, axis=0, keepdims=True)


def _mean1(v):
    return jnp.mean(v, axis=-1, keepdims=True)


def _gelu(x):
    return 0.5 * x * (1.0 + lax.erf(x * (1.0 / math.sqrt(2.0))))


def _gelu_grad(x):
    cdf = 0.5 * (1.0 + lax.erf(x * (1.0 / math.sqrt(2.0))))
    return cdf + x * jnp.exp(-0.5 * x * x) * (1.0 / math.sqrt(2.0 * math.pi))


def _dot(a, b, dims):
    return lax.dot_general(a, b, (dims, ((), ())), preferred_element_type=F32)


NN = ((1,), (0,))
NT = ((1,), (1,))
TN = ((0,), (0,))


def _mm(kind, a, b, out_dtypes, name, tm=2048, tn=512, extras=(), epilogue=None):
    if kind == "nn":
        (M, K), N = a.shape, b.shape[1]
    elif kind == "nt":
        (M, K), N = a.shape, b.shape[0]
    else:
        (K, M), N = a.shape, b.shape[1]
    tm, tn = _tile(M, tm), _tile(N, tn)
    a_spec = pl.BlockSpec((K, tm), lambda i, j: (0, i)) if kind == "tn" else pl.BlockSpec((tm, K), lambda i, j: (i, 0))
    b_spec = pl.BlockSpec((tn, K), lambda i, j: (j, 0)) if kind == "nt" else pl.BlockSpec((K, tn), lambda i, j: (0, j))
    mn_spec = pl.BlockSpec((tm, tn), lambda i, j: (i, j))
    dims = {"nn": NN, "nt": NT, "tn": TN}[kind]
    n_extra = len(extras)

    n_chunks = 1 if epilogue is None or kind == "tn" else max(1, tm // 256)
    rows_per = tm // n_chunks

    def body(a_ref, b_ref, *rest):
        for r in range(n_chunks):
            rows = slice(r * rows_per, (r + 1) * rows_per)
            acc = _dot(a_ref[...] if n_chunks == 1 else a_ref[rows, :], b_ref[...], dims)
            res = (acc,) if epilogue is None else epilogue(acc, *[e[rows, :] for e in rest[:n_extra]])
            for o_ref, val in zip(rest[n_extra:], res):
                o_ref[rows, :] = val.astype(o_ref.dtype)

    outs = pl.pallas_call(
        body, name=name, grid=(M // tm, N // tn),
        in_specs=[a_spec, b_spec] + [mn_spec] * n_extra,
        out_specs=[mn_spec] * len(out_dtypes),
        out_shape=[jax.ShapeDtypeStruct((M, N), d) for d in out_dtypes],
        compiler_params=_params("parallel", "arbitrary"),
    )(a, b, *extras)
    return outs


def _mm_ktiled(kind, pairs, name, tm=2048, tn=1024, tk=1408):
    a0, b0 = pairs[0]
    M, K = a0.shape
    N = b0.shape[1] if kind == "nn" else b0.shape[0]
    tm, tn, tk = _tile(M, tm), _tile(N, tn), _tile(K, tk)
    a_spec = pl.BlockSpec((tm, tk), lambda i, j, k: (i, k))
    b_spec = pl.BlockSpec((tk, tn), lambda i, j, k: (k, j)) if kind == "nn" else pl.BlockSpec((tn, tk), lambda i, j, k: (j, k))
    dims = NN if kind == "nn" else NT
    n_pairs = len(pairs)

    def body(*refs):
        o_ref = refs[2 * n_pairs]
        acc = _dot(refs[0][...], refs[1][...], dims)
        for p in range(1, n_pairs):
            acc = acc + _dot(refs[2 * p][...], refs[2 * p + 1][...], dims)

        @pl.when(pl.program_id(2) == 0)
        def _():
            o_ref[...] = acc

        @pl.when(pl.program_id(2) != 0)
        def _():
            o_ref[...] += acc

    return pl.pallas_call(
        body, name=name, grid=(M // tm, N // tn, K // tk),
        in_specs=[a_spec, b_spec] * n_pairs,
        out_specs=pl.BlockSpec((tm, tn), lambda i, j, k: (i, j)),
        out_shape=jax.ShapeDtypeStruct((M, N), F32),
        compiler_params=_params("parallel", "parallel", "arbitrary"),
    )(*[x for pair in pairs for x in pair])


def _gate_up(h, wg, wu, name):
    (M, K), N = h.shape, wg.shape[1]
    tm, tn = _tile(M, 2048), _tile(N, 512)

    n_chunks = max(1, tm // 256)
    rows_per = tm // n_chunks

    def body(h_ref, wg_ref, wu_ref, ag_ref, au_ref, f_ref):
        for r in range(n_chunks):
            rows = slice(r * rows_per, (r + 1) * rows_per)
            hv = h_ref[rows, :]
            ag = _dot(hv, wg_ref[...], NN)
            au = _dot(hv, wu_ref[...], NN)
            ag_ref[rows, :] = ag.astype(ag_ref.dtype)
            au_ref[rows, :] = au.astype(au_ref.dtype)
            f_ref[rows, :] = (ag * jax.nn.sigmoid(ag) * au).astype(f_ref.dtype)

    w_spec = pl.BlockSpec((K, tn), lambda i, j: (0, j))
    mn_spec = pl.BlockSpec((tm, tn), lambda i, j: (i, j))
    return pl.pallas_call(
        body, name=name, grid=(M // tm, N // tn),
        in_specs=[pl.BlockSpec((tm, K), lambda i, j: (i, 0)), w_spec, w_spec],
        out_specs=[mn_spec] * 3,
        out_shape=[jax.ShapeDtypeStruct((M, N), MXU_DTYPE)] * 3,
        compiler_params=_params("parallel", "arbitrary"),
    )(h, wg, wu)


def _swiglu_bwd_epilogue(dfin, ag, au):
    ag, au = ag.astype(F32), au.astype(F32)
    sg = jax.nn.sigmoid(ag)
    d_au = dfin * (ag * sg)
    d_ag = dfin * au * (sg * (1.0 + ag * (1.0 - sg)))
    return d_ag, d_au


def _row_specs(ts, width):
    return pl.BlockSpec((ts, width), lambda i: (i, 0)), pl.BlockSpec((1, width), lambda i: (0, 0))


def _cast_into_full(place, shard, g, name):
    R, C = shard.shape
    tr = _tile(R, 256, 16)
    n_blk = R // tr
    out_map = (lambda i, p: (i, p[0])) if g.by_cols else (lambda i, p: (p[0] * n_blk + i, 0))

    def body(p_ref, a_ref, o_ref):
        o_ref[...] = a_ref[...].astype(o_ref.dtype)

    return pl.pallas_call(
        body, name=name,
        grid_spec=pltpu.PrefetchScalarGridSpec(
            num_scalar_prefetch=1, grid=(n_blk,),
            in_specs=[pl.BlockSpec((tr, C), lambda i, p: (i, 0))],
            out_specs=pl.BlockSpec((tr, C), out_map)),
        out_shape=jax.ShapeDtypeStruct(g.full, WIRE_DTYPE),
        compiler_params=_params("arbitrary"),
    )(place, shard)


def _norm_mod(x, g, scale, shift, name):
    S, D = x.shape
    ts = _tile(S, 256, 16)
    tile, vec = _row_specs(ts, D)

    def body(x_ref, g_ref, sc_ref, sh_ref, h_ref):
        xv = x_ref[...]
        r = lax.rsqrt(_mean1(xv * xv) + EPS)
        h_ref[...] = ((xv * r) * g_ref[...] * (1.0 + sc_ref[...]) + sh_ref[...]).astype(h_ref.dtype)

    return pl.pallas_call(body, name=name, grid=(S // ts,), in_specs=[tile, vec, vec, vec], out_specs=tile,
                          out_shape=jax.ShapeDtypeStruct((S, D), MXU_DTYPE), compiler_params=_params("parallel"))(x, g, scale, shift)


def _residual_norm_mod(x, attn, gate, g, scale, shift, name):
    S, D = x.shape
    ts = _tile(S, 256, 16)
    tile, vec = _row_specs(ts, D)

    def body(x_ref, a_ref, gate_ref, g_ref, sc_ref, sh_ref, x1_ref, h_ref):
        x1 = x_ref[...] + gate_ref[...] * a_ref[...]
        x1_ref[...] = x1
        r = lax.rsqrt(_mean1(x1 * x1) + EPS)
        h_ref[...] = ((x1 * r) * g_ref[...] * (1.0 + sc_ref[...]) + sh_ref[...]).astype(h_ref.dtype)

    return pl.pallas_call(body, name=name, grid=(S // ts,), in_specs=[tile, tile, vec, vec, vec, vec],
                          out_specs=[tile, tile],
                          out_shape=[jax.ShapeDtypeStruct((S, D), F32), jax.ShapeDtypeStruct((S, D), MXU_DTYPE)],
                          compiler_params=_params("parallel"))(x, attn, gate, g, scale, shift)


def _final_loss_bwd(x1, f, gate2, final_g, target, name):
    S, D = x1.shape
    ts = _tile(S, 256, 16)
    tile, vec = _row_specs(ts, D)
    loss_spec = pl.BlockSpec((1, LANE), lambda i: (0, 0))

    def body(x1_ref, f_ref, gate_ref, g_ref, t_ref, dx2_ref, df_ref, dgate_ref, dg_ref, loss_ref):
        @pl.when(pl.program_id(0) == 0)
        def _():
            dgate_ref[...] = jnp.zeros_like(dgate_ref)
            dg_ref[...] = jnp.zeros_like(dg_ref)
            loss_ref[...] = jnp.zeros_like(loss_ref)

        fv, gate, g = f_ref[...], gate_ref[...], g_ref[...]
        x2 = x1_ref[...] + gate * fv
        r = lax.rsqrt(_mean1(x2 * x2) + EPS)
        xn = x2 * r
        err = xn * g - t_ref[...]
        loss_ref[...] += jnp.broadcast_to(0.5 * _sum0(_mean1(err * err)), loss_ref.shape)
        dy = err * (1.0 / D)
        dg_ref[...] += _sum0(dy * xn)
        dxn = dy * g
        dx2 = r * (dxn - xn * _mean1(dxn * xn))
        dx2_ref[...] = dx2
        dgate_ref[...] += _sum0(dx2 * fv)
        df_ref[...] = (dx2 * gate).astype(df_ref.dtype)

    return pl.pallas_call(
        body, name=name, grid=(S // ts,), in_specs=[tile, tile, vec, vec, tile],
        out_specs=[tile, tile, vec, vec, loss_spec],
        out_shape=[jax.ShapeDtypeStruct((S, D), F32), jax.ShapeDtypeStruct((S, D), MXU_DTYPE),
                   jax.ShapeDtypeStruct((1, D), F32), jax.ShapeDtypeStruct((1, D), F32),
                   jax.ShapeDtypeStruct((1, LANE), F32)],
        compiler_params=_params("arbitrary"),
    )(x1, f, gate2, final_g, target)


def _norm_mod_bwd(dh, xin, dres, g, scale, name, branch=None, gate=None):
    S, D = xin.shape
    ts = _tile(S, 256, 16)
    tile, vec = _row_specs(ts, D)
    with_gate = branch is not None

    def body(*refs):
        if with_gate:
            dh_ref, x_ref, dres_ref, g_ref, sc_ref, br_ref, gate_ref, dx_ref, dshift_ref, dscale_ref, dg_ref, dgate_ref, dbr_ref = refs
            accs = (dshift_ref, dscale_ref, dg_ref, dgate_ref)
        else:
            dh_ref, x_ref, dres_ref, g_ref, sc_ref, dx_ref, dshift_ref, dscale_ref, dg_ref = refs
            accs = (dshift_ref, dscale_ref, dg_ref)

        @pl.when(pl.program_id(0) == 0)
        def _():
            for acc in accs:
                acc[...] = jnp.zeros_like(acc)

        dh_v, xv, g_v = dh_ref[...], x_ref[...], g_ref[...]
        one_sc = 1.0 + sc_ref[...]
        r = lax.rsqrt(_mean1(xv * xv) + EPS)
        xn = xv * r
        dshift_ref[...] += _sum0(dh_v)
        dscale_ref[...] += _sum0(dh_v * (xn * g_v))
        dg_ref[...] += _sum0(dh_v * one_sc * xn)
        dxn = dh_v * (g_v * one_sc)
        dx = dres_ref[...] + r * (dxn - xn * _mean1(dxn * xn))
        dx_ref[...] = dx
        if with_gate:
            dgate_ref[...] += _sum0(dx * br_ref[...])
            dbr_ref[...] = (dx * gate_ref[...]).astype(dbr_ref.dtype)

    ins = [dh, xin, dres, g, scale] + ([branch, gate] if with_gate else [])
    in_specs = [tile, tile, tile, vec, vec] + ([tile, vec] if with_gate else [])
    out_specs = [tile, vec, vec, vec] + ([vec, tile] if with_gate else [])
    out_shape = [jax.ShapeDtypeStruct((S, D), F32)] + [jax.ShapeDtypeStruct((1, D), F32)] * 3
    if with_gate:
        out_shape += [jax.ShapeDtypeStruct((1, D), F32), jax.ShapeDtypeStruct((S, D), MXU_DTYPE)]
    return pl.pallas_call(body, name=name, grid=(S // ts,), in_specs=in_specs, out_specs=out_specs,
                          out_shape=out_shape, compiler_params=_params("arbitrary"))(*ins)


def _causal_weights(ws_ref, wt_ref, n_g):
    row = lax.broadcasted_iota(jnp.int32, (LANE, LANE), 0)
    col = lax.broadcasted_iota(jnp.int32, (LANE, LANE), 1)
    for g in range(n_g):
        wt_ref[g] = jnp.where(col <= row, ws_ref[g], 0.0).astype(wt_ref.dtype)


def _group_layernorm(v):
    xc = v - _mean1(v)
    rstd = lax.rsqrt(_mean1(xc * xc) + EPS)
    return xc * rstd, rstd


def _gmlp_fwd(proj, v_gain, w_s, b_t, out_gain, n_g, name):
    S = proj.shape[0]
    GW = n_g * LANE
    D = out_gain.shape[1]

    def body(p_ref, vg_ref, ws_ref, bt_ref, og_ref, on_ref, wt_ref):
        @pl.when(pl.program_id(0) == 0)
        def _():
            _causal_weights(ws_ref, wt_ref, n_g)

        for g in range(n_g):
            cols = slice(g * LANE, (g + 1) * LANE)
            u = _gelu(p_ref[:, cols])
            v = _gelu(p_ref[:, GW + g * LANE:GW + (g + 1) * LANE])
            vhat, _ = _group_layernorm(v)
            vln = (vhat * vg_ref[:, cols]).astype(MXU_DTYPE)
            mixed = _dot(wt_ref[g], vln, NN) + bt_ref[:, g:g + 1]
            o = u * mixed
            r = lax.rsqrt(_mean1(o * o) + EPS)
            on_ref[:, cols] = (o * r * og_ref[:, cols]).astype(on_ref.dtype)

    return pl.pallas_call(
        body, name=name, grid=(S // LANE,),
        in_specs=[pl.BlockSpec((LANE, 2 * GW), lambda n: (n, 0)),
                  pl.BlockSpec((1, GW), lambda n: (0, 0)),
                  pl.BlockSpec((n_g, LANE, LANE), lambda n: (0, 0, 0)),
                  pl.BlockSpec((LANE, n_g), lambda n: (0, 0)),
                  pl.BlockSpec((1, GW), lambda n: (0, 0))],
        out_specs=pl.BlockSpec((LANE, GW), lambda n: (n, 0)),
        out_shape=jax.ShapeDtypeStruct((S, D), MXU_DTYPE),
        scratch_shapes=[pltpu.VMEM((n_g, LANE, LANE), MXU_DTYPE)],
        compiler_params=_params("arbitrary"),
    )(proj, v_gain, w_s, b_t, out_gain)


def _gmlp_bwd(proj, d_on, v_gain, w_s, b_t, out_gain, dqkv, n_g, name):
    S, N_IN = proj.shape
    GW = n_g * LANE
    SBW = dqkv[0].shape[1]

    def body(p_ref, dn_ref, vg_ref, ws_ref, bt_ref, og_ref, dq_ref, dk_ref, dv_ref, dp_ref, dws_ref, dbt_ref, dvg_ref, dog_ref, wt_ref):
        for i, part_ref in enumerate((dq_ref, dk_ref, dv_ref)):
            dp_ref[:, 2 * GW + i * SBW:2 * GW + (i + 1) * SBW] = part_ref[...]

        @pl.when(pl.program_id(0) == 0)
        def _():
            _causal_weights(ws_ref, wt_ref, n_g)
            dws_ref[...] = jnp.zeros_like(dws_ref)
            dbt_ref[...] = jnp.zeros_like(dbt_ref)
            dvg_ref[...] = jnp.zeros_like(dvg_ref)
            dog_ref[...] = jnp.zeros_like(dog_ref)

        row = lax.broadcasted_iota(jnp.int32, (LANE, LANE), 0)
        col = lax.broadcasted_iota(jnp.int32, (LANE, LANE), 1)
        for g in range(n_g):
            cols = slice(g * LANE, (g + 1) * LANE)
            vcols = slice(GW + g * LANE, GW + (g + 1) * LANE)
            pu, pv = p_ref[:, cols], p_ref[:, vcols]
            u, v = _gelu(pu), _gelu(pv)
            vhat, rstd = _group_layernorm(v)
            gain = vg_ref[:, cols]
            vln = (vhat * gain).astype(MXU_DTYPE)
            mixed = _dot(wt_ref[g], vln, NN) + bt_ref[:, g:g + 1]
            o = u * mixed
            r = lax.rsqrt(_mean1(o * o) + EPS)
            oh = o * r
            dn = dn_ref[:, cols]
            dog_ref[:, cols] += _sum0(dn * oh)
            dhn = dn * og_ref[:, cols]
            d_o = r * (dhn - oh * _mean1(dhn * oh))
            du = d_o * mixed
            dmix = d_o * u
            dbt_ref[:, g:g + 1] += jnp.sum(dmix, axis=1, keepdims=True)
            dmix_b = dmix.astype(MXU_DTYPE)
            dws_ref[g] += jnp.where(col <= row, _dot(dmix_b, vln, NT), 0.0)
            dvln = _dot(wt_ref[g], dmix_b, TN)
            dvg_ref[:, cols] += _sum0(dvln * vhat)
            dxh = dvln * gain
            dv = rstd * (dxh - _mean1(dxh) - vhat * _mean1(dxh * vhat))
            dp_ref[:, cols] = (du * _gelu_grad(pu)).astype(dp_ref.dtype)
            dp_ref[:, vcols] = (dv * _gelu_grad(pv)).astype(dp_ref.dtype)

    return pl.pallas_call(
        body, name=name, grid=(S // LANE,),
        in_specs=[pl.BlockSpec((LANE, 2 * GW), lambda n: (n, 0)),
                  pl.BlockSpec((LANE, GW), lambda n: (n, 0)),
                  pl.BlockSpec((1, GW), lambda n: (0, 0)),
                  pl.BlockSpec((n_g, LANE, LANE), lambda n: (0, 0, 0)),
                  pl.BlockSpec((LANE, n_g), lambda n: (0, 0)),
                  pl.BlockSpec((1, GW), lambda n: (0, 0))] + [pl.BlockSpec((LANE, SBW), lambda n: (n, 0))] * 3,
        out_specs=[pl.BlockSpec((LANE, N_IN), lambda n: (n, 0)),
                   pl.BlockSpec((n_g, LANE, LANE), lambda n: (0, 0, 0)),
                   pl.BlockSpec((LANE, n_g), lambda n: (0, 0)),
                   pl.BlockSpec((1, GW), lambda n: (0, 0)),
                   pl.BlockSpec((1, GW), lambda n: (0, 0))],
        out_shape=[jax.ShapeDtypeStruct((S, N_IN), MXU_DTYPE),
                   jax.ShapeDtypeStruct((n_g, LANE, LANE), F32),
                   jax.ShapeDtypeStruct((LANE, n_g), F32),
                   jax.ShapeDtypeStruct((1, GW), F32),
                   jax.ShapeDtypeStruct((1, GW), F32)],
        scratch_shapes=[pltpu.VMEM((n_g, LANE, LANE), MXU_DTYPE)],
        compiler_params=_params("arbitrary"),
    )(proj, d_on, v_gain, w_s, b_t, out_gain, *dqkv)


def _tri_sum(v, tri, exact=True):
    hi = v.astype(MXU_DTYPE)
    if not exact:
        return _dot(hi, tri, NN)
    lo = (v - hi.astype(F32)).astype(MXU_DTYPE)
    return _dot(hi, tri, NN) + _dot(lo, tri, NN)


def _log_sigmoids(z):
    sp = jnp.log(1.0 + jnp.exp(-jnp.abs(z)))
    return jnp.minimum(z, 0.0) - sp, jnp.minimum(-z, 0.0) - sp


def _rows(i, size):
    return pl.ds(pl.multiple_of(i * size, size), size)


SB_QUERY_TILE = 2048
SB_KEY_TILE = 256


def _sb_tiles(S):
    tq = _tile(S, SB_QUERY_TILE)
    tk = _tile(tq, SB_KEY_TILE)
    assert (tq // tk) % 2 == 0, "the key sweep takes two blocks a pass"
    return tq, tk, S // tq, tq // tk


def _triangle(n, keep):
    row = lax.broadcasted_iota(jnp.int32, (n, n), 0)
    col = lax.broadcasted_iota(jnp.int32, (n, n), 1)
    return jnp.where(keep(row, col), 1.0, 0.0).astype(MXU_DTYPE)


def _strictly_before(tq, tk, key_offset):
    row = lax.broadcasted_iota(jnp.int32, (tq, tk), 0)
    col = lax.broadcasted_iota(jnp.int32, (tq, tk), 1)
    return col + key_offset < row


def _sb_specs(S, n_g, n_h):
    base = 2 * n_g
    q_spec = pl.BlockSpec((S, LANE), lambda h: (0, base + h))
    k_spec = pl.BlockSpec((S, LANE), lambda h: (0, base + n_h + h))
    v_spec = pl.BlockSpec((S, LANE), lambda h: (0, base + 2 * n_h + h))
    gain_spec = pl.BlockSpec((1, LANE), lambda h: (0, n_g + h))
    head_spec = pl.BlockSpec((S, LANE), lambda h: (0, h))
    return q_spec, k_spec, v_spec, gain_spec, head_spec


def _sb_fwd(proj, out_gain, on_buffer, n_g, n_h, name):
    S = proj.shape[0]
    TQ, TK, NQ, KPQ = _sb_tiles(S)
    scale = LANE ** -0.5
    q_spec, k_spec, v_spec, gain_spec, head_spec = _sb_specs(S, n_g, n_h)

    def body(q_ref, k_ref, v_ref, og_ref, _, o_ref, on_ref, ls_ref, qb, kb, vb):
        qb[...] = q_ref[...].astype(MXU_DTYPE)
        kb[...] = k_ref[...].astype(MXU_DTYPE)
        vb[...] = v_ref[...].astype(MXU_DTYPE)
        after = _triangle(TK, lambda r, c: r > c)

        def block(qi, j, ctail, acc, key_offset):
            skip = key_offset or 0
            z = _dot(qi[skip:], kb[_rows(j, TK), :], NT) * scale
            lb, l1m = _log_sigmoids(z)
            if key_offset is not None:
                strict = _strictly_before(TQ - skip, TK, 0)
                l1m = jnp.where(strict, l1m, 0.0)
            a = jnp.exp(lb + ctail[skip:] + _tri_sum(l1m, after))
            if key_offset is not None:
                a = jnp.where(strict, a, 0.0)
            acc_new = acc[skip:] + _dot(a.astype(MXU_DTYPE), vb[_rows(j, TK), :], NN)
            ctail_new = ctail[skip:] + jnp.sum(l1m, axis=1, keepdims=True)
            if skip:
                ctail_new = jnp.concatenate([ctail[:skip], ctail_new], axis=0)
                acc_new = jnp.concatenate([acc[:skip], acc_new], axis=0)
            return ctail_new, acc_new

        def q_loop(i, carry):
            qi = qb[_rows(i, TQ), :]
            state = (jnp.zeros((TQ, 1), F32), jnp.zeros((TQ, LANE), F32))
            for d in reversed(range(KPQ)):
                state = block(qi, i * KPQ + d, state[0], state[1], d * TK)
            def pair(jj, st):
                st = block(qi, i * KPQ - 1 - 2 * jj, st[0], st[1], None)
                return block(qi, i * KPQ - 2 - 2 * jj, st[0], st[1], None)

            ctail, acc = lax.fori_loop(0, i * (KPQ // 2), pair, state)
            ls_ref[_rows(i, TQ), :] = jnp.broadcast_to(ctail, (TQ, LANE))
            o_ref[_rows(i, TQ), :] = acc
            r = lax.rsqrt(_mean1(acc * acc) + EPS)
            on_ref[_rows(i, TQ), :] = (acc * r * og_ref[...]).astype(on_ref.dtype)
            return carry

        lax.fori_loop(0, NQ, q_loop, 0)

    return pl.pallas_call(
        body, name=name, grid=(n_h,),
        in_specs=[q_spec, k_spec, v_spec, gain_spec, pl.BlockSpec(memory_space=pl.ANY)],
        out_specs=[head_spec, pl.BlockSpec((S, LANE), lambda h: (0, n_g + h)), head_spec],
        out_shape=[jax.ShapeDtypeStruct((S, n_h * LANE), F32), jax.ShapeDtypeStruct(on_buffer.shape, MXU_DTYPE),
                   jax.ShapeDtypeStruct((S, n_h * LANE), F32)],
        input_output_aliases={4: 1},
        scratch_shapes=[pltpu.VMEM((S, LANE), MXU_DTYPE)] * 3,
        compiler_params=_params("parallel"),
    )(proj, proj, proj, out_gain, on_buffer)


def _sb_bwd(proj, o_sb, l_sum, d_on, out_gain, n_g, n_h, name):
    S = proj.shape[0]
    TQ, TK, NQ, KPQ = _sb_tiles(S)
    scale = LANE ** -0.5
    q_spec, k_spec, v_spec, gain_spec, head_spec = _sb_specs(S, n_g, n_h)
    dn_spec = pl.BlockSpec((S, LANE), lambda h: (0, n_g + h))
    dgain_spec = pl.BlockSpec((1, LANE), lambda h: (0, h))

    def body(q_ref, k_ref, v_ref, o_ref, ls_ref, dn_ref, og_ref, dq_ref, dk_ref, dv_ref, dog_ref,
             qb, kb, vb, dob, dk_acc, dv_acc):
        qb[...] = q_ref[...].astype(MXU_DTYPE)
        kb[...] = k_ref[...].astype(MXU_DTYPE)
        vb[...] = v_ref[...].astype(MXU_DTYPE)
        o, dn = o_ref[...], dn_ref[...]
        r = lax.rsqrt(_mean1(o * o) + EPS)
        oh = o * r
        dog_ref[...] = _sum0(dn * oh)
        dhn = dn * og_ref[...]
        dob[...] = (r * (dhn - oh * _mean1(dhn * oh))).astype(MXU_DTYPE)
        dk_acc[...] = jnp.zeros_like(dk_acc)
        dv_acc[...] = jnp.zeros_like(dv_acc)

        up_to = _triangle(TK, lambda r, c: r <= c)
        before = _triangle(TK, lambda r, c: r < c)

        def block(qi, doi, ltot, j, cl, cdl, dq, key_offset):
            skip = key_offset or 0
            q_in, do_in = qi[skip:], doi[skip:]
            kj, vj = kb[_rows(j, TK), :], vb[_rows(j, TK), :]
            z = _dot(q_in, kj, NT) * scale
            lb, l1m = _log_sigmoids(z)
            if key_offset is not None:
                strict = _strictly_before(TQ - skip, TK, 0)
                l1m = jnp.where(strict, l1m, 0.0)
            a = jnp.exp(lb + (ltot[skip:] - (cl[skip:] + _tri_sum(l1m, up_to))))
            if key_offset is not None:
                a = jnp.where(strict, a, 0.0)
            dl = _dot(do_in, vj, NT) * a
            d_l1m = cdl[skip:] + _tri_sum(dl, before, exact=False)
            beta = jnp.exp(lb)
            dz = dl * (1.0 - beta) - beta * d_l1m
            if key_offset is not None:
                dz = jnp.where(strict, dz, 0.0)
            dzs = (dz * scale).astype(MXU_DTYPE)
            dk_acc[_rows(j, TK), :] += _dot(dzs, q_in, TN)
            dv_acc[_rows(j, TK), :] += _dot(a.astype(MXU_DTYPE), do_in, TN)
            cl_new = cl[skip:] + jnp.sum(l1m, axis=1, keepdims=True)
            cdl_new = cdl[skip:] + jnp.sum(dl, axis=1, keepdims=True)
            dq_new = dq[skip:] + _dot(dzs, kj, NN)
            if skip:
                cl_new = jnp.concatenate([cl[:skip], cl_new], axis=0)
                cdl_new = jnp.concatenate([cdl[:skip], cdl_new], axis=0)
                dq_new = jnp.concatenate([dq[:skip], dq_new], axis=0)
            return cl_new, cdl_new, dq_new

        def q_loop(i, carry):
            qi, doi = qb[_rows(i, TQ), :], dob[_rows(i, TQ), :]
            ltot = ls_ref[_rows(i, TQ), :][:, :1]
            zero_col = jnp.zeros((TQ, 1), F32)
            def pair(jj, st):
                st = block(qi, doi, ltot, 2 * jj, st[0], st[1], st[2], None)
                return block(qi, doi, ltot, 2 * jj + 1, st[0], st[1], st[2], None)

            state = lax.fori_loop(0, i * (KPQ // 2), pair, (zero_col, zero_col, jnp.zeros((TQ, LANE), F32)))
            for d in range(KPQ):
                state = block(qi, doi, ltot, i * KPQ + d, state[0], state[1], state[2], d * TK)
            dq_ref[_rows(i, TQ), :] = state[2].astype(dq_ref.dtype)
            return carry

        lax.fori_loop(0, NQ, q_loop, 0)
        dk_ref[...] = dk_acc[...].astype(dk_ref.dtype)
        dv_ref[...] = dv_acc[...].astype(dv_ref.dtype)

    W = n_h * LANE
    return pl.pallas_call(
        body, name=name, grid=(n_h,),
        in_specs=[q_spec, k_spec, v_spec, head_spec, head_spec, dn_spec, gain_spec],
        out_specs=[head_spec, head_spec, head_spec, dgain_spec],
        out_shape=[jax.ShapeDtypeStruct((S, W), MXU_DTYPE)] * 3 + [jax.ShapeDtypeStruct((1, W), F32)],
        scratch_shapes=[pltpu.VMEM((S, LANE), MXU_DTYPE)] * 4 + [pltpu.VMEM((S, LANE), F32)] * 2,
        compiler_params=_params("parallel"),
    )(proj, proj, proj, o_sb, l_sum, d_on, out_gain)


def _mod_part(c_all, w_ada, b_ada_cols, name):
    B, K = c_all.shape
    N = w_ada.shape[1]
    tn = _tile(N, 512)

    def body(c_ref, w_ref, b_ref, o_ref):
        cv = c_ref[...]
        ca = (cv * jax.nn.sigmoid(cv)).astype(MXU_DTYPE)
        o_ref[...] = _dot(ca, w_ref[...].astype(MXU_DTYPE), NN) + b_ref[...]

    return pl.pallas_call(
        body, name=name, grid=(N // tn,),
        in_specs=[pl.BlockSpec((B, K), lambda j: (0, 0)), pl.BlockSpec((K, tn), lambda j: (0, j)),
                  pl.BlockSpec((1, tn), lambda j: (0, j))],
        out_specs=pl.BlockSpec((B, tn), lambda j: (0, j)),
        out_shape=jax.ShapeDtypeStruct((B, N), F32), compiler_params=_params("parallel"))(c_all, w_ada, b_ada_cols)


def _adamw_math(w, g, m, v):
    m = ADAM_B1 * m + (1.0 - ADAM_B1) * g
    v = ADAM_B2 * v + (1.0 - ADAM_B2) * (g * g)
    m_hat = m / (1.0 - ADAM_B1 ** ADAM_STEP)
    v_hat = v / (1.0 - ADAM_B2 ** ADAM_STEP)
    delta = -ADAM_LR * (m_hat / (jnp.sqrt(v_hat) + ADAM_EPS) + ADAM_WD * w)
    return delta, m, v


def _adamw(w, g, m, v, name):
    R, C = w.shape
    tr = _tile(R, max(8, (1 << 19) // C), 8)
    spec = pl.BlockSpec((tr, C), lambda i: (i, 0))

    def body(w_ref, g_ref, m_ref, v_ref, go_ref, d_ref, mo_ref, vo_ref):
        g = g_ref[...]
        go_ref[...] = g
        d_ref[...], mo_ref[...], vo_ref[...] = _adamw_math(w_ref[...], g, m_ref[...], v_ref[...])

    return pl.pallas_call(body, name=name, grid=(R // tr,), in_specs=[spec] * 4, out_specs=[spec] * 4,
                          out_shape=[jax.ShapeDtypeStruct((R, C), F32)] * 4, compiler_params=_params("parallel"))(w, g, m, v)


def _adamw_ada(c_all, dmod_cols, w, m, v, name):
    K, N = w.shape
    B = c_all.shape[0]
    tk, tn = _tile(K, 512), _tile(N, 1024)
    spec = pl.BlockSpec((tk, tn), lambda i, j: (i, j))

    def body(c_ref, dm_ref, w_ref, m_ref, v_ref, g_ref, d_ref, mo_ref, vo_ref):
        cv = c_ref[...]
        ca = (cv * jax.nn.sigmoid(cv)).astype(MXU_DTYPE)
        g = _dot(ca, dm_ref[...].astype(MXU_DTYPE), TN)
        g_ref[...] = g
        d_ref[...], mo_ref[...], vo_ref[...] = _adamw_math(w_ref[...], g, m_ref[...], v_ref[...])

    return pl.pallas_call(
        body, name=name, grid=(K // tk, N // tn),
        in_specs=[pl.BlockSpec((B, tk), lambda i, j: (0, i)), pl.BlockSpec((B, tn), lambda i, j: (0, j)), spec, spec, spec],
        out_specs=[spec] * 4, out_shape=[jax.ShapeDtypeStruct((K, N), F32)] * 4,
        compiler_params=_params("parallel", "parallel"))(c_all, dmod_cols, w, m, v)


def _sum_devices(gathered, n_dev, name):
    R = gathered.shape[0] // n_dev
    C = gathered.shape[1]
    tr = _tile(R, 512, 8)
    n_blk = R // tr

    def body(*refs):
        acc = refs[0][...]
        for r in refs[1:n_dev]:
            acc = acc + r[...]
        refs[n_dev][...] = acc

    in_specs = [pl.BlockSpec((tr, C), functools.partial(lambda i, d: (d * n_blk + i, 0), d=d)) for d in range(n_dev)]
    return pl.pallas_call(body, name=name, grid=(n_blk,), in_specs=in_specs,
                          out_specs=pl.BlockSpec((tr, C), lambda i: (i, 0)),
                          out_shape=jax.ShapeDtypeStruct((R, C), F32), compiler_params=_params("parallel"))(*([gathered] * n_dev))


def _place():
    x, y, c = lax.axis_index("x"), lax.axis_index("y"), lax.axis_index("c")
    return x, y, c


def _allgather8(blk, name):
    m_per, n = blk.shape

    def body(x_ref, out_ref, send_sems, recv_sems, local_sem):
        x, y, c = _place()
        me, sibling = (x, y, c), (x, y, 1 - c)
        chips = [(1 - x, y), (x, 1 - y), (1 - x, 1 - y)]

        def rows(px, py, pc):
            return out_ref.at[pl.ds((4 * px + 2 * py + pc) * m_per, m_per), :]

        def copy(k, block, to, src=None):
            return pltpu.make_async_remote_copy(
                src_ref=rows(*block) if src is None else src, dst_ref=rows(*block),
                send_sem=send_sems.at[k], recv_sem=recv_sems.at[k], device_id=to, device_id_type=MESH)

        mine = pltpu.make_async_copy(x_ref, rows(*me), local_sem)
        mine.start()
        first = [copy(0, me, sibling, src=x_ref)]
        first += [copy(1 + j, me, (*chip, c), src=x_ref) for j, chip in enumerate(chips)]
        for cp in first:
            cp.start()
        passed = [copy(4 + j, (*chip, c), sibling) for j, chip in enumerate(chips)]
        for j, chip in enumerate(chips):
            copy(1 + j, (*chip, c), me).wait_recv()
            passed[j].start()
        copy(0, sibling, me).wait_recv()
        for j, chip in enumerate(chips):
            copy(4 + j, (*chip, 1 - c), me).wait_recv()
        for cp in first + passed:
            cp.wait_send()
        mine.wait()

    return pl.pallas_call(
        body, name=name,
        out_shape=jax.ShapeDtypeStruct((8 * m_per, n), blk.dtype),
        in_specs=[pl.BlockSpec(memory_space=pltpu.VMEM)],
        out_specs=pl.BlockSpec(memory_space=pltpu.VMEM),
        scratch_shapes=[pltpu.SemaphoreType.DMA((7,)), pltpu.SemaphoreType.DMA((7,)), pltpu.SemaphoreType.DMA],
        compiler_params=pltpu.CompilerParams(vmem_limit_bytes=V7X_VMEM_LIMIT),
    )(blk)


class _Sharded:
    def __init__(self, shard_shape, by_cols):
        r, c = shard_shape
        self.by_cols = by_cols
        self.full = (r, N_CHIPS * c) if by_cols else (N_CHIPS * r, c)
        self.shard = (r, c)
        self.half_rows = r // 2
        self.half = (r // 2, c)

    def shard_of(self, ref, k):
        r, c = self.shard
        return ref.at[:, pl.ds(k * c, c)] if self.by_cols else ref.at[pl.ds(k * r, r), :]

    def half_of(self, ref, k, hc):
        r, c = self.shard
        h = self.half_rows
        if self.by_cols:
            return ref.at[pl.ds(hc * h, h), pl.ds(k * c, c)]
        return ref.at[pl.ds(k * r + hc * h, h), :]

    def chunk_of(self, ref, k, hc, ch, n):
        r, c = self.shard
        h = self.half_rows
        q = h // n
        if self.by_cols:
            return ref.at[pl.ds(hc * h + ch * q, q), pl.ds(k * c, c)]
        return ref.at[pl.ds(k * r + hc * h + ch * q, q), :]

    def half_of_shard(self, ref, hc):
        return ref.at[pl.ds(hc * self.half_rows, self.half_rows), :]

    def part_of_halves(self, ref, k):
        r, c = self.shard
        h = self.half_rows
        return ref.at[:, pl.ds(k * c, c)] if self.by_cols else ref.at[pl.ds(k * h, h), :]


def _on_each_place(x, y, c, fn, by_chip=True, by_core=True):
    q = 2 * x + y
    for k in range(N_CHIPS if by_chip else 1):
        for cc in range(2 if by_core else 1):
            cond = None
            if by_chip:
                cond = q == k
            if by_core:
                cond = (c == cc) if cond is None else jnp.logical_and(cond, c == cc)
            pl.when(cond)(functools.partial(fn, k, cc))


def _chip_id(k, c):
    return (k // 2, k % 2, c)


def _handshake(peers):
    barrier = pltpu.get_barrier_semaphore()
    for peer in peers:
        pl.semaphore_signal(barrier, inc=1, device_id=peer, device_id_type=MESH)
    pl.semaphore_wait(barrier, len(peers))


def _on_sequencer(body, inputs, out_structs, n_copies, peers_of, name, collective_id, return_inputs=False):
    in_refs = [jax.new_ref(a, memory_space=pltpu.MemorySpace.HBM) for a in inputs]
    out_refs = [jax.empty_ref(s, memory_space=pltpu.MemorySpace.HBM) for s in out_structs]

    @pl.kernel(mesh=plsc.ScalarSubcoreMesh(axis_name="sequencer", num_cores=1), name=name,
               scratch_types=(pltpu.SemaphoreType.DMA((n_copies,)), pltpu.SemaphoreType.DMA((n_copies,))),
               compiler_params=pltpu.CompilerParams(collective_id=collective_id))
    def launch(send_sems, recv_sems):
        x, y, c = _place()
        _handshake(peers_of(x, y, c))
        body(in_refs, out_refs, send_sems, recv_sems, x, y, c)

    launch()
    return [r[...] for r in (in_refs if return_inputs else out_refs)]


def _sibling(x, y, c):
    return [(x, y, 1 - c)]


def _same_core_of_other_chips(x, y, c):
    return [(1 - x, y, c), (x, 1 - y, c), (1 - x, 1 - y, c)]


GATHER_CHUNKS = 4
GATHER_COPIES = 6 * GATHER_CHUNKS


def _place_slab(place, parts, loss_part, rows, name):
    n_parts = len(parts)

    def body(p_ref, *refs):
        loss_ref, out_ref = refs[n_parts], refs[n_parts + 1]
        at = 0
        for ref in refs[:n_parts]:
            if len(ref.shape) == 2 and ref.shape[0] == 1:
                for r in range(ref.shape[1] // LANE):
                    out_ref[at + r:at + r + 1, :] = ref[:, r * LANE:(r + 1) * LANE]
                at += ref.shape[1] // LANE
            else:
                n_rows = math.prod(ref.shape) // LANE
                out_ref[at:at + n_rows, :] = ref[...].reshape(n_rows, LANE)
                at += n_rows
        out_ref[at:at + 8, :] = jnp.broadcast_to(loss_ref[...], (8, LANE))
        out_ref[at + 8:, :] = jnp.zeros((rows - at - 8, LANE), F32)

    def whole(shape):
        return pl.BlockSpec(shape, functools.partial(lambda i, p, nd: (0,) * nd, nd=len(shape)))

    return pl.pallas_call(
        body, name=name,
        grid_spec=pltpu.PrefetchScalarGridSpec(
            num_scalar_prefetch=1, grid=(1,),
            in_specs=[whole(a.shape) for a in parts] + [whole(loss_part.shape)],
            out_specs=pl.BlockSpec((rows, LANE), lambda i, p: (2 * p[0] + p[1], 0))),
        out_shape=jax.ShapeDtypeStruct((8 * rows, LANE), F32),
        compiler_params=_params(),
    )(place, *parts, loss_part)


def _allgather8_on_sequencer(placed, m_per, name, collective_id):
    def body(refs, _, send_sems, recv_sems, x, y, c):
        out_ref, = refs

        def at_place(k, cc):
            def rows(kk, pc):
                return out_ref.at[pl.ds((2 * kk + pc) * m_per, m_per), :]

            def copy(slot, block, to):
                return pltpu.make_async_remote_copy(src_ref=rows(*block), dst_ref=rows(*block), send_sem=send_sems.at[slot],
                                                    recv_sem=recv_sems.at[slot], device_id=to, device_id_type=MESH)

            others = [k ^ flip for flip in FLIPS]
            sends = [copy(0, (k, cc), _chip_id(k, 1 - cc))] + [copy(1 + j, (k, cc), _chip_id(kk, cc)) for j, kk in enumerate(others)]
            for cp in sends:
                cp.start()
            for j, kk in enumerate(others):
                copy(1 + j, (kk, cc), _chip_id(k, cc)).wait_recv()
                cp = copy(4 + j, (kk, cc), _chip_id(k, 1 - cc))
                cp.start()
                sends.append(cp)
            copy(0, (k, 1 - cc), _chip_id(k, cc)).wait_recv()
            for j, kk in enumerate(others):
                copy(4 + j, (kk, 1 - cc), _chip_id(k, cc)).wait_recv()
            for cp in sends:
                cp.wait_send()

        _on_each_place(x, y, c, at_place)

    def peers(x, y, c):
        return _sibling(x, y, c) + _same_core_of_other_chips(x, y, c)

    return _on_sequencer(body, [placed], [], 7, peers, name, collective_id, return_inputs=True)[0]


def _gather_weights(fulls, geoms, name, collective_id):
    n_w = len(fulls)
    n_ch, n_relay = GATHER_CHUNKS, GATHER_CHUNKS // 2
    f_refs = [jax.new_ref(f, memory_space=pltpu.MemorySpace.HBM) for f in fulls]
    FLIP_X, FLIP_Y, FLIP_BOTH = FLIPS
    TO_X, TO_Y, RELAY_TO_Y, RELAY_TO_X, ON_X, ON_Y, ON_DIAG = 0, n_ch, 2 * n_ch, 2 * n_ch + n_relay, 3 * n_ch, 4 * n_ch, 5 * n_ch

    @pl.kernel(mesh=plsc.ScalarSubcoreMesh(axis_name="sequencer", num_cores=1), name=name,
               scratch_types=(pltpu.SemaphoreType.DMA((GATHER_COPIES * n_w,)), pltpu.SemaphoreType.DMA((GATHER_COPIES * n_w,))),
               compiler_params=pltpu.CompilerParams(collective_id=collective_id))
    def launch(send_sems, recv_sems):
        x, y, c = _place()
        _handshake([(x, y, 1 - c), (1 - x, y, c), (x, 1 - y, c)])

        def at_place(k, cc):
            kx, ky, kd = k ^ FLIP_X, k ^ FLIP_Y, k ^ FLIP_BOTH
            me, sibling = _chip_id(k, cc), _chip_id(k, 1 - cc)
            started = []

            def copy(i, slot, src, dst, to, start=True):
                cp = pltpu.make_async_remote_copy(src_ref=src, dst_ref=dst, send_sem=send_sems.at[GATHER_COPIES * i + slot],
                                                  recv_sem=recv_sems.at[GATHER_COPIES * i + slot], device_id=to, device_id_type=MESH)
                if start:
                    cp.start()
                    started.append(cp)
                return cp

            def pass_on(i, slot, ref, to):
                copy(i, slot, ref, ref, to)

            def landed(i, slot, ref):
                copy(i, slot, ref, ref, me, start=False).wait_recv()

            y_order = [(n_relay + s) % n_ch for s in range(n_ch)]
            for i, (g, f_ref) in enumerate(zip(geoms, f_refs)):
                for s in range(n_ch):
                    pass_on(i, TO_X + s, g.chunk_of(f_ref, k, cc, s, n_ch), _chip_id(kx, cc))
                    pass_on(i, TO_Y + y_order[s], g.chunk_of(f_ref, k, cc, y_order[s], n_ch), _chip_id(ky, cc))
            for i, (g, f_ref) in enumerate(zip(geoms, f_refs)):
                for s in range(n_ch):
                    from_x = g.chunk_of(f_ref, kx, cc, s, n_ch)
                    landed(i, TO_X + s, from_x)
                    if s < n_relay:
                        pass_on(i, RELAY_TO_Y + s, from_x, _chip_id(ky, cc))
                    pass_on(i, ON_X + s, from_x, sibling)
                    ch = y_order[s]
                    from_y = g.chunk_of(f_ref, ky, cc, ch, n_ch)
                    landed(i, TO_Y + ch, from_y)
                    if ch >= n_relay:
                        pass_on(i, RELAY_TO_X + ch - n_relay, from_y, _chip_id(kx, cc))
                    pass_on(i, ON_Y + ch, from_y, sibling)
                for r in range(n_relay):
                    via_y = g.chunk_of(f_ref, kd, cc, r, n_ch)
                    landed(i, RELAY_TO_Y + r, via_y)
                    pass_on(i, ON_DIAG + r, via_y, sibling)
                    via_x = g.chunk_of(f_ref, kd, cc, n_relay + r, n_ch)
                    landed(i, RELAY_TO_X + r, via_x)
                    pass_on(i, ON_DIAG + n_relay + r, via_x, sibling)
            for i, (g, f_ref) in enumerate(zip(geoms, f_refs)):
                for slot, kk in ((ON_X, kx), (ON_Y, ky), (ON_DIAG, kd)):
                    for ch in range(n_ch):
                        landed(i, slot + ch, g.chunk_of(f_ref, kk, 1 - cc, ch, n_ch))
            for cp in started:
                cp.wait_send()

        _on_each_place(x, y, c, at_place)

    launch()
    return [f_ref[...] for f_ref in f_refs]


def _swap_core_halves(grads, geoms, name, collective_id):
    n_cp = sum(1 if g.by_cols else N_CHIPS for g in geoms)

    def body(g_refs, t_refs, send_sems, recv_sems, x, y, c):

        def at_place(_, cc):
            def pairs(hc):
                out = []
                for g, g_ref, t_ref in zip(geoms, g_refs, t_refs):
                    if g.by_cols:
                        out.append((g_ref.at[pl.ds(hc * g.half_rows, g.half_rows), :], t_ref))
                    else:
                        out += [(g.half_of(g_ref, k, hc), g.part_of_halves(t_ref, k)) for k in range(N_CHIPS)]
                return out

            sends = [pltpu.make_async_remote_copy(src_ref=src, dst_ref=dst, send_sem=send_sems.at[n],
                                                  recv_sem=recv_sems.at[n], device_id=(x, y, 1 - cc), device_id_type=MESH)
                     for n, (src, dst) in enumerate(pairs(1 - cc))]
            for cp in sends:
                cp.start()
            for n, (src, dst) in enumerate(pairs(cc)):
                pltpu.make_async_remote_copy(src_ref=src, dst_ref=dst, send_sem=send_sems.at[n], recv_sem=recv_sems.at[n],
                                             device_id=(x, y, cc), device_id_type=MESH).wait_recv()
            for cp in sends:
                cp.wait_send()

        _on_each_place(x, y, c, at_place, by_chip=False)

    return _on_sequencer(body, grads, [jax.ShapeDtypeStruct((g.full[0] // 2, g.full[1]), F32) for g in geoms],
                         n_cp, _sibling, name, collective_id)


def _send_to_sibling(buffers, name, collective_id):
    def body(src_refs, dst_refs, send_sems, recv_sems, x, y, c):
        def copy(i):
            return pltpu.make_async_remote_copy(src_ref=src_refs[i], dst_ref=dst_refs[i], send_sem=send_sems.at[i],
                                                recv_sem=recv_sems.at[i], device_id=(x, y, 1 - c), device_id_type=MESH)

        for i in range(len(buffers)):
            copy(i).start()
        for i in range(len(buffers)):
            copy(i).wait()

    return _on_sequencer(body, buffers, [jax.ShapeDtypeStruct(t.shape, t.dtype) for t in buffers], len(buffers),
                         _sibling, name, collective_id)


def _scatter_chip_sums(sums, geoms, name, collective_id):
    def body(s_refs, r_refs, send_sems, recv_sems, x, y, c):

        def at_place(k, _):
            sends = []
            for i, (g, s_ref, r_ref) in enumerate(zip(geoms, s_refs, r_refs)):
                for j, flip in enumerate(FLIPS):
                    kk = k ^ flip
                    cp = pltpu.make_async_remote_copy(
                        src_ref=g.part_of_halves(s_ref, kk), dst_ref=r_ref.at[j], send_sem=send_sems.at[3 * i + j],
                        recv_sem=recv_sems.at[3 * i + j], device_id=(kk // 2, kk % 2, c), device_id_type=MESH)
                    cp.start()
                    sends.append(cp)
            for i, (g, s_ref, r_ref) in enumerate(zip(geoms, s_refs, r_refs)):
                for j in range(len(FLIPS)):
                    pltpu.make_async_remote_copy(
                        src_ref=g.part_of_halves(s_ref, k), dst_ref=r_ref.at[j], send_sem=send_sems.at[3 * i + j],
                        recv_sem=recv_sems.at[3 * i + j], device_id=(x, y, c), device_id_type=MESH).wait_recv()
            for cp in sends:
                cp.wait_send()

        _on_each_place(x, y, c, at_place, by_core=False)

    return _on_sequencer(body, sums, [jax.ShapeDtypeStruct((len(FLIPS),) + g.half, WIRE_DTYPE) for g in geoms],
                         len(FLIPS) * len(sums), _same_core_of_other_chips, name, collective_id)


def _share_reduced_halves(reduced, geoms, name, collective_id):
    def body(out_refs, _, send_sems, recv_sems, x, y, c):

        def at_place(_, cc):
            sends = []
            for i, (g, ref) in enumerate(zip(geoms, out_refs)):
                mine = g.half_of_shard(ref, cc)
                cp = pltpu.make_async_remote_copy(src_ref=mine, dst_ref=mine, send_sem=send_sems.at[i],
                                                  recv_sem=recv_sems.at[i], device_id=(x, y, 1 - cc), device_id_type=MESH)
                cp.start()
                sends.append(cp)
            for i, (g, ref) in enumerate(zip(geoms, out_refs)):
                theirs = g.half_of_shard(ref, 1 - cc)
                pltpu.make_async_remote_copy(src_ref=theirs, dst_ref=theirs, send_sem=send_sems.at[i],
                                             recv_sem=recv_sems.at[i], device_id=(x, y, cc), device_id_type=MESH).wait_recv()
            for cp in sends:
                cp.wait_send()

        _on_each_place(x, y, c, at_place, by_chip=False)

    return _on_sequencer(body, reduced, [], len(reduced), _sibling, name, collective_id, return_inputs=True)


def _chip_sum(place, grad, theirs, g, name):
    RH, C = theirs.shape
    h = g.half_rows
    tr = _tile(h, 256, 16)
    tc = _tile(C, 2048)
    per_half = h // tr

    if g.by_cols:
        grad_map = lambda i, j, p: (p[1] * per_half + i, j)
    else:
        grad_map = lambda i, j, p: ((i // per_half) * 2 * per_half + p[1] * per_half + i % per_half, j)

    def body(p_ref, a_ref, b_ref, f_ref, o_ref):
        total = a_ref[...] + b_ref[...]
        f_ref[...] = total
        o_ref[...] = total.astype(o_ref.dtype)

    return pl.pallas_call(
        body, name=name,
        grid_spec=pltpu.PrefetchScalarGridSpec(
            num_scalar_prefetch=1, grid=(RH // tr, C // tc),
            in_specs=[pl.BlockSpec((tr, tc), grad_map), pl.BlockSpec((tr, tc), lambda i, j, p: (i, j))],
            out_specs=[pl.BlockSpec((tr, tc), lambda i, j, p: (i, j))] * 2),
        out_shape=[jax.ShapeDtypeStruct((RH, C), F32), jax.ShapeDtypeStruct((RH, C), WIRE_DTYPE)],
        compiler_params=_params("parallel", "parallel"),
    )(place, grad, theirs)


def _dw_half(place, a, b, g, mine, name, add=None):
    K, R = a.shape
    C = b.shape[1]
    h = g.half_rows
    tm, tn = _tile(h, 1024, 16), _tile(C, 512)
    per_half = h // tm
    n_i = (R // 2) // tm

    def a_map(i, j, p):
        hc = p[1] if mine else 1 - p[1]
        if g.by_cols:
            return 0, hc * n_i + i
        return 0, (i // per_half) * 2 * per_half + hc * per_half + i % per_half

    mn_spec = pl.BlockSpec((tm, tn), lambda i, j, p: (i, j))

    def body(p_ref, a_ref, b_ref, *rest):
        acc = _dot(a_ref[...], b_ref[...], TN)
        if add is None:
            rest[0][...] = acc
        else:
            total = acc + rest[0][...]
            rest[1][...] = total
            rest[2][...] = total.astype(rest[2].dtype)

    out_shape = [jax.ShapeDtypeStruct((R // 2, C), F32)] + ([] if add is None else [jax.ShapeDtypeStruct((R // 2, C), WIRE_DTYPE)])
    return pl.pallas_call(
        body, name=name,
        grid_spec=pltpu.PrefetchScalarGridSpec(
            num_scalar_prefetch=1, grid=(n_i, C // tn),
            in_specs=[pl.BlockSpec((K, tm), a_map), pl.BlockSpec((K, tn), lambda i, j, p: (0, j))] + ([] if add is None else [mn_spec]),
            out_specs=[mn_spec] * len(out_shape)),
        out_shape=out_shape,
        compiler_params=_params("parallel", "arbitrary"),
    )(place, a, b, *([] if add is None else [add]))


def _reduce_half(place, sums, others, g, name):
    h, tc = g.half
    tr = _tile(h, 256, 16)
    per_half = h // tr
    sums_map = (lambda i, p: (i, p[0])) if g.by_cols else (lambda i, p: (p[0] * per_half + i, 0))

    def body(p_ref, s_ref, o0_ref, o1_ref, o2_ref, out_ref):
        acc = s_ref[...]
        for o_ref in (o0_ref, o1_ref, o2_ref):
            acc = acc + o_ref[...].astype(F32)
        out_ref[...] = acc

    other_specs = [pl.BlockSpec((None, tr, tc), functools.partial(lambda i, p, j: (j, i, 0), j=j)) for j in range(len(FLIPS))]
    return pl.pallas_call(
        body, name=name,
        grid_spec=pltpu.PrefetchScalarGridSpec(
            num_scalar_prefetch=1, grid=(per_half,),
            in_specs=[pl.BlockSpec((tr, tc), sums_map)] + other_specs,
            out_specs=pl.BlockSpec((tr, tc), lambda i, p: (p[1] * per_half + i, 0))),
        out_shape=jax.ShapeDtypeStruct(g.shard, F32),
        compiler_params=_params("arbitrary"),
    )(place, sums, others, others, others)


SLAB_ROW_UNIT = 256
SMALL = ("b_ada", "norm1_g", "v_norm_g", "w_spatial", "b_spatial", "out_norm_g", "norm2_g", "final_g")
BIG = ("w_in", "w_out", "w_gate", "w_up", "w_down")
BY_COLS = {"w_in": True, "w_out": False, "w_gate": True, "w_up": True, "w_down": False}
ORDER = ("w_ada", "b_ada", "norm1_g", "w_in", "v_norm_g", "w_spatial", "b_spatial", "out_norm_g", "w_out",
         "norm2_g", "w_gate", "w_up", "w_down", "final_g")


def _pack(parts):
    return jnp.concatenate([parts[n].reshape(-1) for n in SMALL]).reshape(-1, LANE)


def _adamw_small(w, g, m, v, shapes, name):
    R = w.shape[0]
    slab_spec = pl.BlockSpec((R, LANE), lambda: (0, 0))
    out_shapes = [shapes[n] if len(shapes[n]) > 1 else (1,) + tuple(shapes[n]) for n in SMALL]

    def body(w_ref, g_ref, m_ref, v_ref, *out_refs):
        gv = g_ref[...]
        results = (gv,) + _adamw_math(w_ref[...], gv, m_ref[...], v_ref[...])
        for kind, val in enumerate(results):
            at = 0
            for i, shp in enumerate(out_shapes):
                o_ref = out_refs[kind * len(SMALL) + i]
                n_rows = math.prod(shp) // LANE
                if len(shp) == 2:
                    for r in range(n_rows):
                        o_ref[:, r * LANE:(r + 1) * LANE] = val[at + r:at + r + 1, :]
                else:
                    o_ref[0] = val[at:at + n_rows, :].reshape(shp[1:])
                at += n_rows

    outs = pl.pallas_call(
        body, name=name, in_specs=[slab_spec] * 4,
        out_specs=[pl.BlockSpec(shp, functools.partial(lambda nd: (0,) * nd, len(shp))) for shp in out_shapes] * 4,
        out_shape=[jax.ShapeDtypeStruct(shp, F32) for shp in out_shapes] * 4,
        compiler_params=_params(),
    )(w, g, m, v)
    dicts = []
    for kind in range(4):
        part = outs[kind * len(SMALL):(kind + 1) * len(SMALL)]
        dicts.append({n: a.reshape(shapes[n]) for n, a in zip(SMALL, part)})
    return dicts


def kernel(x, c, w_ada, b_ada, norm1_g, w_in, v_norm_g, w_spatial, b_spatial, out_norm_g, w_out, norm2_g, w_gate, w_up, w_down, final_g, loss_target, m_w_ada, m_b_ada, m_norm1_g, m_w_in, m_v_norm_g, m_w_spatial, m_b_spatial, m_out_norm_g, m_w_out, m_norm2_g, m_w_gate, m_w_up, m_w_down, m_final_g, v_w_ada, v_b_ada, v_norm1_g, v_w_in, v_v_norm_g, v_w_spatial, v_b_spatial, v_out_norm_g, v_w_out, v_norm2_g, v_w_gate, v_w_up, v_w_down, v_final_g):
    weights = dict(w_ada=w_ada, b_ada=b_ada, norm1_g=norm1_g, w_in=w_in, v_norm_g=v_norm_g, w_spatial=w_spatial,
                   b_spatial=b_spatial, out_norm_g=out_norm_g, w_out=w_out, norm2_g=norm2_g, w_gate=w_gate, w_up=w_up,
                   w_down=w_down, final_g=final_g)
    m_in = dict(w_ada=m_w_ada, b_ada=m_b_ada, norm1_g=m_norm1_g, w_in=m_w_in, v_norm_g=m_v_norm_g, w_spatial=m_w_spatial,
                b_spatial=m_b_spatial, out_norm_g=m_out_norm_g, w_out=m_w_out, norm2_g=m_norm2_g, w_gate=m_w_gate,
                w_up=m_w_up, w_down=m_w_down, final_g=m_final_g)
    v_in = dict(w_ada=v_w_ada, b_ada=v_b_ada, norm1_g=v_norm1_g, w_in=v_w_in, v_norm_g=v_v_norm_g, w_spatial=v_w_spatial,
                b_spatial=v_b_spatial, out_norm_g=v_out_norm_g, w_out=v_w_out, norm2_g=v_norm2_g, w_gate=v_w_gate,
                w_up=v_w_up, w_down=v_w_down, final_g=v_final_g)

    S, D = x.shape[1], x.shape[2]
    n_g = v_norm_g.shape[-1] // LANE
    n_h = (D - n_g * LANE) // LANE
    GW = n_g * LANE
    xi, yi, ci = _place()
    chip = 2 * xi + yi
    me = 4 * xi + 2 * yi + ci
    place = jnp.stack([chip, ci]).astype(jnp.int32)

    xs, target = x[0], loss_target[0]
    geoms = [_Sharded(weights[n].shape[1:], BY_COLS[n]) for n in BIG]

    full = {}
    for i, group in enumerate((("w_in",), ("w_out",), ("w_gate", "w_up"), ("w_down",))):
        gg = [geoms[BIG.index(n)] for n in group]
        own = [_cast_into_full(place, weights[n][0], g, "cast_" + n) for n, g in zip(group, gg)]
        gathered = _gather_weights(own, gg, "gather_" + "_".join(group), 1 + i)
        full.update(zip(group, gathered))

    c_pad = jnp.concatenate([c, jnp.zeros((7, D), F32)], axis=0)
    c_all = _allgather8(c_pad, "gather_c")[::8]
    n_ada = w_ada.shape[2]
    b_cols = lax.dynamic_slice(b_ada, (0, chip * n_ada), (1, n_ada))
    mod_parts = _allgather8(_mod_part(c_all, w_ada[0], b_cols, "mod_part"), "gather_mod")
    mod_all = mod_parts.reshape(N_CHIPS, 2, 8, n_ada)[:, 0].transpose(1, 0, 2).reshape(8, N_CHIPS * n_ada)
    mod = lax.dynamic_slice(mod_all, (me, 0), (1, 6 * D))
    shift1, scale1, gate1, shift2, scale2, gate2 = [mod[:, i * D:(i + 1) * D] for i in range(6)]

    b_t = b_spatial[0].T
    h1 = _norm_mod(xs, norm1_g, scale1, shift1, "norm1")
    proj, = _mm("nn", h1, full["w_in"], [F32], "proj")
    on_gm = _gmlp_fwd(proj, v_norm_g, w_spatial[0], b_t, out_norm_g, n_g, "gmlp_fwd")
    o_sb, o_n, l_sum = _sb_fwd(proj, out_norm_g, on_gm, n_g, n_h, "sb_fwd")
    attn, = _mm("nn", o_n, full["w_out"], [F32], "attn_out")
    x1, h2 = _residual_norm_mod(xs, attn, gate1, norm2_g, scale2, shift2, "norm2")
    a_g, a_u, f_in = _gate_up(h2, full["w_gate"], full["w_up"], "gate_up")
    f, = _mm("nn", f_in, full["w_down"], [F32], "down", tm=1024)
    dx2, df, d_gate2, d_final_g, loss_part = _final_loss_bwd(x1, f, gate2, final_g.reshape(1, D), target, "final")

    geom_of = dict(zip(BIG, geoms))
    grad_out, delta, new_m, new_v = {}, {}, {}, {}

    def theirs_first(group, operands, collective_id, after=None):
        outs = []
        for n, (a_op, b_op) in zip(group, operands):
            outs.append(_dw_half(place, a_op, b_op if after is None else _then(after, b_op), geom_of[n], False, "d_" + n + "_theirs")[0])
            after = outs[-1]
        return outs, _send_to_sibling(outs, "swap_" + "_".join(group), collective_id)

    def chip_sums(group, operands, theirs, after):
        f32s, wires = [], []
        for n, (a_op, b_op), t in zip(group, operands, theirs):
            sf, sw = _dw_half(place, a_op, b_op, geom_of[n], True, "d_" + n + "_mine", add=_then(after, t))
            f32s.append(sf)
            wires.append(sw)
            after = sw
        return f32s, wires

    def scatter(group, sums, collective_id):
        return _scatter_chip_sums(sums, [geom_of[n] for n in group], "scatter_" + "_".join(group), collective_id)

    def reduce_halves(group, sums, others, after):
        return [_reduce_half(place, sf, _then(after, o), geom_of[n], "reduce_" + n) for n, sf, o in zip(group, sums, others)]

    def share(group, halves, collective_id):
        return _share_reduced_halves(halves, [geom_of[n] for n in group], "share_" + "_".join(group), collective_id)

    def adamw(group, reduced, after):
        for n, r in zip(group, reduced):
            go, d, mo, vo = _adamw(weights[n][0], _then(after, r), m_in[n][0], v_in[n][0], "adamw_" + n)
            grad_out[n], delta[n], new_m[n], new_v[n] = go[None], d[None], mo[None], vo[None]
        return d

    g_down = ("w_down",)
    g_ffn = ("w_gate", "w_up")
    g_out = ("w_out",)
    g_in = ("w_in",)

    gr_down, = _mm("tn", f_in, df, [F32], "d_w_down", tm=1408, tn=1024)
    th_down, = _swap_core_halves([gr_down], [geom_of["w_down"]], "swap_w_down", 6)
    d_ag, d_au = _mm("nt", df, full["w_down"], [MXU_DTYPE, MXU_DTYPE], "d_ffn_in", extras=(a_g, a_u),
                     epilogue=_swiglu_bwd_epilogue)
    sf_down, sw_down = [[t] for t in _chip_sum(place, gr_down, _then(d_ag, th_down), geom_of["w_down"], "chip_sum_w_down")]
    ot_down = scatter(g_down, sw_down, 7)
    sent, th_ffn = theirs_first(g_ffn, [(h2, d_ag), (h2, d_au)], 9, after=sw_down)
    dh2 = _mm_ktiled("nt", [(_then(sent, d_ag), full["w_gate"]), (d_au, full["w_up"])], "d_h2", tn=512)
    sf_ffn, sw_ffn = chip_sums(g_ffn, [(h2, d_ag), (h2, d_au)], th_ffn, after=dh2)
    ot_ffn = scatter(g_ffn, sw_ffn, 10)
    hv_down = reduce_halves(g_down, sf_down, ot_down, after=sw_ffn)
    rd_down = share(g_down, hv_down, 8)
    dx1, d_shift2, d_scale2, d_norm2_g, d_gate1, d_attn = _norm_mod_bwd(
        _then(hv_down, dh2), x1, dx2, norm2_g, scale2, "norm2_bwd", branch=attn, gate=gate1)
    gr_out, = _mm("tn", o_n, d_attn, [F32], "d_w_out")
    th_out, = _swap_core_halves([gr_out], [geom_of["w_out"]], "swap_w_out", 12)
    d_on, = _mm("nt", _then(gr_out, d_attn), full["w_out"], [F32], "d_o")
    sf_out, sw_out = [[t] for t in _chip_sum(place, gr_out, _then(d_on, th_out), geom_of["w_out"], "chip_sum_w_out")]
    ot_out = scatter(g_out, sw_out, 13)
    dq, dk, dv, d_og_sb = _sb_bwd(proj, o_sb, l_sum, _then(sw_out, d_on), out_norm_g, n_g, n_h, "sb_bwd")
    dproj, d_w_spatial, d_b_t, d_v_norm_g, d_og_gm = _gmlp_bwd(proj, d_on, v_norm_g, w_spatial[0], b_t, out_norm_g,
                                                                (dq, dk, dv), n_g, "gmlp_bwd")
    sent, th_in = theirs_first(g_in, [(h1, dproj)], 15)
    hv_ffn = reduce_halves(g_ffn, sf_ffn, ot_ffn, after=sent)
    rd_ffn = share(g_ffn, hv_ffn, 11)
    dh1, = _mm("nt", _then(sent, dproj), full["w_in"], [F32], "d_h1", tm=1024)
    hv_out = reduce_halves(g_out, sf_out, ot_out, after=dh1)
    rd_out = share(g_out, hv_out, 14)
    grad_x, d_shift1, d_scale1, d_norm1_g = _norm_mod_bwd(_then(hv_out, dh1), xs, dx1, norm1_g, scale1, "norm1_bwd")

    small_parts = [d_shift1, d_scale1, d_gate1, d_shift2, d_scale2, d_gate2,
                   d_norm1_g, d_v_norm_g, d_w_spatial, d_b_t.T, d_og_gm, d_og_sb, d_norm2_g, d_final_g]
    small_rows = sum(math.prod(p.shape) for p in small_parts) // LANE
    rows = -(-(small_rows + 8) // SLAB_ROW_UNIT) * SLAB_ROW_UNIT
    slab = _place_slab(place, small_parts, loss_part, rows, "place_slab")
    gathered = _allgather8_on_sequencer(slab, rows, "gather_small", 18)
    sf_in, sw_in = chip_sums(g_in, [(h1, dproj)], th_in, after=slab)
    ot_in = scatter(g_in, sw_in, 16)
    done = adamw(g_down, rd_down, after=sw_in)
    done = adamw(g_ffn, rd_ffn, after=done)
    done = adamw(g_out, rd_out, after=done)
    gathered = _then(done, gathered)
    small_shapes = {n: weights[n].shape for n in SMALL}
    slab_sum = _sum_devices(gathered, 8, "sum_small")
    small_sum, loss = slab_sum[:small_rows], slab_sum[small_rows, 0]
    dmod_all = gathered.reshape(8, rows * LANE)[:, :6 * D]
    dmod_cols = lax.dynamic_slice(dmod_all, (0, chip * n_ada), (8, n_ada))
    g_ada, d, mo, vo = _adamw_ada(c_all, dmod_cols, w_ada[0], m_w_ada[0], v_w_ada[0], "adamw_w_ada")
    grad_out["w_ada"], delta["w_ada"], new_m["w_ada"], new_v["w_ada"] = g_ada[None], d[None], mo[None], vo[None]
    small_out = _adamw_small(_pack({n: weights[n] for n in SMALL}), small_sum, _pack({n: m_in[n] for n in SMALL}),
                             _pack({n: v_in[n] for n in SMALL}), small_shapes, "adamw_small")
    for dst, part in zip((grad_out, delta, new_m, new_v), small_out):
        dst.update(part)
    hv_in = reduce_halves(g_in, sf_in, ot_in, after=d)
    adamw(g_in, share(g_in, hv_in, 17), after=d)

    return (loss, grad_x[None], *[grad_out[n] for n in ORDER], *[delta[n] for n in ORDER],
            *[new_m[n] for n in ORDER], *[new_v[n] for n in ORDER])
```

```python
import functools
import math

import jax
import jax.numpy as jnp
from jax import lax
from jax.experimental import pallas as pl
from jax.experimental.pallas import tpu as pltpu
from jax.experimental.pallas import tpu_sc as plsc

F32 = jnp.float32
MXU_DTYPE = jnp.bfloat16
WIRE_DTYPE = jnp.bfloat16
EPS = 1e-6
LANE = 128
V7X_VMEM_LIMIT = 56 * 1024 * 1024
MESH = pl.DeviceIdType.MESH
N_CHIPS = 4
FLIPS = (2, 1, 3)

ADAM_LR = 0.001
ADAM_B1 = 0.9
ADAM_B2 = 0.999
ADAM_EPS = 1e-08
ADAM_WD = 0.01
ADAM_STEP = 10


def _params(*semantics):
    return pltpu.CompilerParams(dimension_semantics=semantics or None, vmem_limit_bytes=V7X_VMEM_LIMIT)


def _tile(dim, pref, unit=LANE):
    best = None
    t = unit
    while t <= min(dim, pref):
        if dim % t == 0:
            best = t
        t += unit
    return best if best is not None else dim


def _then(first, second):
    return lax.optimization_barrier((first, second))[1]


def _sum0(v):
    return jnp.sum(v, axis=0, keepdims=True)


def _mean1(v):
    return jnp.mean(v, axis=-1, keepdims=True)


def _gelu(x):
    return 0.5 * x * (1.0 + lax.erf(x * (1.0 / math.sqrt(2.0))))


def _gelu_grad(x):
    cdf = 0.5 * (1.0 + lax.erf(x * (1.0 / math.sqrt(2.0))))
    return cdf + x * jnp.exp(-0.5 * x * x) * (1.0 / math.sqrt(2.0 * math.pi))


def _dot(a, b, dims):
    return lax.dot_general(a, b, (dims, ((), ())), preferred_element_type=F32)


NN = ((1,), (0,))
NT = ((1,), (1,))
TN = ((0,), (0,))


def _mm(kind, a, b, out_dtypes, name, tm=2048, tn=512, extras=(), epilogue=None):
    if kind == "nn":
        (M, K), N = a.shape, b.shape[1]
    elif kind == "nt":
        (M, K), N = a.shape, b.shape[0]
    else:
        (K, M), N = a.shape, b.shape[1]
    tm, tn = _tile(M, tm), _tile(N, tn)
    a_spec = pl.BlockSpec((K, tm), lambda i, j: (0, i)) if kind == "tn" else pl.BlockSpec((tm, K), lambda i, j: (i, 0))
    b_spec = pl.BlockSpec((tn, K), lambda i, j: (j, 0)) if kind == "nt" else pl.BlockSpec((K, tn), lambda i, j: (0, j))
    mn_spec = pl.BlockSpec((tm, tn), lambda i, j: (i, j))
    dims = {"nn": NN, "nt": NT, "tn": TN}[kind]
    n_extra = len(extras)

    n_chunks = 1 if epilogue is None or kind == "tn" else max(1, tm // 256)
    rows_per = tm // n_chunks

    def body(a_ref, b_ref, *rest):
        for r in range(n_chunks):
            rows = slice(r * rows_per, (r + 1) * rows_per)
            acc = _dot(a_ref[...] if n_chunks == 1 else a_ref[rows, :], b_ref[...], dims)
            res = (acc,) if epilogue is None else epilogue(acc, *[e[rows, :] for e in rest[:n_extra]])
            for o_ref, val in zip(rest[n_extra:], res):
                o_ref[rows, :] = val.astype(o_ref.dtype)

    outs = pl.pallas_call(
        body, name=name, grid=(M // tm, N // tn),
        in_specs=[a_spec, b_spec] + [mn_spec] * n_extra,
        out_specs=[mn_spec] * len(out_dtypes),
        out_shape=[jax.ShapeDtypeStruct((M, N), d) for d in out_dtypes],
        compiler_params=_params("parallel", "arbitrary"),
    )(a, b, *extras)
    return outs


def _mm_ktiled(kind, pairs, name, tm=2048, tn=1024, tk=1408):
    a0, b0 = pairs[0]
    M, K = a0.shape
    N = b0.shape[1] if kind == "nn" else b0.shape[0]
    tm, tn, tk = _tile(M, tm), _tile(N, tn), _tile(K, tk)
    a_spec = pl.BlockSpec((tm, tk), lambda i, j, k: (i, k))
    b_spec = pl.BlockSpec((tk, tn), lambda i, j, k: (k, j)) if kind == "nn" else pl.BlockSpec((tn, tk), lambda i, j, k: (j, k))
    dims = NN if kind == "nn" else NT
    n_pairs = len(pairs)

    def body(*refs):
        o_ref = refs[2 * n_pairs]
        acc = _dot(refs[0][...], refs[1][...], dims)
        for p in range(1, n_pairs):
            acc = acc + _dot(refs[2 * p][...], refs[2 * p + 1][...], dims)

        @pl.when(pl.program_id(2) == 0)
        def _():
            o_ref[...] = acc

        @pl.when(pl.program_id(2) != 0)
        def _():
            o_ref[...] += acc

    return pl.pallas_call(
        body, name=name, grid=(M // tm, N // tn, K // tk),
        in_specs=[a_spec, b_spec] * n_pairs,
        out_specs=pl.BlockSpec((tm, tn), lambda i, j, k: (i, j)),
        out_shape=jax.ShapeDtypeStruct((M, N), F32),
        compiler_params=_params("parallel", "parallel", "arbitrary"),
    )(*[x for pair in pairs for x in pair])


def _gate_up(h, wg, wu, name):
    (M, K), N = h.shape, wg.shape[1]
    tm, tn = _tile(M, 2048), _tile(N, 512)

    n_chunks = max(1, tm // 256)
    rows_per = tm // n_chunks

    def body(h_ref, wg_ref, wu_ref, ag_ref, au_ref, f_ref):
        for r in range(n_chunks):
            rows = slice(r * rows_per, (r + 1) * rows_per)
            hv = h_ref[rows, :]
            ag = _dot(hv, wg_ref[...], NN)
            au = _dot(hv, wu_ref[...], NN)
            ag_ref[rows, :] = ag.astype(ag_ref.dtype)
            au_ref[rows, :] = au.astype(au_ref.dtype)
            f_ref[rows, :] = (ag * jax.nn.sigmoid(ag) * au).astype(f_ref.dtype)

    w_spec = pl.BlockSpec((K, tn), lambda i, j: (0, j))
    mn_spec = pl.BlockSpec((tm, tn), lambda i, j: (i, j))
    return pl.pallas_call(
        body, name=name, grid=(M // tm, N // tn),
        in_specs=[pl.BlockSpec((tm, K), lambda i, j: (i, 0)), w_spec, w_spec],
        out_specs=[mn_spec] * 3,
        out_shape=[jax.ShapeDtypeStruct((M, N), MXU_DTYPE)] * 3,
        compiler_params=_params("parallel", "arbitrary"),
    )(h, wg, wu)


def _swiglu_bwd_epilogue(dfin, ag, au):
    ag, au = ag.astype(F32), au.astype(F32)
    sg = jax.nn.sigmoid(ag)
    d_au = dfin * (ag * sg)
    d_ag = dfin * au * (sg * (1.0 + ag * (1.0 - sg)))
    return d_ag, d_au


def _row_specs(ts, width):
    return pl.BlockSpec((ts, width), lambda i: (i, 0)), pl.BlockSpec((1, width), lambda i: (0, 0))


def _cast_into_full(place, shard, g, name):
    R, C = shard.shape
    tr = _tile(R, 256, 16)
    n_blk = R // tr
    out_map = (lambda i, p: (i, p[0])) if g.by_cols else (lambda i, p: (p[0] * n_blk + i, 0))

    def body(p_ref, a_ref, o_ref):
        o_ref[...] = a_ref[...].astype(o_ref.dtype)

    return pl.pallas_call(
        body, name=name,
        grid_spec=pltpu.PrefetchScalarGridSpec(
            num_scalar_prefetch=1, grid=(n_blk,),
            in_specs=[pl.BlockSpec((tr, C), lambda i, p: (i, 0))],
            out_specs=pl.BlockSpec((tr, C), out_map)),
        out_shape=jax.ShapeDtypeStruct(g.full, WIRE_DTYPE),
        compiler_params=_params("arbitrary"),
    )(place, shard)


def _norm_mod(x, g, scale, shift, name):
    S, D = x.shape
    ts = _tile(S, 256, 16)
    tile, vec = _row_specs(ts, D)

    def body(x_ref, g_ref, sc_ref, sh_ref, h_ref):
        xv = x_ref[...]
        r = lax.rsqrt(_mean1(xv * xv) + EPS)
        h_ref[...] = ((xv * r) * g_ref[...] * (1.0 + sc_ref[...]) + sh_ref[...]).astype(h_ref.dtype)

    return pl.pallas_call(body, name=name, grid=(S // ts,), in_specs=[tile, vec, vec, vec], out_specs=tile,
                          out_shape=jax.ShapeDtypeStruct((S, D), MXU_DTYPE), compiler_params=_params("parallel"))(x, g, scale, shift)


def _residual_norm_mod(x, attn, gate, g, scale, shift, name):
    S, D = x.shape
    ts = _tile(S, 256, 16)
    tile, vec = _row_specs(ts, D)

    def body(x_ref, a_ref, gate_ref, g_ref, sc_ref, sh_ref, x1_ref, h_ref):
        x1 = x_ref[...] + gate_ref[...] * a_ref[...]
        x1_ref[...] = x1
        r = lax.rsqrt(_mean1(x1 * x1) + EPS)
        h_ref[...] = ((x1 * r) * g_ref[...] * (1.0 + sc_ref[...]) + sh_ref[...]).astype(h_ref.dtype)

    return pl.pallas_call(body, name=name, grid=(S // ts,), in_specs=[tile, tile, vec, vec, vec, vec],
                          out_specs=[tile, tile],
                          out_shape=[jax.ShapeDtypeStruct((S, D), F32), jax.ShapeDtypeStruct((S, D), MXU_DTYPE)],
                          compiler_params=_params("parallel"))(x, attn, gate, g, scale, shift)


def _final_loss_bwd(x1, f, gate2, final_g, target, name):
    S, D = x1.shape
    ts = _tile(S, 256, 16)
    tile, vec = _row_specs(ts, D)
    loss_spec = pl.BlockSpec((1, LANE), lambda i: (0, 0))

    def body(x1_ref, f_ref, gate_ref, g_ref, t_ref, dx2_ref, df_ref, dgate_ref, dg_ref, loss_ref):
        @pl.when(pl.program_id(0) == 0)
        def _():
            dgate_ref[...] = jnp.zeros_like(dgate_ref)
            dg_ref[...] = jnp.zeros_like(dg_ref)
            loss_ref[...] = jnp.zeros_like(loss_ref)

        fv, gate, g = f_ref[...], gate_ref[...], g_ref[...]
        x2 = x1_ref[...] + gate * fv
        r = lax.rsqrt(_mean1(x2 * x2) + EPS)
        xn = x2 * r
        err = xn * g - t_ref[...]
        loss_ref[...] += jnp.broadcast_to(0.5 * _sum0(_mean1(err * err)), loss_ref.shape)
        dy = err * (1.0 / D)
        dg_ref[...] += _sum0(dy * xn)
        dxn = dy * g
        dx2 = r * (dxn - xn * _mean1(dxn * xn))
        dx2_ref[...] = dx2
        dgate_ref[...] += _sum0(dx2 * fv)
        df_ref[...] = (dx2 * gate).astype(df_ref.dtype)

    return pl.pallas_call(
        body, name=name, grid=(S // ts,), in_specs=[tile, tile, vec, vec, tile],
        out_specs=[tile, tile, vec, vec, loss_spec],
        out_shape=[jax.ShapeDtypeStruct((S, D), F32), jax.ShapeDtypeStruct((S, D), MXU_DTYPE),
                   jax.ShapeDtypeStruct((1, D), F32), jax.ShapeDtypeStruct((1, D), F32),
                   jax.ShapeDtypeStruct((1, LANE), F32)],
        compiler_params=_params("arbitrary"),
    )(x1, f, gate2, final_g, target)


def _norm_mod_bwd(dh, xin, dres, g, scale, name, branch=None, gate=None):
    S, D = xin.shape
    ts = _tile(S, 256, 16)
    tile, vec = _row_specs(ts, D)
    with_gate = branch is not None

    def body(*refs):
        if with_gate:
            dh_ref, x_ref, dres_ref, g_ref, sc_ref, br_ref, gate_ref, dx_ref, dshift_ref, dscale_ref, dg_ref, dgate_ref, dbr_ref = refs
            accs = (dshift_ref, dscale_ref, dg_ref, dgate_ref)
        else:
            dh_ref, x_ref, dres_ref, g_ref, sc_ref, dx_ref, dshift_ref, dscale_ref, dg_ref = refs
            accs = (dshift_ref, dscale_ref, dg_ref)

        @pl.when(pl.program_id(0) == 0)
        def _():
            for acc in accs:
                acc[...] = jnp.zeros_like(acc)

        dh_v, xv, g_v = dh_ref[...], x_ref[...], g_ref[...]
        one_sc = 1.0 + sc_ref[...]
        r = lax.rsqrt(_mean1(xv * xv) + EPS)
        xn = xv * r
        dshift_ref[...] += _sum0(dh_v)
        dscale_ref[...] += _sum0(dh_v * (xn * g_v))
        dg_ref[...] += _sum0(dh_v * one_sc * xn)
        dxn = dh_v * (g_v * one_sc)
        dx = dres_ref[...] + r * (dxn - xn * _mean1(dxn * xn))
        dx_ref[...] = dx
        if with_gate:
            dgate_ref[...] += _sum0(dx * br_ref[...])
            dbr_ref[...] = (dx * gate_ref[...]).astype(dbr_ref.dtype)

    ins = [dh, xin, dres, g, scale] + ([branch, gate] if with_gate else [])
    in_specs = [tile, tile, tile, vec, vec] + ([tile, vec] if with_gate else [])
    out_specs = [tile, vec, vec, vec] + ([vec, tile] if with_gate else [])
    out_shape = [jax.ShapeDtypeStruct((S, D), F32)] + [jax.ShapeDtypeStruct((1, D), F32)] * 3
    if with_gate:
        out_shape += [jax.ShapeDtypeStruct((1, D), F32), jax.ShapeDtypeStruct((S, D), MXU_DTYPE)]
    return pl.pallas_call(body, name=name, grid=(S // ts,), in_specs=in_specs, out_specs=out_specs,
                          out_shape=out_shape, compiler_params=_params("arbitrary"))(*ins)


def _causal_weights(ws_ref, wt_ref, n_g):
    row = lax.broadcasted_iota(jnp.int32, (LANE, LANE), 0)
    col = lax.broadcasted_iota(jnp.int32, (LANE, LANE), 1)
    for g in range(n_g):
        wt_ref[g] = jnp.where(col <= row, ws_ref[g], 0.0).astype(wt_ref.dtype)


def _group_layernorm(v):
    xc = v - _mean1(v)
    rstd = lax.rsqrt(_mean1(xc * xc) + EPS)
    return xc * rstd, rstd


def _gmlp_fwd(proj, v_gain, w_s, b_t, out_gain, n_g, name):
    S = proj.shape[0]
    GW = n_g * LANE
    D = out_gain.shape[1]

    def body(p_ref, vg_ref, ws_ref, bt_ref, og_ref, on_ref, wt_ref):
        @pl.when(pl.program_id(0) == 0)
        def _():
            _causal_weights(ws_ref, wt_ref, n_g)

        for g in range(n_g):
            cols = slice(g * LANE, (g + 1) * LANE)
            u = _gelu(p_ref[:, cols])
            v = _gelu(p_ref[:, GW + g * LANE:GW + (g + 1) * LANE])
            vhat, _ = _group_layernorm(v)
            vln = (vhat * vg_ref[:, cols]).astype(MXU_DTYPE)
            mixed = _dot(wt_ref[g], vln, NN) + bt_ref[:, g:g + 1]
            o = u * mixed
            r = lax.rsqrt(_mean1(o * o) + EPS)
            on_ref[:, cols] = (o * r * og_ref[:, cols]).astype(on_ref.dtype)

    return pl.pallas_call(
        body, name=name, grid=(S // LANE,),
        in_specs=[pl.BlockSpec((LANE, 2 * GW), lambda n: (n, 0)),
                  pl.BlockSpec((1, GW), lambda n: (0, 0)),
                  pl.BlockSpec((n_g, LANE, LANE), lambda n: (0, 0, 0)),
                  pl.BlockSpec((LANE, n_g), lambda n: (0, 0)),
                  pl.BlockSpec((1, GW), lambda n: (0, 0))],
        out_specs=pl.BlockSpec((LANE, GW), lambda n: (n, 0)),
        out_shape=jax.ShapeDtypeStruct((S, D), MXU_DTYPE),
        scratch_shapes=[pltpu.VMEM((n_g, LANE, LANE), MXU_DTYPE)],
        compiler_params=_params("arbitrary"),
    )(proj, v_gain, w_s, b_t, out_gain)


def _gmlp_bwd(proj, d_on, v_gain, w_s, b_t, out_gain, dqkv, n_g, name):
    S, N_IN = proj.shape
    GW = n_g * LANE
    SBW = dqkv[0].shape[1]

    def body(p_ref, dn_ref, vg_ref, ws_ref, bt_ref, og_ref, dq_ref, dk_ref, dv_ref, dp_ref, dws_ref, dbt_ref, dvg_ref, dog_ref, wt_ref):
        for i, part_ref in enumerate((dq_ref, dk_ref, dv_ref)):
            dp_ref[:, 2 * GW + i * SBW:2 * GW + (i + 1) * SBW] = part_ref[...]

        @pl.when(pl.program_id(0) == 0)
        def _():
            _causal_weights(ws_ref, wt_ref, n_g)
            dws_ref[...] = jnp.zeros_like(dws_ref)
            dbt_ref[...] = jnp.zeros_like(dbt_ref)
            dvg_ref[...] = jnp.zeros_like(dvg_ref)
            dog_ref[...] = jnp.zeros_like(dog_ref)

        row = lax.broadcasted_iota(jnp.int32, (LANE, LANE), 0)
        col = lax.broadcasted_iota(jnp.int32, (LANE, LANE), 1)
        for g in range(n_g):
            cols = slice(g * LANE, (g + 1) * LANE)
            vcols = slice(GW + g * LANE, GW + (g + 1) * LANE)
            pu, pv = p_ref[:, cols], p_ref[:, vcols]
            u, v = _gelu(pu), _gelu(pv)
            vhat, rstd = _group_layernorm(v)
            gain = vg_ref[:, cols]
            vln = (vhat * gain).astype(MXU_DTYPE)
            mixed = _dot(wt_ref[g], vln, NN) + bt_ref[:, g:g + 1]
            o = u * mixed
            r = lax.rsqrt(_mean1(o * o) + EPS)
            oh = o * r
            dn = dn_ref[:, cols]
            dog_ref[:, cols] += _sum0(dn * oh)
            dhn = dn * og_ref[:, cols]
            d_o = r * (dhn - oh * _mean1(dhn * oh))
            du = d_o * mixed
            dmix = d_o * u
            dbt_ref[:, g:g + 1] += jnp.sum(dmix, axis=1, keepdims=True)
            dmix_b = dmix.astype(MXU_DTYPE)
            dws_ref[g] += jnp.where(col <= row, _dot(dmix_b, vln, NT), 0.0)
            dvln = _dot(wt_ref[g], dmix_b, TN)
            dvg_ref[:, cols] += _sum0(dvln * vhat)
            dxh = dvln * gain
            dv = rstd * (dxh - _mean1(dxh) - vhat * _mean1(dxh * vhat))
            dp_ref[:, cols] = (du * _gelu_grad(pu)).astype(dp_ref.dtype)
            dp_ref[:, vcols] = (dv * _gelu_grad(pv)).astype(dp_ref.dtype)

    return pl.pallas_call(
        body, name=name, grid=(S // LANE,),
        in_specs=[pl.BlockSpec((LANE, 2 * GW), lambda n: (n, 0)),
                  pl.BlockSpec((LANE, GW), lambda n: (n, 0)),
                  pl.BlockSpec((1, GW), lambda n: (0, 0)),
                  pl.BlockSpec((n_g, LANE, LANE), lambda n: (0, 0, 0)),
                  pl.BlockSpec((LANE, n_g), lambda n: (0, 0)),
                  pl.BlockSpec((1, GW), lambda n: (0, 0))] + [pl.BlockSpec((LANE, SBW), lambda n: (n, 0))] * 3,
        out_specs=[pl.BlockSpec((LANE, N_IN), lambda n: (n, 0)),
                   pl.BlockSpec((n_g, LANE, LANE), lambda n: (0, 0, 0)),
                   pl.BlockSpec((LANE, n_g), lambda n: (0, 0)),
                   pl.BlockSpec((1, GW), lambda n: (0, 0)),
                   pl.BlockSpec((1, GW), lambda n: (0, 0))],
        out_shape=[jax.ShapeDtypeStruct((S, N_IN), MXU_DTYPE),
                   jax.ShapeDtypeStruct((n_g, LANE, LANE), F32),
                   jax.ShapeDtypeStruct((LANE, n_g), F32),
                   jax.ShapeDtypeStruct((1, GW), F32),
                   jax.ShapeDtypeStruct((1, GW), F32)],
        scratch_shapes=[pltpu.VMEM((n_g, LANE, LANE), MXU_DTYPE)],
        compiler_params=_params("arbitrary"),
    )(proj, d_on, v_gain, w_s, b_t, out_gain, *dqkv)


def _tri_sum(v, tri, exact=True):
    hi = v.astype(MXU_DTYPE)
    if not exact:
        return _dot(hi, tri, NN)
    lo = (v - hi.astype(F32)).astype(MXU_DTYPE)
    return _dot(hi, tri, NN) + _dot(lo, tri, NN)


def _log_sigmoids(z):
    sp = jnp.log(1.0 + jnp.exp(-jnp.abs(z)))
    return jnp.minimum(z, 0.0) - sp, jnp.minimum(-z, 0.0) - sp


def _rows(i, size):
    return pl.ds(pl.multiple_of(i * size, size), size)


SB_QUERY_TILE = 2048
SB_KEY_TILE = 256


def _sb_tiles(S):
    tq = _tile(S, SB_QUERY_TILE)
    tk = _tile(tq, SB_KEY_TILE)
    assert (tq // tk) % 2 == 0, "the key sweep takes two blocks a pass"
    return tq, tk, S // tq, tq // tk


def _triangle(n, keep):
    row = lax.broadcasted_iota(jnp.int32, (n, n), 0)
    col = lax.broadcasted_iota(jnp.int32, (n, n), 1)
    return jnp.where(keep(row, col), 1.0, 0.0).astype(MXU_DTYPE)


def _strictly_before(tq, tk, key_offset):
    row = lax.broadcasted_iota(jnp.int32, (tq, tk), 0)
    col = lax.broadcasted_iota(jnp.int32, (tq, tk), 1)
    return col + key_offset < row


def _sb_specs(S, n_g, n_h):
    base = 2 * n_g
    q_spec = pl.BlockSpec((S, LANE), lambda h: (0, base + h))
    k_spec = pl.BlockSpec((S, LANE), lambda h: (0, base + n_h + h))
    v_spec = pl.BlockSpec((S, LANE), lambda h: (0, base + 2 * n_h + h))
    gain_spec = pl.BlockSpec((1, LANE), lambda h: (0, n_g + h))
    head_spec = pl.BlockSpec((S, LANE), lambda h: (0, h))
    return q_spec, k_spec, v_spec, gain_spec, head_spec


def _sb_fwd(proj, out_gain, on_buffer, n_g, n_h, name):
    S = proj.shape[0]
    TQ, TK, NQ, KPQ = _sb_tiles(S)
    scale = LANE ** -0.5
    q_spec, k_spec, v_spec, gain_spec, head_spec = _sb_specs(S, n_g, n_h)

    def body(q_ref, k_ref, v_ref, og_ref, _, o_ref, on_ref, ls_ref, qb, kb, vb):
        qb[...] = q_ref[...].astype(MXU_DTYPE)
        kb[...] = k_ref[...].astype(MXU_DTYPE)
        vb[...] = v_ref[...].astype(MXU_DTYPE)
        after = _triangle(TK, lambda r, c: r > c)

        def block(qi, j, ctail, acc, key_offset):
            skip = key_offset or 0
            z = _dot(qi[skip:], kb[_rows(j, TK), :], NT) * scale
            lb, l1m = _log_sigmoids(z)
            if key_offset is not None:
                strict = _strictly_before(TQ - skip, TK, 0)
                l1m = jnp.where(strict, l1m, 0.0)
            a = jnp.exp(lb + ctail[skip:] + _tri_sum(l1m, after))
            if key_offset is not None:
                a = jnp.where(strict, a, 0.0)
            acc_new = acc[skip:] + _dot(a.astype(MXU_DTYPE), vb[_rows(j, TK), :], NN)
            ctail_new = ctail[skip:] + jnp.sum(l1m, axis=1, keepdims=True)
            if skip:
                ctail_new = jnp.concatenate([ctail[:skip], ctail_new], axis=0)
                acc_new = jnp.concatenate([acc[:skip], acc_new], axis=0)
            return ctail_new, acc_new

        def q_loop(i, carry):
            qi = qb[_rows(i, TQ), :]
            state = (jnp.zeros((TQ, 1), F32), jnp.zeros((TQ, LANE), F32))
            for d in reversed(range(KPQ)):
                state = block(qi, i * KPQ + d, state[0], state[1], d * TK)
            def pair(jj, st):
                st = block(qi, i * KPQ - 1 - 2 * jj, st[0], st[1], None)
                return block(qi, i * KPQ - 2 - 2 * jj, st[0], st[1], None)

            ctail, acc = lax.fori_loop(0, i * (KPQ // 2), pair, state)
            ls_ref[_rows(i, TQ), :] = jnp.broadcast_to(ctail, (TQ, LANE))
            o_ref[_rows(i, TQ), :] = acc
            r = lax.rsqrt(_mean1(acc * acc) + EPS)
            on_ref[_rows(i, TQ), :] = (acc * r * og_ref[...]).astype(on_ref.dtype)
            return carry

        lax.fori_loop(0, NQ, q_loop, 0)

    return pl.pallas_call(
        body, name=name, grid=(n_h,),
        in_specs=[q_spec, k_spec, v_spec, gain_spec, pl.BlockSpec(memory_space=pl.ANY)],
        out_specs=[head_spec, pl.BlockSpec((S, LANE), lambda h: (0, n_g + h)), head_spec],
        out_shape=[jax.ShapeDtypeStruct((S, n_h * LANE), F32), jax.ShapeDtypeStruct(on_buffer.shape, MXU_DTYPE),
                   jax.ShapeDtypeStruct((S, n_h * LANE), F32)],
        input_output_aliases={4: 1},
        scratch_shapes=[pltpu.VMEM((S, LANE), MXU_DTYPE)] * 3,
        compiler_params=_params("parallel"),
    )(proj, proj, proj, out_gain, on_buffer)


def _sb_bwd(proj, o_sb, l_sum, d_on, out_gain, n_g, n_h, name):
    S = proj.shape[0]
    TQ, TK, NQ, KPQ = _sb_tiles(S)
    scale = LANE ** -0.5
    q_spec, k_spec, v_spec, gain_spec, head_spec = _sb_specs(S, n_g, n_h)
    dn_spec = pl.BlockSpec((S, LANE), lambda h: (0, n_g + h))
    dgain_spec = pl.BlockSpec((1, LANE), lambda h: (0, h))

    def body(q_ref, k_ref, v_ref, o_ref, ls_ref, dn_ref, og_ref, dq_ref, dk_ref, dv_ref, dog_ref,
             qb, kb, vb, dob, dk_acc, dv_acc):
        qb[...] = q_ref[...].astype(MXU_DTYPE)
        kb[...] = k_ref[...].astype(MXU_DTYPE)
        vb[...] = v_ref[...].astype(MXU_DTYPE)
        o, dn = o_ref[...], dn_ref[...]
        r = lax.rsqrt(_mean1(o * o) + EPS)
        oh = o * r
        dog_ref[...] = _sum0(dn * oh)
        dhn = dn * og_ref[...]
        dob[...] = (r * (dhn - oh * _mean1(dhn * oh))).astype(MXU_DTYPE)
        dk_acc[...] = jnp.zeros_like(dk_acc)
        dv_acc[...] = jnp.zeros_like(dv_acc)

        up_to = _triangle(TK, lambda r, c: r <= c)
        before = _triangle(TK, lambda r, c: r < c)

        def block(qi, doi, ltot, j, cl, cdl, dq, key_offset):
            skip = key_offset or 0
            q_in, do_in = qi[skip:], doi[skip:]
            kj, vj = kb[_rows(j, TK), :], vb[_rows(j, TK), :]
            z = _dot(q_in, kj, NT) * scale
            lb, l1m = _log_sigmoids(z)
            if key_offset is not None:
                strict = _strictly_before(TQ - skip, TK, 0)
                l1m = jnp.where(strict, l1m, 0.0)
            a = jnp.exp(lb + (ltot[skip:] - (cl[skip:] + _tri_sum(l1m, up_to))))
            if key_offset is not None:
                a = jnp.where(strict, a, 0.0)
            dl = _dot(do_in, vj, NT) * a
            d_l1m = cdl[skip:] + _tri_sum(dl, before, exact=False)
            beta = jnp.exp(lb)
            dz = dl * (1.0 - beta) - beta * d_l1m
            if key_offset is not None:
                dz = jnp.where(strict, dz, 0.0)
            dzs = (dz * scale).astype(MXU_DTYPE)
            dk_acc[_rows(j, TK), :] += _dot(dzs, q_in, TN)
            dv_acc[_rows(j, TK), :] += _dot(a.astype(MXU_DTYPE), do_in, TN)
            cl_new = cl[skip:] + jnp.sum(l1m, axis=1, keepdims=True)
            cdl_new = cdl[skip:] + jnp.sum(dl, axis=1, keepdims=True)
            dq_new = dq[skip:] + _dot(dzs, kj, NN)
            if skip:
                cl_new = jnp.concatenate([cl[:skip], cl_new], axis=0)
                cdl_new = jnp.concatenate([cdl[:skip], cdl_new], axis=0)
                dq_new = jnp.concatenate([dq[:skip], dq_new], axis=0)
            return cl_new, cdl_new, dq_new

        def q_loop(i, carry):
            qi, doi = qb[_rows(i, TQ), :], dob[_rows(i, TQ), :]
            ltot = ls_ref[_rows(i, TQ), :][:, :1]
            zero_col = jnp.zeros((TQ, 1), F32)
            def pair(jj, st):
                st = block(qi, doi, ltot, 2 * jj, st[0], st[1], st[2], None)
                return block(qi, doi, ltot, 2 * jj + 1, st[0], st[1], st[2], None)

            state = lax.fori_loop(0, i * (KPQ // 2), pair, (zero_col, zero_col, jnp.zeros((TQ, LANE), F32)))
            for d in range(KPQ):
                state = block(qi, doi, ltot, i * KPQ + d, state[0], state[1], state[2], d * TK)
            dq_ref[_rows(i, TQ), :] = state[2].astype(dq_ref.dtype)
            return carry

        lax.fori_loop(0, NQ, q_loop, 0)
        dk_ref[...] = dk_acc[...].astype(dk_ref.dtype)
        dv_ref[...] = dv_acc[...].astype(dv_ref.dtype)

    W = n_h * LANE
    return pl.pallas_call(
        body, name=name, grid=(n_h,),
        in_specs=[q_spec, k_spec, v_spec, head_spec, head_spec, dn_spec, gain_spec],
        out_specs=[head_spec, head_spec, head_spec, dgain_spec],
        out_shape=[jax.ShapeDtypeStruct((S, W), MXU_DTYPE)] * 3 + [jax.ShapeDtypeStruct((1, W), F32)],
        scratch_shapes=[pltpu.VMEM((S, LANE), MXU_DTYPE)] * 4 + [pltpu.VMEM((S, LANE), F32)] * 2,
        compiler_params=_params("parallel"),
    )(proj, proj, proj, o_sb, l_sum, d_on, out_gain)


def _mod_part(c_all, w_ada, b_ada_cols, name):
    B, K = c_all.shape
    N = w_ada.shape[1]
    tn = _tile(N, 512)

    def body(c_ref, w_ref, b_ref, o_ref):
        cv = c_ref[...]
        ca = (cv * jax.nn.sigmoid(cv)).astype(MXU_DTYPE)
        o_ref[...] = _dot(ca, w_ref[...].astype(MXU_DTYPE), NN) + b_ref[...]

    return pl.pallas_call(
        body, name=name, grid=(N // tn,),
        in_specs=[pl.BlockSpec((B, K), lambda j: (0, 0)), pl.BlockSpec((K, tn), lambda j: (0, j)),
                  pl.BlockSpec((1, tn), lambda j: (0, j))],
        out_specs=pl.BlockSpec((B, tn), lambda j: (0, j)),
        out_shape=jax.ShapeDtypeStruct((B, N), F32), compiler_params=_params("parallel"))(c_all, w_ada, b_ada_cols)


def _adamw_math(w, g, m, v):
    m = ADAM_B1 * m + (1.0 - ADAM_B1) * g
    v = ADAM_B2 * v + (1.0 - ADAM_B2) * (g * g)
    m_hat = m / (1.0 - ADAM_B1 ** ADAM_STEP)
    v_hat = v / (1.0 - ADAM_B2 ** ADAM_STEP)
    delta = -ADAM_LR * (m_hat / (jnp.sqrt(v_hat) + ADAM_EPS) + ADAM_WD * w)
    return delta, m, v


def _adamw(w, g, m, v, name):
    R, C = w.shape
    tr = _tile(R, max(8, (1 << 19) // C), 8)
    spec = pl.BlockSpec((tr, C), lambda i: (i, 0))

    def body(w_ref, g_ref, m_ref, v_ref, go_ref, d_ref, mo_ref, vo_ref):
        g = g_ref[...]
        go_ref[...] = g
        d_ref[...], mo_ref[...], vo_ref[...] = _adamw_math(w_ref[...], g, m_ref[...], v_ref[...])

    return pl.pallas_call(body, name=name, grid=(R // tr,), in_specs=[spec] * 4, out_specs=[spec] * 4,
                          out_shape=[jax.ShapeDtypeStruct((R, C), F32)] * 4, compiler_params=_params("parallel"))(w, g, m, v)


def _adamw_ada(c_all, dmod_cols, w, m, v, name):
    K, N = w.shape
    B = c_all.shape[0]
    tk, tn = _tile(K, 512), _tile(N, 1024)
    spec = pl.BlockSpec((tk, tn), lambda i, j: (i, j))

    def body(c_ref, dm_ref, w_ref, m_ref, v_ref, g_ref, d_ref, mo_ref, vo_ref):
        cv = c_ref[...]
        ca = (cv * jax.nn.sigmoid(cv)).astype(MXU_DTYPE)
        g = _dot(ca, dm_ref[...].astype(MXU_DTYPE), TN)
        g_ref[...] = g
        d_ref[...], mo_ref[...], vo_ref[...] = _adamw_math(w_ref[...], g, m_ref[...], v_ref[...])

    return pl.pallas_call(
        body, name=name, grid=(K // tk, N // tn),
        in_specs=[pl.BlockSpec((B, tk), lambda i, j: (0, i)), pl.BlockSpec((B, tn), lambda i, j: (0, j)), spec, spec, spec],
        out_specs=[spec] * 4, out_shape=[jax.ShapeDtypeStruct((K, N), F32)] * 4,
        compiler_params=_params("parallel", "parallel"))(c_all, dmod_cols, w, m, v)


def _sum_devices(gathered, n_dev, name):
    R = gathered.shape[0] // n_dev
    C = gathered.shape[1]
    tr = _tile(R, 512, 8)
    n_blk = R // tr

    def body(*refs):
        acc = refs[0][...]
        for r in refs[1:n_dev]:
            acc = acc + r[...]
        refs[n_dev][...] = acc

    in_specs = [pl.BlockSpec((tr, C), functools.partial(lambda i, d: (d * n_blk + i, 0), d=d)) for d in range(n_dev)]
    return pl.pallas_call(body, name=name, grid=(n_blk,), in_specs=in_specs,
                          out_specs=pl.BlockSpec((tr, C), lambda i: (i, 0)),
                          out_shape=jax.ShapeDtypeStruct((R, C), F32), compiler_params=_params("parallel"))(*([gathered] * n_dev))


def _place():
    x, y, c = lax.axis_index("x"), lax.axis_index("y"), lax.axis_index("c")
    return x, y, c


def _allgather8(blk, name):
    m_per, n = blk.shape

    def body(x_ref, out_ref, send_sems, recv_sems, local_sem):
        x, y, c = _place()
        me, sibling = (x, y, c), (x, y, 1 - c)
        chips = [(1 - x, y), (x, 1 - y), (1 - x, 1 - y)]

        def rows(px, py, pc):
            return out_ref.at[pl.ds((4 * px + 2 * py + pc) * m_per, m_per), :]

        def copy(k, block, to, src=None):
            return pltpu.make_async_remote_copy(
                src_ref=rows(*block) if src is None else src, dst_ref=rows(*block),
                send_sem=send_sems.at[k], recv_sem=recv_sems.at[k], device_id=to, device_id_type=MESH)

        mine = pltpu.make_async_copy(x_ref, rows(*me), local_sem)
        mine.start()
        first = [copy(0, me, sibling, src=x_ref)]
        first += [copy(1 + j, me, (*chip, c), src=x_ref) for j, chip in enumerate(chips)]
        for cp in first:
            cp.start()
        passed = [copy(4 + j, (*chip, c), sibling) for j, chip in enumerate(chips)]
        for j, chip in enumerate(chips):
            copy(1 + j, (*chip, c), me).wait_recv()
            passed[j].start()
        copy(0, sibling, me).wait_recv()
        for j, chip in enumerate(chips):
            copy(4 + j, (*chip, 1 - c), me).wait_recv()
        for cp in first + passed:
            cp.wait_send()
        mine.wait()

    return pl.pallas_call(
        body, name=name,
        out_shape=jax.ShapeDtypeStruct((8 * m_per, n), blk.dtype),
        in_specs=[pl.BlockSpec(memory_space=pltpu.VMEM)],
        out_specs=pl.BlockSpec(memory_space=pltpu.VMEM),
        scratch_shapes=[pltpu.SemaphoreType.DMA((7,)), pltpu.SemaphoreType.DMA((7,)), pltpu.SemaphoreType.DMA],
        compiler_params=pltpu.CompilerParams(vmem_limit_bytes=V7X_VMEM_LIMIT),
    )(blk)


class _Sharded:
    def __init__(self, shard_shape, by_cols):
        r, c = shard_shape
        self.by_cols = by_cols
        self.full = (r, N_CHIPS * c) if by_cols else (N_CHIPS * r, c)
        self.shard = (r, c)
        self.half_rows = r // 2
        self.half = (r // 2, c)

    def half_of(self, ref, k, hc):
        r, c = self.shard
        h = self.half_rows
        if self.by_cols:
            return ref.at[pl.ds(hc * h, h), pl.ds(k * c, c)]
        return ref.at[pl.ds(k * r + hc * h, h), :]

    def chunk_of(self, ref, k, hc, ch, n):
        r, c = self.shard
        h = self.half_rows
        q = h // n
        if self.by_cols:
            return ref.at[pl.ds(hc * h + ch * q, q), pl.ds(k * c, c)]
        return ref.at[pl.ds(k * r + hc * h + ch * q, q), :]

    def half_of_shard(self, ref, hc):
        return ref.at[pl.ds(hc * self.half_rows, self.half_rows), :]

    def part_of_halves(self, ref, k):
        r, c = self.shard
        h = self.half_rows
        return ref.at[:, pl.ds(k * c, c)] if self.by_cols else ref.at[pl.ds(k * h, h), :]


def _on_each_place(x, y, c, fn, by_chip=True, by_core=True):
    q = 2 * x + y
    for k in range(N_CHIPS if by_chip else 1):
        for cc in range(2 if by_core else 1):
            cond = None
            if by_chip:
                cond = q == k
            if by_core:
                cond = (c == cc) if cond is None else jnp.logical_and(cond, c == cc)
            pl.when(cond)(functools.partial(fn, k, cc))


def _chip_id(k, c):
    return (k // 2, k % 2, c)


def _handshake(peers):
    barrier = pltpu.get_barrier_semaphore()
    for peer in peers:
        pl.semaphore_signal(barrier, inc=1, device_id=peer, device_id_type=MESH)
    pl.semaphore_wait(barrier, len(peers))


def _on_sequencer(body, inputs, out_structs, n_copies, peers_of, name, collective_id, return_inputs=False):
    in_refs = [jax.new_ref(a, memory_space=pltpu.MemorySpace.HBM) for a in inputs]
    out_refs = [jax.empty_ref(s, memory_space=pltpu.MemorySpace.HBM) for s in out_structs]

    @pl.kernel(mesh=plsc.ScalarSubcoreMesh(axis_name="sequencer", num_cores=1), name=name,
               scratch_types=(pltpu.SemaphoreType.DMA((n_copies,)), pltpu.SemaphoreType.DMA((n_copies,))),
               compiler_params=pltpu.CompilerParams(collective_id=collective_id))
    def launch(send_sems, recv_sems):
        x, y, c = _place()
        _handshake(peers_of(x, y, c))
        body(in_refs, out_refs, send_sems, recv_sems, x, y, c)

    launch()
    return [r[...] for r in (in_refs if return_inputs else out_refs)]


def _sibling(x, y, c):
    return [(x, y, 1 - c)]


def _same_core_of_other_chips(x, y, c):
    return [(1 - x, y, c), (x, 1 - y, c), (1 - x, 1 - y, c)]


GATHER_CHUNKS = 4
GATHER_COPIES = 6 * GATHER_CHUNKS


def _place_slab(place, parts, loss_part, rows, name):
    n_parts = len(parts)

    def body(p_ref, *refs):
        loss_ref, out_ref = refs[n_parts], refs[n_parts + 1]
        at = 0
        for ref in refs[:n_parts]:
            if len(ref.shape) == 2 and ref.shape[0] == 1:
                for r in range(ref.shape[1] // LANE):
                    out_ref[at + r:at + r + 1, :] = ref[:, r * LANE:(r + 1) * LANE]
                at += ref.shape[1] // LANE
            else:
                n_rows = math.prod(ref.shape) // LANE
                out_ref[at:at + n_rows, :] = ref[...].reshape(n_rows, LANE)
                at += n_rows
        out_ref[at:at + 8, :] = jnp.broadcast_to(loss_ref[...], (8, LANE))
        out_ref[at + 8:, :] = jnp.zeros((rows - at - 8, LANE), F32)

    def whole(shape):
        return pl.BlockSpec(shape, functools.partial(lambda i, p, nd: (0,) * nd, nd=len(shape)))

    return pl.pallas_call(
        body, name=name,
        grid_spec=pltpu.PrefetchScalarGridSpec(
            num_scalar_prefetch=1, grid=(1,),
            in_specs=[whole(a.shape) for a in parts] + [whole(loss_part.shape)],
            out_specs=pl.BlockSpec((rows, LANE), lambda i, p: (2 * p[0] + p[1], 0))),
        out_shape=jax.ShapeDtypeStruct((8 * rows, LANE), F32),
        compiler_params=_params(),
    )(place, *parts, loss_part)


def _allgather8_on_sequencer(placed, m_per, name, collective_id):
    def body(refs, _, send_sems, recv_sems, x, y, c):
        out_ref, = refs

        def at_place(k, cc):
            def rows(kk, pc):
                return out_ref.at[pl.ds((2 * kk + pc) * m_per, m_per), :]

            def copy(slot, block, to):
                return pltpu.make_async_remote_copy(src_ref=rows(*block), dst_ref=rows(*block), send_sem=send_sems.at[slot],
                                                    recv_sem=recv_sems.at[slot], device_id=to, device_id_type=MESH)

            others = [k ^ flip for flip in FLIPS]
            sends = [copy(0, (k, cc), _chip_id(k, 1 - cc))] + [copy(1 + j, (k, cc), _chip_id(kk, cc)) for j, kk in enumerate(others)]
            for cp in sends:
                cp.start()
            for j, kk in enumerate(others):
                copy(1 + j, (kk, cc), _chip_id(k, cc)).wait_recv()
                cp = copy(4 + j, (kk, cc), _chip_id(k, 1 - cc))
                cp.start()
                sends.append(cp)
            copy(0, (k, 1 - cc), _chip_id(k, cc)).wait_recv()
            for j, kk in enumerate(others):
                copy(4 + j, (kk, 1 - cc), _chip_id(k, cc)).wait_recv()
            for cp in sends:
                cp.wait_send()

        _on_each_place(x, y, c, at_place)

    def peers(x, y, c):
        return _sibling(x, y, c) + _same_core_of_other_chips(x, y, c)

    return _on_sequencer(body, [placed], [], 7, peers, name, collective_id, return_inputs=True)[0]


def _gather_weights(fulls, geoms, name, collective_id):
    n_w = len(fulls)
    n_ch, n_relay = GATHER_CHUNKS, GATHER_CHUNKS // 2
    f_refs = [jax.new_ref(f, memory_space=pltpu.MemorySpace.HBM) for f in fulls]
    FLIP_X, FLIP_Y, FLIP_BOTH = FLIPS
    TO_X, TO_Y, RELAY_TO_Y, RELAY_TO_X, ON_X, ON_Y, ON_DIAG = 0, n_ch, 2 * n_ch, 2 * n_ch + n_relay, 3 * n_ch, 4 * n_ch, 5 * n_ch

    @pl.kernel(mesh=plsc.ScalarSubcoreMesh(axis_name="sequencer", num_cores=1), name=name,
               scratch_types=(pltpu.SemaphoreType.DMA((GATHER_COPIES * n_w,)), pltpu.SemaphoreType.DMA((GATHER_COPIES * n_w,))),
               compiler_params=pltpu.CompilerParams(collective_id=collective_id))
    def launch(send_sems, recv_sems):
        x, y, c = _place()
        _handshake([(x, y, 1 - c), (1 - x, y, c), (x, 1 - y, c)])

        def at_place(k, cc):
            kx, ky, kd = k ^ FLIP_X, k ^ FLIP_Y, k ^ FLIP_BOTH
            me, sibling = _chip_id(k, cc), _chip_id(k, 1 - cc)
            started = []

            def copy(i, slot, src, dst, to, start=True):
                cp = pltpu.make_async_remote_copy(src_ref=src, dst_ref=dst, send_sem=send_sems.at[GATHER_COPIES * i + slot],
                                                  recv_sem=recv_sems.at[GATHER_COPIES * i + slot], device_id=to, device_id_type=MESH)
                if start:
                    cp.start()
                    started.append(cp)
                return cp

            def pass_on(i, slot, ref, to):
                copy(i, slot, ref, ref, to)

            def landed(i, slot, ref):
                copy(i, slot, ref, ref, me, start=False).wait_recv()

            y_order = [(n_relay + s) % n_ch for s in range(n_ch)]
            for i, (g, f_ref) in enumerate(zip(geoms, f_refs)):
                for s in range(n_ch):
                    pass_on(i, TO_X + s, g.chunk_of(f_ref, k, cc, s, n_ch), _chip_id(kx, cc))
                    pass_on(i, TO_Y + y_order[s], g.chunk_of(f_ref, k, cc, y_order[s], n_ch), _chip_id(ky, cc))
            for i, (g, f_ref) in enumerate(zip(geoms, f_refs)):
                for s in range(n_ch):
                    from_x = g.chunk_of(f_ref, kx, cc, s, n_ch)
                    landed(i, TO_X + s, from_x)
                    if s < n_relay:
                        pass_on(i, RELAY_TO_Y + s, from_x, _chip_id(ky, cc))
                    pass_on(i, ON_X + s, from_x, sibling)
                    ch = y_order[s]
                    from_y = g.chunk_of(f_ref, ky, cc, ch, n_ch)
                    landed(i, TO_Y + ch, from_y)
                    if ch >= n_relay:
                        pass_on(i, RELAY_TO_X + ch - n_relay, from_y, _chip_id(kx, cc))
                    pass_on(i, ON_Y + ch, from_y, sibling)
                for r in range(n_relay):
                    via_y = g.chunk_of(f_ref, kd, cc, r, n_ch)
                    landed(i, RELAY_TO_Y + r, via_y)
                    pass_on(i, ON_DIAG + r, via_y, sibling)
                    via_x = g.chunk_of(f_ref, kd, cc, n_relay + r, n_ch)
                    landed(i, RELAY_TO_X + r, via_x)
                    pass_on(i, ON_DIAG + n_relay + r, via_x, sibling)
            for i, (g, f_ref) in enumerate(zip(geoms, f_refs)):
                for slot, kk in ((ON_X, kx), (ON_Y, ky), (ON_DIAG, kd)):
                    for ch in range(n_ch):
                        landed(i, slot + ch, g.chunk_of(f_ref, kk, 1 - cc, ch, n_ch))
            for cp in started:
                cp.wait_send()

        _on_each_place(x, y, c, at_place)

    launch()
    return [f_ref[...] for f_ref in f_refs]


def _swap_core_halves(grads, geoms, name, collective_id):
    n_cp = sum(1 if g.by_cols else N_CHIPS for g in geoms)

    def body(g_refs, t_refs, send_sems, recv_sems, x, y, c):

        def at_place(_, cc):
            def pairs(hc):
                out = []
                for g, g_ref, t_ref in zip(geoms, g_refs, t_refs):
                    if g.by_cols:
                        out.append((g_ref.at[pl.ds(hc * g.half_rows, g.half_rows), :], t_ref))
                    else:
                        out += [(g.half_of(g_ref, k, hc), g.part_of_halves(t_ref, k)) for k in range(N_CHIPS)]
                return out

            sends = [pltpu.make_async_remote_copy(src_ref=src, dst_ref=dst, send_sem=send_sems.at[n],
                                                  recv_sem=recv_sems.at[n], device_id=(x, y, 1 - cc), device_id_type=MESH)
                     for n, (src, dst) in enumerate(pairs(1 - cc))]
            for cp in sends:
                cp.start()
            for n, (src, dst) in enumerate(pairs(cc)):
                pltpu.make_async_remote_copy(src_ref=src, dst_ref=dst, send_sem=send_sems.at[n], recv_sem=recv_sems.at[n],
                                             device_id=(x, y, cc), device_id_type=MESH).wait_recv()
            for cp in sends:
                cp.wait_send()

        _on_each_place(x, y, c, at_place, by_chip=False)

    return _on_sequencer(body, grads, [jax.ShapeDtypeStruct((g.full[0] // 2, g.full[1]), F32) for g in geoms],
                         n_cp, _sibling, name, collective_id)


def _send_to_sibling(buffers, name, collective_id):
    def body(src_refs, dst_refs, send_sems, recv_sems, x, y, c):
        def copy(i):
            return pltpu.make_async_remote_copy(src_ref=src_refs[i], dst_ref=dst_refs[i], send_sem=send_sems.at[i],
                                                recv_sem=recv_sems.at[i], device_id=(x, y, 1 - c), device_id_type=MESH)

        for i in range(len(buffers)):
            copy(i).start()
        for i in range(len(buffers)):
            copy(i).wait()

    return _on_sequencer(body, buffers, [jax.ShapeDtypeStruct(t.shape, t.dtype) for t in buffers], len(buffers),
                         _sibling, name, collective_id)


def _scatter_chip_sums(sums, geoms, name, collective_id):
    def body(s_refs, r_refs, send_sems, recv_sems, x, y, c):

        def at_place(k, _):
            sends = []
            for i, (g, s_ref, r_ref) in enumerate(zip(geoms, s_refs, r_refs)):
                for j, flip in enumerate(FLIPS):
                    kk = k ^ flip
                    cp = pltpu.make_async_remote_copy(
                        src_ref=g.part_of_halves(s_ref, kk), dst_ref=r_ref.at[j], send_sem=send_sems.at[3 * i + j],
                        recv_sem=recv_sems.at[3 * i + j], device_id=(kk // 2, kk % 2, c), device_id_type=MESH)
                    cp.start()
                    sends.append(cp)
            for i, (g, s_ref, r_ref) in enumerate(zip(geoms, s_refs, r_refs)):
                for j in range(len(FLIPS)):
                    pltpu.make_async_remote_copy(
                        src_ref=g.part_of_halves(s_ref, k), dst_ref=r_ref.at[j], send_sem=send_sems.at[3 * i + j],
                        recv_sem=recv_sems.at[3 * i + j], device_id=(x, y, c), device_id_type=MESH).wait_recv()
            for cp in sends:
                cp.wait_send()

        _on_each_place(x, y, c, at_place, by_core=False)

    return _on_sequencer(body, sums, [jax.ShapeDtypeStruct((len(FLIPS),) + g.half, WIRE_DTYPE) for g in geoms],
                         len(FLIPS) * len(sums), _same_core_of_other_chips, name, collective_id)


def _share_reduced_halves(reduced, geoms, name, collective_id):
    def body(out_refs, _, send_sems, recv_sems, x, y, c):

        def at_place(_, cc):
            sends = []
            for i, (g, ref) in enumerate(zip(geoms, out_refs)):
                mine = g.half_of_shard(ref, cc)
                cp = pltpu.make_async_remote_copy(src_ref=mine, dst_ref=mine, send_sem=send_sems.at[i],
                                                  recv_sem=recv_sems.at[i], device_id=(x, y, 1 - cc), device_id_type=MESH)
                cp.start()
                sends.append(cp)
            for i, (g, ref) in enumerate(zip(geoms, out_refs)):
                theirs = g.half_of_shard(ref, 1 - cc)
                pltpu.make_async_remote_copy(src_ref=theirs, dst_ref=theirs, send_sem=send_sems.at[i],
                                             recv_sem=recv_sems.at[i], device_id=(x, y, cc), device_id_type=MESH).wait_recv()
            for cp in sends:
                cp.wait_send()

        _on_each_place(x, y, c, at_place, by_chip=False)

    return _on_sequencer(body, reduced, [], len(reduced), _sibling, name, collective_id, return_inputs=True)


def _chip_sum(place, grad, theirs, g, name):
    RH, C = theirs.shape
    h = g.half_rows
    tr = _tile(h, 256, 16)
    tc = _tile(C, 2048)
    per_half = h // tr

    if g.by_cols:
        grad_map = lambda i, j, p: (p[1] * per_half + i, j)
    else:
        grad_map = lambda i, j, p: ((i // per_half) * 2 * per_half + p[1] * per_half + i % per_half, j)

    def body(p_ref, a_ref, b_ref, f_ref, o_ref):
        total = a_ref[...] + b_ref[...]
        f_ref[...] = total
        o_ref[...] = total.astype(o_ref.dtype)

    return pl.pallas_call(
        body, name=name,
        grid_spec=pltpu.PrefetchScalarGridSpec(
            num_scalar_prefetch=1, grid=(RH // tr, C // tc),
            in_specs=[pl.BlockSpec((tr, tc), grad_map), pl.BlockSpec((tr, tc), lambda i, j, p: (i, j))],
            out_specs=[pl.BlockSpec((tr, tc), lambda i, j, p: (i, j))] * 2),
        out_shape=[jax.ShapeDtypeStruct((RH, C), F32), jax.ShapeDtypeStruct((RH, C), WIRE_DTYPE)],
        compiler_params=_params("parallel", "parallel"),
    )(place, grad, theirs)


def _dw_half(place, a, b, g, mine, name, add=None):
    K, R = a.shape
    C = b.shape[1]
    h = g.half_rows
    tm, tn = _tile(h, 1024, 16), _tile(C, 512)
    per_half = h // tm
    n_i = (R // 2) // tm

    def a_map(i, j, p):
        hc = p[1] if mine else 1 - p[1]
        if g.by_cols:
            return 0, hc * n_i + i
        return 0, (i // per_half) * 2 * per_half + hc * per_half + i % per_half

    mn_spec = pl.BlockSpec((tm, tn), lambda i, j, p: (i, j))

    def body(p_ref, a_ref, b_ref, *rest):
        acc = _dot(a_ref[...], b_ref[...], TN)
        if add is None:
            rest[0][...] = acc
        else:
            total = acc + rest[0][...]
            rest[1][...] = total
            rest[2][...] = total.astype(rest[2].dtype)

    out_shape = [jax.ShapeDtypeStruct((R // 2, C), F32)] + ([] if add is None else [jax.ShapeDtypeStruct((R // 2, C), WIRE_DTYPE)])
    return pl.pallas_call(
        body, name=name,
        grid_spec=pltpu.PrefetchScalarGridSpec(
            num_scalar_prefetch=1, grid=(n_i, C // tn),
            in_specs=[pl.BlockSpec((K, tm), a_map), pl.BlockSpec((K, tn), lambda i, j, p: (0, j))] + ([] if add is None else [mn_spec]),
            out_specs=[mn_spec] * len(out_shape)),
        out_shape=out_shape,
        compiler_params=_params("parallel", "arbitrary"),
    )(place, a, b, *([] if add is None else [add]))


def _reduce_half(place, sums, others, g, name):
    h, tc = g.half
    tr = _tile(h, 256, 16)
    per_half = h // tr
    sums_map = (lambda i, p: (i, p[0])) if g.by_cols else (lambda i, p: (p[0] * per_half + i, 0))

    def body(p_ref, s_ref, o0_ref, o1_ref, o2_ref, out_ref):
        acc = s_ref[...]
        for o_ref in (o0_ref, o1_ref, o2_ref):
            acc = acc + o_ref[...].astype(F32)
        out_ref[...] = acc

    other_specs = [pl.BlockSpec((None, tr, tc), functools.partial(lambda i, p, j: (j, i, 0), j=j)) for j in range(len(FLIPS))]
    return pl.pallas_call(
        body, name=name,
        grid_spec=pltpu.PrefetchScalarGridSpec(
            num_scalar_prefetch=1, grid=(per_half,),
            in_specs=[pl.BlockSpec((tr, tc), sums_map)] + other_specs,
            out_specs=pl.BlockSpec((tr, tc), lambda i, p: (p[1] * per_half + i, 0))),
        out_shape=jax.ShapeDtypeStruct(g.shard, F32),
        compiler_params=_params("arbitrary"),
    )(place, sums, others, others, others)


SLAB_ROW_UNIT = 256
SMALL = ("b_ada", "norm1_g", "v_norm_g", "w_spatial", "b_spatial", "out_norm_g", "norm2_g", "final_g")
BIG = ("w_in", "w_out", "w_gate", "w_up", "w_down")
BY_COLS = {"w_in": True, "w_out": False, "w_gate": True, "w_up": True, "w_down": False}
ORDER = ("w_ada", "b_ada", "norm1_g", "w_in", "v_norm_g", "w_spatial", "b_spatial", "out_norm_g", "w_out",
         "norm2_g", "w_gate", "w_up", "w_down", "final_g")


def _pack(parts):
    return jnp.concatenate([parts[n].reshape(-1) for n in SMALL]).reshape(-1, LANE)


def _adamw_small(w, g, m, v, shapes, name):
    R = w.shape[0]
    slab_spec = pl.BlockSpec((R, LANE), lambda: (0, 0))
    out_shapes = [shapes[n] if len(shapes[n]) > 1 else (1,) + tuple(shapes[n]) for n in SMALL]

    def body(w_ref, g_ref, m_ref, v_ref, *out_refs):
        gv = g_ref[...]
        results = (gv,) + _adamw_math(w_ref[...], gv, m_ref[...], v_ref[...])
        for kind, val in enumerate(results):
            at = 0
            for i, shp in enumerate(out_shapes):
                o_ref = out_refs[kind * len(SMALL) + i]
                n_rows = math.prod(shp) // LANE
                if len(shp) == 2:
                    for r in range(n_rows):
                        o_ref[:, r * LANE:(r + 1) * LANE] = val[at + r:at + r + 1, :]
                else:
                    o_ref[0] = val[at:at + n_rows, :].reshape(shp[1:])
                at += n_rows

    outs = pl.pallas_call(
        body, name=name, in_specs=[slab_spec] * 4,
        out_specs=[pl.BlockSpec(shp, functools.partial(lambda nd: (0,) * nd, len(shp))) for shp in out_shapes] * 4,
        out_shape=[jax.ShapeDtypeStruct(shp, F32) for shp in out_shapes] * 4,
        compiler_params=_params(),
    )(w, g, m, v)
    dicts = []
    for kind in range(4):
        part = outs[kind * len(SMALL):(kind + 1) * len(SMALL)]
        dicts.append({n: a.reshape(shapes[n]) for n, a in zip(SMALL, part)})
    return dicts


def kernel(x, c, w_ada, b_ada, norm1_g, w_in, v_norm_g, w_spatial, b_spatial, out_norm_g, w_out, norm2_g, w_gate, w_up, w_down, final_g, loss_target, m_w_ada, m_b_ada, m_norm1_g, m_w_in, m_v_norm_g, m_w_spatial, m_b_spatial, m_out_norm_g, m_w_out, m_norm2_g, m_w_gate, m_w_up, m_w_down, m_final_g, v_w_ada, v_b_ada, v_norm1_g, v_w_in, v_v_norm_g, v_w_spatial, v_b_spatial, v_out_norm_g, v_w_out, v_norm2_g, v_w_gate, v_w_up, v_w_down, v_final_g):
    weights = dict(w_ada=w_ada, b_ada=b_ada, norm1_g=norm1_g, w_in=w_in, v_norm_g=v_norm_g, w_spatial=w_spatial,
                   b_spatial=b_spatial, out_norm_g=out_norm_g, w_out=w_out, norm2_g=norm2_g, w_gate=w_gate, w_up=w_up,
                   w_down=w_down, final_g=final_g)
    m_in = dict(w_ada=m_w_ada, b_ada=m_b_ada, norm1_g=m_norm1_g, w_in=m_w_in, v_norm_g=m_v_norm_g, w_spatial=m_w_spatial,
                b_spatial=m_b_spatial, out_norm_g=m_out_norm_g, w_out=m_w_out, norm2_g=m_norm2_g, w_gate=m_w_gate,
                w_up=m_w_up, w_down=m_w_down, final_g=m_final_g)
    v_in = dict(w_ada=v_w_ada, b_ada=v_b_ada, norm1_g=v_norm1_g, w_in=v_w_in, v_norm_g=v_v_norm_g, w_spatial=v_w_spatial,
                b_spatial=v_b_spatial, out_norm_g=v_out_norm_g, w_out=v_w_out, norm2_g=v_norm2_g, w_gate=v_w_gate,
                w_up=v_w_up, w_down=v_w_down, final_g=v_final_g)

    S, D = x.shape[1], x.shape[2]
    n_g = v_norm_g.shape[-1] // LANE
    n_h = (D - n_g * LANE) // LANE
    GW = n_g * LANE
    xi, yi, ci = _place()
    chip = 2 * xi + yi
    me = 4 * xi + 2 * yi + ci
    place = jnp.stack([chip, ci]).astype(jnp.int32)

    xs, target = x[0], loss_target[0]
    geoms = [_Sharded(weights[n].shape[1:], BY_COLS[n]) for n in BIG]

    full = {}
    for i, group in enumerate((("w_in",), ("w_out",), ("w_gate", "w_up"), ("w_down",))):
        gg = [geoms[BIG.index(n)] for n in group]
        own = [_cast_into_full(place, weights[n][0], g, "cast_" + n) for n, g in zip(group, gg)]
        gathered = _gather_weights(own, gg, "gather_" + "_".join(group), 1 + i)
        full.update(zip(group, gathered))

    c_pad = jnp.concatenate([c, jnp.zeros((7, D), F32)], axis=0)
    c_all = _allgather8(c_pad, "gather_c")[::8]
    n_ada = w_ada.shape[2]
    b_cols = lax.dynamic_slice(b_ada, (0, chip * n_ada), (1, n_ada))
    mod_parts = _allgather8(_mod_part(c_all, w_ada[0], b_cols, "mod_part"), "gather_mod")
    mod_all = mod_parts.reshape(N_CHIPS, 2, 8, n_ada)[:, 0].transpose(1, 0, 2).reshape(8, N_CHIPS * n_ada)
    mod = lax.dynamic_slice(mod_all, (me, 0), (1, 6 * D))
    shift1, scale1, gate1, shift2, scale2, gate2 = [mod[:, i * D:(i + 1) * D] for i in range(6)]

    b_t = b_spatial[0].T
    h1 = _norm_mod(xs, norm1_g, scale1, shift1, "norm1")
    proj, = _mm("nn", h1, full["w_in"], [F32], "proj")
    on_gm = _gmlp_fwd(proj, v_norm_g, w_spatial[0], b_t, out_norm_g, n_g, "gmlp_fwd")
    o_sb, o_n, l_sum = _sb_fwd(proj, out_norm_g, on_gm, n_g, n_h, "sb_fwd")
    attn, = _mm("nn", o_n, full["w_out"], [F32], "attn_out")
    x1, h2 = _residual_norm_mod(xs, attn, gate1, norm2_g, scale2, shift2, "norm2")
    a_g, a_u, f_in = _gate_up(h2, full["w_gate"], full["w_up"], "gate_up")
    f, = _mm("nn", f_in, full["w_down"], [F32], "down", tm=1024)
    dx2, df, d_gate2, d_final_g, loss_part = _final_loss_bwd(x1, f, gate2, final_g.reshape(1, D), target, "final")

    geom_of = dict(zip(BIG, geoms))
    grad_out, delta, new_m, new_v = {}, {}, {}, {}

    def theirs_first(group, operands, collective_id, after=None):
        outs = []
        for n, (a_op, b_op) in zip(group, operands):
            outs.append(_dw_half(place, a_op, b_op if after is None else _then(after, b_op), geom_of[n], False, "d_" + n + "_theirs")[0])
            after = outs[-1]
        return outs, _send_to_sibling(outs, "swap_" + "_".join(group), collective_id)

    def chip_sums(group, operands, theirs, after):
        f32s, wires = [], []
        for n, (a_op, b_op), t in zip(group, operands, theirs):
            sf, sw = _dw_half(place, a_op, b_op, geom_of[n], True, "d_" + n + "_mine", add=_then(after, t))
            f32s.append(sf)
            wires.append(sw)
            after = sw
        return f32s, wires

    def scatter(group, sums, collective_id):
        return _scatter_chip_sums(sums, [geom_of[n] for n in group], "scatter_" + "_".join(group), collective_id)

    def reduce_halves(group, sums, others, after):
        return [_reduce_half(place, sf, _then(after, o), geom_of[n], "reduce_" + n) for n, sf, o in zip(group, sums, others)]

    def share(group, halves, collective_id):
        return _share_reduced_halves(halves, [geom_of[n] for n in group], "share_" + "_".join(group), collective_id)

    def adamw(group, reduced, after):
        for n, r in zip(group, reduced):
            go, d, mo, vo = _adamw(weights[n][0], _then(after, r), m_in[n][0], v_in[n][0], "adamw_" + n)
            grad_out[n], delta[n], new_m[n], new_v[n] = go[None], d[None], mo[None], vo[None]
        return d

    g_down = ("w_down",)
    g_ffn = ("w_gate", "w_up")
    g_out = ("w_out",)
    g_in = ("w_in",)

    gr_down, = _mm("tn", f_in, df, [F32], "d_w_down", tm=1408, tn=1024)
    th_down, = _swap_core_halves([gr_down], [geom_of["w_down"]], "swap_w_down", 6)
    d_ag, d_au = _mm("nt", df, full["w_down"], [MXU_DTYPE, MXU_DTYPE], "d_ffn_in", extras=(a_g, a_u),
                     epilogue=_swiglu_bwd_epilogue)
    sf_down, sw_down = [[t] for t in _chip_sum(place, gr_down, _then(d_ag, th_down), geom_of["w_down"], "chip_sum_w_down")]
    ot_down = scatter(g_down, sw_down, 7)
    sent, th_ffn = theirs_first(g_ffn, [(h2, d_ag), (h2, d_au)], 9, after=sw_down)
    dh2 = _mm_ktiled("nt", [(_then(sent, d_ag), full["w_gate"]), (d_au, full["w_up"])], "d_h2", tn=512)
    sf_ffn, sw_ffn = chip_sums(g_ffn, [(h2, d_ag), (h2, d_au)], th_ffn, after=dh2)
    ot_ffn = scatter(g_ffn, sw_ffn, 10)
    hv_down = reduce_halves(g_down, sf_down, ot_down, after=sw_ffn)
    rd_down = share(g_down, hv_down, 8)
    dx1, d_shift2, d_scale2, d_norm2_g, d_gate1, d_attn = _norm_mod_bwd(
        _then(hv_down, dh2), x1, dx2, norm2_g, scale2, "norm2_bwd", branch=attn, gate=gate1)
    gr_out, = _mm("tn", o_n, d_attn, [F32], "d_w_out")
    th_out, = _swap_core_halves([gr_out], [geom_of["w_out"]], "swap_w_out", 12)
    d_on, = _mm("nt", _then(gr_out, d_attn), full["w_out"], [F32], "d_o")
    sf_out, sw_out = [[t] for t in _chip_sum(place, gr_out, _then(d_on, th_out), geom_of["w_out"], "chip_sum_w_out")]
    ot_out = scatter(g_out, sw_out, 13)
    dq, dk, dv, d_og_sb = _sb_bwd(proj, o_sb, l_sum, _then(sw_out, d_on), out_norm_g, n_g, n_h, "sb_bwd")
    dproj, d_w_spatial, d_b_t, d_v_norm_g, d_og_gm = _gmlp_bwd(proj, d_on, v_norm_g, w_spatial[0], b_t, out_norm_g,
                                                                (dq, dk, dv), n_g, "gmlp_bwd")
    sent, th_in = theirs_first(g_in, [(h1, dproj)], 15)
    hv_ffn = reduce_halves(g_ffn, sf_ffn, ot_ffn, after=sent)
    rd_ffn = share(g_ffn, hv_ffn, 11)
    dh1, = _mm("nt", _then(sent, dproj), full["w_in"], [F32], "d_h1", tm=1024)
    hv_out = reduce_halves(g_out, sf_out, ot_out, after=dh1)
    rd_out = share(g_out, hv_out, 14)
    grad_x, d_shift1, d_scale1, d_norm1_g = _norm_mod_bwd(_then(hv_out, dh1), xs, dx1, norm1_g, scale1, "norm1_bwd")

    small_parts = [d_shift1, d_scale1, d_gate1, d_shift2, d_scale2, d_gate2,
                   d_norm1_g, d_v_norm_g, d_w_spatial, d_b_t.T, d_og_gm, d_og_sb, d_norm2_g, d_final_g]
    small_rows = sum(math.prod(p.shape) for p in small_parts) // LANE
    rows = -(-(small_rows + 8) // SLAB_ROW_UNIT) * SLAB_ROW_UNIT
    slab = _place_slab(place, small_parts, loss_part, rows, "place_slab")
    gathered = _allgather8_on_sequencer(slab, rows, "gather_small", 18)
    sf_in, sw_in = chip_sums(g_in, [(h1, dproj)], th_in, after=slab)
    ot_in = scatter(g_in, sw_in, 16)
    done = adamw(g_down, rd_down, after=sw_in)
    done = adamw(g_ffn, rd_ffn, after=done)
    done = adamw(g_out, rd_out, after=done)
    gathered = _then(done, gathered)
    small_shapes = {n: weights[n].shape for n in SMALL}
    slab_sum = _sum_devices(gathered, 8, "sum_small")
    small_sum, loss = slab_sum[:small_rows], slab_sum[small_rows, 0]
    ada_rows = n_ada // LANE
    dmod_cols = lax.dynamic_slice(gathered.reshape(8, rows, LANE), (0, chip * ada_rows, 0), (8, ada_rows, LANE))
    dmod_cols = dmod_cols.reshape(8, n_ada)
    g_ada, d, mo, vo = _adamw_ada(c_all, dmod_cols, w_ada[0], m_w_ada[0], v_w_ada[0], "adamw_w_ada")
    grad_out["w_ada"], delta["w_ada"], new_m["w_ada"], new_v["w_ada"] = g_ada[None], d[None], mo[None], vo[None]
    small_out = _adamw_small(_pack({n: weights[n] for n in SMALL}), small_sum, _pack({n: m_in[n] for n in SMALL}),
                             _pack({n: v_in[n] for n in SMALL}), small_shapes, "adamw_small")
    for dst, part in zip((grad_out, delta, new_m, new_v), small_out):
        dst.update(part)
    hv_in = reduce_halves(g_in, sf_in, ot_in, after=d)
    adamw(g_in, share(g_in, hv_in, 17), after=d)

    return (loss, grad_x[None], *[grad_out[n] for n in ORDER], *[delta[n] for n in ORDER],
            *[new_m[n] for n in ORDER], *[new_v[n] for n in ORDER])
```

```python
import functools
import math

import jax
import jax.numpy as jnp
from jax import lax
from jax.experimental import pallas as pl
from jax.experimental.pallas import tpu as pltpu
from jax.experimental.pallas import tpu_sc as plsc

F32 = jnp.float32
MXU_DTYPE = jnp.bfloat16
WIRE_DTYPE = jnp.bfloat16
EPS = 1e-6
LANE = 128
V7X_VMEM_LIMIT = 56 * 1024 * 1024
MESH = pl.DeviceIdType.MESH
N_CHIPS = 4
FLIPS = (2, 1, 3)

ADAM_LR = 0.001
ADAM_B1 = 0.9
ADAM_B2 = 0.999
ADAM_EPS = 1e-08
ADAM_WD = 0.01
ADAM_STEP = 10


def _params(*semantics):
    return pltpu.CompilerParams(dimension_semantics=semantics or None, vmem_limit_bytes=V7X_VMEM_LIMIT)


def _tile(dim, pref, unit=LANE):
    best = None
    t = unit
    while t <= min(dim, pref):
        if dim % t == 0:
            best = t
        t += unit
    return best if best is not None else dim


def _then(first, second):
    return lax.optimization_barrier((first, second))[1]


def _sum0(v):
    return jnp.sum(v, axis=0, keepdims=True)


def _mean1(v):
    return jnp.mean(v, axis=-1, keepdims=True)


def _gelu(x):
    return 0.5 * x * (1.0 + lax.erf(x * (1.0 / math.sqrt(2.0))))


def _gelu_grad(x):
    cdf = 0.5 * (1.0 + lax.erf(x * (1.0 / math.sqrt(2.0))))
    return cdf + x * jnp.exp(-0.5 * x * x) * (1.0 / math.sqrt(2.0 * math.pi))


def _dot(a, b, dims):
    return lax.dot_general(a, b, (dims, ((), ())), preferred_element_type=F32)


NN = ((1,), (0,))
NT = ((1,), (1,))
TN = ((0,), (0,))


def _mm(kind, a, b, out_dtypes, name, tm=2048, tn=512, extras=(), epilogue=None):
    if kind == "nn":
        (M, K), N = a.shape, b.shape[1]
    elif kind == "nt":
        (M, K), N = a.shape, b.shape[0]
    else:
        (K, M), N = a.shape, b.shape[1]
    tm, tn = _tile(M, tm), _tile(N, tn)
    a_spec = pl.BlockSpec((K, tm), lambda i, j: (0, i)) if kind == "tn" else pl.BlockSpec((tm, K), lambda i, j: (i, 0))
    b_spec = pl.BlockSpec((tn, K), lambda i, j: (j, 0)) if kind == "nt" else pl.BlockSpec((K, tn), lambda i, j: (0, j))
    mn_spec = pl.BlockSpec((tm, tn), lambda i, j: (i, j))
    dims = {"nn": NN, "nt": NT, "tn": TN}[kind]
    n_extra = len(extras)

    n_chunks = 1 if epilogue is None or kind == "tn" else max(1, tm // 256)
    rows_per = tm // n_chunks

    def body(a_ref, b_ref, *rest):
        for r in range(n_chunks):
            rows = slice(r * rows_per, (r + 1) * rows_per)
            acc = _dot(a_ref[...] if n_chunks == 1 else a_ref[rows, :], b_ref[...], dims)
            res = (acc,) if epilogue is None else epilogue(acc, *[e[rows, :] for e in rest[:n_extra]])
            for o_ref, val in zip(rest[n_extra:], res):
                o_ref[rows, :] = val.astype(o_ref.dtype)

    outs = pl.pallas_call(
        body, name=name, grid=(M // tm, N // tn),
        in_specs=[a_spec, b_spec] + [mn_spec] * n_extra,
        out_specs=[mn_spec] * len(out_dtypes),
        out_shape=[jax.ShapeDtypeStruct((M, N), d) for d in out_dtypes],
        compiler_params=_params("parallel", "arbitrary"),
    )(a, b, *extras)
    return outs


def _mm_ktiled(kind, pairs, name, tm=2048, tn=1024, tk=1408):
    a0, b0 = pairs[0]
    M, K = a0.shape
    N = b0.shape[1] if kind == "nn" else b0.shape[0]
    tm, tn, tk = _tile(M, tm), _tile(N, tn), _tile(K, tk)
    a_spec = pl.BlockSpec((tm, tk), lambda i, j, k: (i, k))
    b_spec = pl.BlockSpec((tk, tn), lambda i, j, k: (k, j)) if kind == "nn" else pl.BlockSpec((tn, tk), lambda i, j, k: (j, k))
    dims = NN if kind == "nn" else NT
    n_pairs = len(pairs)

    def body(*refs):
        o_ref = refs[2 * n_pairs]
        acc = _dot(refs[0][...], refs[1][...], dims)
        for p in range(1, n_pairs):
            acc = acc + _dot(refs[2 * p][...], refs[2 * p + 1][...], dims)

        @pl.when(pl.program_id(2) == 0)
        def _():
            o_ref[...] = acc

        @pl.when(pl.program_id(2) != 0)
        def _():
            o_ref[...] += acc

    return pl.pallas_call(
        body, name=name, grid=(M // tm, N // tn, K // tk),
        in_specs=[a_spec, b_spec] * n_pairs,
        out_specs=pl.BlockSpec((tm, tn), lambda i, j, k: (i, j)),
        out_shape=jax.ShapeDtypeStruct((M, N), F32),
        compiler_params=_params("parallel", "parallel", "arbitrary"),
    )(*[x for pair in pairs for x in pair])


def _gate_up(h, wg, wu, name):
    (M, K), N = h.shape, wg.shape[1]
    tm, tn = _tile(M, 2048), _tile(N, 512)

    n_chunks = max(1, tm // 256)
    rows_per = tm // n_chunks

    def body(h_ref, wg_ref, wu_ref, ag_ref, au_ref, f_ref):
        for r in range(n_chunks):
            rows = slice(r * rows_per, (r + 1) * rows_per)
            hv = h_ref[rows, :]
            ag = _dot(hv, wg_ref[...], NN)
            au = _dot(hv, wu_ref[...], NN)
            ag_ref[rows, :] = ag.astype(ag_ref.dtype)
            au_ref[rows, :] = au.astype(au_ref.dtype)
            f_ref[rows, :] = (ag * jax.nn.sigmoid(ag) * au).astype(f_ref.dtype)

    w_spec = pl.BlockSpec((K, tn), lambda i, j: (0, j))
    mn_spec = pl.BlockSpec((tm, tn), lambda i, j: (i, j))
    return pl.pallas_call(
        body, name=name, grid=(M // tm, N // tn),
        in_specs=[pl.BlockSpec((tm, K), lambda i, j: (i, 0)), w_spec, w_spec],
        out_specs=[mn_spec] * 3,
        out_shape=[jax.ShapeDtypeStruct((M, N), MXU_DTYPE)] * 3,
        compiler_params=_params("parallel", "arbitrary"),
    )(h, wg, wu)


def _swiglu_bwd_epilogue(dfin, ag, au):
    ag, au = ag.astype(F32), au.astype(F32)
    sg = jax.nn.sigmoid(ag)
    d_au = dfin * (ag * sg)
    d_ag = dfin * au * (sg * (1.0 + ag * (1.0 - sg)))
    return d_ag, d_au


def _row_specs(ts, width):
    return pl.BlockSpec((ts, width), lambda i: (i, 0)), pl.BlockSpec((1, width), lambda i: (0, 0))


def _cast_into_full(place, shard, g, name):
    R, C = shard.shape
    tr = _tile(R, 256, 16)
    n_blk = R // tr
    out_map = (lambda i, p: (i, p[0])) if g.by_cols else (lambda i, p: (p[0] * n_blk + i, 0))

    def body(p_ref, a_ref, o_ref):
        o_ref[...] = a_ref[...].astype(o_ref.dtype)

    return pl.pallas_call(
        body, name=name,
        grid_spec=pltpu.PrefetchScalarGridSpec(
            num_scalar_prefetch=1, grid=(n_blk,),
            in_specs=[pl.BlockSpec((tr, C), lambda i, p: (i, 0))],
            out_specs=pl.BlockSpec((tr, C), out_map)),
        out_shape=jax.ShapeDtypeStruct(g.full, WIRE_DTYPE),
        compiler_params=_params("arbitrary"),
    )(place, shard)


def _norm_mod(x, g, scale, shift, name):
    S, D = x.shape
    ts = _tile(S, 256, 16)
    tile, vec = _row_specs(ts, D)

    def body(x_ref, g_ref, sc_ref, sh_ref, h_ref):
        xv = x_ref[...]
        r = lax.rsqrt(_mean1(xv * xv) + EPS)
        h_ref[...] = ((xv * r) * g_ref[...] * (1.0 + sc_ref[...]) + sh_ref[...]).astype(h_ref.dtype)

    return pl.pallas_call(body, name=name, grid=(S // ts,), in_specs=[tile, vec, vec, vec], out_specs=tile,
                          out_shape=jax.ShapeDtypeStruct((S, D), MXU_DTYPE), compiler_params=_params("parallel"))(x, g, scale, shift)


def _residual_norm_mod(x, attn, gate, g, scale, shift, name):
    S, D = x.shape
    ts = _tile(S, 256, 16)
    tile, vec = _row_specs(ts, D)

    def body(x_ref, a_ref, gate_ref, g_ref, sc_ref, sh_ref, x1_ref, h_ref):
        x1 = x_ref[...] + gate_ref[...] * a_ref[...]
        x1_ref[...] = x1
        r = lax.rsqrt(_mean1(x1 * x1) + EPS)
        h_ref[...] = ((x1 * r) * g_ref[...] * (1.0 + sc_ref[...]) + sh_ref[...]).astype(h_ref.dtype)

    return pl.pallas_call(body, name=name, grid=(S // ts,), in_specs=[tile, tile, vec, vec, vec, vec],
                          out_specs=[tile, tile],
                          out_shape=[jax.ShapeDtypeStruct((S, D), F32), jax.ShapeDtypeStruct((S, D), MXU_DTYPE)],
                          compiler_params=_params("parallel"))(x, attn, gate, g, scale, shift)


def _final_loss_bwd(x1, f, gate2, final_g, target, name):
    S, D = x1.shape
    ts = _tile(S, 256, 16)
    tile, vec = _row_specs(ts, D)
    loss_spec = pl.BlockSpec((1, LANE), lambda i: (0, 0))

    def body(x1_ref, f_ref, gate_ref, g_ref, t_ref, dx2_ref, df_ref, dgate_ref, dg_ref, loss_ref):
        @pl.when(pl.program_id(0) == 0)
        def _():
            dgate_ref[...] = jnp.zeros_like(dgate_ref)
            dg_ref[...] = jnp.zeros_like(dg_ref)
            loss_ref[...] = jnp.zeros_like(loss_ref)

        fv, gate, g = f_ref[...], gate_ref[...], g_ref[...]
        x2 = x1_ref[...] + gate * fv
        r = lax.rsqrt(_mean1(x2 * x2) + EPS)
        xn = x2 * r
        err = xn * g - t_ref[...]
        loss_ref[...] += jnp.broadcast_to(0.5 * _sum0(_mean1(err * err)), loss_ref.shape)
        dy = err * (1.0 / D)
        dg_ref[...] += _sum0(dy * xn)
        dxn = dy * g
        dx2 = r * (dxn - xn * _mean1(dxn * xn))
        dx2_ref[...] = dx2
        dgate_ref[...] += _sum0(dx2 * fv)
        df_ref[...] = (dx2 * gate).astype(df_ref.dtype)

    return pl.pallas_call(
        body, name=name, grid=(S // ts,), in_specs=[tile, tile, vec, vec, tile],
        out_specs=[tile, tile, vec, vec, loss_spec],
        out_shape=[jax.ShapeDtypeStruct((S, D), F32), jax.ShapeDtypeStruct((S, D), MXU_DTYPE),
                   jax.ShapeDtypeStruct((1, D), F32), jax.ShapeDtypeStruct((1, D), F32),
                   jax.ShapeDtypeStruct((1, LANE), F32)],
        compiler_params=_params("arbitrary"),
    )(x1, f, gate2, final_g, target)


def _norm_mod_bwd(dh, xin, dres, g, scale, name, branch=None, gate=None):
    S, D = xin.shape
    ts = _tile(S, 256, 16)
    tile, vec = _row_specs(ts, D)
    with_gate = branch is not None

    def body(*refs):
        if with_gate:
            dh_ref, x_ref, dres_ref, g_ref, sc_ref, br_ref, gate_ref, dx_ref, dshift_ref, dscale_ref, dg_ref, dgate_ref, dbr_ref = refs
            accs = (dshift_ref, dscale_ref, dg_ref, dgate_ref)
        else:
            dh_ref, x_ref, dres_ref, g_ref, sc_ref, dx_ref, dshift_ref, dscale_ref, dg_ref = refs
            accs = (dshift_ref, dscale_ref, dg_ref)

        @pl.when(pl.program_id(0) == 0)
        def _():
            for acc in accs:
                acc[...] = jnp.zeros_like(acc)

        dh_v, xv, g_v = dh_ref[...], x_ref[...], g_ref[...]
        one_sc = 1.0 + sc_ref[...]
        r = lax.rsqrt(_mean1(xv * xv) + EPS)
        xn = xv * r
        dshift_ref[...] += _sum0(dh_v)
        dscale_ref[...] += _sum0(dh_v * (xn * g_v))
        dg_ref[...] += _sum0(dh_v * one_sc * xn)
        dxn = dh_v * (g_v * one_sc)
        dx = dres_ref[...] + r * (dxn - xn * _mean1(dxn * xn))
        dx_ref[...] = dx
        if with_gate:
            dgate_ref[...] += _sum0(dx * br_ref[...])
            dbr_ref[...] = (dx * gate_ref[...]).astype(dbr_ref.dtype)

    ins = [dh, xin, dres, g, scale] + ([branch, gate] if with_gate else [])
    in_specs = [tile, tile, tile, vec, vec] + ([tile, vec] if with_gate else [])
    out_specs = [tile, vec, vec, vec] + ([vec, tile] if with_gate else [])
    out_shape = [jax.ShapeDtypeStruct((S, D), F32)] + [jax.ShapeDtypeStruct((1, D), F32)] * 3
    if with_gate:
        out_shape += [jax.ShapeDtypeStruct((1, D), F32), jax.ShapeDtypeStruct((S, D), MXU_DTYPE)]
    return pl.pallas_call(body, name=name, grid=(S // ts,), in_specs=in_specs, out_specs=out_specs,
                          out_shape=out_shape, compiler_params=_params("arbitrary"))(*ins)


def _causal_weights(ws_ref, wt_ref, n_g):
    row = lax.broadcasted_iota(jnp.int32, (LANE, LANE), 0)
    col = lax.broadcasted_iota(jnp.int32, (LANE, LANE), 1)
    for g in range(n_g):
        wt_ref[g] = jnp.where(col <= row, ws_ref[g], 0.0).astype(wt_ref.dtype)


def _group_layernorm(v):
    xc = v - _mean1(v)
    rstd = lax.rsqrt(_mean1(xc * xc) + EPS)
    return xc * rstd, rstd


def _gmlp_fwd(proj, v_gain, w_s, b_t, out_gain, n_g, name):
    S = proj.shape[0]
    GW = n_g * LANE
    D = out_gain.shape[1]

    def body(p_ref, vg_ref, ws_ref, bt_ref, og_ref, on_ref, wt_ref):
        @pl.when(pl.program_id(0) == 0)
        def _():
            _causal_weights(ws_ref, wt_ref, n_g)

        for g in range(n_g):
            cols = slice(g * LANE, (g + 1) * LANE)
            u = _gelu(p_ref[:, cols])
            v = _gelu(p_ref[:, GW + g * LANE:GW + (g + 1) * LANE])
            vhat, _ = _group_layernorm(v)
            vln = (vhat * vg_ref[:, cols]).astype(MXU_DTYPE)
            mixed = _dot(wt_ref[g], vln, NN) + bt_ref[:, g:g + 1]
            o = u * mixed
            r = lax.rsqrt(_mean1(o * o) + EPS)
            on_ref[:, cols] = (o * r * og_ref[:, cols]).astype(on_ref.dtype)

    return pl.pallas_call(
        body, name=name, grid=(S // LANE,),
        in_specs=[pl.BlockSpec((LANE, 2 * GW), lambda n: (n, 0)),
                  pl.BlockSpec((1, GW), lambda n: (0, 0)),
                  pl.BlockSpec((n_g, LANE, LANE), lambda n: (0, 0, 0)),
                  pl.BlockSpec((LANE, n_g), lambda n: (0, 0)),
                  pl.BlockSpec((1, GW), lambda n: (0, 0))],
        out_specs=pl.BlockSpec((LANE, GW), lambda n: (n, 0)),
        out_shape=jax.ShapeDtypeStruct((S, D), MXU_DTYPE),
        scratch_shapes=[pltpu.VMEM((n_g, LANE, LANE), MXU_DTYPE)],
        compiler_params=_params("arbitrary"),
    )(proj, v_gain, w_s, b_t, out_gain)


def _gmlp_bwd(proj, d_on, v_gain, w_s, b_t, out_gain, dqkv, n_g, name):
    S, N_IN = proj.shape
    GW = n_g * LANE
    SBW = dqkv[0].shape[1]

    def body(p_ref, dn_ref, vg_ref, ws_ref, bt_ref, og_ref, dq_ref, dk_ref, dv_ref, dp_ref, dws_ref, dbt_ref, dvg_ref, dog_ref, wt_ref):
        for i, part_ref in enumerate((dq_ref, dk_ref, dv_ref)):
            dp_ref[:, 2 * GW + i * SBW:2 * GW + (i + 1) * SBW] = part_ref[...]

        @pl.when(pl.program_id(0) == 0)
        def _():
            _causal_weights(ws_ref, wt_ref, n_g)
            dws_ref[...] = jnp.zeros_like(dws_ref)
            dbt_ref[...] = jnp.zeros_like(dbt_ref)
            dvg_ref[...] = jnp.zeros_like(dvg_ref)
            dog_ref[...] = jnp.zeros_like(dog_ref)

        row = lax.broadcasted_iota(jnp.int32, (LANE, LANE), 0)
        col = lax.broadcasted_iota(jnp.int32, (LANE, LANE), 1)
        for g in range(n_g):
            cols = slice(g * LANE, (g + 1) * LANE)
            vcols = slice(GW + g * LANE, GW + (g + 1) * LANE)
            pu, pv = p_ref[:, cols], p_ref[:, vcols]
            u, v = _gelu(pu), _gelu(pv)
            vhat, rstd = _group_layernorm(v)
            gain = vg_ref[:, cols]
            vln = (vhat * gain).astype(MXU_DTYPE)
            mixed = _dot(wt_ref[g], vln, NN) + bt_ref[:, g:g + 1]
            o = u * mixed
            r = lax.rsqrt(_mean1(o * o) + EPS)
            oh = o * r
            dn = dn_ref[:, cols]
            dog_ref[:, cols] += _sum0(dn * oh)
            dhn = dn * og_ref[:, cols]
            d_o = r * (dhn - oh * _mean1(dhn * oh))
            du = d_o * mixed
            dmix = d_o * u
            dbt_ref[:, g:g + 1] += jnp.sum(dmix, axis=1, keepdims=True)
            dmix_b = dmix.astype(MXU_DTYPE)
            dws_ref[g] += jnp.where(col <= row, _dot(dmix_b, vln, NT), 0.0)
            dvln = _dot(wt_ref[g], dmix_b, TN)
            dvg_ref[:, cols] += _sum0(dvln * vhat)
            dxh = dvln * gain
            dv = rstd * (dxh - _mean1(dxh) - vhat * _mean1(dxh * vhat))
            dp_ref[:, cols] = (du * _gelu_grad(pu)).astype(dp_ref.dtype)
            dp_ref[:, vcols] = (dv * _gelu_grad(pv)).astype(dp_ref.dtype)

    return pl.pallas_call(
        body, name=name, grid=(S // LANE,),
        in_specs=[pl.BlockSpec((LANE, 2 * GW), lambda n: (n, 0)),
                  pl.BlockSpec((LANE, GW), lambda n: (n, 0)),
                  pl.BlockSpec((1, GW), lambda n: (0, 0)),
                  pl.BlockSpec((n_g, LANE, LANE), lambda n: (0, 0, 0)),
                  pl.BlockSpec((LANE, n_g), lambda n: (0, 0)),
                  pl.BlockSpec((1, GW), lambda n: (0, 0))] + [pl.BlockSpec((LANE, SBW), lambda n: (n, 0))] * 3,
        out_specs=[pl.BlockSpec((LANE, N_IN), lambda n: (n, 0)),
                   pl.BlockSpec((n_g, LANE, LANE), lambda n: (0, 0, 0)),
                   pl.BlockSpec((LANE, n_g), lambda n: (0, 0)),
                   pl.BlockSpec((1, GW), lambda n: (0, 0)),
                   pl.BlockSpec((1, GW), lambda n: (0, 0))],
        out_shape=[jax.ShapeDtypeStruct((S, N_IN), MXU_DTYPE),
                   jax.ShapeDtypeStruct((n_g, LANE, LANE), F32),
                   jax.ShapeDtypeStruct((LANE, n_g), F32),
                   jax.ShapeDtypeStruct((1, GW), F32),
                   jax.ShapeDtypeStruct((1, GW), F32)],
        scratch_shapes=[pltpu.VMEM((n_g, LANE, LANE), MXU_DTYPE)],
        compiler_params=_params("arbitrary"),
    )(proj, d_on, v_gain, w_s, b_t, out_gain, *dqkv)


def _tri_sum(v, tri, exact=True):
    hi = v.astype(MXU_DTYPE)
    if not exact:
        return _dot(hi, tri, NN)
    lo = (v - hi.astype(F32)).astype(MXU_DTYPE)
    return _dot(hi, tri, NN) + _dot(lo, tri, NN)


def _log_sigmoids(z):
    sp = jnp.log(1.0 + jnp.exp(-jnp.abs(z)))
    return jnp.minimum(z, 0.0) - sp, jnp.minimum(-z, 0.0) - sp


def _rows(i, size):
    return pl.ds(pl.multiple_of(i * size, size), size)


SB_QUERY_TILE = 2048
SB_KEY_TILE = 256


def _sb_tiles(S):
    tq = _tile(S, SB_QUERY_TILE)
    tk = _tile(tq, SB_KEY_TILE)
    assert (tq // tk) % 2 == 0, "the key sweep takes two blocks a pass"
    return tq, tk, S // tq, tq // tk


def _triangle(n, keep):
    row = lax.broadcasted_iota(jnp.int32, (n, n), 0)
    col = lax.broadcasted_iota(jnp.int32, (n, n), 1)
    return jnp.where(keep(row, col), 1.0, 0.0).astype(MXU_DTYPE)


def _strictly_before(tq, tk, key_offset):
    row = lax.broadcasted_iota(jnp.int32, (tq, tk), 0)
    col = lax.broadcasted_iota(jnp.int32, (tq, tk), 1)
    return col + key_offset < row


def _sb_specs(S, n_g, n_h):
    base = 2 * n_g
    q_spec = pl.BlockSpec((S, LANE), lambda h: (0, base + h))
    k_spec = pl.BlockSpec((S, LANE), lambda h: (0, base + n_h + h))
    v_spec = pl.BlockSpec((S, LANE), lambda h: (0, base + 2 * n_h + h))
    gain_spec = pl.BlockSpec((1, LANE), lambda h: (0, n_g + h))
    head_spec = pl.BlockSpec((S, LANE), lambda h: (0, h))
    return q_spec, k_spec, v_spec, gain_spec, head_spec


def _sb_fwd(proj, out_gain, on_buffer, n_g, n_h, name):
    S = proj.shape[0]
    TQ, TK, NQ, KPQ = _sb_tiles(S)
    scale = LANE ** -0.5
    q_spec, k_spec, v_spec, gain_spec, head_spec = _sb_specs(S, n_g, n_h)

    def body(q_ref, k_ref, v_ref, og_ref, _, o_ref, on_ref, ls_ref, qb, kb, vb):
        qb[...] = q_ref[...].astype(MXU_DTYPE)
        kb[...] = k_ref[...].astype(MXU_DTYPE)
        vb[...] = v_ref[...].astype(MXU_DTYPE)
        after = _triangle(TK, lambda r, c: r > c)

        def block(qi, j, ctail, acc, key_offset):
            skip = key_offset or 0
            z = _dot(qi[skip:], kb[_rows(j, TK), :], NT) * scale
            lb, l1m = _log_sigmoids(z)
            if key_offset is not None:
                strict = _strictly_before(TQ - skip, TK, 0)
                l1m = jnp.where(strict, l1m, 0.0)
            a = jnp.exp(lb + ctail[skip:] + _tri_sum(l1m, after))
            if key_offset is not None:
                a = jnp.where(strict, a, 0.0)
            acc_new = acc[skip:] + _dot(a.astype(MXU_DTYPE), vb[_rows(j, TK), :], NN)
            ctail_new = ctail[skip:] + jnp.sum(l1m, axis=1, keepdims=True)
            if skip:
                ctail_new = jnp.concatenate([ctail[:skip], ctail_new], axis=0)
                acc_new = jnp.concatenate([acc[:skip], acc_new], axis=0)
            return ctail_new, acc_new

        def q_loop(i, carry):
            qi = qb[_rows(i, TQ), :]
            state = (jnp.zeros((TQ, 1), F32), jnp.zeros((TQ, LANE), F32))
            for d in reversed(range(KPQ)):
                state = block(qi, i * KPQ + d, state[0], state[1], d * TK)
            def pair(jj, st):
                st = block(qi, i * KPQ - 1 - 2 * jj, st[0], st[1], None)
                return block(qi, i * KPQ - 2 - 2 * jj, st[0], st[1], None)

            ctail, acc = lax.fori_loop(0, i * (KPQ // 2), pair, state)
            ls_ref[_rows(i, TQ), :] = jnp.broadcast_to(ctail, (TQ, LANE))
            o_ref[_rows(i, TQ), :] = acc
            r = lax.rsqrt(_mean1(acc * acc) + EPS)
            on_ref[_rows(i, TQ), :] = (acc * r * og_ref[...]).astype(on_ref.dtype)
            return carry

        lax.fori_loop(0, NQ, q_loop, 0)

    return pl.pallas_call(
        body, name=name, grid=(n_h,),
        in_specs=[q_spec, k_spec, v_spec, gain_spec, pl.BlockSpec(memory_space=pl.ANY)],
        out_specs=[head_spec, pl.BlockSpec((S, LANE), lambda h: (0, n_g + h)), head_spec],
        out_shape=[jax.ShapeDtypeStruct((S, n_h * LANE), F32), jax.ShapeDtypeStruct(on_buffer.shape, MXU_DTYPE),
                   jax.ShapeDtypeStruct((S, n_h * LANE), F32)],
        input_output_aliases={4: 1},
        scratch_shapes=[pltpu.VMEM((S, LANE), MXU_DTYPE)] * 3,
        compiler_params=_params("parallel"),
    )(proj, proj, proj, out_gain, on_buffer)


def _sb_bwd(proj, o_sb, l_sum, d_on, out_gain, n_g, n_h, name):
    S = proj.shape[0]
    TQ, TK, NQ, KPQ = _sb_tiles(S)
    scale = LANE ** -0.5
    q_spec, k_spec, v_spec, gain_spec, head_spec = _sb_specs(S, n_g, n_h)
    dn_spec = pl.BlockSpec((S, LANE), lambda h: (0, n_g + h))
    dgain_spec = pl.BlockSpec((1, LANE), lambda h: (0, h))

    def body(q_ref, k_ref, v_ref, o_ref, ls_ref, dn_ref, og_ref, dq_ref, dk_ref, dv_ref, dog_ref,
             qb, kb, vb, dob, dk_acc, dv_acc):
        qb[...] = q_ref[...].astype(MXU_DTYPE)
        kb[...] = k_ref[...].astype(MXU_DTYPE)
        vb[...] = v_ref[...].astype(MXU_DTYPE)
        o, dn = o_ref[...], dn_ref[...]
        r = lax.rsqrt(_mean1(o * o) + EPS)
        oh = o * r
        dog_ref[...] = _sum0(dn * oh)
        dhn = dn * og_ref[...]
        dob[...] = (r * (dhn - oh * _mean1(dhn * oh))).astype(MXU_DTYPE)
        dk_acc[...] = jnp.zeros_like(dk_acc)
        dv_acc[...] = jnp.zeros_like(dv_acc)

        up_to = _triangle(TK, lambda r, c: r <= c)
        before = _triangle(TK, lambda r, c: r < c)

        def block(qi, doi, ltot, j, cl, cdl, dq, key_offset):
            skip = key_offset or 0
            q_in, do_in = qi[skip:], doi[skip:]
            kj, vj = kb[_rows(j, TK), :], vb[_rows(j, TK), :]
            z = _dot(q_in, kj, NT) * scale
            lb, l1m = _log_sigmoids(z)
            if key_offset is not None:
                strict = _strictly_before(TQ - skip, TK, 0)
                l1m = jnp.where(strict, l1m, 0.0)
            a = jnp.exp(lb + (ltot[skip:] - (cl[skip:] + _tri_sum(l1m, up_to))))
            if key_offset is not None:
                a = jnp.where(strict, a, 0.0)
            dl = _dot(do_in, vj, NT) * a
            d_l1m = cdl[skip:] + _tri_sum(dl, before, exact=False)
            beta = jnp.exp(lb)
            dz = dl * (1.0 - beta) - beta * d_l1m
            if key_offset is not None:
                dz = jnp.where(strict, dz, 0.0)
            dzs = (dz * scale).astype(MXU_DTYPE)
            dk_acc[_rows(j, TK), :] += _dot(dzs, q_in, TN)
            dv_acc[_rows(j, TK), :] += _dot(a.astype(MXU_DTYPE), do_in, TN)
            cl_new = cl[skip:] + jnp.sum(l1m, axis=1, keepdims=True)
            cdl_new = cdl[skip:] + jnp.sum(dl, axis=1, keepdims=True)
            dq_new = dq[skip:] + _dot(dzs, kj, NN)
            if skip:
                cl_new = jnp.concatenate([cl[:skip], cl_new], axis=0)
                cdl_new = jnp.concatenate([cdl[:skip], cdl_new], axis=0)
                dq_new = jnp.concatenate([dq[:skip], dq_new], axis=0)
            return cl_new, cdl_new, dq_new

        def q_loop(i, carry):
            qi, doi = qb[_rows(i, TQ), :], dob[_rows(i, TQ), :]
            ltot = ls_ref[_rows(i, TQ), :][:, :1]
            zero_col = jnp.zeros((TQ, 1), F32)
            def pair(jj, st):
                st = block(qi, doi, ltot, 2 * jj, st[0], st[1], st[2], None)
                return block(qi, doi, ltot, 2 * jj + 1, st[0], st[1], st[2], None)

            state = lax.fori_loop(0, i * (KPQ // 2), pair, (zero_col, zero_col, jnp.zeros((TQ, LANE), F32)))
            for d in range(KPQ):
                state = block(qi, doi, ltot, i * KPQ + d, state[0], state[1], state[2], d * TK)
            dq_ref[_rows(i, TQ), :] = state[2].astype(dq_ref.dtype)
            return carry

        lax.fori_loop(0, NQ, q_loop, 0)
        dk_ref[...] = dk_acc[...].astype(dk_ref.dtype)
        dv_ref[...] = dv_acc[...].astype(dv_ref.dtype)

    W = n_h * LANE
    return pl.pallas_call(
        body, name=name, grid=(n_h,),
        in_specs=[q_spec, k_spec, v_spec, head_spec, head_spec, dn_spec, gain_spec],
        out_specs=[head_spec, head_spec, head_spec, dgain_spec],
        out_shape=[jax.ShapeDtypeStruct((S, W), MXU_DTYPE)] * 3 + [jax.ShapeDtypeStruct((1, W), F32)],
        scratch_shapes=[pltpu.VMEM((S, LANE), MXU_DTYPE)] * 4 + [pltpu.VMEM((S, LANE), F32)] * 2,
        compiler_params=_params("parallel"),
    )(proj, proj, proj, o_sb, l_sum, d_on, out_gain)


def _mod_part(c_all, w_ada, b_ada_cols, name):
    B, K = c_all.shape
    N = w_ada.shape[1]
    tn = _tile(N, 512)

    def body(c_ref, w_ref, b_ref, o_ref):
        cv = c_ref[...]
        ca = (cv * jax.nn.sigmoid(cv)).astype(MXU_DTYPE)
        o_ref[...] = _dot(ca, w_ref[...].astype(MXU_DTYPE), NN) + b_ref[...]

    return pl.pallas_call(
        body, name=name, grid=(N // tn,),
        in_specs=[pl.BlockSpec((B, K), lambda j: (0, 0)), pl.BlockSpec((K, tn), lambda j: (0, j)),
                  pl.BlockSpec((1, tn), lambda j: (0, j))],
        out_specs=pl.BlockSpec((B, tn), lambda j: (0, j)),
        out_shape=jax.ShapeDtypeStruct((B, N), F32), compiler_params=_params("parallel"))(c_all, w_ada, b_ada_cols)


def _adamw_math(w, g, m, v):
    m = ADAM_B1 * m + (1.0 - ADAM_B1) * g
    v = ADAM_B2 * v + (1.0 - ADAM_B2) * (g * g)
    m_hat = m / (1.0 - ADAM_B1 ** ADAM_STEP)
    v_hat = v / (1.0 - ADAM_B2 ** ADAM_STEP)
    delta = -ADAM_LR * (m_hat / (jnp.sqrt(v_hat) + ADAM_EPS) + ADAM_WD * w)
    return delta, m, v


def _adamw(w, g, m, v, name):
    R, C = w.shape
    tr = _tile(R, max(8, (3 << 18) // C), 8)
    spec = pl.BlockSpec((tr, C), lambda i: (i, 0))

    def body(w_ref, g_ref, m_ref, v_ref, go_ref, d_ref, mo_ref, vo_ref):
        g = g_ref[...]
        go_ref[...] = g
        d_ref[...], mo_ref[...], vo_ref[...] = _adamw_math(w_ref[...], g, m_ref[...], v_ref[...])

    return pl.pallas_call(body, name=name, grid=(R // tr,), in_specs=[spec] * 4, out_specs=[spec] * 4,
                          out_shape=[jax.ShapeDtypeStruct((R, C), F32)] * 4, compiler_params=_params("parallel"))(w, g, m, v)


def _adamw_ada(c_all, dmod_cols, w, m, v, name):
    K, N = w.shape
    B = c_all.shape[0]
    tk, tn = _tile(K, 512), _tile(N, 1024)
    spec = pl.BlockSpec((tk, tn), lambda i, j: (i, j))

    def body(c_ref, dm_ref, w_ref, m_ref, v_ref, g_ref, d_ref, mo_ref, vo_ref):
        cv = c_ref[...]
        ca = (cv * jax.nn.sigmoid(cv)).astype(MXU_DTYPE)
        g = _dot(ca, dm_ref[...].astype(MXU_DTYPE), TN)
        g_ref[...] = g
        d_ref[...], mo_ref[...], vo_ref[...] = _adamw_math(w_ref[...], g, m_ref[...], v_ref[...])

    return pl.pallas_call(
        body, name=name, grid=(K // tk, N // tn),
        in_specs=[pl.BlockSpec((B, tk), lambda i, j: (0, i)), pl.BlockSpec((B, tn), lambda i, j: (0, j)), spec, spec, spec],
        out_specs=[spec] * 4, out_shape=[jax.ShapeDtypeStruct((K, N), F32)] * 4,
        compiler_params=_params("parallel", "parallel"))(c_all, dmod_cols, w, m, v)


def _sum_devices(gathered, n_dev, name):
    R = gathered.shape[0] // n_dev
    C = gathered.shape[1]
    tr = _tile(R, 512, 8)
    n_blk = R // tr

    def body(*refs):
        acc = refs[0][...]
        for r in refs[1:n_dev]:
            acc = acc + r[...]
        refs[n_dev][...] = acc

    in_specs = [pl.BlockSpec((tr, C), functools.partial(lambda i, d: (d * n_blk + i, 0), d=d)) for d in range(n_dev)]
    return pl.pallas_call(body, name=name, grid=(n_blk,), in_specs=in_specs,
                          out_specs=pl.BlockSpec((tr, C), lambda i: (i, 0)),
                          out_shape=jax.ShapeDtypeStruct((R, C), F32), compiler_params=_params("parallel"))(*([gathered] * n_dev))


def _place():
    x, y, c = lax.axis_index("x"), lax.axis_index("y"), lax.axis_index("c")
    return x, y, c


def _allgather8(blk, name):
    m_per, n = blk.shape

    def body(x_ref, out_ref, send_sems, recv_sems, local_sem):
        x, y, c = _place()
        me, sibling = (x, y, c), (x, y, 1 - c)
        chips = [(1 - x, y), (x, 1 - y), (1 - x, 1 - y)]

        def rows(px, py, pc):
            return out_ref.at[pl.ds((4 * px + 2 * py + pc) * m_per, m_per), :]

        def copy(k, block, to, src=None):
            return pltpu.make_async_remote_copy(
                src_ref=rows(*block) if src is None else src, dst_ref=rows(*block),
                send_sem=send_sems.at[k], recv_sem=recv_sems.at[k], device_id=to, device_id_type=MESH)

        mine = pltpu.make_async_copy(x_ref, rows(*me), local_sem)
        mine.start()
        first = [copy(0, me, sibling, src=x_ref)]
        first += [copy(1 + j, me, (*chip, c), src=x_ref) for j, chip in enumerate(chips)]
        for cp in first:
            cp.start()
        passed = [copy(4 + j, (*chip, c), sibling) for j, chip in enumerate(chips)]
        for j, chip in enumerate(chips):
            copy(1 + j, (*chip, c), me).wait_recv()
            passed[j].start()
        copy(0, sibling, me).wait_recv()
        for j, chip in enumerate(chips):
            copy(4 + j, (*chip, 1 - c), me).wait_recv()
        for cp in first + passed:
            cp.wait_send()
        mine.wait()

    return pl.pallas_call(
        body, name=name,
        out_shape=jax.ShapeDtypeStruct((8 * m_per, n), blk.dtype),
        in_specs=[pl.BlockSpec(memory_space=pltpu.VMEM)],
        out_specs=pl.BlockSpec(memory_space=pltpu.VMEM),
        scratch_shapes=[pltpu.SemaphoreType.DMA((7,)), pltpu.SemaphoreType.DMA((7,)), pltpu.SemaphoreType.DMA],
        compiler_params=pltpu.CompilerParams(vmem_limit_bytes=V7X_VMEM_LIMIT),
    )(blk)


class _Sharded:
    def __init__(self, shard_shape, by_cols):
        r, c = shard_shape
        self.by_cols = by_cols
        self.full = (r, N_CHIPS * c) if by_cols else (N_CHIPS * r, c)
        self.shard = (r, c)
        self.half_rows = r // 2
        self.half = (r // 2, c)

    def half_of(self, ref, k, hc):
        r, c = self.shard
        h = self.half_rows
        if self.by_cols:
            return ref.at[pl.ds(hc * h, h), pl.ds(k * c, c)]
        return ref.at[pl.ds(k * r + hc * h, h), :]

    def chunk_of(self, ref, k, hc, ch, n):
        r, c = self.shard
        h = self.half_rows
        q = h // n
        if self.by_cols:
            return ref.at[pl.ds(hc * h + ch * q, q), pl.ds(k * c, c)]
        return ref.at[pl.ds(k * r + hc * h + ch * q, q), :]

    def half_of_shard(self, ref, hc):
        return ref.at[pl.ds(hc * self.half_rows, self.half_rows), :]

    def part_of_halves(self, ref, k):
        r, c = self.shard
        h = self.half_rows
        return ref.at[:, pl.ds(k * c, c)] if self.by_cols else ref.at[pl.ds(k * h, h), :]


def _on_each_place(x, y, c, fn, by_chip=True, by_core=True):
    q = 2 * x + y
    for k in range(N_CHIPS if by_chip else 1):
        for cc in range(2 if by_core else 1):
            cond = None
            if by_chip:
                cond = q == k
            if by_core:
                cond = (c == cc) if cond is None else jnp.logical_and(cond, c == cc)
            pl.when(cond)(functools.partial(fn, k, cc))


def _chip_id(k, c):
    return (k // 2, k % 2, c)


def _handshake(peers):
    barrier = pltpu.get_barrier_semaphore()
    for peer in peers:
        pl.semaphore_signal(barrier, inc=1, device_id=peer, device_id_type=MESH)
    pl.semaphore_wait(barrier, len(peers))


def _on_sequencer(body, inputs, out_structs, n_copies, peers_of, name, collective_id, return_inputs=False):
    in_refs = [jax.new_ref(a, memory_space=pltpu.MemorySpace.HBM) for a in inputs]
    out_refs = [jax.empty_ref(s, memory_space=pltpu.MemorySpace.HBM) for s in out_structs]

    @pl.kernel(mesh=plsc.ScalarSubcoreMesh(axis_name="sequencer", num_cores=1), name=name,
               scratch_types=(pltpu.SemaphoreType.DMA((n_copies,)), pltpu.SemaphoreType.DMA((n_copies,))),
               compiler_params=pltpu.CompilerParams(collective_id=collective_id))
    def launch(send_sems, recv_sems):
        x, y, c = _place()
        _handshake(peers_of(x, y, c))
        body(in_refs, out_refs, send_sems, recv_sems, x, y, c)

    launch()
    return [r[...] for r in (in_refs if return_inputs else out_refs)]


def _sibling(x, y, c):
    return [(x, y, 1 - c)]


def _same_core_of_other_chips(x, y, c):
    return [(1 - x, y, c), (x, 1 - y, c), (1 - x, 1 - y, c)]


GATHER_CHUNKS = 4
GATHER_COPIES = 6 * GATHER_CHUNKS


def _place_slab(place, parts, loss_part, rows, name):
    n_parts = len(parts)

    def body(p_ref, *refs):
        loss_ref, out_ref = refs[n_parts], refs[n_parts + 1]
        at = 0
        for ref in refs[:n_parts]:
            if len(ref.shape) == 2 and ref.shape[0] == 1:
                for r in range(ref.shape[1] // LANE):
                    out_ref[at + r:at + r + 1, :] = ref[:, r * LANE:(r + 1) * LANE]
                at += ref.shape[1] // LANE
            else:
                n_rows = math.prod(ref.shape) // LANE
                out_ref[at:at + n_rows, :] = ref[...].reshape(n_rows, LANE)
                at += n_rows
        out_ref[at:at + 8, :] = jnp.broadcast_to(loss_ref[...], (8, LANE))
        out_ref[at + 8:, :] = jnp.zeros((rows - at - 8, LANE), F32)

    def whole(shape):
        return pl.BlockSpec(shape, functools.partial(lambda i, p, nd: (0,) * nd, nd=len(shape)))

    return pl.pallas_call(
        body, name=name,
        grid_spec=pltpu.PrefetchScalarGridSpec(
            num_scalar_prefetch=1, grid=(1,),
            in_specs=[whole(a.shape) for a in parts] + [whole(loss_part.shape)],
            out_specs=pl.BlockSpec((rows, LANE), lambda i, p: (2 * p[0] + p[1], 0))),
        out_shape=jax.ShapeDtypeStruct((8 * rows, LANE), F32),
        compiler_params=_params(),
    )(place, *parts, loss_part)


def _allgather8_on_sequencer(placed, m_per, name, collective_id):
    def body(refs, _, send_sems, recv_sems, x, y, c):
        out_ref, = refs

        def at_place(k, cc):
            def rows(kk, pc):
                return out_ref.at[pl.ds((2 * kk + pc) * m_per, m_per), :]

            def copy(slot, block, to):
                return pltpu.make_async_remote_copy(src_ref=rows(*block), dst_ref=rows(*block), send_sem=send_sems.at[slot],
                                                    recv_sem=recv_sems.at[slot], device_id=to, device_id_type=MESH)

            others = [k ^ flip for flip in FLIPS]
            sends = [copy(0, (k, cc), _chip_id(k, 1 - cc))] + [copy(1 + j, (k, cc), _chip_id(kk, cc)) for j, kk in enumerate(others)]
            for cp in sends:
                cp.start()
            for j, kk in enumerate(others):
                copy(1 + j, (kk, cc), _chip_id(k, cc)).wait_recv()
                cp = copy(4 + j, (kk, cc), _chip_id(k, 1 - cc))
                cp.start()
                sends.append(cp)
            copy(0, (k, 1 - cc), _chip_id(k, cc)).wait_recv()
            for j, kk in enumerate(others):
                copy(4 + j, (kk, 1 - cc), _chip_id(k, cc)).wait_recv()
            for cp in sends:
                cp.wait_send()

        _on_each_place(x, y, c, at_place)

    def peers(x, y, c):
        return _sibling(x, y, c) + _same_core_of_other_chips(x, y, c)

    return _on_sequencer(body, [placed], [], 7, peers, name, collective_id, return_inputs=True)[0]


def _gather_weights(fulls, geoms, name, collective_id):
    n_w = len(fulls)
    n_ch, n_relay = GATHER_CHUNKS, GATHER_CHUNKS // 2
    f_refs = [jax.new_ref(f, memory_space=pltpu.MemorySpace.HBM) for f in fulls]
    FLIP_X, FLIP_Y, FLIP_BOTH = FLIPS
    TO_X, TO_Y, RELAY_TO_Y, RELAY_TO_X, ON_X, ON_Y, ON_DIAG = 0, n_ch, 2 * n_ch, 2 * n_ch + n_relay, 3 * n_ch, 4 * n_ch, 5 * n_ch

    @pl.kernel(mesh=plsc.ScalarSubcoreMesh(axis_name="sequencer", num_cores=1), name=name,
               scratch_types=(pltpu.SemaphoreType.DMA((GATHER_COPIES * n_w,)), pltpu.SemaphoreType.DMA((GATHER_COPIES * n_w,))),
               compiler_params=pltpu.CompilerParams(collective_id=collective_id))
    def launch(send_sems, recv_sems):
        x, y, c = _place()
        _handshake([(x, y, 1 - c), (1 - x, y, c), (x, 1 - y, c)])

        def at_place(k, cc):
            kx, ky, kd = k ^ FLIP_X, k ^ FLIP_Y, k ^ FLIP_BOTH
            me, sibling = _chip_id(k, cc), _chip_id(k, 1 - cc)
            started = []

            def copy(i, slot, src, dst, to, start=True):
                cp = pltpu.make_async_remote_copy(src_ref=src, dst_ref=dst, send_sem=send_sems.at[GATHER_COPIES * i + slot],
                                                  recv_sem=recv_sems.at[GATHER_COPIES * i + slot], device_id=to, device_id_type=MESH)
                if start:
                    cp.start()
                    started.append(cp)
                return cp

            def pass_on(i, slot, ref, to):
                copy(i, slot, ref, ref, to)

            def landed(i, slot, ref):
                copy(i, slot, ref, ref, me, start=False).wait_recv()

            y_order = [(n_relay + s) % n_ch for s in range(n_ch)]
            for i, (g, f_ref) in enumerate(zip(geoms, f_refs)):
                for s in range(n_ch):
                    pass_on(i, TO_X + s, g.chunk_of(f_ref, k, cc, s, n_ch), _chip_id(kx, cc))
                    pass_on(i, TO_Y + y_order[s], g.chunk_of(f_ref, k, cc, y_order[s], n_ch), _chip_id(ky, cc))
            for i, (g, f_ref) in enumerate(zip(geoms, f_refs)):
                for s in range(n_ch):
                    from_x = g.chunk_of(f_ref, kx, cc, s, n_ch)
                    landed(i, TO_X + s, from_x)
                    if s < n_relay:
                        pass_on(i, RELAY_TO_Y + s, from_x, _chip_id(ky, cc))
                    pass_on(i, ON_X + s, from_x, sibling)
                    ch = y_order[s]
                    from_y = g.chunk_of(f_ref, ky, cc, ch, n_ch)
                    landed(i, TO_Y + ch, from_y)
                    if ch >= n_relay:
                        pass_on(i, RELAY_TO_X + ch - n_relay, from_y, _chip_id(kx, cc))
                    pass_on(i, ON_Y + ch, from_y, sibling)
                for r in range(n_relay):
                    via_y = g.chunk_of(f_ref, kd, cc, r, n_ch)
                    landed(i, RELAY_TO_Y + r, via_y)
                    pass_on(i, ON_DIAG + r, via_y, sibling)
                    via_x = g.chunk_of(f_ref, kd, cc, n_relay + r, n_ch)
                    landed(i, RELAY_TO_X + r, via_x)
                    pass_on(i, ON_DIAG + n_relay + r, via_x, sibling)
            for i, (g, f_ref) in enumerate(zip(geoms, f_refs)):
                for slot, kk in ((ON_X, kx), (ON_Y, ky), (ON_DIAG, kd)):
                    for ch in range(n_ch):
                        landed(i, slot + ch, g.chunk_of(f_ref, kk, 1 - cc, ch, n_ch))
            for cp in started:
                cp.wait_send()

        _on_each_place(x, y, c, at_place)

    launch()
    return [f_ref[...] for f_ref in f_refs]


def _swap_core_halves(grads, geoms, name, collective_id):
    n_cp = sum(1 if g.by_cols else N_CHIPS for g in geoms)

    def body(g_refs, t_refs, send_sems, recv_sems, x, y, c):

        def at_place(_, cc):
            def pairs(hc):
                out = []
                for g, g_ref, t_ref in zip(geoms, g_refs, t_refs):
                    if g.by_cols:
                        out.append((g_ref.at[pl.ds(hc * g.half_rows, g.half_rows), :], t_ref))
                    else:
                        out += [(g.half_of(g_ref, k, hc), g.part_of_halves(t_ref, k)) for k in range(N_CHIPS)]
                return out

            sends = [pltpu.make_async_remote_copy(src_ref=src, dst_ref=dst, send_sem=send_sems.at[n],
                                                  recv_sem=recv_sems.at[n], device_id=(x, y, 1 - cc), device_id_type=MESH)
                     for n, (src, dst) in enumerate(pairs(1 - cc))]
            for cp in sends:
                cp.start()
            for n, (src, dst) in enumerate(pairs(cc)):
                pltpu.make_async_remote_copy(src_ref=src, dst_ref=dst, send_sem=send_sems.at[n], recv_sem=recv_sems.at[n],
                                             device_id=(x, y, cc), device_id_type=MESH).wait_recv()
            for cp in sends:
                cp.wait_send()

        _on_each_place(x, y, c, at_place, by_chip=False)

    return _on_sequencer(body, grads, [jax.ShapeDtypeStruct((g.full[0] // 2, g.full[1]), F32) for g in geoms],
                         n_cp, _sibling, name, collective_id)


def _send_to_sibling(buffers, name, collective_id):
    def body(src_refs, dst_refs, send_sems, recv_sems, x, y, c):
        def copy(i):
            return pltpu.make_async_remote_copy(src_ref=src_refs[i], dst_ref=dst_refs[i], send_sem=send_sems.at[i],
                                                recv_sem=recv_sems.at[i], device_id=(x, y, 1 - c), device_id_type=MESH)

        for i in range(len(buffers)):
            copy(i).start()
        for i in range(len(buffers)):
            copy(i).wait()

    return _on_sequencer(body, buffers, [jax.ShapeDtypeStruct(t.shape, t.dtype) for t in buffers], len(buffers),
                         _sibling, name, collective_id)


def _scatter_chip_sums(sums, geoms, name, collective_id):
    def body(s_refs, r_refs, send_sems, recv_sems, x, y, c):

        def at_place(k, _):
            sends = []
            for i, (g, s_ref, r_ref) in enumerate(zip(geoms, s_refs, r_refs)):
                for j, flip in enumerate(FLIPS):
                    kk = k ^ flip
                    cp = pltpu.make_async_remote_copy(
                        src_ref=g.part_of_halves(s_ref, kk), dst_ref=r_ref.at[j], send_sem=send_sems.at[3 * i + j],
                        recv_sem=recv_sems.at[3 * i + j], device_id=(kk // 2, kk % 2, c), device_id_type=MESH)
                    cp.start()
                    sends.append(cp)
            for i, (g, s_ref, r_ref) in enumerate(zip(geoms, s_refs, r_refs)):
                for j in range(len(FLIPS)):
                    pltpu.make_async_remote_copy(
                        src_ref=g.part_of_halves(s_ref, k), dst_ref=r_ref.at[j], send_sem=send_sems.at[3 * i + j],
                        recv_sem=recv_sems.at[3 * i + j], device_id=(x, y, c), device_id_type=MESH).wait_recv()
            for cp in sends:
                cp.wait_send()

        _on_each_place(x, y, c, at_place, by_core=False)

    return _on_sequencer(body, sums, [jax.ShapeDtypeStruct((len(FLIPS),) + g.half, WIRE_DTYPE) for g in geoms],
                         len(FLIPS) * len(sums), _same_core_of_other_chips, name, collective_id)


def _share_reduced_halves(reduced, geoms, name, collective_id):
    def body(out_refs, _, send_sems, recv_sems, x, y, c):

        def at_place(_, cc):
            sends = []
            for i, (g, ref) in enumerate(zip(geoms, out_refs)):
                mine = g.half_of_shard(ref, cc)
                cp = pltpu.make_async_remote_copy(src_ref=mine, dst_ref=mine, send_sem=send_sems.at[i],
                                                  recv_sem=recv_sems.at[i], device_id=(x, y, 1 - cc), device_id_type=MESH)
                cp.start()
                sends.append(cp)
            for i, (g, ref) in enumerate(zip(geoms, out_refs)):
                theirs = g.half_of_shard(ref, 1 - cc)
                pltpu.make_async_remote_copy(src_ref=theirs, dst_ref=theirs, send_sem=send_sems.at[i],
                                             recv_sem=recv_sems.at[i], device_id=(x, y, cc), device_id_type=MESH).wait_recv()
            for cp in sends:
                cp.wait_send()

        _on_each_place(x, y, c, at_place, by_chip=False)

    return _on_sequencer(body, reduced, [], len(reduced), _sibling, name, collective_id, return_inputs=True)


def _chip_sum(place, grad, theirs, g, name):
    RH, C = theirs.shape
    h = g.half_rows
    tr = _tile(h, 256, 16)
    tc = _tile(C, 2048)
    per_half = h // tr

    if g.by_cols:
        grad_map = lambda i, j, p: (p[1] * per_half + i, j)
    else:
        grad_map = lambda i, j, p: ((i // per_half) * 2 * per_half + p[1] * per_half + i % per_half, j)

    def body(p_ref, a_ref, b_ref, f_ref, o_ref):
        total = a_ref[...] + b_ref[...]
        f_ref[...] = total
        o_ref[...] = total.astype(o_ref.dtype)

    return pl.pallas_call(
        body, name=name,
        grid_spec=pltpu.PrefetchScalarGridSpec(
            num_scalar_prefetch=1, grid=(RH // tr, C // tc),
            in_specs=[pl.BlockSpec((tr, tc), grad_map), pl.BlockSpec((tr, tc), lambda i, j, p: (i, j))],
            out_specs=[pl.BlockSpec((tr, tc), lambda i, j, p: (i, j))] * 2),
        out_shape=[jax.ShapeDtypeStruct((RH, C), F32), jax.ShapeDtypeStruct((RH, C), WIRE_DTYPE)],
        compiler_params=_params("parallel", "parallel"),
    )(place, grad, theirs)


def _dw_half(place, a, b, g, mine, name, add=None):
    K, R = a.shape
    C = b.shape[1]
    h = g.half_rows
    tm, tn = _tile(h, 1024, 16), _tile(C, 512)
    per_half = h // tm
    n_i = (R // 2) // tm

    def a_map(i, j, p):
        hc = p[1] if mine else 1 - p[1]
        if g.by_cols:
            return 0, hc * n_i + i
        return 0, (i // per_half) * 2 * per_half + hc * per_half + i % per_half

    mn_spec = pl.BlockSpec((tm, tn), lambda i, j, p: (i, j))

    def body(p_ref, a_ref, b_ref, *rest):
        acc = _dot(a_ref[...], b_ref[...], TN)
        if add is None:
            rest[0][...] = acc
        else:
            total = acc + rest[0][...]
            rest[1][...] = total
            rest[2][...] = total.astype(rest[2].dtype)

    out_shape = [jax.ShapeDtypeStruct((R // 2, C), F32)] + ([] if add is None else [jax.ShapeDtypeStruct((R // 2, C), WIRE_DTYPE)])
    return pl.pallas_call(
        body, name=name,
        grid_spec=pltpu.PrefetchScalarGridSpec(
            num_scalar_prefetch=1, grid=(n_i, C // tn),
            in_specs=[pl.BlockSpec((K, tm), a_map), pl.BlockSpec((K, tn), lambda i, j, p: (0, j))] + ([] if add is None else [mn_spec]),
            out_specs=[mn_spec] * len(out_shape)),
        out_shape=out_shape,
        compiler_params=_params("parallel", "arbitrary"),
    )(place, a, b, *([] if add is None else [add]))


def _reduce_half(place, sums, others, g, name):
    h, tc = g.half
    tr = _tile(h, 256, 16)
    per_half = h // tr
    sums_map = (lambda i, p: (i, p[0])) if g.by_cols else (lambda i, p: (p[0] * per_half + i, 0))

    def body(p_ref, s_ref, o0_ref, o1_ref, o2_ref, out_ref):
        acc = s_ref[...]
        for o_ref in (o0_ref, o1_ref, o2_ref):
            acc = acc + o_ref[...].astype(F32)
        out_ref[...] = acc

    other_specs = [pl.BlockSpec((None, tr, tc), functools.partial(lambda i, p, j: (j, i, 0), j=j)) for j in range(len(FLIPS))]
    return pl.pallas_call(
        body, name=name,
        grid_spec=pltpu.PrefetchScalarGridSpec(
            num_scalar_prefetch=1, grid=(per_half,),
            in_specs=[pl.BlockSpec((tr, tc), sums_map)] + other_specs,
            out_specs=pl.BlockSpec((tr, tc), lambda i, p: (p[1] * per_half + i, 0))),
        out_shape=jax.ShapeDtypeStruct(g.shard, F32),
        compiler_params=_params("arbitrary"),
    )(place, sums, others, others, others)


SLAB_ROW_UNIT = 256
SMALL = ("b_ada", "norm1_g", "v_norm_g", "w_spatial", "b_spatial", "out_norm_g", "norm2_g", "final_g")
BIG = ("w_in", "w_out", "w_gate", "w_up", "w_down")
BY_COLS = {"w_in": True, "w_out": False, "w_gate": True, "w_up": True, "w_down": False}
ORDER = ("w_ada", "b_ada", "norm1_g", "w_in", "v_norm_g", "w_spatial", "b_spatial", "out_norm_g", "w_out",
         "norm2_g", "w_gate", "w_up", "w_down", "final_g")


def _pack(parts):
    return jnp.concatenate([parts[n].reshape(-1) for n in SMALL]).reshape(-1, LANE)


def _adamw_small(w, g, m, v, shapes, name):
    R = w.shape[0]
    slab_spec = pl.BlockSpec((R, LANE), lambda: (0, 0))
    out_shapes = [shapes[n] if len(shapes[n]) > 1 else (1,) + tuple(shapes[n]) for n in SMALL]

    def body(w_ref, g_ref, m_ref, v_ref, *out_refs):
        gv = g_ref[...]
        results = (gv,) + _adamw_math(w_ref[...], gv, m_ref[...], v_ref[...])
        for kind, val in enumerate(results):
            at = 0
            for i, shp in enumerate(out_shapes):
                o_ref = out_refs[kind * len(SMALL) + i]
                n_rows = math.prod(shp) // LANE
                if len(shp) == 2:
                    for r in range(n_rows):
                        o_ref[:, r * LANE:(r + 1) * LANE] = val[at + r:at + r + 1, :]
                else:
                    o_ref[0] = val[at:at + n_rows, :].reshape(shp[1:])
                at += n_rows

    outs = pl.pallas_call(
        body, name=name, in_specs=[slab_spec] * 4,
        out_specs=[pl.BlockSpec(shp, functools.partial(lambda nd: (0,) * nd, len(shp))) for shp in out_shapes] * 4,
        out_shape=[jax.ShapeDtypeStruct(shp, F32) for shp in out_shapes] * 4,
        compiler_params=_params(),
    )(w, g, m, v)
    dicts = []
    for kind in range(4):
        part = outs[kind * len(SMALL):(kind + 1) * len(SMALL)]
        dicts.append({n: a.reshape(shapes[n]) for n, a in zip(SMALL, part)})
    return dicts


def kernel(x, c, w_ada, b_ada, norm1_g, w_in, v_norm_g, w_spatial, b_spatial, out_norm_g, w_out, norm2_g, w_gate, w_up, w_down, final_g, loss_target, m_w_ada, m_b_ada, m_norm1_g, m_w_in, m_v_norm_g, m_w_spatial, m_b_spatial, m_out_norm_g, m_w_out, m_norm2_g, m_w_gate, m_w_up, m_w_down, m_final_g, v_w_ada, v_b_ada, v_norm1_g, v_w_in, v_v_norm_g, v_w_spatial, v_b_spatial, v_out_norm_g, v_w_out, v_norm2_g, v_w_gate, v_w_up, v_w_down, v_final_g):
    weights = dict(w_ada=w_ada, b_ada=b_ada, norm1_g=norm1_g, w_in=w_in, v_norm_g=v_norm_g, w_spatial=w_spatial,
                   b_spatial=b_spatial, out_norm_g=out_norm_g, w_out=w_out, norm2_g=norm2_g, w_gate=w_gate, w_up=w_up,
                   w_down=w_down, final_g=final_g)
    m_in = dict(w_ada=m_w_ada, b_ada=m_b_ada, norm1_g=m_norm1_g, w_in=m_w_in, v_norm_g=m_v_norm_g, w_spatial=m_w_spatial,
                b_spatial=m_b_spatial, out_norm_g=m_out_norm_g, w_out=m_w_out, norm2_g=m_norm2_g, w_gate=m_w_gate,
                w_up=m_w_up, w_down=m_w_down, final_g=m_final_g)
    v_in = dict(w_ada=v_w_ada, b_ada=v_b_ada, norm1_g=v_norm1_g, w_in=v_w_in, v_norm_g=v_v_norm_g, w_spatial=v_w_spatial,
                b_spatial=v_b_spatial, out_norm_g=v_out_norm_g, w_out=v_w_out, norm2_g=v_norm2_g, w_gate=v_w_gate,
                w_up=v_w_up, w_down=v_w_down, final_g=v_final_g)

    S, D = x.shape[1], x.shape[2]
    n_g = v_norm_g.shape[-1] // LANE
    n_h = (D - n_g * LANE) // LANE
    GW = n_g * LANE
    xi, yi, ci = _place()
    chip = 2 * xi + yi
    me = 4 * xi + 2 * yi + ci
    place = jnp.stack([chip, ci]).astype(jnp.int32)

    xs, target = x[0], loss_target[0]
    geoms = [_Sharded(weights[n].shape[1:], BY_COLS[n]) for n in BIG]

    full = {}
    for i, group in enumerate((("w_in",), ("w_out",), ("w_gate", "w_up"), ("w_down",))):
        gg = [geoms[BIG.index(n)] for n in group]
        own = [_cast_into_full(place, weights[n][0], g, "cast_" + n) for n, g in zip(group, gg)]
        gathered = _gather_weights(own, gg, "gather_" + "_".join(group), 1 + i)
        full.update(zip(group, gathered))

    c_pad = jnp.concatenate([c, jnp.zeros((7, D), F32)], axis=0)
    c_all = _allgather8(c_pad, "gather_c")[::8]
    n_ada = w_ada.shape[2]
    b_cols = lax.dynamic_slice(b_ada, (0, chip * n_ada), (1, n_ada))
    mod_parts = _allgather8(_mod_part(c_all, w_ada[0], b_cols, "mod_part"), "gather_mod")
    mod_all = mod_parts.reshape(N_CHIPS, 2, 8, n_ada)[:, 0].transpose(1, 0, 2).reshape(8, N_CHIPS * n_ada)
    mod = lax.dynamic_slice(mod_all, (me, 0), (1, 6 * D))
    shift1, scale1, gate1, shift2, scale2, gate2 = [mod[:, i * D:(i + 1) * D] for i in range(6)]

    b_t = b_spatial[0].T
    h1 = _norm_mod(xs, norm1_g, scale1, shift1, "norm1")
    proj, = _mm("nn", h1, full["w_in"], [F32], "proj")
    on_gm = _gmlp_fwd(proj, v_norm_g, w_spatial[0], b_t, out_norm_g, n_g, "gmlp_fwd")
    o_sb, o_n, l_sum = _sb_fwd(proj, out_norm_g, on_gm, n_g, n_h, "sb_fwd")
    attn, = _mm("nn", o_n, full["w_out"], [F32], "attn_out")
    x1, h2 = _residual_norm_mod(xs, attn, gate1, norm2_g, scale2, shift2, "norm2")
    a_g, a_u, f_in = _gate_up(h2, full["w_gate"], full["w_up"], "gate_up")
    f, = _mm("nn", f_in, full["w_down"], [F32], "down", tm=1024)
    dx2, df, d_gate2, d_final_g, loss_part = _final_loss_bwd(x1, f, gate2, final_g.reshape(1, D), target, "final")

    geom_of = dict(zip(BIG, geoms))
    grad_out, delta, new_m, new_v = {}, {}, {}, {}

    def theirs_first(group, operands, collective_id, after=None):
        outs = []
        for n, (a_op, b_op) in zip(group, operands):
            outs.append(_dw_half(place, a_op, b_op if after is None else _then(after, b_op), geom_of[n], False, "d_" + n + "_theirs")[0])
            after = outs[-1]
        return outs, _send_to_sibling(outs, "swap_" + "_".join(group), collective_id)

    def chip_sums(group, operands, theirs, after):
        f32s, wires = [], []
        for n, (a_op, b_op), t in zip(group, operands, theirs):
            sf, sw = _dw_half(place, a_op, b_op, geom_of[n], True, "d_" + n + "_mine", add=_then(after, t))
            f32s.append(sf)
            wires.append(sw)
            after = sw
        return f32s, wires

    def scatter(group, sums, collective_id):
        return _scatter_chip_sums(sums, [geom_of[n] for n in group], "scatter_" + "_".join(group), collective_id)

    def reduce_halves(group, sums, others, after):
        return [_reduce_half(place, sf, _then(after, o), geom_of[n], "reduce_" + n) for n, sf, o in zip(group, sums, others)]

    def share(group, halves, collective_id):
        return _share_reduced_halves(halves, [geom_of[n] for n in group], "share_" + "_".join(group), collective_id)

    def adamw(group, reduced, after):
        for n, r in zip(group, reduced):
            go, d, mo, vo = _adamw(weights[n][0], _then(after, r), m_in[n][0], v_in[n][0], "adamw_" + n)
            grad_out[n], delta[n], new_m[n], new_v[n] = go[None], d[None], mo[None], vo[None]
        return d

    g_down = ("w_down",)
    g_ffn = ("w_gate", "w_up")
    g_out = ("w_out",)
    g_in = ("w_in",)

    gr_down, = _mm("tn", f_in, df, [F32], "d_w_down", tm=1408, tn=1024)
    th_down, = _swap_core_halves([gr_down], [geom_of["w_down"]], "swap_w_down", 6)
    d_ag, d_au = _mm("nt", df, full["w_down"], [MXU_DTYPE, MXU_DTYPE], "d_ffn_in", extras=(a_g, a_u),
                     epilogue=_swiglu_bwd_epilogue)
    sf_down, sw_down = [[t] for t in _chip_sum(place, gr_down, _then(d_ag, th_down), geom_of["w_down"], "chip_sum_w_down")]
    ot_down = scatter(g_down, sw_down, 7)
    sent, th_ffn = theirs_first(g_ffn, [(h2, d_ag), (h2, d_au)], 9, after=sw_down)
    dh2 = _mm_ktiled("nt", [(_then(sent, d_ag), full["w_gate"]), (d_au, full["w_up"])], "d_h2", tn=512)
    sf_ffn, sw_ffn = chip_sums(g_ffn, [(h2, d_ag), (h2, d_au)], th_ffn, after=dh2)
    ot_ffn = scatter(g_ffn, sw_ffn, 10)
    hv_down = reduce_halves(g_down, sf_down, ot_down, after=sw_ffn)
    rd_down = share(g_down, hv_down, 8)
    dx1, d_shift2, d_scale2, d_norm2_g, d_gate1, d_attn = _norm_mod_bwd(
        _then(hv_down, dh2), x1, dx2, norm2_g, scale2, "norm2_bwd", branch=attn, gate=gate1)
    gr_out, = _mm("tn", o_n, d_attn, [F32], "d_w_out")
    th_out, = _swap_core_halves([gr_out], [geom_of["w_out"]], "swap_w_out", 12)
    d_on, = _mm("nt", _then(gr_out, d_attn), full["w_out"], [F32], "d_o")
    sf_out, sw_out = [[t] for t in _chip_sum(place, gr_out, _then(d_on, th_out), geom_of["w_out"], "chip_sum_w_out")]
    ot_out = scatter(g_out, sw_out, 13)
    dq, dk, dv, d_og_sb = _sb_bwd(proj, o_sb, l_sum, _then(sw_out, d_on), out_norm_g, n_g, n_h, "sb_bwd")
    dproj, d_w_spatial, d_b_t, d_v_norm_g, d_og_gm = _gmlp_bwd(proj, d_on, v_norm_g, w_spatial[0], b_t, out_norm_g,
                                                                (dq, dk, dv), n_g, "gmlp_bwd")
    sent, th_in = theirs_first(g_in, [(h1, dproj)], 15)
    hv_ffn = reduce_halves(g_ffn, sf_ffn, ot_ffn, after=sent)
    rd_ffn = share(g_ffn, hv_ffn, 11)
    dh1, = _mm("nt", _then(sent, dproj), full["w_in"], [F32], "d_h1", tm=1024)
    hv_out = reduce_halves(g_out, sf_out, ot_out, after=dh1)
    rd_out = share(g_out, hv_out, 14)
    grad_x, d_shift1, d_scale1, d_norm1_g = _norm_mod_bwd(_then(hv_out, dh1), xs, dx1, norm1_g, scale1, "norm1_bwd")

    small_parts = [d_shift1, d_scale1, d_gate1, d_shift2, d_scale2, d_gate2,
                   d_norm1_g, d_v_norm_g, d_w_spatial, d_b_t.T, d_og_gm, d_og_sb, d_norm2_g, d_final_g]
    small_rows = sum(math.prod(p.shape) for p in small_parts) // LANE
    rows = -(-(small_rows + 8) // SLAB_ROW_UNIT) * SLAB_ROW_UNIT
    slab = _place_slab(place, small_parts, loss_part, rows, "place_slab")
    gathered = _allgather8_on_sequencer(slab, rows, "gather_small", 18)
    sf_in, sw_in = chip_sums(g_in, [(h1, dproj)], th_in, after=slab)
    ot_in = scatter(g_in, sw_in, 16)
    done = adamw(g_down, rd_down, after=sw_in)
    done = adamw(g_ffn, rd_ffn, after=done)
    done = adamw(g_out, rd_out, after=done)
    gathered = _then(done, gathered)
    small_shapes = {n: weights[n].shape for n in SMALL}
    slab_sum = _sum_devices(gathered, 8, "sum_small")
    small_sum, loss = slab_sum[:small_rows], slab_sum[small_rows, 0]
    ada_rows = n_ada // LANE
    dmod_cols = lax.dynamic_slice(gathered.reshape(8, rows, LANE), (0, chip * ada_rows, 0), (8, ada_rows, LANE))
    dmod_cols = dmod_cols.reshape(8, n_ada)
    g_ada, d, mo, vo = _adamw_ada(c_all, dmod_cols, w_ada[0], m_w_ada[0], v_w_ada[0], "adamw_w_ada")
    grad_out["w_ada"], delta["w_ada"], new_m["w_ada"], new_v["w_ada"] = g_ada[None], d[None], mo[None], vo[None]
    small_out = _adamw_small(_pack({n: weights[n] for n in SMALL}), small_sum, _pack({n: m_in[n] for n in SMALL}),
                             _pack({n: v_in[n] for n in SMALL}), small_shapes, "adamw_small")
    for dst, part in zip((grad_out, delta, new_m, new_v), small_out):
        dst.update(part)
    hv_in = reduce_halves(g_in, sf_in, ot_in, after=d)
    adamw(g_in, share(g_in, hv_in, 17), after=d)

    return (loss, grad_x[None], *[grad_out[n] for n in ORDER], *[delta[n] for n in ORDER],
            *[new_m[n] for n in ORDER], *[new_v[n] for n in ORDER])
```

```python
import functools
import math

import jax
import jax.numpy as jnp
from jax import lax
from jax.experimental import pallas as pl
from jax.experimental.pallas import tpu as pltpu
from jax.experimental.pallas import tpu_sc as plsc

F32 = jnp.float32
MXU_DTYPE = jnp.bfloat16
WIRE_DTYPE = jnp.bfloat16
EPS = 1e-6
LANE = 128
V7X_VMEM_LIMIT = 56 * 1024 * 1024
MESH = pl.DeviceIdType.MESH
N_CHIPS = 4
FLIPS = (2, 1, 3)

ADAM_LR = 0.001
ADAM_B1 = 0.9
ADAM_B2 = 0.999
ADAM_EPS = 1e-08
ADAM_WD = 0.01
ADAM_STEP = 10


def _params(*semantics):
    return pltpu.CompilerParams(dimension_semantics=semantics or None, vmem_limit_bytes=V7X_VMEM_LIMIT)


def _tile(dim, pref, unit=LANE):
    best = None
    t = unit
    while t <= min(dim, pref):
        if dim % t == 0:
            best = t
        t += unit
    return best if best is not None else dim


def _then(first, second):
    return lax.optimization_barrier((first, second))[1]


def _sum0(v):
    return jnp.sum(v, axis=0, keepdims=True)


def _mean1(v):
    return jnp.mean(v, axis=-1, keepdims=True)


def _gelu(x):
    return 0.5 * x * (1.0 + lax.erf(x * (1.0 / math.sqrt(2.0))))


def _gelu_grad(x):
    cdf = 0.5 * (1.0 + lax.erf(x * (1.0 / math.sqrt(2.0))))
    return cdf + x * jnp.exp(-0.5 * x * x) * (1.0 / math.sqrt(2.0 * math.pi))


def _dot(a, b, dims):
    return lax.dot_general(a, b, (dims, ((), ())), preferred_element_type=F32)


NN = ((1,), (0,))
NT = ((1,), (1,))
TN = ((0,), (0,))


def _mm(kind, a, b, out_dtypes, name, tm=2048, tn=512, extras=(), epilogue=None):
    if kind == "nn":
        (M, K), N = a.shape, b.shape[1]
    elif kind == "nt":
        (M, K), N = a.shape, b.shape[0]
    else:
        (K, M), N = a.shape, b.shape[1]
    tm, tn = _tile(M, tm), _tile(N, tn)
    a_spec = pl.BlockSpec((K, tm), lambda i, j: (0, i)) if kind == "tn" else pl.BlockSpec((tm, K), lambda i, j: (i, 0))
    b_spec = pl.BlockSpec((tn, K), lambda i, j: (j, 0)) if kind == "nt" else pl.BlockSpec((K, tn), lambda i, j: (0, j))
    mn_spec = pl.BlockSpec((tm, tn), lambda i, j: (i, j))
    dims = {"nn": NN, "nt": NT, "tn": TN}[kind]
    n_extra = len(extras)

    n_chunks = 1 if epilogue is None or kind == "tn" else max(1, tm // 256)
    rows_per = tm // n_chunks

    def body(a_ref, b_ref, *rest):
        for r in range(n_chunks):
            rows = slice(r * rows_per, (r + 1) * rows_per)
            acc = _dot(a_ref[...] if n_chunks == 1 else a_ref[rows, :], b_ref[...], dims)
            res = (acc,) if epilogue is None else epilogue(acc, *[e[rows, :] for e in rest[:n_extra]])
            for o_ref, val in zip(rest[n_extra:], res):
                o_ref[rows, :] = val.astype(o_ref.dtype)

    outs = pl.pallas_call(
        body, name=name, grid=(M // tm, N // tn),
        in_specs=[a_spec, b_spec] + [mn_spec] * n_extra,
        out_specs=[mn_spec] * len(out_dtypes),
        out_shape=[jax.ShapeDtypeStruct((M, N), d) for d in out_dtypes],
        compiler_params=_params("parallel", "arbitrary"),
    )(a, b, *extras)
    return outs


def _mm_ktiled(kind, pairs, name, tm=2048, tn=1024, tk=1408):
    a0, b0 = pairs[0]
    M, K = a0.shape
    N = b0.shape[1] if kind == "nn" else b0.shape[0]
    tm, tn, tk = _tile(M, tm), _tile(N, tn), _tile(K, tk)
    a_spec = pl.BlockSpec((tm, tk), lambda i, j, k: (i, k))
    b_spec = pl.BlockSpec((tk, tn), lambda i, j, k: (k, j)) if kind == "nn" else pl.BlockSpec((tn, tk), lambda i, j, k: (j, k))
    dims = NN if kind == "nn" else NT
    n_pairs = len(pairs)

    def body(*refs):
        o_ref = refs[2 * n_pairs]
        acc = _dot(refs[0][...], refs[1][...], dims)
        for p in range(1, n_pairs):
            acc = acc + _dot(refs[2 * p][...], refs[2 * p + 1][...], dims)

        @pl.when(pl.program_id(2) == 0)
        def _():
            o_ref[...] = acc

        @pl.when(pl.program_id(2) != 0)
        def _():
            o_ref[...] += acc

    return pl.pallas_call(
        body, name=name, grid=(M // tm, N // tn, K // tk),
        in_specs=[a_spec, b_spec] * n_pairs,
        out_specs=pl.BlockSpec((tm, tn), lambda i, j, k: (i, j)),
        out_shape=jax.ShapeDtypeStruct((M, N), F32),
        compiler_params=_params("parallel", "parallel", "arbitrary"),
    )(*[x for pair in pairs for x in pair])


def _gate_up(h, wg, wu, name):
    (M, K), N = h.shape, wg.shape[1]
    tm, tn = _tile(M, 2048), _tile(N, 512)

    n_chunks = max(1, tm // 256)
    rows_per = tm // n_chunks

    def body(h_ref, wg_ref, wu_ref, ag_ref, au_ref, f_ref):
        for r in range(n_chunks):
            rows = slice(r * rows_per, (r + 1) * rows_per)
            hv = h_ref[rows, :]
            ag = _dot(hv, wg_ref[...], NN)
            au = _dot(hv, wu_ref[...], NN)
            ag_ref[rows, :] = ag.astype(ag_ref.dtype)
            au_ref[rows, :] = au.astype(au_ref.dtype)
            f_ref[rows, :] = (ag * jax.nn.sigmoid(ag) * au).astype(f_ref.dtype)

    w_spec = pl.BlockSpec((K, tn), lambda i, j: (0, j))
    mn_spec = pl.BlockSpec((tm, tn), lambda i, j: (i, j))
    return pl.pallas_call(
        body, name=name, grid=(M // tm, N // tn),
        in_specs=[pl.BlockSpec((tm, K), lambda i, j: (i, 0)), w_spec, w_spec],
        out_specs=[mn_spec] * 3,
        out_shape=[jax.ShapeDtypeStruct((M, N), MXU_DTYPE)] * 3,
        compiler_params=_params("parallel", "arbitrary"),
    )(h, wg, wu)


def _swiglu_bwd_epilogue(dfin, ag, au):
    ag, au = ag.astype(F32), au.astype(F32)
    sg = jax.nn.sigmoid(ag)
    d_au = dfin * (ag * sg)
    d_ag = dfin * au * (sg * (1.0 + ag * (1.0 - sg)))
    return d_ag, d_au


def _row_specs(ts, width):
    return pl.BlockSpec((ts, width), lambda i: (i, 0)), pl.BlockSpec((1, width), lambda i: (0, 0))


def _cast_into_full(place, shard, g, name):
    R, C = shard.shape
    tr = _tile(R, 256, 16)
    n_blk = R // tr
    out_map = (lambda i, p: (i, p[0])) if g.by_cols else (lambda i, p: (p[0] * n_blk + i, 0))

    def body(p_ref, a_ref, o_ref):
        o_ref[...] = a_ref[...].astype(o_ref.dtype)

    return pl.pallas_call(
        body, name=name,
        grid_spec=pltpu.PrefetchScalarGridSpec(
            num_scalar_prefetch=1, grid=(n_blk,),
            in_specs=[pl.BlockSpec((tr, C), lambda i, p: (i, 0))],
            out_specs=pl.BlockSpec((tr, C), out_map)),
        out_shape=jax.ShapeDtypeStruct(g.full, WIRE_DTYPE),
        compiler_params=_params("arbitrary"),
    )(place, shard)


def _norm_mod(x, g, scale, shift, name):
    S, D = x.shape
    ts = _tile(S, 256, 16)
    tile, vec = _row_specs(ts, D)

    def body(x_ref, g_ref, sc_ref, sh_ref, h_ref):
        xv = x_ref[...]
        r = lax.rsqrt(_mean1(xv * xv) + EPS)
        h_ref[...] = ((xv * r) * g_ref[...] * (1.0 + sc_ref[...]) + sh_ref[...]).astype(h_ref.dtype)

    return pl.pallas_call(body, name=name, grid=(S // ts,), in_specs=[tile, vec, vec, vec], out_specs=tile,
                          out_shape=jax.ShapeDtypeStruct((S, D), MXU_DTYPE), compiler_params=_params("parallel"))(x, g, scale, shift)


def _residual_norm_mod(x, attn, gate, g, scale, shift, name):
    S, D = x.shape
    ts = _tile(S, 256, 16)
    tile, vec = _row_specs(ts, D)

    def body(x_ref, a_ref, gate_ref, g_ref, sc_ref, sh_ref, x1_ref, h_ref):
        x1 = x_ref[...] + gate_ref[...] * a_ref[...]
        x1_ref[...] = x1
        r = lax.rsqrt(_mean1(x1 * x1) + EPS)
        h_ref[...] = ((x1 * r) * g_ref[...] * (1.0 + sc_ref[...]) + sh_ref[...]).astype(h_ref.dtype)

    return pl.pallas_call(body, name=name, grid=(S // ts,), in_specs=[tile, tile, vec, vec, vec, vec],
                          out_specs=[tile, tile],
                          out_shape=[jax.ShapeDtypeStruct((S, D), F32), jax.ShapeDtypeStruct((S, D), MXU_DTYPE)],
                          compiler_params=_params("parallel"))(x, attn, gate, g, scale, shift)


def _final_loss_bwd(x1, f, gate2, final_g, target, name):
    S, D = x1.shape
    ts = _tile(S, 256, 16)
    tile, vec = _row_specs(ts, D)
    loss_spec = pl.BlockSpec((1, LANE), lambda i: (0, 0))

    def body(x1_ref, f_ref, gate_ref, g_ref, t_ref, dx2_ref, df_ref, dgate_ref, dg_ref, loss_ref):
        @pl.when(pl.program_id(0) == 0)
        def _():
            dgate_ref[...] = jnp.zeros_like(dgate_ref)
            dg_ref[...] = jnp.zeros_like(dg_ref)
            loss_ref[...] = jnp.zeros_like(loss_ref)

        fv, gate, g = f_ref[...], gate_ref[...], g_ref[...]
        x2 = x1_ref[...] + gate * fv
        r = lax.rsqrt(_mean1(x2 * x2) + EPS)
        xn = x2 * r
        err = xn * g - t_ref[...]
        loss_ref[...] += jnp.broadcast_to(0.5 * _sum0(_mean1(err * err)), loss_ref.shape)
        dy = err * (1.0 / D)
        dg_ref[...] += _sum0(dy * xn)
        dxn = dy * g
        dx2 = r * (dxn - xn * _mean1(dxn * xn))
        dx2_ref[...] = dx2
        dgate_ref[...] += _sum0(dx2 * fv)
        df_ref[...] = (dx2 * gate).astype(df_ref.dtype)

    return pl.pallas_call(
        body, name=name, grid=(S // ts,), in_specs=[tile, tile, vec, vec, tile],
        out_specs=[tile, tile, vec, vec, loss_spec],
        out_shape=[jax.ShapeDtypeStruct((S, D), F32), jax.ShapeDtypeStruct((S, D), MXU_DTYPE),
                   jax.ShapeDtypeStruct((1, D), F32), jax.ShapeDtypeStruct((1, D), F32),
                   jax.ShapeDtypeStruct((1, LANE), F32)],
        compiler_params=_params("arbitrary"),
    )(x1, f, gate2, final_g, target)


def _norm_mod_bwd(dh, xin, dres, g, scale, name, branch=None, gate=None):
    S, D = xin.shape
    ts = _tile(S, 256, 16)
    tile, vec = _row_specs(ts, D)
    with_gate = branch is not None

    def body(*refs):
        if with_gate:
            dh_ref, x_ref, dres_ref, g_ref, sc_ref, br_ref, gate_ref, dx_ref, dshift_ref, dscale_ref, dg_ref, dgate_ref, dbr_ref = refs
            accs = (dshift_ref, dscale_ref, dg_ref, dgate_ref)
        else:
            dh_ref, x_ref, dres_ref, g_ref, sc_ref, dx_ref, dshift_ref, dscale_ref, dg_ref = refs
            accs = (dshift_ref, dscale_ref, dg_ref)

        @pl.when(pl.program_id(0) == 0)
        def _():
            for acc in accs:
                acc[...] = jnp.zeros_like(acc)

        dh_v, xv, g_v = dh_ref[...], x_ref[...], g_ref[...]
        one_sc = 1.0 + sc_ref[...]
        r = lax.rsqrt(_mean1(xv * xv) + EPS)
        xn = xv * r
        dshift_ref[...] += _sum0(dh_v)
        dscale_ref[...] += _sum0(dh_v * (xn * g_v))
        dg_ref[...] += _sum0(dh_v * one_sc * xn)
        dxn = dh_v * (g_v * one_sc)
        dx = dres_ref[...] + r * (dxn - xn * _mean1(dxn * xn))
        dx_ref[...] = dx
        if with_gate:
            dgate_ref[...] += _sum0(dx * br_ref[...])
            dbr_ref[...] = (dx * gate_ref[...]).astype(dbr_ref.dtype)

    ins = [dh, xin, dres, g, scale] + ([branch, gate] if with_gate else [])
    in_specs = [tile, tile, tile, vec, vec] + ([tile, vec] if with_gate else [])
    out_specs = [tile, vec, vec, vec] + ([vec, tile] if with_gate else [])
    out_shape = [jax.ShapeDtypeStruct((S, D), F32)] + [jax.ShapeDtypeStruct((1, D), F32)] * 3
    if with_gate:
        out_shape += [jax.ShapeDtypeStruct((1, D), F32), jax.ShapeDtypeStruct((S, D), MXU_DTYPE)]
    return pl.pallas_call(body, name=name, grid=(S // ts,), in_specs=in_specs, out_specs=out_specs,
                          out_shape=out_shape, compiler_params=_params("arbitrary"))(*ins)


def _causal_weights(ws_ref, wt_ref, n_g):
    row = lax.broadcasted_iota(jnp.int32, (LANE, LANE), 0)
    col = lax.broadcasted_iota(jnp.int32, (LANE, LANE), 1)
    for g in range(n_g):
        wt_ref[g] = jnp.where(col <= row, ws_ref[g], 0.0).astype(wt_ref.dtype)


def _group_layernorm(v):
    xc = v - _mean1(v)
    rstd = lax.rsqrt(_mean1(xc * xc) + EPS)
    return xc * rstd, rstd


def _gmlp_fwd(proj, v_gain, w_s, b_t, out_gain, n_g, name):
    S = proj.shape[0]
    GW = n_g * LANE
    D = out_gain.shape[1]

    def body(p_ref, vg_ref, ws_ref, bt_ref, og_ref, on_ref, wt_ref):
        @pl.when(pl.program_id(0) == 0)
        def _():
            _causal_weights(ws_ref, wt_ref, n_g)

        for g in range(n_g):
            cols = slice(g * LANE, (g + 1) * LANE)
            u = _gelu(p_ref[:, cols])
            v = _gelu(p_ref[:, GW + g * LANE:GW + (g + 1) * LANE])
            vhat, _ = _group_layernorm(v)
            vln = (vhat * vg_ref[:, cols]).astype(MXU_DTYPE)
            mixed = _dot(wt_ref[g], vln, NN) + bt_ref[:, g:g + 1]
            o = u * mixed
            r = lax.rsqrt(_mean1(o * o) + EPS)
            on_ref[:, cols] = (o * r * og_ref[:, cols]).astype(on_ref.dtype)

    return pl.pallas_call(
        body, name=name, grid=(S // LANE,),
        in_specs=[pl.BlockSpec((LANE, 2 * GW), lambda n: (n, 0)),
                  pl.BlockSpec((1, GW), lambda n: (0, 0)),
                  pl.BlockSpec((n_g, LANE, LANE), lambda n: (0, 0, 0)),
                  pl.BlockSpec((LANE, n_g), lambda n: (0, 0)),
                  pl.BlockSpec((1, GW), lambda n: (0, 0))],
        out_specs=pl.BlockSpec((LANE, GW), lambda n: (n, 0)),
        out_shape=jax.ShapeDtypeStruct((S, D), MXU_DTYPE),
        scratch_shapes=[pltpu.VMEM((n_g, LANE, LANE), MXU_DTYPE)],
        compiler_params=_params("arbitrary"),
    )(proj, v_gain, w_s, b_t, out_gain)


def _gmlp_bwd(proj, d_on, v_gain, w_s, b_t, out_gain, dqkv, n_g, name):
    S, N_IN = proj.shape
    GW = n_g * LANE
    SBW = dqkv[0].shape[1]

    def body(p_ref, dn_ref, vg_ref, ws_ref, bt_ref, og_ref, dq_ref, dk_ref, dv_ref, dp_ref, dws_ref, dbt_ref, dvg_ref, dog_ref, wt_ref):
        for i, part_ref in enumerate((dq_ref, dk_ref, dv_ref)):
            dp_ref[:, 2 * GW + i * SBW:2 * GW + (i + 1) * SBW] = part_ref[...]

        @pl.when(pl.program_id(0) == 0)
        def _():
            _causal_weights(ws_ref, wt_ref, n_g)
            dws_ref[...] = jnp.zeros_like(dws_ref)
            dbt_ref[...] = jnp.zeros_like(dbt_ref)
            dvg_ref[...] = jnp.zeros_like(dvg_ref)
            dog_ref[...] = jnp.zeros_like(dog_ref)

        row = lax.broadcasted_iota(jnp.int32, (LANE, LANE), 0)
        col = lax.broadcasted_iota(jnp.int32, (LANE, LANE), 1)
        for g in range(n_g):
            cols = slice(g * LANE, (g + 1) * LANE)
            vcols = slice(GW + g * LANE, GW + (g + 1) * LANE)
            pu, pv = p_ref[:, cols], p_ref[:, vcols]
            u, v = _gelu(pu), _gelu(pv)
            vhat, rstd = _group_layernorm(v)
            gain = vg_ref[:, cols]
            vln = (vhat * gain).astype(MXU_DTYPE)
            mixed = _dot(wt_ref[g], vln, NN) + bt_ref[:, g:g + 1]
            o = u * mixed
            r = lax.rsqrt(_mean1(o * o) + EPS)
            oh = o * r
            dn = dn_ref[:, cols]
            dog_ref[:, cols] += _sum0(dn * oh)
            dhn = dn * og_ref[:, cols]
            d_o = r * (dhn - oh * _mean1(dhn * oh))
            du = d_o * mixed
            dmix = d_o * u
            dbt_ref[:, g:g + 1] += jnp.sum(dmix, axis=1, keepdims=True)
            dmix_b = dmix.astype(MXU_DTYPE)
            dws_ref[g] += jnp.where(col <= row, _dot(dmix_b, vln, NT), 0.0)
            dvln = _dot(wt_ref[g], dmix_b, TN)
            dvg_ref[:, cols] += _sum0(dvln * vhat)
            dxh = dvln * gain
            dv = rstd * (dxh - _mean1(dxh) - vhat * _mean1(dxh * vhat))
            dp_ref[:, cols] = (du * _gelu_grad(pu)).astype(dp_ref.dtype)
            dp_ref[:, vcols] = (dv * _gelu_grad(pv)).astype(dp_ref.dtype)

    return pl.pallas_call(
        body, name=name, grid=(S // LANE,),
        in_specs=[pl.BlockSpec((LANE, 2 * GW), lambda n: (n, 0)),
                  pl.BlockSpec((LANE, GW), lambda n: (n, 0)),
                  pl.BlockSpec((1, GW), lambda n: (0, 0)),
                  pl.BlockSpec((n_g, LANE, LANE), lambda n: (0, 0, 0)),
                  pl.BlockSpec((LANE, n_g), lambda n: (0, 0)),
                  pl.BlockSpec((1, GW), lambda n: (0, 0))] + [pl.BlockSpec((LANE, SBW), lambda n: (n, 0))] * 3,
        out_specs=[pl.BlockSpec((LANE, N_IN), lambda n: (n, 0)),
                   pl.BlockSpec((n_g, LANE, LANE), lambda n: (0, 0, 0)),
                   pl.BlockSpec((LANE, n_g), lambda n: (0, 0)),
                   pl.BlockSpec((1, GW), lambda n: (0, 0)),
                   pl.BlockSpec((1, GW), lambda n: (0, 0))],
        out_shape=[jax.ShapeDtypeStruct((S, N_IN), MXU_DTYPE),
                   jax.ShapeDtypeStruct((n_g, LANE, LANE), F32),
                   jax.ShapeDtypeStruct((LANE, n_g), F32),
                   jax.ShapeDtypeStruct((1, GW), F32),
                   jax.ShapeDtypeStruct((1, GW), F32)],
        scratch_shapes=[pltpu.VMEM((n_g, LANE, LANE), MXU_DTYPE)],
        compiler_params=_params("arbitrary"),
    )(proj, d_on, v_gain, w_s, b_t, out_gain, *dqkv)


def _tri_sum(v, tri, exact=True):
    hi = v.astype(MXU_DTYPE)
    if not exact:
        return _dot(hi, tri, NN)
    lo = (v - hi.astype(F32)).astype(MXU_DTYPE)
    return _dot(hi, tri, NN) + _dot(lo, tri, NN)


def _log_sigmoids(z):
    sp = jnp.log(1.0 + jnp.exp(-jnp.abs(z)))
    return jnp.minimum(z, 0.0) - sp, jnp.minimum(-z, 0.0) - sp


def _rows(i, size):
    return pl.ds(pl.multiple_of(i * size, size), size)


SB_QUERY_TILE = 2048
SB_KEY_TILE = 256


def _sb_tiles(S):
    tq = _tile(S, SB_QUERY_TILE)
    tk = _tile(tq, SB_KEY_TILE)
    assert (tq // tk) % 2 == 0, "the key sweep takes two blocks a pass"
    return tq, tk, S // tq, tq // tk


def _triangle(n, keep):
    row = lax.broadcasted_iota(jnp.int32, (n, n), 0)
    col = lax.broadcasted_iota(jnp.int32, (n, n), 1)
    return jnp.where(keep(row, col), 1.0, 0.0).astype(MXU_DTYPE)


def _strictly_before(tq, tk, key_offset):
    row = lax.broadcasted_iota(jnp.int32, (tq, tk), 0)
    col = lax.broadcasted_iota(jnp.int32, (tq, tk), 1)
    return col + key_offset < row


def _sb_specs(S, n_g, n_h):
    base = 2 * n_g
    q_spec = pl.BlockSpec((S, LANE), lambda h: (0, base + h))
    k_spec = pl.BlockSpec((S, LANE), lambda h: (0, base + n_h + h))
    v_spec = pl.BlockSpec((S, LANE), lambda h: (0, base + 2 * n_h + h))
    gain_spec = pl.BlockSpec((1, LANE), lambda h: (0, n_g + h))
    head_spec = pl.BlockSpec((S, LANE), lambda h: (0, h))
    return q_spec, k_spec, v_spec, gain_spec, head_spec


def _sb_fwd(proj, out_gain, on_buffer, n_g, n_h, name):
    S = proj.shape[0]
    TQ, TK, NQ, KPQ = _sb_tiles(S)
    scale = LANE ** -0.5
    q_spec, k_spec, v_spec, gain_spec, head_spec = _sb_specs(S, n_g, n_h)

    def body(q_ref, k_ref, v_ref, og_ref, _, o_ref, on_ref, ls_ref, qb, kb, vb):
        qb[...] = q_ref[...].astype(MXU_DTYPE)
        kb[...] = k_ref[...].astype(MXU_DTYPE)
        vb[...] = v_ref[...].astype(MXU_DTYPE)
        after = _triangle(TK, lambda r, c: r > c)

        def block(qi, j, ctail, acc, key_offset):
            skip = key_offset or 0
            z = _dot(qi[skip:], kb[_rows(j, TK), :], NT) * scale
            lb, l1m = _log_sigmoids(z)
            if key_offset is not None:
                strict = _strictly_before(TQ - skip, TK, 0)
                l1m = jnp.where(strict, l1m, 0.0)
            a = jnp.exp(lb + ctail[skip:] + _tri_sum(l1m, after))
            if key_offset is not None:
                a = jnp.where(strict, a, 0.0)
            acc_new = acc[skip:] + _dot(a.astype(MXU_DTYPE), vb[_rows(j, TK), :], NN)
            ctail_new = ctail[skip:] + jnp.sum(l1m, axis=1, keepdims=True)
            if skip:
                ctail_new = jnp.concatenate([ctail[:skip], ctail_new], axis=0)
                acc_new = jnp.concatenate([acc[:skip], acc_new], axis=0)
            return ctail_new, acc_new

        def q_loop(i, carry):
            qi = qb[_rows(i, TQ), :]
            state = (jnp.zeros((TQ, 1), F32), jnp.zeros((TQ, LANE), F32))
            for d in reversed(range(KPQ)):
                state = block(qi, i * KPQ + d, state[0], state[1], d * TK)
            def pair(jj, st):
                st = block(qi, i * KPQ - 1 - 2 * jj, st[0], st[1], None)
                return block(qi, i * KPQ - 2 - 2 * jj, st[0], st[1], None)

            ctail, acc = lax.fori_loop(0, i * (KPQ // 2), pair, state)
            ls_ref[_rows(i, TQ), :] = jnp.broadcast_to(ctail, (TQ, LANE))
            o_ref[_rows(i, TQ), :] = acc
            r = lax.rsqrt(_mean1(acc * acc) + EPS)
            on_ref[_rows(i, TQ), :] = (acc * r * og_ref[...]).astype(on_ref.dtype)
            return carry

        lax.fori_loop(0, NQ, q_loop, 0)

    return pl.pallas_call(
        body, name=name, grid=(n_h,),
        in_specs=[q_spec, k_spec, v_spec, gain_spec, pl.BlockSpec(memory_space=pl.ANY)],
        out_specs=[head_spec, pl.BlockSpec((S, LANE), lambda h: (0, n_g + h)), head_spec],
        out_shape=[jax.ShapeDtypeStruct((S, n_h * LANE), F32), jax.ShapeDtypeStruct(on_buffer.shape, MXU_DTYPE),
                   jax.ShapeDtypeStruct((S, n_h * LANE), F32)],
        input_output_aliases={4: 1},
        scratch_shapes=[pltpu.VMEM((S, LANE), MXU_DTYPE)] * 3,
        compiler_params=_params("parallel"),
    )(proj, proj, proj, out_gain, on_buffer)


def _sb_bwd(proj, o_sb, l_sum, d_on, out_gain, n_g, n_h, name):
    S = proj.shape[0]
    TQ, TK, NQ, KPQ = _sb_tiles(S)
    scale = LANE ** -0.5
    q_spec, k_spec, v_spec, gain_spec, head_spec = _sb_specs(S, n_g, n_h)
    dn_spec = pl.BlockSpec((S, LANE), lambda h: (0, n_g + h))
    dgain_spec = pl.BlockSpec((1, LANE), lambda h: (0, h))

    def body(q_ref, k_ref, v_ref, o_ref, ls_ref, dn_ref, og_ref, dq_ref, dk_ref, dv_ref, dog_ref,
             qb, kb, vb, dob, dk_acc, dv_acc):
        qb[...] = q_ref[...].astype(MXU_DTYPE)
        kb[...] = k_ref[...].astype(MXU_DTYPE)
        vb[...] = v_ref[...].astype(MXU_DTYPE)
        o, dn = o_ref[...], dn_ref[...]
        r = lax.rsqrt(_mean1(o * o) + EPS)
        oh = o * r
        dog_ref[...] = _sum0(dn * oh)
        dhn = dn * og_ref[...]
        dob[...] = (r * (dhn - oh * _mean1(dhn * oh))).astype(MXU_DTYPE)
        dk_acc[...] = jnp.zeros_like(dk_acc)
        dv_acc[...] = jnp.zeros_like(dv_acc)

        up_to = _triangle(TK, lambda r, c: r <= c)
        before = _triangle(TK, lambda r, c: r < c)

        def block(qi, doi, ltot, j, cl, cdl, dq, key_offset):
            skip = key_offset or 0
            q_in, do_in = qi[skip:], doi[skip:]
            kj, vj = kb[_rows(j, TK), :], vb[_rows(j, TK), :]
            z = _dot(q_in, kj, NT) * scale
            lb, l1m = _log_sigmoids(z)
            if key_offset is not None:
                strict = _strictly_before(TQ - skip, TK, 0)
                l1m = jnp.where(strict, l1m, 0.0)
            a = jnp.exp(lb + (ltot[skip:] - (cl[skip:] + _tri_sum(l1m, up_to))))
            if key_offset is not None:
                a = jnp.where(strict, a, 0.0)
            dl = _dot(do_in, vj, NT) * a
            d_l1m = cdl[skip:] + _tri_sum(dl, before, exact=False)
            beta = jnp.exp(lb)
            dz = dl * (1.0 - beta) - beta * d_l1m
            if key_offset is not None:
                dz = jnp.where(strict, dz, 0.0)
            dzs = (dz * scale).astype(MXU_DTYPE)
            dk_acc[_rows(j, TK), :] += _dot(dzs, q_in, TN)
            dv_acc[_rows(j, TK), :] += _dot(a.astype(MXU_DTYPE), do_in, TN)
            cl_new = cl[skip:] + jnp.sum(l1m, axis=1, keepdims=True)
            cdl_new = cdl[skip:] + jnp.sum(dl, axis=1, keepdims=True)
            dq_new = dq[skip:] + _dot(dzs, kj, NN)
            if skip:
                cl_new = jnp.concatenate([cl[:skip], cl_new], axis=0)
                cdl_new = jnp.concatenate([cdl[:skip], cdl_new], axis=0)
                dq_new = jnp.concatenate([dq[:skip], dq_new], axis=0)
            return cl_new, cdl_new, dq_new

        def q_loop(i, carry):
            qi, doi = qb[_rows(i, TQ), :], dob[_rows(i, TQ), :]
            ltot = ls_ref[_rows(i, TQ), :][:, :1]
            zero_col = jnp.zeros((TQ, 1), F32)
            def pair(jj, st):
                st = block(qi, doi, ltot, 2 * jj, st[0], st[1], st[2], None)
                return block(qi, doi, ltot, 2 * jj + 1, st[0], st[1], st[2], None)

            state = lax.fori_loop(0, i * (KPQ // 2), pair, (zero_col, zero_col, jnp.zeros((TQ, LANE), F32)))
            for d in range(KPQ):
                state = block(qi, doi, ltot, i * KPQ + d, state[0], state[1], state[2], d * TK)
            dq_ref[_rows(i, TQ), :] = state[2].astype(dq_ref.dtype)
            return carry

        lax.fori_loop(0, NQ, q_loop, 0)
        dk_ref[...] = dk_acc[...].astype(dk_ref.dtype)
        dv_ref[...] = dv_acc[...].astype(dv_ref.dtype)

    W = n_h * LANE
    return pl.pallas_call(
        body, name=name, grid=(n_h,),
        in_specs=[q_spec, k_spec, v_spec, head_spec, head_spec, dn_spec, gain_spec],
        out_specs=[head_spec, head_spec, head_spec, dgain_spec],
        out_shape=[jax.ShapeDtypeStruct((S, W), MXU_DTYPE)] * 3 + [jax.ShapeDtypeStruct((1, W), F32)],
        scratch_shapes=[pltpu.VMEM((S, LANE), MXU_DTYPE)] * 4 + [pltpu.VMEM((S, LANE), F32)] * 2,
        compiler_params=_params("parallel"),
    )(proj, proj, proj, o_sb, l_sum, d_on, out_gain)


def _mod_part(c_all, w_ada, b_ada_cols, name):
    B, K = c_all.shape
    N = w_ada.shape[1]
    tn = _tile(N, 512)

    def body(c_ref, w_ref, b_ref, o_ref):
        cv = c_ref[...]
        ca = (cv * jax.nn.sigmoid(cv)).astype(MXU_DTYPE)
        o_ref[...] = _dot(ca, w_ref[...].astype(MXU_DTYPE), NN) + b_ref[...]

    return pl.pallas_call(
        body, name=name, grid=(N // tn,),
        in_specs=[pl.BlockSpec((B, K), lambda j: (0, 0)), pl.BlockSpec((K, tn), lambda j: (0, j)),
                  pl.BlockSpec((1, tn), lambda j: (0, j))],
        out_specs=pl.BlockSpec((B, tn), lambda j: (0, j)),
        out_shape=jax.ShapeDtypeStruct((B, N), F32), compiler_params=_params("parallel"))(c_all, w_ada, b_ada_cols)


def _adamw_math(w, g, m, v):
    m = ADAM_B1 * m + (1.0 - ADAM_B1) * g
    v = ADAM_B2 * v + (1.0 - ADAM_B2) * (g * g)
    m_hat = m / (1.0 - ADAM_B1 ** ADAM_STEP)
    v_hat = v / (1.0 - ADAM_B2 ** ADAM_STEP)
    delta = -ADAM_LR * (m_hat / (jnp.sqrt(v_hat) + ADAM_EPS) + ADAM_WD * w)
    return delta, m, v


def _adamw(w, g, m, v, name):
    R, C = w.shape
    tr = _tile(R, max(8, (3 << 18) // C), 8)
    spec = pl.BlockSpec((tr, C), lambda i: (i, 0))

    def body(w_ref, g_ref, m_ref, v_ref, go_ref, d_ref, mo_ref, vo_ref):
        g = g_ref[...]
        go_ref[...] = g
        d_ref[...], mo_ref[...], vo_ref[...] = _adamw_math(w_ref[...], g, m_ref[...], v_ref[...])

    return pl.pallas_call(body, name=name, grid=(R // tr,), in_specs=[spec] * 4, out_specs=[spec] * 4,
                          out_shape=[jax.ShapeDtypeStruct((R, C), F32)] * 4, compiler_params=_params("parallel"))(w, g, m, v)


def _adamw_ada(c_all, dmod_cols, w, m, v, name):
    K, N = w.shape
    B = c_all.shape[0]
    tk, tn = _tile(K, 512), _tile(N, 1024)
    spec = pl.BlockSpec((tk, tn), lambda i, j: (i, j))

    def body(c_ref, dm_ref, w_ref, m_ref, v_ref, g_ref, d_ref, mo_ref, vo_ref):
        cv = c_ref[...]
        ca = (cv * jax.nn.sigmoid(cv)).astype(MXU_DTYPE)
        g = _dot(ca, dm_ref[...].astype(MXU_DTYPE), TN)
        g_ref[...] = g
        d_ref[...], mo_ref[...], vo_ref[...] = _adamw_math(w_ref[...], g, m_ref[...], v_ref[...])

    return pl.pallas_call(
        body, name=name, grid=(K // tk, N // tn),
        in_specs=[pl.BlockSpec((B, tk), lambda i, j: (0, i)), pl.BlockSpec((B, tn), lambda i, j: (0, j)), spec, spec, spec],
        out_specs=[spec] * 4, out_shape=[jax.ShapeDtypeStruct((K, N), F32)] * 4,
        compiler_params=_params("parallel", "parallel"))(c_all, dmod_cols, w, m, v)


def _sum_devices(gathered, n_dev, name):
    R = gathered.shape[0] // n_dev
    C = gathered.shape[1]
    tr = _tile(R, 512, 8)
    n_blk = R // tr

    def body(*refs):
        acc = refs[0][...]
        for r in refs[1:n_dev]:
            acc = acc + r[...]
        refs[n_dev][...] = acc

    in_specs = [pl.BlockSpec((tr, C), functools.partial(lambda i, d: (d * n_blk + i, 0), d=d)) for d in range(n_dev)]
    return pl.pallas_call(body, name=name, grid=(n_blk,), in_specs=in_specs,
                          out_specs=pl.BlockSpec((tr, C), lambda i: (i, 0)),
                          out_shape=jax.ShapeDtypeStruct((R, C), F32), compiler_params=_params("parallel"))(*([gathered] * n_dev))


def _place():
    x, y, c = lax.axis_index("x"), lax.axis_index("y"), lax.axis_index("c")
    return x, y, c


def _allgather8(blk, name):
    m_per, n = blk.shape

    def body(x_ref, out_ref, send_sems, recv_sems, local_sem):
        x, y, c = _place()
        me, sibling = (x, y, c), (x, y, 1 - c)
        chips = [(1 - x, y), (x, 1 - y), (1 - x, 1 - y)]

        def rows(px, py, pc):
            return out_ref.at[pl.ds((4 * px + 2 * py + pc) * m_per, m_per), :]

        def copy(k, block, to, src=None):
            return pltpu.make_async_remote_copy(
                src_ref=rows(*block) if src is None else src, dst_ref=rows(*block),
                send_sem=send_sems.at[k], recv_sem=recv_sems.at[k], device_id=to, device_id_type=MESH)

        mine = pltpu.make_async_copy(x_ref, rows(*me), local_sem)
        mine.start()
        first = [copy(0, me, sibling, src=x_ref)]
        first += [copy(1 + j, me, (*chip, c), src=x_ref) for j, chip in enumerate(chips)]
        for cp in first:
            cp.start()
        passed = [copy(4 + j, (*chip, c), sibling) for j, chip in enumerate(chips)]
        for j, chip in enumerate(chips):
            copy(1 + j, (*chip, c), me).wait_recv()
            passed[j].start()
        copy(0, sibling, me).wait_recv()
        for j, chip in enumerate(chips):
            copy(4 + j, (*chip, 1 - c), me).wait_recv()
        for cp in first + passed:
            cp.wait_send()
        mine.wait()

    return pl.pallas_call(
        body, name=name,
        out_shape=jax.ShapeDtypeStruct((8 * m_per, n), blk.dtype),
        in_specs=[pl.BlockSpec(memory_space=pltpu.VMEM)],
        out_specs=pl.BlockSpec(memory_space=pltpu.VMEM),
        scratch_shapes=[pltpu.SemaphoreType.DMA((7,)), pltpu.SemaphoreType.DMA((7,)), pltpu.SemaphoreType.DMA],
        compiler_params=pltpu.CompilerParams(vmem_limit_bytes=V7X_VMEM_LIMIT),
    )(blk)


class _Sharded:
    def __init__(self, shard_shape, by_cols):
        r, c = shard_shape
        self.by_cols = by_cols
        self.full = (r, N_CHIPS * c) if by_cols else (N_CHIPS * r, c)
        self.shard = (r, c)
        self.half_rows = r // 2
        self.half = (r // 2, c)

    def half_of(self, ref, k, hc):
        r, c = self.shard
        h = self.half_rows
        if self.by_cols:
            return ref.at[pl.ds(hc * h, h), pl.ds(k * c, c)]
        return ref.at[pl.ds(k * r + hc * h, h), :]

    def chunk_of(self, ref, k, hc, ch, n):
        r, c = self.shard
        h = self.half_rows
        q = h // n
        if self.by_cols:
            return ref.at[pl.ds(hc * h + ch * q, q), pl.ds(k * c, c)]
        return ref.at[pl.ds(k * r + hc * h + ch * q, q), :]

    def half_of_shard(self, ref, hc):
        return ref.at[pl.ds(hc * self.half_rows, self.half_rows), :]

    def part_of_halves(self, ref, k):
        r, c = self.shard
        h = self.half_rows
        return ref.at[:, pl.ds(k * c, c)] if self.by_cols else ref.at[pl.ds(k * h, h), :]


def _on_each_place(x, y, c, fn, by_chip=True, by_core=True):
    q = 2 * x + y
    for k in range(N_CHIPS if by_chip else 1):
        for cc in range(2 if by_core else 1):
            cond = None
            if by_chip:
                cond = q == k
            if by_core:
                cond = (c == cc) if cond is None else jnp.logical_and(cond, c == cc)
            pl.when(cond)(functools.partial(fn, k, cc))


def _chip_id(k, c):
    return (k // 2, k % 2, c)


def _handshake(peers):
    barrier = pltpu.get_barrier_semaphore()
    for peer in peers:
        pl.semaphore_signal(barrier, inc=1, device_id=peer, device_id_type=MESH)
    pl.semaphore_wait(barrier, len(peers))


def _on_sequencer(body, inputs, out_structs, n_copies, peers_of, name, collective_id, return_inputs=False):
    in_refs = [jax.new_ref(a, memory_space=pltpu.MemorySpace.HBM) for a in inputs]
    out_refs = [jax.empty_ref(s, memory_space=pltpu.MemorySpace.HBM) for s in out_structs]

    @pl.kernel(mesh=plsc.ScalarSubcoreMesh(axis_name="sequencer", num_cores=1), name=name,
               scratch_types=(pltpu.SemaphoreType.DMA((n_copies,)), pltpu.SemaphoreType.DMA((n_copies,))),
               compiler_params=pltpu.CompilerParams(collective_id=collective_id))
    def launch(send_sems, recv_sems):
        x, y, c = _place()
        _handshake(peers_of(x, y, c))
        body(in_refs, out_refs, send_sems, recv_sems, x, y, c)

    launch()
    return [r[...] for r in (in_refs if return_inputs else out_refs)]


def _sibling(x, y, c):
    return [(x, y, 1 - c)]


def _same_core_of_other_chips(x, y, c):
    return [(1 - x, y, c), (x, 1 - y, c), (1 - x, 1 - y, c)]


GATHER_CHUNKS = 4
GATHER_COPIES = 6 * GATHER_CHUNKS


def _place_slab(place, parts, loss_part, rows, name):
    n_parts = len(parts)

    def body(p_ref, *refs):
        loss_ref, out_ref = refs[n_parts], refs[n_parts + 1]
        at = 0
        for ref in refs[:n_parts]:
            if len(ref.shape) == 2 and ref.shape[0] == 1:
                for r in range(ref.shape[1] // LANE):
                    out_ref[at + r:at + r + 1, :] = ref[:, r * LANE:(r + 1) * LANE]
                at += ref.shape[1] // LANE
            else:
                n_rows = math.prod(ref.shape) // LANE
                out_ref[at:at + n_rows, :] = ref[...].reshape(n_rows, LANE)
                at += n_rows
        out_ref[at:at + 8, :] = jnp.broadcast_to(loss_ref[...], (8, LANE))
        out_ref[at + 8:, :] = jnp.zeros((rows - at - 8, LANE), F32)

    def whole(shape):
        return pl.BlockSpec(shape, functools.partial(lambda i, p, nd: (0,) * nd, nd=len(shape)))

    return pl.pallas_call(
        body, name=name,
        grid_spec=pltpu.PrefetchScalarGridSpec(
            num_scalar_prefetch=1, grid=(1,),
            in_specs=[whole(a.shape) for a in parts] + [whole(loss_part.shape)],
            out_specs=pl.BlockSpec((rows, LANE), lambda i, p: (2 * p[0] + p[1], 0))),
        out_shape=jax.ShapeDtypeStruct((8 * rows, LANE), F32),
        compiler_params=_params(),
    )(place, *parts, loss_part)


def _allgather8_on_sequencer(placed, m_per, name, collective_id):
    def body(refs, _, send_sems, recv_sems, x, y, c):
        out_ref, = refs

        def at_place(k, cc):
            def rows(kk, pc):
                return out_ref.at[pl.ds((2 * kk + pc) * m_per, m_per), :]

            def copy(slot, block, to):
                return pltpu.make_async_remote_copy(src_ref=rows(*block), dst_ref=rows(*block), send_sem=send_sems.at[slot],
                                                    recv_sem=recv_sems.at[slot], device_id=to, device_id_type=MESH)

            others = [k ^ flip for flip in FLIPS]
            sends = [copy(0, (k, cc), _chip_id(k, 1 - cc))] + [copy(1 + j, (k, cc), _chip_id(kk, cc)) for j, kk in enumerate(others)]
            for cp in sends:
                cp.start()
            for j, kk in enumerate(others):
                copy(1 + j, (kk, cc), _chip_id(k, cc)).wait_recv()
                cp = copy(4 + j, (kk, cc), _chip_id(k, 1 - cc))
                cp.start()
                sends.append(cp)
            copy(0, (k, 1 - cc), _chip_id(k, cc)).wait_recv()
            for j, kk in enumerate(others):
                copy(4 + j, (kk, 1 - cc), _chip_id(k, cc)).wait_recv()
            for cp in sends:
                cp.wait_send()

        _on_each_place(x, y, c, at_place)

    def peers(x, y, c):
        return _sibling(x, y, c) + _same_core_of_other_chips(x, y, c)

    return _on_sequencer(body, [placed], [], 7, peers, name, collective_id, return_inputs=True)[0]


def _gather_weights(fulls, geoms, name, collective_id):
    n_w = len(fulls)
    n_ch, n_relay = GATHER_CHUNKS, GATHER_CHUNKS // 2
    f_refs = [jax.new_ref(f, memory_space=pltpu.MemorySpace.HBM) for f in fulls]
    FLIP_X, FLIP_Y, FLIP_BOTH = FLIPS
    TO_X, TO_Y, RELAY_TO_Y, RELAY_TO_X, ON_X, ON_Y, ON_DIAG = 0, n_ch, 2 * n_ch, 2 * n_ch + n_relay, 3 * n_ch, 4 * n_ch, 5 * n_ch

    @pl.kernel(mesh=plsc.ScalarSubcoreMesh(axis_name="sequencer", num_cores=1), name=name,
               scratch_types=(pltpu.SemaphoreType.DMA((GATHER_COPIES * n_w,)), pltpu.SemaphoreType.DMA((GATHER_COPIES * n_w,))),
               compiler_params=pltpu.CompilerParams(collective_id=collective_id))
    def launch(send_sems, recv_sems):
        x, y, c = _place()
        _handshake([(x, y, 1 - c), (1 - x, y, c), (x, 1 - y, c)])

        def at_place(k, cc):
            kx, ky, kd = k ^ FLIP_X, k ^ FLIP_Y, k ^ FLIP_BOTH
            me, sibling = _chip_id(k, cc), _chip_id(k, 1 - cc)
            started = []

            def copy(i, slot, src, dst, to, start=True):
                cp = pltpu.make_async_remote_copy(src_ref=src, dst_ref=dst, send_sem=send_sems.at[GATHER_COPIES * i + slot],
                                                  recv_sem=recv_sems.at[GATHER_COPIES * i + slot], device_id=to, device_id_type=MESH)
                if start:
                    cp.start()
                    started.append(cp)
                return cp

            def pass_on(i, slot, ref, to):
                copy(i, slot, ref, ref, to)

            def landed(i, slot, ref):
                copy(i, slot, ref, ref, me, start=False).wait_recv()

            y_order = [(n_relay + s) % n_ch for s in range(n_ch)]
            for i, (g, f_ref) in enumerate(zip(geoms, f_refs)):
                for s in range(n_ch):
                    pass_on(i, TO_X + s, g.chunk_of(f_ref, k, cc, s, n_ch), _chip_id(kx, cc))
                    pass_on(i, TO_Y + y_order[s], g.chunk_of(f_ref, k, cc, y_order[s], n_ch), _chip_id(ky, cc))
            for i, (g, f_ref) in enumerate(zip(geoms, f_refs)):
                for s in range(n_ch):
                    from_x = g.chunk_of(f_ref, kx, cc, s, n_ch)
                    landed(i, TO_X + s, from_x)
                    if s < n_relay:
                        pass_on(i, RELAY_TO_Y + s, from_x, _chip_id(ky, cc))
                    pass_on(i, ON_X + s, from_x, sibling)
                    ch = y_order[s]
                    from_y = g.chunk_of(f_ref, ky, cc, ch, n_ch)
                    landed(i, TO_Y + ch, from_y)
                    if ch >= n_relay:
                        pass_on(i, RELAY_TO_X + ch - n_relay, from_y, _chip_id(kx, cc))
                    pass_on(i, ON_Y + ch, from_y, sibling)
                for r in range(n_relay):
                    via_y = g.chunk_of(f_ref, kd, cc, r, n_ch)
                    landed(i, RELAY_TO_Y + r, via_y)
                    pass_on(i, ON_DIAG + r, via_y, sibling)
                    via_x = g.chunk_of(f_ref, kd, cc, n_relay + r, n_ch)
                    landed(i, RELAY_TO_X + r, via_x)
                    pass_on(i, ON_DIAG + n_relay + r, via_x, sibling)
            for i, (g, f_ref) in enumerate(zip(geoms, f_refs)):
                for slot, kk in ((ON_X, kx), (ON_Y, ky), (ON_DIAG, kd)):
                    for ch in range(n_ch):
                        landed(i, slot + ch, g.chunk_of(f_ref, kk, 1 - cc, ch, n_ch))
            for cp in started:
                cp.wait_send()

        _on_each_place(x, y, c, at_place)

    launch()
    return [f_ref[...] for f_ref in f_refs]


def _swap_core_halves(grads, geoms, name, collective_id):
    n_cp = sum(1 if g.by_cols else N_CHIPS for g in geoms)

    def body(g_refs, t_refs, send_sems, recv_sems, x, y, c):

        def at_place(_, cc):
            def pairs(hc):
                out = []
                for g, g_ref, t_ref in zip(geoms, g_refs, t_refs):
                    if g.by_cols:
                        out.append((g_ref.at[pl.ds(hc * g.half_rows, g.half_rows), :], t_ref))
                    else:
                        out += [(g.half_of(g_ref, k, hc), g.part_of_halves(t_ref, k)) for k in range(N_CHIPS)]
                return out

            sends = [pltpu.make_async_remote_copy(src_ref=src, dst_ref=dst, send_sem=send_sems.at[n],
                                                  recv_sem=recv_sems.at[n], device_id=(x, y, 1 - cc), device_id_type=MESH)
                     for n, (src, dst) in enumerate(pairs(1 - cc))]
            for cp in sends:
                cp.start()
            for n, (src, dst) in enumerate(pairs(cc)):
                pltpu.make_async_remote_copy(src_ref=src, dst_ref=dst, send_sem=send_sems.at[n], recv_sem=recv_sems.at[n],
                                             device_id=(x, y, cc), device_id_type=MESH).wait_recv()
            for cp in sends:
                cp.wait_send()

        _on_each_place(x, y, c, at_place, by_chip=False)

    return _on_sequencer(body, grads, [jax.ShapeDtypeStruct((g.full[0] // 2, g.full[1]), F32) for g in geoms],
                         n_cp, _sibling, name, collective_id)


def _send_to_sibling(buffers, name, collective_id):
    def body(src_refs, dst_refs, send_sems, recv_sems, x, y, c):
        def copy(i):
            return pltpu.make_async_remote_copy(src_ref=src_refs[i], dst_ref=dst_refs[i], send_sem=send_sems.at[i],
                                                recv_sem=recv_sems.at[i], device_id=(x, y, 1 - c), device_id_type=MESH)

        for i in range(len(buffers)):
            copy(i).start()
        for i in range(len(buffers)):
            copy(i).wait()

    return _on_sequencer(body, buffers, [jax.ShapeDtypeStruct(t.shape, t.dtype) for t in buffers], len(buffers),
                         _sibling, name, collective_id)


def _scatter_chip_sums(sums, geoms, name, collective_id):
    def body(s_refs, r_refs, send_sems, recv_sems, x, y, c):

        def at_place(k, _):
            sends = []
            for i, (g, s_ref, r_ref) in enumerate(zip(geoms, s_refs, r_refs)):
                for j, flip in enumerate(FLIPS):
                    kk = k ^ flip
                    cp = pltpu.make_async_remote_copy(
                        src_ref=g.part_of_halves(s_ref, kk), dst_ref=r_ref.at[j], send_sem=send_sems.at[3 * i + j],
                        recv_sem=recv_sems.at[3 * i + j], device_id=(kk // 2, kk % 2, c), device_id_type=MESH)
                    cp.start()
                    sends.append(cp)
            for i, (g, s_ref, r_ref) in enumerate(zip(geoms, s_refs, r_refs)):
                for j in range(len(FLIPS)):
                    pltpu.make_async_remote_copy(
                        src_ref=g.part_of_halves(s_ref, k), dst_ref=r_ref.at[j], send_sem=send_sems.at[3 * i + j],
                        recv_sem=recv_sems.at[3 * i + j], device_id=(x, y, c), device_id_type=MESH).wait_recv()
            for cp in sends:
                cp.wait_send()

        _on_each_place(x, y, c, at_place, by_core=False)

    return _on_sequencer(body, sums, [jax.ShapeDtypeStruct((len(FLIPS),) + g.half, WIRE_DTYPE) for g in geoms],
                         len(FLIPS) * len(sums), _same_core_of_other_chips, name, collective_id)


def _share_reduced_halves(reduced, geoms, name, collective_id):
    def body(out_refs, _, send_sems, recv_sems, x, y, c):

        def at_place(_, cc):
            sends = []
            for i, (g, ref) in enumerate(zip(geoms, out_refs)):
                mine = g.half_of_shard(ref, cc)
                cp = pltpu.make_async_remote_copy(src_ref=mine, dst_ref=mine, send_sem=send_sems.at[i],
                                                  recv_sem=recv_sems.at[i], device_id=(x, y, 1 - cc), device_id_type=MESH)
                cp.start()
                sends.append(cp)
            for i, (g, ref) in enumerate(zip(geoms, out_refs)):
                theirs = g.half_of_shard(ref, 1 - cc)
                pltpu.make_async_remote_copy(src_ref=theirs, dst_ref=theirs, send_sem=send_sems.at[i],
                                             recv_sem=recv_sems.at[i], device_id=(x, y, cc), device_id_type=MESH).wait_recv()
            for cp in sends:
                cp.wait_send()

        _on_each_place(x, y, c, at_place, by_chip=False)

    return _on_sequencer(body, reduced, [], len(reduced), _sibling, name, collective_id, return_inputs=True)


def _chip_sum(place, grad, theirs, g, name):
    RH, C = theirs.shape
    h = g.half_rows
    tr = _tile(h, 256, 16)
    tc = _tile(C, 2048)
    per_half = h // tr

    if g.by_cols:
        grad_map = lambda i, j, p: (p[1] * per_half + i, j)
    else:
        grad_map = lambda i, j, p: ((i // per_half) * 2 * per_half + p[1] * per_half + i % per_half, j)

    def body(p_ref, a_ref, b_ref, f_ref, o_ref):
        total = a_ref[...] + b_ref[...]
        f_ref[...] = total
        o_ref[...] = total.astype(o_ref.dtype)

    return pl.pallas_call(
        body, name=name,
        grid_spec=pltpu.PrefetchScalarGridSpec(
            num_scalar_prefetch=1, grid=(RH // tr, C // tc),
            in_specs=[pl.BlockSpec((tr, tc), grad_map), pl.BlockSpec((tr, tc), lambda i, j, p: (i, j))],
            out_specs=[pl.BlockSpec((tr, tc), lambda i, j, p: (i, j))] * 2),
        out_shape=[jax.ShapeDtypeStruct((RH, C), F32), jax.ShapeDtypeStruct((RH, C), WIRE_DTYPE)],
        compiler_params=_params("parallel", "parallel"),
    )(place, grad, theirs)


def _dw_half(place, a, b, g, mine, name, add=None):
    K, R = a.shape
    C = b.shape[1]
    h = g.half_rows
    tm, tn = _tile(h, 1024, 16), _tile(C, 512)
    per_half = h // tm
    n_i = (R // 2) // tm

    def a_map(i, j, p):
        hc = p[1] if mine else 1 - p[1]
        if g.by_cols:
            return 0, hc * n_i + i
        return 0, (i // per_half) * 2 * per_half + hc * per_half + i % per_half

    mn_spec = pl.BlockSpec((tm, tn), lambda i, j, p: (i, j))

    def body(p_ref, a_ref, b_ref, *rest):
        acc = _dot(a_ref[...], b_ref[...], TN)
        if add is None:
            rest[0][...] = acc
        else:
            total = acc + rest[0][...]
            rest[1][...] = total
            rest[2][...] = total.astype(rest[2].dtype)

    out_shape = [jax.ShapeDtypeStruct((R // 2, C), F32)] + ([] if add is None else [jax.ShapeDtypeStruct((R // 2, C), WIRE_DTYPE)])
    return pl.pallas_call(
        body, name=name,
        grid_spec=pltpu.PrefetchScalarGridSpec(
            num_scalar_prefetch=1, grid=(n_i, C // tn),
            in_specs=[pl.BlockSpec((K, tm), a_map), pl.BlockSpec((K, tn), lambda i, j, p: (0, j))] + ([] if add is None else [mn_spec]),
            out_specs=[mn_spec] * len(out_shape)),
        out_shape=out_shape,
        compiler_params=_params("parallel", "arbitrary"),
    )(place, a, b, *([] if add is None else [add]))


def _reduce_half(place, sums, others, g, name):
    h, tc = g.half
    tr = _tile(h, 256, 16)
    per_half = h // tr
    sums_map = (lambda i, p: (i, p[0])) if g.by_cols else (lambda i, p: (p[0] * per_half + i, 0))

    def body(p_ref, s_ref, o0_ref, o1_ref, o2_ref, out_ref):
        acc = s_ref[...]
        for o_ref in (o0_ref, o1_ref, o2_ref):
            acc = acc + o_ref[...].astype(F32)
        out_ref[...] = acc

    other_specs = [pl.BlockSpec((None, tr, tc), functools.partial(lambda i, p, j: (j, i, 0), j=j)) for j in range(len(FLIPS))]
    return pl.pallas_call(
        body, name=name,
        grid_spec=pltpu.PrefetchScalarGridSpec(
            num_scalar_prefetch=1, grid=(per_half,),
            in_specs=[pl.BlockSpec((tr, tc), sums_map)] + other_specs,
            out_specs=pl.BlockSpec((tr, tc), lambda i, p: (p[1] * per_half + i, 0))),
        out_shape=jax.ShapeDtypeStruct(g.shard, F32),
        compiler_params=_params("arbitrary"),
    )(place, sums, others, others, others)


SLAB_ROW_UNIT = 256
SMALL = ("b_ada", "norm1_g", "v_norm_g", "w_spatial", "b_spatial", "out_norm_g", "norm2_g", "final_g")
BIG = ("w_in", "w_out", "w_gate", "w_up", "w_down")
BY_COLS = {"w_in": True, "w_out": False, "w_gate": True, "w_up": True, "w_down": False}
ORDER = ("w_ada", "b_ada", "norm1_g", "w_in", "v_norm_g", "w_spatial", "b_spatial", "out_norm_g", "w_out",
         "norm2_g", "w_gate", "w_up", "w_down", "final_g")


def _pack(parts):
    return jnp.concatenate([parts[n].reshape(-1) for n in SMALL]).reshape(-1, LANE)


def _adamw_small(w, gathered, m, v, shapes, rows, name):
    R = w.shape[0]
    slab_spec = pl.BlockSpec((R, LANE), lambda: (0, 0))
    out_shapes = [shapes[n] if len(shapes[n]) > 1 else (1,) + tuple(shapes[n]) for n in SMALL]

    def body(w_ref, g_ref, m_ref, v_ref, *out_refs):
        gv, loss = g_ref[0:R, :], g_ref[R:R + 8, :]
        for dev in range(1, 8):
            gv = gv + g_ref[dev * rows:dev * rows + R, :]
            loss = loss + g_ref[dev * rows + R:dev * rows + R + 8, :]
        out_refs[4 * len(SMALL)][...] = loss
        results = (gv,) + _adamw_math(w_ref[...], gv, m_ref[...], v_ref[...])
        for kind, val in enumerate(results):
            at = 0
            for i, shp in enumerate(out_shapes):
                o_ref = out_refs[kind * len(SMALL) + i]
                n_rows = math.prod(shp) // LANE
                if len(shp) == 2:
                    for r in range(n_rows):
                        o_ref[:, r * LANE:(r + 1) * LANE] = val[at + r:at + r + 1, :]
                else:
                    o_ref[0] = val[at:at + n_rows, :].reshape(shp[1:])
                at += n_rows

    outs = pl.pallas_call(
        body, name=name,
        in_specs=[slab_spec, pl.BlockSpec(gathered.shape, lambda: (0, 0)), slab_spec, slab_spec],
        out_specs=[pl.BlockSpec(shp, functools.partial(lambda nd: (0,) * nd, len(shp))) for shp in out_shapes] * 4
        + [pl.BlockSpec((8, LANE), lambda: (0, 0))],
        out_shape=[jax.ShapeDtypeStruct(shp, F32) for shp in out_shapes] * 4 + [jax.ShapeDtypeStruct((8, LANE), F32)],
        compiler_params=_params(),
    )(w, gathered, m, v)
    dicts = []
    for kind in range(4):
        part = outs[kind * len(SMALL):(kind + 1) * len(SMALL)]
        dicts.append({n: a.reshape(shapes[n]) for n, a in zip(SMALL, part)})
    return dicts, outs[-1]


def kernel(x, c, w_ada, b_ada, norm1_g, w_in, v_norm_g, w_spatial, b_spatial, out_norm_g, w_out, norm2_g, w_gate, w_up, w_down, final_g, loss_target, m_w_ada, m_b_ada, m_norm1_g, m_w_in, m_v_norm_g, m_w_spatial, m_b_spatial, m_out_norm_g, m_w_out, m_norm2_g, m_w_gate, m_w_up, m_w_down, m_final_g, v_w_ada, v_b_ada, v_norm1_g, v_w_in, v_v_norm_g, v_w_spatial, v_b_spatial, v_out_norm_g, v_w_out, v_norm2_g, v_w_gate, v_w_up, v_w_down, v_final_g):
    weights = dict(w_ada=w_ada, b_ada=b_ada, norm1_g=norm1_g, w_in=w_in, v_norm_g=v_norm_g, w_spatial=w_spatial,
                   b_spatial=b_spatial, out_norm_g=out_norm_g, w_out=w_out, norm2_g=norm2_g, w_gate=w_gate, w_up=w_up,
                   w_down=w_down, final_g=final_g)
    m_in = dict(w_ada=m_w_ada, b_ada=m_b_ada, norm1_g=m_norm1_g, w_in=m_w_in, v_norm_g=m_v_norm_g, w_spatial=m_w_spatial,
                b_spatial=m_b_spatial, out_norm_g=m_out_norm_g, w_out=m_w_out, norm2_g=m_norm2_g, w_gate=m_w_gate,
                w_up=m_w_up, w_down=m_w_down, final_g=m_final_g)
    v_in = dict(w_ada=v_w_ada, b_ada=v_b_ada, norm1_g=v_norm1_g, w_in=v_w_in, v_norm_g=v_v_norm_g, w_spatial=v_w_spatial,
                b_spatial=v_b_spatial, out_norm_g=v_out_norm_g, w_out=v_w_out, norm2_g=v_norm2_g, w_gate=v_w_gate,
                w_up=v_w_up, w_down=v_w_down, final_g=v_final_g)

    S, D = x.shape[1], x.shape[2]
    n_g = v_norm_g.shape[-1] // LANE
    n_h = (D - n_g * LANE) // LANE
    GW = n_g * LANE
    xi, yi, ci = _place()
    chip = 2 * xi + yi
    me = 4 * xi + 2 * yi + ci
    place = jnp.stack([chip, ci]).astype(jnp.int32)

    xs, target = x[0], loss_target[0]
    geoms = [_Sharded(weights[n].shape[1:], BY_COLS[n]) for n in BIG]

    full = {}
    for i, group in enumerate((("w_in",), ("w_out",), ("w_gate", "w_up"), ("w_down",))):
        gg = [geoms[BIG.index(n)] for n in group]
        own = [_cast_into_full(place, weights[n][0], g, "cast_" + n) for n, g in zip(group, gg)]
        gathered = _gather_weights(own, gg, "gather_" + "_".join(group), 1 + i)
        full.update(zip(group, gathered))

    c_pad = jnp.concatenate([c, jnp.zeros((7, D), F32)], axis=0)
    c_all = _allgather8(c_pad, "gather_c")[::8]
    n_ada = w_ada.shape[2]
    b_cols = lax.dynamic_slice(b_ada, (0, chip * n_ada), (1, n_ada))
    mod_parts = _allgather8(_mod_part(c_all, w_ada[0], b_cols, "mod_part"), "gather_mod")
    mod_all = mod_parts.reshape(N_CHIPS, 2, 8, n_ada)[:, 0].transpose(1, 0, 2).reshape(8, N_CHIPS * n_ada)
    mod = lax.dynamic_slice(mod_all, (me, 0), (1, 6 * D))
    shift1, scale1, gate1, shift2, scale2, gate2 = [mod[:, i * D:(i + 1) * D] for i in range(6)]

    b_t = b_spatial[0].T
    h1 = _norm_mod(xs, norm1_g, scale1, shift1, "norm1")
    proj, = _mm("nn", h1, full["w_in"], [F32], "proj")
    on_gm = _gmlp_fwd(proj, v_norm_g, w_spatial[0], b_t, out_norm_g, n_g, "gmlp_fwd")
    o_sb, o_n, l_sum = _sb_fwd(proj, out_norm_g, on_gm, n_g, n_h, "sb_fwd")
    attn, = _mm("nn", o_n, full["w_out"], [F32], "attn_out")
    x1, h2 = _residual_norm_mod(xs, attn, gate1, norm2_g, scale2, shift2, "norm2")
    a_g, a_u, f_in = _gate_up(h2, full["w_gate"], full["w_up"], "gate_up")
    f, = _mm("nn", f_in, full["w_down"], [F32], "down", tm=1024)
    dx2, df, d_gate2, d_final_g, loss_part = _final_loss_bwd(x1, f, gate2, final_g.reshape(1, D), target, "final")

    geom_of = dict(zip(BIG, geoms))
    grad_out, delta, new_m, new_v = {}, {}, {}, {}

    def theirs_first(group, operands, collective_id, after=None):
        outs = []
        for n, (a_op, b_op) in zip(group, operands):
            outs.append(_dw_half(place, a_op, b_op if after is None else _then(after, b_op), geom_of[n], False, "d_" + n + "_theirs")[0])
            after = outs[-1]
        return outs, _send_to_sibling(outs, "swap_" + "_".join(group), collective_id)

    def chip_sums(group, operands, theirs, after):
        f32s, wires = [], []
        for n, (a_op, b_op), t in zip(group, operands, theirs):
            sf, sw = _dw_half(place, a_op, b_op, geom_of[n], True, "d_" + n + "_mine", add=_then(after, t))
            f32s.append(sf)
            wires.append(sw)
            after = sw
        return f32s, wires

    def scatter(group, sums, collective_id):
        return _scatter_chip_sums(sums, [geom_of[n] for n in group], "scatter_" + "_".join(group), collective_id)

    def reduce_halves(group, sums, others, after):
        return [_reduce_half(place, sf, _then(after, o), geom_of[n], "reduce_" + n) for n, sf, o in zip(group, sums, others)]

    def share(group, halves, collective_id):
        return _share_reduced_halves(halves, [geom_of[n] for n in group], "share_" + "_".join(group), collective_id)

    def adamw(group, reduced, after):
        for n, r in zip(group, reduced):
            go, d, mo, vo = _adamw(weights[n][0], _then(after, r), m_in[n][0], v_in[n][0], "adamw_" + n)
            grad_out[n], delta[n], new_m[n], new_v[n] = go[None], d[None], mo[None], vo[None]
        return d

    g_down = ("w_down",)
    g_ffn = ("w_gate", "w_up")
    g_out = ("w_out",)
    g_in = ("w_in",)

    gr_down, = _mm("tn", f_in, df, [F32], "d_w_down", tm=1408, tn=1024)
    th_down, = _swap_core_halves([gr_down], [geom_of["w_down"]], "swap_w_down", 6)
    d_ag, d_au = _mm("nt", df, full["w_down"], [MXU_DTYPE, MXU_DTYPE], "d_ffn_in", extras=(a_g, a_u),
                     epilogue=_swiglu_bwd_epilogue)
    sf_down, sw_down = [[t] for t in _chip_sum(place, gr_down, _then(d_ag, th_down), geom_of["w_down"], "chip_sum_w_down")]
    ot_down = scatter(g_down, sw_down, 7)
    sent, th_ffn = theirs_first(g_ffn, [(h2, d_ag), (h2, d_au)], 9, after=sw_down)
    dh2 = _mm_ktiled("nt", [(_then(sent, d_ag), full["w_gate"]), (d_au, full["w_up"])], "d_h2", tn=512)
    sf_ffn, sw_ffn = chip_sums(g_ffn, [(h2, d_ag), (h2, d_au)], th_ffn, after=dh2)
    ot_ffn = scatter(g_ffn, sw_ffn, 10)
    hv_down = reduce_halves(g_down, sf_down, ot_down, after=sw_ffn)
    rd_down = share(g_down, hv_down, 8)
    dx1, d_shift2, d_scale2, d_norm2_g, d_gate1, d_attn = _norm_mod_bwd(
        _then(hv_down, dh2), x1, dx2, norm2_g, scale2, "norm2_bwd", branch=attn, gate=gate1)
    gr_out, = _mm("tn", o_n, d_attn, [F32], "d_w_out")
    th_out, = _swap_core_halves([gr_out], [geom_of["w_out"]], "swap_w_out", 12)
    d_on, = _mm("nt", _then(gr_out, d_attn), full["w_out"], [F32], "d_o")
    sf_out, sw_out = [[t] for t in _chip_sum(place, gr_out, _then(d_on, th_out), geom_of["w_out"], "chip_sum_w_out")]
    ot_out = scatter(g_out, sw_out, 13)
    dq, dk, dv, d_og_sb = _sb_bwd(proj, o_sb, l_sum, _then(sw_out, d_on), out_norm_g, n_g, n_h, "sb_bwd")
    dproj, d_w_spatial, d_b_t, d_v_norm_g, d_og_gm = _gmlp_bwd(proj, d_on, v_norm_g, w_spatial[0], b_t, out_norm_g,
                                                                (dq, dk, dv), n_g, "gmlp_bwd")
    sent, th_in = theirs_first(g_in, [(h1, dproj)], 15)
    hv_ffn = reduce_halves(g_ffn, sf_ffn, ot_ffn, after=sent)
    rd_ffn = share(g_ffn, hv_ffn, 11)
    dh1, = _mm("nt", _then(sent, dproj), full["w_in"], [F32], "d_h1", tm=1024)
    hv_out = reduce_halves(g_out, sf_out, ot_out, after=dh1)
    rd_out = share(g_out, hv_out, 14)
    grad_x, d_shift1, d_scale1, d_norm1_g = _norm_mod_bwd(_then(hv_out, dh1), xs, dx1, norm1_g, scale1, "norm1_bwd")

    small_parts = [d_shift1, d_scale1, d_gate1, d_shift2, d_scale2, d_gate2,
                   d_norm1_g, d_v_norm_g, d_w_spatial, d_b_t.T, d_og_gm, d_og_sb, d_norm2_g, d_final_g]
    small_rows = sum(math.prod(p.shape) for p in small_parts) // LANE
    rows = -(-(small_rows + 8) // SLAB_ROW_UNIT) * SLAB_ROW_UNIT
    slab = _place_slab(place, small_parts, loss_part, rows, "place_slab")
    gathered = _allgather8_on_sequencer(slab, rows, "gather_small", 18)
    sf_in, sw_in = chip_sums(g_in, [(h1, dproj)], th_in, after=slab)
    ot_in = scatter(g_in, sw_in, 16)
    done = adamw(g_down, rd_down, after=sw_in)
    done = adamw(g_ffn, rd_ffn, after=done)
    done = adamw(g_out, rd_out, after=done)
    gathered = _then(done, gathered)
    small_shapes = {n: weights[n].shape for n in SMALL}
    ada_rows = n_ada // LANE
    dmod_cols = lax.dynamic_slice(gathered.reshape(8, rows, LANE), (0, chip * ada_rows, 0), (8, ada_rows, LANE))
    dmod_cols = dmod_cols.reshape(8, n_ada)
    g_ada, d, mo, vo = _adamw_ada(c_all, dmod_cols, w_ada[0], m_w_ada[0], v_w_ada[0], "adamw_w_ada")
    grad_out["w_ada"], delta["w_ada"], new_m["w_ada"], new_v["w_ada"] = g_ada[None], d[None], mo[None], vo[None]
    small_out, loss_rows = _adamw_small(_pack({n: weights[n] for n in SMALL}), gathered, _pack({n: m_in[n] for n in SMALL}),
                                        _pack({n: v_in[n] for n in SMALL}), small_shapes, rows, "adamw_small")
    loss = loss_rows[0, 0]
    for dst, part in zip((grad_out, delta, new_m, new_v), small_out):
        dst.update(part)
    hv_in = reduce_halves(g_in, sf_in, ot_in, after=d)
    adamw(g_in, share(g_in, hv_in, 17), after=d)

    return (loss, grad_x[None], *[grad_out[n] for n in ORDER], *[delta[n] for n in ORDER],
            *[new_m[n] for n in ORDER], *[new_v[n] for n in ORDER])
```
